```python
import jax, jax.numpy as jnp
from jax import lax
import numpy as np

D_MODEL = 1024
BATCH = 4
SEQ = 8192
DEPTH = 4

N_META = 16
BLOCK = 128
PAD = BLOCK - N_META
MLA_HEADS = 8
MLA_NOPE = 64
MLA_ROPE = 32
MLA_QK = MLA_NOPE + MLA_ROPE
MLA_V = 64
Q_LORA = 384
KV_LORA = 256
ROPE_BASE = 10000.0
FOX_HEADS = 8
FOX_DIM = 64
CONV_WIDTH = 3
D_FF = 4 * D_MODEL
EPS = 1e-6
NEG = -1e30
N_EVEN = (DEPTH + 1) // 2
N_ODD = DEPTH // 2
ATTN_SPLITS = (Q_LORA, KV_LORA, MLA_ROPE, FOX_HEADS * FOX_DIM, FOX_HEADS * FOX_DIM, FOX_HEADS * FOX_DIM, FOX_HEADS)
ATTN_IN = Q_LORA + KV_LORA + MLA_ROPE + 3 * FOX_HEADS * FOX_DIM + FOX_HEADS
MIX_OUT = MLA_HEADS * MLA_V + FOX_HEADS * FOX_DIM

kernel_name = "hybrid_mla_fox_shortconv_trunk"


def _offsets(sizes):
    out, acc = [], 0
    for s in sizes[:-1]:
        acc += s
        out.append(acc)
    return out


def rms_norm(x, g):
    xf = x.astype(jnp.float32)
    y = xf * lax.rsqrt(jnp.mean(xf * xf, axis=-1, keepdims=True) + EPS)
    return (y * g.astype(jnp.float32)).astype(x.dtype)


def rope_tables(length):
    pos = jnp.arange(length, dtype=jnp.float32)
    inv_freq = ROPE_BASE ** (-jnp.arange(0, MLA_ROPE, 2, dtype=jnp.float32) / MLA_ROPE)
    ang = pos[:, None] * inv_freq[None, :]
    return jnp.cos(ang), jnp.sin(ang)


def rope_tail(x, cos, sin):
    x_nope = x[..., :MLA_NOPE]
    xr = x[..., MLA_NOPE:].astype(jnp.float32)
    x1, x2 = xr[..., : MLA_ROPE // 2], xr[..., MLA_ROPE // 2:]
    c, s = cos[None, :, None, :], sin[None, :, None, :]
    rot = jnp.concatenate([x1 * c - x2 * s, x2 * c + x1 * s], axis=-1).astype(x.dtype)
    return jnp.concatenate([x_nope, rot], axis=-1)


def pad_front(x):
    return jnp.pad(x, [(0, 0), (PAD, 0)] + [(0, 0)] * (x.ndim - 2))


def blocked_causal_attention(q, k, v, scale, cum_log_f=None):
    b, lp, h, dk = q.shape
    nb = lp // BLOCK
    key_pos = jnp.arange(lp)
    qb = q.reshape(b, nb, BLOCK, h, dk).transpose(1, 0, 2, 3, 4)
    use_decay = cum_log_f is not None
    if use_decay:
        f_bh = cum_log_f.transpose(0, 2, 1)
        f_q = f_bh.reshape(b, h, nb, BLOCK).transpose(2, 0, 1, 3)
        xs = (jnp.arange(nb), qb, f_q)
    else:
        xs = (jnp.arange(nb), qb)

    def one_block(args):
        if use_decay:
            i, q_blk, fq = args
        else:
            i, q_blk = args
        s = jnp.einsum('bqhd,bkhd->bhqk', q_blk, k, preferred_element_type=jnp.float32) * scale
        if use_decay:
            s = s + fq[..., :, None] - f_bh[:, :, None, :]
        q_pos = i * BLOCK + jnp.arange(BLOCK)
        mask = (key_pos[None, :] <= q_pos[:, None]) & (key_pos[None, :] >= PAD)
        s = jnp.where(mask[None, None], s, NEG)
        p = jax.nn.softmax(s, axis=-1)
        return jnp.einsum('bhqk,bkhd->bqhd', p.astype(v.dtype), v)

    out = lax.map(one_block, xs)
    return out.transpose(1, 0, 2, 3, 4).reshape(b, lp, h, v.shape[-1])


def attention_mixer(h, cos, sin, w_in, g_cq, w_uq, g_ckv, w_ukv, g_q_mla, g_k_mla,
                    g_q_fox, g_k_fox, b_forget, w_out):
    b, l, _ = h.shape
    z = h @ w_in
    c_q, c_kv, k_pe, fq, fk, fv, f_logit = jnp.split(z, _offsets(ATTN_SPLITS), axis=-1)

    q = (rms_norm(c_q, g_cq) @ w_uq).reshape(b, l, MLA_HEADS, MLA_QK)
    kv = (rms_norm(c_kv, g_ckv) @ w_ukv).reshape(b, l, MLA_HEADS, MLA_NOPE + MLA_V)
    k_nope, v_mla = kv[..., :MLA_NOPE], kv[..., MLA_NOPE:]
    k_rope = jnp.broadcast_to(k_pe[:, :, None, :], (b, l, MLA_HEADS, MLA_ROPE))
    k = jnp.concatenate([k_nope, k_rope], axis=-1)
    q = rope_tail(rms_norm(q, g_q_mla), cos, sin)
    k = rope_tail(rms_norm(k, g_k_mla), cos, sin)
    o_mla = blocked_causal_attention(pad_front(q), pad_front(k), pad_front(v_mla),
                                     MLA_QK ** -0.5)[:, PAD:]
    o_mla = o_mla.reshape(b, l, MLA_HEADS * MLA_V)

    qf = rms_norm(fq.reshape(b, l, FOX_HEADS, FOX_DIM), g_q_fox)
    kf = rms_norm(fk.reshape(b, l, FOX_HEADS, FOX_DIM), g_k_fox)
    vf = fv.reshape(b, l, FOX_HEADS, FOX_DIM)
    log_f = jax.nn.log_sigmoid(f_logit.astype(jnp.float32) + b_forget.astype(jnp.float32))
    cum_log_f = jnp.cumsum(pad_front(log_f), axis=1)
    o_fox = blocked_causal_attention(pad_front(qf), pad_front(kf), pad_front(vf),
                                     FOX_DIM ** -0.5, cum_log_f)[:, PAD:]
    o_fox = o_fox.reshape(b, l, FOX_HEADS * FOX_DIM)

    return jnp.concatenate([o_mla, o_fox], axis=-1) @ w_out


def short_conv_mixer(h, w_in, conv_w, w_out):
    z = h @ w_in
    gate_b, gate_c, u = jnp.split(z, 3, axis=-1)
    g = gate_c * u
    y = lax.conv_general_dilated(
        g, conv_w[:, None, :].astype(g.dtype), window_strides=(1,),
        padding=[(CONV_WIDTH - 1, 0)], dimension_numbers=('NWC', 'WIO', 'NWC'),
        feature_group_count=D_MODEL)
    return (gate_b * y) @ w_out


def sq_relu_mlp(h, w_up, w_down):
    return jnp.square(jax.nn.relu(h @ w_up)) @ w_down


def setup_inputs(seed: int = 0) -> dict:
    key = jax.random.key(seed)
    ks = iter(jax.random.split(key, 32))

    def nrm(shape, scale):
        return jax.random.normal(next(ks), shape, jnp.float32) * scale

    def gain(shape):
        return 1.0 + 0.02 * jax.random.normal(next(ks), shape, jnp.float32)

    out_scale = (2.0 * DEPTH) ** -0.5
    return {
        "x": nrm((BATCH, SEQ, D_MODEL), 1.0),
        "meta_tokens": nrm((N_META, D_MODEL), 1.0),
        "g_mix": gain((DEPTH, D_MODEL)),
        "g_mlp": gain((DEPTH, D_MODEL)),
        "w_in_attn": nrm((N_EVEN, D_MODEL, ATTN_IN), D_MODEL ** -0.5),
        "g_cq": gain((N_EVEN, Q_LORA)),
        "w_uq": nrm((N_EVEN, Q_LORA, MLA_HEADS * MLA_QK), Q_LORA ** -0.5),
        "g_ckv": gain((N_EVEN, KV_LORA)),
        "w_ukv": nrm((N_EVEN, KV_LORA, MLA_HEADS * (MLA_NOPE + MLA_V)), KV_LORA ** -0.5),
        "g_q_mla": gain((N_EVEN, MLA_QK)),
        "g_k_mla": gain((N_EVEN, MLA_QK)),
        "g_q_fox": gain((N_EVEN, FOX_DIM)),
        "g_k_fox": gain((N_EVEN, FOX_DIM)),
        "b_forget": 2.0 + nrm((N_EVEN, FOX_HEADS), 0.1),
        "w_out_attn": nrm((N_EVEN, MIX_OUT, D_MODEL), MIX_OUT ** -0.5 * out_scale),
        "w_in_conv": nrm((N_ODD, D_MODEL, 3 * D_MODEL), D_MODEL ** -0.5),
        "conv_w": nrm((N_ODD, CONV_WIDTH, D_MODEL), CONV_WIDTH ** -0.5),
        "w_out_conv": nrm((N_ODD, D_MODEL, D_MODEL), D_MODEL ** -0.5 * out_scale),
        "w_mlp_up": nrm((DEPTH, D_MODEL, D_FF), D_MODEL ** -0.5),
        "w_mlp_down": nrm((DEPTH, D_FF, D_MODEL), D_FF ** -0.5 * out_scale),
    }


def reference(x, meta_tokens, g_mix, g_mlp, w_in_attn, g_cq, w_uq, g_ckv, w_ukv,
              g_q_mla, g_k_mla, g_q_fox, g_k_fox, b_forget, w_out_attn,
              w_in_conv, conv_w, w_out_conv, w_mlp_up, w_mlp_down):
    b = x.shape[0]
    meta = jnp.broadcast_to(meta_tokens.astype(x.dtype)[None], (b, N_META, D_MODEL))
    h = jnp.concatenate([meta, x], axis=1)
    cos, sin = rope_tables(h.shape[1])
    for layer in range(DEPTH):
        j = layer // 2
        hn = rms_norm(h, g_mix[layer])
        if layer % 2 == 0:
            h = h + attention_mixer(hn, cos, sin, w_in_attn[j], g_cq[j], w_uq[j], g_ckv[j],
                                    w_ukv[j], g_q_mla[j], g_k_mla[j], g_q_fox[j], g_k_fox[j],
                                    b_forget[j], w_out_attn[j])
        else:
            h = h + short_conv_mixer(hn, w_in_conv[j], conv_w[j], w_out_conv[j])
        h = h + sq_relu_mlp(rms_norm(h, g_mlp[layer]), w_mlp_up[layer], w_mlp_down[layer])
    return h[:, N_META:]
```

```python
import functools

import numpy as np
import jax
import jax.numpy as jnp
from jax import lax
from jax.experimental import pallas as pl
from jax.experimental.pallas import tpu as pltpu

D_MODEL = 1024
DEPTH = 4
N_META = 16
BLOCK = 128
PAD = BLOCK - N_META
MLA_HEADS = 8
MLA_NOPE = 64
MLA_ROPE = 32
MLA_QK = MLA_NOPE + MLA_ROPE
MLA_V = 64
Q_LORA = 384
KV_LORA = 256
ROPE_BASE = 10000.0
FOX_HEADS = 8
FOX_DIM = 64
D_FF = 4 * D_MODEL
EPS = 1e-6
NEG = -1e30

LANE = 128
HEADS = MLA_HEADS + FOX_HEADS
HALF_ROPE = MLA_ROPE // 2
GATE_LANE = FOX_DIM
N_SPLIT = 3
MISC_GATE = 0
MISC_ROPE = MLA_NOPE

OFF_CQ = 0
OFF_CKV = OFF_CQ + Q_LORA
OFF_FQ = OFF_CKV + KV_LORA
OFF_FK = OFF_FQ + FOX_HEADS * LANE
OFF_FV = OFF_FK + FOX_HEADS * LANE
OFF_MISC = OFF_FV + FOX_HEADS * FOX_DIM
W_CAT = OFF_MISC + LANE

FF_CHUNK = 1024
VMEM_LIMIT = 56 * 1024 * 1024


def _const_spec(shape):
    nd = len(shape)
    return pl.BlockSpec(shape, lambda *_: (0,) * nd, pipeline_mode=pl.Buffered(1))


def _rms(x, g, n):
    ms = jnp.sum(x * x, axis=-1, keepdims=True) * (1.0 / n)
    return x * lax.rsqrt(ms + EPS) * g


def _split3(x):
    hi = x.astype(jnp.bfloat16)
    r1 = x - hi.astype(jnp.float32)
    mid = r1.astype(jnp.bfloat16)
    lo = (r1 - mid.astype(jnp.float32)).astype(jnp.bfloat16)
    return jnp.concatenate([hi, mid, lo], axis=1)


def _dot(a, b):
    return jnp.dot(a, b, preferred_element_type=jnp.float32)


def _attn_in_kernel(h_ref, gmix_ref, wcat_ref, gcq_ref, wuq_ref, gckv_ref, wkn_ref, wv_ref,
                    vec_ref, rope_ref, tri_ref, selq_ref, selk_ref,
                    q_ref, k_ref, v_ref, carry_ref, *, tm):
    i = pl.program_id(1)

    @pl.when(i == 0)
    def _():
        carry_ref[...] = jnp.zeros_like(carry_ref)

    x = h_ref[0]
    hn = _rms(x, gmix_ref[...], D_MODEL).astype(jnp.bfloat16)

    def seg(lo, width):
        return _dot(hn, wcat_ref[:, lo:lo + width])

    gq_mla = vec_ref[0:1, :]
    gk_mla = vec_ref[1:2, :]
    gq_fox = vec_ref[2:3, :]
    gk_fox = vec_ref[3:4, :]
    b_forget = vec_ref[4:5, :]
    ones_q = vec_ref[5:6, :]
    ones_k = vec_ref[6:7, :]
    cos_t = rope_ref[:, 0:LANE]
    sin_up = rope_ref[:, LANE:2 * LANE]
    sin_dn = rope_ref[:, 2 * LANE:3 * LANE]

    lane = lax.broadcasted_iota(jnp.int32, (tm, LANE), 1)
    row = lax.broadcasted_iota(jnp.int32, (tm, LANE), 0)
    valid = (i * tm + row) >= PAD

    misc = seg(OFF_MISC, LANE)
    kpe = jnp.where((lane >= MISC_ROPE) & (lane < MISC_ROPE + MLA_ROPE), misc, 0.0)
    xl = misc + b_forget
    logf = jnp.minimum(xl, 0.0) - jnp.log1p(jnp.exp(-jnp.abs(xl)))
    logf = jnp.where(valid & (lane >= MISC_GATE) & (lane < MISC_GATE + FOX_HEADS), logf, 0.0)
    cs = _dot(tri_ref[...], _split3(logf))
    cum = (cs[:, 0:LANE] + cs[:, LANE:2 * LANE]) + cs[:, 2 * LANE:3 * LANE] + carry_ref[0:1, :]
    carry_ref[0:1, :] = cum[tm - 1:tm, :]
    cum3 = _split3(cum)
    gate_q = _dot(cum3, selq_ref[...])
    gate_k = _dot(cum3, selk_ref[...])

    def rope(y):
        return (y * cos_t + pltpu.roll(y, HALF_ROPE, 1) * sin_up
                + pltpu.roll(y, LANE - HALF_ROPE, 1) * sin_dn)

    cqn = _rms(seg(OFF_CQ, Q_LORA), gcq_ref[...], Q_LORA).astype(jnp.bfloat16)
    ckvn = _rms(seg(OFF_CKV, KV_LORA), gckv_ref[...], KV_LORA).astype(jnp.bfloat16)
    v_ref[0, :, 0:MLA_HEADS * MLA_V] = _dot(ckvn, wv_ref[...]).astype(jnp.bfloat16)
    for hd in range(MLA_HEADS):
        sl = slice(hd * LANE, (hd + 1) * LANE)
        xq = _dot(cqn, wuq_ref[:, sl])
        q_ref[0, hd] = rope(_rms(xq, gq_mla, MLA_QK)).astype(jnp.bfloat16)
        xk = _dot(ckvn, wkn_ref[:, sl]) + kpe
        k_ref[0, hd] = rope(_rms(xk, gk_mla, MLA_QK)).astype(jnp.bfloat16)

    v_ref[0, :, MLA_HEADS * MLA_V:] = seg(OFF_FV, FOX_HEADS * FOX_DIM).astype(jnp.bfloat16)
    for hd in range(FOX_HEADS):
        sl = slice(hd * LANE, (hd + 1) * LANE)
        xq = seg(OFF_FQ + hd * LANE, LANE)
        q_ref[0, MLA_HEADS + hd] = (_rms(xq, gq_fox, FOX_DIM) + gate_q[:, sl] + ones_q
                                    ).astype(jnp.bfloat16)
        xk = seg(OFF_FK + hd * LANE, LANE)
        k_ref[0, MLA_HEADS + hd] = (_rms(xk, gk_fox, FOX_DIM) + gate_k[:, sl] + ones_k
                                    ).astype(jnp.bfloat16)


def _attn_in(h, gmix, p, rope_tab, tri, selq, selk, tm):
    b, lp, d = h.shape
    nt = lp // tm
    kern = functools.partial(_attn_in_kernel, tm=tm)
    qk_shape = jax.ShapeDtypeStruct((b, HEADS, lp, LANE), jnp.bfloat16)
    qk_spec = pl.BlockSpec((1, HEADS, tm, LANE), lambda bi, i: (bi, 0, i, 0))
    return pl.pallas_call(
        kern,
        grid=(b, nt),
        in_specs=[
            pl.BlockSpec((1, tm, d), lambda bi, i: (bi, i, 0)),
            _const_spec((1, d)),
            _const_spec((d, W_CAT)),
            _const_spec((1, Q_LORA)),
            _const_spec((Q_LORA, MLA_HEADS * LANE)),
            _const_spec((1, KV_LORA)),
            _const_spec((KV_LORA, MLA_HEADS * LANE)),
            _const_spec((KV_LORA, MLA_HEADS * MLA_V)),
            _const_spec((8, LANE)),
            pl.BlockSpec((tm, 3 * LANE), lambda bi, i: (i, 0)),
            _const_spec((tm, tm)),
            _const_spec((N_SPLIT * LANE, FOX_HEADS * LANE)),
            _const_spec((N_SPLIT * LANE, FOX_HEADS * LANE)),
        ],
        out_specs=[qk_spec, qk_spec,
                   pl.BlockSpec((1, tm, HEADS * MLA_V), lambda bi, i: (bi, i, 0))],
        out_shape=[qk_shape, qk_shape,
                   jax.ShapeDtypeStruct((b, lp, HEADS * MLA_V), jnp.bfloat16)],
        scratch_shapes=[pltpu.VMEM((8, LANE), jnp.float32)],
        compiler_params=pltpu.CompilerParams(
            dimension_semantics=("arbitrary", "arbitrary"), vmem_limit_bytes=VMEM_LIMIT),
        name="attn_in",
    )(h, gmix, p["wcat"], p["gcq"], p["wuq"], p["gckv"], p["wkn"], p["wv"], p["vec"],
      rope_tab, tri, selq, selk)


def _flash_kernel(q_ref, k_ref, v_ref, o_ref, m_ref, l_ref, acc_ref, *, tq, tk):
    qi = pl.program_id(2)
    m_ref[...] = jnp.full(m_ref.shape, NEG, jnp.float32)
    l_ref[...] = jnp.zeros(l_ref.shape, jnp.float32)
    acc_ref[...] = jnp.zeros(acc_ref.shape, jnp.float32)
    reps = tk // LANE

    def step(ki, masked):
        start = ki * tk
        if not isinstance(start, int):
            start = pl.multiple_of(start, tk)
        vb = v_ref[0, pl.ds(start, tk), :]
        if masked:
            qpos = qi * tq + lax.broadcasted_iota(jnp.int32, (tq, tk), 0)
            kpos = start + lax.broadcasted_iota(jnp.int32, (tq, tk), 1)
            keep = (kpos <= qpos) & (kpos >= PAD)
        for j in range(2):
            s = lax.dot_general(q_ref[0, j], k_ref[0, j, pl.ds(start, tk), :],
                                (((1,), (1,)), ((), ())), preferred_element_type=jnp.float32)
            if masked:
                s = jnp.where(keep, s, NEG)
            m_prev = m_ref[j]
            m_next = jnp.maximum(m_prev, jnp.max(s, axis=1, keepdims=True))
            p = jnp.exp(s - jnp.concatenate([m_next] * reps, axis=1))
            alpha = jnp.exp(m_prev - m_next)
            l_ref[j] = alpha * l_ref[j] + jnp.sum(p, axis=1, keepdims=True)
            acc_ref[j] = acc_ref[j] * alpha + _dot(p.astype(jnp.bfloat16), vb)
            m_ref[j] = m_next

    step(0, True)

    def body(ki, c):
        step(ki, False)
        return c

    lax.fori_loop(1, qi, body, 0)

    @pl.when(qi > 0)
    def _():
        step(qi, True)

    lane = lax.broadcasted_iota(jnp.int32, (tq, LANE), 1)
    o = jnp.where(lane < MLA_V, acc_ref[0] / l_ref[0], acc_ref[1] / l_ref[1])
    o_ref[0] = o.astype(jnp.bfloat16)


def _flash(q, k, v, tq):
    b, _, lp, _ = q.shape
    nq = lp // tq
    kern = functools.partial(_flash_kernel, tq=tq, tk=tq)
    return pl.pallas_call(
        kern,
        grid=(b, HEADS // 2, nq),
        in_specs=[
            pl.BlockSpec((1, 2, tq, LANE), lambda bi, hp, i: (bi, hp, i, 0)),
            pl.BlockSpec((1, 2, lp, LANE), lambda bi, hp, i: (bi, hp, 0, 0)),
            pl.BlockSpec((1, lp, LANE), lambda bi, hp, i: (bi, 0, hp)),
        ],
        out_specs=pl.BlockSpec((1, tq, LANE), lambda bi, hp, i: (bi, i, hp)),
        out_shape=jax.ShapeDtypeStruct((b, lp, HEADS * MLA_V), jnp.bfloat16),
        scratch_shapes=[pltpu.VMEM((2, tq, LANE), jnp.float32)] * 3,
        compiler_params=pltpu.CompilerParams(
            dimension_semantics=("arbitrary", "arbitrary", "arbitrary"),
            vmem_limit_bytes=VMEM_LIMIT),
        name="flash",
    )(q, k, v)


def _conv_in_kernel(h_ref, gmix_ref, win_ref, cw_ref, y_ref, gs_ref, *, tm):
    i = pl.program_id(1)

    @pl.when(i == 0)
    def _():
        gs_ref[0:8, :] = jnp.zeros((8, D_MODEL), jnp.float32)

    x = h_ref[0]
    hn = _rms(x, gmix_ref[...], D_MODEL).astype(jnp.bfloat16)
    gate_c = _dot(hn, win_ref[:, D_MODEL:2 * D_MODEL])
    u = _dot(hn, win_ref[:, 2 * D_MODEL:3 * D_MODEL])
    row = lax.broadcasted_iota(jnp.int32, (tm, D_MODEL), 0)
    g = jnp.where((i * tm + row) >= PAD, gate_c * u, 0.0)
    gs_ref[8:tm + 8, :] = g
    y = (cw_ref[0:1, :] * gs_ref[6:tm + 6, :] + cw_ref[1:2, :] * gs_ref[7:tm + 7, :]
         + cw_ref[2:3, :] * g)
    gs_ref[0:8, :] = gs_ref[tm:tm + 8, :]
    gate_b = _dot(hn, win_ref[:, 0:D_MODEL])
    y_ref[0] = (gate_b * y).astype(jnp.bfloat16)


def _conv_in(h, gmix, win, cw, tm):
    b, lp, d = h.shape
    kern = functools.partial(_conv_in_kernel, tm=tm)
    return pl.pallas_call(
        kern,
        grid=(b, lp // tm),
        in_specs=[
            pl.BlockSpec((1, tm, d), lambda bi, i: (bi, i, 0)),
            _const_spec((1, d)),
            _const_spec((d, 3 * d)),
            _const_spec((8, d)),
        ],
        out_specs=pl.BlockSpec((1, tm, d), lambda bi, i: (bi, i, 0)),
        out_shape=jax.ShapeDtypeStruct((b, lp, d), jnp.bfloat16),
        scratch_shapes=[pltpu.VMEM((tm + 8, d), jnp.float32)],
        compiler_params=pltpu.CompilerParams(
            dimension_semantics=("arbitrary", "arbitrary"), vmem_limit_bytes=VMEM_LIMIT),
        name="conv_in",
    )(h, gmix, win, cw)


def _mix_out_mlp_kernel(h_ref, y_ref, wo_ref, gmlp_ref, wup_ref, wdn_ref, out_ref):
    h1 = h_ref[...] + _dot(y_ref[...], wo_ref[...])
    n = _rms(h1, gmlp_ref[...], D_MODEL).astype(jnp.bfloat16)
    acc = h1
    for c in range(D_FF // FF_CHUNK):
        sl = slice(c * FF_CHUNK, (c + 1) * FF_CHUNK)
        a = jnp.maximum(_dot(n, wup_ref[:, sl]), 0.0)
        acc = acc + _dot((a * a).astype(jnp.bfloat16), wdn_ref[sl, :])
    out_ref[...] = acc


def _mix_out_mlp(h, y, wo, gmlp, wup, wdn, tm):
    r, d = h.shape
    return pl.pallas_call(
        _mix_out_mlp_kernel,
        grid=(r // tm,),
        in_specs=[
            pl.BlockSpec((tm, d), lambda i: (i, 0)),
            pl.BlockSpec((tm, d), lambda i: (i, 0)),
            _const_spec((d, d)),
            _const_spec((1, d)),
            _const_spec((d, D_FF)),
            _const_spec((D_FF, d)),
        ],
        out_specs=pl.BlockSpec((tm, d), lambda i: (i, 0)),
        out_shape=jax.ShapeDtypeStruct((r, d), jnp.float32),
        compiler_params=pltpu.CompilerParams(
            dimension_semantics=("arbitrary",), vmem_limit_bytes=VMEM_LIMIT),
        name="mix_out_mlp",
    )(h, y, wo, gmlp, wup, wdn)


def _pad_heads(w, heads, dim):
    k = w.shape[0]
    w = w.reshape(k, heads, dim)
    w = jnp.pad(w, ((0, 0), (0, 0), (0, LANE - dim)))
    return w.reshape(k, heads * LANE)


def _lane_vec(v, offset=0):
    return jnp.zeros((LANE,), jnp.float32).at[offset:offset + v.shape[0]].set(v)


def _attn_params(w_in, g_cq, w_uq, g_ckv, w_ukv, g_q_mla, g_k_mla, g_q_fox, g_k_fox, b_forget):
    bf = jnp.bfloat16
    o1 = Q_LORA
    o2 = o1 + KV_LORA
    o3 = o2 + MLA_ROPE
    o4 = o3 + FOX_HEADS * FOX_DIM
    o5 = o4 + FOX_HEADS * FOX_DIM
    o6 = o5 + FOX_HEADS * FOX_DIM
    misc = jnp.zeros((D_MODEL, LANE), jnp.float32)
    misc = misc.at[:, MISC_GATE:MISC_GATE + FOX_HEADS].set(w_in[:, o6:])
    misc = misc.at[:, MISC_ROPE:MISC_ROPE + MLA_ROPE].set(w_in[:, o2:o3])
    wcat = jnp.concatenate([
        w_in[:, :o1], w_in[:, o1:o2],
        _pad_heads(w_in[:, o3:o4], FOX_HEADS, FOX_DIM),
        _pad_heads(w_in[:, o4:o5], FOX_HEADS, FOX_DIM),
        w_in[:, o5:o6], misc], axis=1).astype(bf)
    kv = w_ukv.reshape(KV_LORA, MLA_HEADS, MLA_NOPE + MLA_V)
    wkn = _pad_heads(kv[:, :, :MLA_NOPE].reshape(KV_LORA, -1), MLA_HEADS, MLA_NOPE).astype(bf)
    wv = kv[:, :, MLA_NOPE:].reshape(KV_LORA, MLA_HEADS * MLA_V).astype(bf)
    wuq = _pad_heads(w_uq, MLA_HEADS, MLA_QK).astype(bf)
    ones_q = jnp.zeros((LANE,), jnp.float32).at[GATE_LANE + N_SPLIT:GATE_LANE + 2 * N_SPLIT].set(1.0)
    ones_k = jnp.zeros((LANE,), jnp.float32).at[GATE_LANE:GATE_LANE + N_SPLIT].set(1.0)
    vec = jnp.stack([
        _lane_vec(g_q_mla) * (MLA_QK ** -0.5), _lane_vec(g_k_mla),
        _lane_vec(g_q_fox) * (FOX_DIM ** -0.5), _lane_vec(g_k_fox),
        _lane_vec(b_forget, MISC_GATE), ones_q, ones_k, jnp.zeros((LANE,), jnp.float32)])
    return dict(wcat=wcat, gcq=g_cq[None], wuq=wuq, gckv=g_ckv[None], wkn=wkn, wv=wv, vec=vec)


def _gate_selectors():
    selq = np.zeros((N_SPLIT * LANE, FOX_HEADS * LANE), np.float32)
    selk = np.zeros((N_SPLIT * LANE, FOX_HEADS * LANE), np.float32)
    for part in range(N_SPLIT):
        for hd in range(FOX_HEADS):
            selq[part * LANE + MISC_GATE + hd, hd * LANE + GATE_LANE + part] = 1.0
            selk[part * LANE + MISC_GATE + hd, hd * LANE + GATE_LANE + N_SPLIT + part] = -1.0
    return jnp.asarray(selq, jnp.bfloat16), jnp.asarray(selk, jnp.bfloat16)


def _rope_table(lp):
    pos = (jnp.arange(lp, dtype=jnp.int32) - PAD).astype(jnp.float32)
    inv_freq = ROPE_BASE ** (-jnp.arange(0, MLA_ROPE, 2, dtype=jnp.float32) / MLA_ROPE)
    ang = pos[:, None] * inv_freq[None, :]
    cos, sin = jnp.cos(ang), jnp.sin(ang)
    z = jnp.zeros((lp, LANE), jnp.float32)
    lo, mid, hi = MLA_NOPE, MLA_NOPE + HALF_ROPE, MLA_NOPE + MLA_ROPE
    cos_t = z.at[:, :lo].set(1.0).at[:, lo:mid].set(cos).at[:, mid:hi].set(cos)
    sin_up = z.at[:, mid:hi].set(sin)
    sin_dn = z.at[:, lo:mid].set(-sin)
    return jnp.concatenate([cos_t, sin_up, sin_dn], axis=1)


def _token_tile(lp):
    for t in (640, 512, 384, 256, 128):
        if lp % t == 0:
            return t
    raise ValueError(f"padded length {lp} is not a multiple of {BLOCK}")


def kernel(x, meta_tokens, g_mix, g_mlp, w_in_attn, g_cq, w_uq, g_ckv, w_ukv, g_q_mla, g_k_mla,
           g_q_fox, g_k_fox, b_forget, w_out_attn, w_in_conv, conv_w, w_out_conv, w_mlp_up,
           w_mlp_down):
    b, seq, d = x.shape
    assert d == D_MODEL and (PAD + N_META + seq) % BLOCK == 0
    lp = PAD + N_META + seq
    tm = _token_tile(lp)
    bf = jnp.bfloat16

    meta = jnp.broadcast_to(meta_tokens.astype(x.dtype)[None], (b, N_META, d))
    h = jnp.concatenate([jnp.zeros((b, PAD, d), x.dtype), meta, x], axis=1)

    rope_tab = _rope_table(lp)
    tri = (jnp.arange(tm)[:, None] >= jnp.arange(tm)[None, :]).astype(bf)
    selq, selk = _gate_selectors()

    for layer in range(DEPTH):
        j = layer // 2
        gmix = g_mix[layer][None]
        if layer % 2 == 0:
            p = _attn_params(w_in_attn[j], g_cq[j], w_uq[j], g_ckv[j], w_ukv[j], g_q_mla[j],
                             g_k_mla[j], g_q_fox[j], g_k_fox[j], b_forget[j])
            q, k, v = _attn_in(h, gmix, p, rope_tab, tri, selq, selk, tm)
            y = _flash(q, k, v, tm)
            wo = w_out_attn[j].astype(bf)
        else:
            cw = jnp.zeros((8, d), jnp.float32).at[0:3].set(conv_w[j])
            y = _conv_in(h, gmix, w_in_conv[j].astype(bf), cw, tm)
            wo = w_out_conv[j].astype(bf)
        h = _mix_out_mlp(h.reshape(b * lp, d), y.reshape(b * lp, d), wo, g_mlp[layer][None],
                         w_mlp_up[layer].astype(bf), w_mlp_down[layer].astype(bf), tm
                         ).reshape(b, lp, d)
    return h[:, PAD + N_META:]
```

```python
import functools

import numpy as np
import jax
import jax.numpy as jnp
from jax import lax
from jax.experimental import pallas as pl
from jax.experimental.pallas import tpu as pltpu

D_MODEL = 1024
DEPTH = 4
N_META = 16
BLOCK = 128
PAD = 2 * BLOCK - N_META
MLA_HEADS = 8
MLA_NOPE = 64
MLA_ROPE = 32
MLA_QK = MLA_NOPE + MLA_ROPE
MLA_V = 64
Q_LORA = 384
KV_LORA = 256
ROPE_BASE = 10000.0
FOX_HEADS = 8
FOX_DIM = 64
D_FF = 4 * D_MODEL
EPS = 1e-6
NEG = -1e30

LANE = 128
HEADS = MLA_HEADS + FOX_HEADS
HALF_ROPE = MLA_ROPE // 2
GATE_LANE = FOX_DIM
N_SPLIT = 3
FLAG_MLA = MLA_QK
FLAG_FOX = GATE_LANE + 2 * N_SPLIT
PAD_KEY = -30000.0
LOG2E = 1.4426950408889634
MISC_GATE = 0
MISC_ROPE = MLA_NOPE

OFF_CQ = 0
OFF_CKV = OFF_CQ + Q_LORA
OFF_FQ = OFF_CKV + KV_LORA
OFF_FK = OFF_FQ + FOX_HEADS * LANE
OFF_FV = OFF_FK + FOX_HEADS * LANE
OFF_MISC = OFF_FV + FOX_HEADS * FOX_DIM
W_CAT = OFF_MISC + LANE

FF_CHUNK = 1024
VMEM_LIMIT = 56 * 1024 * 1024


def _const_spec(shape):
    nd = len(shape)
    return pl.BlockSpec(shape, lambda *_: (0,) * nd, pipeline_mode=pl.Buffered(1))


def _rms(x, g, n):
    ms = jnp.sum(x * x, axis=-1, keepdims=True) * (1.0 / n)
    return x * lax.rsqrt(ms + EPS) * g


def _split3(x):
    hi = x.astype(jnp.bfloat16)
    r1 = x - hi.astype(jnp.float32)
    mid = r1.astype(jnp.bfloat16)
    lo = (r1 - mid.astype(jnp.float32)).astype(jnp.bfloat16)
    return jnp.concatenate([hi, mid, lo], axis=1)


def _dot(a, b):
    return jnp.dot(a, b, preferred_element_type=jnp.float32)


def _attn_in_kernel(h_ref, gmix_ref, wcat_ref, gcq_ref, wuq_ref, gckv_ref, wkn_ref, wv_ref,
                    vec_ref, rope_ref, tri_ref, selq_ref, selk_ref,
                    q_ref, k_ref, v_ref, carry_ref, *, tm):
    i = pl.program_id(1)

    @pl.when(i == 0)
    def _():
        carry_ref[...] = jnp.zeros_like(carry_ref)

    x = h_ref[0]
    hn = _rms(x, gmix_ref[...], D_MODEL).astype(jnp.bfloat16)

    def seg(lo, width):
        return _dot(hn, wcat_ref[:, lo:lo + width])

    gq_mla = vec_ref[0:1, :]
    gk_mla = vec_ref[1:2, :]
    gq_fox = vec_ref[2:3, :]
    gk_fox = vec_ref[3:4, :]
    b_forget = vec_ref[4:5, :]
    add_q_fox = vec_ref[5:6, :]
    ones_k_fox = vec_ref[6:7, :]
    add_q_mla = vec_ref[7:8, :]
    cos_t = rope_ref[:, 0:LANE]
    sin_up = rope_ref[:, LANE:2 * LANE]
    sin_dn = rope_ref[:, 2 * LANE:3 * LANE]

    lane = lax.broadcasted_iota(jnp.int32, (tm, LANE), 1)
    row = lax.broadcasted_iota(jnp.int32, (tm, LANE), 0)
    valid = (i * tm + row) >= PAD
    pad_key = jnp.where(valid, 0.0, PAD_KEY)
    add_k_mla = jnp.where(lane == FLAG_MLA, pad_key, 0.0)
    add_k_fox = ones_k_fox + jnp.where(lane == FLAG_FOX, pad_key, 0.0)

    misc = seg(OFF_MISC, LANE)
    kpe = jnp.where((lane >= MISC_ROPE) & (lane < MISC_ROPE + MLA_ROPE), misc, 0.0)
    xl = misc + b_forget
    logf = jnp.minimum(xl, 0.0) - jnp.log1p(jnp.exp(-jnp.abs(xl)))
    logf = jnp.where(valid & (lane >= MISC_GATE) & (lane < MISC_GATE + FOX_HEADS), logf, 0.0)
    cs = _dot(tri_ref[...], _split3(logf))
    cum = (cs[:, 0:LANE] + cs[:, LANE:2 * LANE]) + cs[:, 2 * LANE:3 * LANE] + carry_ref[0:1, :]
    carry_ref[0:1, :] = cum[tm - 1:tm, :]
    cum3 = _split3(cum * LOG2E)
    gate_q = _dot(cum3, selq_ref[...])
    gate_k = _dot(cum3, selk_ref[...])

    def rope(y):
        return (y * cos_t + pltpu.roll(y, HALF_ROPE, 1) * sin_up
                + pltpu.roll(y, LANE - HALF_ROPE, 1) * sin_dn)

    cqn = _rms(seg(OFF_CQ, Q_LORA), gcq_ref[...], Q_LORA).astype(jnp.bfloat16)
    ckvn = _rms(seg(OFF_CKV, KV_LORA), gckv_ref[...], KV_LORA).astype(jnp.bfloat16)
    v_ref[0, :, 0:MLA_HEADS * MLA_V] = _dot(ckvn, wv_ref[...]).astype(jnp.bfloat16)
    for hd in range(MLA_HEADS):
        sl = slice(hd * LANE, (hd + 1) * LANE)
        xq = _dot(cqn, wuq_ref[:, sl])
        q_ref[0, hd] = (rope(_rms(xq, gq_mla, MLA_QK)) + add_q_mla).astype(jnp.bfloat16)
        xk = _dot(ckvn, wkn_ref[:, sl]) + kpe
        k_ref[0, hd] = (rope(_rms(xk, gk_mla, MLA_QK)) + add_k_mla).astype(jnp.bfloat16)

    v_ref[0, :, MLA_HEADS * MLA_V:] = seg(OFF_FV, FOX_HEADS * FOX_DIM).astype(jnp.bfloat16)
    for hd in range(FOX_HEADS):
        sl = slice(hd * LANE, (hd + 1) * LANE)
        xq = seg(OFF_FQ + hd * LANE, LANE)
        q_ref[0, MLA_HEADS + hd] = (_rms(xq, gq_fox, FOX_DIM) + gate_q[:, sl] + add_q_fox
                                    ).astype(jnp.bfloat16)
        xk = seg(OFF_FK + hd * LANE, LANE)
        k_ref[0, MLA_HEADS + hd] = (_rms(xk, gk_fox, FOX_DIM) + gate_k[:, sl] + add_k_fox
                                    ).astype(jnp.bfloat16)


def _attn_in(h, gmix, p, rope_tab, tri, selq, selk, tm):
    b, lp, d = h.shape
    nt = lp // tm
    kern = functools.partial(_attn_in_kernel, tm=tm)
    qk_shape = jax.ShapeDtypeStruct((b, HEADS, lp, LANE), jnp.bfloat16)
    qk_spec = pl.BlockSpec((1, HEADS, tm, LANE), lambda bi, i: (bi, 0, i, 0))
    return pl.pallas_call(
        kern,
        grid=(b, nt),
        in_specs=[
            pl.BlockSpec((1, tm, d), lambda bi, i: (bi, i, 0)),
            _const_spec((1, d)),
            _const_spec((d, W_CAT)),
            _const_spec((1, Q_LORA)),
            _const_spec((Q_LORA, MLA_HEADS * LANE)),
            _const_spec((1, KV_LORA)),
            _const_spec((KV_LORA, MLA_HEADS * LANE)),
            _const_spec((KV_LORA, MLA_HEADS * MLA_V)),
            _const_spec((8, LANE)),
            pl.BlockSpec((tm, 3 * LANE), lambda bi, i: (i, 0)),
            _const_spec((tm, tm)),
            _const_spec((N_SPLIT * LANE, FOX_HEADS * LANE)),
            _const_spec((N_SPLIT * LANE, FOX_HEADS * LANE)),
        ],
        out_specs=[qk_spec, qk_spec,
                   pl.BlockSpec((1, tm, HEADS * MLA_V), lambda bi, i: (bi, i, 0))],
        out_shape=[qk_shape, qk_shape,
                   jax.ShapeDtypeStruct((b, lp, HEADS * MLA_V), jnp.bfloat16)],
        scratch_shapes=[pltpu.VMEM((8, LANE), jnp.float32)],
        compiler_params=pltpu.CompilerParams(
            dimension_semantics=("arbitrary", "arbitrary"), vmem_limit_bytes=VMEM_LIMIT),
        name="attn_in",
    )(h, gmix, p["wcat"], p["gcq"], p["wuq"], p["gckv"], p["wkn"], p["wv"], p["vec"],
      rope_tab, tri, selq, selk)


def _flash_kernel(q_ref, k_ref, v_ref, o_ref, m_ref, l_ref, acc_ref, al_ref, p_ref, *, tq, tk):
    qi = pl.program_id(2)
    m_ref[...] = jnp.full(m_ref.shape, NEG, jnp.float32)
    l_ref[...] = jnp.zeros(l_ref.shape, jnp.float32)
    acc_ref[...] = jnp.zeros(acc_ref.shape, jnp.float32)
    al_ref[1] = jnp.zeros((tq, LANE), jnp.float32)
    p_ref[1] = jnp.zeros((tq, tk), jnp.bfloat16)
    reps = tk // LANE

    def softmax(j, start, masked):
        s = lax.dot_general(q_ref[0, j], k_ref[0, j, pl.ds(start, tk), :],
                            (((1,), (1,)), ((), ())), preferred_element_type=jnp.float32)
        if masked:
            row = lax.broadcasted_iota(jnp.int32, (tq, tk), 0)
            col = lax.broadcasted_iota(jnp.int32, (tq, tk), 1)
            s = jnp.where(col <= row, s, NEG)
        m_prev = m_ref[j]
        m_next = jnp.maximum(m_prev, jnp.max(s, axis=1, keepdims=True))
        p = jnp.exp2(s - jnp.concatenate([m_next] * reps, axis=1))
        alpha = jnp.exp2(m_prev - m_next)
        l_ref[j] = alpha * l_ref[j] + jnp.sum(p, axis=1, keepdims=True)
        al_ref[j] = alpha
        p_ref[j] = p.astype(jnp.bfloat16)
        m_ref[j] = m_next

    def pv(j, start):
        acc_ref[j] = acc_ref[j] * al_ref[j] + _dot(p_ref[j], v_ref[0, pl.ds(start, tk), :])

    def step(ki, masked):
        start = pl.multiple_of(ki * tk, tk)
        prev = pl.multiple_of(jnp.maximum(ki - 1, 0) * tk, tk)
        softmax(0, start, masked)
        pv(1, prev)
        softmax(1, start, masked)
        pv(0, start)

    def body(ki, c):
        step(ki, False)
        return c

    lax.fori_loop(0, qi, body, 0)
    step(qi, True)
    pv(1, pl.multiple_of(qi * tk, tk))

    lane = lax.broadcasted_iota(jnp.int32, (tq, LANE), 1)
    o = jnp.where(lane < MLA_V, acc_ref[0] / l_ref[0], acc_ref[1] / l_ref[1])
    o_ref[0] = o.astype(jnp.bfloat16)


def _flash(q, k, v, tq):
    b, _, lp, _ = q.shape
    nq = lp // tq
    kern = functools.partial(_flash_kernel, tq=tq, tk=tq)
    return pl.pallas_call(
        kern,
        grid=(b, HEADS // 2, nq),
        in_specs=[
            pl.BlockSpec((1, 2, tq, LANE), lambda bi, hp, i: (bi, hp, i, 0)),
            pl.BlockSpec((1, 2, lp, LANE), lambda bi, hp, i: (bi, hp, 0, 0)),
            pl.BlockSpec((1, lp, LANE), lambda bi, hp, i: (bi, 0, hp)),
        ],
        out_specs=pl.BlockSpec((1, tq, LANE), lambda bi, hp, i: (bi, i, hp)),
        out_shape=jax.ShapeDtypeStruct((b, lp, HEADS * MLA_V), jnp.bfloat16),
        scratch_shapes=[pltpu.VMEM((2, tq, LANE), jnp.float32)] * 4
        + [pltpu.VMEM((2, tq, tq), jnp.bfloat16)],
        compiler_params=pltpu.CompilerParams(
            dimension_semantics=("arbitrary", "arbitrary", "arbitrary"),
            vmem_limit_bytes=VMEM_LIMIT),
        name="flash",
    )(q, k, v)


def _conv_in_kernel(h_ref, gmix_ref, win_ref, cw_ref, y_ref, gs_ref, *, tm):
    i = pl.program_id(1)

    @pl.when(i == 0)
    def _():
        gs_ref[0:8, :] = jnp.zeros((8, D_MODEL), jnp.float32)

    x = h_ref[0]
    hn = _rms(x, gmix_ref[...], D_MODEL).astype(jnp.bfloat16)
    gate_c = _dot(hn, win_ref[:, D_MODEL:2 * D_MODEL])
    u = _dot(hn, win_ref[:, 2 * D_MODEL:3 * D_MODEL])
    row = lax.broadcasted_iota(jnp.int32, (tm, D_MODEL), 0)
    g = jnp.where((i * tm + row) >= PAD, gate_c * u, 0.0)
    gs_ref[8:tm + 8, :] = g
    y = (cw_ref[0:1, :] * gs_ref[6:tm + 6, :] + cw_ref[1:2, :] * gs_ref[7:tm + 7, :]
         + cw_ref[2:3, :] * g)
    gs_ref[0:8, :] = gs_ref[tm:tm + 8, :]
    gate_b = _dot(hn, win_ref[:, 0:D_MODEL])
    y_ref[0] = (gate_b * y).astype(jnp.bfloat16)


def _conv_in(h, gmix, win, cw, tm):
    b, lp, d = h.shape
    kern = functools.partial(_conv_in_kernel, tm=tm)
    return pl.pallas_call(
        kern,
        grid=(b, lp // tm),
        in_specs=[
            pl.BlockSpec((1, tm, d), lambda bi, i: (bi, i, 0)),
            _const_spec((1, d)),
            _const_spec((d, 3 * d)),
            _const_spec((8, d)),
        ],
        out_specs=pl.BlockSpec((1, tm, d), lambda bi, i: (bi, i, 0)),
        out_shape=jax.ShapeDtypeStruct((b, lp, d), jnp.bfloat16),
        scratch_shapes=[pltpu.VMEM((tm + 8, d), jnp.float32)],
        compiler_params=pltpu.CompilerParams(
            dimension_semantics=("arbitrary", "arbitrary"), vmem_limit_bytes=VMEM_LIMIT),
        name="conv_in",
    )(h, gmix, win, cw)


def _mix_out_mlp_kernel(h_ref, y_ref, wo_ref, gmlp_ref, wup_ref, wdn_ref, out_ref):
    h1 = h_ref[...] + _dot(y_ref[...], wo_ref[...])
    n = _rms(h1, gmlp_ref[...], D_MODEL).astype(jnp.bfloat16)
    acc = h1
    for c in range(D_FF // FF_CHUNK):
        sl = slice(c * FF_CHUNK, (c + 1) * FF_CHUNK)
        a = jnp.maximum(_dot(n, wup_ref[:, sl]), 0.0)
        acc = acc + _dot((a * a).astype(jnp.bfloat16), wdn_ref[sl, :])
    out_ref[...] = acc


def _mix_out_mlp(h, y, wo, gmlp, wup, wdn, tm):
    r, d = h.shape
    return pl.pallas_call(
        _mix_out_mlp_kernel,
        grid=(r // tm,),
        in_specs=[
            pl.BlockSpec((tm, d), lambda i: (i, 0)),
            pl.BlockSpec((tm, d), lambda i: (i, 0)),
            _const_spec((d, d)),
            _const_spec((1, d)),
            _const_spec((d, D_FF)),
            _const_spec((D_FF, d)),
        ],
        out_specs=pl.BlockSpec((tm, d), lambda i: (i, 0)),
        out_shape=jax.ShapeDtypeStruct((r, d), jnp.float32),
        compiler_params=pltpu.CompilerParams(
            dimension_semantics=("arbitrary",), vmem_limit_bytes=VMEM_LIMIT),
        name="mix_out_mlp",
    )(h, y, wo, gmlp, wup, wdn)


def _pad_heads(w, heads, dim):
    k = w.shape[0]
    w = w.reshape(k, heads, dim)
    w = jnp.pad(w, ((0, 0), (0, 0), (0, LANE - dim)))
    return w.reshape(k, heads * LANE)


def _lane_vec(v, offset=0):
    return jnp.zeros((LANE,), jnp.float32).at[offset:offset + v.shape[0]].set(v)


def _attn_params(w_in, g_cq, w_uq, g_ckv, w_ukv, g_q_mla, g_k_mla, g_q_fox, g_k_fox, b_forget):
    bf = jnp.bfloat16
    o1 = Q_LORA
    o2 = o1 + KV_LORA
    o3 = o2 + MLA_ROPE
    o4 = o3 + FOX_HEADS * FOX_DIM
    o5 = o4 + FOX_HEADS * FOX_DIM
    o6 = o5 + FOX_HEADS * FOX_DIM
    misc = jnp.zeros((D_MODEL, LANE), jnp.float32)
    misc = misc.at[:, MISC_GATE:MISC_GATE + FOX_HEADS].set(w_in[:, o6:])
    misc = misc.at[:, MISC_ROPE:MISC_ROPE + MLA_ROPE].set(w_in[:, o2:o3])
    wcat = jnp.concatenate([
        w_in[:, :o1], w_in[:, o1:o2],
        _pad_heads(w_in[:, o3:o4], FOX_HEADS, FOX_DIM),
        _pad_heads(w_in[:, o4:o5], FOX_HEADS, FOX_DIM),
        w_in[:, o5:o6], misc], axis=1).astype(bf)
    kv = w_ukv.reshape(KV_LORA, MLA_HEADS, MLA_NOPE + MLA_V)
    wkn = _pad_heads(kv[:, :, :MLA_NOPE].reshape(KV_LORA, -1), MLA_HEADS, MLA_NOPE).astype(bf)
    wv = kv[:, :, MLA_NOPE:].reshape(KV_LORA, MLA_HEADS * MLA_V).astype(bf)
    wuq = _pad_heads(w_uq, MLA_HEADS, MLA_QK).astype(bf)
    zero = jnp.zeros((LANE,), jnp.float32)
    add_q_fox = zero.at[GATE_LANE + N_SPLIT:GATE_LANE + 2 * N_SPLIT].set(1.0).at[FLAG_FOX].set(1.0)
    ones_k_fox = zero.at[GATE_LANE:GATE_LANE + N_SPLIT].set(1.0)
    add_q_mla = zero.at[FLAG_MLA].set(1.0)
    vec = jnp.stack([
        _lane_vec(g_q_mla) * (MLA_QK ** -0.5 * LOG2E), _lane_vec(g_k_mla),
        _lane_vec(g_q_fox) * (FOX_DIM ** -0.5 * LOG2E), _lane_vec(g_k_fox),
        _lane_vec(b_forget, MISC_GATE), add_q_fox, ones_k_fox, add_q_mla])
    return dict(wcat=wcat, gcq=g_cq[None], wuq=wuq, gckv=g_ckv[None], wkn=wkn, wv=wv, vec=vec)


def _gate_selectors():
    selq = np.zeros((N_SPLIT * LANE, FOX_HEADS * LANE), np.float32)
    selk = np.zeros((N_SPLIT * LANE, FOX_HEADS * LANE), np.float32)
    for part in range(N_SPLIT):
        for hd in range(FOX_HEADS):
            selq[part * LANE + MISC_GATE + hd, hd * LANE + GATE_LANE + part] = 1.0
            selk[part * LANE + MISC_GATE + hd, hd * LANE + GATE_LANE + N_SPLIT + part] = -1.0
    return jnp.asarray(selq, jnp.bfloat16), jnp.asarray(selk, jnp.bfloat16)


def _rope_table(lp):
    pos = (jnp.arange(lp, dtype=jnp.int32) - PAD).astype(jnp.float32)
    inv_freq = ROPE_BASE ** (-jnp.arange(0, MLA_ROPE, 2, dtype=jnp.float32) / MLA_ROPE)
    ang = pos[:, None] * inv_freq[None, :]
    cos, sin = jnp.cos(ang), jnp.sin(ang)
    z = jnp.zeros((lp, LANE), jnp.float32)
    lo, mid, hi = MLA_NOPE, MLA_NOPE + HALF_ROPE, MLA_NOPE + MLA_ROPE
    cos_t = z.at[:, :lo].set(1.0).at[:, lo:mid].set(cos).at[:, mid:hi].set(cos)
    sin_up = z.at[:, mid:hi].set(sin)
    sin_dn = z.at[:, lo:mid].set(-sin)
    return jnp.concatenate([cos_t, sin_up, sin_dn], axis=1)


def _token_tile(lp):
    for t in (768, 512, 256):
        if lp % t == 0:
            return t
    raise ValueError(f"padded length {lp} is not a multiple of 256")


def kernel(x, meta_tokens, g_mix, g_mlp, w_in_attn, g_cq, w_uq, g_ckv, w_ukv, g_q_mla, g_k_mla,
           g_q_fox, g_k_fox, b_forget, w_out_attn, w_in_conv, conv_w, w_out_conv, w_mlp_up,
           w_mlp_down):
    b, seq, d = x.shape
    assert d == D_MODEL and (PAD + N_META + seq) % BLOCK == 0
    lp = PAD + N_META + seq
    tm = _token_tile(lp)
    bf = jnp.bfloat16

    meta = jnp.broadcast_to(meta_tokens.astype(x.dtype)[None], (b, N_META, d))
    h = jnp.concatenate([jnp.zeros((b, PAD, d), x.dtype), meta, x], axis=1)

    rope_tab = _rope_table(lp)
    tri = (jnp.arange(tm)[:, None] >= jnp.arange(tm)[None, :]).astype(bf)
    selq, selk = _gate_selectors()

    for layer in range(DEPTH):
        j = layer // 2
        gmix = g_mix[layer][None]
        if layer % 2 == 0:
            p = _attn_params(w_in_attn[j], g_cq[j], w_uq[j], g_ckv[j], w_ukv[j], g_q_mla[j],
                             g_k_mla[j], g_q_fox[j], g_k_fox[j], b_forget[j])
            q, k, v = _attn_in(h, gmix, p, rope_tab, tri, selq, selk, tm)
            y = _flash(q, k, v, tm)
            wo = w_out_attn[j].astype(bf)
        else:
            cw = jnp.zeros((8, d), jnp.float32).at[0:3].set(conv_w[j])
            y = _conv_in(h, gmix, w_in_conv[j].astype(bf), cw, tm)
            wo = w_out_conv[j].astype(bf)
        h = _mix_out_mlp(h.reshape(b * lp, d), y.reshape(b * lp, d), wo, g_mlp[layer][None],
                         w_mlp_up[layer].astype(bf), w_mlp_down[layer].astype(bf), tm
                         ).reshape(b, lp, d)
    return h[:, PAD + N_META:]
```

```python
import functools

import numpy as np
import jax
import jax.numpy as jnp
from jax import lax
from jax.experimental import pallas as pl
from jax.experimental.pallas import tpu as pltpu

D_MODEL = 1024
DEPTH = 4
N_META = 16
BLOCK = 128
PAD = 2 * BLOCK - N_META
MLA_HEADS = 8
MLA_NOPE = 64
MLA_ROPE = 32
MLA_QK = MLA_NOPE + MLA_ROPE
MLA_V = 64
Q_LORA = 384
KV_LORA = 256
ROPE_BASE = 10000.0
FOX_HEADS = 8
FOX_DIM = 64
D_FF = 4 * D_MODEL
EPS = 1e-6
NEG = -1e30

LANE = 128
HEADS = MLA_HEADS + FOX_HEADS
HALF_ROPE = MLA_ROPE // 2
GATE_LANE = FOX_DIM
N_SPLIT = 3
FLAG_MLA = MLA_QK
FLAG_FOX = GATE_LANE + 2 * N_SPLIT
PAD_KEY = -30000.0
LOG2E = 1.4426950408889634
MISC_GATE = 0
MISC_ROPE = MLA_NOPE

OFF_CQ = 0
OFF_CKV = OFF_CQ + Q_LORA
OFF_FQ = OFF_CKV + KV_LORA
OFF_FK = OFF_FQ + FOX_HEADS * LANE
OFF_FV = OFF_FK + FOX_HEADS * LANE
OFF_MISC = OFF_FV + FOX_HEADS * FOX_DIM
OFF_MISC_SW = OFF_MISC + LANE
W_CAT = OFF_MISC_SW + LANE

(V_GQ_MLA, V_GQ_MLA_SW, V_GK_MLA, V_GK_MLA_SW, V_GQ_FOX, V_GK_FOX, V_B_FORGET, V_ADD_Q_FOX,
 V_ONES_K_FOX, V_ADD_Q_MLA) = range(10)
VEC_ROWS = 16

FF_CHUNK = 1024
VMEM_LIMIT = 56 * 1024 * 1024


def _const_spec(shape):
    nd = len(shape)
    return pl.BlockSpec(shape, lambda *_: (0,) * nd, pipeline_mode=pl.Buffered(1))


def _rms(x, g, n):
    ms = jnp.sum(x * x, axis=-1, keepdims=True) * (1.0 / n)
    return x * lax.rsqrt(ms + EPS) * g


def _split3(x):
    hi = x.astype(jnp.bfloat16)
    r1 = x - hi.astype(jnp.float32)
    mid = r1.astype(jnp.bfloat16)
    lo = (r1 - mid.astype(jnp.float32)).astype(jnp.bfloat16)
    return jnp.concatenate([hi, mid, lo], axis=1)


def _dot(a, b):
    return jnp.dot(a, b, preferred_element_type=jnp.float32)


def _attn_in_kernel(h_ref, gmix_ref, wcat_ref, gcq_ref, wuq_ref, gckv_ref, wkn_ref, wv_ref,
                    vec_ref, rope_ref, tri_ref, selq_ref, selk_ref,
                    q_ref, k_ref, v_ref, carry_ref, *, tm):
    i = pl.program_id(1)

    @pl.when(i == 0)
    def _():
        carry_ref[...] = jnp.zeros_like(carry_ref)

    x = h_ref[0]
    hn = _rms(x, gmix_ref[...], D_MODEL).astype(jnp.bfloat16)

    def seg(lo, width):
        return _dot(hn, wcat_ref[:, lo:lo + width])

    def vec(r):
        return vec_ref[r:r + 1, :]

    cos_t = rope_ref[:, 0:LANE]
    sin_sw = rope_ref[:, LANE:2 * LANE]
    gc_q, gs_q = vec(V_GQ_MLA) * cos_t, vec(V_GQ_MLA_SW) * sin_sw
    gc_k, gs_k = vec(V_GK_MLA) * cos_t, vec(V_GK_MLA_SW) * sin_sw
    gq_fox, gk_fox = vec(V_GQ_FOX), vec(V_GK_FOX)
    add_q_fox, add_q_mla = vec(V_ADD_Q_FOX), vec(V_ADD_Q_MLA)

    lane = lax.broadcasted_iota(jnp.int32, (tm, LANE), 1)
    row = lax.broadcasted_iota(jnp.int32, (tm, LANE), 0)
    valid = (i * tm + row) >= PAD
    pad_key = jnp.where(valid, 0.0, PAD_KEY)
    add_k_mla = jnp.where(lane == FLAG_MLA, pad_key, 0.0)
    add_k_fox = vec(V_ONES_K_FOX) + jnp.where(lane == FLAG_FOX, pad_key, 0.0)

    misc = seg(OFF_MISC, LANE)
    kpe = jnp.where((lane >= MISC_ROPE) & (lane < MISC_ROPE + MLA_ROPE), misc, 0.0)
    k_rot = seg(OFF_MISC_SW, LANE) * gs_k
    xl = misc + vec(V_B_FORGET)
    logf = jnp.minimum(xl, 0.0) - jnp.log1p(jnp.exp(-jnp.abs(xl)))
    logf = jnp.where(valid & (lane >= MISC_GATE) & (lane < MISC_GATE + FOX_HEADS), logf, 0.0)
    cs = _dot(tri_ref[...], _split3(logf))
    cum = (cs[:, 0:LANE] + cs[:, LANE:2 * LANE]) + cs[:, 2 * LANE:3 * LANE] + carry_ref[0:1, :]
    carry_ref[0:1, :] = cum[tm - 1:tm, :]
    cum3 = _split3(cum * LOG2E)
    gate_q = _dot(cum3, selq_ref[...])
    gate_k = _dot(cum3, selk_ref[...])

    def inv_rms(xv, n):
        return lax.rsqrt(jnp.sum(xv * xv, axis=-1, keepdims=True) * (1.0 / n) + EPS)

    cqn = _rms(seg(OFF_CQ, Q_LORA), gcq_ref[...], Q_LORA).astype(jnp.bfloat16)
    ckvn = _rms(seg(OFF_CKV, KV_LORA), gckv_ref[...], KV_LORA).astype(jnp.bfloat16)
    v_ref[0, :, 0:MLA_HEADS * MLA_V] = _dot(ckvn, wv_ref[...]).astype(jnp.bfloat16)
    for hd in range(MLA_HEADS):
        sl = slice(hd * LANE, (hd + 1) * LANE)
        sw = slice((MLA_HEADS + hd) * LANE, (MLA_HEADS + hd + 1) * LANE)
        xq = _dot(cqn, wuq_ref[:, sl])
        xq_sw = _dot(cqn, wuq_ref[:, sw])
        q_ref[0, hd] = ((xq * gc_q + xq_sw * gs_q) * inv_rms(xq, MLA_QK) + add_q_mla
                        ).astype(jnp.bfloat16)
        xk = _dot(ckvn, wkn_ref[:, sl]) + kpe
        k_ref[0, hd] = ((xk * gc_k + k_rot) * inv_rms(xk, MLA_QK) + add_k_mla
                        ).astype(jnp.bfloat16)

    v_ref[0, :, MLA_HEADS * MLA_V:] = seg(OFF_FV, FOX_HEADS * FOX_DIM).astype(jnp.bfloat16)
    for hd in range(FOX_HEADS):
        sl = slice(hd * LANE, (hd + 1) * LANE)
        xq = seg(OFF_FQ + hd * LANE, LANE)
        q_ref[0, MLA_HEADS + hd] = (xq * gq_fox * inv_rms(xq, FOX_DIM) + gate_q[:, sl] + add_q_fox
                                    ).astype(jnp.bfloat16)
        xk = seg(OFF_FK + hd * LANE, LANE)
        k_ref[0, MLA_HEADS + hd] = (xk * gk_fox * inv_rms(xk, FOX_DIM) + gate_k[:, sl] + add_k_fox
                                    ).astype(jnp.bfloat16)


def _attn_in(h, gmix, p, rope_tab, tri, selq, selk, tm):
    b, lp, d = h.shape
    nt = lp // tm
    kern = functools.partial(_attn_in_kernel, tm=tm)
    qk_shape = jax.ShapeDtypeStruct((b, HEADS, lp, LANE), jnp.bfloat16)
    qk_spec = pl.BlockSpec((1, HEADS, tm, LANE), lambda bi, i: (bi, 0, i, 0))
    return pl.pallas_call(
        kern,
        grid=(b, nt),
        in_specs=[
            pl.BlockSpec((1, tm, d), lambda bi, i: (bi, i, 0)),
            _const_spec((1, d)),
            _const_spec((d, W_CAT)),
            _const_spec((1, Q_LORA)),
            _const_spec((Q_LORA, 2 * MLA_HEADS * LANE)),
            _const_spec((1, KV_LORA)),
            _const_spec((KV_LORA, MLA_HEADS * LANE)),
            _const_spec((KV_LORA, MLA_HEADS * MLA_V)),
            _const_spec((VEC_ROWS, LANE)),
            pl.BlockSpec((tm, 2 * LANE), lambda bi, i: (i, 0)),
            _const_spec((tm, tm)),
            _const_spec((N_SPLIT * LANE, FOX_HEADS * LANE)),
            _const_spec((N_SPLIT * LANE, FOX_HEADS * LANE)),
        ],
        out_specs=[qk_spec, qk_spec,
                   pl.BlockSpec((1, tm, HEADS * MLA_V), lambda bi, i: (bi, i, 0))],
        out_shape=[qk_shape, qk_shape,
                   jax.ShapeDtypeStruct((b, lp, HEADS * MLA_V), jnp.bfloat16)],
        scratch_shapes=[pltpu.VMEM((8, LANE), jnp.float32)],
        compiler_params=pltpu.CompilerParams(
            dimension_semantics=("arbitrary", "arbitrary"), vmem_limit_bytes=VMEM_LIMIT),
        name="attn_in",
    )(h, gmix, p["wcat"], p["gcq"], p["wuq"], p["gckv"], p["wkn"], p["wv"], p["vec"],
      rope_tab, tri, selq, selk)


def _flash_kernel(q_ref, k_ref, v_ref, o_ref, m_ref, l_ref, acc_ref, al_ref, p_ref, *, tq, tk):
    qi = pl.program_id(2)
    m_ref[...] = jnp.full(m_ref.shape, NEG, jnp.float32)
    l_ref[...] = jnp.zeros(l_ref.shape, jnp.float32)
    acc_ref[...] = jnp.zeros(acc_ref.shape, jnp.float32)
    al_ref[1] = jnp.zeros((tq, LANE), jnp.float32)
    p_ref[1] = jnp.zeros((tq, tk), jnp.bfloat16)
    reps = tk // LANE

    def softmax(j, start, masked):
        s = lax.dot_general(q_ref[0, j], k_ref[0, j, pl.ds(start, tk), :],
                            (((1,), (1,)), ((), ())), preferred_element_type=jnp.float32)
        if masked:
            row = lax.broadcasted_iota(jnp.int32, (tq, tk), 0)
            col = lax.broadcasted_iota(jnp.int32, (tq, tk), 1)
            s = jnp.where(col <= row, s, NEG)
        m_prev = m_ref[j]
        m_next = jnp.maximum(m_prev, jnp.max(s, axis=1, keepdims=True))
        p = jnp.exp2(s - jnp.concatenate([m_next] * reps, axis=1))
        alpha = jnp.exp2(m_prev - m_next)
        l_ref[j] = alpha * l_ref[j] + jnp.sum(p, axis=1, keepdims=True)
        al_ref[j] = alpha
        p_ref[j] = p.astype(jnp.bfloat16)
        m_ref[j] = m_next

    def pv(j, start):
        acc_ref[j] = acc_ref[j] * al_ref[j] + _dot(p_ref[j], v_ref[0, pl.ds(start, tk), :])

    def step(ki, masked):
        start = pl.multiple_of(ki * tk, tk)
        prev = pl.multiple_of(jnp.maximum(ki - 1, 0) * tk, tk)
        softmax(0, start, masked)
        pv(1, prev)
        softmax(1, start, masked)
        pv(0, start)

    def body(ki, c):
        step(ki, False)
        return c

    lax.fori_loop(0, qi, body, 0)
    step(qi, True)
    pv(1, pl.multiple_of(qi * tk, tk))

    lane = lax.broadcasted_iota(jnp.int32, (tq, LANE), 1)
    o = jnp.where(lane < MLA_V, acc_ref[0] / l_ref[0], acc_ref[1] / l_ref[1])
    o_ref[0] = o.astype(jnp.bfloat16)


def _flash(q, k, v, tq):
    b, _, lp, _ = q.shape
    nq = lp // tq
    kern = functools.partial(_flash_kernel, tq=tq, tk=tq)
    return pl.pallas_call(
        kern,
        grid=(b, HEADS // 2, nq),
        in_specs=[
            pl.BlockSpec((1, 2, tq, LANE), lambda bi, hp, i: (bi, hp, i, 0)),
            pl.BlockSpec((1, 2, lp, LANE), lambda bi, hp, i: (bi, hp, 0, 0)),
            pl.BlockSpec((1, lp, LANE), lambda bi, hp, i: (bi, 0, hp)),
        ],
        out_specs=pl.BlockSpec((1, tq, LANE), lambda bi, hp, i: (bi, i, hp)),
        out_shape=jax.ShapeDtypeStruct((b, lp, HEADS * MLA_V), jnp.bfloat16),
        scratch_shapes=[pltpu.VMEM((2, tq, LANE), jnp.float32)] * 4
        + [pltpu.VMEM((2, tq, tq), jnp.bfloat16)],
        compiler_params=pltpu.CompilerParams(
            dimension_semantics=("arbitrary", "arbitrary", "arbitrary"),
            vmem_limit_bytes=VMEM_LIMIT),
        name="flash",
    )(q, k, v)


def _conv_in_kernel(h_ref, gmix_ref, win_ref, cw_ref, y_ref, gs_ref, *, tm):
    i = pl.program_id(1)

    @pl.when(i == 0)
    def _():
        gs_ref[0:8, :] = jnp.zeros((8, D_MODEL), jnp.float32)

    x = h_ref[0]
    hn = _rms(x, gmix_ref[...], D_MODEL).astype(jnp.bfloat16)
    gate_c = _dot(hn, win_ref[:, D_MODEL:2 * D_MODEL])
    u = _dot(hn, win_ref[:, 2 * D_MODEL:3 * D_MODEL])
    row = lax.broadcasted_iota(jnp.int32, (tm, D_MODEL), 0)
    g = jnp.where((i * tm + row) >= PAD, gate_c * u, 0.0)
    gs_ref[8:tm + 8, :] = g
    y = (cw_ref[0:1, :] * gs_ref[6:tm + 6, :] + cw_ref[1:2, :] * gs_ref[7:tm + 7, :]
         + cw_ref[2:3, :] * g)
    gs_ref[0:8, :] = gs_ref[tm:tm + 8, :]
    gate_b = _dot(hn, win_ref[:, 0:D_MODEL])
    y_ref[0] = (gate_b * y).astype(jnp.bfloat16)


def _conv_in(h, gmix, win, cw, tm):
    b, lp, d = h.shape
    kern = functools.partial(_conv_in_kernel, tm=tm)
    return pl.pallas_call(
        kern,
        grid=(b, lp // tm),
        in_specs=[
            pl.BlockSpec((1, tm, d), lambda bi, i: (bi, i, 0)),
            _const_spec((1, d)),
            _const_spec((d, 3 * d)),
            _const_spec((8, d)),
        ],
        out_specs=pl.BlockSpec((1, tm, d), lambda bi, i: (bi, i, 0)),
        out_shape=jax.ShapeDtypeStruct((b, lp, d), jnp.bfloat16),
        scratch_shapes=[pltpu.VMEM((tm + 8, d), jnp.float32)],
        compiler_params=pltpu.CompilerParams(
            dimension_semantics=("arbitrary", "arbitrary"), vmem_limit_bytes=VMEM_LIMIT),
        name="conv_in",
    )(h, gmix, win, cw)


def _mix_out_mlp_kernel(h_ref, y_ref, wo_ref, gmlp_ref, wup_ref, wdn_ref, out_ref):
    h1 = h_ref[...] + _dot(y_ref[...], wo_ref[...])
    n = _rms(h1, gmlp_ref[...], D_MODEL).astype(jnp.bfloat16)
    acc = h1
    for c in range(D_FF // FF_CHUNK):
        sl = slice(c * FF_CHUNK, (c + 1) * FF_CHUNK)
        a = jnp.maximum(_dot(n, wup_ref[:, sl]), 0.0)
        acc = acc + _dot((a * a).astype(jnp.bfloat16), wdn_ref[sl, :])
    out_ref[...] = acc


def _mix_out_mlp(h, y, wo, gmlp, wup, wdn, tm):
    r, d = h.shape
    return pl.pallas_call(
        _mix_out_mlp_kernel,
        grid=(r // tm,),
        in_specs=[
            pl.BlockSpec((tm, d), lambda i: (i, 0)),
            pl.BlockSpec((tm, d), lambda i: (i, 0)),
            _const_spec((d, d)),
            _const_spec((1, d)),
            _const_spec((d, D_FF)),
            _const_spec((D_FF, d)),
        ],
        out_specs=pl.BlockSpec((tm, d), lambda i: (i, 0)),
        out_shape=jax.ShapeDtypeStruct((r, d), jnp.float32),
        compiler_params=pltpu.CompilerParams(
            dimension_semantics=("arbitrary",), vmem_limit_bytes=VMEM_LIMIT),
        name="mix_out_mlp",
    )(h, y, wo, gmlp, wup, wdn)


def _pad_heads(w, heads, dim):
    k = w.shape[0]
    w = w.reshape(k, heads, dim)
    w = jnp.pad(w, ((0, 0), (0, 0), (0, LANE - dim)))
    return w.reshape(k, heads * LANE)


def _lane_vec(v, offset=0):
    return jnp.zeros((LANE,), jnp.float32).at[offset:offset + v.shape[0]].set(v)


def _attn_params(w_in, g_cq, w_uq, g_ckv, w_ukv, g_q_mla, g_k_mla, g_q_fox, g_k_fox, b_forget):
    bf = jnp.bfloat16
    o1 = Q_LORA
    o2 = o1 + KV_LORA
    o3 = o2 + MLA_ROPE
    o4 = o3 + FOX_HEADS * FOX_DIM
    o5 = o4 + FOX_HEADS * FOX_DIM
    o6 = o5 + FOX_HEADS * FOX_DIM
    misc = jnp.zeros((D_MODEL, LANE), jnp.float32)
    misc = misc.at[:, MISC_GATE:MISC_GATE + FOX_HEADS].set(w_in[:, o6:])
    misc = misc.at[:, MISC_ROPE:MISC_ROPE + MLA_ROPE].set(w_in[:, o2:o3])
    misc_sw = jnp.zeros((D_MODEL, LANE), jnp.float32)
    misc_sw = misc_sw.at[:, MISC_ROPE:MISC_ROPE + HALF_ROPE].set(w_in[:, o2 + HALF_ROPE:o3])
    misc_sw = misc_sw.at[:, MISC_ROPE + HALF_ROPE:MISC_ROPE + MLA_ROPE].set(
        w_in[:, o2:o2 + HALF_ROPE])
    wcat = jnp.concatenate([
        w_in[:, :o1], w_in[:, o1:o2],
        _pad_heads(w_in[:, o3:o4], FOX_HEADS, FOX_DIM),
        _pad_heads(w_in[:, o4:o5], FOX_HEADS, FOX_DIM),
        w_in[:, o5:o6], misc, misc_sw], axis=1).astype(bf)
    kv = w_ukv.reshape(KV_LORA, MLA_HEADS, MLA_NOPE + MLA_V)
    wkn = _pad_heads(kv[:, :, :MLA_NOPE].reshape(KV_LORA, -1), MLA_HEADS, MLA_NOPE).astype(bf)
    wv = kv[:, :, MLA_NOPE:].reshape(KV_LORA, MLA_HEADS * MLA_V).astype(bf)
    lo, mid, hi = MLA_NOPE, MLA_NOPE + HALF_ROPE, MLA_NOPE + MLA_ROPE
    uq = w_uq.reshape(Q_LORA, MLA_HEADS, MLA_QK)
    uq_sw = jnp.zeros((Q_LORA, MLA_HEADS, LANE), jnp.float32)
    uq_sw = uq_sw.at[:, :, lo:mid].set(uq[:, :, mid:hi]).at[:, :, mid:hi].set(uq[:, :, lo:mid])
    wuq = jnp.concatenate([_pad_heads(w_uq, MLA_HEADS, MLA_QK),
                           uq_sw.reshape(Q_LORA, MLA_HEADS * LANE)], axis=1).astype(bf)

    def swapped(g):
        return jnp.zeros((LANE,), jnp.float32).at[lo:mid].set(g[mid:hi]).at[mid:hi].set(g[lo:mid])

    zero = jnp.zeros((LANE,), jnp.float32)
    q_scale_mla = MLA_QK ** -0.5 * LOG2E
    rows = [zero] * VEC_ROWS
    rows[V_GQ_MLA] = _lane_vec(g_q_mla) * q_scale_mla
    rows[V_GQ_MLA_SW] = swapped(g_q_mla) * q_scale_mla
    rows[V_GK_MLA] = _lane_vec(g_k_mla)
    rows[V_GK_MLA_SW] = swapped(g_k_mla)
    rows[V_GQ_FOX] = _lane_vec(g_q_fox) * (FOX_DIM ** -0.5 * LOG2E)
    rows[V_GK_FOX] = _lane_vec(g_k_fox)
    rows[V_B_FORGET] = _lane_vec(b_forget, MISC_GATE)
    rows[V_ADD_Q_FOX] = (zero.at[GATE_LANE + N_SPLIT:GATE_LANE + 2 * N_SPLIT].set(1.0)
                         .at[FLAG_FOX].set(1.0))
    rows[V_ONES_K_FOX] = zero.at[GATE_LANE:GATE_LANE + N_SPLIT].set(1.0)
    rows[V_ADD_Q_MLA] = zero.at[FLAG_MLA].set(1.0)
    vec = jnp.stack(rows)
    return dict(wcat=wcat, gcq=g_cq[None], wuq=wuq, gckv=g_ckv[None], wkn=wkn, wv=wv, vec=vec)


def _gate_selectors():
    selq = np.zeros((N_SPLIT * LANE, FOX_HEADS * LANE), np.float32)
    selk = np.zeros((N_SPLIT * LANE, FOX_HEADS * LANE), np.float32)
    for part in range(N_SPLIT):
        for hd in range(FOX_HEADS):
            selq[part * LANE + MISC_GATE + hd, hd * LANE + GATE_LANE + part] = 1.0
            selk[part * LANE + MISC_GATE + hd, hd * LANE + GATE_LANE + N_SPLIT + part] = -1.0
    return jnp.asarray(selq, jnp.bfloat16), jnp.asarray(selk, jnp.bfloat16)


def _rope_table(lp):
    pos = (jnp.arange(lp, dtype=jnp.int32) - PAD).astype(jnp.float32)
    inv_freq = ROPE_BASE ** (-jnp.arange(0, MLA_ROPE, 2, dtype=jnp.float32) / MLA_ROPE)
    ang = pos[:, None] * inv_freq[None, :]
    cos, sin = jnp.cos(ang), jnp.sin(ang)
    z = jnp.zeros((lp, LANE), jnp.float32)
    lo, mid, hi = MLA_NOPE, MLA_NOPE + HALF_ROPE, MLA_NOPE + MLA_ROPE
    cos_t = z.at[:, :lo].set(1.0).at[:, lo:mid].set(cos).at[:, mid:hi].set(cos)
    sin_sw = z.at[:, lo:mid].set(-sin).at[:, mid:hi].set(sin)
    return jnp.concatenate([cos_t, sin_sw], axis=1)


def _token_tile(lp):
    for t in (768, 512, 256):
        if lp % t == 0:
            return t
    raise ValueError(f"padded length {lp} is not a multiple of 256")


def kernel(x, meta_tokens, g_mix, g_mlp, w_in_attn, g_cq, w_uq, g_ckv, w_ukv, g_q_mla, g_k_mla,
           g_q_fox, g_k_fox, b_forget, w_out_attn, w_in_conv, conv_w, w_out_conv, w_mlp_up,
           w_mlp_down):
    b, seq, d = x.shape
    assert d == D_MODEL and (PAD + N_META + seq) % BLOCK == 0
    lp = PAD + N_META + seq
    tm = _token_tile(lp)
    bf = jnp.bfloat16

    meta = jnp.broadcast_to(meta_tokens.astype(x.dtype)[None], (b, N_META, d))
    h = jnp.concatenate([jnp.zeros((b, PAD, d), x.dtype), meta, x], axis=1)

    rope_tab = _rope_table(lp)
    tri = (jnp.arange(tm)[:, None] >= jnp.arange(tm)[None, :]).astype(bf)
    selq, selk = _gate_selectors()

    for layer in range(DEPTH):
        j = layer // 2
        gmix = g_mix[layer][None]
        if layer % 2 == 0:
            p = _attn_params(w_in_attn[j], g_cq[j], w_uq[j], g_ckv[j], w_ukv[j], g_q_mla[j],
                             g_k_mla[j], g_q_fox[j], g_k_fox[j], b_forget[j])
            q, k, v = _attn_in(h, gmix, p, rope_tab, tri, selq, selk, tm)
            y = _flash(q, k, v, tm)
            wo = w_out_attn[j].astype(bf)
        else:
            cw = jnp.zeros((8, d), jnp.float32).at[0:3].set(conv_w[j])
            y = _conv_in(h, gmix, w_in_conv[j].astype(bf), cw, tm)
            wo = w_out_conv[j].astype(bf)
        h = _mix_out_mlp(h.reshape(b * lp, d), y.reshape(b * lp, d), wo, g_mlp[layer][None],
                         w_mlp_up[layer].astype(bf), w_mlp_down[layer].astype(bf), tm
                         ).reshape(b, lp, d)
    return h[:, PAD + N_META:]
```

```python
import functools

import numpy as np
import jax
import jax.numpy as jnp
from jax import lax
from jax.experimental import pallas as pl
from jax.experimental.pallas import tpu as pltpu

D_MODEL = 1024
DEPTH = 4
N_META = 16
BLOCK = 128
PAD = 2 * BLOCK - N_META
MLA_HEADS = 8
MLA_NOPE = 64
MLA_ROPE = 32
MLA_QK = MLA_NOPE + MLA_ROPE
MLA_V = 64
Q_LORA = 384
KV_LORA = 256
ROPE_BASE = 10000.0
FOX_HEADS = 8
FOX_DIM = 64
D_FF = 4 * D_MODEL
EPS = 1e-6
NEG = -1e30

LANE = 128
HEADS = MLA_HEADS + FOX_HEADS
HALF_ROPE = MLA_ROPE // 2
GATE_LANE = FOX_DIM
N_SPLIT = 3
FLAG_MLA = MLA_QK
FLAG_FOX = GATE_LANE + 2 * N_SPLIT
PAD_KEY = -30000.0
LOG2E = 1.4426950408889634
MISC_GATE = 0
MISC_ROPE = MLA_NOPE

OFF_CQ = 0
OFF_CKV = OFF_CQ + Q_LORA
OFF_FQ = OFF_CKV + KV_LORA
OFF_FK = OFF_FQ + FOX_HEADS * LANE
OFF_FV = OFF_FK + FOX_HEADS * LANE
OFF_MISC = OFF_FV + FOX_HEADS * LANE
OFF_MISC_SW = OFF_MISC + LANE
W_CAT = OFF_MISC_SW + LANE

(V_GQ_MLA, V_GQ_MLA_SW, V_GK_MLA, V_GK_MLA_SW, V_GQ_FOX, V_GK_FOX, V_B_FORGET, V_ADD_Q_FOX,
 V_ONES_K_FOX, V_ADD_Q_MLA, V_ONES_V) = range(11)
VEC_ROWS = 16
ONES_LANE = MLA_V
PAIR = 2 * LANE

FF_CHUNK = 1024
FLASH_TQ = 768
FLASH_TK = 384
VMEM_LIMIT = 56 * 1024 * 1024


def _const_spec(shape):
    nd = len(shape)
    return pl.BlockSpec(shape, lambda *_: (0,) * nd, pipeline_mode=pl.Buffered(1))


def _rms(x, g, n):
    ms = jnp.sum(x * x, axis=-1, keepdims=True) * (1.0 / n)
    return x * lax.rsqrt(ms + EPS) * g


def _split3(x):
    hi = x.astype(jnp.bfloat16)
    r1 = x - hi.astype(jnp.float32)
    mid = r1.astype(jnp.bfloat16)
    lo = (r1 - mid.astype(jnp.float32)).astype(jnp.bfloat16)
    return jnp.concatenate([hi, mid, lo], axis=1)


def _dot(a, b):
    return jnp.dot(a, b, preferred_element_type=jnp.float32)


def _attn_in_kernel(h_ref, gmix_ref, wcat_ref, gcq_ref, wuq_ref, gckv_ref, wkn_ref, wv_ref,
                    vec_ref, rope_ref, tri_ref, selq_ref, selk_ref,
                    q_ref, k_ref, v_ref, carry_ref, *, tm):
    i = pl.program_id(1)

    @pl.when(i == 0)
    def _():
        carry_ref[...] = jnp.zeros_like(carry_ref)

    x = h_ref[0]
    hn = _rms(x, gmix_ref[...], D_MODEL).astype(jnp.bfloat16)

    def seg(lo, width):
        return _dot(hn, wcat_ref[:, lo:lo + width])

    def vec(r):
        return vec_ref[r:r + 1, :]

    cos_t = rope_ref[:, 0:LANE]
    sin_sw = rope_ref[:, LANE:2 * LANE]
    gc_q, gs_q = vec(V_GQ_MLA) * cos_t, vec(V_GQ_MLA_SW) * sin_sw
    gc_k, gs_k = vec(V_GK_MLA) * cos_t, vec(V_GK_MLA_SW) * sin_sw
    gq_fox, gk_fox = vec(V_GQ_FOX), vec(V_GK_FOX)
    add_q_fox, add_q_mla = vec(V_ADD_Q_FOX), vec(V_ADD_Q_MLA)

    lane = lax.broadcasted_iota(jnp.int32, (tm, LANE), 1)
    row = lax.broadcasted_iota(jnp.int32, (tm, LANE), 0)
    valid = (i * tm + row) >= PAD
    pad_key = jnp.where(valid, 0.0, PAD_KEY)
    add_k_mla = jnp.where(lane == FLAG_MLA, pad_key, 0.0)
    add_k_fox = vec(V_ONES_K_FOX) + jnp.where(lane == FLAG_FOX, pad_key, 0.0)

    misc = seg(OFF_MISC, LANE)
    kpe = jnp.where((lane >= MISC_ROPE) & (lane < MISC_ROPE + MLA_ROPE), misc, 0.0)
    k_rot = seg(OFF_MISC_SW, LANE) * gs_k
    xl = misc + vec(V_B_FORGET)
    logf = jnp.minimum(xl, 0.0) - jnp.log1p(jnp.exp(-jnp.abs(xl)))
    logf = jnp.where(valid & (lane >= MISC_GATE) & (lane < MISC_GATE + FOX_HEADS), logf, 0.0)
    cs = _dot(tri_ref[...], _split3(logf))
    cum = (cs[:, 0:LANE] + cs[:, LANE:2 * LANE]) + cs[:, 2 * LANE:3 * LANE] + carry_ref[0:1, :]
    carry_ref[0:1, :] = cum[tm - 1:tm, :]
    cum3 = _split3(cum * LOG2E)
    gate_q = _dot(cum3, selq_ref[...])
    gate_k = _dot(cum3, selk_ref[...])

    def inv_rms(xv, n):
        return lax.rsqrt(jnp.sum(xv * xv, axis=-1, keepdims=True) * (1.0 / n) + EPS)

    cqn = _rms(seg(OFF_CQ, Q_LORA), gcq_ref[...], Q_LORA).astype(jnp.bfloat16)
    ckvn = _rms(seg(OFF_CKV, KV_LORA), gckv_ref[...], KV_LORA).astype(jnp.bfloat16)
    ones_v = vec(V_ONES_V)
    for g in range(MLA_HEADS // 2):
        cols = slice(g * PAIR, (g + 1) * PAIR)
        cols_sw = slice(MLA_HEADS * LANE + g * PAIR, MLA_HEADS * LANE + (g + 1) * PAIR)
        xq2 = _dot(cqn, wuq_ref[:, cols])
        xq2_sw = _dot(cqn, wuq_ref[:, cols_sw])
        xk2 = _dot(ckvn, wkn_ref[:, cols])
        xv2 = _dot(ckvn, wv_ref[:, cols])
        for e in range(2):
            hd, sl = 2 * g + e, slice(e * LANE, (e + 1) * LANE)
            xq = xq2[:, sl]
            q_ref[0, hd] = ((xq * gc_q + xq2_sw[:, sl] * gs_q) * inv_rms(xq, MLA_QK) + add_q_mla
                            ).astype(jnp.bfloat16)
            xk = xk2[:, sl] + kpe
            k_ref[0, hd] = ((xk * gc_k + k_rot) * inv_rms(xk, MLA_QK) + add_k_mla
                            ).astype(jnp.bfloat16)
            v_ref[0, hd] = (xv2[:, sl] + ones_v).astype(jnp.bfloat16)

    for g in range(FOX_HEADS // 2):
        xq2 = seg(OFF_FQ + g * PAIR, PAIR)
        xk2 = seg(OFF_FK + g * PAIR, PAIR)
        xv2 = seg(OFF_FV + g * PAIR, PAIR)
        for e in range(2):
            hd, sl = 2 * g + e, slice(e * LANE, (e + 1) * LANE)
            gl = slice(hd * LANE, (hd + 1) * LANE)
            xq = xq2[:, sl]
            q_ref[0, MLA_HEADS + hd] = (xq * gq_fox * inv_rms(xq, FOX_DIM) + gate_q[:, gl]
                                        + add_q_fox).astype(jnp.bfloat16)
            xk = xk2[:, sl]
            k_ref[0, MLA_HEADS + hd] = (xk * gk_fox * inv_rms(xk, FOX_DIM) + gate_k[:, gl]
                                        + add_k_fox).astype(jnp.bfloat16)
            v_ref[0, MLA_HEADS + hd] = (xv2[:, sl] + ones_v).astype(jnp.bfloat16)


def _attn_in(h, gmix, p, rope_tab, tri, selq, selk, tm):
    b, lp, d = h.shape
    nt = lp // tm
    kern = functools.partial(_attn_in_kernel, tm=tm)
    qk_shape = jax.ShapeDtypeStruct((b, HEADS, lp, LANE), jnp.bfloat16)
    qk_spec = pl.BlockSpec((1, HEADS, tm, LANE), lambda bi, i: (bi, 0, i, 0))
    return pl.pallas_call(
        kern,
        grid=(b, nt),
        in_specs=[
            pl.BlockSpec((1, tm, d), lambda bi, i: (bi, i, 0)),
            _const_spec((1, d)),
            _const_spec((d, W_CAT)),
            _const_spec((1, Q_LORA)),
            _const_spec((Q_LORA, 2 * MLA_HEADS * LANE)),
            _const_spec((1, KV_LORA)),
            _const_spec((KV_LORA, MLA_HEADS * LANE)),
            _const_spec((KV_LORA, MLA_HEADS * LANE)),
            _const_spec((VEC_ROWS, LANE)),
            pl.BlockSpec((tm, 2 * LANE), lambda bi, i: (i, 0)),
            _const_spec((tm, tm)),
            _const_spec((N_SPLIT * LANE, FOX_HEADS * LANE)),
            _const_spec((N_SPLIT * LANE, FOX_HEADS * LANE)),
        ],
        out_specs=[qk_spec, qk_spec, qk_spec],
        out_shape=[qk_shape, qk_shape, qk_shape],
        scratch_shapes=[pltpu.VMEM((8, LANE), jnp.float32)],
        compiler_params=pltpu.CompilerParams(
            dimension_semantics=("arbitrary", "arbitrary"), vmem_limit_bytes=VMEM_LIMIT),
        name="attn_in",
    )(h, gmix, p["wcat"], p["gcq"], p["wuq"], p["gckv"], p["wkn"], p["wv"], p["vec"],
      rope_tab, tri, selq, selk)


def _flash_kernel(q_ref, k_ref, v_ref, o_ref, m_ref, acc_ref, al_ref, p_ref, *, tq, tk):
    chunks = tq // tk
    qi = pl.program_id(2)
    m_ref[...] = jnp.full(m_ref.shape, NEG, jnp.float32)
    acc_ref[...] = jnp.zeros(acc_ref.shape, jnp.float32)
    al_ref[1, chunks - 1] = jnp.zeros((tq, LANE), jnp.float32)
    p_ref[1, chunks - 1] = jnp.zeros((tq, tk), jnp.bfloat16)

    def softmax(j, u, base, diagonal):
        r0 = u * tk if diagonal else 0
        rows = slice(r0, tq)
        start = pl.multiple_of(base + u * tk, tk)
        s = lax.dot_general(q_ref[0, j, rows, :], k_ref[0, j, pl.ds(start, tk), :],
                            (((1,), (1,)), ((), ())), preferred_element_type=jnp.float32)
        if diagonal:
            row = lax.broadcasted_iota(jnp.int32, (tq - r0, tk), 0)
            col = lax.broadcasted_iota(jnp.int32, (tq - r0, tk), 1)
            s = jnp.where(col <= row, s, NEG)
        m_prev = m_ref[j, rows, :]
        m_next = jnp.maximum(m_prev, jnp.max(s, axis=1, keepdims=True))
        p = jnp.exp2(s - jnp.concatenate([m_next] * (tk // LANE), axis=1))
        al_ref[j, u, rows, :] = jnp.exp2(m_prev - m_next)
        p_ref[j, u, rows, :] = p.astype(jnp.bfloat16)
        m_ref[j, rows, :] = m_next

    def pv(j, u, base, diagonal):
        r0 = u * tk if diagonal else 0
        rows = slice(r0, tq)
        start = pl.multiple_of(base + u * tk, tk)
        acc_ref[j, rows, :] = (acc_ref[j, rows, :] * al_ref[j, u, rows, :]
                               + _dot(p_ref[j, u, rows, :], v_ref[0, j, pl.ds(start, tk), :]))

    def block(kb, diagonal):
        base = kb * tq
        for u in range(chunks):
            softmax(0, u, base, diagonal)
            if u == 0:
                pv(1, chunks - 1, jnp.maximum(kb - 1, 0) * tq, False)
            else:
                pv(1, u - 1, base, diagonal)
            softmax(1, u, base, diagonal)
            pv(0, u, base, diagonal)

    def body(kb, carry):
        block(kb, False)
        return carry

    lax.fori_loop(0, qi, body, 0)
    block(qi, True)
    pv(1, chunks - 1, qi * tq, True)

    o0 = acc_ref[0]
    o1 = acc_ref[1]
    o0 = o0 / o0[:, ONES_LANE:ONES_LANE + 1]
    o1 = o1 / o1[:, ONES_LANE:ONES_LANE + 1]
    lane = lax.broadcasted_iota(jnp.int32, (tq, LANE), 1)
    o_ref[0] = jnp.where(lane < MLA_V, o0, pltpu.roll(o1, MLA_V, 1)).astype(jnp.bfloat16)


def _flash(q, k, v):
    b, _, lp, _ = q.shape
    tq, tk = FLASH_TQ, FLASH_TK
    kern = functools.partial(_flash_kernel, tq=tq, tk=tk)
    kv_spec = pl.BlockSpec((1, 2, lp, LANE), lambda bi, hp, i: (bi, hp, 0, 0))
    return pl.pallas_call(
        kern,
        grid=(b, HEADS // 2, lp // tq),
        in_specs=[pl.BlockSpec((1, 2, tq, LANE), lambda bi, hp, i: (bi, hp, i, 0)),
                  kv_spec, kv_spec],
        out_specs=pl.BlockSpec((1, tq, LANE), lambda bi, hp, i: (bi, i, hp)),
        out_shape=jax.ShapeDtypeStruct((b, lp, HEADS * MLA_V), jnp.bfloat16),
        scratch_shapes=[pltpu.VMEM((2, tq, LANE), jnp.float32)] * 2
        + [pltpu.VMEM((2, tq // tk, tq, LANE), jnp.float32),
           pltpu.VMEM((2, tq // tk, tq, tk), jnp.bfloat16)],
        compiler_params=pltpu.CompilerParams(
            dimension_semantics=("arbitrary", "arbitrary", "arbitrary"),
            vmem_limit_bytes=VMEM_LIMIT),
        name="flash",
    )(q, k, v)


def _conv_in_kernel(h_ref, gmix_ref, win_ref, cw_ref, y_ref, gs_ref, *, tm):
    i = pl.program_id(1)

    @pl.when(i == 0)
    def _():
        gs_ref[0:8, :] = jnp.zeros((8, D_MODEL), jnp.float32)

    x = h_ref[0]
    hn = _rms(x, gmix_ref[...], D_MODEL).astype(jnp.bfloat16)
    gate_c = _dot(hn, win_ref[:, D_MODEL:2 * D_MODEL])
    u = _dot(hn, win_ref[:, 2 * D_MODEL:3 * D_MODEL])
    row = lax.broadcasted_iota(jnp.int32, (tm, D_MODEL), 0)
    g = jnp.where((i * tm + row) >= PAD, gate_c * u, 0.0)
    gs_ref[8:tm + 8, :] = g
    y = (cw_ref[0:1, :] * gs_ref[6:tm + 6, :] + cw_ref[1:2, :] * gs_ref[7:tm + 7, :]
         + cw_ref[2:3, :] * g)
    gs_ref[0:8, :] = gs_ref[tm:tm + 8, :]
    gate_b = _dot(hn, win_ref[:, 0:D_MODEL])
    y_ref[0] = (gate_b * y).astype(jnp.bfloat16)


def _conv_in(h, gmix, win, cw, tm):
    b, lp, d = h.shape
    kern = functools.partial(_conv_in_kernel, tm=tm)
    return pl.pallas_call(
        kern,
        grid=(b, lp // tm),
        in_specs=[
            pl.BlockSpec((1, tm, d), lambda bi, i: (bi, i, 0)),
            _const_spec((1, d)),
            _const_spec((d, 3 * d)),
            _const_spec((8, d)),
        ],
        out_specs=pl.BlockSpec((1, tm, d), lambda bi, i: (bi, i, 0)),
        out_shape=jax.ShapeDtypeStruct((b, lp, d), jnp.bfloat16),
        scratch_shapes=[pltpu.VMEM((tm + 8, d), jnp.float32)],
        compiler_params=pltpu.CompilerParams(
            dimension_semantics=("arbitrary", "arbitrary"), vmem_limit_bytes=VMEM_LIMIT),
        name="conv_in",
    )(h, gmix, win, cw)


def _mix_out_mlp_kernel(h_ref, y_ref, wo_ref, gmlp_ref, wup_ref, wdn_ref, out_ref):
    h1 = h_ref[...] + _dot(y_ref[...], wo_ref[...])
    n = _rms(h1, gmlp_ref[...], D_MODEL).astype(jnp.bfloat16)
    acc = h1
    for c in range(D_FF // FF_CHUNK):
        sl = slice(c * FF_CHUNK, (c + 1) * FF_CHUNK)
        a = jnp.maximum(_dot(n, wup_ref[:, sl]), 0.0)
        acc = acc + _dot((a * a).astype(jnp.bfloat16), wdn_ref[sl, :])
    out_ref[...] = acc


def _mix_out_mlp(h, y, wo, gmlp, wup, wdn, tm):
    r, d = h.shape
    return pl.pallas_call(
        _mix_out_mlp_kernel,
        grid=(r // tm,),
        in_specs=[
            pl.BlockSpec((tm, d), lambda i: (i, 0)),
            pl.BlockSpec((tm, d), lambda i: (i, 0)),
            _const_spec((d, d)),
            _const_spec((1, d)),
            _const_spec((d, D_FF)),
            _const_spec((D_FF, d)),
        ],
        out_specs=pl.BlockSpec((tm, d), lambda i: (i, 0)),
        out_shape=jax.ShapeDtypeStruct((r, d), jnp.float32),
        compiler_params=pltpu.CompilerParams(
            dimension_semantics=("arbitrary",), vmem_limit_bytes=VMEM_LIMIT),
        name="mix_out_mlp",
    )(h, y, wo, gmlp, wup, wdn)


def _pad_heads(w, heads, dim):
    k = w.shape[0]
    w = w.reshape(k, heads, dim)
    w = jnp.pad(w, ((0, 0), (0, 0), (0, LANE - dim)))
    return w.reshape(k, heads * LANE)


def _lane_vec(v, offset=0):
    return jnp.zeros((LANE,), jnp.float32).at[offset:offset + v.shape[0]].set(v)


def _attn_params(w_in, g_cq, w_uq, g_ckv, w_ukv, g_q_mla, g_k_mla, g_q_fox, g_k_fox, b_forget):
    bf = jnp.bfloat16
    o1 = Q_LORA
    o2 = o1 + KV_LORA
    o3 = o2 + MLA_ROPE
    o4 = o3 + FOX_HEADS * FOX_DIM
    o5 = o4 + FOX_HEADS * FOX_DIM
    o6 = o5 + FOX_HEADS * FOX_DIM
    misc = jnp.zeros((D_MODEL, LANE), jnp.float32)
    misc = misc.at[:, MISC_GATE:MISC_GATE + FOX_HEADS].set(w_in[:, o6:])
    misc = misc.at[:, MISC_ROPE:MISC_ROPE + MLA_ROPE].set(w_in[:, o2:o3])
    misc_sw = jnp.zeros((D_MODEL, LANE), jnp.float32)
    misc_sw = misc_sw.at[:, MISC_ROPE:MISC_ROPE + HALF_ROPE].set(w_in[:, o2 + HALF_ROPE:o3])
    misc_sw = misc_sw.at[:, MISC_ROPE + HALF_ROPE:MISC_ROPE + MLA_ROPE].set(
        w_in[:, o2:o2 + HALF_ROPE])
    wcat = jnp.concatenate([
        w_in[:, :o1], w_in[:, o1:o2],
        _pad_heads(w_in[:, o3:o4], FOX_HEADS, FOX_DIM),
        _pad_heads(w_in[:, o4:o5], FOX_HEADS, FOX_DIM),
        _pad_heads(w_in[:, o5:o6], FOX_HEADS, FOX_DIM), misc, misc_sw], axis=1).astype(bf)
    kv = w_ukv.reshape(KV_LORA, MLA_HEADS, MLA_NOPE + MLA_V)
    wkn = _pad_heads(kv[:, :, :MLA_NOPE].reshape(KV_LORA, -1), MLA_HEADS, MLA_NOPE).astype(bf)
    wv = _pad_heads(kv[:, :, MLA_NOPE:].reshape(KV_LORA, -1), MLA_HEADS, MLA_V).astype(bf)
    lo, mid, hi = MLA_NOPE, MLA_NOPE + HALF_ROPE, MLA_NOPE + MLA_ROPE
    uq = w_uq.reshape(Q_LORA, MLA_HEADS, MLA_QK)
    uq_sw = jnp.zeros((Q_LORA, MLA_HEADS, LANE), jnp.float32)
    uq_sw = uq_sw.at[:, :, lo:mid].set(uq[:, :, mid:hi]).at[:, :, mid:hi].set(uq[:, :, lo:mid])
    wuq = jnp.concatenate([_pad_heads(w_uq, MLA_HEADS, MLA_QK),
                           uq_sw.reshape(Q_LORA, MLA_HEADS * LANE)], axis=1).astype(bf)

    def swapped(g):
        return jnp.zeros((LANE,), jnp.float32).at[lo:mid].set(g[mid:hi]).at[mid:hi].set(g[lo:mid])

    zero = jnp.zeros((LANE,), jnp.float32)
    q_scale_mla = MLA_QK ** -0.5 * LOG2E
    rows = [zero] * VEC_ROWS
    rows[V_GQ_MLA] = _lane_vec(g_q_mla) * q_scale_mla
    rows[V_GQ_MLA_SW] = swapped(g_q_mla) * q_scale_mla
    rows[V_GK_MLA] = _lane_vec(g_k_mla)
    rows[V_GK_MLA_SW] = swapped(g_k_mla)
    rows[V_GQ_FOX] = _lane_vec(g_q_fox) * (FOX_DIM ** -0.5 * LOG2E)
    rows[V_GK_FOX] = _lane_vec(g_k_fox)
    rows[V_B_FORGET] = _lane_vec(b_forget, MISC_GATE)
    rows[V_ADD_Q_FOX] = (zero.at[GATE_LANE + N_SPLIT:GATE_LANE + 2 * N_SPLIT].set(1.0)
                         .at[FLAG_FOX].set(1.0))
    rows[V_ONES_K_FOX] = zero.at[GATE_LANE:GATE_LANE + N_SPLIT].set(1.0)
    rows[V_ADD_Q_MLA] = zero.at[FLAG_MLA].set(1.0)
    rows[V_ONES_V] = zero.at[ONES_LANE].set(1.0)
    vec = jnp.stack(rows)
    return dict(wcat=wcat, gcq=g_cq[None], wuq=wuq, gckv=g_ckv[None], wkn=wkn, wv=wv, vec=vec)


def _gate_selectors():
    selq = np.zeros((N_SPLIT * LANE, FOX_HEADS * LANE), np.float32)
    selk = np.zeros((N_SPLIT * LANE, FOX_HEADS * LANE), np.float32)
    for part in range(N_SPLIT):
        for hd in range(FOX_HEADS):
            selq[part * LANE + MISC_GATE + hd, hd * LANE + GATE_LANE + part] = 1.0
            selk[part * LANE + MISC_GATE + hd, hd * LANE + GATE_LANE + N_SPLIT + part] = -1.0
    return jnp.asarray(selq, jnp.bfloat16), jnp.asarray(selk, jnp.bfloat16)


def _rope_table(lp):
    pos = (jnp.arange(lp, dtype=jnp.int32) - PAD).astype(jnp.float32)
    inv_freq = ROPE_BASE ** (-jnp.arange(0, MLA_ROPE, 2, dtype=jnp.float32) / MLA_ROPE)
    ang = pos[:, None] * inv_freq[None, :]
    cos, sin = jnp.cos(ang), jnp.sin(ang)
    z = jnp.zeros((lp, LANE), jnp.float32)
    lo, mid, hi = MLA_NOPE, MLA_NOPE + HALF_ROPE, MLA_NOPE + MLA_ROPE
    cos_t = z.at[:, :lo].set(1.0).at[:, lo:mid].set(cos).at[:, mid:hi].set(cos)
    sin_sw = z.at[:, lo:mid].set(-sin).at[:, mid:hi].set(sin)
    return jnp.concatenate([cos_t, sin_sw], axis=1)


def _token_tile(lp):
    if lp % FLASH_TQ:
        raise ValueError(f"padded length {lp} is not a multiple of {FLASH_TQ}")
    return FLASH_TQ


def kernel(x, meta_tokens, g_mix, g_mlp, w_in_attn, g_cq, w_uq, g_ckv, w_ukv, g_q_mla, g_k_mla,
           g_q_fox, g_k_fox, b_forget, w_out_attn, w_in_conv, conv_w, w_out_conv, w_mlp_up,
           w_mlp_down):
    b, seq, d = x.shape
    assert d == D_MODEL and (PAD + N_META + seq) % BLOCK == 0
    lp = PAD + N_META + seq
    tm = _token_tile(lp)
    bf = jnp.bfloat16

    meta = jnp.broadcast_to(meta_tokens.astype(x.dtype)[None], (b, N_META, d))
    h = jnp.concatenate([jnp.zeros((b, PAD, d), x.dtype), meta, x], axis=1)

    rope_tab = _rope_table(lp)
    tri = (jnp.arange(tm)[:, None] >= jnp.arange(tm)[None, :]).astype(bf)
    selq, selk = _gate_selectors()

    for layer in range(DEPTH):
        j = layer // 2
        gmix = g_mix[layer][None]
        if layer % 2 == 0:
            p = _attn_params(w_in_attn[j], g_cq[j], w_uq[j], g_ckv[j], w_ukv[j], g_q_mla[j],
                             g_k_mla[j], g_q_fox[j], g_k_fox[j], b_forget[j])
            q, k, v = _attn_in(h, gmix, p, rope_tab, tri, selq, selk, tm)
            y = _flash(q, k, v)
            wo = w_out_attn[j].astype(bf)
        else:
            cw = jnp.zeros((8, d), jnp.float32).at[0:3].set(conv_w[j])
            y = _conv_in(h, gmix, w_in_conv[j].astype(bf), cw, tm)
            wo = w_out_conv[j].astype(bf)
        h = _mix_out_mlp(h.reshape(b * lp, d), y.reshape(b * lp, d), wo, g_mlp[layer][None],
                         w_mlp_up[layer].astype(bf), w_mlp_down[layer].astype(bf), tm
                         ).reshape(b, lp, d)
    return h[:, PAD + N_META:]
```

```python
import functools

import numpy as np
import jax
import jax.numpy as jnp
from jax import lax
from jax.experimental import pallas as pl
from jax.experimental.pallas import tpu as pltpu

D_MODEL = 1024
DEPTH = 4
N_META = 16
BLOCK = 128
PAD = 2 * BLOCK - N_META
MLA_HEADS = 8
MLA_NOPE = 64
MLA_ROPE = 32
MLA_QK = MLA_NOPE + MLA_ROPE
MLA_V = 64
Q_LORA = 384
KV_LORA = 256
ROPE_BASE = 10000.0
FOX_HEADS = 8
FOX_DIM = 64
D_FF = 4 * D_MODEL
EPS = 1e-6
NEG = -1e30

LANE = 128
HEADS = MLA_HEADS + FOX_HEADS
HALF_ROPE = MLA_ROPE // 2
GATE_LANE = FOX_DIM
N_SPLIT = 3
FLAG_MLA = MLA_QK
FLAG_FOX = GATE_LANE + 2 * N_SPLIT
PAD_KEY = -30000.0
LOG2E = 1.4426950408889634
MISC_GATE = 0
MISC_ROPE = MLA_NOPE

OFF_CQ = 0
OFF_CKV = OFF_CQ + Q_LORA
OFF_FQ = OFF_CKV + KV_LORA
OFF_FK = OFF_FQ + FOX_HEADS * LANE
OFF_FV = OFF_FK + FOX_HEADS * LANE
OFF_MISC = OFF_FV + FOX_HEADS * LANE
OFF_MISC_SW = OFF_MISC + LANE
W_CAT = OFF_MISC_SW + LANE

(V_GQ_MLA, V_GQ_MLA_SW, V_GK_MLA, V_GK_MLA_SW, V_GQ_FOX, V_GK_FOX, V_B_FORGET, V_ADD_Q_FOX,
 V_ONES_K_FOX, V_ADD_Q_MLA, V_ONES_V) = range(11)
VEC_ROWS = 16
ONES_LANE = MLA_V
PAIR = 2 * LANE

FF_CHUNK = 1024
FLASH_TQ = 768
FLASH_TK = 384
VMEM_LIMIT = 56 * 1024 * 1024


def _const_spec(shape):
    nd = len(shape)
    return pl.BlockSpec(shape, lambda *_: (0,) * nd, pipeline_mode=pl.Buffered(1))


def _rms(x, g, n):
    ms = jnp.sum(x * x, axis=-1, keepdims=True) * (1.0 / n)
    return x * lax.rsqrt(ms + EPS) * g


def _split3(x):
    hi = x.astype(jnp.bfloat16)
    r1 = x - hi.astype(jnp.float32)
    mid = r1.astype(jnp.bfloat16)
    lo = (r1 - mid.astype(jnp.float32)).astype(jnp.bfloat16)
    return jnp.concatenate([hi, mid, lo], axis=1)


def _dot(a, b):
    return jnp.dot(a, b, preferred_element_type=jnp.float32)


def _attn_in_kernel(h_ref, gmix_ref, wcat_ref, gcq_ref, wuq_ref, gckv_ref, wkn_ref, wv_ref,
                    vec_ref, rope_ref, tri_ref, selq_ref, selk_ref,
                    q_ref, k_ref, v_ref, carry_ref, *, tm):
    i = pl.program_id(1)

    @pl.when(i == 0)
    def _():
        carry_ref[...] = jnp.zeros_like(carry_ref)

    x = h_ref[0]
    hn = _rms(x, gmix_ref[...], D_MODEL).astype(jnp.bfloat16)

    def seg(lo, width):
        return _dot(hn, wcat_ref[:, lo:lo + width])

    def vec(r):
        return vec_ref[r:r + 1, :]

    cos_t = rope_ref[:, 0:LANE]
    sin_sw = rope_ref[:, LANE:2 * LANE]
    gc_q, gs_q = vec(V_GQ_MLA) * cos_t, vec(V_GQ_MLA_SW) * sin_sw
    gc_k, gs_k = vec(V_GK_MLA) * cos_t, vec(V_GK_MLA_SW) * sin_sw
    gq_fox, gk_fox = vec(V_GQ_FOX), vec(V_GK_FOX)
    add_q_fox, add_q_mla = vec(V_ADD_Q_FOX), vec(V_ADD_Q_MLA)

    lane = lax.broadcasted_iota(jnp.int32, (tm, LANE), 1)
    row = lax.broadcasted_iota(jnp.int32, (tm, LANE), 0)
    valid = (i * tm + row) >= PAD
    pad_key = jnp.where(valid, 0.0, PAD_KEY)
    add_k_mla = jnp.where(lane == FLAG_MLA, pad_key, 0.0)
    add_k_fox = vec(V_ONES_K_FOX) + jnp.where(lane == FLAG_FOX, pad_key, 0.0)

    misc2 = seg(OFF_MISC, PAIR)
    misc = misc2[:, 0:LANE]
    kpe = jnp.where((lane >= MISC_ROPE) & (lane < MISC_ROPE + MLA_ROPE), misc, 0.0)
    k_rot = misc2[:, LANE:PAIR] * gs_k
    xl = misc + vec(V_B_FORGET)
    logf = jnp.minimum(xl, 0.0) - jnp.log1p(jnp.exp(-jnp.abs(xl)))
    logf = jnp.where(valid & (lane >= MISC_GATE) & (lane < MISC_GATE + FOX_HEADS), logf, 0.0)
    cs = _dot(tri_ref[...], _split3(logf))
    cum = (cs[:, 0:LANE] + cs[:, LANE:2 * LANE]) + cs[:, 2 * LANE:3 * LANE] + carry_ref[0:1, :]
    carry_ref[0:1, :] = cum[tm - 1:tm, :]
    cum3 = _split3(cum * LOG2E)
    gate_q = _dot(cum3, selq_ref[...])
    gate_k = _dot(cum3, selk_ref[...])

    def inv_rms(xv, n):
        return lax.rsqrt(jnp.sum(xv * xv, axis=-1, keepdims=True) * (1.0 / n) + EPS)

    cqn = _rms(seg(OFF_CQ, Q_LORA), gcq_ref[...], Q_LORA).astype(jnp.bfloat16)
    ckvn = _rms(seg(OFF_CKV, KV_LORA), gckv_ref[...], KV_LORA).astype(jnp.bfloat16)
    ones_v = vec(V_ONES_V)
    for g in range(MLA_HEADS // 2):
        cols = slice(g * PAIR, (g + 1) * PAIR)
        cols_sw = slice(MLA_HEADS * LANE + g * PAIR, MLA_HEADS * LANE + (g + 1) * PAIR)
        xq2 = _dot(cqn, wuq_ref[:, cols])
        xq2_sw = _dot(cqn, wuq_ref[:, cols_sw])
        xk2 = _dot(ckvn, wkn_ref[:, cols])
        xv2 = _dot(ckvn, wv_ref[:, cols])
        for e in range(2):
            hd, sl = 2 * g + e, slice(e * LANE, (e + 1) * LANE)
            xq = xq2[:, sl]
            q_ref[0, hd] = ((xq * gc_q + xq2_sw[:, sl] * gs_q) * inv_rms(xq, MLA_QK) + add_q_mla
                            ).astype(jnp.bfloat16)
            xk = xk2[:, sl] + kpe
            k_ref[0, hd] = ((xk * gc_k + k_rot) * inv_rms(xk, MLA_QK) + add_k_mla
                            ).astype(jnp.bfloat16)
            v_ref[0, hd] = (xv2[:, sl] + ones_v).astype(jnp.bfloat16)

    for g in range(FOX_HEADS // 2):
        xq2 = seg(OFF_FQ + g * PAIR, PAIR)
        xk2 = seg(OFF_FK + g * PAIR, PAIR)
        xv2 = seg(OFF_FV + g * PAIR, PAIR)
        for e in range(2):
            hd, sl = 2 * g + e, slice(e * LANE, (e + 1) * LANE)
            gl = slice(hd * LANE, (hd + 1) * LANE)
            xq = xq2[:, sl]
            q_ref[0, MLA_HEADS + hd] = (xq * gq_fox * inv_rms(xq, FOX_DIM) + gate_q[:, gl]
                                        + add_q_fox).astype(jnp.bfloat16)
            xk = xk2[:, sl]
            k_ref[0, MLA_HEADS + hd] = (xk * gk_fox * inv_rms(xk, FOX_DIM) + gate_k[:, gl]
                                        + add_k_fox).astype(jnp.bfloat16)
            v_ref[0, MLA_HEADS + hd] = (xv2[:, sl] + ones_v).astype(jnp.bfloat16)


def _attn_in(h, gmix, p, rope_tab, tri, selq, selk, tm):
    b, lp, d = h.shape
    nt = lp // tm
    kern = functools.partial(_attn_in_kernel, tm=tm)
    qk_shape = jax.ShapeDtypeStruct((b, HEADS, lp, LANE), jnp.bfloat16)
    qk_spec = pl.BlockSpec((1, HEADS, tm, LANE), lambda bi, i: (bi, 0, i, 0))
    return pl.pallas_call(
        kern,
        grid=(b, nt),
        in_specs=[
            pl.BlockSpec((1, tm, d), lambda bi, i: (bi, i, 0)),
            _const_spec((1, d)),
            _const_spec((d, W_CAT)),
            _const_spec((1, Q_LORA)),
            _const_spec((Q_LORA, 2 * MLA_HEADS * LANE)),
            _const_spec((1, KV_LORA)),
            _const_spec((KV_LORA, MLA_HEADS * LANE)),
            _const_spec((KV_LORA, MLA_HEADS * LANE)),
            _const_spec((VEC_ROWS, LANE)),
            pl.BlockSpec((tm, 2 * LANE), lambda bi, i: (i, 0)),
            _const_spec((tm, tm)),
            _const_spec((N_SPLIT * LANE, FOX_HEADS * LANE)),
            _const_spec((N_SPLIT * LANE, FOX_HEADS * LANE)),
        ],
        out_specs=[qk_spec, qk_spec, qk_spec],
        out_shape=[qk_shape, qk_shape, qk_shape],
        scratch_shapes=[pltpu.VMEM((8, LANE), jnp.float32)],
        compiler_params=pltpu.CompilerParams(
            dimension_semantics=("arbitrary", "arbitrary"), vmem_limit_bytes=VMEM_LIMIT),
        name="attn_in",
    )(h, gmix, p["wcat"], p["gcq"], p["wuq"], p["gckv"], p["wkn"], p["wv"], p["vec"],
      rope_tab, tri, selq, selk)


def _flash_kernel(q_ref, k_ref, v_ref, o_ref, m_ref, acc_ref, al_ref, p_ref, *, tq, tk, nq):
    chunks = tq // tk

    def query_block(qi, carry):
        qbase = qi * tq
        m_ref[...] = jnp.full(m_ref.shape, NEG, jnp.float32)
        acc_ref[...] = jnp.zeros(acc_ref.shape, jnp.float32)
        al_ref[1, chunks - 1] = jnp.zeros((tq, LANE), jnp.float32)
        p_ref[1, chunks - 1] = jnp.zeros((tq, tk), jnp.bfloat16)

        def softmax(j, u, base, diagonal):
            r0 = u * tk if diagonal else 0
            rows = slice(r0, tq)
            start = pl.multiple_of(base + u * tk, tk)
            q_rows = pl.ds(pl.multiple_of(qbase + r0, tk), tq - r0)
            s = lax.dot_general(q_ref[0, j, q_rows, :], k_ref[0, j, pl.ds(start, tk), :],
                                (((1,), (1,)), ((), ())), preferred_element_type=jnp.float32)
            if diagonal:
                row = lax.broadcasted_iota(jnp.int32, (tq - r0, tk), 0)
                col = lax.broadcasted_iota(jnp.int32, (tq - r0, tk), 1)
                s = jnp.where(col <= row, s, NEG)
            m_prev = m_ref[j, rows, :]
            m_next = jnp.maximum(m_prev, jnp.max(s, axis=1, keepdims=True))
            p = jnp.exp2(s - jnp.concatenate([m_next] * (tk // LANE), axis=1))
            al_ref[j, u, rows, :] = jnp.exp2(m_prev - m_next)
            p_ref[j, u, rows, :] = p.astype(jnp.bfloat16)
            m_ref[j, rows, :] = m_next

        def pv(j, u, base, diagonal):
            r0 = u * tk if diagonal else 0
            rows = slice(r0, tq)
            start = pl.multiple_of(base + u * tk, tk)
            acc_ref[j, rows, :] = (acc_ref[j, rows, :] * al_ref[j, u, rows, :]
                                   + _dot(p_ref[j, u, rows, :], v_ref[0, j, pl.ds(start, tk), :]))

        def block(kb, diagonal):
            base = kb * tq
            for u in range(chunks):
                softmax(0, u, base, diagonal)
                if u == 0:
                    pv(1, chunks - 1, jnp.maximum(kb - 1, 0) * tq, False)
                else:
                    pv(1, u - 1, base, diagonal)
                softmax(1, u, base, diagonal)
                pv(0, u, base, diagonal)

        odd = qi & 1

        @pl.when(odd == 1)
        def _():
            block(0, False)

        def body(pair, c):
            kb = odd + 2 * pair
            block(kb, False)
            block(kb + 1, False)
            return c

        lax.fori_loop(0, lax.shift_right_logical(qi, 1), body, 0)
        block(qi, True)
        pv(1, chunks - 1, qbase, True)

        o0 = acc_ref[0]
        o1 = acc_ref[1]
        o0 = o0 / o0[:, ONES_LANE:ONES_LANE + 1]
        o1 = o1 / o1[:, ONES_LANE:ONES_LANE + 1]
        lane = lax.broadcasted_iota(jnp.int32, (tq, LANE), 1)
        o = jnp.where(lane < MLA_V, o0, pltpu.roll(o1, MLA_V, 1))
        o_ref[0, pl.ds(pl.multiple_of(qbase, tq), tq), :] = o.astype(jnp.bfloat16)
        return carry

    lax.fori_loop(0, nq, query_block, 0)


def _flash(q, k, v):
    b, _, lp, _ = q.shape
    tq, tk = FLASH_TQ, FLASH_TK
    kern = functools.partial(_flash_kernel, tq=tq, tk=tk, nq=lp // tq)
    qkv_spec = pl.BlockSpec((1, 2, lp, LANE), lambda bi, hp: (bi, hp, 0, 0))
    return pl.pallas_call(
        kern,
        grid=(b, HEADS // 2),
        in_specs=[qkv_spec, qkv_spec, qkv_spec],
        out_specs=pl.BlockSpec((1, lp, LANE), lambda bi, hp: (bi, 0, hp)),
        out_shape=jax.ShapeDtypeStruct((b, lp, HEADS * MLA_V), jnp.bfloat16),
        scratch_shapes=[pltpu.VMEM((2, tq, LANE), jnp.float32)] * 2
        + [pltpu.VMEM((2, tq // tk, tq, LANE), jnp.float32),
           pltpu.VMEM((2, tq // tk, tq, tk), jnp.bfloat16)],
        compiler_params=pltpu.CompilerParams(
            dimension_semantics=("arbitrary", "arbitrary"), vmem_limit_bytes=VMEM_LIMIT),
        name="flash",
    )(q, k, v)


def _conv_in_kernel(h_ref, gmix_ref, win_ref, cw_ref, y_ref, gs_ref, *, tm):
    i = pl.program_id(1)

    @pl.when(i == 0)
    def _():
        gs_ref[0:8, :] = jnp.zeros((8, D_MODEL), jnp.float32)

    x = h_ref[0]
    hn = _rms(x, gmix_ref[...], D_MODEL).astype(jnp.bfloat16)
    gate_c = _dot(hn, win_ref[:, D_MODEL:2 * D_MODEL])
    u = _dot(hn, win_ref[:, 2 * D_MODEL:3 * D_MODEL])
    row = lax.broadcasted_iota(jnp.int32, (tm, D_MODEL), 0)
    g = jnp.where((i * tm + row) >= PAD, gate_c * u, 0.0)
    gs_ref[8:tm + 8, :] = g
    y = (cw_ref[0:1, :] * gs_ref[6:tm + 6, :] + cw_ref[1:2, :] * gs_ref[7:tm + 7, :]
         + cw_ref[2:3, :] * g)
    gs_ref[0:8, :] = gs_ref[tm:tm + 8, :]
    gate_b = _dot(hn, win_ref[:, 0:D_MODEL])
    y_ref[0] = (gate_b * y).astype(jnp.bfloat16)


def _conv_in(h, gmix, win, cw, tm):
    b, lp, d = h.shape
    kern = functools.partial(_conv_in_kernel, tm=tm)
    return pl.pallas_call(
        kern,
        grid=(b, lp // tm),
        in_specs=[
            pl.BlockSpec((1, tm, d), lambda bi, i: (bi, i, 0)),
            _const_spec((1, d)),
            _const_spec((d, 3 * d)),
            _const_spec((8, d)),
        ],
        out_specs=pl.BlockSpec((1, tm, d), lambda bi, i: (bi, i, 0)),
        out_shape=jax.ShapeDtypeStruct((b, lp, d), jnp.bfloat16),
        scratch_shapes=[pltpu.VMEM((tm + 8, d), jnp.float32)],
        compiler_params=pltpu.CompilerParams(
            dimension_semantics=("arbitrary", "arbitrary"), vmem_limit_bytes=VMEM_LIMIT),
        name="conv_in",
    )(h, gmix, win, cw)


def _mix_out_mlp_kernel(h_ref, y_ref, wo_ref, gmlp_ref, wup_ref, wdn_ref, out_ref):
    h1 = h_ref[...] + _dot(y_ref[...], wo_ref[...])
    n = _rms(h1, gmlp_ref[...], D_MODEL).astype(jnp.bfloat16)
    acc = h1
    for c in range(D_FF // FF_CHUNK):
        sl = slice(c * FF_CHUNK, (c + 1) * FF_CHUNK)
        a = jnp.maximum(_dot(n, wup_ref[:, sl]), 0.0)
        acc = acc + _dot((a * a).astype(jnp.bfloat16), wdn_ref[sl, :])
    out_ref[...] = acc


def _mix_out_mlp(h, y, wo, gmlp, wup, wdn, tm):
    r, d = h.shape
    return pl.pallas_call(
        _mix_out_mlp_kernel,
        grid=(r // tm,),
        in_specs=[
            pl.BlockSpec((tm, d), lambda i: (i, 0)),
            pl.BlockSpec((tm, d), lambda i: (i, 0)),
            _const_spec((d, d)),
            _const_spec((1, d)),
            _const_spec((d, D_FF)),
            _const_spec((D_FF, d)),
        ],
        out_specs=pl.BlockSpec((tm, d), lambda i: (i, 0)),
        out_shape=jax.ShapeDtypeStruct((r, d), jnp.float32),
        compiler_params=pltpu.CompilerParams(
            dimension_semantics=("arbitrary",), vmem_limit_bytes=VMEM_LIMIT),
        name="mix_out_mlp",
    )(h, y, wo, gmlp, wup, wdn)


def _pad_heads(w, heads, dim):
    k = w.shape[0]
    w = w.reshape(k, heads, dim)
    w = jnp.pad(w, ((0, 0), (0, 0), (0, LANE - dim)))
    return w.reshape(k, heads * LANE)


def _lane_vec(v, offset=0):
    return jnp.zeros((LANE,), jnp.float32).at[offset:offset + v.shape[0]].set(v)


def _attn_params(w_in, g_cq, w_uq, g_ckv, w_ukv, g_q_mla, g_k_mla, g_q_fox, g_k_fox, b_forget):
    bf = jnp.bfloat16
    o1 = Q_LORA
    o2 = o1 + KV_LORA
    o3 = o2 + MLA_ROPE
    o4 = o3 + FOX_HEADS * FOX_DIM
    o5 = o4 + FOX_HEADS * FOX_DIM
    o6 = o5 + FOX_HEADS * FOX_DIM
    misc = jnp.zeros((D_MODEL, LANE), jnp.float32)
    misc = misc.at[:, MISC_GATE:MISC_GATE + FOX_HEADS].set(w_in[:, o6:])
    misc = misc.at[:, MISC_ROPE:MISC_ROPE + MLA_ROPE].set(w_in[:, o2:o3])
    misc_sw = jnp.zeros((D_MODEL, LANE), jnp.float32)
    misc_sw = misc_sw.at[:, MISC_ROPE:MISC_ROPE + HALF_ROPE].set(w_in[:, o2 + HALF_ROPE:o3])
    misc_sw = misc_sw.at[:, MISC_ROPE + HALF_ROPE:MISC_ROPE + MLA_ROPE].set(
        w_in[:, o2:o2 + HALF_ROPE])
    wcat = jnp.concatenate([
        w_in[:, :o1], w_in[:, o1:o2],
        _pad_heads(w_in[:, o3:o4], FOX_HEADS, FOX_DIM),
        _pad_heads(w_in[:, o4:o5], FOX_HEADS, FOX_DIM),
        _pad_heads(w_in[:, o5:o6], FOX_HEADS, FOX_DIM), misc, misc_sw], axis=1).astype(bf)
    kv = w_ukv.reshape(KV_LORA, MLA_HEADS, MLA_NOPE + MLA_V)
    wkn = _pad_heads(kv[:, :, :MLA_NOPE].reshape(KV_LORA, -1), MLA_HEADS, MLA_NOPE).astype(bf)
    wv = _pad_heads(kv[:, :, MLA_NOPE:].reshape(KV_LORA, -1), MLA_HEADS, MLA_V).astype(bf)
    lo, mid, hi = MLA_NOPE, MLA_NOPE + HALF_ROPE, MLA_NOPE + MLA_ROPE
    uq = w_uq.reshape(Q_LORA, MLA_HEADS, MLA_QK)
    uq_sw = jnp.zeros((Q_LORA, MLA_HEADS, LANE), jnp.float32)
    uq_sw = uq_sw.at[:, :, lo:mid].set(uq[:, :, mid:hi]).at[:, :, mid:hi].set(uq[:, :, lo:mid])
    wuq = jnp.concatenate([_pad_heads(w_uq, MLA_HEADS, MLA_QK),
                           uq_sw.reshape(Q_LORA, MLA_HEADS * LANE)], axis=1).astype(bf)

    def swapped(g):
        return jnp.zeros((LANE,), jnp.float32).at[lo:mid].set(g[mid:hi]).at[mid:hi].set(g[lo:mid])

    zero = jnp.zeros((LANE,), jnp.float32)
    q_scale_mla = MLA_QK ** -0.5 * LOG2E
    rows = [zero] * VEC_ROWS
    rows[V_GQ_MLA] = _lane_vec(g_q_mla) * q_scale_mla
    rows[V_GQ_MLA_SW] = swapped(g_q_mla) * q_scale_mla
    rows[V_GK_MLA] = _lane_vec(g_k_mla)
    rows[V_GK_MLA_SW] = swapped(g_k_mla)
    rows[V_GQ_FOX] = _lane_vec(g_q_fox) * (FOX_DIM ** -0.5 * LOG2E)
    rows[V_GK_FOX] = _lane_vec(g_k_fox)
    rows[V_B_FORGET] = _lane_vec(b_forget, MISC_GATE)
    rows[V_ADD_Q_FOX] = (zero.at[GATE_LANE + N_SPLIT:GATE_LANE + 2 * N_SPLIT].set(1.0)
                         .at[FLAG_FOX].set(1.0))
    rows[V_ONES_K_FOX] = zero.at[GATE_LANE:GATE_LANE + N_SPLIT].set(1.0)
    rows[V_ADD_Q_MLA] = zero.at[FLAG_MLA].set(1.0)
    rows[V_ONES_V] = zero.at[ONES_LANE].set(1.0)
    vec = jnp.stack(rows)
    return dict(wcat=wcat, gcq=g_cq[None], wuq=wuq, gckv=g_ckv[None], wkn=wkn, wv=wv, vec=vec)


def _gate_selectors():
    selq = np.zeros((N_SPLIT * LANE, FOX_HEADS * LANE), np.float32)
    selk = np.zeros((N_SPLIT * LANE, FOX_HEADS * LANE), np.float32)
    for part in range(N_SPLIT):
        for hd in range(FOX_HEADS):
            selq[part * LANE + MISC_GATE + hd, hd * LANE + GATE_LANE + part] = 1.0
            selk[part * LANE + MISC_GATE + hd, hd * LANE + GATE_LANE + N_SPLIT + part] = -1.0
    return jnp.asarray(selq, jnp.bfloat16), jnp.asarray(selk, jnp.bfloat16)


def _rope_table(lp):
    pos = (jnp.arange(lp, dtype=jnp.int32) - PAD).astype(jnp.float32)
    inv_freq = ROPE_BASE ** (-jnp.arange(0, MLA_ROPE, 2, dtype=jnp.float32) / MLA_ROPE)
    ang = pos[:, None] * inv_freq[None, :]
    cos, sin = jnp.cos(ang), jnp.sin(ang)
    z = jnp.zeros((lp, LANE), jnp.float32)
    lo, mid, hi = MLA_NOPE, MLA_NOPE + HALF_ROPE, MLA_NOPE + MLA_ROPE
    cos_t = z.at[:, :lo].set(1.0).at[:, lo:mid].set(cos).at[:, mid:hi].set(cos)
    sin_sw = z.at[:, lo:mid].set(-sin).at[:, mid:hi].set(sin)
    return jnp.concatenate([cos_t, sin_sw], axis=1)


def _token_tile(lp):
    if lp % FLASH_TQ:
        raise ValueError(f"padded length {lp} is not a multiple of {FLASH_TQ}")
    return FLASH_TQ


def kernel(x, meta_tokens, g_mix, g_mlp, w_in_attn, g_cq, w_uq, g_ckv, w_ukv, g_q_mla, g_k_mla,
           g_q_fox, g_k_fox, b_forget, w_out_attn, w_in_conv, conv_w, w_out_conv, w_mlp_up,
           w_mlp_down):
    b, seq, d = x.shape
    assert d == D_MODEL and (PAD + N_META + seq) % BLOCK == 0
    lp = PAD + N_META + seq
    tm = _token_tile(lp)
    bf = jnp.bfloat16

    meta = jnp.broadcast_to(meta_tokens.astype(x.dtype)[None], (b, N_META, d))
    h = jnp.concatenate([jnp.zeros((b, PAD, d), x.dtype), meta, x], axis=1)

    rope_tab = _rope_table(lp)
    tri = (jnp.arange(tm)[:, None] >= jnp.arange(tm)[None, :]).astype(bf)
    selq, selk = _gate_selectors()

    for layer in range(DEPTH):
        j = layer // 2
        gmix = g_mix[layer][None]
        if layer % 2 == 0:
            p = _attn_params(w_in_attn[j], g_cq[j], w_uq[j], g_ckv[j], w_ukv[j], g_q_mla[j],
                             g_k_mla[j], g_q_fox[j], g_k_fox[j], b_forget[j])
            q, k, v = _attn_in(h, gmix, p, rope_tab, tri, selq, selk, tm)
            y = _flash(q, k, v)
            wo = w_out_attn[j].astype(bf)
        else:
            cw = jnp.zeros((8, d), jnp.float32).at[0:3].set(conv_w[j])
            y = _conv_in(h, gmix, w_in_conv[j].astype(bf), cw, tm)
            wo = w_out_conv[j].astype(bf)
        h = _mix_out_mlp(h.reshape(b * lp, d), y.reshape(b * lp, d), wo, g_mlp[layer][None],
                         w_mlp_up[layer].astype(bf), w_mlp_down[layer].astype(bf), tm
                         ).reshape(b, lp, d)
    return h[:, PAD + N_META:]
```

```python
import functools

import numpy as np
import jax
import jax.numpy as jnp
from jax import lax
from jax.experimental import pallas as pl
from jax.experimental.pallas import tpu as pltpu

D_MODEL = 1024
DEPTH = 4
N_META = 16
BLOCK = 128
PAD = 2 * BLOCK - N_META
MLA_HEADS = 8
MLA_NOPE = 64
MLA_ROPE = 32
MLA_QK = MLA_NOPE + MLA_ROPE
MLA_V = 64
Q_LORA = 384
KV_LORA = 256
ROPE_BASE = 10000.0
FOX_HEADS = 8
FOX_DIM = 64
D_FF = 4 * D_MODEL
EPS = 1e-6
NEG = -1e30

LANE = 128
HEADS = MLA_HEADS + FOX_HEADS
HALF_ROPE = MLA_ROPE // 2
GATE_LANE = FOX_DIM
N_SPLIT = 3
FLAG_MLA = MLA_QK
FLAG_FOX = GATE_LANE + 2 * N_SPLIT
PAD_KEY = -30000.0
LOG2E = 1.4426950408889634
MISC_GATE = 0
MISC_ROPE = MLA_NOPE

OFF_CQ = 0
OFF_CKV = OFF_CQ + Q_LORA
OFF_FQ = OFF_CKV + KV_LORA
OFF_FK = OFF_FQ + FOX_HEADS * LANE
OFF_FV = OFF_FK + FOX_HEADS * LANE
OFF_MISC = OFF_FV + FOX_HEADS * LANE
OFF_MISC_SW = OFF_MISC + LANE
W_CAT = OFF_MISC_SW + LANE

(V_GQ_MLA, V_GQ_MLA_SW, V_GK_MLA, V_GK_MLA_SW, V_GQ_FOX, V_GK_FOX, V_B_FORGET, V_ADD_Q_FOX,
 V_ONES_K_FOX, V_ADD_Q_MLA, V_ONES_V) = range(11)
VEC_ROWS = 16
ONES_LANE = MLA_V
PAIR = 2 * LANE

FF_CHUNK = 1024
FLASH_TQ = 768
FLASH_TK = 384
VMEM_LIMIT = 56 * 1024 * 1024


def _const_spec(shape):
    nd = len(shape)
    return pl.BlockSpec(shape, lambda *_: (0,) * nd, pipeline_mode=pl.Buffered(1))


def _layer_spec(shape, layer):
    nd = len(shape)
    return pl.BlockSpec((1,) + shape, lambda *_: (layer,) + (0,) * nd,
                        pipeline_mode=pl.Buffered(1))


def _rms(x, g, n):
    ms = jnp.sum(x * x, axis=-1, keepdims=True) * (1.0 / n)
    return x * lax.rsqrt(ms + EPS) * g


def _split3(x):
    hi = x.astype(jnp.bfloat16).astype(jnp.float32)
    r1 = x - hi
    mid = r1.astype(jnp.bfloat16).astype(jnp.float32)
    lo = r1 - mid
    packed = hi + pltpu.roll(mid, FOX_HEADS, 1) + pltpu.roll(lo, 2 * FOX_HEADS, 1)
    return packed.astype(jnp.bfloat16)


def _dot(a, b):
    return jnp.dot(a, b, preferred_element_type=jnp.float32)


def _attn_in_kernel(h_ref, gmix_ref, wcat_ref, gcq_ref, wuq_ref, gckv_ref, wkn_ref, wv_ref,
                    vec_ref, rope_ref, tri_ref, selq_ref, selk_ref,
                    q_ref, k_ref, v_ref, carry_ref, *, tm):
    i = pl.program_id(1)

    @pl.when(i == 0)
    def _():
        carry_ref[...] = jnp.zeros_like(carry_ref)

    x = h_ref[0]
    hn = _rms(x, gmix_ref[...], D_MODEL).astype(jnp.bfloat16)

    def seg(lo, width):
        return _dot(hn, wcat_ref[:, lo:lo + width])

    def vec(r):
        return vec_ref[r:r + 1, :]

    cos_t = rope_ref[:, 0:LANE]
    sin_sw = rope_ref[:, LANE:2 * LANE]
    gc_q, gs_q = vec(V_GQ_MLA) * cos_t, vec(V_GQ_MLA_SW) * sin_sw
    gc_k, gs_k = vec(V_GK_MLA) * cos_t, vec(V_GK_MLA_SW) * sin_sw
    gq_fox, gk_fox = vec(V_GQ_FOX), vec(V_GK_FOX)
    add_q_fox, add_q_mla = vec(V_ADD_Q_FOX), vec(V_ADD_Q_MLA)

    lane = lax.broadcasted_iota(jnp.int32, (tm, LANE), 1)
    row = lax.broadcasted_iota(jnp.int32, (tm, LANE), 0)
    valid = (i * tm + row) >= PAD
    pad_key = jnp.where(valid, 0.0, PAD_KEY)
    add_k_mla = jnp.where(lane == FLAG_MLA, pad_key, 0.0)
    add_k_fox = vec(V_ONES_K_FOX) + jnp.where(lane == FLAG_FOX, pad_key, 0.0)

    misc2 = seg(OFF_MISC, PAIR)
    misc = misc2[:, 0:LANE]
    kpe = jnp.where((lane >= MISC_ROPE) & (lane < MISC_ROPE + MLA_ROPE), misc, 0.0)
    k_rot = misc2[:, LANE:PAIR] * gs_k
    xl = misc + vec(V_B_FORGET)
    logf = jnp.minimum(xl, 0.0) - jnp.log1p(jnp.exp(-jnp.abs(xl)))
    logf = jnp.where(valid & (lane >= MISC_GATE) & (lane < MISC_GATE + FOX_HEADS), logf, 0.0)
    cs = _dot(tri_ref[...], _split3(logf))
    cs = (cs + pltpu.roll(cs, LANE - FOX_HEADS, 1)) + pltpu.roll(cs, LANE - 2 * FOX_HEADS, 1)
    cum = jnp.where(lane < FOX_HEADS, cs, 0.0) + carry_ref[0:1, :]
    carry_ref[0:1, :] = cum[tm - 1:tm, :]
    cum3 = _split3(cum * LOG2E)
    gate_q = _dot(cum3, selq_ref[...])
    gate_k = _dot(cum3, selk_ref[...])

    def inv_rms(xv, n):
        return lax.rsqrt(jnp.sum(xv * xv, axis=-1, keepdims=True) * (1.0 / n) + EPS)

    cqn = _rms(seg(OFF_CQ, Q_LORA), gcq_ref[...], Q_LORA).astype(jnp.bfloat16)
    ckvn = _rms(seg(OFF_CKV, KV_LORA), gckv_ref[...], KV_LORA).astype(jnp.bfloat16)
    ones_v = vec(V_ONES_V)
    for g in range(MLA_HEADS // 2):
        cols = slice(g * PAIR, (g + 1) * PAIR)
        cols_sw = slice(MLA_HEADS * LANE + g * PAIR, MLA_HEADS * LANE + (g + 1) * PAIR)
        xq2 = _dot(cqn, wuq_ref[:, cols])
        xq2_sw = _dot(cqn, wuq_ref[:, cols_sw])
        xk2 = _dot(ckvn, wkn_ref[:, cols])
        xv2 = _dot(ckvn, wv_ref[:, cols])
        for e in range(2):
            hd, sl = 2 * g + e, slice(e * LANE, (e + 1) * LANE)
            xq = xq2[:, sl]
            q_ref[0, hd] = ((xq * gc_q + xq2_sw[:, sl] * gs_q) * inv_rms(xq, MLA_QK) + add_q_mla
                            ).astype(jnp.bfloat16)
            xk = xk2[:, sl] + kpe
            k_ref[0, hd] = ((xk * gc_k + k_rot) * inv_rms(xk, MLA_QK) + add_k_mla
                            ).astype(jnp.bfloat16)
            v_ref[0, hd] = (xv2[:, sl] + ones_v).astype(jnp.bfloat16)

    for g in range(FOX_HEADS // 2):
        xq2 = seg(OFF_FQ + g * PAIR, PAIR)
        xk2 = seg(OFF_FK + g * PAIR, PAIR)
        xv2 = seg(OFF_FV + g * PAIR, PAIR)
        for e in range(2):
            hd, sl = 2 * g + e, slice(e * LANE, (e + 1) * LANE)
            gl = slice(hd * LANE, (hd + 1) * LANE)
            xq = xq2[:, sl]
            q_ref[0, MLA_HEADS + hd] = (xq * gq_fox * inv_rms(xq, FOX_DIM) + gate_q[:, gl]
                                        + add_q_fox).astype(jnp.bfloat16)
            xk = xk2[:, sl]
            k_ref[0, MLA_HEADS + hd] = (xk * gk_fox * inv_rms(xk, FOX_DIM) + gate_k[:, gl]
                                        + add_k_fox).astype(jnp.bfloat16)
            v_ref[0, MLA_HEADS + hd] = (xv2[:, sl] + ones_v).astype(jnp.bfloat16)


def _attn_in(h, gmix, p, rope_tab, tri, selq, selk, tm):
    b, lp, d = h.shape
    nt = lp // tm
    kern = functools.partial(_attn_in_kernel, tm=tm)
    qk_shape = jax.ShapeDtypeStruct((b, HEADS, lp, LANE), jnp.bfloat16)
    qk_spec = pl.BlockSpec((1, HEADS, tm, LANE), lambda bi, i: (bi, 0, i, 0))
    return pl.pallas_call(
        kern,
        grid=(b, nt),
        in_specs=[
            pl.BlockSpec((1, tm, d), lambda bi, i: (bi, i, 0)),
            _const_spec((1, d)),
            _const_spec((d, W_CAT)),
            _const_spec((1, Q_LORA)),
            _const_spec((Q_LORA, 2 * MLA_HEADS * LANE)),
            _const_spec((1, KV_LORA)),
            _const_spec((KV_LORA, MLA_HEADS * LANE)),
            _const_spec((KV_LORA, MLA_HEADS * LANE)),
            _const_spec((VEC_ROWS, LANE)),
            pl.BlockSpec((tm, 2 * LANE), lambda bi, i: (i, 0)),
            _const_spec((tm, tm)),
            _const_spec((LANE, FOX_HEADS * LANE)),
            _const_spec((LANE, FOX_HEADS * LANE)),
        ],
        out_specs=[qk_spec, qk_spec, qk_spec],
        out_shape=[qk_shape, qk_shape, qk_shape],
        scratch_shapes=[pltpu.VMEM((8, LANE), jnp.float32)],
        compiler_params=pltpu.CompilerParams(
            dimension_semantics=("arbitrary", "arbitrary"), vmem_limit_bytes=VMEM_LIMIT),
        name="attn_in",
    )(h, gmix, p["wcat"], p["gcq"], p["wuq"], p["gckv"], p["wkn"], p["wv"], p["vec"],
      rope_tab, tri, selq, selk)


def _flash_kernel(q_ref, k_ref, v_ref, o_ref, m_ref, acc_ref, al_ref, p_ref, *, tq, tk, nq):
    chunks = tq // tk

    def query_block(qi, carry):
        qbase = qi * tq
        m_ref[...] = jnp.full(m_ref.shape, NEG, jnp.float32)
        acc_ref[...] = jnp.zeros(acc_ref.shape, jnp.float32)
        al_ref[1, chunks - 1] = jnp.zeros((tq, LANE), jnp.float32)
        p_ref[1, chunks - 1] = jnp.zeros((tq, tk), jnp.bfloat16)

        def softmax(j, u, base, diagonal):
            r0 = u * tk if diagonal else 0
            rows = slice(r0, tq)
            start = pl.multiple_of(base + u * tk, tk)
            q_rows = pl.ds(pl.multiple_of(qbase + r0, tk), tq - r0)
            s = lax.dot_general(q_ref[0, j, q_rows, :], k_ref[0, j, pl.ds(start, tk), :],
                                (((1,), (1,)), ((), ())), preferred_element_type=jnp.float32)
            if diagonal:
                row = lax.broadcasted_iota(jnp.int32, (tq - r0, tk), 0)
                col = lax.broadcasted_iota(jnp.int32, (tq - r0, tk), 1)
                s = jnp.where(col <= row, s, NEG)
            m_prev = m_ref[j, rows, :]
            m_next = jnp.maximum(m_prev, jnp.max(s, axis=1, keepdims=True))
            p = jnp.exp2(s - jnp.concatenate([m_next] * (tk // LANE), axis=1))
            al_ref[j, u, rows, :] = jnp.exp2(m_prev - m_next)
            p_ref[j, u, rows, :] = p.astype(jnp.bfloat16)
            m_ref[j, rows, :] = m_next

        def pv(j, u, base, diagonal):
            r0 = u * tk if diagonal else 0
            rows = slice(r0, tq)
            start = pl.multiple_of(base + u * tk, tk)
            acc_ref[j, rows, :] = (acc_ref[j, rows, :] * al_ref[j, u, rows, :]
                                   + _dot(p_ref[j, u, rows, :], v_ref[0, j, pl.ds(start, tk), :]))

        def block(kb, diagonal):
            base = kb * tq
            for u in range(chunks):
                softmax(0, u, base, diagonal)
                if u == 0:
                    pv(1, chunks - 1, jnp.maximum(kb - 1, 0) * tq, False)
                else:
                    pv(1, u - 1, base, diagonal)
                softmax(1, u, base, diagonal)
                pv(0, u, base, diagonal)

        odd = qi & 1

        @pl.when(odd == 1)
        def _():
            block(0, False)

        def body(pair, c):
            kb = odd + 2 * pair
            block(kb, False)
            block(kb + 1, False)
            return c

        lax.fori_loop(0, lax.shift_right_logical(qi, 1), body, 0)
        block(qi, True)
        pv(1, chunks - 1, qbase, True)

        o0 = acc_ref[0]
        o1 = acc_ref[1]
        o0 = o0 / o0[:, ONES_LANE:ONES_LANE + 1]
        o1 = o1 / o1[:, ONES_LANE:ONES_LANE + 1]
        lane = lax.broadcasted_iota(jnp.int32, (tq, LANE), 1)
        o = jnp.where(lane < MLA_V, o0, pltpu.roll(o1, MLA_V, 1))
        o_ref[0, pl.ds(pl.multiple_of(qbase, tq), tq), :] = o.astype(jnp.bfloat16)
        return carry

    lax.fori_loop(0, nq, query_block, 0)


def _flash(q, k, v):
    b, _, lp, _ = q.shape
    tq, tk = FLASH_TQ, FLASH_TK
    kern = functools.partial(_flash_kernel, tq=tq, tk=tk, nq=lp // tq)
    qkv_spec = pl.BlockSpec((1, 2, lp, LANE), lambda bi, hp: (bi, hp, 0, 0))
    return pl.pallas_call(
        kern,
        grid=(b, HEADS // 2),
        in_specs=[qkv_spec, qkv_spec, qkv_spec],
        out_specs=pl.BlockSpec((1, lp, LANE), lambda bi, hp: (bi, 0, hp)),
        out_shape=jax.ShapeDtypeStruct((b, lp, HEADS * MLA_V), jnp.bfloat16),
        scratch_shapes=[pltpu.VMEM((2, tq, LANE), jnp.float32)] * 2
        + [pltpu.VMEM((2, tq // tk, tq, LANE), jnp.float32),
           pltpu.VMEM((2, tq // tk, tq, tk), jnp.bfloat16)],
        compiler_params=pltpu.CompilerParams(
            dimension_semantics=("arbitrary", "arbitrary"), vmem_limit_bytes=VMEM_LIMIT),
        name="flash",
    )(q, k, v)


def _conv_in_kernel(h_ref, gmix_ref, win_ref, cw_ref, y_ref, gs_ref, *, tm):
    i = pl.program_id(1)

    @pl.when(i == 0)
    def _():
        gs_ref[0:8, :] = jnp.zeros((8, D_MODEL), jnp.float32)

    x = h_ref[0]
    hn = _rms(x, gmix_ref[...], D_MODEL).astype(jnp.bfloat16)
    gate_c = _dot(hn, win_ref[0, :, D_MODEL:2 * D_MODEL])
    u = _dot(hn, win_ref[0, :, 2 * D_MODEL:3 * D_MODEL])
    row = lax.broadcasted_iota(jnp.int32, (tm, D_MODEL), 0)
    g = jnp.where((i * tm + row) >= PAD, gate_c * u, 0.0)
    gs_ref[8:tm + 8, :] = g
    y = (cw_ref[0:1, :] * gs_ref[6:tm + 6, :] + cw_ref[1:2, :] * gs_ref[7:tm + 7, :]
         + cw_ref[2:3, :] * g)
    gs_ref[0:8, :] = gs_ref[tm:tm + 8, :]
    gate_b = _dot(hn, win_ref[0, :, 0:D_MODEL])
    y_ref[0] = (gate_b * y).astype(jnp.bfloat16)


def _conv_in(h, gmix, win, layer, cw, tm):
    b, lp, d = h.shape
    kern = functools.partial(_conv_in_kernel, tm=tm)
    return pl.pallas_call(
        kern,
        grid=(b, lp // tm),
        in_specs=[
            pl.BlockSpec((1, tm, d), lambda bi, i: (bi, i, 0)),
            _const_spec((1, d)),
            _layer_spec((d, 3 * d), layer),
            _const_spec((8, d)),
        ],
        out_specs=pl.BlockSpec((1, tm, d), lambda bi, i: (bi, i, 0)),
        out_shape=jax.ShapeDtypeStruct((b, lp, d), jnp.bfloat16),
        scratch_shapes=[pltpu.VMEM((tm + 8, d), jnp.float32)],
        compiler_params=pltpu.CompilerParams(
            dimension_semantics=("arbitrary", "arbitrary"), vmem_limit_bytes=VMEM_LIMIT),
        name="conv_in",
    )(h, gmix, win, cw)


def _mix_out_mlp_kernel(h_ref, y_ref, wo_ref, gmlp_ref, wup_ref, wdn_ref, out_ref):
    h1 = h_ref[...] + _dot(y_ref[...], wo_ref[0])
    n = _rms(h1, gmlp_ref[...], D_MODEL).astype(jnp.bfloat16)
    acc = h1
    for c in range(D_FF // FF_CHUNK):
        sl = slice(c * FF_CHUNK, (c + 1) * FF_CHUNK)
        a = jnp.maximum(_dot(n, wup_ref[0, :, sl]), 0.0)
        acc = acc + _dot((a * a).astype(jnp.bfloat16), wdn_ref[0, sl, :])
    out_ref[...] = acc


def _mix_out_mlp(h, y, wo, wo_layer, gmlp, wup, wdn, layer, tm):
    r, d = h.shape
    return pl.pallas_call(
        _mix_out_mlp_kernel,
        grid=(r // tm,),
        in_specs=[
            pl.BlockSpec((tm, d), lambda i: (i, 0)),
            pl.BlockSpec((tm, d), lambda i: (i, 0)),
            _layer_spec((d, d), wo_layer),
            _const_spec((1, d)),
            _layer_spec((d, D_FF), layer),
            _layer_spec((D_FF, d), layer),
        ],
        out_specs=pl.BlockSpec((tm, d), lambda i: (i, 0)),
        out_shape=jax.ShapeDtypeStruct((r, d), jnp.float32),
        compiler_params=pltpu.CompilerParams(
            dimension_semantics=("arbitrary",), vmem_limit_bytes=VMEM_LIMIT),
        name="mix_out_mlp",
    )(h, y, wo, gmlp, wup, wdn)


def _pad_heads(w, heads, dim):
    k = w.shape[0]
    w = w.reshape(k, heads, dim)
    w = jnp.pad(w, ((0, 0), (0, 0), (0, LANE - dim)))
    return w.reshape(k, heads * LANE)


def _lane_vec(v, offset=0):
    return jnp.zeros((LANE,), jnp.float32).at[offset:offset + v.shape[0]].set(v)


def _attn_params(w_in, g_cq, w_uq, g_ckv, w_ukv, g_q_mla, g_k_mla, g_q_fox, g_k_fox, b_forget):
    bf = jnp.bfloat16
    o1 = Q_LORA
    o2 = o1 + KV_LORA
    o3 = o2 + MLA_ROPE
    o4 = o3 + FOX_HEADS * FOX_DIM
    o5 = o4 + FOX_HEADS * FOX_DIM
    o6 = o5 + FOX_HEADS * FOX_DIM
    misc = jnp.zeros((D_MODEL, LANE), jnp.float32)
    misc = misc.at[:, MISC_GATE:MISC_GATE + FOX_HEADS].set(w_in[:, o6:])
    misc = misc.at[:, MISC_ROPE:MISC_ROPE + MLA_ROPE].set(w_in[:, o2:o3])
    misc_sw = jnp.zeros((D_MODEL, LANE), jnp.float32)
    misc_sw = misc_sw.at[:, MISC_ROPE:MISC_ROPE + HALF_ROPE].set(w_in[:, o2 + HALF_ROPE:o3])
    misc_sw = misc_sw.at[:, MISC_ROPE + HALF_ROPE:MISC_ROPE + MLA_ROPE].set(
        w_in[:, o2:o2 + HALF_ROPE])
    wcat = jnp.concatenate([
        w_in[:, :o1], w_in[:, o1:o2],
        _pad_heads(w_in[:, o3:o4], FOX_HEADS, FOX_DIM),
        _pad_heads(w_in[:, o4:o5], FOX_HEADS, FOX_DIM),
        _pad_heads(w_in[:, o5:o6], FOX_HEADS, FOX_DIM), misc, misc_sw], axis=1).astype(bf)
    kv = w_ukv.reshape(KV_LORA, MLA_HEADS, MLA_NOPE + MLA_V)
    wkn = _pad_heads(kv[:, :, :MLA_NOPE].reshape(KV_LORA, -1), MLA_HEADS, MLA_NOPE).astype(bf)
    wv = _pad_heads(kv[:, :, MLA_NOPE:].reshape(KV_LORA, -1), MLA_HEADS, MLA_V).astype(bf)
    lo, mid, hi = MLA_NOPE, MLA_NOPE + HALF_ROPE, MLA_NOPE + MLA_ROPE
    uq = w_uq.reshape(Q_LORA, MLA_HEADS, MLA_QK)
    uq_sw = jnp.zeros((Q_LORA, MLA_HEADS, LANE), jnp.float32)
    uq_sw = uq_sw.at[:, :, lo:mid].set(uq[:, :, mid:hi]).at[:, :, mid:hi].set(uq[:, :, lo:mid])
    wuq = jnp.concatenate([_pad_heads(w_uq, MLA_HEADS, MLA_QK),
                           uq_sw.reshape(Q_LORA, MLA_HEADS * LANE)], axis=1).astype(bf)

    def swapped(g):
        return jnp.zeros((LANE,), jnp.float32).at[lo:mid].set(g[mid:hi]).at[mid:hi].set(g[lo:mid])

    zero = jnp.zeros((LANE,), jnp.float32)
    q_scale_mla = MLA_QK ** -0.5 * LOG2E
    rows = [zero] * VEC_ROWS
    rows[V_GQ_MLA] = _lane_vec(g_q_mla) * q_scale_mla
    rows[V_GQ_MLA_SW] = swapped(g_q_mla) * q_scale_mla
    rows[V_GK_MLA] = _lane_vec(g_k_mla)
    rows[V_GK_MLA_SW] = swapped(g_k_mla)
    rows[V_GQ_FOX] = _lane_vec(g_q_fox) * (FOX_DIM ** -0.5 * LOG2E)
    rows[V_GK_FOX] = _lane_vec(g_k_fox)
    rows[V_B_FORGET] = _lane_vec(b_forget, MISC_GATE)
    rows[V_ADD_Q_FOX] = (zero.at[GATE_LANE + N_SPLIT:GATE_LANE + 2 * N_SPLIT].set(1.0)
                         .at[FLAG_FOX].set(1.0))
    rows[V_ONES_K_FOX] = zero.at[GATE_LANE:GATE_LANE + N_SPLIT].set(1.0)
    rows[V_ADD_Q_MLA] = zero.at[FLAG_MLA].set(1.0)
    rows[V_ONES_V] = zero.at[ONES_LANE].set(1.0)
    vec = jnp.stack(rows)
    return dict(wcat=wcat, gcq=g_cq[None], wuq=wuq, gckv=g_ckv[None], wkn=wkn, wv=wv, vec=vec)


def _gate_selectors():
    selq = np.zeros((LANE, FOX_HEADS * LANE), np.float32)
    selk = np.zeros((LANE, FOX_HEADS * LANE), np.float32)
    for part in range(N_SPLIT):
        for hd in range(FOX_HEADS):
            selq[part * FOX_HEADS + hd, hd * LANE + GATE_LANE + part] = 1.0
            selk[part * FOX_HEADS + hd, hd * LANE + GATE_LANE + N_SPLIT + part] = -1.0
    return jnp.asarray(selq, jnp.bfloat16), jnp.asarray(selk, jnp.bfloat16)


def _rope_table(lp):
    lane = jnp.arange(LANE, dtype=jnp.int32)
    rotary = (lane >= MLA_NOPE) & (lane < MLA_NOPE + MLA_ROPE)
    first_half = rotary & (lane < MLA_NOPE + HALF_ROPE)
    pair = ((lane - MLA_NOPE) % HALF_ROPE).astype(jnp.float32)
    inv_freq = ROPE_BASE ** (-(2.0 * pair) / MLA_ROPE)
    pos = (jnp.arange(lp, dtype=jnp.int32) - PAD).astype(jnp.float32)
    ang = pos[:, None] * inv_freq[None, :]
    cos_t = jnp.where(lane < MLA_NOPE, 1.0, jnp.where(rotary, jnp.cos(ang), 0.0))
    sin_sw = jnp.where(rotary, jnp.where(first_half, -jnp.sin(ang), jnp.sin(ang)), 0.0)
    return jnp.concatenate([cos_t, sin_sw], axis=1)


def _token_tile(lp):
    if lp % FLASH_TQ:
        raise ValueError(f"padded length {lp} is not a multiple of {FLASH_TQ}")
    return FLASH_TQ


def kernel(x, meta_tokens, g_mix, g_mlp, w_in_attn, g_cq, w_uq, g_ckv, w_ukv, g_q_mla, g_k_mla,
           g_q_fox, g_k_fox, b_forget, w_out_attn, w_in_conv, conv_w, w_out_conv, w_mlp_up,
           w_mlp_down):
    b, seq, d = x.shape
    assert d == D_MODEL and (PAD + N_META + seq) % BLOCK == 0
    lp = PAD + N_META + seq
    tm = _token_tile(lp)
    bf = jnp.bfloat16

    meta = jnp.broadcast_to(meta_tokens.astype(x.dtype)[None], (b, N_META, d))
    h = jnp.concatenate([jnp.zeros((b, PAD, d), x.dtype), meta, x], axis=1)

    rope_tab = _rope_table(lp)
    tri = (jnp.arange(tm)[:, None] >= jnp.arange(tm)[None, :]).astype(bf)
    selq, selk = _gate_selectors()

    wo_attn, wo_conv, w_conv = w_out_attn.astype(bf), w_out_conv.astype(bf), w_in_conv.astype(bf)
    w_up, w_down = w_mlp_up.astype(bf), w_mlp_down.astype(bf)
    for layer in range(DEPTH):
        j = layer // 2
        gmix = g_mix[layer][None]
        if layer % 2 == 0:
            p = _attn_params(w_in_attn[j], g_cq[j], w_uq[j], g_ckv[j], w_ukv[j], g_q_mla[j],
                             g_k_mla[j], g_q_fox[j], g_k_fox[j], b_forget[j])
            q, k, v = _attn_in(h, gmix, p, rope_tab, tri, selq, selk, tm)
            y = _flash(q, k, v)
            wo = wo_attn
        else:
            cw = jnp.zeros((8, d), jnp.float32).at[0:3].set(conv_w[j])
            y = _conv_in(h, gmix, w_conv, j, cw, tm)
            wo = wo_conv
        h = _mix_out_mlp(h.reshape(b * lp, d), y.reshape(b * lp, d), wo, j, g_mlp[layer][None],
                         w_up, w_down, layer, tm).reshape(b, lp, d)
    return h[:, PAD + N_META:]
```

```python
import functools

import numpy as np
import jax
import jax.numpy as jnp
from jax import lax
from jax.experimental import pallas as pl
from jax.experimental.pallas import tpu as pltpu

D_MODEL = 1024
DEPTH = 4
N_META = 16
BLOCK = 128
PAD = 2 * BLOCK - N_META
MLA_HEADS = 8
MLA_NOPE = 64
MLA_ROPE = 32
MLA_QK = MLA_NOPE + MLA_ROPE
MLA_V = 64
Q_LORA = 384
KV_LORA = 256
ROPE_BASE = 10000.0
FOX_HEADS = 8
FOX_DIM = 64
D_FF = 4 * D_MODEL
EPS = 1e-6
NEG = -1e30

LANE = 128
HEADS = MLA_HEADS + FOX_HEADS
HALF_ROPE = MLA_ROPE // 2
GATE_LANE = FOX_DIM
N_SPLIT = 3
FLAG_MLA = MLA_QK
FLAG_FOX = GATE_LANE + 2 * N_SPLIT
PAD_KEY = NEG
LOG2E = 1.4426950408889634
MISC_GATE = 0
MISC_ROPE = MLA_NOPE

OFF_CQ = 0
OFF_CKV = OFF_CQ + Q_LORA
OFF_FQ = OFF_CKV + KV_LORA
OFF_FK = OFF_FQ + FOX_HEADS * FOX_DIM
OFF_FV = OFF_FK + FOX_HEADS * FOX_DIM
OFF_MISC = OFF_FV + FOX_HEADS * FOX_DIM
OFF_MISC_SW = OFF_MISC + LANE
W_CAT = OFF_MISC_SW + LANE

(V_GQ_MLA, V_GQ_MLA_SW, V_GK_MLA, V_GK_MLA_SW, V_GQ_FOX, V_GK_FOX, V_B_FORGET, V_ADD_Q_FOX,
 V_ONES_K_FOX, V_ADD_Q_MLA, V_ONES_V) = range(11)
VEC_ROWS = 16
ONES_LANE = MLA_V
PAIR = 2 * LANE

FF_CHUNK = 1024
FLASH_TQ = 768
FLASH_TK = 384
VMEM_LIMIT = 56 * 1024 * 1024


def _const_spec(shape):
    nd = len(shape)
    return pl.BlockSpec(shape, lambda *_: (0,) * nd, pipeline_mode=pl.Buffered(1))


def _layer_spec(shape, layer):
    nd = len(shape)
    return pl.BlockSpec((1,) + shape, lambda *_: (layer,) + (0,) * nd,
                        pipeline_mode=pl.Buffered(1))


def _rms(x, g, n):
    ms = jnp.sum(x * x, axis=-1, keepdims=True) * (1.0 / n)
    return x * lax.rsqrt(ms + EPS) * g


def _split3(x):
    hi = x.astype(jnp.bfloat16).astype(jnp.float32)
    r1 = x - hi
    mid = r1.astype(jnp.bfloat16).astype(jnp.float32)
    lo = r1 - mid
    packed = hi + pltpu.roll(mid, FOX_HEADS, 1) + pltpu.roll(lo, 2 * FOX_HEADS, 1)
    return packed.astype(jnp.bfloat16)


def _dot(a, b):
    return jnp.dot(a, b, preferred_element_type=jnp.float32)


def _attn_in_kernel(h_ref, gmix_ref, wcat_ref, gcq_ref, wuq_ref, gckv_ref, wkn_ref, wv_ref,
                    vec_ref, rope_ref, tri_ref, selq_ref, selk_ref,
                    q_ref, k_ref, v_ref, carry_ref, *, tm):
    i = pl.program_id(1)

    @pl.when(i == 0)
    def _():
        carry_ref[...] = jnp.zeros_like(carry_ref)

    x = h_ref[0]
    hn = _rms(x, gmix_ref[...], D_MODEL).astype(jnp.bfloat16)

    def seg(lo, width):
        return _dot(hn, wcat_ref[:, lo:lo + width])

    def vec(r):
        return vec_ref[r:r + 1, :]

    cos_t = rope_ref[:, 0:LANE]
    sin_sw = rope_ref[:, LANE:2 * LANE]
    gc_q, gs_q = vec(V_GQ_MLA) * cos_t, vec(V_GQ_MLA_SW) * sin_sw
    gc_k, gs_k = vec(V_GK_MLA) * cos_t, vec(V_GK_MLA_SW) * sin_sw
    gq_fox, gk_fox = vec(V_GQ_FOX), vec(V_GK_FOX)
    add_q_fox, add_q_mla = vec(V_ADD_Q_FOX), vec(V_ADD_Q_MLA)

    lane = lax.broadcasted_iota(jnp.int32, (tm, LANE), 1)
    row = lax.broadcasted_iota(jnp.int32, (tm, LANE), 0)
    valid = (i * tm + row) >= PAD
    pad_key = jnp.where(valid, 0.0, PAD_KEY)
    add_k_mla = jnp.where(lane == FLAG_MLA, pad_key, 0.0)
    add_k_fox = vec(V_ONES_K_FOX) + jnp.where(lane == FLAG_FOX, pad_key, 0.0)

    misc2 = seg(OFF_MISC, PAIR)
    misc = misc2[:, 0:LANE]
    kpe = jnp.where((lane >= MISC_ROPE) & (lane < MISC_ROPE + MLA_ROPE), misc, 0.0)
    k_rot = misc2[:, LANE:PAIR] * gs_k
    xl = misc + vec(V_B_FORGET)
    logf = jnp.minimum(xl, 0.0) - jnp.log1p(jnp.exp(-jnp.abs(xl)))
    logf = jnp.where(valid & (lane >= MISC_GATE) & (lane < MISC_GATE + FOX_HEADS), logf, 0.0)
    cs = _dot(tri_ref[...], _split3(logf))
    cs = (cs + pltpu.roll(cs, LANE - FOX_HEADS, 1)) + pltpu.roll(cs, LANE - 2 * FOX_HEADS, 1)
    cum = jnp.where(lane < FOX_HEADS, cs, 0.0) + carry_ref[0:1, :]
    carry_ref[0:1, :] = cum[tm - 1:tm, :]
    cum3 = _split3(cum * LOG2E)
    gate_q = _dot(cum3, selq_ref[...])
    gate_k = _dot(cum3, selk_ref[...])

    def inv_rms(xv, n):
        return lax.rsqrt(jnp.sum(xv * xv, axis=-1, keepdims=True) * (1.0 / n) + EPS)

    cqn = _rms(seg(OFF_CQ, Q_LORA), gcq_ref[...], Q_LORA).astype(jnp.bfloat16)
    ckvn = _rms(seg(OFF_CKV, KV_LORA), gckv_ref[...], KV_LORA).astype(jnp.bfloat16)
    ones_v = vec(V_ONES_V)
    for g in range(MLA_HEADS // 2):
        cols = slice(g * PAIR, (g + 1) * PAIR)
        cols_sw = slice(MLA_HEADS * LANE + g * PAIR, MLA_HEADS * LANE + (g + 1) * PAIR)
        xq2 = _dot(cqn, wuq_ref[:, cols])
        xq2_sw = _dot(cqn, wuq_ref[:, cols_sw])
        xk2 = _dot(ckvn, wkn_ref[:, cols])
        xv2 = _dot(ckvn, wv_ref[:, cols])
        for e in range(2):
            hd, sl = 2 * g + e, slice(e * LANE, (e + 1) * LANE)
            xq = xq2[:, sl]
            q_ref[0, hd] = ((xq * gc_q + xq2_sw[:, sl] * gs_q) * inv_rms(xq, MLA_QK) + add_q_mla
                            ).astype(jnp.bfloat16)
            xk = xk2[:, sl] + kpe
            k_ref[0, hd] = ((xk * gc_k + k_rot) * inv_rms(xk, MLA_QK) + add_k_mla
                            ).astype(jnp.bfloat16)
            v_ref[0, hd] = (xv2[:, sl] + ones_v).astype(jnp.bfloat16)

    lower_half = lane < FOX_DIM
    for g in range(FOX_HEADS // 4):
        xq4 = seg(OFF_FQ + g * PAIR, PAIR)
        xk4 = seg(OFF_FK + g * PAIR, PAIR)
        xv4 = seg(OFF_FV + g * PAIR, PAIR)
        for e in range(4):
            hd, sl = 4 * g + e, slice((e // 2) * LANE, (e // 2 + 1) * LANE)
            gl = slice(hd * LANE, (hd + 1) * LANE)

            def head(x2):
                xh = x2[:, sl]
                return jnp.where(lower_half, pltpu.roll(xh, FOX_DIM, 1) if e % 2 else xh, 0.0)

            xq, xk = head(xq4), head(xk4)
            q_ref[0, MLA_HEADS + hd] = (xq * gq_fox * inv_rms(xq, FOX_DIM) + gate_q[:, gl]
                                        + add_q_fox).astype(jnp.bfloat16)
            k_ref[0, MLA_HEADS + hd] = (xk * gk_fox * inv_rms(xk, FOX_DIM) + gate_k[:, gl]
                                        + add_k_fox).astype(jnp.bfloat16)
            v_ref[0, MLA_HEADS + hd] = (head(xv4) + ones_v).astype(jnp.bfloat16)


def _attn_in(h, gmix, p, rope_tab, tri, selq, selk, tm):
    b, lp, d = h.shape
    nt = lp // tm
    kern = functools.partial(_attn_in_kernel, tm=tm)
    qk_shape = jax.ShapeDtypeStruct((b, HEADS, lp, LANE), jnp.bfloat16)
    qk_spec = pl.BlockSpec((1, HEADS, tm, LANE), lambda bi, i: (bi, 0, i, 0))
    return pl.pallas_call(
        kern,
        grid=(b, nt),
        in_specs=[
            pl.BlockSpec((1, tm, d), lambda bi, i: (bi, i, 0)),
            _const_spec((1, d)),
            _const_spec((d, W_CAT)),
            _const_spec((1, Q_LORA)),
            _const_spec((Q_LORA, 2 * MLA_HEADS * LANE)),
            _const_spec((1, KV_LORA)),
            _const_spec((KV_LORA, MLA_HEADS * LANE)),
            _const_spec((KV_LORA, MLA_HEADS * LANE)),
            _const_spec((VEC_ROWS, LANE)),
            pl.BlockSpec((tm, 2 * LANE), lambda bi, i: (i, 0)),
            _const_spec((tm, tm)),
            _const_spec((LANE, FOX_HEADS * LANE)),
            _const_spec((LANE, FOX_HEADS * LANE)),
        ],
        out_specs=[qk_spec, qk_spec, qk_spec],
        out_shape=[qk_shape, qk_shape, qk_shape],
        scratch_shapes=[pltpu.VMEM((8, LANE), jnp.float32)],
        compiler_params=pltpu.CompilerParams(
            dimension_semantics=("arbitrary", "arbitrary"), vmem_limit_bytes=VMEM_LIMIT),
        name="attn_in",
    )(h, gmix, p["wcat"], p["gcq"], p["wuq"], p["gckv"], p["wkn"], p["wv"], p["vec"],
      rope_tab, tri, selq, selk)


def _flash_kernel(q_ref, k_ref, v_ref, o_ref, m_ref, acc_ref, al_ref, p_ref, *, tq, tk, nq):
    chunks = tq // tk

    def query_block(qi, carry):
        qbase = qi * tq
        m_ref[...] = jnp.full(m_ref.shape, NEG, jnp.float32)
        acc_ref[...] = jnp.zeros(acc_ref.shape, jnp.float32)
        al_ref[1, chunks - 1] = jnp.zeros((tq, LANE), jnp.float32)
        p_ref[1, chunks - 1] = jnp.zeros((tq, tk), jnp.bfloat16)

        def softmax(j, u, base, diagonal):
            r0 = u * tk if diagonal else 0
            rows = slice(r0, tq)
            start = pl.multiple_of(base + u * tk, tk)
            q_rows = pl.ds(pl.multiple_of(qbase + r0, tk), tq - r0)
            s = lax.dot_general(q_ref[0, j, q_rows, :], k_ref[0, j, pl.ds(start, tk), :],
                                (((1,), (1,)), ((), ())), preferred_element_type=jnp.float32)
            if diagonal:
                row = lax.broadcasted_iota(jnp.int32, (tq - r0, tk), 0)
                col = lax.broadcasted_iota(jnp.int32, (tq - r0, tk), 1)
                s = jnp.where(col <= row, s, NEG)
            m_prev = m_ref[j, rows, :]
            m_next = jnp.maximum(m_prev, jnp.max(s, axis=1, keepdims=True))
            p = jnp.exp2(s - jnp.concatenate([m_next] * (tk // LANE), axis=1))
            al_ref[j, u, rows, :] = jnp.exp2(m_prev - m_next)
            p_ref[j, u, rows, :] = p.astype(jnp.bfloat16)
            m_ref[j, rows, :] = m_next

        def pv(j, u, base, diagonal):
            r0 = u * tk if diagonal else 0
            rows = slice(r0, tq)
            start = pl.multiple_of(base + u * tk, tk)
            acc_ref[j, rows, :] = (acc_ref[j, rows, :] * al_ref[j, u, rows, :]
                                   + _dot(p_ref[j, u, rows, :], v_ref[0, j, pl.ds(start, tk), :]))

        def block(kb, diagonal):
            base = kb * tq
            for u in range(chunks):
                softmax(0, u, base, diagonal)
                if u == 0:
                    pv(1, chunks - 1, jnp.maximum(kb - 1, 0) * tq, False)
                else:
                    pv(1, u - 1, base, diagonal)
                softmax(1, u, base, diagonal)
                pv(0, u, base, diagonal)

        odd = qi & 1

        @pl.when(odd == 1)
        def _():
            block(0, False)

        def body(pair, c):
            kb = odd + 2 * pair
            block(kb, False)
            block(kb + 1, False)
            return c

        lax.fori_loop(0, lax.shift_right_logical(qi, 1), body, 0)
        block(qi, True)
        pv(1, chunks - 1, qbase, True)

        o0 = acc_ref[0]
        o1 = acc_ref[1]
        o0 = o0 / o0[:, ONES_LANE:ONES_LANE + 1]
        o1 = o1 / o1[:, ONES_LANE:ONES_LANE + 1]
        lane = lax.broadcasted_iota(jnp.int32, (tq, LANE), 1)
        o = jnp.where(lane < MLA_V, o0, pltpu.roll(o1, MLA_V, 1))
        o_ref[0, pl.ds(pl.multiple_of(qbase, tq), tq), :] = o.astype(jnp.bfloat16)
        return carry

    lax.fori_loop(0, nq, query_block, 0)


def _flash(q, k, v):
    b, _, lp, _ = q.shape
    tq, tk = FLASH_TQ, FLASH_TK
    kern = functools.partial(_flash_kernel, tq=tq, tk=tk, nq=lp // tq)
    qkv_spec = pl.BlockSpec((1, 2, lp, LANE), lambda bi, hp: (bi, hp, 0, 0))
    return pl.pallas_call(
        kern,
        grid=(b, HEADS // 2),
        in_specs=[qkv_spec, qkv_spec, qkv_spec],
        out_specs=pl.BlockSpec((1, lp, LANE), lambda bi, hp: (bi, 0, hp)),
        out_shape=jax.ShapeDtypeStruct((b, lp, HEADS * MLA_V), jnp.bfloat16),
        scratch_shapes=[pltpu.VMEM((2, tq, LANE), jnp.float32)] * 2
        + [pltpu.VMEM((2, tq // tk, tq, LANE), jnp.float32),
           pltpu.VMEM((2, tq // tk, tq, tk), jnp.bfloat16)],
        compiler_params=pltpu.CompilerParams(
            dimension_semantics=("arbitrary", "arbitrary"), vmem_limit_bytes=VMEM_LIMIT),
        name="flash",
    )(q, k, v)


def _conv_in_kernel(h_ref, gmix_ref, win_ref, cw_ref, y_ref, gs_ref, *, tm):
    i = pl.program_id(1)

    @pl.when(i == 0)
    def _():
        gs_ref[0:8, :] = jnp.zeros((8, D_MODEL), jnp.float32)

    x = h_ref[0]
    hn = _rms(x, gmix_ref[...], D_MODEL).astype(jnp.bfloat16)
    gate_c = _dot(hn, win_ref[0, :, D_MODEL:2 * D_MODEL])
    u = _dot(hn, win_ref[0, :, 2 * D_MODEL:3 * D_MODEL])
    row = lax.broadcasted_iota(jnp.int32, (tm, D_MODEL), 0)
    g = jnp.where((i * tm + row) >= PAD, gate_c * u, 0.0)
    gs_ref[8:tm + 8, :] = g
    y = (cw_ref[0:1, :] * gs_ref[6:tm + 6, :] + cw_ref[1:2, :] * gs_ref[7:tm + 7, :]
         + cw_ref[2:3, :] * g)
    gs_ref[0:8, :] = gs_ref[tm:tm + 8, :]
    gate_b = _dot(hn, win_ref[0, :, 0:D_MODEL])
    y_ref[0] = (gate_b * y).astype(jnp.bfloat16)


def _conv_in(h, gmix, win, layer, cw, tm):
    b, lp, d = h.shape
    kern = functools.partial(_conv_in_kernel, tm=tm)
    return pl.pallas_call(
        kern,
        grid=(b, lp // tm),
        in_specs=[
            pl.BlockSpec((1, tm, d), lambda bi, i: (bi, i, 0)),
            _const_spec((1, d)),
            _layer_spec((d, 3 * d), layer),
            _const_spec((8, d)),
        ],
        out_specs=pl.BlockSpec((1, tm, d), lambda bi, i: (bi, i, 0)),
        out_shape=jax.ShapeDtypeStruct((b, lp, d), jnp.bfloat16),
        scratch_shapes=[pltpu.VMEM((tm + 8, d), jnp.float32)],
        compiler_params=pltpu.CompilerParams(
            dimension_semantics=("arbitrary", "arbitrary"), vmem_limit_bytes=VMEM_LIMIT),
        name="conv_in",
    )(h, gmix, win, cw)


def _mix_out_mlp_kernel(h_ref, y_ref, wo_ref, gmlp_ref, wup_ref, wdn_ref, out_ref):
    h1 = h_ref[...] + _dot(y_ref[...], wo_ref[0])
    n = _rms(h1, gmlp_ref[...], D_MODEL).astype(jnp.bfloat16)
    acc = h1
    for c in range(D_FF // FF_CHUNK):
        sl = slice(c * FF_CHUNK, (c + 1) * FF_CHUNK)
        a = jnp.maximum(_dot(n, wup_ref[0, :, sl]), 0.0)
        acc = acc + _dot((a * a).astype(jnp.bfloat16), wdn_ref[0, sl, :])
    out_ref[...] = acc


def _mix_out_mlp(h, y, wo, wo_layer, gmlp, wup, wdn, layer, tm):
    r, d = h.shape
    return pl.pallas_call(
        _mix_out_mlp_kernel,
        grid=(r // tm,),
        in_specs=[
            pl.BlockSpec((tm, d), lambda i: (i, 0)),
            pl.BlockSpec((tm, d), lambda i: (i, 0)),
            _layer_spec((d, d), wo_layer),
            _const_spec((1, d)),
            _layer_spec((d, D_FF), layer),
            _layer_spec((D_FF, d), layer),
        ],
        out_specs=pl.BlockSpec((tm, d), lambda i: (i, 0)),
        out_shape=jax.ShapeDtypeStruct((r, d), jnp.float32),
        compiler_params=pltpu.CompilerParams(
            dimension_semantics=("arbitrary",), vmem_limit_bytes=VMEM_LIMIT),
        name="mix_out_mlp",
    )(h, y, wo, gmlp, wup, wdn)


def _pad_heads(w, heads, dim):
    k = w.shape[0]
    w = w.reshape(k, heads, dim)
    w = jnp.pad(w, ((0, 0), (0, 0), (0, LANE - dim)))
    return w.reshape(k, heads * LANE)


def _lane_vec(v, offset=0):
    return jnp.zeros((LANE,), jnp.float32).at[offset:offset + v.shape[0]].set(v)


def _attn_params(w_in, g_cq, w_uq, g_ckv, w_ukv, g_q_mla, g_k_mla, g_q_fox, g_k_fox, b_forget):
    bf = jnp.bfloat16
    o1 = Q_LORA
    o2 = o1 + KV_LORA
    o3 = o2 + MLA_ROPE
    o4 = o3 + FOX_HEADS * FOX_DIM
    o5 = o4 + FOX_HEADS * FOX_DIM
    o6 = o5 + FOX_HEADS * FOX_DIM
    misc = jnp.zeros((D_MODEL, LANE), jnp.float32)
    misc = misc.at[:, MISC_GATE:MISC_GATE + FOX_HEADS].set(w_in[:, o6:])
    misc = misc.at[:, MISC_ROPE:MISC_ROPE + MLA_ROPE].set(w_in[:, o2:o3])
    misc_sw = jnp.zeros((D_MODEL, LANE), jnp.float32)
    misc_sw = misc_sw.at[:, MISC_ROPE:MISC_ROPE + HALF_ROPE].set(w_in[:, o2 + HALF_ROPE:o3])
    misc_sw = misc_sw.at[:, MISC_ROPE + HALF_ROPE:MISC_ROPE + MLA_ROPE].set(
        w_in[:, o2:o2 + HALF_ROPE])
    wcat = jnp.concatenate([
        w_in[:, :o1], w_in[:, o1:o2],
        w_in[:, o3:o6], misc, misc_sw], axis=1).astype(bf)
    kv = w_ukv.reshape(KV_LORA, MLA_HEADS, MLA_NOPE + MLA_V)
    wkn = _pad_heads(kv[:, :, :MLA_NOPE].reshape(KV_LORA, -1), MLA_HEADS, MLA_NOPE).astype(bf)
    wv = _pad_heads(kv[:, :, MLA_NOPE:].reshape(KV_LORA, -1), MLA_HEADS, MLA_V).astype(bf)
    lo, mid, hi = MLA_NOPE, MLA_NOPE + HALF_ROPE, MLA_NOPE + MLA_ROPE
    uq = w_uq.reshape(Q_LORA, MLA_HEADS, MLA_QK)
    uq_sw = jnp.zeros((Q_LORA, MLA_HEADS, LANE), jnp.float32)
    uq_sw = uq_sw.at[:, :, lo:mid].set(uq[:, :, mid:hi]).at[:, :, mid:hi].set(uq[:, :, lo:mid])
    wuq = jnp.concatenate([_pad_heads(w_uq, MLA_HEADS, MLA_QK),
                           uq_sw.reshape(Q_LORA, MLA_HEADS * LANE)], axis=1).astype(bf)

    def swapped(g):
        return jnp.zeros((LANE,), jnp.float32).at[lo:mid].set(g[mid:hi]).at[mid:hi].set(g[lo:mid])

    zero = jnp.zeros((LANE,), jnp.float32)
    q_scale_mla = MLA_QK ** -0.5 * LOG2E
    rows = [zero] * VEC_ROWS
    rows[V_GQ_MLA] = _lane_vec(g_q_mla) * q_scale_mla
    rows[V_GQ_MLA_SW] = swapped(g_q_mla) * q_scale_mla
    rows[V_GK_MLA] = _lane_vec(g_k_mla)
    rows[V_GK_MLA_SW] = swapped(g_k_mla)
    rows[V_GQ_FOX] = _lane_vec(g_q_fox) * (FOX_DIM ** -0.5 * LOG2E)
    rows[V_GK_FOX] = _lane_vec(g_k_fox)
    rows[V_B_FORGET] = _lane_vec(b_forget, MISC_GATE)
    rows[V_ADD_Q_FOX] = (zero.at[GATE_LANE + N_SPLIT:GATE_LANE + 2 * N_SPLIT].set(1.0)
                         .at[FLAG_FOX].set(1.0))
    rows[V_ONES_K_FOX] = zero.at[GATE_LANE:GATE_LANE + N_SPLIT].set(1.0)
    rows[V_ADD_Q_MLA] = zero.at[FLAG_MLA].set(1.0)
    rows[V_ONES_V] = zero.at[ONES_LANE].set(1.0)
    vec = jnp.stack(rows)
    return dict(wcat=wcat, gcq=g_cq[None], wuq=wuq, gckv=g_ckv[None], wkn=wkn, wv=wv, vec=vec)


def _gate_selectors():
    selq = np.zeros((LANE, FOX_HEADS * LANE), np.float32)
    selk = np.zeros((LANE, FOX_HEADS * LANE), np.float32)
    for part in range(N_SPLIT):
        for hd in range(FOX_HEADS):
            selq[part * FOX_HEADS + hd, hd * LANE + GATE_LANE + part] = 1.0
            selk[part * FOX_HEADS + hd, hd * LANE + GATE_LANE + N_SPLIT + part] = -1.0
    return jnp.asarray(selq, jnp.bfloat16), jnp.asarray(selk, jnp.bfloat16)


def _rope_table(lp):
    lane = jnp.arange(LANE, dtype=jnp.int32)
    rotary = (lane >= MLA_NOPE) & (lane < MLA_NOPE + MLA_ROPE)
    first_half = rotary & (lane < MLA_NOPE + HALF_ROPE)
    pair = ((lane - MLA_NOPE) % HALF_ROPE).astype(jnp.float32)
    inv_freq = ROPE_BASE ** (-(2.0 * pair) / MLA_ROPE)
    pos = (jnp.arange(lp, dtype=jnp.int32) - PAD).astype(jnp.float32)
    ang = pos[:, None] * inv_freq[None, :]
    cos_t = jnp.where(lane < MLA_NOPE, 1.0, jnp.where(rotary, jnp.cos(ang), 0.0))
    sin_sw = jnp.where(rotary, jnp.where(first_half, -jnp.sin(ang), jnp.sin(ang)), 0.0)
    return jnp.concatenate([cos_t, sin_sw], axis=1)


def _token_tile(lp):
    if lp % FLASH_TQ:
        raise ValueError(f"padded length {lp} is not a multiple of {FLASH_TQ}")
    return FLASH_TQ


def kernel(x, meta_tokens, g_mix, g_mlp, w_in_attn, g_cq, w_uq, g_ckv, w_ukv, g_q_mla, g_k_mla,
           g_q_fox, g_k_fox, b_forget, w_out_attn, w_in_conv, conv_w, w_out_conv, w_mlp_up,
           w_mlp_down):
    b, seq, d = x.shape
    assert d == D_MODEL and (PAD + N_META + seq) % BLOCK == 0
    lp = PAD + N_META + seq
    tm = _token_tile(lp)
    bf = jnp.bfloat16

    meta = jnp.broadcast_to(meta_tokens.astype(x.dtype)[None], (b, N_META, d))
    h = jnp.concatenate([jnp.zeros((b, PAD, d), x.dtype), meta, x], axis=1)

    rope_tab = _rope_table(lp)
    tri = (jnp.arange(tm)[:, None] >= jnp.arange(tm)[None, :]).astype(bf)
    selq, selk = _gate_selectors()

    wo_attn, wo_conv, w_conv = w_out_attn.astype(bf), w_out_conv.astype(bf), w_in_conv.astype(bf)
    w_up, w_down = w_mlp_up.astype(bf), w_mlp_down.astype(bf)
    for layer in range(DEPTH):
        j = layer // 2
        gmix = g_mix[layer][None]
        if layer % 2 == 0:
            p = _attn_params(w_in_attn[j], g_cq[j], w_uq[j], g_ckv[j], w_ukv[j], g_q_mla[j],
                             g_k_mla[j], g_q_fox[j], g_k_fox[j], b_forget[j])
            q, k, v = _attn_in(h, gmix, p, rope_tab, tri, selq, selk, tm)
            y = _flash(q, k, v)
            wo = wo_attn
        else:
            cw = jnp.zeros((8, d), jnp.float32).at[0:3].set(conv_w[j])
            y = _conv_in(h, gmix, w_conv, j, cw, tm)
            wo = wo_conv
        h = _mix_out_mlp(h.reshape(b * lp, d), y.reshape(b * lp, d), wo, j, g_mlp[layer][None],
                         w_up, w_down, layer, tm).reshape(b, lp, d)
    return h[:, PAD + N_META:]
```

```python
import functools

import numpy as np
import jax
import jax.numpy as jnp
from jax import lax
from jax.experimental import pallas as pl
from jax.experimental.pallas import tpu as pltpu

D_MODEL = 1024
DEPTH = 4
N_META = 16
BLOCK = 128
PAD = 2 * BLOCK - N_META
MLA_HEADS = 8
MLA_NOPE = 64
MLA_ROPE = 32
MLA_QK = MLA_NOPE + MLA_ROPE
MLA_V = 64
Q_LORA = 384
KV_LORA = 256
ROPE_BASE = 10000.0
FOX_HEADS = 8
FOX_DIM = 64
D_FF = 4 * D_MODEL
EPS = 1e-6
NEG = -1e30

LANE = 128
HEADS = MLA_HEADS + FOX_HEADS
HALF_ROPE = MLA_ROPE // 2
FEATURE_BASE = (0, FOX_DIM)
EXTRA_BASE = (FOX_DIM, 0)
N_SPLIT = 3
FLAG_FOX_OFF = 2 * N_SPLIT
FLAG_MLA = MLA_QK
PAD_KEY = NEG
LOG2E = 1.4426950408889634
MISC_GATE = 0
MISC_ROPE = MLA_NOPE

OFF_CQ = 0
OFF_CKV = OFF_CQ + Q_LORA
OFF_FQ = OFF_CKV + KV_LORA
OFF_FK = OFF_FQ + FOX_HEADS * FOX_DIM
OFF_FV = OFF_FK + FOX_HEADS * FOX_DIM
OFF_MISC = OFF_FV + FOX_HEADS * FOX_DIM
OFF_MISC_SW = OFF_MISC + LANE
W_CAT = OFF_MISC_SW + LANE

(V_GQ_MLA, V_GQ_MLA_SW, V_GK_MLA, V_GK_MLA_SW, V_ADD_Q_MLA, V_B_FORGET) = range(6)
V_GQ_FOX, V_GK_FOX, V_ADD_Q_FOX, V_ONES_K_FOX, V_ONES_V = 6, 8, 10, 12, 14
VEC_ROWS = 16
PAIR = 2 * LANE

FF_CHUNK = 1024
FLASH_TQ = 768
FLASH_TK = 384
VMEM_LIMIT = 56 * 1024 * 1024


def _const_spec(shape):
    nd = len(shape)
    return pl.BlockSpec(shape, lambda *_: (0,) * nd, pipeline_mode=pl.Buffered(1))


def _layer_spec(shape, layer):
    nd = len(shape)
    return pl.BlockSpec((1,) + shape, lambda *_: (layer,) + (0,) * nd,
                        pipeline_mode=pl.Buffered(1))


def _rms(x, g, n):
    ms = jnp.sum(x * x, axis=-1, keepdims=True) * (1.0 / n)
    return x * lax.rsqrt(ms + EPS) * g


def _split3(x):
    hi = x.astype(jnp.bfloat16).astype(jnp.float32)
    r1 = x - hi
    mid = r1.astype(jnp.bfloat16).astype(jnp.float32)
    lo = r1 - mid
    packed = hi + pltpu.roll(mid, FOX_HEADS, 1) + pltpu.roll(lo, 2 * FOX_HEADS, 1)
    return packed.astype(jnp.bfloat16)


def _dot(a, b):
    return jnp.dot(a, b, preferred_element_type=jnp.float32)


def _attn_in_kernel(h_ref, gmix_ref, wcat_ref, gcq_ref, wuq_ref, gckv_ref, wkn_ref, wv_ref,
                    vec_ref, rope_ref, tri_ref, selq_ref, selk_ref,
                    q_ref, k_ref, v_ref, carry_ref, *, tm):
    i = pl.program_id(1)

    @pl.when(i == 0)
    def _():
        carry_ref[...] = jnp.zeros_like(carry_ref)

    x = h_ref[0]
    hn = _rms(x, gmix_ref[...], D_MODEL).astype(jnp.bfloat16)

    def seg(lo, width):
        return _dot(hn, wcat_ref[:, lo:lo + width])

    def vec(r):
        return vec_ref[r:r + 1, :]

    cos_t = rope_ref[:, 0:LANE]
    sin_sw = rope_ref[:, LANE:2 * LANE]
    gc_q, gs_q = vec(V_GQ_MLA) * cos_t, vec(V_GQ_MLA_SW) * sin_sw
    gc_k, gs_k = vec(V_GK_MLA) * cos_t, vec(V_GK_MLA_SW) * sin_sw
    add_q_mla = vec(V_ADD_Q_MLA)

    lane = lax.broadcasted_iota(jnp.int32, (tm, LANE), 1)
    row = lax.broadcasted_iota(jnp.int32, (tm, LANE), 0)
    valid = (i * tm + row) >= PAD
    pad_key = jnp.where(valid, 0.0, PAD_KEY)
    add_k_mla = jnp.where(lane == FLAG_MLA, pad_key, 0.0)
    halves = (lane < FOX_DIM, lane >= FOX_DIM)
    add_k_fox = [vec(V_ONES_K_FOX + par)
                 + jnp.where(lane == EXTRA_BASE[par] + FLAG_FOX_OFF, pad_key, 0.0)
                 for par in range(2)]

    misc2 = seg(OFF_MISC, PAIR)
    misc = misc2[:, 0:LANE]
    kpe = jnp.where((lane >= MISC_ROPE) & (lane < MISC_ROPE + MLA_ROPE), misc, 0.0)
    k_rot = misc2[:, LANE:PAIR] * gs_k
    xl = misc + vec(V_B_FORGET)
    logf = jnp.minimum(xl, 0.0) - jnp.log1p(jnp.exp(-jnp.abs(xl)))
    logf = jnp.where(valid & (lane >= MISC_GATE) & (lane < MISC_GATE + FOX_HEADS), logf, 0.0)
    cs = _dot(tri_ref[...], _split3(logf))
    cs = (cs + pltpu.roll(cs, LANE - FOX_HEADS, 1)) + pltpu.roll(cs, LANE - 2 * FOX_HEADS, 1)
    cum = jnp.where(lane < FOX_HEADS, cs, 0.0) + carry_ref[0:1, :]
    carry_ref[0:1, :] = cum[tm - 1:tm, :]
    cum3 = _split3(cum * LOG2E)
    gate_q = _dot(cum3, selq_ref[...])
    gate_k = _dot(cum3, selk_ref[...])

    def inv_rms(xv, n):
        return lax.rsqrt(jnp.sum(xv * xv, axis=-1, keepdims=True) * (1.0 / n) + EPS)

    cqn = _rms(seg(OFF_CQ, Q_LORA), gcq_ref[...], Q_LORA).astype(jnp.bfloat16)
    ckvn = _rms(seg(OFF_CKV, KV_LORA), gckv_ref[...], KV_LORA).astype(jnp.bfloat16)
    for g in range(MLA_HEADS // 2):
        cols = slice(g * PAIR, (g + 1) * PAIR)
        cols_sw = slice(MLA_HEADS * LANE + g * PAIR, MLA_HEADS * LANE + (g + 1) * PAIR)
        xq2 = _dot(cqn, wuq_ref[:, cols])
        xq2_sw = _dot(cqn, wuq_ref[:, cols_sw])
        xk2 = _dot(ckvn, wkn_ref[:, cols])
        xv2 = _dot(ckvn, wv_ref[:, cols])
        for e in range(2):
            hd, sl = 2 * g + e, slice(e * LANE, (e + 1) * LANE)
            xq = xq2[:, sl]
            q_ref[0, hd] = ((xq * gc_q + xq2_sw[:, sl] * gs_q) * inv_rms(xq, MLA_QK) + add_q_mla
                            ).astype(jnp.bfloat16)
            xk = xk2[:, sl] + kpe
            k_ref[0, hd] = ((xk * gc_k + k_rot) * inv_rms(xk, MLA_QK) + add_k_mla
                            ).astype(jnp.bfloat16)
            v_ref[0, hd] = (xv2[:, sl] + vec(V_ONES_V + e)).astype(jnp.bfloat16)

    for g in range(FOX_HEADS // 4):
        xq4 = seg(OFF_FQ + g * PAIR, PAIR)
        xk4 = seg(OFF_FK + g * PAIR, PAIR)
        xv4 = seg(OFF_FV + g * PAIR, PAIR)
        for e in range(4):
            hd, par = 4 * g + e, e % 2
            sl = slice((e // 2) * LANE, (e // 2 + 1) * LANE)
            gl = slice(hd * LANE, (hd + 1) * LANE)
            xq, xk = xq4[:, sl], xk4[:, sl]
            rq = inv_rms(jnp.where(halves[par], xq, 0.0), FOX_DIM)
            rk = inv_rms(jnp.where(halves[par], xk, 0.0), FOX_DIM)
            q_ref[0, MLA_HEADS + hd] = (xq * vec(V_GQ_FOX + par) * rq + gate_q[:, gl]
                                        + vec(V_ADD_Q_FOX + par)).astype(jnp.bfloat16)
            k_ref[0, MLA_HEADS + hd] = (xk * vec(V_GK_FOX + par) * rk + gate_k[:, gl]
                                        + add_k_fox[par]).astype(jnp.bfloat16)
            v_ref[0, MLA_HEADS + hd] = (jnp.where(halves[par], xv4[:, sl], 0.0)
                                        + vec(V_ONES_V + par)).astype(jnp.bfloat16)


def _attn_in(h, gmix, p, rope_tab, tri, selq, selk, tm):
    b, lp, d = h.shape
    nt = lp // tm
    kern = functools.partial(_attn_in_kernel, tm=tm)
    qk_shape = jax.ShapeDtypeStruct((b, HEADS, lp, LANE), jnp.bfloat16)
    qk_spec = pl.BlockSpec((1, HEADS, tm, LANE), lambda bi, i: (bi, 0, i, 0))
    return pl.pallas_call(
        kern,
        grid=(b, nt),
        in_specs=[
            pl.BlockSpec((1, tm, d), lambda bi, i: (bi, i, 0)),
            _const_spec((1, d)),
            _const_spec((d, W_CAT)),
            _const_spec((1, Q_LORA)),
            _const_spec((Q_LORA, 2 * MLA_HEADS * LANE)),
            _const_spec((1, KV_LORA)),
            _const_spec((KV_LORA, MLA_HEADS * LANE)),
            _const_spec((KV_LORA, MLA_HEADS * LANE)),
            _const_spec((VEC_ROWS, LANE)),
            pl.BlockSpec((tm, 2 * LANE), lambda bi, i: (i, 0)),
            _const_spec((tm, tm)),
            _const_spec((LANE, FOX_HEADS * LANE)),
            _const_spec((LANE, FOX_HEADS * LANE)),
        ],
        out_specs=[qk_spec, qk_spec, qk_spec],
        out_shape=[qk_shape, qk_shape, qk_shape],
        scratch_shapes=[pltpu.VMEM((8, LANE), jnp.float32)],
        compiler_params=pltpu.CompilerParams(
            dimension_semantics=("arbitrary", "arbitrary"), vmem_limit_bytes=VMEM_LIMIT),
        name="attn_in",
    )(h, gmix, p["wcat"], p["gcq"], p["wuq"], p["gckv"], p["wkn"], p["wv"], p["vec"],
      rope_tab, tri, selq, selk)


def _flash_kernel(q_ref, k_ref, v_ref, o_ref, m_ref, acc_ref, al_ref, p_ref, *, tq, tk, nq):
    chunks = tq // tk

    def query_block(qi, carry):
        qbase = qi * tq
        m_ref[...] = jnp.full(m_ref.shape, NEG, jnp.float32)
        acc_ref[...] = jnp.zeros(acc_ref.shape, jnp.float32)
        al_ref[1, chunks - 1] = jnp.zeros((tq, LANE), jnp.float32)
        p_ref[1, chunks - 1] = jnp.zeros((tq, tk), jnp.bfloat16)

        def softmax(j, u, base, diagonal):
            r0 = u * tk if diagonal else 0
            rows = slice(r0, tq)
            start = pl.multiple_of(base + u * tk, tk)
            q_rows = pl.ds(pl.multiple_of(qbase + r0, tk), tq - r0)
            s = lax.dot_general(q_ref[0, j, q_rows, :], k_ref[0, j, pl.ds(start, tk), :],
                                (((1,), (1,)), ((), ())), preferred_element_type=jnp.float32)
            if diagonal:
                row = lax.broadcasted_iota(jnp.int32, (tq - r0, tk), 0)
                col = lax.broadcasted_iota(jnp.int32, (tq - r0, tk), 1)
                s = jnp.where(col <= row, s, NEG)
            m_prev = m_ref[j, rows, :]
            m_next = jnp.maximum(m_prev, jnp.max(s, axis=1, keepdims=True))
            p = jnp.exp2(s - jnp.concatenate([m_next] * (tk // LANE), axis=1))
            al_ref[j, u, rows, :] = jnp.exp2(m_prev - m_next)
            p_ref[j, u, rows, :] = p.astype(jnp.bfloat16)
            m_ref[j, rows, :] = m_next

        def pv(j, u, base, diagonal):
            r0 = u * tk if diagonal else 0
            rows = slice(r0, tq)
            start = pl.multiple_of(base + u * tk, tk)
            acc_ref[j, rows, :] = (acc_ref[j, rows, :] * al_ref[j, u, rows, :]
                                   + _dot(p_ref[j, u, rows, :], v_ref[0, j, pl.ds(start, tk), :]))

        def block(kb, diagonal):
            base = kb * tq
            for u in range(chunks):
                softmax(0, u, base, diagonal)
                if u == 0:
                    pv(1, chunks - 1, jnp.maximum(kb - 1, 0) * tq, False)
                else:
                    pv(1, u - 1, base, diagonal)
                softmax(1, u, base, diagonal)
                pv(0, u, base, diagonal)

        odd = qi & 1

        @pl.when(odd == 1)
        def _():
            block(0, False)

        def body(pair, c):
            kb = odd + 2 * pair
            block(kb, False)
            block(kb + 1, False)
            return c

        lax.fori_loop(0, lax.shift_right_logical(qi, 1), body, 0)
        block(qi, True)
        pv(1, chunks - 1, qbase, True)

        o0 = acc_ref[0]
        o1 = acc_ref[1]
        o0 = o0 / o0[:, EXTRA_BASE[0]:EXTRA_BASE[0] + 1]
        o1 = o1 / o1[:, EXTRA_BASE[1]:EXTRA_BASE[1] + 1]
        lane = lax.broadcasted_iota(jnp.int32, (tq, LANE), 1)
        o = jnp.where(lane < MLA_V, o0, o1)
        o_ref[0, pl.ds(pl.multiple_of(qbase, tq), tq), :] = o.astype(jnp.bfloat16)
        return carry

    lax.fori_loop(0, nq, query_block, 0)


def _flash(q, k, v):
    b, _, lp, _ = q.shape
    tq, tk = FLASH_TQ, FLASH_TK
    kern = functools.partial(_flash_kernel, tq=tq, tk=tk, nq=lp // tq)
    qkv_spec = pl.BlockSpec((1, 2, lp, LANE), lambda bi, hp: (bi, hp, 0, 0))
    return pl.pallas_call(
        kern,
        grid=(b, HEADS // 2),
        in_specs=[qkv_spec, qkv_spec, qkv_spec],
        out_specs=pl.BlockSpec((1, lp, LANE), lambda bi, hp: (bi, 0, hp)),
        out_shape=jax.ShapeDtypeStruct((b, lp, HEADS * MLA_V), jnp.bfloat16),
        scratch_shapes=[pltpu.VMEM((2, tq, LANE), jnp.float32)] * 2
        + [pltpu.VMEM((2, tq // tk, tq, LANE), jnp.float32),
           pltpu.VMEM((2, tq // tk, tq, tk), jnp.bfloat16)],
        compiler_params=pltpu.CompilerParams(
            dimension_semantics=("arbitrary", "arbitrary"), vmem_limit_bytes=VMEM_LIMIT),
        name="flash",
    )(q, k, v)


def _conv_in_kernel(h_ref, gmix_ref, win_ref, cw_ref, y_ref, gs_ref, *, tm):
    i = pl.program_id(1)

    @pl.when(i == 0)
    def _():
        gs_ref[0:8, :] = jnp.zeros((8, D_MODEL), jnp.float32)

    x = h_ref[0]
    hn = _rms(x, gmix_ref[...], D_MODEL).astype(jnp.bfloat16)
    gate_c = _dot(hn, win_ref[0, :, D_MODEL:2 * D_MODEL])
    u = _dot(hn, win_ref[0, :, 2 * D_MODEL:3 * D_MODEL])
    row = lax.broadcasted_iota(jnp.int32, (tm, D_MODEL), 0)
    g = jnp.where((i * tm + row) >= PAD, gate_c * u, 0.0)
    gs_ref[8:tm + 8, :] = g
    y = (cw_ref[0:1, :] * gs_ref[6:tm + 6, :] + cw_ref[1:2, :] * gs_ref[7:tm + 7, :]
         + cw_ref[2:3, :] * g)
    gs_ref[0:8, :] = gs_ref[tm:tm + 8, :]
    gate_b = _dot(hn, win_ref[0, :, 0:D_MODEL])
    y_ref[0] = (gate_b * y).astype(jnp.bfloat16)


def _conv_in(h, gmix, win, layer, cw, tm):
    b, lp, d = h.shape
    kern = functools.partial(_conv_in_kernel, tm=tm)
    return pl.pallas_call(
        kern,
        grid=(b, lp // tm),
        in_specs=[
            pl.BlockSpec((1, tm, d), lambda bi, i: (bi, i, 0)),
            _const_spec((1, d)),
            _layer_spec((d, 3 * d), layer),
            _const_spec((8, d)),
        ],
        out_specs=pl.BlockSpec((1, tm, d), lambda bi, i: (bi, i, 0)),
        out_shape=jax.ShapeDtypeStruct((b, lp, d), jnp.bfloat16),
        scratch_shapes=[pltpu.VMEM((tm + 8, d), jnp.float32)],
        compiler_params=pltpu.CompilerParams(
            dimension_semantics=("arbitrary", "arbitrary"), vmem_limit_bytes=VMEM_LIMIT),
        name="conv_in",
    )(h, gmix, win, cw)


def _mix_out_mlp_kernel(h_ref, y_ref, wo_ref, gmlp_ref, wup_ref, wdn_ref, out_ref):
    h1 = h_ref[...] + _dot(y_ref[...], wo_ref[0])
    n = _rms(h1, gmlp_ref[...], D_MODEL).astype(jnp.bfloat16)
    acc = h1
    for c in range(D_FF // FF_CHUNK):
        sl = slice(c * FF_CHUNK, (c + 1) * FF_CHUNK)
        a = jnp.maximum(_dot(n, wup_ref[0, :, sl]), 0.0)
        acc = acc + _dot((a * a).astype(jnp.bfloat16), wdn_ref[0, sl, :])
    out_ref[...] = acc


def _mix_out_mlp(h, y, wo, wo_layer, gmlp, wup, wdn, layer, tm):
    r, d = h.shape
    return pl.pallas_call(
        _mix_out_mlp_kernel,
        grid=(r // tm,),
        in_specs=[
            pl.BlockSpec((tm, d), lambda i: (i, 0)),
            pl.BlockSpec((tm, d), lambda i: (i, 0)),
            _layer_spec((d, d), wo_layer),
            _const_spec((1, d)),
            _layer_spec((d, D_FF), layer),
            _layer_spec((D_FF, d), layer),
        ],
        out_specs=pl.BlockSpec((tm, d), lambda i: (i, 0)),
        out_shape=jax.ShapeDtypeStruct((r, d), jnp.float32),
        compiler_params=pltpu.CompilerParams(
            dimension_semantics=("arbitrary",), vmem_limit_bytes=VMEM_LIMIT),
        name="mix_out_mlp",
    )(h, y, wo, gmlp, wup, wdn)


def _pad_heads(w, heads, dim):
    k = w.shape[0]
    w = w.reshape(k, heads, dim)
    w = jnp.pad(w, ((0, 0), (0, 0), (0, LANE - dim)))
    return w.reshape(k, heads * LANE)


def _lane_vec(v, offset=0):
    return jnp.zeros((LANE,), jnp.float32).at[offset:offset + v.shape[0]].set(v)


def _attn_params(w_in, g_cq, w_uq, g_ckv, w_ukv, g_q_mla, g_k_mla, g_q_fox, g_k_fox, b_forget):
    bf = jnp.bfloat16
    o1 = Q_LORA
    o2 = o1 + KV_LORA
    o3 = o2 + MLA_ROPE
    o4 = o3 + FOX_HEADS * FOX_DIM
    o5 = o4 + FOX_HEADS * FOX_DIM
    o6 = o5 + FOX_HEADS * FOX_DIM
    misc = jnp.zeros((D_MODEL, LANE), jnp.float32)
    misc = misc.at[:, MISC_GATE:MISC_GATE + FOX_HEADS].set(w_in[:, o6:])
    misc = misc.at[:, MISC_ROPE:MISC_ROPE + MLA_ROPE].set(w_in[:, o2:o3])
    misc_sw = jnp.zeros((D_MODEL, LANE), jnp.float32)
    misc_sw = misc_sw.at[:, MISC_ROPE:MISC_ROPE + HALF_ROPE].set(w_in[:, o2 + HALF_ROPE:o3])
    misc_sw = misc_sw.at[:, MISC_ROPE + HALF_ROPE:MISC_ROPE + MLA_ROPE].set(
        w_in[:, o2:o2 + HALF_ROPE])
    wcat = jnp.concatenate([
        w_in[:, :o1], w_in[:, o1:o2],
        w_in[:, o3:o6], misc, misc_sw], axis=1).astype(bf)
    kv = w_ukv.reshape(KV_LORA, MLA_HEADS, MLA_NOPE + MLA_V)
    wkn = _pad_heads(kv[:, :, :MLA_NOPE].reshape(KV_LORA, -1), MLA_HEADS, MLA_NOPE).astype(bf)
    wv = jnp.pad(kv[:, :, MLA_NOPE:].reshape(KV_LORA, MLA_HEADS // 2, 2, MLA_V),
                 ((0, 0), (0, 0), (0, 0), (0, LANE - MLA_V)))
    wv = jnp.concatenate([wv[:, :, 0], jnp.roll(wv[:, :, 1], MLA_V, axis=-1)], axis=-1)
    wv = wv.reshape(KV_LORA, MLA_HEADS * LANE).astype(bf)
    lo, mid, hi = MLA_NOPE, MLA_NOPE + HALF_ROPE, MLA_NOPE + MLA_ROPE
    uq = w_uq.reshape(Q_LORA, MLA_HEADS, MLA_QK)
    uq_sw = jnp.zeros((Q_LORA, MLA_HEADS, LANE), jnp.float32)
    uq_sw = uq_sw.at[:, :, lo:mid].set(uq[:, :, mid:hi]).at[:, :, mid:hi].set(uq[:, :, lo:mid])
    wuq = jnp.concatenate([_pad_heads(w_uq, MLA_HEADS, MLA_QK),
                           uq_sw.reshape(Q_LORA, MLA_HEADS * LANE)], axis=1).astype(bf)

    def swapped(g):
        return jnp.zeros((LANE,), jnp.float32).at[lo:mid].set(g[mid:hi]).at[mid:hi].set(g[lo:mid])

    zero = jnp.zeros((LANE,), jnp.float32)
    q_scale_mla = MLA_QK ** -0.5 * LOG2E
    rows = [zero] * VEC_ROWS
    rows[V_GQ_MLA] = _lane_vec(g_q_mla) * q_scale_mla
    rows[V_GQ_MLA_SW] = swapped(g_q_mla) * q_scale_mla
    rows[V_GK_MLA] = _lane_vec(g_k_mla)
    rows[V_GK_MLA_SW] = swapped(g_k_mla)
    for par in range(2):
        feat, extra = FEATURE_BASE[par], EXTRA_BASE[par]
        rows[V_GQ_FOX + par] = _lane_vec(g_q_fox, feat) * (FOX_DIM ** -0.5 * LOG2E)
        rows[V_GK_FOX + par] = _lane_vec(g_k_fox, feat)
        rows[V_ADD_Q_FOX + par] = (zero.at[extra + N_SPLIT:extra + 2 * N_SPLIT].set(1.0)
                                   .at[extra + FLAG_FOX_OFF].set(1.0))
        rows[V_ONES_K_FOX + par] = zero.at[extra:extra + N_SPLIT].set(1.0)
        rows[V_ONES_V + par] = zero.at[extra].set(1.0)
    rows[V_B_FORGET] = _lane_vec(b_forget, MISC_GATE)
    rows[V_ADD_Q_MLA] = zero.at[FLAG_MLA].set(1.0)
    vec = jnp.stack(rows)
    return dict(wcat=wcat, gcq=g_cq[None], wuq=wuq, gckv=g_ckv[None], wkn=wkn, wv=wv, vec=vec)


def _gate_selectors():
    selq = np.zeros((LANE, FOX_HEADS * LANE), np.float32)
    selk = np.zeros((LANE, FOX_HEADS * LANE), np.float32)
    for part in range(N_SPLIT):
        for hd in range(FOX_HEADS):
            extra = hd * LANE + EXTRA_BASE[hd % 2]
            selq[part * FOX_HEADS + hd, extra + part] = 1.0
            selk[part * FOX_HEADS + hd, extra + N_SPLIT + part] = -1.0
    return jnp.asarray(selq, jnp.bfloat16), jnp.asarray(selk, jnp.bfloat16)


def _rope_table(lp):
    lane = jnp.arange(LANE, dtype=jnp.int32)
    rotary = (lane >= MLA_NOPE) & (lane < MLA_NOPE + MLA_ROPE)
    first_half = rotary & (lane < MLA_NOPE + HALF_ROPE)
    pair = ((lane - MLA_NOPE) % HALF_ROPE).astype(jnp.float32)
    inv_freq = ROPE_BASE ** (-(2.0 * pair) / MLA_ROPE)
    pos = (jnp.arange(lp, dtype=jnp.int32) - PAD).astype(jnp.float32)
    ang = pos[:, None] * inv_freq[None, :]
    cos_t = jnp.where(lane < MLA_NOPE, 1.0, jnp.where(rotary, jnp.cos(ang), 0.0))
    sin_sw = jnp.where(rotary, jnp.where(first_half, -jnp.sin(ang), jnp.sin(ang)), 0.0)
    return jnp.concatenate([cos_t, sin_sw], axis=1)


def _token_tile(lp):
    if lp % FLASH_TQ:
        raise ValueError(f"padded length {lp} is not a multiple of {FLASH_TQ}")
    return FLASH_TQ


def kernel(x, meta_tokens, g_mix, g_mlp, w_in_attn, g_cq, w_uq, g_ckv, w_ukv, g_q_mla, g_k_mla,
           g_q_fox, g_k_fox, b_forget, w_out_attn, w_in_conv, conv_w, w_out_conv, w_mlp_up,
           w_mlp_down):
    b, seq, d = x.shape
    assert d == D_MODEL and (PAD + N_META + seq) % BLOCK == 0
    lp = PAD + N_META + seq
    tm = _token_tile(lp)
    bf = jnp.bfloat16

    meta = jnp.broadcast_to(meta_tokens.astype(x.dtype)[None], (b, N_META, d))
    h = jnp.concatenate([jnp.zeros((b, PAD, d), x.dtype), meta, x], axis=1)

    rope_tab = _rope_table(lp)
    tri = (jnp.arange(tm)[:, None] >= jnp.arange(tm)[None, :]).astype(bf)
    selq, selk = _gate_selectors()

    wo_attn, wo_conv, w_conv = w_out_attn.astype(bf), w_out_conv.astype(bf), w_in_conv.astype(bf)
    w_up, w_down = w_mlp_up.astype(bf), w_mlp_down.astype(bf)
    for layer in range(DEPTH):
        j = layer // 2
        gmix = g_mix[layer][None]
        if layer % 2 == 0:
            p = _attn_params(w_in_attn[j], g_cq[j], w_uq[j], g_ckv[j], w_ukv[j], g_q_mla[j],
                             g_k_mla[j], g_q_fox[j], g_k_fox[j], b_forget[j])
            q, k, v = _attn_in(h, gmix, p, rope_tab, tri, selq, selk, tm)
            y = _flash(q, k, v)
            wo = wo_attn
        else:
            cw = jnp.zeros((8, d), jnp.float32).at[0:3].set(conv_w[j])
            y = _conv_in(h, gmix, w_conv, j, cw, tm)
            wo = wo_conv
        h = _mix_out_mlp(h.reshape(b * lp, d), y.reshape(b * lp, d), wo, j, g_mlp[layer][None],
                         w_up, w_down, layer, tm).reshape(b, lp, d)
    return h[:, PAD + N_META:]
```

```python
import functools

import numpy as np
import jax
import jax.numpy as jnp
from jax import lax
from jax.experimental import pallas as pl
from jax.experimental.pallas import tpu as pltpu

D_MODEL = 1024
DEPTH = 4
N_META = 16
BLOCK = 128
PAD = 2 * BLOCK - N_META
MLA_HEADS = 8
MLA_NOPE = 64
MLA_ROPE = 32
MLA_QK = MLA_NOPE + MLA_ROPE
MLA_V = 64
Q_LORA = 384
KV_LORA = 256
ROPE_BASE = 10000.0
FOX_HEADS = 8
FOX_DIM = 64
D_FF = 4 * D_MODEL
EPS = 1e-6
NEG = -1e30

LANE = 128
HEADS = MLA_HEADS + FOX_HEADS
HALF_ROPE = MLA_ROPE // 2
FEATURE_BASE = (0, FOX_DIM)
EXTRA_BASE = (FOX_DIM, 0)
N_SPLIT = 3
FLAG_FOX_OFF = 2 * N_SPLIT
FLAG_MLA = MLA_QK
PAD_KEY = NEG
LOG2E = 1.4426950408889634
MISC_GATE = 0
MISC_ROPE = MLA_NOPE

OFF_CQ = 0
OFF_CKV = OFF_CQ + Q_LORA
OFF_FQ = OFF_CKV + KV_LORA
OFF_FK = OFF_FQ + FOX_HEADS * FOX_DIM
OFF_FV = OFF_FK + FOX_HEADS * FOX_DIM
OFF_MISC = OFF_FV + FOX_HEADS * FOX_DIM
OFF_MISC_SW = OFF_MISC + LANE
W_CAT = OFF_MISC_SW + LANE

(V_GQ_MLA, V_GQ_MLA_SW, V_GK_MLA, V_GK_MLA_SW, V_ADD_Q_MLA, V_B_FORGET) = range(6)
V_GQ_FOX, V_GK_FOX, V_ADD_Q_FOX, V_ONES_K_FOX, V_ONES_V = 6, 8, 10, 12, 14
VEC_ROWS = 16
PAIR = 2 * LANE

FF_CHUNK = 1024
UNDERFLOW_LOG2 = 150.0
FLASH_TQ = 768
FLASH_TK = 384
VMEM_LIMIT = 56 * 1024 * 1024


def _const_spec(shape):
    nd = len(shape)
    return pl.BlockSpec(shape, lambda *_: (0,) * nd, pipeline_mode=pl.Buffered(1))


def _layer_spec(shape, layer):
    nd = len(shape)
    return pl.BlockSpec((1,) + shape, lambda *_: (layer,) + (0,) * nd,
                        pipeline_mode=pl.Buffered(1))


def _rms(x, g, n):
    ms = jnp.sum(x * x, axis=-1, keepdims=True) * (1.0 / n)
    return x * lax.rsqrt(ms + EPS) * g


def _split3(x):
    hi = x.astype(jnp.bfloat16).astype(jnp.float32)
    r1 = x - hi
    mid = r1.astype(jnp.bfloat16).astype(jnp.float32)
    lo = r1 - mid
    packed = hi + pltpu.roll(mid, FOX_HEADS, 1) + pltpu.roll(lo, 2 * FOX_HEADS, 1)
    return packed.astype(jnp.bfloat16)


def _dot(a, b):
    return jnp.dot(a, b, preferred_element_type=jnp.float32)


def _attn_in_kernel(h_ref, gmix_ref, wcat_ref, gcq_ref, wuq_ref, gckv_ref, wkn_ref, wv_ref,
                    vec_ref, rope_ref, tri_ref, selq_ref, selk_ref,
                    q_ref, k_ref, v_ref, gate_end_ref, carry_ref, *, tm):
    i = pl.program_id(1)

    @pl.when(i == 0)
    def _():
        carry_ref[...] = jnp.zeros_like(carry_ref)

    x = h_ref[0]
    hn = _rms(x, gmix_ref[...], D_MODEL).astype(jnp.bfloat16)

    def seg(lo, width):
        return _dot(hn, wcat_ref[:, lo:lo + width])

    def vec(r):
        return vec_ref[r:r + 1, :]

    cos_t = rope_ref[:, 0:LANE]
    sin_sw = rope_ref[:, LANE:2 * LANE]
    gc_q, gs_q = vec(V_GQ_MLA) * cos_t, vec(V_GQ_MLA_SW) * sin_sw
    gc_k, gs_k = vec(V_GK_MLA) * cos_t, vec(V_GK_MLA_SW) * sin_sw
    add_q_mla = vec(V_ADD_Q_MLA)

    lane = lax.broadcasted_iota(jnp.int32, (tm, LANE), 1)
    row = lax.broadcasted_iota(jnp.int32, (tm, LANE), 0)
    valid = (i * tm + row) >= PAD
    pad_key = jnp.where(valid, 0.0, PAD_KEY)
    add_k_mla = jnp.where(lane == FLAG_MLA, pad_key, 0.0)
    halves = (lane < FOX_DIM, lane >= FOX_DIM)
    add_k_fox = [vec(V_ONES_K_FOX + par)
                 + jnp.where(lane == EXTRA_BASE[par] + FLAG_FOX_OFF, pad_key, 0.0)
                 for par in range(2)]

    misc2 = seg(OFF_MISC, PAIR)
    misc = misc2[:, 0:LANE]
    kpe = jnp.where((lane >= MISC_ROPE) & (lane < MISC_ROPE + MLA_ROPE), misc, 0.0)
    k_rot = misc2[:, LANE:PAIR] * gs_k
    xl = misc + vec(V_B_FORGET)
    logf = jnp.minimum(xl, 0.0) - jnp.log1p(jnp.exp(-jnp.abs(xl)))
    logf = jnp.where(valid & (lane >= MISC_GATE) & (lane < MISC_GATE + FOX_HEADS), logf, 0.0)
    cs = _dot(tri_ref[...], _split3(logf))
    cs = (cs + pltpu.roll(cs, LANE - FOX_HEADS, 1)) + pltpu.roll(cs, LANE - 2 * FOX_HEADS, 1)
    cum = jnp.where(lane < FOX_HEADS, cs, 0.0) + carry_ref[0:1, :]
    carry_ref[0:1, :] = cum[tm - 1:tm, :]
    gate_end_ref[0, 0] = carry_ref[...]
    cum3 = _split3(cum * LOG2E)
    gate_q = _dot(cum3, selq_ref[...])
    gate_k = _dot(cum3, selk_ref[...])

    def inv_rms(xv, n):
        return lax.rsqrt(jnp.sum(xv * xv, axis=-1, keepdims=True) * (1.0 / n) + EPS)

    cqn = _rms(seg(OFF_CQ, Q_LORA), gcq_ref[...], Q_LORA).astype(jnp.bfloat16)
    ckvn = _rms(seg(OFF_CKV, KV_LORA), gckv_ref[...], KV_LORA).astype(jnp.bfloat16)
    for g in range(MLA_HEADS // 2):
        cols = slice(g * PAIR, (g + 1) * PAIR)
        cols_sw = slice(MLA_HEADS * LANE + g * PAIR, MLA_HEADS * LANE + (g + 1) * PAIR)
        xq2 = _dot(cqn, wuq_ref[:, cols])
        xq2_sw = _dot(cqn, wuq_ref[:, cols_sw])
        xk2 = _dot(ckvn, wkn_ref[:, cols])
        xv2 = _dot(ckvn, wv_ref[:, cols])
        for e in range(2):
            hd, sl = 2 * g + e, slice(e * LANE, (e + 1) * LANE)
            xq = xq2[:, sl]
            q_ref[0, hd] = ((xq * gc_q + xq2_sw[:, sl] * gs_q) * inv_rms(xq, MLA_QK) + add_q_mla
                            ).astype(jnp.bfloat16)
            xk = xk2[:, sl] + kpe
            k_ref[0, hd] = ((xk * gc_k + k_rot) * inv_rms(xk, MLA_QK) + add_k_mla
                            ).astype(jnp.bfloat16)
            v_ref[0, hd] = (xv2[:, sl] + vec(V_ONES_V + e)).astype(jnp.bfloat16)

    for g in range(FOX_HEADS // 4):
        xq4 = seg(OFF_FQ + g * PAIR, PAIR)
        xk4 = seg(OFF_FK + g * PAIR, PAIR)
        xv4 = seg(OFF_FV + g * PAIR, PAIR)
        for e in range(4):
            hd, par = 4 * g + e, e % 2
            sl = slice((e // 2) * LANE, (e // 2 + 1) * LANE)
            gl = slice(hd * LANE, (hd + 1) * LANE)
            xq, xk = xq4[:, sl], xk4[:, sl]
            rq = inv_rms(jnp.where(halves[par], xq, 0.0), FOX_DIM)
            rk = inv_rms(jnp.where(halves[par], xk, 0.0), FOX_DIM)
            q_ref[0, MLA_HEADS + hd] = (xq * vec(V_GQ_FOX + par) * rq + gate_q[:, gl]
                                        + vec(V_ADD_Q_FOX + par)).astype(jnp.bfloat16)
            k_ref[0, MLA_HEADS + hd] = (xk * vec(V_GK_FOX + par) * rk + gate_k[:, gl]
                                        + add_k_fox[par]).astype(jnp.bfloat16)
            v_ref[0, MLA_HEADS + hd] = (jnp.where(halves[par], xv4[:, sl], 0.0)
                                        + vec(V_ONES_V + par)).astype(jnp.bfloat16)


def _attn_in(h, gmix, p, rope_tab, tri, selq, selk, tm):
    b, lp, d = h.shape
    nt = lp // tm
    kern = functools.partial(_attn_in_kernel, tm=tm)
    qk_shape = jax.ShapeDtypeStruct((b, HEADS, lp, LANE), jnp.bfloat16)
    qk_spec = pl.BlockSpec((1, HEADS, tm, LANE), lambda bi, i: (bi, 0, i, 0))
    return pl.pallas_call(
        kern,
        grid=(b, nt),
        in_specs=[
            pl.BlockSpec((1, tm, d), lambda bi, i: (bi, i, 0)),
            _const_spec((1, d)),
            _const_spec((d, W_CAT)),
            _const_spec((1, Q_LORA)),
            _const_spec((Q_LORA, 2 * MLA_HEADS * LANE)),
            _const_spec((1, KV_LORA)),
            _const_spec((KV_LORA, MLA_HEADS * LANE)),
            _const_spec((KV_LORA, MLA_HEADS * LANE)),
            _const_spec((VEC_ROWS, LANE)),
            pl.BlockSpec((tm, 2 * LANE), lambda bi, i: (i, 0)),
            _const_spec((tm, tm)),
            _const_spec((LANE, FOX_HEADS * LANE)),
            _const_spec((LANE, FOX_HEADS * LANE)),
        ],
        out_specs=[qk_spec, qk_spec, qk_spec,
                   pl.BlockSpec((1, 1, 8, LANE), lambda bi, i: (bi, i, 0, 0))],
        out_shape=[qk_shape, qk_shape, qk_shape,
                   jax.ShapeDtypeStruct((b, nt, 8, LANE), jnp.float32)],
        scratch_shapes=[pltpu.VMEM((8, LANE), jnp.float32)],
        compiler_params=pltpu.CompilerParams(
            dimension_semantics=("arbitrary", "arbitrary"), vmem_limit_bytes=VMEM_LIMIT),
        name="attn_in",
    )(h, gmix, p["wcat"], p["gcq"], p["wuq"], p["gckv"], p["wkn"], p["wv"], p["vec"],
      rope_tab, tri, selq, selk)


def _flash_kernel(gate_end_ref, slack_ref, q_ref, k_ref, v_ref, o_ref, m_ref, acc_ref, al_ref,
                  p_ref, *, tq, tk, nq):
    chunks = tq // tk
    bi, hp = pl.program_id(0), pl.program_id(1)

    def query_block(qi, carry):
        qbase = qi * tq
        m_ref[...] = jnp.full(m_ref.shape, NEG, jnp.float32)
        acc_ref[...] = jnp.zeros(acc_ref.shape, jnp.float32)
        al_ref[1, chunks - 1] = jnp.zeros((tq, LANE), jnp.float32)
        p_ref[1, chunks - 1] = jnp.zeros((tq, tk), jnp.bfloat16)

        def softmax(j, u, base, diagonal):
            r0 = u * tk if diagonal else 0
            rows = slice(r0, tq)
            start = pl.multiple_of(base + u * tk, tk)
            q_rows = pl.ds(pl.multiple_of(qbase + r0, tk), tq - r0)
            s = lax.dot_general(q_ref[0, j, q_rows, :], k_ref[0, j, pl.ds(start, tk), :],
                                (((1,), (1,)), ((), ())), preferred_element_type=jnp.float32)
            if diagonal:
                row = lax.broadcasted_iota(jnp.int32, (tq - r0, tk), 0)
                col = lax.broadcasted_iota(jnp.int32, (tq - r0, tk), 1)
                s = jnp.where(col <= row, s, NEG)
            m_prev = m_ref[j, rows, :]
            m_next = jnp.maximum(m_prev, jnp.max(s, axis=1, keepdims=True))
            p = jnp.exp2(s - jnp.concatenate([m_next] * (tk // LANE), axis=1))
            al_ref[j, u, rows, :] = jnp.exp2(m_prev - m_next)
            p_ref[j, u, rows, :] = p.astype(jnp.bfloat16)
            m_ref[j, rows, :] = m_next

        def pv(j, u, base, diagonal):
            r0 = u * tk if diagonal else 0
            rows = slice(r0, tq)
            start = pl.multiple_of(base + u * tk, tk)
            acc_ref[j, rows, :] = (acc_ref[j, rows, :] * al_ref[j, u, rows, :]
                                   + _dot(p_ref[j, u, rows, :], v_ref[0, j, pl.ds(start, tk), :]))

        def block(kb, diagonal):
            base = kb * tq
            for u in range(chunks):
                softmax(0, u, base, diagonal)
                if u == 0:
                    pv(1, chunks - 1, jnp.maximum(kb - 1, 0) * tq, False)
                else:
                    pv(1, u - 1, base, diagonal)
                softmax(1, u, base, diagonal)
                pv(0, u, base, diagonal)

        def blocks_needed(j):
            hd = 2 * hp + j
            gate_q = gate_end_ref[bi, hd, jnp.maximum(qi - 1, 0)]
            count = jnp.int32(0)
            for kb in range(nq - 1):
                keep = (kb < qi) & (gate_q - gate_end_ref[bi, hd, kb] >= slack_ref[hd])
                count = count + keep.astype(jnp.int32)
            return count

        n_off = jnp.maximum(blocks_needed(0), blocks_needed(1))
        first = qi - n_off
        odd = n_off & 1

        @pl.when(odd == 1)
        def _():
            block(first, False)

        def body(pair, c):
            kb = first + odd + 2 * pair
            block(kb, False)
            block(kb + 1, False)
            return c

        lax.fori_loop(0, lax.shift_right_logical(n_off, 1), body, 0)
        block(qi, True)
        pv(1, chunks - 1, qbase, True)

        o0 = acc_ref[0]
        o1 = acc_ref[1]
        o0 = o0 / o0[:, EXTRA_BASE[0]:EXTRA_BASE[0] + 1]
        o1 = o1 / o1[:, EXTRA_BASE[1]:EXTRA_BASE[1] + 1]
        lane = lax.broadcasted_iota(jnp.int32, (tq, LANE), 1)
        o = jnp.where(lane < MLA_V, o0, o1)
        o_ref[0, pl.ds(pl.multiple_of(qbase, tq), tq), :] = o.astype(jnp.bfloat16)
        return carry

    lax.fori_loop(0, nq, query_block, 0)


def _flash(gate_end, slack, q, k, v):
    b, _, lp, _ = q.shape
    tq, tk = FLASH_TQ, FLASH_TK
    kern = functools.partial(_flash_kernel, tq=tq, tk=tk, nq=lp // tq)
    qkv_spec = pl.BlockSpec((1, 2, lp, LANE), lambda bi, hp: (bi, hp, 0, 0))
    return pl.pallas_call(
        kern,
        grid=(b, HEADS // 2),
        in_specs=[pl.BlockSpec(memory_space=pltpu.SMEM), pl.BlockSpec(memory_space=pltpu.SMEM),
                  qkv_spec, qkv_spec, qkv_spec],
        out_specs=pl.BlockSpec((1, lp, LANE), lambda bi, hp: (bi, 0, hp)),
        out_shape=jax.ShapeDtypeStruct((b, lp, HEADS * MLA_V), jnp.bfloat16),
        scratch_shapes=[pltpu.VMEM((2, tq, LANE), jnp.float32)] * 2
        + [pltpu.VMEM((2, tq // tk, tq, LANE), jnp.float32),
           pltpu.VMEM((2, tq // tk, tq, tk), jnp.bfloat16)],
        compiler_params=pltpu.CompilerParams(
            dimension_semantics=("arbitrary", "arbitrary"), vmem_limit_bytes=VMEM_LIMIT),
        name="flash",
    )(gate_end, slack, q, k, v)


def _conv_in_kernel(h_ref, gmix_ref, win_ref, cw_ref, y_ref, gs_ref, *, tm):
    i = pl.program_id(1)

    @pl.when(i == 0)
    def _():
        gs_ref[0:8, :] = jnp.zeros((8, D_MODEL), jnp.float32)

    x = h_ref[0]
    hn = _rms(x, gmix_ref[...], D_MODEL).astype(jnp.bfloat16)
    gate_c = _dot(hn, win_ref[0, :, D_MODEL:2 * D_MODEL])
    u = _dot(hn, win_ref[0, :, 2 * D_MODEL:3 * D_MODEL])
    row = lax.broadcasted_iota(jnp.int32, (tm, D_MODEL), 0)
    g = jnp.where((i * tm + row) >= PAD, gate_c * u, 0.0)
    gs_ref[8:tm + 8, :] = g
    y = (cw_ref[0:1, :] * gs_ref[6:tm + 6, :] + cw_ref[1:2, :] * gs_ref[7:tm + 7, :]
         + cw_ref[2:3, :] * g)
    gs_ref[0:8, :] = gs_ref[tm:tm + 8, :]
    gate_b = _dot(hn, win_ref[0, :, 0:D_MODEL])
    y_ref[0] = (gate_b * y).astype(jnp.bfloat16)


def _conv_in(h, gmix, win, layer, cw, tm):
    b, lp, d = h.shape
    kern = functools.partial(_conv_in_kernel, tm=tm)
    return pl.pallas_call(
        kern,
        grid=(b, lp // tm),
        in_specs=[
            pl.BlockSpec((1, tm, d), lambda bi, i: (bi, i, 0)),
            _const_spec((1, d)),
            _layer_spec((d, 3 * d), layer),
            _const_spec((8, d)),
        ],
        out_specs=pl.BlockSpec((1, tm, d), lambda bi, i: (bi, i, 0)),
        out_shape=jax.ShapeDtypeStruct((b, lp, d), jnp.bfloat16),
        scratch_shapes=[pltpu.VMEM((tm + 8, d), jnp.float32)],
        compiler_params=pltpu.CompilerParams(
            dimension_semantics=("arbitrary", "arbitrary"), vmem_limit_bytes=VMEM_LIMIT),
        name="conv_in",
    )(h, gmix, win, cw)


def _mix_out_mlp_kernel(h_ref, y_ref, wo_ref, gmlp_ref, wup_ref, wdn_ref, out_ref):
    h1 = h_ref[...] + _dot(y_ref[...], wo_ref[0])
    n = _rms(h1, gmlp_ref[...], D_MODEL).astype(jnp.bfloat16)
    acc = h1
    for c in range(D_FF // FF_CHUNK):
        sl = slice(c * FF_CHUNK, (c + 1) * FF_CHUNK)
        a = jnp.maximum(_dot(n, wup_ref[0, :, sl]), 0.0)
        acc = acc + _dot((a * a).astype(jnp.bfloat16), wdn_ref[0, sl, :])
    out_ref[...] = acc


def _mix_out_mlp(h, y, wo, wo_layer, gmlp, wup, wdn, layer, tm):
    r, d = h.shape
    return pl.pallas_call(
        _mix_out_mlp_kernel,
        grid=(r // tm,),
        in_specs=[
            pl.BlockSpec((tm, d), lambda i: (i, 0)),
            pl.BlockSpec((tm, d), lambda i: (i, 0)),
            _layer_spec((d, d), wo_layer),
            _const_spec((1, d)),
            _layer_spec((d, D_FF), layer),
            _layer_spec((D_FF, d), layer),
        ],
        out_specs=pl.BlockSpec((tm, d), lambda i: (i, 0)),
        out_shape=jax.ShapeDtypeStruct((r, d), jnp.float32),
        compiler_params=pltpu.CompilerParams(
            dimension_semantics=("arbitrary",), vmem_limit_bytes=VMEM_LIMIT),
        name="mix_out_mlp",
    )(h, y, wo, gmlp, wup, wdn)


def _pad_heads(w, heads, dim):
    k = w.shape[0]
    w = w.reshape(k, heads, dim)
    w = jnp.pad(w, ((0, 0), (0, 0), (0, LANE - dim)))
    return w.reshape(k, heads * LANE)


def _lane_vec(v, offset=0):
    return jnp.zeros((LANE,), jnp.float32).at[offset:offset + v.shape[0]].set(v)


def _attn_params(w_in, g_cq, w_uq, g_ckv, w_ukv, g_q_mla, g_k_mla, g_q_fox, g_k_fox, b_forget):
    bf = jnp.bfloat16
    o1 = Q_LORA
    o2 = o1 + KV_LORA
    o3 = o2 + MLA_ROPE
    o4 = o3 + FOX_HEADS * FOX_DIM
    o5 = o4 + FOX_HEADS * FOX_DIM
    o6 = o5 + FOX_HEADS * FOX_DIM
    misc = jnp.zeros((D_MODEL, LANE), jnp.float32)
    misc = misc.at[:, MISC_GATE:MISC_GATE + FOX_HEADS].set(w_in[:, o6:])
    misc = misc.at[:, MISC_ROPE:MISC_ROPE + MLA_ROPE].set(w_in[:, o2:o3])
    misc_sw = jnp.zeros((D_MODEL, LANE), jnp.float32)
    misc_sw = misc_sw.at[:, MISC_ROPE:MISC_ROPE + HALF_ROPE].set(w_in[:, o2 + HALF_ROPE:o3])
    misc_sw = misc_sw.at[:, MISC_ROPE + HALF_ROPE:MISC_ROPE + MLA_ROPE].set(
        w_in[:, o2:o2 + HALF_ROPE])
    wcat = jnp.concatenate([
        w_in[:, :o1], w_in[:, o1:o2],
        w_in[:, o3:o6], misc, misc_sw], axis=1).astype(bf)
    kv = w_ukv.reshape(KV_LORA, MLA_HEADS, MLA_NOPE + MLA_V)
    wkn = _pad_heads(kv[:, :, :MLA_NOPE].reshape(KV_LORA, -1), MLA_HEADS, MLA_NOPE).astype(bf)
    wv = jnp.pad(kv[:, :, MLA_NOPE:].reshape(KV_LORA, MLA_HEADS // 2, 2, MLA_V),
                 ((0, 0), (0, 0), (0, 0), (0, LANE - MLA_V)))
    wv = jnp.concatenate([wv[:, :, 0], jnp.roll(wv[:, :, 1], MLA_V, axis=-1)], axis=-1)
    wv = wv.reshape(KV_LORA, MLA_HEADS * LANE).astype(bf)
    lo, mid, hi = MLA_NOPE, MLA_NOPE + HALF_ROPE, MLA_NOPE + MLA_ROPE
    uq = w_uq.reshape(Q_LORA, MLA_HEADS, MLA_QK)
    uq_sw = jnp.zeros((Q_LORA, MLA_HEADS, LANE), jnp.float32)
    uq_sw = uq_sw.at[:, :, lo:mid].set(uq[:, :, mid:hi]).at[:, :, mid:hi].set(uq[:, :, lo:mid])
    wuq = jnp.concatenate([_pad_heads(w_uq, MLA_HEADS, MLA_QK),
                           uq_sw.reshape(Q_LORA, MLA_HEADS * LANE)], axis=1).astype(bf)

    def swapped(g):
        return jnp.zeros((LANE,), jnp.float32).at[lo:mid].set(g[mid:hi]).at[mid:hi].set(g[lo:mid])

    zero = jnp.zeros((LANE,), jnp.float32)
    q_scale_mla = MLA_QK ** -0.5 * LOG2E
    rows = [zero] * VEC_ROWS
    rows[V_GQ_MLA] = _lane_vec(g_q_mla) * q_scale_mla
    rows[V_GQ_MLA_SW] = swapped(g_q_mla) * q_scale_mla
    rows[V_GK_MLA] = _lane_vec(g_k_mla)
    rows[V_GK_MLA_SW] = swapped(g_k_mla)
    for par in range(2):
        feat, extra = FEATURE_BASE[par], EXTRA_BASE[par]
        rows[V_GQ_FOX + par] = _lane_vec(g_q_fox, feat) * (FOX_DIM ** -0.5 * LOG2E)
        rows[V_GK_FOX + par] = _lane_vec(g_k_fox, feat)
        rows[V_ADD_Q_FOX + par] = (zero.at[extra + N_SPLIT:extra + 2 * N_SPLIT].set(1.0)
                                   .at[extra + FLAG_FOX_OFF].set(1.0))
        rows[V_ONES_K_FOX + par] = zero.at[extra:extra + N_SPLIT].set(1.0)
        rows[V_ONES_V + par] = zero.at[extra].set(1.0)
    rows[V_B_FORGET] = _lane_vec(b_forget, MISC_GATE)
    rows[V_ADD_Q_MLA] = zero.at[FLAG_MLA].set(1.0)
    vec = jnp.stack(rows)
    return dict(wcat=wcat, gcq=g_cq[None], wuq=wuq, gckv=g_ckv[None], wkn=wkn, wv=wv, vec=vec)


def _gate_selectors():
    selq = np.zeros((LANE, FOX_HEADS * LANE), np.float32)
    selk = np.zeros((LANE, FOX_HEADS * LANE), np.float32)
    for part in range(N_SPLIT):
        for hd in range(FOX_HEADS):
            extra = hd * LANE + EXTRA_BASE[hd % 2]
            selq[part * FOX_HEADS + hd, extra + part] = 1.0
            selk[part * FOX_HEADS + hd, extra + N_SPLIT + part] = -1.0
    return jnp.asarray(selq, jnp.bfloat16), jnp.asarray(selk, jnp.bfloat16)


def _rope_table(lp):
    lane = jnp.arange(LANE, dtype=jnp.int32)
    rotary = (lane >= MLA_NOPE) & (lane < MLA_NOPE + MLA_ROPE)
    first_half = rotary & (lane < MLA_NOPE + HALF_ROPE)
    pair = ((lane - MLA_NOPE) % HALF_ROPE).astype(jnp.float32)
    inv_freq = ROPE_BASE ** (-(2.0 * pair) / MLA_ROPE)
    pos = (jnp.arange(lp, dtype=jnp.int32) - PAD).astype(jnp.float32)
    ang = pos[:, None] * inv_freq[None, :]
    cos_t = jnp.where(lane < MLA_NOPE, 1.0, jnp.where(rotary, jnp.cos(ang), 0.0))
    sin_sw = jnp.where(rotary, jnp.where(first_half, -jnp.sin(ang), jnp.sin(ang)), 0.0)
    return jnp.concatenate([cos_t, sin_sw], axis=1)


def _pruning_tables(gate_end, g_q, g_k):
    b, nt = gate_end.shape[:2]
    fox = jnp.transpose(gate_end[:, :, 0, MISC_GATE:MISC_GATE + FOX_HEADS] * LOG2E, (0, 2, 1))
    table = jnp.concatenate([jnp.zeros((b, MLA_HEADS, nt), jnp.float32), fox], axis=1)
    bound = 1.02 * FOX_DIM * (FOX_DIM ** -0.5 * LOG2E) * jnp.max(jnp.abs(g_q)) * jnp.max(jnp.abs(g_k))
    slack_fox = -(2.0 * bound + UNDERFLOW_LOG2 + 4.0)
    slack = jnp.concatenate([jnp.full((MLA_HEADS,), NEG, jnp.float32),
                             jnp.full((FOX_HEADS,), slack_fox, jnp.float32)])
    return table, slack


def _token_tile(lp):
    if lp % FLASH_TQ:
        raise ValueError(f"padded length {lp} is not a multiple of {FLASH_TQ}")
    return FLASH_TQ


def kernel(x, meta_tokens, g_mix, g_mlp, w_in_attn, g_cq, w_uq, g_ckv, w_ukv, g_q_mla, g_k_mla,
           g_q_fox, g_k_fox, b_forget, w_out_attn, w_in_conv, conv_w, w_out_conv, w_mlp_up,
           w_mlp_down):
    b, seq, d = x.shape
    assert d == D_MODEL and (PAD + N_META + seq) % BLOCK == 0
    lp = PAD + N_META + seq
    tm = _token_tile(lp)
    bf = jnp.bfloat16

    meta = jnp.broadcast_to(meta_tokens.astype(x.dtype)[None], (b, N_META, d))
    h = jnp.concatenate([jnp.zeros((b, PAD, d), x.dtype), meta, x], axis=1)

    rope_tab = _rope_table(lp)
    tri = (jnp.arange(tm)[:, None] >= jnp.arange(tm)[None, :]).astype(bf)
    selq, selk = _gate_selectors()

    wo_attn, wo_conv, w_conv = w_out_attn.astype(bf), w_out_conv.astype(bf), w_in_conv.astype(bf)
    w_up, w_down = w_mlp_up.astype(bf), w_mlp_down.astype(bf)
    for layer in range(DEPTH):
        j = layer // 2
        gmix = g_mix[layer][None]
        if layer % 2 == 0:
            p = _attn_params(w_in_attn[j], g_cq[j], w_uq[j], g_ckv[j], w_ukv[j], g_q_mla[j],
                             g_k_mla[j], g_q_fox[j], g_k_fox[j], b_forget[j])
            q, k, v, gate_end = _attn_in(h, gmix, p, rope_tab, tri, selq, selk, tm)
            y = _flash(*_pruning_tables(gate_end, g_q_fox[j], g_k_fox[j]), q, k, v)
            wo = wo_attn
        else:
            cw = jnp.zeros((8, d), jnp.float32).at[0:3].set(conv_w[j])
            y = _conv_in(h, gmix, w_conv, j, cw, tm)
            wo = wo_conv
        h = _mix_out_mlp(h.reshape(b * lp, d), y.reshape(b * lp, d), wo, j, g_mlp[layer][None],
                         w_up, w_down, layer, tm).reshape(b, lp, d)
    return h[:, PAD + N_META:]
```

```python
import functools

import numpy as np
import jax
import jax.numpy as jnp
from jax import lax
from jax.experimental import pallas as pl
from jax.experimental.pallas import tpu as pltpu

D_MODEL = 1024
DEPTH = 4
N_META = 16
BLOCK = 128
PAD = 2 * BLOCK - N_META
REAL_START = PAD + N_META
REAL_PARTS = 3
MLA_HEADS = 8
MLA_NOPE = 64
MLA_ROPE = 32
MLA_QK = MLA_NOPE + MLA_ROPE
MLA_V = 64
Q_LORA = 384
KV_LORA = 256
ROPE_BASE = 10000.0
FOX_HEADS = 8
FOX_DIM = 64
D_FF = 4 * D_MODEL
EPS = 1e-6
NEG = -1e30

LANE = 128
HEADS = MLA_HEADS + FOX_HEADS
HALF_ROPE = MLA_ROPE // 2
FEATURE_BASE = (0, FOX_DIM)
EXTRA_BASE = (FOX_DIM, 0)
N_SPLIT = 3
FLAG_FOX_OFF = 2 * N_SPLIT
FLAG_MLA = MLA_QK
PAD_KEY = NEG
LOG2E = 1.4426950408889634
MISC_GATE = 0
MISC_ROPE = MLA_NOPE

OFF_CQ = 0
OFF_CKV = OFF_CQ + Q_LORA
OFF_FQ = OFF_CKV + KV_LORA
OFF_FK = OFF_FQ + FOX_HEADS * FOX_DIM
OFF_FV = OFF_FK + FOX_HEADS * FOX_DIM
OFF_MISC = OFF_FV + FOX_HEADS * FOX_DIM
OFF_MISC_SW = OFF_MISC + LANE
W_CAT = OFF_MISC_SW + LANE

(V_GQ_MLA, V_GQ_MLA_SW, V_GK_MLA, V_GK_MLA_SW, V_ADD_Q_MLA, V_B_FORGET) = range(6)
V_GQ_FOX, V_GK_FOX, V_ADD_Q_FOX, V_ONES_K_FOX, V_ONES_V = 6, 8, 10, 12, 14
VEC_ROWS = 16
PAIR = 2 * LANE

FF_CHUNK = 1024
UNDERFLOW_LOG2 = 150.0
FLASH_TQ = 768
FLASH_TK = 384
VMEM_LIMIT = 56 * 1024 * 1024


def _const_spec(shape):
    nd = len(shape)
    return pl.BlockSpec(shape, lambda *_: (0,) * nd, pipeline_mode=pl.Buffered(1))


def _layer_spec(shape, layer):
    nd = len(shape)
    return pl.BlockSpec((1,) + shape, lambda *_: (layer,) + (0,) * nd,
                        pipeline_mode=pl.Buffered(1))


def _rms(x, g, n):
    ms = jnp.sum(x * x, axis=-1, keepdims=True) * (1.0 / n)
    return x * lax.rsqrt(ms + EPS) * g


def _split3(x):
    hi = x.astype(jnp.bfloat16).astype(jnp.float32)
    r1 = x - hi
    mid = r1.astype(jnp.bfloat16).astype(jnp.float32)
    lo = r1 - mid
    packed = hi + pltpu.roll(mid, FOX_HEADS, 1) + pltpu.roll(lo, 2 * FOX_HEADS, 1)
    return packed.astype(jnp.bfloat16)


def _dot(a, b):
    return jnp.dot(a, b, preferred_element_type=jnp.float32)


def _attn_in_kernel(h_ref, gmix_ref, wcat_ref, gcq_ref, wuq_ref, gckv_ref, wkn_ref, wv_ref,
                    vec_ref, rope_ref, tri_ref, selq_ref, selk_ref,
                    q_ref, k_ref, v_ref, gate_end_ref, carry_ref, *, tm):
    i = pl.program_id(1)

    @pl.when(i == 0)
    def _():
        carry_ref[...] = jnp.zeros_like(carry_ref)

    x = h_ref[0]
    hn = _rms(x, gmix_ref[...], D_MODEL).astype(jnp.bfloat16)

    def seg(lo, width):
        return _dot(hn, wcat_ref[:, lo:lo + width])

    def vec(r):
        return vec_ref[r:r + 1, :]

    cos_t = rope_ref[:, 0:LANE]
    sin_sw = rope_ref[:, LANE:2 * LANE]
    gc_q, gs_q = vec(V_GQ_MLA) * cos_t, vec(V_GQ_MLA_SW) * sin_sw
    gc_k, gs_k = vec(V_GK_MLA) * cos_t, vec(V_GK_MLA_SW) * sin_sw
    add_q_mla = vec(V_ADD_Q_MLA)

    lane = lax.broadcasted_iota(jnp.int32, (tm, LANE), 1)
    row = lax.broadcasted_iota(jnp.int32, (tm, LANE), 0)
    valid = (i * tm + row) >= PAD
    pad_key = jnp.where(valid, 0.0, PAD_KEY)
    add_k_mla = jnp.where(lane == FLAG_MLA, pad_key, 0.0)
    halves = (lane < FOX_DIM, lane >= FOX_DIM)
    add_k_fox = [vec(V_ONES_K_FOX + par)
                 + jnp.where(lane == EXTRA_BASE[par] + FLAG_FOX_OFF, pad_key, 0.0)
                 for par in range(2)]

    misc2 = seg(OFF_MISC, PAIR)
    misc = misc2[:, 0:LANE]
    kpe = jnp.where((lane >= MISC_ROPE) & (lane < MISC_ROPE + MLA_ROPE), misc, 0.0)
    k_rot = misc2[:, LANE:PAIR] * gs_k
    xl = misc + vec(V_B_FORGET)
    logf = jnp.minimum(xl, 0.0) - jnp.log1p(jnp.exp(-jnp.abs(xl)))
    logf = jnp.where(valid & (lane >= MISC_GATE) & (lane < MISC_GATE + FOX_HEADS), logf, 0.0)
    cs = _dot(tri_ref[...], _split3(logf))
    cs = (cs + pltpu.roll(cs, LANE - FOX_HEADS, 1)) + pltpu.roll(cs, LANE - 2 * FOX_HEADS, 1)
    cum = jnp.where(lane < FOX_HEADS, cs, 0.0) + carry_ref[0:1, :]
    carry_ref[0:1, :] = cum[tm - 1:tm, :]
    gate_end_ref[0, 0] = carry_ref[...]
    cum3 = _split3(cum * LOG2E)
    gate_q = _dot(cum3, selq_ref[...])
    gate_k = _dot(cum3, selk_ref[...])

    def inv_rms(xv, n):
        return lax.rsqrt(jnp.sum(xv * xv, axis=-1, keepdims=True) * (1.0 / n) + EPS)

    cqn = _rms(seg(OFF_CQ, Q_LORA), gcq_ref[...], Q_LORA).astype(jnp.bfloat16)
    ckvn = _rms(seg(OFF_CKV, KV_LORA), gckv_ref[...], KV_LORA).astype(jnp.bfloat16)
    for g in range(MLA_HEADS // 2):
        cols = slice(g * PAIR, (g + 1) * PAIR)
        cols_sw = slice(MLA_HEADS * LANE + g * PAIR, MLA_HEADS * LANE + (g + 1) * PAIR)
        xq2 = _dot(cqn, wuq_ref[:, cols])
        xq2_sw = _dot(cqn, wuq_ref[:, cols_sw])
        xk2 = _dot(ckvn, wkn_ref[:, cols])
        xv2 = _dot(ckvn, wv_ref[:, cols])
        for e in range(2):
            hd, sl = 2 * g + e, slice(e * LANE, (e + 1) * LANE)
            xq = xq2[:, sl]
            q_ref[0, hd] = ((xq * gc_q + xq2_sw[:, sl] * gs_q) * inv_rms(xq, MLA_QK) + add_q_mla
                            ).astype(jnp.bfloat16)
            xk = xk2[:, sl] + kpe
            k_ref[0, hd] = ((xk * gc_k + k_rot) * inv_rms(xk, MLA_QK) + add_k_mla
                            ).astype(jnp.bfloat16)
            v_ref[0, hd] = (xv2[:, sl] + vec(V_ONES_V + e)).astype(jnp.bfloat16)

    for g in range(FOX_HEADS // 4):
        xq4 = seg(OFF_FQ + g * PAIR, PAIR)
        xk4 = seg(OFF_FK + g * PAIR, PAIR)
        xv4 = seg(OFF_FV + g * PAIR, PAIR)
        for e in range(4):
            hd, par = 4 * g + e, e % 2
            sl = slice((e // 2) * LANE, (e // 2 + 1) * LANE)
            gl = slice(hd * LANE, (hd + 1) * LANE)
            xq, xk = xq4[:, sl], xk4[:, sl]
            rq = inv_rms(jnp.where(halves[par], xq, 0.0), FOX_DIM)
            rk = inv_rms(jnp.where(halves[par], xk, 0.0), FOX_DIM)
            q_ref[0, MLA_HEADS + hd] = (xq * vec(V_GQ_FOX + par) * rq + gate_q[:, gl]
                                        + vec(V_ADD_Q_FOX + par)).astype(jnp.bfloat16)
            k_ref[0, MLA_HEADS + hd] = (xk * vec(V_GK_FOX + par) * rk + gate_k[:, gl]
                                        + add_k_fox[par]).astype(jnp.bfloat16)
            v_ref[0, MLA_HEADS + hd] = (jnp.where(halves[par], xv4[:, sl], 0.0)
                                        + vec(V_ONES_V + par)).astype(jnp.bfloat16)


def _attn_in(h, gmix, p, rope_tab, tri, selq, selk, tm):
    b, lp, d = h.shape
    nt = lp // tm
    kern = functools.partial(_attn_in_kernel, tm=tm)
    qk_shape = jax.ShapeDtypeStruct((b, HEADS, lp, LANE), jnp.bfloat16)
    qk_spec = pl.BlockSpec((1, HEADS, tm, LANE), lambda bi, i: (bi, 0, i, 0))
    return pl.pallas_call(
        kern,
        grid=(b, nt),
        in_specs=[
            pl.BlockSpec((1, tm, d), lambda bi, i: (bi, i, 0)),
            _const_spec((1, d)),
            _const_spec((d, W_CAT)),
            _const_spec((1, Q_LORA)),
            _const_spec((Q_LORA, 2 * MLA_HEADS * LANE)),
            _const_spec((1, KV_LORA)),
            _const_spec((KV_LORA, MLA_HEADS * LANE)),
            _const_spec((KV_LORA, MLA_HEADS * LANE)),
            _const_spec((VEC_ROWS, LANE)),
            pl.BlockSpec((tm, 2 * LANE), lambda bi, i: (i, 0)),
            _const_spec((tm, tm)),
            _const_spec((LANE, FOX_HEADS * LANE)),
            _const_spec((LANE, FOX_HEADS * LANE)),
        ],
        out_specs=[qk_spec, qk_spec, qk_spec,
                   pl.BlockSpec((1, 1, 8, LANE), lambda bi, i: (bi, i, 0, 0))],
        out_shape=[qk_shape, qk_shape, qk_shape,
                   jax.ShapeDtypeStruct((b, nt, 8, LANE), jnp.float32)],
        scratch_shapes=[pltpu.VMEM((8, LANE), jnp.float32)],
        compiler_params=pltpu.CompilerParams(
            dimension_semantics=("arbitrary", "arbitrary"), vmem_limit_bytes=VMEM_LIMIT),
        name="attn_in",
    )(h, gmix, p["wcat"], p["gcq"], p["wuq"], p["gckv"], p["wkn"], p["wv"], p["vec"],
      rope_tab, tri, selq, selk)


def _flash_kernel(gate_end_ref, slack_ref, q_ref, k_ref, v_ref, o_ref, m_ref, acc_ref, al_ref,
                  p_ref, *, tq, tk, nq):
    chunks = tq // tk
    bi, hp = pl.program_id(0), pl.program_id(1)

    def query_block(qi, carry):
        qbase = qi * tq

        def softmax(j, u, base, diagonal, first=False):
            r0 = u * tk if diagonal else 0
            rows = slice(r0, tq)
            start = pl.multiple_of(base + u * tk, tk)
            q_rows = pl.ds(pl.multiple_of(qbase + r0, tk), tq - r0)
            s = lax.dot_general(q_ref[0, j, q_rows, :], k_ref[0, j, pl.ds(start, tk), :],
                                (((1,), (1,)), ((), ())), preferred_element_type=jnp.float32)
            if diagonal:
                row = lax.broadcasted_iota(jnp.int32, (tq - r0, tk), 0)
                col = lax.broadcasted_iota(jnp.int32, (tq - r0, tk), 1)
                s = jnp.where(col <= row, s, NEG)
            if first:
                m_next = jnp.broadcast_to(jnp.max(s, axis=1, keepdims=True), (tq - r0, LANE))
            else:
                m_prev = m_ref[j, rows, :]
                m_next = jnp.maximum(m_prev, jnp.max(s, axis=1, keepdims=True))
                al_ref[j, u, rows, :] = jnp.exp2(m_prev - m_next)
            p = jnp.exp2(s - jnp.concatenate([m_next] * (tk // LANE), axis=1))
            p_ref[j, u, rows, :] = p.astype(jnp.bfloat16)
            m_ref[j, rows, :] = m_next

        def pv(j, u, base, diagonal, first=False):
            r0 = u * tk if diagonal else 0
            rows = slice(r0, tq)
            start = pl.multiple_of(base + u * tk, tk)
            new = _dot(p_ref[j, u, rows, :], v_ref[0, j, pl.ds(start, tk), :])
            if first:
                acc_ref[j, rows, :] = new
            else:
                acc_ref[j, rows, :] = acc_ref[j, rows, :] * al_ref[j, u, rows, :] + new

        def diagonal_block():
            for u in range(chunks):
                softmax(0, u, qbase, True, first=(u == 0))
                if u > 0:
                    pv(1, u - 1, qbase, True, first=(u == 1))
                softmax(1, u, qbase, True, first=(u == 0))
                pv(0, u, qbase, True, first=(u == 0))
            pv(1, chunks - 1, qbase, True, first=(chunks == 1))

        def block(kb, diagonal):
            base = kb * tq
            for u in range(chunks):
                softmax(0, u, base, diagonal)
                if u == 0:
                    pv(1, chunks - 1, jnp.maximum(kb - 1, 0) * tq, False)
                else:
                    pv(1, u - 1, base, diagonal)
                softmax(1, u, base, diagonal)
                pv(0, u, base, diagonal)

        def blocks_needed(j):
            hd = 2 * hp + j
            gate_q = gate_end_ref[bi, hd, jnp.maximum(qi - 1, 0)]
            count = jnp.int32(0)
            for kb in range(nq - 1):
                keep = (kb < qi) & (gate_q - gate_end_ref[bi, hd, kb] >= slack_ref[hd])
                count = count + keep.astype(jnp.int32)
            return count

        diagonal_block()
        al_ref[1, chunks - 1] = jnp.ones((tq, LANE), jnp.float32)
        p_ref[1, chunks - 1] = jnp.zeros((tq, tk), jnp.bfloat16)
        n_off = jnp.maximum(blocks_needed(0), blocks_needed(1))
        first = qi - n_off
        odd = n_off & 1

        @pl.when(odd == 1)
        def _():
            block(first, False)

        def body(pair, c):
            kb = first + odd + 2 * pair
            block(kb, False)
            block(kb + 1, False)
            return c

        lax.fori_loop(0, lax.shift_right_logical(n_off, 1), body, 0)
        pv(1, chunks - 1, jnp.maximum(qi - 1, 0) * tq, False)

        o0 = acc_ref[0]
        o1 = acc_ref[1]
        o0 = o0 / pltpu.roll(o0, FOX_DIM, 1)
        o1 = o1 / pltpu.roll(o1, FOX_DIM, 1)
        lane = lax.broadcasted_iota(jnp.int32, (tq, LANE), 1)
        o = jnp.where(lane < MLA_V, o0, o1)
        o_ref[0, pl.ds(pl.multiple_of(qbase, tq), tq), :] = o.astype(jnp.bfloat16)
        return carry

    lax.fori_loop(0, nq, query_block, 0)


def _flash(gate_end, slack, q, k, v):
    b, _, lp, _ = q.shape
    tq, tk = FLASH_TQ, FLASH_TK
    kern = functools.partial(_flash_kernel, tq=tq, tk=tk, nq=lp // tq)
    qkv_spec = pl.BlockSpec((1, 2, lp, LANE), lambda bi, hp: (bi, hp, 0, 0))
    return pl.pallas_call(
        kern,
        grid=(b, HEADS // 2),
        in_specs=[pl.BlockSpec(memory_space=pltpu.SMEM), pl.BlockSpec(memory_space=pltpu.SMEM),
                  qkv_spec, qkv_spec, qkv_spec],
        out_specs=pl.BlockSpec((1, lp, LANE), lambda bi, hp: (bi, 0, hp)),
        out_shape=jax.ShapeDtypeStruct((b, lp, HEADS * MLA_V), jnp.bfloat16),
        scratch_shapes=[pltpu.VMEM((2, tq, LANE), jnp.float32)] * 2
        + [pltpu.VMEM((2, tq // tk, tq, LANE), jnp.float32),
           pltpu.VMEM((2, tq // tk, tq, tk), jnp.bfloat16)],
        compiler_params=pltpu.CompilerParams(
            dimension_semantics=("arbitrary", "arbitrary"), vmem_limit_bytes=VMEM_LIMIT),
        name="flash",
    )(gate_end, slack, q, k, v)


def _conv_in_kernel(h_ref, gmix_ref, win_ref, cw_ref, y_ref, gs_ref, *, tm):
    i = pl.program_id(1)

    @pl.when(i == 0)
    def _():
        gs_ref[0:8, :] = jnp.zeros((8, D_MODEL), jnp.float32)

    x = h_ref[0]
    hn = _rms(x, gmix_ref[...], D_MODEL).astype(jnp.bfloat16)
    gate_c = _dot(hn, win_ref[0, :, D_MODEL:2 * D_MODEL])
    u = _dot(hn, win_ref[0, :, 2 * D_MODEL:3 * D_MODEL])
    row = lax.broadcasted_iota(jnp.int32, (tm, D_MODEL), 0)
    g = jnp.where((i * tm + row) >= PAD, gate_c * u, 0.0)
    gs_ref[8:tm + 8, :] = g
    y = (cw_ref[0:1, :] * gs_ref[6:tm + 6, :] + cw_ref[1:2, :] * gs_ref[7:tm + 7, :]
         + cw_ref[2:3, :] * g)
    gs_ref[0:8, :] = gs_ref[tm:tm + 8, :]
    gate_b = _dot(hn, win_ref[0, :, 0:D_MODEL])
    y_ref[0] = (gate_b * y).astype(jnp.bfloat16)


def _conv_in(h, gmix, win, layer, cw, tm):
    b, lp, d = h.shape
    kern = functools.partial(_conv_in_kernel, tm=tm)
    return pl.pallas_call(
        kern,
        grid=(b, lp // tm),
        in_specs=[
            pl.BlockSpec((1, tm, d), lambda bi, i: (bi, i, 0)),
            _const_spec((1, d)),
            _layer_spec((d, 3 * d), layer),
            _const_spec((8, d)),
        ],
        out_specs=pl.BlockSpec((1, tm, d), lambda bi, i: (bi, i, 0)),
        out_shape=jax.ShapeDtypeStruct((b, lp, d), jnp.bfloat16),
        scratch_shapes=[pltpu.VMEM((tm + 8, d), jnp.float32)],
        compiler_params=pltpu.CompilerParams(
            dimension_semantics=("arbitrary", "arbitrary"), vmem_limit_bytes=VMEM_LIMIT),
        name="conv_in",
    )(h, gmix, win, cw)


def _mlp_tile(h, y, wo_ref, gmlp_ref, wup_ref, wdn_ref):
    h1 = h + _dot(y, wo_ref[0])
    n = _rms(h1, gmlp_ref[...], D_MODEL).astype(jnp.bfloat16)
    acc = h1
    for c in range(D_FF // FF_CHUNK):
        sl = slice(c * FF_CHUNK, (c + 1) * FF_CHUNK)
        a = jnp.maximum(_dot(n, wup_ref[0, :, sl]), 0.0)
        acc = acc + _dot((a * a).astype(jnp.bfloat16), wdn_ref[0, sl, :])
    return acc


def _mix_out_mlp_kernel(h_ref, y_ref, wo_ref, gmlp_ref, wup_ref, wdn_ref, out_ref):
    out_ref[...] = _mlp_tile(h_ref[...], y_ref[...], wo_ref, gmlp_ref, wup_ref, wdn_ref)


def _mix_out_mlp(h, y, wo, wo_layer, gmlp, wup, wdn, layer, tm):
    r, d = h.shape
    return pl.pallas_call(
        _mix_out_mlp_kernel,
        grid=(r // tm,),
        in_specs=[
            pl.BlockSpec((tm, d), lambda i: (i, 0)),
            pl.BlockSpec((tm, d), lambda i: (i, 0)),
            _layer_spec((d, d), wo_layer),
            _const_spec((1, d)),
            _layer_spec((d, D_FF), layer),
            _layer_spec((D_FF, d), layer),
        ],
        out_specs=pl.BlockSpec((tm, d), lambda i: (i, 0)),
        out_shape=jax.ShapeDtypeStruct((r, d), jnp.float32),
        compiler_params=pltpu.CompilerParams(
            dimension_semantics=("arbitrary",), vmem_limit_bytes=VMEM_LIMIT),
        name="mix_out_mlp",
    )(h, y, wo, gmlp, wup, wdn)


def _mix_out_mlp_last_kernel(*refs):
    h_parts, y_parts = refs[0:REAL_PARTS], refs[REAL_PARTS:2 * REAL_PARTS]
    wo_ref, gmlp_ref, wup_ref, wdn_ref, out_ref = refs[2 * REAL_PARTS:]
    h = jnp.concatenate([r[0] for r in h_parts], axis=0)
    y = jnp.concatenate([r[0] for r in y_parts], axis=0)
    out_ref[0] = _mlp_tile(h, y, wo_ref, gmlp_ref, wup_ref, wdn_ref)


def _mix_out_mlp_last(h, y, wo, wo_layer, gmlp, wup, wdn, layer, tm, seq):
    b, lp, d = h.shape
    part = tm // REAL_PARTS
    last_part = lp // part - 1

    def part_spec(k):
        return pl.BlockSpec(
            (1, part, d),
            lambda bi, i: (bi, jnp.minimum(REAL_START // part + REAL_PARTS * i + k, last_part), 0))

    parts = [part_spec(k) for k in range(REAL_PARTS)]
    return pl.pallas_call(
        _mix_out_mlp_last_kernel,
        grid=(b, pl.cdiv(seq, tm)),
        in_specs=parts + parts + [
            _layer_spec((d, d), wo_layer),
            _const_spec((1, d)),
            _layer_spec((d, D_FF), layer),
            _layer_spec((D_FF, d), layer),
        ],
        out_specs=pl.BlockSpec((1, tm, d), lambda bi, i: (bi, i, 0)),
        out_shape=jax.ShapeDtypeStruct((b, seq, d), jnp.float32),
        compiler_params=pltpu.CompilerParams(
            dimension_semantics=("arbitrary", "arbitrary"), vmem_limit_bytes=VMEM_LIMIT),
        name="mix_out_mlp_last",
    )(*([h] * REAL_PARTS + [y] * REAL_PARTS), wo, gmlp, wup, wdn)


def _pad_heads(w, heads, dim):
    k = w.shape[0]
    w = w.reshape(k, heads, dim)
    w = jnp.pad(w, ((0, 0), (0, 0), (0, LANE - dim)))
    return w.reshape(k, heads * LANE)


def _lane_vec(v, offset=0):
    return jnp.zeros((LANE,), jnp.float32).at[offset:offset + v.shape[0]].set(v)


def _attn_params(w_in, g_cq, w_uq, g_ckv, w_ukv, g_q_mla, g_k_mla, g_q_fox, g_k_fox, b_forget):
    bf = jnp.bfloat16
    o1 = Q_LORA
    o2 = o1 + KV_LORA
    o3 = o2 + MLA_ROPE
    o4 = o3 + FOX_HEADS * FOX_DIM
    o5 = o4 + FOX_HEADS * FOX_DIM
    o6 = o5 + FOX_HEADS * FOX_DIM
    misc = jnp.zeros((D_MODEL, LANE), jnp.float32)
    misc = misc.at[:, MISC_GATE:MISC_GATE + FOX_HEADS].set(w_in[:, o6:])
    misc = misc.at[:, MISC_ROPE:MISC_ROPE + MLA_ROPE].set(w_in[:, o2:o3])
    misc_sw = jnp.zeros((D_MODEL, LANE), jnp.float32)
    misc_sw = misc_sw.at[:, MISC_ROPE:MISC_ROPE + HALF_ROPE].set(w_in[:, o2 + HALF_ROPE:o3])
    misc_sw = misc_sw.at[:, MISC_ROPE + HALF_ROPE:MISC_ROPE + MLA_ROPE].set(
        w_in[:, o2:o2 + HALF_ROPE])
    wcat = jnp.concatenate([
        w_in[:, :o1], w_in[:, o1:o2],
        w_in[:, o3:o6], misc, misc_sw], axis=1).astype(bf)
    kv = w_ukv.reshape(KV_LORA, MLA_HEADS, MLA_NOPE + MLA_V)
    wkn = _pad_heads(kv[:, :, :MLA_NOPE].reshape(KV_LORA, -1), MLA_HEADS, MLA_NOPE).astype(bf)
    wv = jnp.pad(kv[:, :, MLA_NOPE:].reshape(KV_LORA, MLA_HEADS // 2, 2, MLA_V),
                 ((0, 0), (0, 0), (0, 0), (0, LANE - MLA_V)))
    wv = jnp.concatenate([wv[:, :, 0], jnp.roll(wv[:, :, 1], MLA_V, axis=-1)], axis=-1)
    wv = wv.reshape(KV_LORA, MLA_HEADS * LANE).astype(bf)
    lo, mid, hi = MLA_NOPE, MLA_NOPE + HALF_ROPE, MLA_NOPE + MLA_ROPE
    uq = w_uq.reshape(Q_LORA, MLA_HEADS, MLA_QK)
    uq_sw = jnp.zeros((Q_LORA, MLA_HEADS, LANE), jnp.float32)
    uq_sw = uq_sw.at[:, :, lo:mid].set(uq[:, :, mid:hi]).at[:, :, mid:hi].set(uq[:, :, lo:mid])
    wuq = jnp.concatenate([_pad_heads(w_uq, MLA_HEADS, MLA_QK),
                           uq_sw.reshape(Q_LORA, MLA_HEADS * LANE)], axis=1).astype(bf)

    def swapped(g):
        return jnp.zeros((LANE,), jnp.float32).at[lo:mid].set(g[mid:hi]).at[mid:hi].set(g[lo:mid])

    zero = jnp.zeros((LANE,), jnp.float32)
    q_scale_mla = MLA_QK ** -0.5 * LOG2E
    rows = [zero] * VEC_ROWS
    rows[V_GQ_MLA] = _lane_vec(g_q_mla) * q_scale_mla
    rows[V_GQ_MLA_SW] = swapped(g_q_mla) * q_scale_mla
    rows[V_GK_MLA] = _lane_vec(g_k_mla)
    rows[V_GK_MLA_SW] = swapped(g_k_mla)
    for par in range(2):
        feat, extra = FEATURE_BASE[par], EXTRA_BASE[par]
        rows[V_GQ_FOX + par] = _lane_vec(g_q_fox, feat) * (FOX_DIM ** -0.5 * LOG2E)
        rows[V_GK_FOX + par] = _lane_vec(g_k_fox, feat)
        rows[V_ADD_Q_FOX + par] = (zero.at[extra + N_SPLIT:extra + 2 * N_SPLIT].set(1.0)
                                   .at[extra + FLAG_FOX_OFF].set(1.0))
        rows[V_ONES_K_FOX + par] = zero.at[extra:extra + N_SPLIT].set(1.0)
        rows[V_ONES_V + par] = zero.at[extra:extra + FOX_DIM].set(1.0)
    rows[V_B_FORGET] = _lane_vec(b_forget, MISC_GATE)
    rows[V_ADD_Q_MLA] = zero.at[FLAG_MLA].set(1.0)
    vec = jnp.stack(rows)
    return dict(wcat=wcat, gcq=g_cq[None], wuq=wuq, gckv=g_ckv[None], wkn=wkn, wv=wv, vec=vec)


def _gate_selectors():
    selq = np.zeros((LANE, FOX_HEADS * LANE), np.float32)
    selk = np.zeros((LANE, FOX_HEADS * LANE), np.float32)
    for part in range(N_SPLIT):
        for hd in range(FOX_HEADS):
            extra = hd * LANE + EXTRA_BASE[hd % 2]
            selq[part * FOX_HEADS + hd, extra + part] = 1.0
            selk[part * FOX_HEADS + hd, extra + N_SPLIT + part] = -1.0
    return jnp.asarray(selq, jnp.bfloat16), jnp.asarray(selk, jnp.bfloat16)


def _rope_table(lp):
    lane = jnp.arange(LANE, dtype=jnp.int32)
    rotary = (lane >= MLA_NOPE) & (lane < MLA_NOPE + MLA_ROPE)
    first_half = rotary & (lane < MLA_NOPE + HALF_ROPE)
    pair = ((lane - MLA_NOPE) % HALF_ROPE).astype(jnp.float32)
    inv_freq = ROPE_BASE ** (-(2.0 * pair) / MLA_ROPE)
    pos = (jnp.arange(lp, dtype=jnp.int32) - PAD).astype(jnp.float32)
    ang = pos[:, None] * inv_freq[None, :]
    cos_t = jnp.where(lane < MLA_NOPE, 1.0, jnp.where(rotary, jnp.cos(ang), 0.0))
    sin_sw = jnp.where(rotary, jnp.where(first_half, -jnp.sin(ang), jnp.sin(ang)), 0.0)
    return jnp.concatenate([cos_t, sin_sw], axis=1)


def _pruning_tables(gate_end, g_q, g_k):
    b, nt = gate_end.shape[:2]
    fox = jnp.transpose(gate_end[:, :, 0, MISC_GATE:MISC_GATE + FOX_HEADS] * LOG2E, (0, 2, 1))
    table = jnp.concatenate([jnp.zeros((b, MLA_HEADS, nt), jnp.float32), fox], axis=1)
    bound = 1.02 * FOX_DIM * (FOX_DIM ** -0.5 * LOG2E) * jnp.max(jnp.abs(g_q)) * jnp.max(jnp.abs(g_k))
    slack_fox = -(2.0 * bound + UNDERFLOW_LOG2 + 4.0)
    slack = jnp.concatenate([jnp.full((MLA_HEADS,), NEG, jnp.float32),
                             jnp.full((FOX_HEADS,), slack_fox, jnp.float32)])
    return table, slack


def _token_tile(lp):
    if lp % FLASH_TQ:
        raise ValueError(f"padded length {lp} is not a multiple of {FLASH_TQ}")
    return FLASH_TQ


def kernel(x, meta_tokens, g_mix, g_mlp, w_in_attn, g_cq, w_uq, g_ckv, w_ukv, g_q_mla, g_k_mla,
           g_q_fox, g_k_fox, b_forget, w_out_attn, w_in_conv, conv_w, w_out_conv, w_mlp_up,
           w_mlp_down):
    b, seq, d = x.shape
    assert d == D_MODEL and (PAD + N_META + seq) % BLOCK == 0
    lp = PAD + N_META + seq
    tm = _token_tile(lp)
    bf = jnp.bfloat16

    meta = jnp.broadcast_to(meta_tokens.astype(x.dtype)[None], (b, N_META, d))
    h = jnp.concatenate([jnp.zeros((b, PAD, d), x.dtype), meta, x], axis=1)

    rope_tab = _rope_table(lp)
    tri = (jnp.arange(tm)[:, None] >= jnp.arange(tm)[None, :]).astype(bf)
    selq, selk = _gate_selectors()

    wo_attn, wo_conv, w_conv = w_out_attn.astype(bf), w_out_conv.astype(bf), w_in_conv.astype(bf)
    w_up, w_down = w_mlp_up.astype(bf), w_mlp_down.astype(bf)
    for layer in range(DEPTH):
        j = layer // 2
        gmix = g_mix[layer][None]
        if layer % 2 == 0:
            p = _attn_params(w_in_attn[j], g_cq[j], w_uq[j], g_ckv[j], w_ukv[j], g_q_mla[j],
                             g_k_mla[j], g_q_fox[j], g_k_fox[j], b_forget[j])
            q, k, v, gate_end = _attn_in(h, gmix, p, rope_tab, tri, selq, selk, tm)
            y = _flash(*_pruning_tables(gate_end, g_q_fox[j], g_k_fox[j]), q, k, v)
            wo = wo_attn
        else:
            cw = jnp.zeros((8, d), jnp.float32).at[0:3].set(conv_w[j])
            y = _conv_in(h, gmix, w_conv, j, cw, tm)
            wo = wo_conv
        if layer == DEPTH - 1:
            return _mix_out_mlp_last(h, y, wo, j, g_mlp[layer][None], w_up, w_down, layer, tm, seq)
        h = _mix_out_mlp(h.reshape(b * lp, d), y.reshape(b * lp, d), wo, j, g_mlp[layer][None],
                         w_up, w_down, layer, tm).reshape(b, lp, d)
```

```python
import functools

import numpy as np
import jax
import jax.numpy as jnp
from jax import lax
from jax.experimental import pallas as pl
from jax.experimental.pallas import tpu as pltpu

D_MODEL = 1024
DEPTH = 4
N_META = 16
BLOCK = 128
PAD = 2 * BLOCK - N_META
REAL_START = PAD + N_META
REAL_PARTS = 3
MLA_HEADS = 8
MLA_NOPE = 64
MLA_ROPE = 32
MLA_QK = MLA_NOPE + MLA_ROPE
MLA_V = 64
Q_LORA = 384
KV_LORA = 256
ROPE_BASE = 10000.0
FOX_HEADS = 8
FOX_DIM = 64
D_FF = 4 * D_MODEL
EPS = 1e-6
NEG = -1e30

LANE = 128
HEADS = MLA_HEADS + FOX_HEADS
HALF_ROPE = MLA_ROPE // 2
FEATURE_BASE = (0, FOX_DIM)
EXTRA_BASE = (FOX_DIM, 0)
N_SPLIT = 3
FLAG_FOX_OFF = 2 * N_SPLIT
FLAG_MLA = MLA_QK
PAD_KEY = NEG
LOG2E = 1.4426950408889634
MISC_GATE = 0
MISC_ROPE = MLA_NOPE

OFF_CQ = 0
OFF_CKV = OFF_CQ + Q_LORA
OFF_FQ = OFF_CKV + KV_LORA
OFF_FK = OFF_FQ + FOX_HEADS * FOX_DIM
OFF_FV = OFF_FK + FOX_HEADS * FOX_DIM
OFF_MISC = OFF_FV + FOX_HEADS * FOX_DIM
OFF_MISC_SW = OFF_MISC + LANE
W_CAT = OFF_MISC_SW + LANE

(V_GQ_MLA, V_GQ_MLA_SW, V_GK_MLA, V_GK_MLA_SW, V_ADD_Q_MLA, V_B_FORGET) = range(6)
V_GQ_FOX, V_GK_FOX, V_ADD_Q_FOX, V_ONES_K_FOX, V_ONES_V = 6, 8, 10, 12, 14
VEC_ROWS = 16
PAIR = 2 * LANE

FF_CHUNK = 1024
UNDERFLOW_LOG2 = 150.0
FLASH_TQ = 768
FLASH_TK = 384
VMEM_LIMIT = 56 * 1024 * 1024


def _const_spec(shape):
    nd = len(shape)
    return pl.BlockSpec(shape, lambda *_: (0,) * nd, pipeline_mode=pl.Buffered(1))


def _layer_spec(shape, layer):
    nd = len(shape)
    return pl.BlockSpec((1,) + shape, lambda *_: (layer,) + (0,) * nd,
                        pipeline_mode=pl.Buffered(1))


def _input_specs(tm):
    part = tm // REAL_PARTS
    assert REAL_START == part

    def part_spec(k):
        return pl.BlockSpec((1, part, D_MODEL),
                            lambda bi, i: (bi, jnp.maximum(REAL_PARTS * i + k - 1, 0), 0))

    return [part_spec(k) for k in range(REAL_PARTS)] + [_const_spec((N_META, D_MODEL))]


def _input_tile(i, x_parts, meta_ref):
    lead = jnp.concatenate([jnp.zeros((PAD, D_MODEL), jnp.float32), meta_ref[...]], axis=0)
    first = jnp.where(i == 0, lead, x_parts[0][0])
    return jnp.concatenate([first] + [r[0] for r in x_parts[1:]], axis=0)


def _rms(x, g, n):
    ms = jnp.sum(x * x, axis=-1, keepdims=True) * (1.0 / n)
    return x * lax.rsqrt(ms + EPS) * g


def _split3(x):
    hi = x.astype(jnp.bfloat16).astype(jnp.float32)
    r1 = x - hi
    mid = r1.astype(jnp.bfloat16).astype(jnp.float32)
    lo = r1 - mid
    packed = hi + pltpu.roll(mid, FOX_HEADS, 1) + pltpu.roll(lo, 2 * FOX_HEADS, 1)
    return packed.astype(jnp.bfloat16)


def _dot(a, b):
    return jnp.dot(a, b, preferred_element_type=jnp.float32)


def _attn_in_kernel(*refs, tm, from_x):
    n_stream = REAL_PARTS + 1 if from_x else 1
    stream = refs[:n_stream]
    (gmix_ref, wcat_ref, gcq_ref, wuq_ref, gckv_ref, wkn_ref, wv_ref, vec_ref, rope_ref, tri_ref,
     selq_ref, selk_ref, q_ref, k_ref, v_ref, gate_end_ref, carry_ref) = refs[n_stream:]
    i = pl.program_id(1)

    @pl.when(i == 0)
    def _():
        carry_ref[...] = jnp.zeros_like(carry_ref)

    x = _input_tile(i, stream[:-1], stream[-1]) if from_x else stream[0][0]
    hn = _rms(x, gmix_ref[...], D_MODEL).astype(jnp.bfloat16)

    def seg(lo, width):
        return _dot(hn, wcat_ref[:, lo:lo + width])

    def vec(r):
        return vec_ref[r:r + 1, :]

    cos_t = rope_ref[:, 0:LANE]
    sin_sw = rope_ref[:, LANE:2 * LANE]
    gc_q, gs_q = vec(V_GQ_MLA) * cos_t, vec(V_GQ_MLA_SW) * sin_sw
    gc_k, gs_k = vec(V_GK_MLA) * cos_t, vec(V_GK_MLA_SW) * sin_sw
    add_q_mla = vec(V_ADD_Q_MLA)

    lane = lax.broadcasted_iota(jnp.int32, (tm, LANE), 1)
    row = lax.broadcasted_iota(jnp.int32, (tm, LANE), 0)
    valid = (i * tm + row) >= PAD
    pad_key = jnp.where(valid, 0.0, PAD_KEY)
    add_k_mla = jnp.where(lane == FLAG_MLA, pad_key, 0.0)
    halves = (lane < FOX_DIM, lane >= FOX_DIM)
    add_k_fox = [vec(V_ONES_K_FOX + par)
                 + jnp.where(lane == EXTRA_BASE[par] + FLAG_FOX_OFF, pad_key, 0.0)
                 for par in range(2)]

    misc2 = seg(OFF_MISC, PAIR)
    misc = misc2[:, 0:LANE]
    kpe = jnp.where((lane >= MISC_ROPE) & (lane < MISC_ROPE + MLA_ROPE), misc, 0.0)
    k_rot = misc2[:, LANE:PAIR] * gs_k
    xl = misc + vec(V_B_FORGET)
    logf = jnp.minimum(xl, 0.0) - jnp.log1p(jnp.exp(-jnp.abs(xl)))
    logf = jnp.where(valid & (lane >= MISC_GATE) & (lane < MISC_GATE + FOX_HEADS), logf, 0.0)
    cs = _dot(tri_ref[...], _split3(logf))
    cs = (cs + pltpu.roll(cs, LANE - FOX_HEADS, 1)) + pltpu.roll(cs, LANE - 2 * FOX_HEADS, 1)
    cum = jnp.where(lane < FOX_HEADS, cs, 0.0) + carry_ref[0:1, :]
    carry_ref[0:1, :] = cum[tm - 1:tm, :]
    gate_end_ref[0, 0] = carry_ref[...]
    cum3 = _split3(cum * LOG2E)
    gate_q = _dot(cum3, selq_ref[...])
    gate_k = _dot(cum3, selk_ref[...])

    def inv_rms(xv, n):
        return lax.rsqrt(jnp.sum(xv * xv, axis=-1, keepdims=True) * (1.0 / n) + EPS)

    cqn = _rms(seg(OFF_CQ, Q_LORA), gcq_ref[...], Q_LORA).astype(jnp.bfloat16)
    ckvn = _rms(seg(OFF_CKV, KV_LORA), gckv_ref[...], KV_LORA).astype(jnp.bfloat16)
    for g in range(MLA_HEADS // 2):
        cols = slice(g * PAIR, (g + 1) * PAIR)
        cols_sw = slice(MLA_HEADS * LANE + g * PAIR, MLA_HEADS * LANE + (g + 1) * PAIR)
        xq2 = _dot(cqn, wuq_ref[:, cols])
        xq2_sw = _dot(cqn, wuq_ref[:, cols_sw])
        xk2 = _dot(ckvn, wkn_ref[:, cols])
        xv2 = _dot(ckvn, wv_ref[:, cols])
        for e in range(2):
            hd, sl = 2 * g + e, slice(e * LANE, (e + 1) * LANE)
            xq = xq2[:, sl]
            q_ref[0, hd] = ((xq * gc_q + xq2_sw[:, sl] * gs_q) * inv_rms(xq, MLA_QK) + add_q_mla
                            ).astype(jnp.bfloat16)
            xk = xk2[:, sl] + kpe
            k_ref[0, hd] = ((xk * gc_k + k_rot) * inv_rms(xk, MLA_QK) + add_k_mla
                            ).astype(jnp.bfloat16)
            v_ref[0, hd] = (xv2[:, sl] + vec(V_ONES_V + e)).astype(jnp.bfloat16)

    for g in range(FOX_HEADS // 4):
        xq4 = seg(OFF_FQ + g * PAIR, PAIR)
        xk4 = seg(OFF_FK + g * PAIR, PAIR)
        xv4 = seg(OFF_FV + g * PAIR, PAIR)
        for e in range(4):
            hd, par = 4 * g + e, e % 2
            sl = slice((e // 2) * LANE, (e // 2 + 1) * LANE)
            gl = slice(hd * LANE, (hd + 1) * LANE)
            xq, xk = xq4[:, sl], xk4[:, sl]
            rq = inv_rms(jnp.where(halves[par], xq, 0.0), FOX_DIM)
            rk = inv_rms(jnp.where(halves[par], xk, 0.0), FOX_DIM)
            q_ref[0, MLA_HEADS + hd] = (xq * vec(V_GQ_FOX + par) * rq + gate_q[:, gl]
                                        + vec(V_ADD_Q_FOX + par)).astype(jnp.bfloat16)
            k_ref[0, MLA_HEADS + hd] = (xk * vec(V_GK_FOX + par) * rk + gate_k[:, gl]
                                        + add_k_fox[par]).astype(jnp.bfloat16)
            v_ref[0, MLA_HEADS + hd] = (jnp.where(halves[par], xv4[:, sl], 0.0)
                                        + vec(V_ONES_V + par)).astype(jnp.bfloat16)


def _attn_in(stream, lp, gmix, p, rope_tab, tri, selq, selk, tm):
    from_x = len(stream) == 2
    b, d = stream[0].shape[0], D_MODEL
    nt = lp // tm
    kern = functools.partial(_attn_in_kernel, tm=tm, from_x=from_x)
    stream_specs = (_input_specs(tm) if from_x
                    else [pl.BlockSpec((1, tm, d), lambda bi, i: (bi, i, 0))])
    stream_args = [stream[0]] * REAL_PARTS + [stream[1]] if from_x else [stream[0]]
    qk_shape = jax.ShapeDtypeStruct((b, HEADS, lp, LANE), jnp.bfloat16)
    qk_spec = pl.BlockSpec((1, HEADS, tm, LANE), lambda bi, i: (bi, 0, i, 0))
    return pl.pallas_call(
        kern,
        grid=(b, nt),
        in_specs=stream_specs + [
            _const_spec((1, d)),
            _const_spec((d, W_CAT)),
            _const_spec((1, Q_LORA)),
            _const_spec((Q_LORA, 2 * MLA_HEADS * LANE)),
            _const_spec((1, KV_LORA)),
            _const_spec((KV_LORA, MLA_HEADS * LANE)),
            _const_spec((KV_LORA, MLA_HEADS * LANE)),
            _const_spec((VEC_ROWS, LANE)),
            pl.BlockSpec((tm, 2 * LANE), lambda bi, i: (i, 0)),
            _const_spec((tm, tm)),
            _const_spec((LANE, FOX_HEADS * LANE)),
            _const_spec((LANE, FOX_HEADS * LANE)),
        ],
        out_specs=[qk_spec, qk_spec, qk_spec,
                   pl.BlockSpec((1, 1, 8, LANE), lambda bi, i: (bi, i, 0, 0))],
        out_shape=[qk_shape, qk_shape, qk_shape,
                   jax.ShapeDtypeStruct((b, nt, 8, LANE), jnp.float32)],
        scratch_shapes=[pltpu.VMEM((8, LANE), jnp.float32)],
        compiler_params=pltpu.CompilerParams(
            dimension_semantics=("arbitrary", "arbitrary"), vmem_limit_bytes=VMEM_LIMIT),
        name="attn_in",
    )(*stream_args, gmix, p["wcat"], p["gcq"], p["wuq"], p["gckv"], p["wkn"], p["wv"], p["vec"],
      rope_tab, tri, selq, selk)


def _flash_kernel(gate_end_ref, slack_ref, q_ref, k_ref, v_ref, o_ref, m_ref, acc_ref, al_ref,
                  p_ref, *, tq, tk, nq):
    chunks = tq // tk
    bi, hp = pl.program_id(0), pl.program_id(1)

    def query_block(qi, carry):
        qbase = qi * tq

        def softmax(j, u, base, diagonal, first=False):
            r0 = u * tk if diagonal else 0
            rows = slice(r0, tq)
            start = pl.multiple_of(base + u * tk, tk)
            q_rows = pl.ds(pl.multiple_of(qbase + r0, tk), tq - r0)
            s = lax.dot_general(q_ref[0, j, q_rows, :], k_ref[0, j, pl.ds(start, tk), :],
                                (((1,), (1,)), ((), ())), preferred_element_type=jnp.float32)
            if diagonal:
                row = lax.broadcasted_iota(jnp.int32, (tq - r0, tk), 0)
                col = lax.broadcasted_iota(jnp.int32, (tq - r0, tk), 1)
                s = jnp.where(col <= row, s, NEG)
            if first:
                m_next = jnp.broadcast_to(jnp.max(s, axis=1, keepdims=True), (tq - r0, LANE))
            else:
                m_prev = m_ref[j, rows, :]
                m_next = jnp.maximum(m_prev, jnp.max(s, axis=1, keepdims=True))
                al_ref[j, u, rows, :] = jnp.exp2(m_prev - m_next)
            p = jnp.exp2(s - jnp.concatenate([m_next] * (tk // LANE), axis=1))
            p_ref[j, u, rows, :] = p.astype(jnp.bfloat16)
            m_ref[j, rows, :] = m_next

        def pv(j, u, base, diagonal, first=False):
            r0 = u * tk if diagonal else 0
            rows = slice(r0, tq)
            start = pl.multiple_of(base + u * tk, tk)
            new = _dot(p_ref[j, u, rows, :], v_ref[0, j, pl.ds(start, tk), :])
            if first:
                acc_ref[j, rows, :] = new
            else:
                acc_ref[j, rows, :] = acc_ref[j, rows, :] * al_ref[j, u, rows, :] + new

        def diagonal_block():
            for u in range(chunks):
                softmax(0, u, qbase, True, first=(u == 0))
                if u > 0:
                    pv(1, u - 1, qbase, True, first=(u == 1))
                softmax(1, u, qbase, True, first=(u == 0))
                pv(0, u, qbase, True, first=(u == 0))
            pv(1, chunks - 1, qbase, True, first=(chunks == 1))

        def block(kb, diagonal):
            base = kb * tq
            for u in range(chunks):
                softmax(0, u, base, diagonal)
                if u == 0:
                    pv(1, chunks - 1, jnp.maximum(kb - 1, 0) * tq, False)
                else:
                    pv(1, u - 1, base, diagonal)
                softmax(1, u, base, diagonal)
                pv(0, u, base, diagonal)

        def blocks_needed(j):
            hd = 2 * hp + j
            gate_q = gate_end_ref[bi, hd, jnp.maximum(qi - 1, 0)]
            count = jnp.int32(0)
            for kb in range(nq - 1):
                keep = (kb < qi) & (gate_q - gate_end_ref[bi, hd, kb] >= slack_ref[hd])
                count = count + keep.astype(jnp.int32)
            return count

        diagonal_block()
        al_ref[1, chunks - 1] = jnp.ones((tq, LANE), jnp.float32)
        p_ref[1, chunks - 1] = jnp.zeros((tq, tk), jnp.bfloat16)
        n_off = jnp.maximum(blocks_needed(0), blocks_needed(1))
        first = qi - n_off
        odd = n_off & 1

        @pl.when(odd == 1)
        def _():
            block(first, False)

        def body(pair, c):
            kb = first + odd + 2 * pair
            block(kb, False)
            block(kb + 1, False)
            return c

        lax.fori_loop(0, lax.shift_right_logical(n_off, 1), body, 0)
        pv(1, chunks - 1, jnp.maximum(qi - 1, 0) * tq, False)

        o0 = acc_ref[0]
        o1 = acc_ref[1]
        o0 = o0 / pltpu.roll(o0, FOX_DIM, 1)
        o1 = o1 / pltpu.roll(o1, FOX_DIM, 1)
        lane = lax.broadcasted_iota(jnp.int32, (tq, LANE), 1)
        o = jnp.where(lane < MLA_V, o0, o1)
        o_ref[0, pl.ds(pl.multiple_of(qbase, tq), tq), :] = o.astype(jnp.bfloat16)
        return carry

    lax.fori_loop(0, nq, query_block, 0)


def _flash(gate_end, slack, q, k, v):
    b, _, lp, _ = q.shape
    tq, tk = FLASH_TQ, FLASH_TK
    kern = functools.partial(_flash_kernel, tq=tq, tk=tk, nq=lp // tq)
    qkv_spec = pl.BlockSpec((1, 2, lp, LANE), lambda bi, hp: (bi, hp, 0, 0))
    return pl.pallas_call(
        kern,
        grid=(b, HEADS // 2),
        in_specs=[pl.BlockSpec(memory_space=pltpu.SMEM), pl.BlockSpec(memory_space=pltpu.SMEM),
                  qkv_spec, qkv_spec, qkv_spec],
        out_specs=pl.BlockSpec((1, lp, LANE), lambda bi, hp: (bi, 0, hp)),
        out_shape=jax.ShapeDtypeStruct((b, lp, HEADS * MLA_V), jnp.bfloat16),
        scratch_shapes=[pltpu.VMEM((2, tq, LANE), jnp.float32)] * 2
        + [pltpu.VMEM((2, tq // tk, tq, LANE), jnp.float32),
           pltpu.VMEM((2, tq // tk, tq, tk), jnp.bfloat16)],
        compiler_params=pltpu.CompilerParams(
            dimension_semantics=("arbitrary", "arbitrary"), vmem_limit_bytes=VMEM_LIMIT),
        name="flash",
    )(gate_end, slack, q, k, v)


def _conv_in_kernel(h_ref, gmix_ref, win_ref, cw_ref, y_ref, gs_ref, *, tm):
    i = pl.program_id(1)

    @pl.when(i == 0)
    def _():
        gs_ref[0:8, :] = jnp.zeros((8, D_MODEL), jnp.float32)

    x = h_ref[0]
    hn = _rms(x, gmix_ref[...], D_MODEL).astype(jnp.bfloat16)
    gate_c = _dot(hn, win_ref[0, :, D_MODEL:2 * D_MODEL])
    u = _dot(hn, win_ref[0, :, 2 * D_MODEL:3 * D_MODEL])
    row = lax.broadcasted_iota(jnp.int32, (tm, D_MODEL), 0)
    g = jnp.where((i * tm + row) >= PAD, gate_c * u, 0.0)
    gs_ref[8:tm + 8, :] = g
    y = (cw_ref[0:1, :] * gs_ref[6:tm + 6, :] + cw_ref[1:2, :] * gs_ref[7:tm + 7, :]
         + cw_ref[2:3, :] * g)
    gs_ref[0:8, :] = gs_ref[tm:tm + 8, :]
    gate_b = _dot(hn, win_ref[0, :, 0:D_MODEL])
    y_ref[0] = (gate_b * y).astype(jnp.bfloat16)


def _conv_in(h, gmix, win, layer, cw, tm):
    b, lp, d = h.shape
    kern = functools.partial(_conv_in_kernel, tm=tm)
    return pl.pallas_call(
        kern,
        grid=(b, lp // tm),
        in_specs=[
            pl.BlockSpec((1, tm, d), lambda bi, i: (bi, i, 0)),
            _const_spec((1, d)),
            _layer_spec((d, 3 * d), layer),
            _const_spec((8, d)),
        ],
        out_specs=pl.BlockSpec((1, tm, d), lambda bi, i: (bi, i, 0)),
        out_shape=jax.ShapeDtypeStruct((b, lp, d), jnp.bfloat16),
        scratch_shapes=[pltpu.VMEM((tm + 8, d), jnp.float32)],
        compiler_params=pltpu.CompilerParams(
            dimension_semantics=("arbitrary", "arbitrary"), vmem_limit_bytes=VMEM_LIMIT),
        name="conv_in",
    )(h, gmix, win, cw)


def _mlp_tile(h, y, wo_ref, gmlp_ref, wup_ref, wdn_ref):
    h1 = h + _dot(y, wo_ref[0])
    n = _rms(h1, gmlp_ref[...], D_MODEL).astype(jnp.bfloat16)
    acc = h1
    for c in range(D_FF // FF_CHUNK):
        sl = slice(c * FF_CHUNK, (c + 1) * FF_CHUNK)
        a = jnp.maximum(_dot(n, wup_ref[0, :, sl]), 0.0)
        acc = acc + _dot((a * a).astype(jnp.bfloat16), wdn_ref[0, sl, :])
    return acc


def _mix_out_mlp_kernel(h_ref, y_ref, wo_ref, gmlp_ref, wup_ref, wdn_ref, out_ref):
    out_ref[...] = _mlp_tile(h_ref[...], y_ref[...], wo_ref, gmlp_ref, wup_ref, wdn_ref)


def _mix_out_mlp(h, y, wo, wo_layer, gmlp, wup, wdn, layer, tm):
    r, d = h.shape
    return pl.pallas_call(
        _mix_out_mlp_kernel,
        grid=(r // tm,),
        in_specs=[
            pl.BlockSpec((tm, d), lambda i: (i, 0)),
            pl.BlockSpec((tm, d), lambda i: (i, 0)),
            _layer_spec((d, d), wo_layer),
            _const_spec((1, d)),
            _layer_spec((d, D_FF), layer),
            _layer_spec((D_FF, d), layer),
        ],
        out_specs=pl.BlockSpec((tm, d), lambda i: (i, 0)),
        out_shape=jax.ShapeDtypeStruct((r, d), jnp.float32),
        compiler_params=pltpu.CompilerParams(
            dimension_semantics=("arbitrary",), vmem_limit_bytes=VMEM_LIMIT),
        name="mix_out_mlp",
    )(h, y, wo, gmlp, wup, wdn)


def _mix_out_mlp_first_kernel(*refs):
    x_parts, meta_ref = refs[0:REAL_PARTS], refs[REAL_PARTS]
    y_ref, wo_ref, gmlp_ref, wup_ref, wdn_ref, out_ref = refs[REAL_PARTS + 1:]
    h = _input_tile(pl.program_id(1), x_parts, meta_ref)
    out_ref[0] = _mlp_tile(h, y_ref[0], wo_ref, gmlp_ref, wup_ref, wdn_ref)


def _mix_out_mlp_first(x, meta, y, wo, wo_layer, gmlp, wup, wdn, layer, tm):
    b, lp, d = y.shape
    tile = pl.BlockSpec((1, tm, d), lambda bi, i: (bi, i, 0))
    return pl.pallas_call(
        _mix_out_mlp_first_kernel,
        grid=(b, lp // tm),
        in_specs=_input_specs(tm) + [
            tile,
            _layer_spec((d, d), wo_layer),
            _const_spec((1, d)),
            _layer_spec((d, D_FF), layer),
            _layer_spec((D_FF, d), layer),
        ],
        out_specs=tile,
        out_shape=jax.ShapeDtypeStruct((b, lp, d), jnp.float32),
        compiler_params=pltpu.CompilerParams(
            dimension_semantics=("arbitrary", "arbitrary"), vmem_limit_bytes=VMEM_LIMIT),
        name="mix_out_mlp_first",
    )(*([x] * REAL_PARTS), meta, y, wo, gmlp, wup, wdn)


def _mix_out_mlp_last_kernel(*refs):
    h_parts, y_parts = refs[0:REAL_PARTS], refs[REAL_PARTS:2 * REAL_PARTS]
    wo_ref, gmlp_ref, wup_ref, wdn_ref, out_ref = refs[2 * REAL_PARTS:]
    h = jnp.concatenate([r[0] for r in h_parts], axis=0)
    y = jnp.concatenate([r[0] for r in y_parts], axis=0)
    out_ref[0] = _mlp_tile(h, y, wo_ref, gmlp_ref, wup_ref, wdn_ref)


def _mix_out_mlp_last(h, y, wo, wo_layer, gmlp, wup, wdn, layer, tm, seq):
    b, lp, d = h.shape
    part = tm // REAL_PARTS
    last_part = lp // part - 1

    def part_spec(k):
        return pl.BlockSpec(
            (1, part, d),
            lambda bi, i: (bi, jnp.minimum(REAL_START // part + REAL_PARTS * i + k, last_part), 0))

    parts = [part_spec(k) for k in range(REAL_PARTS)]
    return pl.pallas_call(
        _mix_out_mlp_last_kernel,
        grid=(b, pl.cdiv(seq, tm)),
        in_specs=parts + parts + [
            _layer_spec((d, d), wo_layer),
            _const_spec((1, d)),
            _layer_spec((d, D_FF), layer),
            _layer_spec((D_FF, d), layer),
        ],
        out_specs=pl.BlockSpec((1, tm, d), lambda bi, i: (bi, i, 0)),
        out_shape=jax.ShapeDtypeStruct((b, seq, d), jnp.float32),
        compiler_params=pltpu.CompilerParams(
            dimension_semantics=("arbitrary", "arbitrary"), vmem_limit_bytes=VMEM_LIMIT),
        name="mix_out_mlp_last",
    )(*([h] * REAL_PARTS + [y] * REAL_PARTS), wo, gmlp, wup, wdn)


def _pad_heads(w, heads, dim):
    k = w.shape[0]
    w = w.reshape(k, heads, dim)
    w = jnp.pad(w, ((0, 0), (0, 0), (0, LANE - dim)))
    return w.reshape(k, heads * LANE)


def _lane_vec(v, offset=0):
    return jnp.zeros((LANE,), jnp.float32).at[offset:offset + v.shape[0]].set(v)


def _attn_params(w_in, g_cq, w_uq, g_ckv, w_ukv, g_q_mla, g_k_mla, g_q_fox, g_k_fox, b_forget):
    bf = jnp.bfloat16
    o1 = Q_LORA
    o2 = o1 + KV_LORA
    o3 = o2 + MLA_ROPE
    o4 = o3 + FOX_HEADS * FOX_DIM
    o5 = o4 + FOX_HEADS * FOX_DIM
    o6 = o5 + FOX_HEADS * FOX_DIM
    misc = jnp.zeros((D_MODEL, LANE), jnp.float32)
    misc = misc.at[:, MISC_GATE:MISC_GATE + FOX_HEADS].set(w_in[:, o6:])
    misc = misc.at[:, MISC_ROPE:MISC_ROPE + MLA_ROPE].set(w_in[:, o2:o3])
    misc_sw = jnp.zeros((D_MODEL, LANE), jnp.float32)
    misc_sw = misc_sw.at[:, MISC_ROPE:MISC_ROPE + HALF_ROPE].set(w_in[:, o2 + HALF_ROPE:o3])
    misc_sw = misc_sw.at[:, MISC_ROPE + HALF_ROPE:MISC_ROPE + MLA_ROPE].set(
        w_in[:, o2:o2 + HALF_ROPE])
    wcat = jnp.concatenate([
        w_in[:, :o1], w_in[:, o1:o2],
        w_in[:, o3:o6], misc, misc_sw], axis=1).astype(bf)
    kv = w_ukv.reshape(KV_LORA, MLA_HEADS, MLA_NOPE + MLA_V)
    wkn = _pad_heads(kv[:, :, :MLA_NOPE].reshape(KV_LORA, -1), MLA_HEADS, MLA_NOPE).astype(bf)
    wv = jnp.pad(kv[:, :, MLA_NOPE:].reshape(KV_LORA, MLA_HEADS // 2, 2, MLA_V),
                 ((0, 0), (0, 0), (0, 0), (0, LANE - MLA_V)))
    wv = jnp.concatenate([wv[:, :, 0], jnp.roll(wv[:, :, 1], MLA_V, axis=-1)], axis=-1)
    wv = wv.reshape(KV_LORA, MLA_HEADS * LANE).astype(bf)
    lo, mid, hi = MLA_NOPE, MLA_NOPE + HALF_ROPE, MLA_NOPE + MLA_ROPE
    uq = w_uq.reshape(Q_LORA, MLA_HEADS, MLA_QK)
    uq_sw = jnp.zeros((Q_LORA, MLA_HEADS, LANE), jnp.float32)
    uq_sw = uq_sw.at[:, :, lo:mid].set(uq[:, :, mid:hi]).at[:, :, mid:hi].set(uq[:, :, lo:mid])
    wuq = jnp.concatenate([_pad_heads(w_uq, MLA_HEADS, MLA_QK),
                           uq_sw.reshape(Q_LORA, MLA_HEADS * LANE)], axis=1).astype(bf)

    def swapped(g):
        return jnp.zeros((LANE,), jnp.float32).at[lo:mid].set(g[mid:hi]).at[mid:hi].set(g[lo:mid])

    zero = jnp.zeros((LANE,), jnp.float32)
    q_scale_mla = MLA_QK ** -0.5 * LOG2E
    rows = [zero] * VEC_ROWS
    rows[V_GQ_MLA] = _lane_vec(g_q_mla) * q_scale_mla
    rows[V_GQ_MLA_SW] = swapped(g_q_mla) * q_scale_mla
    rows[V_GK_MLA] = _lane_vec(g_k_mla)
    rows[V_GK_MLA_SW] = swapped(g_k_mla)
    for par in range(2):
        feat, extra = FEATURE_BASE[par], EXTRA_BASE[par]
        rows[V_GQ_FOX + par] = _lane_vec(g_q_fox, feat) * (FOX_DIM ** -0.5 * LOG2E)
        rows[V_GK_FOX + par] = _lane_vec(g_k_fox, feat)
        rows[V_ADD_Q_FOX + par] = (zero.at[extra + N_SPLIT:extra + 2 * N_SPLIT].set(1.0)
                                   .at[extra + FLAG_FOX_OFF].set(1.0))
        rows[V_ONES_K_FOX + par] = zero.at[extra:extra + N_SPLIT].set(1.0)
        rows[V_ONES_V + par] = zero.at[extra:extra + FOX_DIM].set(1.0)
    rows[V_B_FORGET] = _lane_vec(b_forget, MISC_GATE)
    rows[V_ADD_Q_MLA] = zero.at[FLAG_MLA].set(1.0)
    vec = jnp.stack(rows)
    return dict(wcat=wcat, gcq=g_cq[None], wuq=wuq, gckv=g_ckv[None], wkn=wkn, wv=wv, vec=vec)


def _gate_selectors():
    selq = np.zeros((LANE, FOX_HEADS * LANE), np.float32)
    selk = np.zeros((LANE, FOX_HEADS * LANE), np.float32)
    for part in range(N_SPLIT):
        for hd in range(FOX_HEADS):
            extra = hd * LANE + EXTRA_BASE[hd % 2]
            selq[part * FOX_HEADS + hd, extra + part] = 1.0
            selk[part * FOX_HEADS + hd, extra + N_SPLIT + part] = -1.0
    return jnp.asarray(selq, jnp.bfloat16), jnp.asarray(selk, jnp.bfloat16)


def _rope_table(lp):
    lane = jnp.arange(LANE, dtype=jnp.int32)
    rotary = (lane >= MLA_NOPE) & (lane < MLA_NOPE + MLA_ROPE)
    first_half = rotary & (lane < MLA_NOPE + HALF_ROPE)
    pair = ((lane - MLA_NOPE) % HALF_ROPE).astype(jnp.float32)
    inv_freq = ROPE_BASE ** (-(2.0 * pair) / MLA_ROPE)
    pos = (jnp.arange(lp, dtype=jnp.int32) - PAD).astype(jnp.float32)
    ang = pos[:, None] * inv_freq[None, :]
    cos_t = jnp.where(lane < MLA_NOPE, 1.0, jnp.where(rotary, jnp.cos(ang), 0.0))
    sin_sw = jnp.where(rotary, jnp.where(first_half, -jnp.sin(ang), jnp.sin(ang)), 0.0)
    return jnp.concatenate([cos_t, sin_sw], axis=1)


def _pruning_tables(gate_end, g_q, g_k):
    b, nt = gate_end.shape[:2]
    fox = jnp.transpose(gate_end[:, :, 0, MISC_GATE:MISC_GATE + FOX_HEADS] * LOG2E, (0, 2, 1))
    table = jnp.concatenate([jnp.zeros((b, MLA_HEADS, nt), jnp.float32), fox], axis=1)
    bound = 1.02 * FOX_DIM * (FOX_DIM ** -0.5 * LOG2E) * jnp.max(jnp.abs(g_q)) * jnp.max(jnp.abs(g_k))
    slack_fox = -(2.0 * bound + UNDERFLOW_LOG2 + 4.0)
    slack = jnp.concatenate([jnp.full((MLA_HEADS,), NEG, jnp.float32),
                             jnp.full((FOX_HEADS,), slack_fox, jnp.float32)])
    return table, slack


def _token_tile(lp):
    if lp % FLASH_TQ:
        raise ValueError(f"padded length {lp} is not a multiple of {FLASH_TQ}")
    return FLASH_TQ


def kernel(x, meta_tokens, g_mix, g_mlp, w_in_attn, g_cq, w_uq, g_ckv, w_ukv, g_q_mla, g_k_mla,
           g_q_fox, g_k_fox, b_forget, w_out_attn, w_in_conv, conv_w, w_out_conv, w_mlp_up,
           w_mlp_down):
    b, seq, d = x.shape
    assert d == D_MODEL and (PAD + N_META + seq) % BLOCK == 0
    lp = PAD + N_META + seq
    tm = _token_tile(lp)
    bf = jnp.bfloat16

    meta = meta_tokens.astype(x.dtype)
    h = None

    rope_tab = _rope_table(lp)
    tri = (jnp.arange(tm)[:, None] >= jnp.arange(tm)[None, :]).astype(bf)
    selq, selk = _gate_selectors()

    wo_attn, wo_conv, w_conv = w_out_attn.astype(bf), w_out_conv.astype(bf), w_in_conv.astype(bf)
    w_up, w_down = w_mlp_up.astype(bf), w_mlp_down.astype(bf)
    for layer in range(DEPTH):
        j = layer // 2
        gmix = g_mix[layer][None]
        if layer % 2 == 0:
            p = _attn_params(w_in_attn[j], g_cq[j], w_uq[j], g_ckv[j], w_ukv[j], g_q_mla[j],
                             g_k_mla[j], g_q_fox[j], g_k_fox[j], b_forget[j])
            stream = (x, meta) if layer == 0 else (h,)
            q, k, v, gate_end = _attn_in(stream, lp, gmix, p, rope_tab, tri, selq, selk, tm)
            y = _flash(*_pruning_tables(gate_end, g_q_fox[j], g_k_fox[j]), q, k, v)
            wo = wo_attn
        else:
            cw = jnp.zeros((8, d), jnp.float32).at[0:3].set(conv_w[j])
            y = _conv_in(h, gmix, w_conv, j, cw, tm)
            wo = wo_conv
        gmlp = g_mlp[layer][None]
        if layer == 0:
            h = _mix_out_mlp_first(x, meta, y, wo, j, gmlp, w_up, w_down, layer, tm)
        elif layer < DEPTH - 1:
            h = _mix_out_mlp(h.reshape(b * lp, d), y.reshape(b * lp, d), wo, j, gmlp,
                             w_up, w_down, layer, tm).reshape(b, lp, d)
        else:
            return _mix_out_mlp_last(h, y, wo, j, gmlp, w_up, w_down, layer, tm, seq)
```

```python
import functools

import numpy as np
import jax
import jax.numpy as jnp
from jax import lax
from jax.experimental import pallas as pl
from jax.experimental.pallas import tpu as pltpu

D_MODEL = 1024
DEPTH = 4
N_META = 16
BLOCK = 128
PAD = 2 * BLOCK - N_META
REAL_START = PAD + N_META
REAL_PARTS = 3
MLA_HEADS = 8
MLA_NOPE = 64
MLA_ROPE = 32
MLA_QK = MLA_NOPE + MLA_ROPE
MLA_V = 64
Q_LORA = 384
KV_LORA = 256
ROPE_BASE = 10000.0
FOX_HEADS = 8
FOX_DIM = 64
D_FF = 4 * D_MODEL
EPS = 1e-6
NEG = -1e30

LANE = 128
HEADS = MLA_HEADS + FOX_HEADS
HALF_ROPE = MLA_ROPE // 2
FEATURE_BASE = (0, FOX_DIM)
EXTRA_BASE = (FOX_DIM, 0)
N_SPLIT = 3
FLAG_FOX_OFF = 2 * N_SPLIT
FLAG_MLA = MLA_QK
PAD_KEY = NEG
LOG2E = 1.4426950408889634
MISC_GATE = 0
MISC_ROPE = MLA_NOPE

OFF_CQ = 0
OFF_CKV = OFF_CQ + Q_LORA
OFF_FQ = OFF_CKV + KV_LORA
OFF_FK = OFF_FQ + FOX_HEADS * FOX_DIM
OFF_FV = OFF_FK + FOX_HEADS * FOX_DIM
OFF_MISC = OFF_FV + FOX_HEADS * FOX_DIM
OFF_MISC_SW = OFF_MISC + LANE
W_CAT = OFF_MISC_SW + LANE

(V_GQ_MLA, V_GQ_MLA_SW, V_GK_MLA, V_GK_MLA_SW, V_ADD_Q_MLA, V_B_FORGET) = range(6)
V_GQ_FOX, V_GK_FOX, V_ADD_Q_FOX, V_ONES_K_FOX, V_ONES_V = 6, 8, 10, 12, 14
VEC_ROWS = 16
PAIR = 2 * LANE

FF_CHUNK = 1024
UNDERFLOW_LOG2 = 150.0
FLASH_TQ = 768
FLASH_TK = 384
VMEM_LIMIT = 56 * 1024 * 1024


def _const_spec(shape):
    nd = len(shape)
    return pl.BlockSpec(shape, lambda *_: (0,) * nd, pipeline_mode=pl.Buffered(1))


def _layer_spec(shape, layer):
    nd = len(shape)
    return pl.BlockSpec((1,) + shape, lambda *_: (layer,) + (0,) * nd,
                        pipeline_mode=pl.Buffered(1))


def _input_specs(tm):
    part = tm // REAL_PARTS
    assert REAL_START == part

    def part_spec(k):
        return pl.BlockSpec((1, part, D_MODEL),
                            lambda bi, i: (bi, jnp.maximum(REAL_PARTS * i + k - 1, 0), 0))

    return [part_spec(k) for k in range(REAL_PARTS)] + [_const_spec((N_META, D_MODEL))]


def _input_tile(i, x_parts, meta_ref):
    lead = jnp.concatenate([jnp.zeros((PAD, D_MODEL), jnp.float32), meta_ref[...]], axis=0)
    first = jnp.where(i == 0, lead, x_parts[0][0])
    return jnp.concatenate([first] + [r[0] for r in x_parts[1:]], axis=0)


def _rms(x, g, n):
    ms = jnp.sum(x * x, axis=-1, keepdims=True) * (1.0 / n)
    return x * lax.rsqrt(ms + EPS) * g


def _split3(x):
    hi = x.astype(jnp.bfloat16).astype(jnp.float32)
    r1 = x - hi
    mid = r1.astype(jnp.bfloat16).astype(jnp.float32)
    lo = r1 - mid
    packed = hi + pltpu.roll(mid, FOX_HEADS, 1) + pltpu.roll(lo, 2 * FOX_HEADS, 1)
    return packed.astype(jnp.bfloat16)


def _dot(a, b):
    return jnp.dot(a, b, preferred_element_type=jnp.float32)


def _attn_in_kernel(*refs, tm, from_x):
    n_stream = REAL_PARTS + 1 if from_x else 1
    stream = refs[:n_stream]
    (gmix_ref, wcat_ref, gcq_ref, wuq_ref, gckv_ref, wkn_ref, wv_ref, vec_ref, rope_ref, tri_ref,
     selq_ref, selk_ref, q_ref, k_ref, v_ref, gate_end_ref, carry_ref) = refs[n_stream:]
    i = pl.program_id(1)

    @pl.when(i == 0)
    def _():
        carry_ref[...] = jnp.zeros_like(carry_ref)

    x = _input_tile(i, stream[:-1], stream[-1]) if from_x else stream[0][0]
    hn = _rms(x, gmix_ref[...], D_MODEL).astype(jnp.bfloat16)

    def seg(lo, width):
        return _dot(hn, wcat_ref[:, lo:lo + width])

    def vec(r):
        return vec_ref[r:r + 1, :]

    cos_t = rope_ref[:, 0:LANE]
    sin_sw = rope_ref[:, LANE:2 * LANE]
    gc_q, gs_q = vec(V_GQ_MLA) * cos_t, vec(V_GQ_MLA_SW) * sin_sw
    gc_k, gs_k = vec(V_GK_MLA) * cos_t, vec(V_GK_MLA_SW) * sin_sw
    add_q_mla = vec(V_ADD_Q_MLA)

    lane = lax.broadcasted_iota(jnp.int32, (tm, LANE), 1)
    row = lax.broadcasted_iota(jnp.int32, (tm, LANE), 0)
    valid = (i * tm + row) >= PAD
    pad_key = jnp.where(valid, 0.0, PAD_KEY)
    add_k_mla = jnp.where(lane == FLAG_MLA, pad_key, 0.0)
    halves = (lane < FOX_DIM, lane >= FOX_DIM)
    add_k_fox = [vec(V_ONES_K_FOX + par)
                 + jnp.where(lane == EXTRA_BASE[par] + FLAG_FOX_OFF, pad_key, 0.0)
                 for par in range(2)]

    misc2 = seg(OFF_MISC, PAIR)
    misc = misc2[:, 0:LANE]
    kpe = jnp.where((lane >= MISC_ROPE) & (lane < MISC_ROPE + MLA_ROPE), misc, 0.0)
    k_rot = misc2[:, LANE:PAIR] * gs_k
    xl = misc + vec(V_B_FORGET)
    logf = jnp.minimum(xl, 0.0) - jnp.log1p(jnp.exp(-jnp.abs(xl)))
    logf = jnp.where(valid & (lane >= MISC_GATE) & (lane < MISC_GATE + FOX_HEADS), logf, 0.0)
    cs = _dot(tri_ref[...], _split3(logf))
    cs = (cs + pltpu.roll(cs, LANE - FOX_HEADS, 1)) + pltpu.roll(cs, LANE - 2 * FOX_HEADS, 1)
    cum = jnp.where(lane < FOX_HEADS, cs, 0.0) + carry_ref[0:1, :]
    carry_ref[0:1, :] = cum[tm - 1:tm, :]
    gate_end_ref[0, 0] = carry_ref[...]
    cum3 = _split3(cum * LOG2E)
    gate_q = _dot(cum3, selq_ref[...])
    gate_k = _dot(cum3, selk_ref[...])

    def inv_rms(xv, n):
        return lax.rsqrt(jnp.sum(xv * xv, axis=-1, keepdims=True) * (1.0 / n) + EPS)

    cqn = _rms(seg(OFF_CQ, Q_LORA), gcq_ref[...], Q_LORA).astype(jnp.bfloat16)
    ckvn = _rms(seg(OFF_CKV, KV_LORA), gckv_ref[...], KV_LORA).astype(jnp.bfloat16)
    for g in range(MLA_HEADS // 2):
        cols = slice(g * PAIR, (g + 1) * PAIR)
        cols_sw = slice(MLA_HEADS * LANE + g * PAIR, MLA_HEADS * LANE + (g + 1) * PAIR)
        xq2 = _dot(cqn, wuq_ref[:, cols])
        xq2_sw = _dot(cqn, wuq_ref[:, cols_sw])
        xk2 = _dot(ckvn, wkn_ref[:, cols])
        xv2 = _dot(ckvn, wv_ref[:, cols])
        for e in range(2):
            hd, sl = 2 * g + e, slice(e * LANE, (e + 1) * LANE)
            xq = xq2[:, sl]
            q_ref[0, hd] = ((xq * gc_q + xq2_sw[:, sl] * gs_q) * inv_rms(xq, MLA_QK) + add_q_mla
                            ).astype(jnp.bfloat16)
            xk = xk2[:, sl] + kpe
            k_ref[0, hd] = ((xk * gc_k + k_rot) * inv_rms(xk, MLA_QK) + add_k_mla
                            ).astype(jnp.bfloat16)
            v_ref[0, hd] = (xv2[:, sl] + vec(V_ONES_V + e)).astype(jnp.bfloat16)

    for g in range(FOX_HEADS // 4):
        xq4 = seg(OFF_FQ + g * PAIR, PAIR)
        xk4 = seg(OFF_FK + g * PAIR, PAIR)
        xv4 = seg(OFF_FV + g * PAIR, PAIR)
        for e in range(4):
            hd, par = 4 * g + e, e % 2
            sl = slice((e // 2) * LANE, (e // 2 + 1) * LANE)
            gl = slice(hd * LANE, (hd + 1) * LANE)
            xq, xk = xq4[:, sl], xk4[:, sl]
            rq = inv_rms(jnp.where(halves[par], xq, 0.0), FOX_DIM)
            rk = inv_rms(jnp.where(halves[par], xk, 0.0), FOX_DIM)
            q_ref[0, MLA_HEADS + hd] = (xq * vec(V_GQ_FOX + par) * rq + gate_q[:, gl]
                                        + vec(V_ADD_Q_FOX + par)).astype(jnp.bfloat16)
            k_ref[0, MLA_HEADS + hd] = (xk * vec(V_GK_FOX + par) * rk + gate_k[:, gl]
                                        + add_k_fox[par]).astype(jnp.bfloat16)
            v_ref[0, MLA_HEADS + hd] = (jnp.where(halves[par], xv4[:, sl], 0.0)
                                        + vec(V_ONES_V + par)).astype(jnp.bfloat16)


def _attn_in(stream, lp, gmix, p, rope_tab, tri, selq, selk, tm):
    from_x = len(stream) == 2
    b, d = stream[0].shape[0], D_MODEL
    nt = lp // tm
    kern = functools.partial(_attn_in_kernel, tm=tm, from_x=from_x)
    stream_specs = (_input_specs(tm) if from_x
                    else [pl.BlockSpec((1, tm, d), lambda bi, i: (bi, i, 0))])
    stream_args = [stream[0]] * REAL_PARTS + [stream[1]] if from_x else [stream[0]]
    qk_shape = jax.ShapeDtypeStruct((b, HEADS, lp, LANE), jnp.bfloat16)
    qk_spec = pl.BlockSpec((1, HEADS, tm, LANE), lambda bi, i: (bi, 0, i, 0))
    return pl.pallas_call(
        kern,
        grid=(b, nt),
        in_specs=stream_specs + [
            _const_spec((1, d)),
            _const_spec((d, W_CAT)),
            _const_spec((1, Q_LORA)),
            _const_spec((Q_LORA, 2 * MLA_HEADS * LANE)),
            _const_spec((1, KV_LORA)),
            _const_spec((KV_LORA, MLA_HEADS * LANE)),
            _const_spec((KV_LORA, MLA_HEADS * LANE)),
            _const_spec((VEC_ROWS, LANE)),
            pl.BlockSpec((tm, 2 * LANE), lambda bi, i: (i, 0)),
            _const_spec((tm, tm)),
            _const_spec((LANE, FOX_HEADS * LANE)),
            _const_spec((LANE, FOX_HEADS * LANE)),
        ],
        out_specs=[qk_spec, qk_spec, qk_spec,
                   pl.BlockSpec((1, 1, 8, LANE), lambda bi, i: (bi, i, 0, 0))],
        out_shape=[qk_shape, qk_shape, qk_shape,
                   jax.ShapeDtypeStruct((b, nt, 8, LANE), jnp.float32)],
        scratch_shapes=[pltpu.VMEM((8, LANE), jnp.float32)],
        compiler_params=pltpu.CompilerParams(
            dimension_semantics=("arbitrary", "arbitrary"), vmem_limit_bytes=VMEM_LIMIT),
        name="attn_in",
    )(*stream_args, gmix, p["wcat"], p["gcq"], p["wuq"], p["gckv"], p["wkn"], p["wv"], p["vec"],
      rope_tab, tri, selq, selk)


def _flash_kernel(gate_end_ref, slack_ref, q_ref, k_ref, v_ref, o_ref, m_ref, acc_ref, al_ref,
                  p_ref, *, tq, tk, nq):
    chunks = tq // tk
    bi, hp = pl.program_id(0), pl.program_id(1)

    def query_block(qi, carry):
        qbase = qi * tq

        def softmax(j, u, base, diagonal, first=False):
            r0 = u * tk if diagonal else 0
            rows = slice(r0, tq)
            start = pl.multiple_of(base + u * tk, tk)
            q_rows = pl.ds(pl.multiple_of(qbase + r0, tk), tq - r0)
            s = lax.dot_general(q_ref[0, j, q_rows, :], k_ref[0, j, pl.ds(start, tk), :],
                                (((1,), (1,)), ((), ())), preferred_element_type=jnp.float32)
            if diagonal:
                row = lax.broadcasted_iota(jnp.int32, (tq - r0, tk), 0)
                col = lax.broadcasted_iota(jnp.int32, (tq - r0, tk), 1)
                s = jnp.where(col <= row, s, NEG)
            if first:
                m_next = jnp.broadcast_to(jnp.max(s, axis=1, keepdims=True), (tq - r0, LANE))
            else:
                m_prev = m_ref[j, rows, :]
                m_next = jnp.maximum(m_prev, jnp.max(s, axis=1, keepdims=True))
                al_ref[j, u, rows, :] = jnp.exp2(m_prev - m_next)
            p = jnp.exp2(s - jnp.concatenate([m_next] * (tk // LANE), axis=1))
            p_ref[j, u, rows, :] = p.astype(jnp.bfloat16)
            m_ref[j, rows, :] = m_next

        def pv(j, u, base, diagonal, first=False):
            r0 = u * tk if diagonal else 0
            rows = slice(r0, tq)
            start = pl.multiple_of(base + u * tk, tk)
            new = _dot(p_ref[j, u, rows, :], v_ref[0, j, pl.ds(start, tk), :])
            if first:
                acc_ref[j, rows, :] = new
            else:
                acc_ref[j, rows, :] = acc_ref[j, rows, :] * al_ref[j, u, rows, :] + new

        def diagonal_block():
            for u in range(chunks):
                softmax(0, u, qbase, True, first=(u == 0))
                if u > 0:
                    pv(1, u - 1, qbase, True, first=(u == 1))
                softmax(1, u, qbase, True, first=(u == 0))
                pv(0, u, qbase, True, first=(u == 0))
            pv(1, chunks - 1, qbase, True, first=(chunks == 1))

        def block(kb, diagonal):
            base = kb * tq
            for u in range(chunks):
                softmax(0, u, base, diagonal)
                if u == 0:
                    pv(1, chunks - 1, jnp.maximum(kb - 1, 0) * tq, False)
                else:
                    pv(1, u - 1, base, diagonal)
                softmax(1, u, base, diagonal)
                pv(0, u, base, diagonal)

        def blocks_needed(j):
            hd = 2 * hp + j
            gate_q = gate_end_ref[bi, hd, jnp.maximum(qi - 1, 0)]
            count = jnp.int32(0)
            for kb in range(nq - 1):
                keep = (kb < qi) & (gate_q - gate_end_ref[bi, hd, kb] >= slack_ref[hd])
                count = count + keep.astype(jnp.int32)
            return count

        def diagonal_and_previous_block():
            steps = ([(qbase, u, True) for u in range(chunks)]
                     + [(qbase - tq, u, False) for u in range(chunks)])
            for n, (base, u, diagonal) in enumerate(steps):
                softmax(0, u, base, diagonal, first=(n == 0))
                if n > 0:
                    pv(1, steps[n - 1][1], steps[n - 1][0], steps[n - 1][2], first=(n == 1))
                softmax(1, u, base, diagonal, first=(n == 0))
                pv(0, u, base, diagonal, first=(n == 0))
            pv(1, steps[-1][1], steps[-1][0], steps[-1][2])

        n_off = jnp.maximum(blocks_needed(0), blocks_needed(1))

        @pl.when(n_off == 1)
        def _():
            diagonal_and_previous_block()

        @pl.when(n_off != 1)
        def _():
            diagonal_block()
            al_ref[1, chunks - 1] = jnp.ones((tq, LANE), jnp.float32)
            p_ref[1, chunks - 1] = jnp.zeros((tq, tk), jnp.bfloat16)
            first = qi - n_off
            odd = n_off & 1

            @pl.when(odd == 1)
            def _():
                block(first, False)

            def body(pair, c):
                kb = first + odd + 2 * pair
                block(kb, False)
                block(kb + 1, False)
                return c

            lax.fori_loop(0, lax.shift_right_logical(n_off, 1), body, 0)
            pv(1, chunks - 1, jnp.maximum(qi - 1, 0) * tq, False)

        o0 = acc_ref[0]
        o1 = acc_ref[1]
        o0 = o0 / pltpu.roll(o0, FOX_DIM, 1)
        o1 = o1 / pltpu.roll(o1, FOX_DIM, 1)
        lane = lax.broadcasted_iota(jnp.int32, (tq, LANE), 1)
        o = jnp.where(lane < MLA_V, o0, o1)
        o_ref[0, pl.ds(pl.multiple_of(qbase, tq), tq), :] = o.astype(jnp.bfloat16)
        return carry

    lax.fori_loop(0, nq, query_block, 0)


def _flash(gate_end, slack, q, k, v):
    b, _, lp, _ = q.shape
    tq, tk = FLASH_TQ, FLASH_TK
    kern = functools.partial(_flash_kernel, tq=tq, tk=tk, nq=lp // tq)
    qkv_spec = pl.BlockSpec((1, 2, lp, LANE), lambda bi, hp: (bi, hp, 0, 0))
    return pl.pallas_call(
        kern,
        grid=(b, HEADS // 2),
        in_specs=[pl.BlockSpec(memory_space=pltpu.SMEM), pl.BlockSpec(memory_space=pltpu.SMEM),
                  qkv_spec, qkv_spec, qkv_spec],
        out_specs=pl.BlockSpec((1, lp, LANE), lambda bi, hp: (bi, 0, hp)),
        out_shape=jax.ShapeDtypeStruct((b, lp, HEADS * MLA_V), jnp.bfloat16),
        scratch_shapes=[pltpu.VMEM((2, tq, LANE), jnp.float32)] * 2
        + [pltpu.VMEM((2, tq // tk, tq, LANE), jnp.float32),
           pltpu.VMEM((2, tq // tk, tq, tk), jnp.bfloat16)],
        compiler_params=pltpu.CompilerParams(
            dimension_semantics=("arbitrary", "arbitrary"), vmem_limit_bytes=VMEM_LIMIT),
        name="flash",
    )(gate_end, slack, q, k, v)


def _conv_in_kernel(h_ref, gmix_ref, win_ref, cw_ref, y_ref, gs_ref, *, tm):
    i = pl.program_id(1)

    @pl.when(i == 0)
    def _():
        gs_ref[0:8, :] = jnp.zeros((8, D_MODEL), jnp.float32)

    x = h_ref[0]
    hn = _rms(x, gmix_ref[...], D_MODEL).astype(jnp.bfloat16)
    gate_c = _dot(hn, win_ref[0, :, D_MODEL:2 * D_MODEL])
    u = _dot(hn, win_ref[0, :, 2 * D_MODEL:3 * D_MODEL])
    row = lax.broadcasted_iota(jnp.int32, (tm, D_MODEL), 0)
    g = jnp.where((i * tm + row) >= PAD, gate_c * u, 0.0)
    gs_ref[8:tm + 8, :] = g
    y = (cw_ref[0:1, :] * gs_ref[6:tm + 6, :] + cw_ref[1:2, :] * gs_ref[7:tm + 7, :]
         + cw_ref[2:3, :] * g)
    gs_ref[0:8, :] = gs_ref[tm:tm + 8, :]
    gate_b = _dot(hn, win_ref[0, :, 0:D_MODEL])
    y_ref[0] = (gate_b * y).astype(jnp.bfloat16)


def _conv_in(h, gmix, win, layer, cw, tm):
    b, lp, d = h.shape
    kern = functools.partial(_conv_in_kernel, tm=tm)
    return pl.pallas_call(
        kern,
        grid=(b, lp // tm),
        in_specs=[
            pl.BlockSpec((1, tm, d), lambda bi, i: (bi, i, 0)),
            _const_spec((1, d)),
            _layer_spec((d, 3 * d), layer),
            _const_spec((8, d)),
        ],
        out_specs=pl.BlockSpec((1, tm, d), lambda bi, i: (bi, i, 0)),
        out_shape=jax.ShapeDtypeStruct((b, lp, d), jnp.bfloat16),
        scratch_shapes=[pltpu.VMEM((tm + 8, d), jnp.float32)],
        compiler_params=pltpu.CompilerParams(
            dimension_semantics=("arbitrary", "arbitrary"), vmem_limit_bytes=VMEM_LIMIT),
        name="conv_in",
    )(h, gmix, win, cw)


def _mlp_tile(h, y, wo_ref, gmlp_ref, wup_ref, wdn_ref):
    h1 = h + _dot(y, wo_ref[0])
    n = _rms(h1, gmlp_ref[...], D_MODEL).astype(jnp.bfloat16)
    acc = h1
    for c in range(D_FF // FF_CHUNK):
        sl = slice(c * FF_CHUNK, (c + 1) * FF_CHUNK)
        a = jnp.maximum(_dot(n, wup_ref[0, :, sl]), 0.0)
        acc = acc + _dot((a * a).astype(jnp.bfloat16), wdn_ref[0, sl, :])
    return acc


def _mix_out_mlp_kernel(h_ref, y_ref, wo_ref, gmlp_ref, wup_ref, wdn_ref, out_ref):
    out_ref[...] = _mlp_tile(h_ref[...], y_ref[...], wo_ref, gmlp_ref, wup_ref, wdn_ref)


def _mix_out_mlp(h, y, wo, wo_layer, gmlp, wup, wdn, layer, tm):
    r, d = h.shape
    return pl.pallas_call(
        _mix_out_mlp_kernel,
        grid=(r // tm,),
        in_specs=[
            pl.BlockSpec((tm, d), lambda i: (i, 0)),
            pl.BlockSpec((tm, d), lambda i: (i, 0)),
            _layer_spec((d, d), wo_layer),
            _const_spec((1, d)),
            _layer_spec((d, D_FF), layer),
            _layer_spec((D_FF, d), layer),
        ],
        out_specs=pl.BlockSpec((tm, d), lambda i: (i, 0)),
        out_shape=jax.ShapeDtypeStruct((r, d), jnp.float32),
        compiler_params=pltpu.CompilerParams(
            dimension_semantics=("arbitrary",), vmem_limit_bytes=VMEM_LIMIT),
        name="mix_out_mlp",
    )(h, y, wo, gmlp, wup, wdn)


def _mix_out_mlp_first_kernel(*refs):
    x_parts, meta_ref = refs[0:REAL_PARTS], refs[REAL_PARTS]
    y_ref, wo_ref, gmlp_ref, wup_ref, wdn_ref, out_ref = refs[REAL_PARTS + 1:]
    h = _input_tile(pl.program_id(1), x_parts, meta_ref)
    out_ref[0] = _mlp_tile(h, y_ref[0], wo_ref, gmlp_ref, wup_ref, wdn_ref)


def _mix_out_mlp_first(x, meta, y, wo, wo_layer, gmlp, wup, wdn, layer, tm):
    b, lp, d = y.shape
    tile = pl.BlockSpec((1, tm, d), lambda bi, i: (bi, i, 0))
    return pl.pallas_call(
        _mix_out_mlp_first_kernel,
        grid=(b, lp // tm),
        in_specs=_input_specs(tm) + [
            tile,
            _layer_spec((d, d), wo_layer),
            _const_spec((1, d)),
            _layer_spec((d, D_FF), layer),
            _layer_spec((D_FF, d), layer),
        ],
        out_specs=tile,
        out_shape=jax.ShapeDtypeStruct((b, lp, d), jnp.float32),
        compiler_params=pltpu.CompilerParams(
            dimension_semantics=("arbitrary", "arbitrary"), vmem_limit_bytes=VMEM_LIMIT),
        name="mix_out_mlp_first",
    )(*([x] * REAL_PARTS), meta, y, wo, gmlp, wup, wdn)


def _mix_out_mlp_last_kernel(*refs):
    h_parts, y_parts = refs[0:REAL_PARTS], refs[REAL_PARTS:2 * REAL_PARTS]
    wo_ref, gmlp_ref, wup_ref, wdn_ref, out_ref = refs[2 * REAL_PARTS:]
    h = jnp.concatenate([r[0] for r in h_parts], axis=0)
    y = jnp.concatenate([r[0] for r in y_parts], axis=0)
    out_ref[0] = _mlp_tile(h, y, wo_ref, gmlp_ref, wup_ref, wdn_ref)


def _mix_out_mlp_last(h, y, wo, wo_layer, gmlp, wup, wdn, layer, tm, seq):
    b, lp, d = h.shape
    part = tm // REAL_PARTS
    last_part = lp // part - 1

    def part_spec(k):
        return pl.BlockSpec(
            (1, part, d),
            lambda bi, i: (bi, jnp.minimum(REAL_START // part + REAL_PARTS * i + k, last_part), 0))

    parts = [part_spec(k) for k in range(REAL_PARTS)]
    return pl.pallas_call(
        _mix_out_mlp_last_kernel,
        grid=(b, pl.cdiv(seq, tm)),
        in_specs=parts + parts + [
            _layer_spec((d, d), wo_layer),
            _const_spec((1, d)),
            _layer_spec((d, D_FF), layer),
            _layer_spec((D_FF, d), layer),
        ],
        out_specs=pl.BlockSpec((1, tm, d), lambda bi, i: (bi, i, 0)),
        out_shape=jax.ShapeDtypeStruct((b, seq, d), jnp.float32),
        compiler_params=pltpu.CompilerParams(
            dimension_semantics=("arbitrary", "arbitrary"), vmem_limit_bytes=VMEM_LIMIT),
        name="mix_out_mlp_last",
    )(*([h] * REAL_PARTS + [y] * REAL_PARTS), wo, gmlp, wup, wdn)


def _pad_heads(w, heads, dim):
    k = w.shape[0]
    w = w.reshape(k, heads, dim)
    w = jnp.pad(w, ((0, 0), (0, 0), (0, LANE - dim)))
    return w.reshape(k, heads * LANE)


def _lane_vec(v, offset=0):
    return jnp.zeros((LANE,), jnp.float32).at[offset:offset + v.shape[0]].set(v)


def _attn_params(w_in, g_cq, w_uq, g_ckv, w_ukv, g_q_mla, g_k_mla, g_q_fox, g_k_fox, b_forget):
    bf = jnp.bfloat16
    o1 = Q_LORA
    o2 = o1 + KV_LORA
    o3 = o2 + MLA_ROPE
    o4 = o3 + FOX_HEADS * FOX_DIM
    o5 = o4 + FOX_HEADS * FOX_DIM
    o6 = o5 + FOX_HEADS * FOX_DIM
    misc = jnp.zeros((D_MODEL, LANE), jnp.float32)
    misc = misc.at[:, MISC_GATE:MISC_GATE + FOX_HEADS].set(w_in[:, o6:])
    misc = misc.at[:, MISC_ROPE:MISC_ROPE + MLA_ROPE].set(w_in[:, o2:o3])
    misc_sw = jnp.zeros((D_MODEL, LANE), jnp.float32)
    misc_sw = misc_sw.at[:, MISC_ROPE:MISC_ROPE + HALF_ROPE].set(w_in[:, o2 + HALF_ROPE:o3])
    misc_sw = misc_sw.at[:, MISC_ROPE + HALF_ROPE:MISC_ROPE + MLA_ROPE].set(
        w_in[:, o2:o2 + HALF_ROPE])
    wcat = jnp.concatenate([
        w_in[:, :o1], w_in[:, o1:o2],
        w_in[:, o3:o6], misc, misc_sw], axis=1).astype(bf)
    kv = w_ukv.reshape(KV_LORA, MLA_HEADS, MLA_NOPE + MLA_V)
    wkn = _pad_heads(kv[:, :, :MLA_NOPE].reshape(KV_LORA, -1), MLA_HEADS, MLA_NOPE).astype(bf)
    wv = jnp.pad(kv[:, :, MLA_NOPE:].reshape(KV_LORA, MLA_HEADS // 2, 2, MLA_V),
                 ((0, 0), (0, 0), (0, 0), (0, LANE - MLA_V)))
    wv = jnp.concatenate([wv[:, :, 0], jnp.roll(wv[:, :, 1], MLA_V, axis=-1)], axis=-1)
    wv = wv.reshape(KV_LORA, MLA_HEADS * LANE).astype(bf)
    lo, mid, hi = MLA_NOPE, MLA_NOPE + HALF_ROPE, MLA_NOPE + MLA_ROPE
    uq = w_uq.reshape(Q_LORA, MLA_HEADS, MLA_QK)
    uq_sw = jnp.zeros((Q_LORA, MLA_HEADS, LANE), jnp.float32)
    uq_sw = uq_sw.at[:, :, lo:mid].set(uq[:, :, mid:hi]).at[:, :, mid:hi].set(uq[:, :, lo:mid])
    wuq = jnp.concatenate([_pad_heads(w_uq, MLA_HEADS, MLA_QK),
                           uq_sw.reshape(Q_LORA, MLA_HEADS * LANE)], axis=1).astype(bf)

    def swapped(g):
        return jnp.zeros((LANE,), jnp.float32).at[lo:mid].set(g[mid:hi]).at[mid:hi].set(g[lo:mid])

    zero = jnp.zeros((LANE,), jnp.float32)
    q_scale_mla = MLA_QK ** -0.5 * LOG2E
    rows = [zero] * VEC_ROWS
    rows[V_GQ_MLA] = _lane_vec(g_q_mla) * q_scale_mla
    rows[V_GQ_MLA_SW] = swapped(g_q_mla) * q_scale_mla
    rows[V_GK_MLA] = _lane_vec(g_k_mla)
    rows[V_GK_MLA_SW] = swapped(g_k_mla)
    for par in range(2):
        feat, extra = FEATURE_BASE[par], EXTRA_BASE[par]
        rows[V_GQ_FOX + par] = _lane_vec(g_q_fox, feat) * (FOX_DIM ** -0.5 * LOG2E)
        rows[V_GK_FOX + par] = _lane_vec(g_k_fox, feat)
        rows[V_ADD_Q_FOX + par] = (zero.at[extra + N_SPLIT:extra + 2 * N_SPLIT].set(1.0)
                                   .at[extra + FLAG_FOX_OFF].set(1.0))
        rows[V_ONES_K_FOX + par] = zero.at[extra:extra + N_SPLIT].set(1.0)
        rows[V_ONES_V + par] = zero.at[extra:extra + FOX_DIM].set(1.0)
    rows[V_B_FORGET] = _lane_vec(b_forget, MISC_GATE)
    rows[V_ADD_Q_MLA] = zero.at[FLAG_MLA].set(1.0)
    vec = jnp.stack(rows)
    return dict(wcat=wcat, gcq=g_cq[None], wuq=wuq, gckv=g_ckv[None], wkn=wkn, wv=wv, vec=vec)


def _gate_selectors():
    selq = np.zeros((LANE, FOX_HEADS * LANE), np.float32)
    selk = np.zeros((LANE, FOX_HEADS * LANE), np.float32)
    for part in range(N_SPLIT):
        for hd in range(FOX_HEADS):
            extra = hd * LANE + EXTRA_BASE[hd % 2]
            selq[part * FOX_HEADS + hd, extra + part] = 1.0
            selk[part * FOX_HEADS + hd, extra + N_SPLIT + part] = -1.0
    return jnp.asarray(selq, jnp.bfloat16), jnp.asarray(selk, jnp.bfloat16)


def _rope_table(lp):
    lane = jnp.arange(LANE, dtype=jnp.int32)
    rotary = (lane >= MLA_NOPE) & (lane < MLA_NOPE + MLA_ROPE)
    first_half = rotary & (lane < MLA_NOPE + HALF_ROPE)
    pair = ((lane - MLA_NOPE) % HALF_ROPE).astype(jnp.float32)
    inv_freq = ROPE_BASE ** (-(2.0 * pair) / MLA_ROPE)
    pos = (jnp.arange(lp, dtype=jnp.int32) - PAD).astype(jnp.float32)
    ang = pos[:, None] * inv_freq[None, :]
    cos_t = jnp.where(lane < MLA_NOPE, 1.0, jnp.where(rotary, jnp.cos(ang), 0.0))
    sin_sw = jnp.where(rotary, jnp.where(first_half, -jnp.sin(ang), jnp.sin(ang)), 0.0)
    return jnp.concatenate([cos_t, sin_sw], axis=1)


def _pruning_tables(gate_end, g_q, g_k):
    b, nt = gate_end.shape[:2]
    fox = jnp.transpose(gate_end[:, :, 0, MISC_GATE:MISC_GATE + FOX_HEADS] * LOG2E, (0, 2, 1))
    table = jnp.concatenate([jnp.zeros((b, MLA_HEADS, nt), jnp.float32), fox], axis=1)
    bound = 1.02 * FOX_DIM * (FOX_DIM ** -0.5 * LOG2E) * jnp.max(jnp.abs(g_q)) * jnp.max(jnp.abs(g_k))
    slack_fox = -(2.0 * bound + UNDERFLOW_LOG2 + 4.0)
    slack = jnp.concatenate([jnp.full((MLA_HEADS,), NEG, jnp.float32),
                             jnp.full((FOX_HEADS,), slack_fox, jnp.float32)])
    return table, slack


def _token_tile(lp):
    if lp % FLASH_TQ:
        raise ValueError(f"padded length {lp} is not a multiple of {FLASH_TQ}")
    return FLASH_TQ


def kernel(x, meta_tokens, g_mix, g_mlp, w_in_attn, g_cq, w_uq, g_ckv, w_ukv, g_q_mla, g_k_mla,
           g_q_fox, g_k_fox, b_forget, w_out_attn, w_in_conv, conv_w, w_out_conv, w_mlp_up,
           w_mlp_down):
    b, seq, d = x.shape
    assert d == D_MODEL and (PAD + N_META + seq) % BLOCK == 0
    lp = PAD + N_META + seq
    tm = _token_tile(lp)
    bf = jnp.bfloat16

    meta = meta_tokens.astype(x.dtype)
    h = None

    rope_tab = _rope_table(lp)
    tri = (jnp.arange(tm)[:, None] >= jnp.arange(tm)[None, :]).astype(bf)
    selq, selk = _gate_selectors()

    wo_attn, wo_conv, w_conv = w_out_attn.astype(bf), w_out_conv.astype(bf), w_in_conv.astype(bf)
    w_up, w_down = w_mlp_up.astype(bf), w_mlp_down.astype(bf)
    for layer in range(DEPTH):
        j = layer // 2
        gmix = g_mix[layer][None]
        if layer % 2 == 0:
            p = _attn_params(w_in_attn[j], g_cq[j], w_uq[j], g_ckv[j], w_ukv[j], g_q_mla[j],
                             g_k_mla[j], g_q_fox[j], g_k_fox[j], b_forget[j])
            stream = (x, meta) if layer == 0 else (h,)
            q, k, v, gate_end = _attn_in(stream, lp, gmix, p, rope_tab, tri, selq, selk, tm)
            y = _flash(*_pruning_tables(gate_end, g_q_fox[j], g_k_fox[j]), q, k, v)
            wo = wo_attn
        else:
            cw = jnp.zeros((8, d), jnp.float32).at[0:3].set(conv_w[j])
            y = _conv_in(h, gmix, w_conv, j, cw, tm)
            wo = wo_conv
        gmlp = g_mlp[layer][None]
        if layer == 0:
            h = _mix_out_mlp_first(x, meta, y, wo, j, gmlp, w_up, w_down, layer, tm)
        elif layer < DEPTH - 1:
            h = _mix_out_mlp(h.reshape(b * lp, d), y.reshape(b * lp, d), wo, j, gmlp,
                             w_up, w_down, layer, tm).reshape(b, lp, d)
        else:
            return _mix_out_mlp_last(h, y, wo, j, gmlp, w_up, w_down, layer, tm, seq)
```

```python
import functools

import numpy as np
import jax
import jax.numpy as jnp
from jax import lax
from jax.experimental import pallas as pl
from jax.experimental.pallas import tpu as pltpu

D_MODEL = 1024
DEPTH = 4
N_META = 16
BLOCK = 128
PAD = 2 * BLOCK - N_META
REAL_START = PAD + N_META
REAL_PARTS = 3
MLA_HEADS = 8
MLA_NOPE = 64
MLA_ROPE = 32
MLA_QK = MLA_NOPE + MLA_ROPE
MLA_V = 64
Q_LORA = 384
KV_LORA = 256
ROPE_BASE = 10000.0
FOX_HEADS = 8
FOX_DIM = 64
D_FF = 4 * D_MODEL
EPS = 1e-6
NEG = -1e30

LANE = 128
HEADS = MLA_HEADS + FOX_HEADS
HALF_ROPE = MLA_ROPE // 2
FEATURE_BASE = (0, FOX_DIM)
EXTRA_BASE = (FOX_DIM, 0)
N_SPLIT = 3
FLAG_FOX_OFF = 2 * N_SPLIT
FLAG_MLA = MLA_QK
PAD_KEY = NEG
LOG2E = 1.4426950408889634
MISC_GATE = 0
MISC_ROPE = MLA_NOPE

OFF_CQ = 0
OFF_MISC = OFF_CQ + Q_LORA
OFF_CKV = OFF_MISC + LANE
OFF_FQ = OFF_CKV + KV_LORA
OFF_FK = OFF_FQ + FOX_HEADS * FOX_DIM
OFF_FV = OFF_FK + FOX_HEADS * FOX_DIM
W_CAT = OFF_FV + FOX_HEADS * FOX_DIM

(V_GQ_MLA, V_GQ_MLA_SW, V_GK_MLA, V_GK_MLA_SW, V_ADD_Q_MLA, V_B_FORGET) = range(6)
V_GQ_FOX, V_GK_FOX, V_ADD_Q_FOX, V_ONES_K_FOX, V_ONES_V = 6, 8, 10, 12, 14
VEC_ROWS = 16
PAIR = 2 * LANE

FF_CHUNK = 1024
UNDERFLOW_LOG2 = 150.0
FLASH_TQ = 768
FLASH_TK = 384
VMEM_LIMIT = 56 * 1024 * 1024


def _const_spec(shape):
    nd = len(shape)
    return pl.BlockSpec(shape, lambda *_: (0,) * nd, pipeline_mode=pl.Buffered(1))


def _layer_spec(shape, layer):
    nd = len(shape)
    return pl.BlockSpec((1,) + shape, lambda *_: (layer,) + (0,) * nd,
                        pipeline_mode=pl.Buffered(1))


def _input_specs(tm):
    part = tm // REAL_PARTS
    assert REAL_START == part

    def part_spec(k):
        return pl.BlockSpec((1, part, D_MODEL),
                            lambda bi, i: (bi, jnp.maximum(REAL_PARTS * i + k - 1, 0), 0))

    return [part_spec(k) for k in range(REAL_PARTS)] + [_const_spec((N_META, D_MODEL))]


def _input_tile(i, x_parts, meta_ref):
    lead = jnp.concatenate([jnp.zeros((PAD, D_MODEL), jnp.float32), meta_ref[...]], axis=0)
    first = jnp.where(i == 0, lead, x_parts[0][0])
    return jnp.concatenate([first] + [r[0] for r in x_parts[1:]], axis=0)


def _rms(x, g, n):
    ms = jnp.sum(x * x, axis=-1, keepdims=True) * (1.0 / n)
    return x * lax.rsqrt(ms + EPS) * g


def _split3(x):
    hi = x.astype(jnp.bfloat16).astype(jnp.float32)
    r1 = x - hi
    mid = r1.astype(jnp.bfloat16).astype(jnp.float32)
    lo = r1 - mid
    packed = hi + pltpu.roll(mid, FOX_HEADS, 1) + pltpu.roll(lo, 2 * FOX_HEADS, 1)
    return packed.astype(jnp.bfloat16)


def _dot(a, b):
    return jnp.dot(a, b, preferred_element_type=jnp.float32)


def _attn_in_kernel(*refs, tm, from_x):
    n_stream = REAL_PARTS + 1 if from_x else 1
    stream = refs[:n_stream]
    (gmix_ref, wcat_ref, gcq_ref, wuq_ref, gckv_ref, wkn_ref, wv_ref, vec_ref, rope_ref, tri_ref,
     selq_ref, selk_ref, q_ref, k_ref, v_ref, gate_end_ref, carry_ref) = refs[n_stream:]
    i = pl.program_id(1)

    @pl.when(i == 0)
    def _():
        carry_ref[...] = jnp.zeros_like(carry_ref)

    x = _input_tile(i, stream[:-1], stream[-1]) if from_x else stream[0][0]
    hn = _rms(x, gmix_ref[...], D_MODEL).astype(jnp.bfloat16)

    def seg(lo, width):
        return _dot(hn, wcat_ref[:, lo:lo + width])

    def vec(r):
        return vec_ref[r:r + 1, :]

    cos_t = rope_ref[:, 0:LANE]
    sin_sw = rope_ref[:, LANE:2 * LANE]
    gc_q, gs_q = vec(V_GQ_MLA) * cos_t, vec(V_GQ_MLA_SW) * sin_sw
    gc_k, gs_k = vec(V_GK_MLA) * cos_t, vec(V_GK_MLA_SW) * sin_sw
    add_q_mla = vec(V_ADD_Q_MLA)

    lane = lax.broadcasted_iota(jnp.int32, (tm, LANE), 1)
    row = lax.broadcasted_iota(jnp.int32, (tm, LANE), 0)
    valid = (i * tm + row) >= PAD
    pad_key = jnp.where(valid, 0.0, PAD_KEY)
    add_k_mla = jnp.where(lane == FLAG_MLA, pad_key, 0.0)
    halves = (lane < FOX_DIM, lane >= FOX_DIM)
    add_k_fox = [vec(V_ONES_K_FOX + par)
                 + jnp.where(lane == EXTRA_BASE[par] + FLAG_FOX_OFF, pad_key, 0.0)
                 for par in range(2)]

    cq_misc = seg(OFF_CQ, Q_LORA + LANE)
    misc = cq_misc[:, Q_LORA:]
    kpe = jnp.where((lane >= MISC_ROPE) & (lane < MISC_ROPE + MLA_ROPE), misc, 0.0)
    k_rot = jnp.where(lane < MISC_ROPE + HALF_ROPE, pltpu.roll(kpe, LANE - HALF_ROPE, 1),
                      pltpu.roll(kpe, HALF_ROPE, 1)) * gs_k
    xl = misc + vec(V_B_FORGET)
    logf = jnp.minimum(xl, 0.0) - jnp.log1p(jnp.exp(-jnp.abs(xl)))
    logf = jnp.where(valid & (lane >= MISC_GATE) & (lane < MISC_GATE + FOX_HEADS), logf, 0.0)
    cs = _dot(tri_ref[...], _split3(logf))
    cs = (cs + pltpu.roll(cs, LANE - FOX_HEADS, 1)) + pltpu.roll(cs, LANE - 2 * FOX_HEADS, 1)
    cum = jnp.where(lane < FOX_HEADS, cs, 0.0) + carry_ref[0:1, :]
    carry_ref[0:1, :] = cum[tm - 1:tm, :]
    gate_end_ref[0, 0] = carry_ref[...]
    cum3 = _split3(cum * LOG2E)
    gate_q = _dot(cum3, selq_ref[...])
    gate_k = _dot(cum3, selk_ref[...])

    def inv_rms(xv, n):
        return lax.rsqrt(jnp.sum(xv * xv, axis=-1, keepdims=True) * (1.0 / n) + EPS)

    cqn = _rms(cq_misc[:, :Q_LORA], gcq_ref[...], Q_LORA).astype(jnp.bfloat16)
    ckvn = _rms(seg(OFF_CKV, KV_LORA), gckv_ref[...], KV_LORA).astype(jnp.bfloat16)
    for g in range(MLA_HEADS // 2):
        cols = slice(g * PAIR, (g + 1) * PAIR)
        cols_sw = slice(MLA_HEADS * LANE + g * PAIR, MLA_HEADS * LANE + (g + 1) * PAIR)
        xq2 = _dot(cqn, wuq_ref[:, cols])
        xq2_sw = _dot(cqn, wuq_ref[:, cols_sw])
        xk2 = _dot(ckvn, wkn_ref[:, cols])
        xv2 = _dot(ckvn, wv_ref[:, cols])
        for e in range(2):
            hd, sl = 2 * g + e, slice(e * LANE, (e + 1) * LANE)
            xq = xq2[:, sl]
            q_ref[0, hd] = ((xq * gc_q + xq2_sw[:, sl] * gs_q) * inv_rms(xq, MLA_QK) + add_q_mla
                            ).astype(jnp.bfloat16)
            xk = xk2[:, sl] + kpe
            k_ref[0, hd] = ((xk * gc_k + k_rot) * inv_rms(xk, MLA_QK) + add_k_mla
                            ).astype(jnp.bfloat16)
            v_ref[0, hd] = (xv2[:, sl] + vec(V_ONES_V + e)).astype(jnp.bfloat16)

    for g in range(FOX_HEADS // 4):
        xq4 = seg(OFF_FQ + g * PAIR, PAIR)
        xk4 = seg(OFF_FK + g * PAIR, PAIR)
        xv4 = seg(OFF_FV + g * PAIR, PAIR)
        for e in range(4):
            hd, par = 4 * g + e, e % 2
            sl = slice((e // 2) * LANE, (e // 2 + 1) * LANE)
            gl = slice(hd * LANE, (hd + 1) * LANE)
            xq, xk = xq4[:, sl], xk4[:, sl]
            rq = inv_rms(jnp.where(halves[par], xq, 0.0), FOX_DIM)
            rk = inv_rms(jnp.where(halves[par], xk, 0.0), FOX_DIM)
            q_ref[0, MLA_HEADS + hd] = (xq * vec(V_GQ_FOX + par) * rq + gate_q[:, gl]
                                        + vec(V_ADD_Q_FOX + par)).astype(jnp.bfloat16)
            k_ref[0, MLA_HEADS + hd] = (xk * vec(V_GK_FOX + par) * rk + gate_k[:, gl]
                                        + add_k_fox[par]).astype(jnp.bfloat16)
            v_ref[0, MLA_HEADS + hd] = (jnp.where(halves[par], xv4[:, sl], 0.0)
                                        + vec(V_ONES_V + par)).astype(jnp.bfloat16)


def _attn_in(stream, lp, gmix, p, rope_tab, tri, selq, selk, tm):
    from_x = len(stream) == 2
    b, d = stream[0].shape[0], D_MODEL
    nt = lp // tm
    kern = functools.partial(_attn_in_kernel, tm=tm, from_x=from_x)
    stream_specs = (_input_specs(tm) if from_x
                    else [pl.BlockSpec((1, tm, d), lambda bi, i: (bi, i, 0))])
    stream_args = [stream[0]] * REAL_PARTS + [stream[1]] if from_x else [stream[0]]
    qk_shape = jax.ShapeDtypeStruct((b, HEADS, lp, LANE), jnp.bfloat16)
    qk_spec = pl.BlockSpec((1, HEADS, tm, LANE), lambda bi, i: (bi, 0, i, 0))
    return pl.pallas_call(
        kern,
        grid=(b, nt),
        in_specs=stream_specs + [
            _const_spec((1, d)),
            _const_spec((d, W_CAT)),
            _const_spec((1, Q_LORA)),
            _const_spec((Q_LORA, 2 * MLA_HEADS * LANE)),
            _const_spec((1, KV_LORA)),
            _const_spec((KV_LORA, MLA_HEADS * LANE)),
            _const_spec((KV_LORA, MLA_HEADS * LANE)),
            _const_spec((VEC_ROWS, LANE)),
            pl.BlockSpec((tm, 2 * LANE), lambda bi, i: (i, 0)),
            _const_spec((tm, tm)),
            _const_spec((LANE, FOX_HEADS * LANE)),
            _const_spec((LANE, FOX_HEADS * LANE)),
        ],
        out_specs=[qk_spec, qk_spec, qk_spec,
                   pl.BlockSpec((1, 1, 8, LANE), lambda bi, i: (bi, i, 0, 0))],
        out_shape=[qk_shape, qk_shape, qk_shape,
                   jax.ShapeDtypeStruct((b, nt, 8, LANE), jnp.float32)],
        scratch_shapes=[pltpu.VMEM((8, LANE), jnp.float32)],
        compiler_params=pltpu.CompilerParams(
            dimension_semantics=("arbitrary", "arbitrary"), vmem_limit_bytes=VMEM_LIMIT),
        name="attn_in",
    )(*stream_args, gmix, p["wcat"], p["gcq"], p["wuq"], p["gckv"], p["wkn"], p["wv"], p["vec"],
      rope_tab, tri, selq, selk)


def _flash_kernel(gate_end_ref, slack_ref, q_ref, k_ref, v_ref, o_ref, m_ref, acc_ref, al_ref,
                  p_ref, *, tq, tk, nq):
    chunks = tq // tk
    bi, hp = pl.program_id(0), pl.program_id(1)

    def query_block(qi, carry):
        qbase = qi * tq

        def softmax(j, u, base, diagonal, first=False):
            r0 = u * tk if diagonal else 0
            rows = slice(r0, tq)
            start = pl.multiple_of(base + u * tk, tk)
            q_rows = pl.ds(pl.multiple_of(qbase + r0, tk), tq - r0)
            s = lax.dot_general(q_ref[0, j, q_rows, :], k_ref[0, j, pl.ds(start, tk), :],
                                (((1,), (1,)), ((), ())), preferred_element_type=jnp.float32)
            if diagonal:
                row = lax.broadcasted_iota(jnp.int32, (tq - r0, tk), 0)
                col = lax.broadcasted_iota(jnp.int32, (tq - r0, tk), 1)
                s = jnp.where(col <= row, s, NEG)
            if first:
                m_next = jnp.broadcast_to(jnp.max(s, axis=1, keepdims=True), (tq - r0, LANE))
            else:
                m_prev = m_ref[j, rows, :]
                m_next = jnp.maximum(m_prev, jnp.max(s, axis=1, keepdims=True))
                al_ref[j, u, rows, :] = jnp.exp2(m_prev - m_next)
            p = jnp.exp2(s - jnp.concatenate([m_next] * (tk // LANE), axis=1))
            p_ref[j, u, rows, :] = p.astype(jnp.bfloat16)
            m_ref[j, rows, :] = m_next

        def pv(j, u, base, diagonal, first=False):
            r0 = u * tk if diagonal else 0
            rows = slice(r0, tq)
            start = pl.multiple_of(base + u * tk, tk)
            new = _dot(p_ref[j, u, rows, :], v_ref[0, j, pl.ds(start, tk), :])
            if first:
                acc_ref[j, rows, :] = new
            else:
                acc_ref[j, rows, :] = acc_ref[j, rows, :] * al_ref[j, u, rows, :] + new

        def diagonal_block():
            for u in range(chunks):
                softmax(0, u, qbase, True, first=(u == 0))
                if u > 0:
                    pv(1, u - 1, qbase, True, first=(u == 1))
                softmax(1, u, qbase, True, first=(u == 0))
                pv(0, u, qbase, True, first=(u == 0))
            pv(1, chunks - 1, qbase, True, first=(chunks == 1))

        def block(kb, diagonal):
            base = kb * tq
            for u in range(chunks):
                softmax(0, u, base, diagonal)
                if u == 0:
                    pv(1, chunks - 1, jnp.maximum(kb - 1, 0) * tq, False)
                else:
                    pv(1, u - 1, base, diagonal)
                softmax(1, u, base, diagonal)
                pv(0, u, base, diagonal)

        def blocks_needed(j):
            hd = 2 * hp + j
            gate_q = gate_end_ref[bi, hd, jnp.maximum(qi - 1, 0)]
            count = jnp.int32(0)
            for kb in range(nq - 1):
                keep = (kb < qi) & (gate_q - gate_end_ref[bi, hd, kb] >= slack_ref[hd])
                count = count + keep.astype(jnp.int32)
            return count

        diagonal_block()
        al_ref[1, chunks - 1] = jnp.ones((tq, LANE), jnp.float32)
        p_ref[1, chunks - 1] = jnp.zeros((tq, tk), jnp.bfloat16)
        n_off = jnp.maximum(blocks_needed(0), blocks_needed(1))
        first = qi - n_off
        odd = n_off & 1

        @pl.when(odd == 1)
        def _():
            block(first, False)

        def body(pair, c):
            kb = first + odd + 2 * pair
            block(kb, False)
            block(kb + 1, False)
            return c

        lax.fori_loop(0, lax.shift_right_logical(n_off, 1), body, 0)
        pv(1, chunks - 1, jnp.maximum(qi - 1, 0) * tq, False)

        o0 = acc_ref[0]
        o1 = acc_ref[1]
        o0 = o0 / pltpu.roll(o0, FOX_DIM, 1)
        o1 = o1 / pltpu.roll(o1, FOX_DIM, 1)
        lane = lax.broadcasted_iota(jnp.int32, (tq, LANE), 1)
        o = jnp.where(lane < MLA_V, o0, o1)
        o_ref[0, pl.ds(pl.multiple_of(qbase, tq), tq), :] = o.astype(jnp.bfloat16)
        return carry

    lax.fori_loop(0, nq, query_block, 0)


def _flash(gate_end, slack, q, k, v):
    b, _, lp, _ = q.shape
    tq, tk = FLASH_TQ, FLASH_TK
    kern = functools.partial(_flash_kernel, tq=tq, tk=tk, nq=lp // tq)
    qkv_spec = pl.BlockSpec((1, 2, lp, LANE), lambda bi, hp: (bi, hp, 0, 0))
    return pl.pallas_call(
        kern,
        grid=(b, HEADS // 2),
        in_specs=[pl.BlockSpec(memory_space=pltpu.SMEM), pl.BlockSpec(memory_space=pltpu.SMEM),
                  qkv_spec, qkv_spec, qkv_spec],
        out_specs=pl.BlockSpec((1, lp, LANE), lambda bi, hp: (bi, 0, hp)),
        out_shape=jax.ShapeDtypeStruct((b, lp, HEADS * MLA_V), jnp.bfloat16),
        scratch_shapes=[pltpu.VMEM((2, tq, LANE), jnp.float32)] * 2
        + [pltpu.VMEM((2, tq // tk, tq, LANE), jnp.float32),
           pltpu.VMEM((2, tq // tk, tq, tk), jnp.bfloat16)],
        compiler_params=pltpu.CompilerParams(
            dimension_semantics=("arbitrary", "arbitrary"), vmem_limit_bytes=VMEM_LIMIT),
        name="flash",
    )(gate_end, slack, q, k, v)


def _conv_in_kernel(h_ref, gmix_ref, win_ref, cw_ref, y_ref, gs_ref, *, tm):
    i = pl.program_id(1)

    @pl.when(i == 0)
    def _():
        gs_ref[0:8, :] = jnp.zeros((8, D_MODEL), jnp.float32)

    x = h_ref[0]
    hn = _rms(x, gmix_ref[...], D_MODEL).astype(jnp.bfloat16)
    gate_c = _dot(hn, win_ref[0, :, D_MODEL:2 * D_MODEL])
    u = _dot(hn, win_ref[0, :, 2 * D_MODEL:3 * D_MODEL])
    row = lax.broadcasted_iota(jnp.int32, (tm, D_MODEL), 0)
    g = jnp.where((i * tm + row) >= PAD, gate_c * u, 0.0)
    gs_ref[8:tm + 8, :] = g
    y = (cw_ref[0:1, :] * gs_ref[6:tm + 6, :] + cw_ref[1:2, :] * gs_ref[7:tm + 7, :]
         + cw_ref[2:3, :] * g)
    gs_ref[0:8, :] = gs_ref[tm:tm + 8, :]
    gate_b = _dot(hn, win_ref[0, :, 0:D_MODEL])
    y_ref[0] = (gate_b * y).astype(jnp.bfloat16)


def _conv_in(h, gmix, win, layer, cw, tm):
    b, lp, d = h.shape
    kern = functools.partial(_conv_in_kernel, tm=tm)
    return pl.pallas_call(
        kern,
        grid=(b, lp // tm),
        in_specs=[
            pl.BlockSpec((1, tm, d), lambda bi, i: (bi, i, 0)),
            _const_spec((1, d)),
            _layer_spec((d, 3 * d), layer),
            _const_spec((8, d)),
        ],
        out_specs=pl.BlockSpec((1, tm, d), lambda bi, i: (bi, i, 0)),
        out_shape=jax.ShapeDtypeStruct((b, lp, d), jnp.bfloat16),
        scratch_shapes=[pltpu.VMEM((tm + 8, d), jnp.float32)],
        compiler_params=pltpu.CompilerParams(
            dimension_semantics=("arbitrary", "arbitrary"), vmem_limit_bytes=VMEM_LIMIT),
        name="conv_in",
    )(h, gmix, win, cw)


def _mlp_tile(h, y, wo_ref, gmlp_ref, wup_ref, wdn_ref):
    h1 = h + _dot(y, wo_ref[0])
    n = _rms(h1, gmlp_ref[...], D_MODEL).astype(jnp.bfloat16)
    acc = h1
    for c in range(D_FF // FF_CHUNK):
        sl = slice(c * FF_CHUNK, (c + 1) * FF_CHUNK)
        a = jnp.maximum(_dot(n, wup_ref[0, :, sl]), 0.0)
        acc = acc + _dot((a * a).astype(jnp.bfloat16), wdn_ref[0, sl, :])
    return acc


def _mix_out_mlp_kernel(h_ref, y_ref, wo_ref, gmlp_ref, wup_ref, wdn_ref, out_ref):
    out_ref[...] = _mlp_tile(h_ref[...], y_ref[...], wo_ref, gmlp_ref, wup_ref, wdn_ref)


def _mix_out_mlp(h, y, wo, wo_layer, gmlp, wup, wdn, layer, tm):
    r, d = h.shape
    return pl.pallas_call(
        _mix_out_mlp_kernel,
        grid=(r // tm,),
        in_specs=[
            pl.BlockSpec((tm, d), lambda i: (i, 0)),
            pl.BlockSpec((tm, d), lambda i: (i, 0)),
            _layer_spec((d, d), wo_layer),
            _const_spec((1, d)),
            _layer_spec((d, D_FF), layer),
            _layer_spec((D_FF, d), layer),
        ],
        out_specs=pl.BlockSpec((tm, d), lambda i: (i, 0)),
        out_shape=jax.ShapeDtypeStruct((r, d), jnp.float32),
        compiler_params=pltpu.CompilerParams(
            dimension_semantics=("arbitrary",), vmem_limit_bytes=VMEM_LIMIT),
        name="mix_out_mlp",
    )(h, y, wo, gmlp, wup, wdn)


def _mix_out_mlp_first_kernel(*refs):
    x_parts, meta_ref = refs[0:REAL_PARTS], refs[REAL_PARTS]
    y_ref, wo_ref, gmlp_ref, wup_ref, wdn_ref, out_ref = refs[REAL_PARTS + 1:]
    h = _input_tile(pl.program_id(1), x_parts, meta_ref)
    out_ref[0] = _mlp_tile(h, y_ref[0], wo_ref, gmlp_ref, wup_ref, wdn_ref)


def _mix_out_mlp_first(x, meta, y, wo, wo_layer, gmlp, wup, wdn, layer, tm):
    b, lp, d = y.shape
    tile = pl.BlockSpec((1, tm, d), lambda bi, i: (bi, i, 0))
    return pl.pallas_call(
        _mix_out_mlp_first_kernel,
        grid=(b, lp // tm),
        in_specs=_input_specs(tm) + [
            tile,
            _layer_spec((d, d), wo_layer),
            _const_spec((1, d)),
            _layer_spec((d, D_FF), layer),
            _layer_spec((D_FF, d), layer),
        ],
        out_specs=tile,
        out_shape=jax.ShapeDtypeStruct((b, lp, d), jnp.float32),
        compiler_params=pltpu.CompilerParams(
            dimension_semantics=("arbitrary", "arbitrary"), vmem_limit_bytes=VMEM_LIMIT),
        name="mix_out_mlp_first",
    )(*([x] * REAL_PARTS), meta, y, wo, gmlp, wup, wdn)


def _mix_out_mlp_last_kernel(*refs):
    h_parts, y_parts = refs[0:REAL_PARTS], refs[REAL_PARTS:2 * REAL_PARTS]
    wo_ref, gmlp_ref, wup_ref, wdn_ref, out_ref = refs[2 * REAL_PARTS:]
    h = jnp.concatenate([r[0] for r in h_parts], axis=0)
    y = jnp.concatenate([r[0] for r in y_parts], axis=0)
    out_ref[0] = _mlp_tile(h, y, wo_ref, gmlp_ref, wup_ref, wdn_ref)


def _mix_out_mlp_last(h, y, wo, wo_layer, gmlp, wup, wdn, layer, tm, seq):
    b, lp, d = h.shape
    part = tm // REAL_PARTS
    last_part = lp // part - 1

    def part_spec(k):
        return pl.BlockSpec(
            (1, part, d),
            lambda bi, i: (bi, jnp.minimum(REAL_START // part + REAL_PARTS * i + k, last_part), 0))

    parts = [part_spec(k) for k in range(REAL_PARTS)]
    return pl.pallas_call(
        _mix_out_mlp_last_kernel,
        grid=(b, pl.cdiv(seq, tm)),
        in_specs=parts + parts + [
            _layer_spec((d, d), wo_layer),
            _const_spec((1, d)),
            _layer_spec((d, D_FF), layer),
            _layer_spec((D_FF, d), layer),
        ],
        out_specs=pl.BlockSpec((1, tm, d), lambda bi, i: (bi, i, 0)),
        out_shape=jax.ShapeDtypeStruct((b, seq, d), jnp.float32),
        compiler_params=pltpu.CompilerParams(
            dimension_semantics=("arbitrary", "arbitrary"), vmem_limit_bytes=VMEM_LIMIT),
        name="mix_out_mlp_last",
    )(*([h] * REAL_PARTS + [y] * REAL_PARTS), wo, gmlp, wup, wdn)


def _pad_heads(w, heads, dim):
    k = w.shape[0]
    w = w.reshape(k, heads, dim)
    w = jnp.pad(w, ((0, 0), (0, 0), (0, LANE - dim)))
    return w.reshape(k, heads * LANE)


def _lane_vec(v, offset=0):
    return jnp.zeros((LANE,), jnp.float32).at[offset:offset + v.shape[0]].set(v)


def _attn_params(w_in, g_cq, w_uq, g_ckv, w_ukv, g_q_mla, g_k_mla, g_q_fox, g_k_fox, b_forget):
    bf = jnp.bfloat16
    o1 = Q_LORA
    o2 = o1 + KV_LORA
    o3 = o2 + MLA_ROPE
    o4 = o3 + FOX_HEADS * FOX_DIM
    o5 = o4 + FOX_HEADS * FOX_DIM
    o6 = o5 + FOX_HEADS * FOX_DIM
    misc = jnp.zeros((D_MODEL, LANE), jnp.float32)
    misc = misc.at[:, MISC_GATE:MISC_GATE + FOX_HEADS].set(w_in[:, o6:])
    misc = misc.at[:, MISC_ROPE:MISC_ROPE + MLA_ROPE].set(w_in[:, o2:o3])
    wcat = jnp.concatenate([w_in[:, :o1], misc, w_in[:, o1:o2], w_in[:, o3:o6]], axis=1).astype(bf)
    kv = w_ukv.reshape(KV_LORA, MLA_HEADS, MLA_NOPE + MLA_V)
    wkn = _pad_heads(kv[:, :, :MLA_NOPE].reshape(KV_LORA, -1), MLA_HEADS, MLA_NOPE).astype(bf)
    wv = jnp.pad(kv[:, :, MLA_NOPE:].reshape(KV_LORA, MLA_HEADS // 2, 2, MLA_V),
                 ((0, 0), (0, 0), (0, 0), (0, LANE - MLA_V)))
    wv = jnp.concatenate([wv[:, :, 0], jnp.roll(wv[:, :, 1], MLA_V, axis=-1)], axis=-1)
    wv = wv.reshape(KV_LORA, MLA_HEADS * LANE).astype(bf)
    lo, mid, hi = MLA_NOPE, MLA_NOPE + HALF_ROPE, MLA_NOPE + MLA_ROPE
    uq = w_uq.reshape(Q_LORA, MLA_HEADS, MLA_QK)
    uq_sw = jnp.zeros((Q_LORA, MLA_HEADS, LANE), jnp.float32)
    uq_sw = uq_sw.at[:, :, lo:mid].set(uq[:, :, mid:hi]).at[:, :, mid:hi].set(uq[:, :, lo:mid])
    wuq = jnp.concatenate([_pad_heads(w_uq, MLA_HEADS, MLA_QK),
                           uq_sw.reshape(Q_LORA, MLA_HEADS * LANE)], axis=1).astype(bf)

    def swapped(g):
        return jnp.zeros((LANE,), jnp.float32).at[lo:mid].set(g[mid:hi]).at[mid:hi].set(g[lo:mid])

    zero = jnp.zeros((LANE,), jnp.float32)
    q_scale_mla = MLA_QK ** -0.5 * LOG2E
    rows = [zero] * VEC_ROWS
    rows[V_GQ_MLA] = _lane_vec(g_q_mla) * q_scale_mla
    rows[V_GQ_MLA_SW] = swapped(g_q_mla) * q_scale_mla
    rows[V_GK_MLA] = _lane_vec(g_k_mla)
    rows[V_GK_MLA_SW] = swapped(g_k_mla)
    for par in range(2):
        feat, extra = FEATURE_BASE[par], EXTRA_BASE[par]
        rows[V_GQ_FOX + par] = _lane_vec(g_q_fox, feat) * (FOX_DIM ** -0.5 * LOG2E)
        rows[V_GK_FOX + par] = _lane_vec(g_k_fox, feat)
        rows[V_ADD_Q_FOX + par] = (zero.at[extra + N_SPLIT:extra + 2 * N_SPLIT].set(1.0)
                                   .at[extra + FLAG_FOX_OFF].set(1.0))
        rows[V_ONES_K_FOX + par] = zero.at[extra:extra + N_SPLIT].set(1.0)
        rows[V_ONES_V + par] = zero.at[extra:extra + FOX_DIM].set(1.0)
    rows[V_B_FORGET] = _lane_vec(b_forget, MISC_GATE)
    rows[V_ADD_Q_MLA] = zero.at[FLAG_MLA].set(1.0)
    vec = jnp.stack(rows)
    return dict(wcat=wcat, gcq=g_cq[None], wuq=wuq, gckv=g_ckv[None], wkn=wkn, wv=wv, vec=vec)


def _gate_selectors():
    selq = np.zeros((LANE, FOX_HEADS * LANE), np.float32)
    selk = np.zeros((LANE, FOX_HEADS * LANE), np.float32)
    for part in range(N_SPLIT):
        for hd in range(FOX_HEADS):
            extra = hd * LANE + EXTRA_BASE[hd % 2]
            selq[part * FOX_HEADS + hd, extra + part] = 1.0
            selk[part * FOX_HEADS + hd, extra + N_SPLIT + part] = -1.0
    return jnp.asarray(selq, jnp.bfloat16), jnp.asarray(selk, jnp.bfloat16)


def _rope_table(lp):
    lane = jnp.arange(LANE, dtype=jnp.int32)
    rotary = (lane >= MLA_NOPE) & (lane < MLA_NOPE + MLA_ROPE)
    first_half = rotary & (lane < MLA_NOPE + HALF_ROPE)
    pair = ((lane - MLA_NOPE) % HALF_ROPE).astype(jnp.float32)
    inv_freq = ROPE_BASE ** (-(2.0 * pair) / MLA_ROPE)
    pos = (jnp.arange(lp, dtype=jnp.int32) - PAD).astype(jnp.float32)
    ang = pos[:, None] * inv_freq[None, :]
    cos_t = jnp.where(lane < MLA_NOPE, 1.0, jnp.where(rotary, jnp.cos(ang), 0.0))
    sin_sw = jnp.where(rotary, jnp.where(first_half, -jnp.sin(ang), jnp.sin(ang)), 0.0)
    return jnp.concatenate([cos_t, sin_sw], axis=1)


def _pruning_tables(gate_end, g_q, g_k):
    b, nt = gate_end.shape[:2]
    fox = jnp.transpose(gate_end[:, :, 0, MISC_GATE:MISC_GATE + FOX_HEADS] * LOG2E, (0, 2, 1))
    table = jnp.concatenate([jnp.zeros((b, MLA_HEADS, nt), jnp.float32), fox], axis=1)
    bound = 1.02 * FOX_DIM * (FOX_DIM ** -0.5 * LOG2E) * jnp.max(jnp.abs(g_q)) * jnp.max(jnp.abs(g_k))
    slack_fox = -(2.0 * bound + UNDERFLOW_LOG2 + 4.0)
    slack = jnp.concatenate([jnp.full((MLA_HEADS,), NEG, jnp.float32),
                             jnp.full((FOX_HEADS,), slack_fox, jnp.float32)])
    return table, slack


def _token_tile(lp):
    if lp % FLASH_TQ:
        raise ValueError(f"padded length {lp} is not a multiple of {FLASH_TQ}")
    return FLASH_TQ


def kernel(x, meta_tokens, g_mix, g_mlp, w_in_attn, g_cq, w_uq, g_ckv, w_ukv, g_q_mla, g_k_mla,
           g_q_fox, g_k_fox, b_forget, w_out_attn, w_in_conv, conv_w, w_out_conv, w_mlp_up,
           w_mlp_down):
    b, seq, d = x.shape
    assert d == D_MODEL and (PAD + N_META + seq) % BLOCK == 0
    lp = PAD + N_META + seq
    tm = _token_tile(lp)
    bf = jnp.bfloat16

    meta = meta_tokens.astype(x.dtype)
    h = None

    rope_tab = _rope_table(lp)
    tri = (jnp.arange(tm)[:, None] >= jnp.arange(tm)[None, :]).astype(bf)
    selq, selk = _gate_selectors()

    wo_attn, wo_conv, w_conv = w_out_attn.astype(bf), w_out_conv.astype(bf), w_in_conv.astype(bf)
    w_up, w_down = w_mlp_up.astype(bf), w_mlp_down.astype(bf)
    for layer in range(DEPTH):
        j = layer // 2
        gmix = g_mix[layer][None]
        if layer % 2 == 0:
            p = _attn_params(w_in_attn[j], g_cq[j], w_uq[j], g_ckv[j], w_ukv[j], g_q_mla[j],
                             g_k_mla[j], g_q_fox[j], g_k_fox[j], b_forget[j])
            stream = (x, meta) if layer == 0 else (h,)
            q, k, v, gate_end = _attn_in(stream, lp, gmix, p, rope_tab, tri, selq, selk, tm)
            y = _flash(*_pruning_tables(gate_end, g_q_fox[j], g_k_fox[j]), q, k, v)
            wo = wo_attn
        else:
            cw = jnp.zeros((8, d), jnp.float32).at[0:3].set(conv_w[j])
            y = _conv_in(h, gmix, w_conv, j, cw, tm)
            wo = wo_conv
        gmlp = g_mlp[layer][None]
        if layer == 0:
            h = _mix_out_mlp_first(x, meta, y, wo, j, gmlp, w_up, w_down, layer, tm)
        elif layer < DEPTH - 1:
            h = _mix_out_mlp(h.reshape(b * lp, d), y.reshape(b * lp, d), wo, j, gmlp,
                             w_up, w_down, layer, tm).reshape(b, lp, d)
        else:
            return _mix_out_mlp_last(h, y, wo, j, gmlp, w_up, w_down, layer, tm, seq)
```

```python
import functools

import numpy as np
import jax
import jax.numpy as jnp
from jax import lax
from jax.experimental import pallas as pl
from jax.experimental.pallas import tpu as pltpu

D_MODEL = 1024
DEPTH = 4
N_META = 16
BLOCK = 128
PAD = 2 * BLOCK - N_META
REAL_START = PAD + N_META
REAL_PARTS = 3
MLA_HEADS = 8
MLA_NOPE = 64
MLA_ROPE = 32
MLA_QK = MLA_NOPE + MLA_ROPE
MLA_V = 64
Q_LORA = 384
KV_LORA = 256
ROPE_BASE = 10000.0
FOX_HEADS = 8
FOX_DIM = 64
D_FF = 4 * D_MODEL
EPS = 1e-6
NEG = -1e30

LANE = 128
HEADS = MLA_HEADS + FOX_HEADS
HALF_ROPE = MLA_ROPE // 2
FEATURE_BASE = (0, FOX_DIM)
EXTRA_BASE = (FOX_DIM, 0)
N_SPLIT = 3
FLAG_FOX_OFF = 2 * N_SPLIT
FLAG_MLA = MLA_QK
PAD_KEY = NEG
LOG2E = 1.4426950408889634
MISC_GATE = 0
MISC_ROPE = MLA_NOPE

OFF_CQ = 0
OFF_MISC = OFF_CQ + Q_LORA
OFF_CKV = OFF_MISC + LANE
OFF_FQ = OFF_CKV + KV_LORA
OFF_FK = OFF_FQ + FOX_HEADS * FOX_DIM
OFF_FV = OFF_FK + FOX_HEADS * FOX_DIM
W_CAT = OFF_FV + FOX_HEADS * FOX_DIM

(V_GQ_MLA, V_GQ_MLA_SW, V_GK_MLA, V_GK_MLA_SW, V_ADD_Q_MLA, V_B_FORGET) = range(6)
V_GQ_FOX, V_GK_FOX, V_ADD_Q_FOX, V_ONES_K_FOX, V_ONES_V = 6, 8, 10, 12, 14
VEC_ROWS = 16
PAIR = 2 * LANE

FF_CHUNK = 1024
UNDERFLOW_LOG2 = 150.0
FLASH_TQ = 768
FLASH_TK = 256
VMEM_LIMIT = 56 * 1024 * 1024


def _const_spec(shape):
    nd = len(shape)
    return pl.BlockSpec(shape, lambda *_: (0,) * nd, pipeline_mode=pl.Buffered(1))


def _layer_spec(shape, layer):
    nd = len(shape)
    return pl.BlockSpec((1,) + shape, lambda *_: (layer,) + (0,) * nd,
                        pipeline_mode=pl.Buffered(1))


def _input_specs(tm):
    part = tm // REAL_PARTS
    assert REAL_START == part

    def part_spec(k):
        return pl.BlockSpec((1, part, D_MODEL),
                            lambda bi, i: (bi, jnp.maximum(REAL_PARTS * i + k - 1, 0), 0))

    return [part_spec(k) for k in range(REAL_PARTS)] + [_const_spec((N_META, D_MODEL))]


def _input_tile(i, x_parts, meta_ref):
    lead = jnp.concatenate([jnp.zeros((PAD, D_MODEL), jnp.float32), meta_ref[...]], axis=0)
    first = jnp.where(i == 0, lead, x_parts[0][0])
    return jnp.concatenate([first] + [r[0] for r in x_parts[1:]], axis=0)


def _rms(x, g, n):
    ms = jnp.sum(x * x, axis=-1, keepdims=True) * (1.0 / n)
    return x * lax.rsqrt(ms + EPS) * g


def _split3(x):
    hi = x.astype(jnp.bfloat16).astype(jnp.float32)
    r1 = x - hi
    mid = r1.astype(jnp.bfloat16).astype(jnp.float32)
    lo = r1 - mid
    packed = hi + pltpu.roll(mid, FOX_HEADS, 1) + pltpu.roll(lo, 2 * FOX_HEADS, 1)
    return packed.astype(jnp.bfloat16)


def _dot(a, b):
    return jnp.dot(a, b, preferred_element_type=jnp.float32)


def _attn_in_kernel(*refs, tm, from_x):
    n_stream = REAL_PARTS + 1 if from_x else 1
    stream = refs[:n_stream]
    (gmix_ref, wcat_ref, gcq_ref, wuq_ref, gckv_ref, wkn_ref, wv_ref, vec_ref, rope_ref, tri_ref,
     selq_ref, selk_ref, q_ref, k_ref, v_ref, gate_end_ref, carry_ref) = refs[n_stream:]
    i = pl.program_id(1)

    @pl.when(i == 0)
    def _():
        carry_ref[...] = jnp.zeros_like(carry_ref)

    x = _input_tile(i, stream[:-1], stream[-1]) if from_x else stream[0][0]
    hn = _rms(x, gmix_ref[...], D_MODEL).astype(jnp.bfloat16)

    def seg(lo, width):
        return _dot(hn, wcat_ref[:, lo:lo + width])

    def vec(r):
        return vec_ref[r:r + 1, :]

    cos_t = rope_ref[:, 0:LANE]
    sin_sw = rope_ref[:, LANE:2 * LANE]
    gc_q, gs_q = vec(V_GQ_MLA) * cos_t, vec(V_GQ_MLA_SW) * sin_sw
    gc_k, gs_k = vec(V_GK_MLA) * cos_t, vec(V_GK_MLA_SW) * sin_sw
    add_q_mla = vec(V_ADD_Q_MLA)

    lane = lax.broadcasted_iota(jnp.int32, (tm, LANE), 1)
    row = lax.broadcasted_iota(jnp.int32, (tm, LANE), 0)
    valid = (i * tm + row) >= PAD
    pad_key = jnp.where(valid, 0.0, PAD_KEY)
    add_k_mla = jnp.where(lane == FLAG_MLA, pad_key, 0.0)
    halves = (lane < FOX_DIM, lane >= FOX_DIM)
    add_k_fox = [vec(V_ONES_K_FOX + par)
                 + jnp.where(lane == EXTRA_BASE[par] + FLAG_FOX_OFF, pad_key, 0.0)
                 for par in range(2)]

    cq_misc = seg(OFF_CQ, Q_LORA + LANE)
    misc = cq_misc[:, Q_LORA:]
    kpe = jnp.where((lane >= MISC_ROPE) & (lane < MISC_ROPE + MLA_ROPE), misc, 0.0)
    k_rot = jnp.where(lane < MISC_ROPE + HALF_ROPE, pltpu.roll(kpe, LANE - HALF_ROPE, 1),
                      pltpu.roll(kpe, HALF_ROPE, 1)) * gs_k
    xl = misc + vec(V_B_FORGET)
    logf = jnp.minimum(xl, 0.0) - jnp.log1p(jnp.exp(-jnp.abs(xl)))
    logf = jnp.where(valid & (lane >= MISC_GATE) & (lane < MISC_GATE + FOX_HEADS), logf, 0.0)
    cs = _dot(tri_ref[...], _split3(logf))
    cs = (cs + pltpu.roll(cs, LANE - FOX_HEADS, 1)) + pltpu.roll(cs, LANE - 2 * FOX_HEADS, 1)
    cum = jnp.where(lane < FOX_HEADS, cs, 0.0) + carry_ref[0:1, :]
    carry_ref[0:1, :] = cum[tm - 1:tm, :]
    gate_end_ref[0, 0] = carry_ref[...]
    cum3 = _split3(cum * LOG2E)
    gate_q = _dot(cum3, selq_ref[...])
    gate_k = _dot(cum3, selk_ref[...])

    def inv_rms(xv, n):
        return lax.rsqrt(jnp.sum(xv * xv, axis=-1, keepdims=True) * (1.0 / n) + EPS)

    cqn = _rms(cq_misc[:, :Q_LORA], gcq_ref[...], Q_LORA).astype(jnp.bfloat16)
    ckvn = _rms(seg(OFF_CKV, KV_LORA), gckv_ref[...], KV_LORA).astype(jnp.bfloat16)
    for g in range(MLA_HEADS // 2):
        cols = slice(g * PAIR, (g + 1) * PAIR)
        cols_sw = slice(MLA_HEADS * LANE + g * PAIR, MLA_HEADS * LANE + (g + 1) * PAIR)
        xq2 = _dot(cqn, wuq_ref[:, cols])
        xq2_sw = _dot(cqn, wuq_ref[:, cols_sw])
        xk2 = _dot(ckvn, wkn_ref[:, cols])
        xv2 = _dot(ckvn, wv_ref[:, cols])
        for e in range(2):
            hd, sl = 2 * g + e, slice(e * LANE, (e + 1) * LANE)
            xq = xq2[:, sl]
            q_ref[0, hd] = ((xq * gc_q + xq2_sw[:, sl] * gs_q) * inv_rms(xq, MLA_QK) + add_q_mla
                            ).astype(jnp.bfloat16)
            xk = xk2[:, sl] + kpe
            k_ref[0, hd] = ((xk * gc_k + k_rot) * inv_rms(xk, MLA_QK) + add_k_mla
                            ).astype(jnp.bfloat16)
            v_ref[0, hd] = (xv2[:, sl] + vec(V_ONES_V + e)).astype(jnp.bfloat16)

    for g in range(FOX_HEADS // 4):
        xq4 = seg(OFF_FQ + g * PAIR, PAIR)
        xk4 = seg(OFF_FK + g * PAIR, PAIR)
        xv4 = seg(OFF_FV + g * PAIR, PAIR)
        for e in range(4):
            hd, par = 4 * g + e, e % 2
            sl = slice((e // 2) * LANE, (e // 2 + 1) * LANE)
            gl = slice(hd * LANE, (hd + 1) * LANE)
            xq, xk = xq4[:, sl], xk4[:, sl]
            rq = inv_rms(jnp.where(halves[par], xq, 0.0), FOX_DIM)
            rk = inv_rms(jnp.where(halves[par], xk, 0.0), FOX_DIM)
            q_ref[0, MLA_HEADS + hd] = (xq * vec(V_GQ_FOX + par) * rq + gate_q[:, gl]
                                        + vec(V_ADD_Q_FOX + par)).astype(jnp.bfloat16)
            k_ref[0, MLA_HEADS + hd] = (xk * vec(V_GK_FOX + par) * rk + gate_k[:, gl]
                                        + add_k_fox[par]).astype(jnp.bfloat16)
            v_ref[0, MLA_HEADS + hd] = (jnp.where(halves[par], xv4[:, sl], 0.0)
                                        + vec(V_ONES_V + par)).astype(jnp.bfloat16)


def _attn_in(stream, lp, gmix, p, rope_tab, tri, selq, selk, tm):
    from_x = len(stream) == 2
    b, d = stream[0].shape[0], D_MODEL
    nt = lp // tm
    kern = functools.partial(_attn_in_kernel, tm=tm, from_x=from_x)
    stream_specs = (_input_specs(tm) if from_x
                    else [pl.BlockSpec((1, tm, d), lambda bi, i: (bi, i, 0))])
    stream_args = [stream[0]] * REAL_PARTS + [stream[1]] if from_x else [stream[0]]
    qk_shape = jax.ShapeDtypeStruct((b, HEADS, lp, LANE), jnp.bfloat16)
    qk_spec = pl.BlockSpec((1, HEADS, tm, LANE), lambda bi, i: (bi, 0, i, 0))
    return pl.pallas_call(
        kern,
        grid=(b, nt),
        in_specs=stream_specs + [
            _const_spec((1, d)),
            _const_spec((d, W_CAT)),
            _const_spec((1, Q_LORA)),
            _const_spec((Q_LORA, 2 * MLA_HEADS * LANE)),
            _const_spec((1, KV_LORA)),
            _const_spec((KV_LORA, MLA_HEADS * LANE)),
            _const_spec((KV_LORA, MLA_HEADS * LANE)),
            _const_spec((VEC_ROWS, LANE)),
            pl.BlockSpec((tm, 2 * LANE), lambda bi, i: (i, 0)),
            _const_spec((tm, tm)),
            _const_spec((LANE, FOX_HEADS * LANE)),
            _const_spec((LANE, FOX_HEADS * LANE)),
        ],
        out_specs=[qk_spec, qk_spec, qk_spec,
                   pl.BlockSpec((1, 1, 8, LANE), lambda bi, i: (bi, i, 0, 0))],
        out_shape=[qk_shape, qk_shape, qk_shape,
                   jax.ShapeDtypeStruct((b, nt, 8, LANE), jnp.float32)],
        scratch_shapes=[pltpu.VMEM((8, LANE), jnp.float32)],
        compiler_params=pltpu.CompilerParams(
            dimension_semantics=("arbitrary", "arbitrary"), vmem_limit_bytes=VMEM_LIMIT),
        name="attn_in",
    )(*stream_args, gmix, p["wcat"], p["gcq"], p["wuq"], p["gckv"], p["wkn"], p["wv"], p["vec"],
      rope_tab, tri, selq, selk)


def _flash_kernel(gate_end_ref, slack_ref, q_ref, k_ref, v_ref, o_ref, m_ref, acc_ref, al_ref,
                  p_ref, *, tq, tk, nq):
    chunks = tq // tk
    bi, hp = pl.program_id(0), pl.program_id(1)

    def query_block(qi, carry):
        qbase = qi * tq

        def softmax(j, u, base, diagonal, first=False):
            r0 = u * tk if diagonal else 0
            rows = slice(r0, tq)
            start = pl.multiple_of(base + u * tk, tk)
            q_rows = pl.ds(pl.multiple_of(qbase + r0, tk), tq - r0)
            s = lax.dot_general(q_ref[0, j, q_rows, :], k_ref[0, j, pl.ds(start, tk), :],
                                (((1,), (1,)), ((), ())), preferred_element_type=jnp.float32)
            if diagonal:
                row = lax.broadcasted_iota(jnp.int32, (tq - r0, tk), 0)
                col = lax.broadcasted_iota(jnp.int32, (tq - r0, tk), 1)
                s = jnp.where(col <= row, s, NEG)
            if first:
                m_next = jnp.broadcast_to(jnp.max(s, axis=1, keepdims=True), (tq - r0, LANE))
            else:
                m_prev = m_ref[j, rows, :]
                m_next = jnp.maximum(m_prev, jnp.max(s, axis=1, keepdims=True))
                al_ref[j, u, rows, :] = jnp.exp2(m_prev - m_next)
            p = jnp.exp2(s - jnp.concatenate([m_next] * (tk // LANE), axis=1))
            p_ref[j, u, rows, :] = p.astype(jnp.bfloat16)
            m_ref[j, rows, :] = m_next

        def pv(j, u, base, diagonal, first=False):
            r0 = u * tk if diagonal else 0
            rows = slice(r0, tq)
            start = pl.multiple_of(base + u * tk, tk)
            new = _dot(p_ref[j, u, rows, :], v_ref[0, j, pl.ds(start, tk), :])
            if first:
                acc_ref[j, rows, :] = new
            else:
                acc_ref[j, rows, :] = acc_ref[j, rows, :] * al_ref[j, u, rows, :] + new

        def diagonal_block():
            for u in range(chunks):
                softmax(0, u, qbase, True, first=(u == 0))
                if u > 0:
                    pv(1, u - 1, qbase, True, first=(u == 1))
                softmax(1, u, qbase, True, first=(u == 0))
                pv(0, u, qbase, True, first=(u == 0))
            pv(1, chunks - 1, qbase, True, first=(chunks == 1))

        def block(kb, diagonal):
            base = kb * tq
            for u in range(chunks):
                softmax(0, u, base, diagonal)
                if u == 0:
                    pv(1, chunks - 1, jnp.maximum(kb - 1, 0) * tq, False)
                else:
                    pv(1, u - 1, base, diagonal)
                softmax(1, u, base, diagonal)
                pv(0, u, base, diagonal)

        def blocks_needed(j):
            hd = 2 * hp + j
            gate_q = gate_end_ref[bi, hd, jnp.maximum(qi - 1, 0)]
            count = jnp.int32(0)
            for kb in range(nq - 1):
                keep = (kb < qi) & (gate_q - gate_end_ref[bi, hd, kb] >= slack_ref[hd])
                count = count + keep.astype(jnp.int32)
            return count

        diagonal_block()
        al_ref[1, chunks - 1] = jnp.ones((tq, LANE), jnp.float32)
        p_ref[1, chunks - 1] = jnp.zeros((tq, tk), jnp.bfloat16)
        n_off = jnp.maximum(blocks_needed(0), blocks_needed(1))
        first = qi - n_off
        odd = n_off & 1

        @pl.when(odd == 1)
        def _():
            block(first, False)

        def body(pair, c):
            kb = first + odd + 2 * pair
            block(kb, False)
            block(kb + 1, False)
            return c

        lax.fori_loop(0, lax.shift_right_logical(n_off, 1), body, 0)
        pv(1, chunks - 1, jnp.maximum(qi - 1, 0) * tq, False)

        o0 = acc_ref[0]
        o1 = acc_ref[1]
        o0 = o0 / pltpu.roll(o0, FOX_DIM, 1)
        o1 = o1 / pltpu.roll(o1, FOX_DIM, 1)
        lane = lax.broadcasted_iota(jnp.int32, (tq, LANE), 1)
        o = jnp.where(lane < MLA_V, o0, o1)
        o_ref[0, pl.ds(pl.multiple_of(qbase, tq), tq), :] = o.astype(jnp.bfloat16)
        return carry

    lax.fori_loop(0, nq, query_block, 0)


def _flash(gate_end, slack, q, k, v):
    b, _, lp, _ = q.shape
    tq, tk = FLASH_TQ, FLASH_TK
    kern = functools.partial(_flash_kernel, tq=tq, tk=tk, nq=lp // tq)
    qkv_spec = pl.BlockSpec((1, 2, lp, LANE), lambda bi, hp: (bi, hp, 0, 0))
    return pl.pallas_call(
        kern,
        grid=(b, HEADS // 2),
        in_specs=[pl.BlockSpec(memory_space=pltpu.SMEM), pl.BlockSpec(memory_space=pltpu.SMEM),
                  qkv_spec, qkv_spec, qkv_spec],
        out_specs=pl.BlockSpec((1, lp, LANE), lambda bi, hp: (bi, 0, hp)),
        out_shape=jax.ShapeDtypeStruct((b, lp, HEADS * MLA_V), jnp.bfloat16),
        scratch_shapes=[pltpu.VMEM((2, tq, LANE), jnp.float32)] * 2
        + [pltpu.VMEM((2, tq // tk, tq, LANE), jnp.float32),
           pltpu.VMEM((2, tq // tk, tq, tk), jnp.bfloat16)],
        compiler_params=pltpu.CompilerParams(
            dimension_semantics=("arbitrary", "arbitrary"), vmem_limit_bytes=VMEM_LIMIT),
        name="flash",
    )(gate_end, slack, q, k, v)


def _conv_in_kernel(h_ref, gmix_ref, win_ref, cw_ref, y_ref, gs_ref, *, tm):
    i = pl.program_id(1)

    @pl.when(i == 0)
    def _():
        gs_ref[0:8, :] = jnp.zeros((8, D_MODEL), jnp.float32)

    x = h_ref[0]
    hn = _rms(x, gmix_ref[...], D_MODEL).astype(jnp.bfloat16)
    gate_c = _dot(hn, win_ref[0, :, D_MODEL:2 * D_MODEL])
    u = _dot(hn, win_ref[0, :, 2 * D_MODEL:3 * D_MODEL])
    row = lax.broadcasted_iota(jnp.int32, (tm, D_MODEL), 0)
    g = jnp.where((i * tm + row) >= PAD, gate_c * u, 0.0)
    gs_ref[8:tm + 8, :] = g
    y = (cw_ref[0:1, :] * gs_ref[6:tm + 6, :] + cw_ref[1:2, :] * gs_ref[7:tm + 7, :]
         + cw_ref[2:3, :] * g)
    gs_ref[0:8, :] = gs_ref[tm:tm + 8, :]
    gate_b = _dot(hn, win_ref[0, :, 0:D_MODEL])
    y_ref[0] = (gate_b * y).astype(jnp.bfloat16)


def _conv_in(h, gmix, win, layer, cw, tm):
    b, lp, d = h.shape
    kern = functools.partial(_conv_in_kernel, tm=tm)
    return pl.pallas_call(
        kern,
        grid=(b, lp // tm),
        in_specs=[
            pl.BlockSpec((1, tm, d), lambda bi, i: (bi, i, 0)),
            _const_spec((1, d)),
            _layer_spec((d, 3 * d), layer),
            _const_spec((8, d)),
        ],
        out_specs=pl.BlockSpec((1, tm, d), lambda bi, i: (bi, i, 0)),
        out_shape=jax.ShapeDtypeStruct((b, lp, d), jnp.bfloat16),
        scratch_shapes=[pltpu.VMEM((tm + 8, d), jnp.float32)],
        compiler_params=pltpu.CompilerParams(
            dimension_semantics=("arbitrary", "arbitrary"), vmem_limit_bytes=VMEM_LIMIT),
        name="conv_in",
    )(h, gmix, win, cw)


def _mlp_tile(h, y, wo_ref, gmlp_ref, wup_ref, wdn_ref):
    h1 = h + _dot(y, wo_ref[0])
    n = _rms(h1, gmlp_ref[...], D_MODEL).astype(jnp.bfloat16)
    acc = h1
    for c in range(D_FF // FF_CHUNK):
        sl = slice(c * FF_CHUNK, (c + 1) * FF_CHUNK)
        a = jnp.maximum(_dot(n, wup_ref[0, :, sl]), 0.0)
        acc = acc + _dot((a * a).astype(jnp.bfloat16), wdn_ref[0, sl, :])
    return acc


def _mix_out_mlp_kernel(h_ref, y_ref, wo_ref, gmlp_ref, wup_ref, wdn_ref, out_ref):
    out_ref[...] = _mlp_tile(h_ref[...], y_ref[...], wo_ref, gmlp_ref, wup_ref, wdn_ref)


def _mix_out_mlp(h, y, wo, wo_layer, gmlp, wup, wdn, layer, tm):
    r, d = h.shape
    return pl.pallas_call(
        _mix_out_mlp_kernel,
        grid=(r // tm,),
        in_specs=[
            pl.BlockSpec((tm, d), lambda i: (i, 0)),
            pl.BlockSpec((tm, d), lambda i: (i, 0)),
            _layer_spec((d, d), wo_layer),
            _const_spec((1, d)),
            _layer_spec((d, D_FF), layer),
            _layer_spec((D_FF, d), layer),
        ],
        out_specs=pl.BlockSpec((tm, d), lambda i: (i, 0)),
        out_shape=jax.ShapeDtypeStruct((r, d), jnp.float32),
        compiler_params=pltpu.CompilerParams(
            dimension_semantics=("arbitrary",), vmem_limit_bytes=VMEM_LIMIT),
        name="mix_out_mlp",
    )(h, y, wo, gmlp, wup, wdn)


def _mix_out_mlp_first_kernel(*refs):
    x_parts, meta_ref = refs[0:REAL_PARTS], refs[REAL_PARTS]
    y_ref, wo_ref, gmlp_ref, wup_ref, wdn_ref, out_ref = refs[REAL_PARTS + 1:]
    h = _input_tile(pl.program_id(1), x_parts, meta_ref)
    out_ref[0] = _mlp_tile(h, y_ref[0], wo_ref, gmlp_ref, wup_ref, wdn_ref)


def _mix_out_mlp_first(x, meta, y, wo, wo_layer, gmlp, wup, wdn, layer, tm):
    b, lp, d = y.shape
    tile = pl.BlockSpec((1, tm, d), lambda bi, i: (bi, i, 0))
    return pl.pallas_call(
        _mix_out_mlp_first_kernel,
        grid=(b, lp // tm),
        in_specs=_input_specs(tm) + [
            tile,
            _layer_spec((d, d), wo_layer),
            _const_spec((1, d)),
            _layer_spec((d, D_FF), layer),
            _layer_spec((D_FF, d), layer),
        ],
        out_specs=tile,
        out_shape=jax.ShapeDtypeStruct((b, lp, d), jnp.float32),
        compiler_params=pltpu.CompilerParams(
            dimension_semantics=("arbitrary", "arbitrary"), vmem_limit_bytes=VMEM_LIMIT),
        name="mix_out_mlp_first",
    )(*([x] * REAL_PARTS), meta, y, wo, gmlp, wup, wdn)


def _mix_out_mlp_last_kernel(*refs):
    h_parts, y_parts = refs[0:REAL_PARTS], refs[REAL_PARTS:2 * REAL_PARTS]
    wo_ref, gmlp_ref, wup_ref, wdn_ref, out_ref = refs[2 * REAL_PARTS:]
    h = jnp.concatenate([r[0] for r in h_parts], axis=0)
    y = jnp.concatenate([r[0] for r in y_parts], axis=0)
    out_ref[0] = _mlp_tile(h, y, wo_ref, gmlp_ref, wup_ref, wdn_ref)


def _mix_out_mlp_last(h, y, wo, wo_layer, gmlp, wup, wdn, layer, tm, seq):
    b, lp, d = h.shape
    part = tm // REAL_PARTS
    last_part = lp // part - 1

    def part_spec(k):
        return pl.BlockSpec(
            (1, part, d),
            lambda bi, i: (bi, jnp.minimum(REAL_START // part + REAL_PARTS * i + k, last_part), 0))

    parts = [part_spec(k) for k in range(REAL_PARTS)]
    return pl.pallas_call(
        _mix_out_mlp_last_kernel,
        grid=(b, pl.cdiv(seq, tm)),
        in_specs=parts + parts + [
            _layer_spec((d, d), wo_layer),
            _const_spec((1, d)),
            _layer_spec((d, D_FF), layer),
            _layer_spec((D_FF, d), layer),
        ],
        out_specs=pl.BlockSpec((1, tm, d), lambda bi, i: (bi, i, 0)),
        out_shape=jax.ShapeDtypeStruct((b, seq, d), jnp.float32),
        compiler_params=pltpu.CompilerParams(
            dimension_semantics=("arbitrary", "arbitrary"), vmem_limit_bytes=VMEM_LIMIT),
        name="mix_out_mlp_last",
    )(*([h] * REAL_PARTS + [y] * REAL_PARTS), wo, gmlp, wup, wdn)


def _pad_heads(w, heads, dim):
    k = w.shape[0]
    w = w.reshape(k, heads, dim)
    w = jnp.pad(w, ((0, 0), (0, 0), (0, LANE - dim)))
    return w.reshape(k, heads * LANE)


def _lane_vec(v, offset=0):
    return jnp.zeros((LANE,), jnp.float32).at[offset:offset + v.shape[0]].set(v)


def _attn_params(w_in, g_cq, w_uq, g_ckv, w_ukv, g_q_mla, g_k_mla, g_q_fox, g_k_fox, b_forget):
    bf = jnp.bfloat16
    o1 = Q_LORA
    o2 = o1 + KV_LORA
    o3 = o2 + MLA_ROPE
    o4 = o3 + FOX_HEADS * FOX_DIM
    o5 = o4 + FOX_HEADS * FOX_DIM
    o6 = o5 + FOX_HEADS * FOX_DIM
    misc = jnp.zeros((D_MODEL, LANE), jnp.float32)
    misc = misc.at[:, MISC_GATE:MISC_GATE + FOX_HEADS].set(w_in[:, o6:])
    misc = misc.at[:, MISC_ROPE:MISC_ROPE + MLA_ROPE].set(w_in[:, o2:o3])
    wcat = jnp.concatenate([w_in[:, :o1], misc, w_in[:, o1:o2], w_in[:, o3:o6]], axis=1).astype(bf)
    kv = w_ukv.reshape(KV_LORA, MLA_HEADS, MLA_NOPE + MLA_V)
    wkn = _pad_heads(kv[:, :, :MLA_NOPE].reshape(KV_LORA, -1), MLA_HEADS, MLA_NOPE).astype(bf)
    wv = jnp.pad(kv[:, :, MLA_NOPE:].reshape(KV_LORA, MLA_HEADS // 2, 2, MLA_V),
                 ((0, 0), (0, 0), (0, 0), (0, LANE - MLA_V)))
    wv = jnp.concatenate([wv[:, :, 0], jnp.roll(wv[:, :, 1], MLA_V, axis=-1)], axis=-1)
    wv = wv.reshape(KV_LORA, MLA_HEADS * LANE).astype(bf)
    lo, mid, hi = MLA_NOPE, MLA_NOPE + HALF_ROPE, MLA_NOPE + MLA_ROPE
    uq = w_uq.reshape(Q_LORA, MLA_HEADS, MLA_QK)
    uq_sw = jnp.zeros((Q_LORA, MLA_HEADS, LANE), jnp.float32)
    uq_sw = uq_sw.at[:, :, lo:mid].set(uq[:, :, mid:hi]).at[:, :, mid:hi].set(uq[:, :, lo:mid])
    wuq = jnp.concatenate([_pad_heads(w_uq, MLA_HEADS, MLA_QK),
                           uq_sw.reshape(Q_LORA, MLA_HEADS * LANE)], axis=1).astype(bf)

    def swapped(g):
        return jnp.zeros((LANE,), jnp.float32).at[lo:mid].set(g[mid:hi]).at[mid:hi].set(g[lo:mid])

    zero = jnp.zeros((LANE,), jnp.float32)
    q_scale_mla = MLA_QK ** -0.5 * LOG2E
    rows = [zero] * VEC_ROWS
    rows[V_GQ_MLA] = _lane_vec(g_q_mla) * q_scale_mla
    rows[V_GQ_MLA_SW] = swapped(g_q_mla) * q_scale_mla
    rows[V_GK_MLA] = _lane_vec(g_k_mla)
    rows[V_GK_MLA_SW] = swapped(g_k_mla)
    for par in range(2):
        feat, extra = FEATURE_BASE[par], EXTRA_BASE[par]
        rows[V_GQ_FOX + par] = _lane_vec(g_q_fox, feat) * (FOX_DIM ** -0.5 * LOG2E)
        rows[V_GK_FOX + par] = _lane_vec(g_k_fox, feat)
        rows[V_ADD_Q_FOX + par] = (zero.at[extra + N_SPLIT:extra + 2 * N_SPLIT].set(1.0)
                                   .at[extra + FLAG_FOX_OFF].set(1.0))
        rows[V_ONES_K_FOX + par] = zero.at[extra:extra + N_SPLIT].set(1.0)
        rows[V_ONES_V + par] = zero.at[extra:extra + FOX_DIM].set(1.0)
    rows[V_B_FORGET] = _lane_vec(b_forget, MISC_GATE)
    rows[V_ADD_Q_MLA] = zero.at[FLAG_MLA].set(1.0)
    vec = jnp.stack(rows)
    return dict(wcat=wcat, gcq=g_cq[None], wuq=wuq, gckv=g_ckv[None], wkn=wkn, wv=wv, vec=vec)


def _gate_selectors():
    selq = np.zeros((LANE, FOX_HEADS * LANE), np.float32)
    selk = np.zeros((LANE, FOX_HEADS * LANE), np.float32)
    for part in range(N_SPLIT):
        for hd in range(FOX_HEADS):
            extra = hd * LANE + EXTRA_BASE[hd % 2]
            selq[part * FOX_HEADS + hd, extra + part] = 1.0
            selk[part * FOX_HEADS + hd, extra + N_SPLIT + part] = -1.0
    return jnp.asarray(selq, jnp.bfloat16), jnp.asarray(selk, jnp.bfloat16)


def _rope_table(lp):
    lane = jnp.arange(LANE, dtype=jnp.int32)
    rotary = (lane >= MLA_NOPE) & (lane < MLA_NOPE + MLA_ROPE)
    first_half = rotary & (lane < MLA_NOPE + HALF_ROPE)
    pair = ((lane - MLA_NOPE) % HALF_ROPE).astype(jnp.float32)
    inv_freq = ROPE_BASE ** (-(2.0 * pair) / MLA_ROPE)
    pos = (jnp.arange(lp, dtype=jnp.int32) - PAD).astype(jnp.float32)
    ang = pos[:, None] * inv_freq[None, :]
    cos_t = jnp.where(lane < MLA_NOPE, 1.0, jnp.where(rotary, jnp.cos(ang), 0.0))
    sin_sw = jnp.where(rotary, jnp.where(first_half, -jnp.sin(ang), jnp.sin(ang)), 0.0)
    return jnp.concatenate([cos_t, sin_sw], axis=1)


def _pruning_tables(gate_end, g_q, g_k):
    b, nt = gate_end.shape[:2]
    fox = jnp.transpose(gate_end[:, :, 0, MISC_GATE:MISC_GATE + FOX_HEADS] * LOG2E, (0, 2, 1))
    table = jnp.concatenate([jnp.zeros((b, MLA_HEADS, nt), jnp.float32), fox], axis=1)
    bound = 1.02 * FOX_DIM * (FOX_DIM ** -0.5 * LOG2E) * jnp.max(jnp.abs(g_q)) * jnp.max(jnp.abs(g_k))
    slack_fox = -(2.0 * bound + UNDERFLOW_LOG2 + 4.0)
    slack = jnp.concatenate([jnp.full((MLA_HEADS,), NEG, jnp.float32),
                             jnp.full((FOX_HEADS,), slack_fox, jnp.float32)])
    return table, slack


def _token_tile(lp):
    if lp % FLASH_TQ:
        raise ValueError(f"padded length {lp} is not a multiple of {FLASH_TQ}")
    return FLASH_TQ


def kernel(x, meta_tokens, g_mix, g_mlp, w_in_attn, g_cq, w_uq, g_ckv, w_ukv, g_q_mla, g_k_mla,
           g_q_fox, g_k_fox, b_forget, w_out_attn, w_in_conv, conv_w, w_out_conv, w_mlp_up,
           w_mlp_down):
    b, seq, d = x.shape
    assert d == D_MODEL and (PAD + N_META + seq) % BLOCK == 0
    lp = PAD + N_META + seq
    tm = _token_tile(lp)
    bf = jnp.bfloat16

    meta = meta_tokens.astype(x.dtype)
    h = None

    rope_tab = _rope_table(lp)
    tri = (jnp.arange(tm)[:, None] >= jnp.arange(tm)[None, :]).astype(bf)
    selq, selk = _gate_selectors()

    wo_attn, wo_conv, w_conv = w_out_attn.astype(bf), w_out_conv.astype(bf), w_in_conv.astype(bf)
    w_up, w_down = w_mlp_up.astype(bf), w_mlp_down.astype(bf)
    for layer in range(DEPTH):
        j = layer // 2
        gmix = g_mix[layer][None]
        if layer % 2 == 0:
            p = _attn_params(w_in_attn[j], g_cq[j], w_uq[j], g_ckv[j], w_ukv[j], g_q_mla[j],
                             g_k_mla[j], g_q_fox[j], g_k_fox[j], b_forget[j])
            stream = (x, meta) if layer == 0 else (h,)
            q, k, v, gate_end = _attn_in(stream, lp, gmix, p, rope_tab, tri, selq, selk, tm)
            y = _flash(*_pruning_tables(gate_end, g_q_fox[j], g_k_fox[j]), q, k, v)
            wo = wo_attn
        else:
            cw = jnp.zeros((8, d), jnp.float32).at[0:3].set(conv_w[j])
            y = _conv_in(h, gmix, w_conv, j, cw, tm)
            wo = wo_conv
        gmlp = g_mlp[layer][None]
        if layer == 0:
            h = _mix_out_mlp_first(x, meta, y, wo, j, gmlp, w_up, w_down, layer, tm)
        elif layer < DEPTH - 1:
            h = _mix_out_mlp(h.reshape(b * lp, d), y.reshape(b * lp, d), wo, j, gmlp,
                             w_up, w_down, layer, tm).reshape(b, lp, d)
        else:
            return _mix_out_mlp_last(h, y, wo, j, gmlp, w_up, w_down, layer, tm, seq)
```

```python
import functools

import numpy as np
import jax
import jax.numpy as jnp
from jax import lax
from jax.experimental import pallas as pl
from jax.experimental.pallas import tpu as pltpu

D_MODEL = 1024
DEPTH = 4
N_META = 16
BLOCK = 128
PAD = 2 * BLOCK - N_META
REAL_START = PAD + N_META
REAL_PARTS = 3
MLA_HEADS = 8
MLA_NOPE = 64
MLA_ROPE = 32
MLA_QK = MLA_NOPE + MLA_ROPE
MLA_V = 64
Q_LORA = 384
KV_LORA = 256
ROPE_BASE = 10000.0
FOX_HEADS = 8
FOX_DIM = 64
D_FF = 4 * D_MODEL
EPS = 1e-6
NEG = -1e30

LANE = 128
HEADS = MLA_HEADS + FOX_HEADS
HALF_ROPE = MLA_ROPE // 2
FEATURE_BASE = (0, FOX_DIM)
EXTRA_BASE = (FOX_DIM, 0)
N_SPLIT = 3
FLAG_FOX_OFF = 2 * N_SPLIT
FLAG_MLA = MLA_QK
PAD_KEY = NEG
LOG2E = 1.4426950408889634
MISC_GATE = 0
MISC_ROPE = MLA_NOPE

OFF_CQ = 0
OFF_MISC = OFF_CQ + Q_LORA
OFF_CKV = OFF_MISC + LANE
OFF_FQ = OFF_CKV + KV_LORA
OFF_FK = OFF_FQ + FOX_HEADS * FOX_DIM
OFF_FV = OFF_FK + FOX_HEADS * FOX_DIM
W_CAT = OFF_FV + FOX_HEADS * FOX_DIM

(V_GQ_MLA, V_GQ_MLA_SW, V_GK_MLA, V_GK_MLA_SW, V_ADD_Q_MLA, V_B_FORGET) = range(6)
V_GQ_FOX, V_GK_FOX, V_ADD_Q_FOX, V_ONES_K_FOX, V_ONES_V = 6, 8, 10, 12, 14
VEC_ROWS = 16
PAIR = 2 * LANE

FF_CHUNK = 1024
UNDERFLOW_LOG2 = 150.0
FLASH_TQ = 768
FLASH_TK = 256
VMEM_LIMIT = 56 * 1024 * 1024


def _const_spec(shape):
    nd = len(shape)
    return pl.BlockSpec(shape, lambda *_: (0,) * nd, pipeline_mode=pl.Buffered(1))


def _layer_spec(shape, layer):
    nd = len(shape)
    return pl.BlockSpec((1,) + shape, lambda *_: (layer,) + (0,) * nd,
                        pipeline_mode=pl.Buffered(1))


def _input_specs(tm):
    part = tm // REAL_PARTS
    assert REAL_START == part

    def part_spec(k):
        return pl.BlockSpec((1, part, D_MODEL),
                            lambda bi, i: (bi, jnp.maximum(REAL_PARTS * i + k - 1, 0), 0))

    return [part_spec(k) for k in range(REAL_PARTS)] + [_const_spec((N_META, D_MODEL))]


def _input_tile(i, x_parts, meta_ref):
    lead = jnp.concatenate([jnp.zeros((PAD, D_MODEL), jnp.float32), meta_ref[...]], axis=0)
    first = jnp.where(i == 0, lead, x_parts[0][0])
    return jnp.concatenate([first] + [r[0] for r in x_parts[1:]], axis=0)


def _rms(x, g, n):
    ms = jnp.sum(x * x, axis=-1, keepdims=True) * (1.0 / n)
    return x * lax.rsqrt(ms + EPS) * g


def _split3(x):
    hi = x.astype(jnp.bfloat16).astype(jnp.float32)
    r1 = x - hi
    mid = r1.astype(jnp.bfloat16).astype(jnp.float32)
    lo = r1 - mid
    packed = hi + pltpu.roll(mid, FOX_HEADS, 1) + pltpu.roll(lo, 2 * FOX_HEADS, 1)
    return packed.astype(jnp.bfloat16)


def _dot(a, b):
    return jnp.dot(a, b, preferred_element_type=jnp.float32)


def _attn_in_kernel(*refs, tm, from_x):
    n_stream = REAL_PARTS + 1 if from_x else 1
    stream = refs[:n_stream]
    (gmix_ref, wcat_ref, gcq_ref, wuq_ref, gckv_ref, wkn_ref, wv_ref, vec_ref, rope_ref, tri_ref,
     selq_ref, selk_ref, q_ref, k_ref, v_ref, gate_end_ref, carry_ref) = refs[n_stream:]
    i = pl.program_id(1)

    @pl.when(i == 0)
    def _():
        carry_ref[...] = jnp.zeros_like(carry_ref)

    x = _input_tile(i, stream[:-1], stream[-1]) if from_x else stream[0][0]
    hn = _rms(x, gmix_ref[...], D_MODEL).astype(jnp.bfloat16)

    def seg(lo, width):
        return _dot(hn, wcat_ref[:, lo:lo + width])

    def vec(r):
        return vec_ref[r:r + 1, :]

    cos_t = rope_ref[:, 0:LANE]
    sin_sw = rope_ref[:, LANE:2 * LANE]
    gc_q, gs_q = vec(V_GQ_MLA) * cos_t, vec(V_GQ_MLA_SW) * sin_sw
    gc_k, gs_k = vec(V_GK_MLA) * cos_t, vec(V_GK_MLA_SW) * sin_sw
    add_q_mla = vec(V_ADD_Q_MLA)

    lane = lax.broadcasted_iota(jnp.int32, (tm, LANE), 1)
    row = lax.broadcasted_iota(jnp.int32, (tm, LANE), 0)
    valid = (i * tm + row) >= PAD
    pad_key = jnp.where(valid, 0.0, PAD_KEY)
    add_k_mla = jnp.where(lane == FLAG_MLA, pad_key, 0.0)
    halves = (lane < FOX_DIM, lane >= FOX_DIM)
    add_k_fox = [vec(V_ONES_K_FOX + par)
                 + jnp.where(lane == EXTRA_BASE[par] + FLAG_FOX_OFF, pad_key, 0.0)
                 for par in range(2)]

    cq_misc = seg(OFF_CQ, Q_LORA + LANE)
    misc = cq_misc[:, Q_LORA:]
    kpe = jnp.where((lane >= MISC_ROPE) & (lane < MISC_ROPE + MLA_ROPE), misc, 0.0)
    k_rot = jnp.where(lane < MISC_ROPE + HALF_ROPE, pltpu.roll(kpe, LANE - HALF_ROPE, 1),
                      pltpu.roll(kpe, HALF_ROPE, 1)) * gs_k
    xl = misc + vec(V_B_FORGET)
    logf = jnp.minimum(xl, 0.0) - jnp.log1p(jnp.exp(-jnp.abs(xl)))
    logf = jnp.where(valid & (lane >= MISC_GATE) & (lane < MISC_GATE + FOX_HEADS), logf, 0.0)
    cs = _dot(tri_ref[...], _split3(logf))
    cs = (cs + pltpu.roll(cs, LANE - FOX_HEADS, 1)) + pltpu.roll(cs, LANE - 2 * FOX_HEADS, 1)
    cum = jnp.where(lane < FOX_HEADS, cs, 0.0) + carry_ref[0:1, :]
    carry_ref[0:1, :] = cum[tm - 1:tm, :]
    gate_end_ref[0, 0] = carry_ref[...]
    cum3 = _split3(cum * LOG2E)
    gate_q = _dot(cum3, selq_ref[...])
    gate_k = _dot(cum3, selk_ref[...])

    def inv_rms(xv, n):
        return lax.rsqrt(jnp.sum(xv * xv, axis=-1, keepdims=True) * (1.0 / n) + EPS)

    def fox_group(g):
        xq4 = seg(OFF_FQ + g * PAIR, PAIR)
        xk4 = seg(OFF_FK + g * PAIR, PAIR)
        xv4 = seg(OFF_FV + g * PAIR, PAIR)
        for e in range(4):
            hd, par = 4 * g + e, e % 2
            sl = slice((e // 2) * LANE, (e // 2 + 1) * LANE)
            gl = slice(hd * LANE, (hd + 1) * LANE)
            xq, xk = xq4[:, sl], xk4[:, sl]
            rq = inv_rms(jnp.where(halves[par], xq, 0.0), FOX_DIM)
            rk = inv_rms(jnp.where(halves[par], xk, 0.0), FOX_DIM)
            q_ref[0, MLA_HEADS + hd] = (xq * vec(V_GQ_FOX + par) * rq + gate_q[:, gl]
                                        + vec(V_ADD_Q_FOX + par)).astype(jnp.bfloat16)
            k_ref[0, MLA_HEADS + hd] = (xk * vec(V_GK_FOX + par) * rk + gate_k[:, gl]
                                        + add_k_fox[par]).astype(jnp.bfloat16)
            v_ref[0, MLA_HEADS + hd] = (jnp.where(halves[par], xv4[:, sl], 0.0)
                                        + vec(V_ONES_V + par)).astype(jnp.bfloat16)

    cqn = _rms(cq_misc[:, :Q_LORA], gcq_ref[...], Q_LORA).astype(jnp.bfloat16)
    ckvn = _rms(seg(OFF_CKV, KV_LORA), gckv_ref[...], KV_LORA).astype(jnp.bfloat16)
    def mla_pair(g):
        cols = slice(g * PAIR, (g + 1) * PAIR)
        cols_sw = slice(MLA_HEADS * LANE + g * PAIR, MLA_HEADS * LANE + (g + 1) * PAIR)
        xq2 = _dot(cqn, wuq_ref[:, cols])
        xq2_sw = _dot(cqn, wuq_ref[:, cols_sw])
        xk2 = _dot(ckvn, wkn_ref[:, cols])
        xv2 = _dot(ckvn, wv_ref[:, cols])
        for e in range(2):
            hd, sl = 2 * g + e, slice(e * LANE, (e + 1) * LANE)
            xq = xq2[:, sl]
            q_ref[0, hd] = ((xq * gc_q + xq2_sw[:, sl] * gs_q) * inv_rms(xq, MLA_QK) + add_q_mla
                            ).astype(jnp.bfloat16)
            xk = xk2[:, sl] + kpe
            k_ref[0, hd] = ((xk * gc_k + k_rot) * inv_rms(xk, MLA_QK) + add_k_mla
                            ).astype(jnp.bfloat16)
            v_ref[0, hd] = (xv2[:, sl] + vec(V_ONES_V + e)).astype(jnp.bfloat16)

    fox_group(0)
    mla_pair(0)
    mla_pair(1)
    fox_group(1)
    mla_pair(2)
    mla_pair(3)


def _attn_in(stream, lp, gmix, p, rope_tab, tri, selq, selk, tm):
    from_x = len(stream) == 2
    b, d = stream[0].shape[0], D_MODEL
    nt = lp // tm
    kern = functools.partial(_attn_in_kernel, tm=tm, from_x=from_x)
    stream_specs = (_input_specs(tm) if from_x
                    else [pl.BlockSpec((1, tm, d), lambda bi, i: (bi, i, 0))])
    stream_args = [stream[0]] * REAL_PARTS + [stream[1]] if from_x else [stream[0]]
    qk_shape = jax.ShapeDtypeStruct((b, HEADS, lp, LANE), jnp.bfloat16)
    qk_spec = pl.BlockSpec((1, HEADS, tm, LANE), lambda bi, i: (bi, 0, i, 0))
    return pl.pallas_call(
        kern,
        grid=(b, nt),
        in_specs=stream_specs + [
            _const_spec((1, d)),
            _const_spec((d, W_CAT)),
            _const_spec((1, Q_LORA)),
            _const_spec((Q_LORA, 2 * MLA_HEADS * LANE)),
            _const_spec((1, KV_LORA)),
            _const_spec((KV_LORA, MLA_HEADS * LANE)),
            _const_spec((KV_LORA, MLA_HEADS * LANE)),
            _const_spec((VEC_ROWS, LANE)),
            pl.BlockSpec((tm, 2 * LANE), lambda bi, i: (i, 0)),
            _const_spec((tm, tm)),
            _const_spec((LANE, FOX_HEADS * LANE)),
            _const_spec((LANE, FOX_HEADS * LANE)),
        ],
        out_specs=[qk_spec, qk_spec, qk_spec,
                   pl.BlockSpec((1, 1, 8, LANE), lambda bi, i: (bi, i, 0, 0))],
        out_shape=[qk_shape, qk_shape, qk_shape,
                   jax.ShapeDtypeStruct((b, nt, 8, LANE), jnp.float32)],
        scratch_shapes=[pltpu.VMEM((8, LANE), jnp.float32)],
        compiler_params=pltpu.CompilerParams(
            dimension_semantics=("arbitrary", "arbitrary"), vmem_limit_bytes=VMEM_LIMIT),
        name="attn_in",
    )(*stream_args, gmix, p["wcat"], p["gcq"], p["wuq"], p["gckv"], p["wkn"], p["wv"], p["vec"],
      rope_tab, tri, selq, selk)


def _flash_kernel(gate_end_ref, slack_ref, q_ref, k_ref, v_ref, o_ref, m_ref, acc_ref, al_ref,
                  p_ref, *, tq, tk, nq):
    chunks = tq // tk
    bi, hp = pl.program_id(0), pl.program_id(1)

    def query_block(qi, carry):
        qbase = qi * tq

        def softmax(j, u, base, diagonal, first=False):
            r0 = u * tk if diagonal else 0
            rows = slice(r0, tq)
            start = pl.multiple_of(base + u * tk, tk)
            q_rows = pl.ds(pl.multiple_of(qbase + r0, tk), tq - r0)
            s = lax.dot_general(q_ref[0, j, q_rows, :], k_ref[0, j, pl.ds(start, tk), :],
                                (((1,), (1,)), ((), ())), preferred_element_type=jnp.float32)
            if diagonal:
                row = lax.broadcasted_iota(jnp.int32, (tq - r0, tk), 0)
                col = lax.broadcasted_iota(jnp.int32, (tq - r0, tk), 1)
                s = jnp.where(col <= row, s, NEG)
            if first:
                m_next = jnp.broadcast_to(jnp.max(s, axis=1, keepdims=True), (tq - r0, LANE))
            else:
                m_prev = m_ref[j, rows, :]
                m_next = jnp.maximum(m_prev, jnp.max(s, axis=1, keepdims=True))
                al_ref[j, u, rows, :] = jnp.exp2(m_prev - m_next)
            p = jnp.exp2(s - jnp.concatenate([m_next] * (tk // LANE), axis=1))
            p_ref[j, u, rows, :] = p.astype(jnp.bfloat16)
            m_ref[j, rows, :] = m_next

        def pv(j, u, base, diagonal, first=False):
            r0 = u * tk if diagonal else 0
            rows = slice(r0, tq)
            start = pl.multiple_of(base + u * tk, tk)
            new = _dot(p_ref[j, u, rows, :], v_ref[0, j, pl.ds(start, tk), :])
            if first:
                acc_ref[j, rows, :] = new
            else:
                acc_ref[j, rows, :] = acc_ref[j, rows, :] * al_ref[j, u, rows, :] + new

        def diagonal_block():
            for u in range(chunks):
                softmax(0, u, qbase, True, first=(u == 0))
                if u > 0:
                    pv(1, u - 1, qbase, True, first=(u == 1))
                softmax(1, u, qbase, True, first=(u == 0))
                pv(0, u, qbase, True, first=(u == 0))
            pv(1, chunks - 1, qbase, True, first=(chunks == 1))

        def block(kb, diagonal):
            base = kb * tq
            for u in range(chunks):
                softmax(0, u, base, diagonal)
                if u == 0:
                    pv(1, chunks - 1, jnp.maximum(kb - 1, 0) * tq, False)
                else:
                    pv(1, u - 1, base, diagonal)
                softmax(1, u, base, diagonal)
                pv(0, u, base, diagonal)

        def blocks_needed(j):
            hd = 2 * hp + j
            gate_q = gate_end_ref[bi, hd, jnp.maximum(qi - 1, 0)]
            count = jnp.int32(0)
            for kb in range(nq - 1):
                keep = (kb < qi) & (gate_q - gate_end_ref[bi, hd, kb] >= slack_ref[hd])
                count = count + keep.astype(jnp.int32)
            return count

        diagonal_block()
        al_ref[1, chunks - 1] = jnp.ones((tq, LANE), jnp.float32)
        p_ref[1, chunks - 1] = jnp.zeros((tq, tk), jnp.bfloat16)
        n_off = jnp.maximum(blocks_needed(0), blocks_needed(1))
        first = qi - n_off
        odd = n_off & 1

        @pl.when(odd == 1)
        def _():
            block(first, False)

        def body(pair, c):
            kb = first + odd + 2 * pair
            block(kb, False)
            block(kb + 1, False)
            return c

        lax.fori_loop(0, lax.shift_right_logical(n_off, 1), body, 0)
        pv(1, chunks - 1, jnp.maximum(qi - 1, 0) * tq, False)

        o0 = acc_ref[0]
        o1 = acc_ref[1]
        o0 = o0 / pltpu.roll(o0, FOX_DIM, 1)
        o1 = o1 / pltpu.roll(o1, FOX_DIM, 1)
        lane = lax.broadcasted_iota(jnp.int32, (tq, LANE), 1)
        o = jnp.where(lane < MLA_V, o0, o1)
        o_ref[0, pl.ds(pl.multiple_of(qbase, tq), tq), :] = o.astype(jnp.bfloat16)
        return carry

    lax.fori_loop(0, nq, query_block, 0)


def _flash(gate_end, slack, q, k, v):
    b, _, lp, _ = q.shape
    tq, tk = FLASH_TQ, FLASH_TK
    kern = functools.partial(_flash_kernel, tq=tq, tk=tk, nq=lp // tq)
    qkv_spec = pl.BlockSpec((1, 2, lp, LANE), lambda bi, hp: (bi, hp, 0, 0))
    return pl.pallas_call(
        kern,
        grid=(b, HEADS // 2),
        in_specs=[pl.BlockSpec(memory_space=pltpu.SMEM), pl.BlockSpec(memory_space=pltpu.SMEM),
                  qkv_spec, qkv_spec, qkv_spec],
        out_specs=pl.BlockSpec((1, lp, LANE), lambda bi, hp: (bi, 0, hp)),
        out_shape=jax.ShapeDtypeStruct((b, lp, HEADS * MLA_V), jnp.bfloat16),
        scratch_shapes=[pltpu.VMEM((2, tq, LANE), jnp.float32)] * 2
        + [pltpu.VMEM((2, tq // tk, tq, LANE), jnp.float32),
           pltpu.VMEM((2, tq // tk, tq, tk), jnp.bfloat16)],
        compiler_params=pltpu.CompilerParams(
            dimension_semantics=("arbitrary", "arbitrary"), vmem_limit_bytes=VMEM_LIMIT),
        name="flash",
    )(gate_end, slack, q, k, v)


def _conv_in_kernel(h_ref, gmix_ref, win_ref, cw_ref, y_ref, gs_ref, *, tm):
    i = pl.program_id(1)

    @pl.when(i == 0)
    def _():
        gs_ref[0:8, :] = jnp.zeros((8, D_MODEL), jnp.float32)

    x = h_ref[0]
    hn = _rms(x, gmix_ref[...], D_MODEL).astype(jnp.bfloat16)
    gate_c = _dot(hn, win_ref[0, :, D_MODEL:2 * D_MODEL])
    u = _dot(hn, win_ref[0, :, 2 * D_MODEL:3 * D_MODEL])
    row = lax.broadcasted_iota(jnp.int32, (tm, D_MODEL), 0)
    g = jnp.where((i * tm + row) >= PAD, gate_c * u, 0.0)
    gs_ref[8:tm + 8, :] = g
    y = (cw_ref[0:1, :] * gs_ref[6:tm + 6, :] + cw_ref[1:2, :] * gs_ref[7:tm + 7, :]
         + cw_ref[2:3, :] * g)
    gs_ref[0:8, :] = gs_ref[tm:tm + 8, :]
    gate_b = _dot(hn, win_ref[0, :, 0:D_MODEL])
    y_ref[0] = (gate_b * y).astype(jnp.bfloat16)


def _conv_in(h, gmix, win, layer, cw, tm):
    b, lp, d = h.shape
    kern = functools.partial(_conv_in_kernel, tm=tm)
    return pl.pallas_call(
        kern,
        grid=(b, lp // tm),
        in_specs=[
            pl.BlockSpec((1, tm, d), lambda bi, i: (bi, i, 0)),
            _const_spec((1, d)),
            _layer_spec((d, 3 * d), layer),
            _const_spec((8, d)),
        ],
        out_specs=pl.BlockSpec((1, tm, d), lambda bi, i: (bi, i, 0)),
        out_shape=jax.ShapeDtypeStruct((b, lp, d), jnp.bfloat16),
        scratch_shapes=[pltpu.VMEM((tm + 8, d), jnp.float32)],
        compiler_params=pltpu.CompilerParams(
            dimension_semantics=("arbitrary", "arbitrary"), vmem_limit_bytes=VMEM_LIMIT),
        name="conv_in",
    )(h, gmix, win, cw)


def _mlp_tile(h, y, wo_ref, gmlp_ref, wup_ref, wdn_ref):
    h1 = h + _dot(y, wo_ref[0])
    n = _rms(h1, gmlp_ref[...], D_MODEL).astype(jnp.bfloat16)
    acc = h1
    for c in range(D_FF // FF_CHUNK):
        sl = slice(c * FF_CHUNK, (c + 1) * FF_CHUNK)
        a = jnp.maximum(_dot(n, wup_ref[0, :, sl]), 0.0)
        acc = acc + _dot((a * a).astype(jnp.bfloat16), wdn_ref[0, sl, :])
    return acc


def _mix_out_mlp_kernel(h_ref, y_ref, wo_ref, gmlp_ref, wup_ref, wdn_ref, out_ref):
    out_ref[...] = _mlp_tile(h_ref[...], y_ref[...], wo_ref, gmlp_ref, wup_ref, wdn_ref)


def _mix_out_mlp(h, y, wo, wo_layer, gmlp, wup, wdn, layer, tm):
    r, d = h.shape
    return pl.pallas_call(
        _mix_out_mlp_kernel,
        grid=(r // tm,),
        in_specs=[
            pl.BlockSpec((tm, d), lambda i: (i, 0)),
            pl.BlockSpec((tm, d), lambda i: (i, 0)),
            _layer_spec((d, d), wo_layer),
            _const_spec((1, d)),
            _layer_spec((d, D_FF), layer),
            _layer_spec((D_FF, d), layer),
        ],
        out_specs=pl.BlockSpec((tm, d), lambda i: (i, 0)),
        out_shape=jax.ShapeDtypeStruct((r, d), jnp.float32),
        compiler_params=pltpu.CompilerParams(
            dimension_semantics=("arbitrary",), vmem_limit_bytes=VMEM_LIMIT),
        name="mix_out_mlp",
    )(h, y, wo, gmlp, wup, wdn)


def _mix_out_mlp_first_kernel(*refs):
    x_parts, meta_ref = refs[0:REAL_PARTS], refs[REAL_PARTS]
    y_ref, wo_ref, gmlp_ref, wup_ref, wdn_ref, out_ref = refs[REAL_PARTS + 1:]
    h = _input_tile(pl.program_id(1), x_parts, meta_ref)
    out_ref[0] = _mlp_tile(h, y_ref[0], wo_ref, gmlp_ref, wup_ref, wdn_ref)


def _mix_out_mlp_first(x, meta, y, wo, wo_layer, gmlp, wup, wdn, layer, tm):
    b, lp, d = y.shape
    tile = pl.BlockSpec((1, tm, d), lambda bi, i: (bi, i, 0))
    return pl.pallas_call(
        _mix_out_mlp_first_kernel,
        grid=(b, lp // tm),
        in_specs=_input_specs(tm) + [
            tile,
            _layer_spec((d, d), wo_layer),
            _const_spec((1, d)),
            _layer_spec((d, D_FF), layer),
            _layer_spec((D_FF, d), layer),
        ],
        out_specs=tile,
        out_shape=jax.ShapeDtypeStruct((b, lp, d), jnp.float32),
        compiler_params=pltpu.CompilerParams(
            dimension_semantics=("arbitrary", "arbitrary"), vmem_limit_bytes=VMEM_LIMIT),
        name="mix_out_mlp_first",
    )(*([x] * REAL_PARTS), meta, y, wo, gmlp, wup, wdn)


def _mix_out_mlp_last_kernel(*refs):
    h_parts, y_parts = refs[0:REAL_PARTS], refs[REAL_PARTS:2 * REAL_PARTS]
    wo_ref, gmlp_ref, wup_ref, wdn_ref, out_ref = refs[2 * REAL_PARTS:]
    h = jnp.concatenate([r[0] for r in h_parts], axis=0)
    y = jnp.concatenate([r[0] for r in y_parts], axis=0)
    out_ref[0] = _mlp_tile(h, y, wo_ref, gmlp_ref, wup_ref, wdn_ref)


def _mix_out_mlp_last(h, y, wo, wo_layer, gmlp, wup, wdn, layer, tm, seq):
    b, lp, d = h.shape
    part = tm // REAL_PARTS
    last_part = lp // part - 1

    def part_spec(k):
        return pl.BlockSpec(
            (1, part, d),
            lambda bi, i: (bi, jnp.minimum(REAL_START // part + REAL_PARTS * i + k, last_part), 0))

    parts = [part_spec(k) for k in range(REAL_PARTS)]
    return pl.pallas_call(
        _mix_out_mlp_last_kernel,
        grid=(b, pl.cdiv(seq, tm)),
        in_specs=parts + parts + [
            _layer_spec((d, d), wo_layer),
            _const_spec((1, d)),
            _layer_spec((d, D_FF), layer),
            _layer_spec((D_FF, d), layer),
        ],
        out_specs=pl.BlockSpec((1, tm, d), lambda bi, i: (bi, i, 0)),
        out_shape=jax.ShapeDtypeStruct((b, seq, d), jnp.float32),
        compiler_params=pltpu.CompilerParams(
            dimension_semantics=("arbitrary", "arbitrary"), vmem_limit_bytes=VMEM_LIMIT),
        name="mix_out_mlp_last",
    )(*([h] * REAL_PARTS + [y] * REAL_PARTS), wo, gmlp, wup, wdn)


def _pad_heads(w, heads, dim):
    k = w.shape[0]
    w = w.reshape(k, heads, dim)
    w = jnp.pad(w, ((0, 0), (0, 0), (0, LANE - dim)))
    return w.reshape(k, heads * LANE)


def _lane_vec(v, offset=0):
    return jnp.zeros((LANE,), jnp.float32).at[offset:offset + v.shape[0]].set(v)


def _attn_params(w_in, g_cq, w_uq, g_ckv, w_ukv, g_q_mla, g_k_mla, g_q_fox, g_k_fox, b_forget):
    bf = jnp.bfloat16
    o1 = Q_LORA
    o2 = o1 + KV_LORA
    o3 = o2 + MLA_ROPE
    o4 = o3 + FOX_HEADS * FOX_DIM
    o5 = o4 + FOX_HEADS * FOX_DIM
    o6 = o5 + FOX_HEADS * FOX_DIM
    misc = jnp.zeros((D_MODEL, LANE), jnp.float32)
    misc = misc.at[:, MISC_GATE:MISC_GATE + FOX_HEADS].set(w_in[:, o6:])
    misc = misc.at[:, MISC_ROPE:MISC_ROPE + MLA_ROPE].set(w_in[:, o2:o3])
    wcat = jnp.concatenate([w_in[:, :o1], misc, w_in[:, o1:o2], w_in[:, o3:o6]], axis=1).astype(bf)
    kv = w_ukv.reshape(KV_LORA, MLA_HEADS, MLA_NOPE + MLA_V)
    wkn = _pad_heads(kv[:, :, :MLA_NOPE].reshape(KV_LORA, -1), MLA_HEADS, MLA_NOPE).astype(bf)
    wv = jnp.pad(kv[:, :, MLA_NOPE:].reshape(KV_LORA, MLA_HEADS // 2, 2, MLA_V),
                 ((0, 0), (0, 0), (0, 0), (0, LANE - MLA_V)))
    wv = jnp.concatenate([wv[:, :, 0], jnp.roll(wv[:, :, 1], MLA_V, axis=-1)], axis=-1)
    wv = wv.reshape(KV_LORA, MLA_HEADS * LANE).astype(bf)
    lo, mid, hi = MLA_NOPE, MLA_NOPE + HALF_ROPE, MLA_NOPE + MLA_ROPE
    uq = w_uq.reshape(Q_LORA, MLA_HEADS, MLA_QK)
    uq_sw = jnp.zeros((Q_LORA, MLA_HEADS, LANE), jnp.float32)
    uq_sw = uq_sw.at[:, :, lo:mid].set(uq[:, :, mid:hi]).at[:, :, mid:hi].set(uq[:, :, lo:mid])
    wuq = jnp.concatenate([_pad_heads(w_uq, MLA_HEADS, MLA_QK),
                           uq_sw.reshape(Q_LORA, MLA_HEADS * LANE)], axis=1).astype(bf)

    def swapped(g):
        return jnp.zeros((LANE,), jnp.float32).at[lo:mid].set(g[mid:hi]).at[mid:hi].set(g[lo:mid])

    zero = jnp.zeros((LANE,), jnp.float32)
    q_scale_mla = MLA_QK ** -0.5 * LOG2E
    rows = [zero] * VEC_ROWS
    rows[V_GQ_MLA] = _lane_vec(g_q_mla) * q_scale_mla
    rows[V_GQ_MLA_SW] = swapped(g_q_mla) * q_scale_mla
    rows[V_GK_MLA] = _lane_vec(g_k_mla)
    rows[V_GK_MLA_SW] = swapped(g_k_mla)
    for par in range(2):
        feat, extra = FEATURE_BASE[par], EXTRA_BASE[par]
        rows[V_GQ_FOX + par] = _lane_vec(g_q_fox, feat) * (FOX_DIM ** -0.5 * LOG2E)
        rows[V_GK_FOX + par] = _lane_vec(g_k_fox, feat)
        rows[V_ADD_Q_FOX + par] = (zero.at[extra + N_SPLIT:extra + 2 * N_SPLIT].set(1.0)
                                   .at[extra + FLAG_FOX_OFF].set(1.0))
        rows[V_ONES_K_FOX + par] = zero.at[extra:extra + N_SPLIT].set(1.0)
        rows[V_ONES_V + par] = zero.at[extra:extra + FOX_DIM].set(1.0)
    rows[V_B_FORGET] = _lane_vec(b_forget, MISC_GATE)
    rows[V_ADD_Q_MLA] = zero.at[FLAG_MLA].set(1.0)
    vec = jnp.stack(rows)
    return dict(wcat=wcat, gcq=g_cq[None], wuq=wuq, gckv=g_ckv[None], wkn=wkn, wv=wv, vec=vec)


def _gate_selectors():
    selq = np.zeros((LANE, FOX_HEADS * LANE), np.float32)
    selk = np.zeros((LANE, FOX_HEADS * LANE), np.float32)
    for part in range(N_SPLIT):
        for hd in range(FOX_HEADS):
            extra = hd * LANE + EXTRA_BASE[hd % 2]
            selq[part * FOX_HEADS + hd, extra + part] = 1.0
            selk[part * FOX_HEADS + hd, extra + N_SPLIT + part] = -1.0
    return jnp.asarray(selq, jnp.bfloat16), jnp.asarray(selk, jnp.bfloat16)


def _rope_table(lp):
    lane = jnp.arange(LANE, dtype=jnp.int32)
    rotary = (lane >= MLA_NOPE) & (lane < MLA_NOPE + MLA_ROPE)
    first_half = rotary & (lane < MLA_NOPE + HALF_ROPE)
    pair = ((lane - MLA_NOPE) % HALF_ROPE).astype(jnp.float32)
    inv_freq = ROPE_BASE ** (-(2.0 * pair) / MLA_ROPE)
    pos = (jnp.arange(lp, dtype=jnp.int32) - PAD).astype(jnp.float32)
    ang = pos[:, None] * inv_freq[None, :]
    cos_t = jnp.where(lane < MLA_NOPE, 1.0, jnp.where(rotary, jnp.cos(ang), 0.0))
    sin_sw = jnp.where(rotary, jnp.where(first_half, -jnp.sin(ang), jnp.sin(ang)), 0.0)
    return jnp.concatenate([cos_t, sin_sw], axis=1)


def _pruning_tables(gate_end, g_q, g_k):
    b, nt = gate_end.shape[:2]
    fox = jnp.transpose(gate_end[:, :, 0, MISC_GATE:MISC_GATE + FOX_HEADS] * LOG2E, (0, 2, 1))
    table = jnp.concatenate([jnp.zeros((b, MLA_HEADS, nt), jnp.float32), fox], axis=1)
    bound = 1.02 * FOX_DIM * (FOX_DIM ** -0.5 * LOG2E) * jnp.max(jnp.abs(g_q)) * jnp.max(jnp.abs(g_k))
    slack_fox = -(2.0 * bound + UNDERFLOW_LOG2 + 4.0)
    slack = jnp.concatenate([jnp.full((MLA_HEADS,), NEG, jnp.float32),
                             jnp.full((FOX_HEADS,), slack_fox, jnp.float32)])
    return table, slack


def _token_tile(lp):
    if lp % FLASH_TQ:
        raise ValueError(f"padded length {lp} is not a multiple of {FLASH_TQ}")
    return FLASH_TQ


def kernel(x, meta_tokens, g_mix, g_mlp, w_in_attn, g_cq, w_uq, g_ckv, w_ukv, g_q_mla, g_k_mla,
           g_q_fox, g_k_fox, b_forget, w_out_attn, w_in_conv, conv_w, w_out_conv, w_mlp_up,
           w_mlp_down):
    b, seq, d = x.shape
    assert d == D_MODEL and (PAD + N_META + seq) % BLOCK == 0
    lp = PAD + N_META + seq
    tm = _token_tile(lp)
    bf = jnp.bfloat16

    meta = meta_tokens.astype(x.dtype)
    h = None

    rope_tab = _rope_table(lp)
    tri = (jnp.arange(tm)[:, None] >= jnp.arange(tm)[None, :]).astype(bf)
    selq, selk = _gate_selectors()

    wo_attn, wo_conv, w_conv = w_out_attn.astype(bf), w_out_conv.astype(bf), w_in_conv.astype(bf)
    w_up, w_down = w_mlp_up.astype(bf), w_mlp_down.astype(bf)
    for layer in range(DEPTH):
        j = layer // 2
        gmix = g_mix[layer][None]
        if layer % 2 == 0:
            p = _attn_params(w_in_attn[j], g_cq[j], w_uq[j], g_ckv[j], w_ukv[j], g_q_mla[j],
                             g_k_mla[j], g_q_fox[j], g_k_fox[j], b_forget[j])
            stream = (x, meta) if layer == 0 else (h,)
            q, k, v, gate_end = _attn_in(stream, lp, gmix, p, rope_tab, tri, selq, selk, tm)
            y = _flash(*_pruning_tables(gate_end, g_q_fox[j], g_k_fox[j]), q, k, v)
            wo = wo_attn
        else:
            cw = jnp.zeros((8, d), jnp.float32).at[0:3].set(conv_w[j])
            y = _conv_in(h, gmix, w_conv, j, cw, tm)
            wo = wo_conv
        gmlp = g_mlp[layer][None]
        if layer == 0:
            h = _mix_out_mlp_first(x, meta, y, wo, j, gmlp, w_up, w_down, layer, tm)
        elif layer < DEPTH - 1:
            h = _mix_out_mlp(h.reshape(b * lp, d), y.reshape(b * lp, d), wo, j, gmlp,
                             w_up, w_down, layer, tm).reshape(b, lp, d)
        else:
            return _mix_out_mlp_last(h, y, wo, j, gmlp, w_up, w_down, layer, tm, seq)
```

```python
import functools

import numpy as np
import jax
import jax.numpy as jnp
from jax import lax
from jax.experimental import pallas as pl
from jax.experimental.pallas import tpu as pltpu

D_MODEL = 1024
DEPTH = 4
N_META = 16
BLOCK = 128
PAD = 2 * BLOCK - N_META
REAL_START = PAD + N_META
REAL_PARTS = 3
MLA_HEADS = 8
MLA_NOPE = 64
MLA_ROPE = 32
MLA_QK = MLA_NOPE + MLA_ROPE
MLA_V = 64
Q_LORA = 384
KV_LORA = 256
ROPE_BASE = 10000.0
FOX_HEADS = 8
FOX_DIM = 64
D_FF = 4 * D_MODEL
EPS = 1e-6
NEG = -1e30

LANE = 128
HEADS = MLA_HEADS + FOX_HEADS
HALF_ROPE = MLA_ROPE // 2
FEATURE_BASE = (0, FOX_DIM)
EXTRA_BASE = (FOX_DIM, 0)
N_SPLIT = 3
FLAG_FOX_OFF = 2 * N_SPLIT
FLAG_MLA = MLA_QK
PAD_KEY = NEG
LOG2E = 1.4426950408889634
MISC_GATE = 0
MISC_ROPE = MLA_NOPE

OFF_CQ = 0
OFF_MISC = OFF_CQ + Q_LORA
OFF_CKV = OFF_MISC + LANE
OFF_FQ = OFF_CKV + KV_LORA
OFF_FK = OFF_FQ + FOX_HEADS * FOX_DIM
OFF_FV = OFF_FK + FOX_HEADS * FOX_DIM
W_CAT = OFF_FV + FOX_HEADS * FOX_DIM

(V_GQ_MLA, V_GQ_MLA_SW, V_GK_MLA, V_GK_MLA_SW, V_ADD_Q_MLA, V_B_FORGET) = range(6)
V_GQ_FOX, V_GK_FOX, V_ADD_Q_FOX, V_ONES_K_FOX, V_ONES_V = 6, 8, 10, 12, 14
VEC_ROWS = 16
PAIR = 2 * LANE

FF_CHUNK = 1024
UNDERFLOW_LOG2 = 150.0
FLASH_TQ = 768
FLASH_TK = 256
VMEM_LIMIT = 56 * 1024 * 1024


def _const_spec(shape):
    nd = len(shape)
    return pl.BlockSpec(shape, lambda *_: (0,) * nd, pipeline_mode=pl.Buffered(1))


def _layer_spec(shape, layer):
    nd = len(shape)
    return pl.BlockSpec((1,) + shape, lambda *_: (layer,) + (0,) * nd,
                        pipeline_mode=pl.Buffered(1))


def _input_specs(tm):
    part = tm // REAL_PARTS
    assert REAL_START == part

    def part_spec(k):
        return pl.BlockSpec((1, part, D_MODEL),
                            lambda bi, i: (bi, jnp.maximum(REAL_PARTS * i + k - 1, 0), 0))

    return [part_spec(k) for k in range(REAL_PARTS)] + [_const_spec((N_META, D_MODEL))]


def _input_tile(i, x_parts, meta_ref):
    lead = jnp.concatenate([jnp.zeros((PAD, D_MODEL), jnp.float32), meta_ref[...]], axis=0)
    first = jnp.where(i == 0, lead, x_parts[0][0])
    return jnp.concatenate([first] + [r[0] for r in x_parts[1:]], axis=0)


def _rms(x, g, n):
    ms = jnp.sum(x * x, axis=-1, keepdims=True) * (1.0 / n)
    return x * lax.rsqrt(ms + EPS) * g


def _split3(x):
    hi = x.astype(jnp.bfloat16).astype(jnp.float32)
    r1 = x - hi
    mid = r1.astype(jnp.bfloat16).astype(jnp.float32)
    lo = r1 - mid
    packed = hi + pltpu.roll(mid, FOX_HEADS, 1) + pltpu.roll(lo, 2 * FOX_HEADS, 1)
    return packed.astype(jnp.bfloat16)


def _dot(a, b):
    return jnp.dot(a, b, preferred_element_type=jnp.float32)


def _attn_in_kernel(*refs, tm, from_x):
    n_stream = REAL_PARTS + 1 if from_x else 1
    stream = refs[:n_stream]
    (gmix_ref, wcat_ref, gcq_ref, wuq_ref, gckv_ref, wkn_ref, wv_ref, vec_ref, rope_ref, tri_ref,
     selq_ref, selk_ref, q_ref, k_ref, v_ref, gate_end_ref, carry_ref) = refs[n_stream:]
    i = pl.program_id(1)

    @pl.when(i == 0)
    def _():
        carry_ref[...] = jnp.zeros_like(carry_ref)

    x = _input_tile(i, stream[:-1], stream[-1]) if from_x else stream[0][0]
    hn = _rms(x, gmix_ref[...], D_MODEL).astype(jnp.bfloat16)

    def seg(lo, width):
        return _dot(hn, wcat_ref[:, lo:lo + width])

    def vec(r):
        return vec_ref[r:r + 1, :]

    cos_t = rope_ref[:, 0:LANE]
    sin_sw = rope_ref[:, LANE:2 * LANE]
    gc_q, gs_q = vec(V_GQ_MLA) * cos_t, vec(V_GQ_MLA_SW) * sin_sw
    gc_k, gs_k = vec(V_GK_MLA) * cos_t, vec(V_GK_MLA_SW) * sin_sw
    add_q_mla = vec(V_ADD_Q_MLA)

    lane = lax.broadcasted_iota(jnp.int32, (tm, LANE), 1)
    row = lax.broadcasted_iota(jnp.int32, (tm, LANE), 0)
    valid = (i * tm + row) >= PAD
    pad_key = jnp.where(valid, 0.0, PAD_KEY)
    add_k_mla = jnp.where(lane == FLAG_MLA, pad_key, 0.0)
    halves = (lane < FOX_DIM, lane >= FOX_DIM)
    add_k_fox = [vec(V_ONES_K_FOX + par)
                 + jnp.where(lane == EXTRA_BASE[par] + FLAG_FOX_OFF, pad_key, 0.0)
                 for par in range(2)]

    cq_misc = seg(OFF_CQ, Q_LORA + LANE)
    misc = cq_misc[:, Q_LORA:]
    cqn = _rms(cq_misc[:, :Q_LORA], gcq_ref[...], Q_LORA).astype(jnp.bfloat16)
    ckvn = _rms(seg(OFF_CKV, KV_LORA), gckv_ref[...], KV_LORA).astype(jnp.bfloat16)
    kpe = jnp.where((lane >= MISC_ROPE) & (lane < MISC_ROPE + MLA_ROPE), misc, 0.0)
    k_rot = jnp.where(lane < MISC_ROPE + HALF_ROPE, pltpu.roll(kpe, LANE - HALF_ROPE, 1),
                      pltpu.roll(kpe, HALF_ROPE, 1)) * gs_k
    xl = misc + vec(V_B_FORGET)
    logf = jnp.minimum(xl, 0.0) - jnp.log1p(jnp.exp(-jnp.abs(xl)))
    logf = jnp.where(valid & (lane >= MISC_GATE) & (lane < MISC_GATE + FOX_HEADS), logf, 0.0)
    cs = _dot(tri_ref[...], _split3(logf))
    cs = (cs + pltpu.roll(cs, LANE - FOX_HEADS, 1)) + pltpu.roll(cs, LANE - 2 * FOX_HEADS, 1)
    cum = jnp.where(lane < FOX_HEADS, cs, 0.0) + carry_ref[0:1, :]
    carry_ref[0:1, :] = cum[tm - 1:tm, :]
    gate_end_ref[0, 0] = carry_ref[...]
    cum3 = _split3(cum * LOG2E)
    gate_q = _dot(cum3, selq_ref[...])
    gate_k = _dot(cum3, selk_ref[...])

    def inv_rms(xv, n):
        return lax.rsqrt(jnp.sum(xv * xv, axis=-1, keepdims=True) * (1.0 / n) + EPS)

    def fox_group(g):
        xq4 = seg(OFF_FQ + g * PAIR, PAIR)
        xk4 = seg(OFF_FK + g * PAIR, PAIR)
        xv4 = seg(OFF_FV + g * PAIR, PAIR)
        for e in range(4):
            hd, par = 4 * g + e, e % 2
            sl = slice((e // 2) * LANE, (e // 2 + 1) * LANE)
            gl = slice(hd * LANE, (hd + 1) * LANE)
            xq, xk = xq4[:, sl], xk4[:, sl]
            rq = inv_rms(jnp.where(halves[par], xq, 0.0), FOX_DIM)
            rk = inv_rms(jnp.where(halves[par], xk, 0.0), FOX_DIM)
            q_ref[0, MLA_HEADS + hd] = (xq * vec(V_GQ_FOX + par) * rq + gate_q[:, gl]
                                        + vec(V_ADD_Q_FOX + par)).astype(jnp.bfloat16)
            k_ref[0, MLA_HEADS + hd] = (xk * vec(V_GK_FOX + par) * rk + gate_k[:, gl]
                                        + add_k_fox[par]).astype(jnp.bfloat16)
            v_ref[0, MLA_HEADS + hd] = (jnp.where(halves[par], xv4[:, sl], 0.0)
                                        + vec(V_ONES_V + par)).astype(jnp.bfloat16)

    def mla_pair(g):
        cols = slice(g * PAIR, (g + 1) * PAIR)
        cols_sw = slice(MLA_HEADS * LANE + g * PAIR, MLA_HEADS * LANE + (g + 1) * PAIR)
        xq2 = _dot(cqn, wuq_ref[:, cols])
        xq2_sw = _dot(cqn, wuq_ref[:, cols_sw])
        xk2 = _dot(ckvn, wkn_ref[:, cols])
        xv2 = _dot(ckvn, wv_ref[:, cols])
        for e in range(2):
            hd, sl = 2 * g + e, slice(e * LANE, (e + 1) * LANE)
            xq = xq2[:, sl]
            q_ref[0, hd] = ((xq * gc_q + xq2_sw[:, sl] * gs_q) * inv_rms(xq, MLA_QK) + add_q_mla
                            ).astype(jnp.bfloat16)
            xk = xk2[:, sl] + kpe
            k_ref[0, hd] = ((xk * gc_k + k_rot) * inv_rms(xk, MLA_QK) + add_k_mla
                            ).astype(jnp.bfloat16)
            v_ref[0, hd] = (xv2[:, sl] + vec(V_ONES_V + e)).astype(jnp.bfloat16)

    mla_pair(0)
    mla_pair(1)
    fox_group(0)
    fox_group(1)
    mla_pair(2)
    mla_pair(3)


def _attn_in(stream, lp, gmix, p, rope_tab, tri, selq, selk, tm):
    from_x = len(stream) == 2
    b, d = stream[0].shape[0], D_MODEL
    nt = lp // tm
    kern = functools.partial(_attn_in_kernel, tm=tm, from_x=from_x)
    stream_specs = (_input_specs(tm) if from_x
                    else [pl.BlockSpec((1, tm, d), lambda bi, i: (bi, i, 0))])
    stream_args = [stream[0]] * REAL_PARTS + [stream[1]] if from_x else [stream[0]]
    qk_shape = jax.ShapeDtypeStruct((b, HEADS, lp, LANE), jnp.bfloat16)
    qk_spec = pl.BlockSpec((1, HEADS, tm, LANE), lambda bi, i: (bi, 0, i, 0))
    return pl.pallas_call(
        kern,
        grid=(b, nt),
        in_specs=stream_specs + [
            _const_spec((1, d)),
            _const_spec((d, W_CAT)),
            _const_spec((1, Q_LORA)),
            _const_spec((Q_LORA, 2 * MLA_HEADS * LANE)),
            _const_spec((1, KV_LORA)),
            _const_spec((KV_LORA, MLA_HEADS * LANE)),
            _const_spec((KV_LORA, MLA_HEADS * LANE)),
            _const_spec((VEC_ROWS, LANE)),
            pl.BlockSpec((tm, 2 * LANE), lambda bi, i: (i, 0)),
            _const_spec((tm, tm)),
            _const_spec((LANE, FOX_HEADS * LANE)),
            _const_spec((LANE, FOX_HEADS * LANE)),
        ],
        out_specs=[qk_spec, qk_spec, qk_spec,
                   pl.BlockSpec((1, 1, 8, LANE), lambda bi, i: (bi, i, 0, 0))],
        out_shape=[qk_shape, qk_shape, qk_shape,
                   jax.ShapeDtypeStruct((b, nt, 8, LANE), jnp.float32)],
        scratch_shapes=[pltpu.VMEM((8, LANE), jnp.float32)],
        compiler_params=pltpu.CompilerParams(
            dimension_semantics=("arbitrary", "arbitrary"), vmem_limit_bytes=VMEM_LIMIT),
        name="attn_in",
    )(*stream_args, gmix, p["wcat"], p["gcq"], p["wuq"], p["gckv"], p["wkn"], p["wv"], p["vec"],
      rope_tab, tri, selq, selk)


def _flash_kernel(gate_end_ref, slack_ref, q_ref, k_ref, v_ref, o_ref, m_ref, acc_ref, al_ref,
                  p_ref, *, tq, tk, nq):
    chunks = tq // tk
    bi, hp = pl.program_id(0), pl.program_id(1)

    def query_block(qi, carry):
        qbase = qi * tq

        def softmax(j, u, base, diagonal, first=False):
            r0 = u * tk if diagonal else 0
            rows = slice(r0, tq)
            start = pl.multiple_of(base + u * tk, tk)
            q_rows = pl.ds(pl.multiple_of(qbase + r0, tk), tq - r0)
            s = lax.dot_general(q_ref[0, j, q_rows, :], k_ref[0, j, pl.ds(start, tk), :],
                                (((1,), (1,)), ((), ())), preferred_element_type=jnp.float32)
            if diagonal:
                row = lax.broadcasted_iota(jnp.int32, (tq - r0, tk), 0)
                col = lax.broadcasted_iota(jnp.int32, (tq - r0, tk), 1)
                s = jnp.where(col <= row, s, NEG)
            if first:
                m_next = jnp.broadcast_to(jnp.max(s, axis=1, keepdims=True), (tq - r0, LANE))
            else:
                m_prev = m_ref[j, rows, :]
                m_next = jnp.maximum(m_prev, jnp.max(s, axis=1, keepdims=True))
                al_ref[j, u, rows, :] = jnp.exp2(m_prev - m_next)
            p = jnp.exp2(s - jnp.concatenate([m_next] * (tk // LANE), axis=1))
            p_ref[j, u, rows, :] = p.astype(jnp.bfloat16)
            m_ref[j, rows, :] = m_next

        def pv(j, u, base, diagonal, first=False):
            r0 = u * tk if diagonal else 0
            rows = slice(r0, tq)
            start = pl.multiple_of(base + u * tk, tk)
            new = _dot(p_ref[j, u, rows, :], v_ref[0, j, pl.ds(start, tk), :])
            if first:
                acc_ref[j, rows, :] = new
            else:
                acc_ref[j, rows, :] = acc_ref[j, rows, :] * al_ref[j, u, rows, :] + new

        def diagonal_block():
            for u in range(chunks):
                softmax(0, u, qbase, True, first=(u == 0))
                if u > 0:
                    pv(1, u - 1, qbase, True, first=(u == 1))
                softmax(1, u, qbase, True, first=(u == 0))
                pv(0, u, qbase, True, first=(u == 0))
            pv(1, chunks - 1, qbase, True, first=(chunks == 1))

        def block(kb, diagonal):
            base = kb * tq
            for u in range(chunks):
                softmax(0, u, base, diagonal)
                if u == 0:
                    pv(1, chunks - 1, jnp.maximum(kb - 1, 0) * tq, False)
                else:
                    pv(1, u - 1, base, diagonal)
                softmax(1, u, base, diagonal)
                pv(0, u, base, diagonal)

        def blocks_needed(j):
            hd = 2 * hp + j
            gate_q = gate_end_ref[bi, hd, jnp.maximum(qi - 1, 0)]
            count = jnp.int32(0)
            for kb in range(nq - 1):
                keep = (kb < qi) & (gate_q - gate_end_ref[bi, hd, kb] >= slack_ref[hd])
                count = count + keep.astype(jnp.int32)
            return count

        diagonal_block()
        al_ref[1, chunks - 1] = jnp.ones((tq, LANE), jnp.float32)
        p_ref[1, chunks - 1] = jnp.zeros((tq, tk), jnp.bfloat16)
        n_off = jnp.maximum(blocks_needed(0), blocks_needed(1))
        first = qi - n_off
        odd = n_off & 1

        @pl.when(odd == 1)
        def _():
            block(first, False)

        def body(pair, c):
            kb = first + odd + 2 * pair
            block(kb, False)
            block(kb + 1, False)
            return c

        lax.fori_loop(0, lax.shift_right_logical(n_off, 1), body, 0)
        pv(1, chunks - 1, jnp.maximum(qi - 1, 0) * tq, False)

        o0 = acc_ref[0]
        o1 = acc_ref[1]
        o0 = o0 / pltpu.roll(o0, FOX_DIM, 1)
        o1 = o1 / pltpu.roll(o1, FOX_DIM, 1)
        lane = lax.broadcasted_iota(jnp.int32, (tq, LANE), 1)
        o = jnp.where(lane < MLA_V, o0, o1)
        o_ref[0, pl.ds(pl.multiple_of(qbase, tq), tq), :] = o.astype(jnp.bfloat16)
        return carry

    lax.fori_loop(0, nq, query_block, 0)


def _flash(gate_end, slack, q, k, v):
    b, _, lp, _ = q.shape
    tq, tk = FLASH_TQ, FLASH_TK
    kern = functools.partial(_flash_kernel, tq=tq, tk=tk, nq=lp // tq)
    qkv_spec = pl.BlockSpec((1, 2, lp, LANE), lambda bi, hp: (bi, hp, 0, 0))
    return pl.pallas_call(
        kern,
        grid=(b, HEADS // 2),
        in_specs=[pl.BlockSpec(memory_space=pltpu.SMEM), pl.BlockSpec(memory_space=pltpu.SMEM),
                  qkv_spec, qkv_spec, qkv_spec],
        out_specs=pl.BlockSpec((1, lp, LANE), lambda bi, hp: (bi, 0, hp)),
        out_shape=jax.ShapeDtypeStruct((b, lp, HEADS * MLA_V), jnp.bfloat16),
        scratch_shapes=[pltpu.VMEM((2, tq, LANE), jnp.float32)] * 2
        + [pltpu.VMEM((2, tq // tk, tq, LANE), jnp.float32),
           pltpu.VMEM((2, tq // tk, tq, tk), jnp.bfloat16)],
        compiler_params=pltpu.CompilerParams(
            dimension_semantics=("arbitrary", "arbitrary"), vmem_limit_bytes=VMEM_LIMIT),
        name="flash",
    )(gate_end, slack, q, k, v)


def _conv_in_kernel(h_ref, gmix_ref, win_ref, cw_ref, y_ref, gs_ref, *, tm):
    i = pl.program_id(1)

    @pl.when(i == 0)
    def _():
        gs_ref[0:8, :] = jnp.zeros((8, D_MODEL), jnp.float32)

    x = h_ref[0]
    hn = _rms(x, gmix_ref[...], D_MODEL).astype(jnp.bfloat16)
    gate_c = _dot(hn, win_ref[0, :, D_MODEL:2 * D_MODEL])
    u = _dot(hn, win_ref[0, :, 2 * D_MODEL:3 * D_MODEL])
    row = lax.broadcasted_iota(jnp.int32, (tm, D_MODEL), 0)
    g = jnp.where((i * tm + row) >= PAD, gate_c * u, 0.0)
    gs_ref[8:tm + 8, :] = g
    y = (cw_ref[0:1, :] * gs_ref[6:tm + 6, :] + cw_ref[1:2, :] * gs_ref[7:tm + 7, :]
         + cw_ref[2:3, :] * g)
    gs_ref[0:8, :] = gs_ref[tm:tm + 8, :]
    gate_b = _dot(hn, win_ref[0, :, 0:D_MODEL])
    y_ref[0] = (gate_b * y).astype(jnp.bfloat16)


def _conv_in(h, gmix, win, layer, cw, tm):
    b, lp, d = h.shape
    kern = functools.partial(_conv_in_kernel, tm=tm)
    return pl.pallas_call(
        kern,
        grid=(b, lp // tm),
        in_specs=[
            pl.BlockSpec((1, tm, d), lambda bi, i: (bi, i, 0)),
            _const_spec((1, d)),
            _layer_spec((d, 3 * d), layer),
            _const_spec((8, d)),
        ],
        out_specs=pl.BlockSpec((1, tm, d), lambda bi, i: (bi, i, 0)),
        out_shape=jax.ShapeDtypeStruct((b, lp, d), jnp.bfloat16),
        scratch_shapes=[pltpu.VMEM((tm + 8, d), jnp.float32)],
        compiler_params=pltpu.CompilerParams(
            dimension_semantics=("arbitrary", "arbitrary"), vmem_limit_bytes=VMEM_LIMIT),
        name="conv_in",
    )(h, gmix, win, cw)


def _mlp_tile(h, y, wo_ref, gmlp_ref, wup_ref, wdn_ref):
    h1 = h + _dot(y, wo_ref[0])
    n = _rms(h1, gmlp_ref[...], D_MODEL).astype(jnp.bfloat16)
    acc = h1
    for c in range(D_FF // FF_CHUNK):
        sl = slice(c * FF_CHUNK, (c + 1) * FF_CHUNK)
        a = jnp.maximum(_dot(n, wup_ref[0, :, sl]), 0.0)
        acc = acc + _dot((a * a).astype(jnp.bfloat16), wdn_ref[0, sl, :])
    return acc


def _mix_out_mlp_kernel(h_ref, y_ref, wo_ref, gmlp_ref, wup_ref, wdn_ref, out_ref):
    out_ref[...] = _mlp_tile(h_ref[...], y_ref[...], wo_ref, gmlp_ref, wup_ref, wdn_ref)


def _mix_out_mlp(h, y, wo, wo_layer, gmlp, wup, wdn, layer, tm):
    r, d = h.shape
    return pl.pallas_call(
        _mix_out_mlp_kernel,
        grid=(r // tm,),
        in_specs=[
            pl.BlockSpec((tm, d), lambda i: (i, 0)),
            pl.BlockSpec((tm, d), lambda i: (i, 0)),
            _layer_spec((d, d), wo_layer),
            _const_spec((1, d)),
            _layer_spec((d, D_FF), layer),
            _layer_spec((D_FF, d), layer),
        ],
        out_specs=pl.BlockSpec((tm, d), lambda i: (i, 0)),
        out_shape=jax.ShapeDtypeStruct((r, d), jnp.float32),
        compiler_params=pltpu.CompilerParams(
            dimension_semantics=("arbitrary",), vmem_limit_bytes=VMEM_LIMIT),
        name="mix_out_mlp",
    )(h, y, wo, gmlp, wup, wdn)


def _mix_out_mlp_first_kernel(*refs):
    x_parts, meta_ref = refs[0:REAL_PARTS], refs[REAL_PARTS]
    y_ref, wo_ref, gmlp_ref, wup_ref, wdn_ref, out_ref = refs[REAL_PARTS + 1:]
    h = _input_tile(pl.program_id(1), x_parts, meta_ref)
    out_ref[0] = _mlp_tile(h, y_ref[0], wo_ref, gmlp_ref, wup_ref, wdn_ref)


def _mix_out_mlp_first(x, meta, y, wo, wo_layer, gmlp, wup, wdn, layer, tm):
    b, lp, d = y.shape
    tile = pl.BlockSpec((1, tm, d), lambda bi, i: (bi, i, 0))
    return pl.pallas_call(
        _mix_out_mlp_first_kernel,
        grid=(b, lp // tm),
        in_specs=_input_specs(tm) + [
            tile,
            _layer_spec((d, d), wo_layer),
            _const_spec((1, d)),
            _layer_spec((d, D_FF), layer),
            _layer_spec((D_FF, d), layer),
        ],
        out_specs=tile,
        out_shape=jax.ShapeDtypeStruct((b, lp, d), jnp.float32),
        compiler_params=pltpu.CompilerParams(
            dimension_semantics=("arbitrary", "arbitrary"), vmem_limit_bytes=VMEM_LIMIT),
        name="mix_out_mlp_first",
    )(*([x] * REAL_PARTS), meta, y, wo, gmlp, wup, wdn)


def _mix_out_mlp_last_kernel(*refs):
    h_parts, y_parts = refs[0:REAL_PARTS], refs[REAL_PARTS:2 * REAL_PARTS]
    wo_ref, gmlp_ref, wup_ref, wdn_ref, out_ref = refs[2 * REAL_PARTS:]
    h = jnp.concatenate([r[0] for r in h_parts], axis=0)
    y = jnp.concatenate([r[0] for r in y_parts], axis=0)
    out_ref[0] = _mlp_tile(h, y, wo_ref, gmlp_ref, wup_ref, wdn_ref)


def _mix_out_mlp_last(h, y, wo, wo_layer, gmlp, wup, wdn, layer, tm, seq):
    b, lp, d = h.shape
    part = tm // REAL_PARTS
    last_part = lp // part - 1

    def part_spec(k):
        return pl.BlockSpec(
            (1, part, d),
            lambda bi, i: (bi, jnp.minimum(REAL_START // part + REAL_PARTS * i + k, last_part), 0))

    parts = [part_spec(k) for k in range(REAL_PARTS)]
    return pl.pallas_call(
        _mix_out_mlp_last_kernel,
        grid=(b, pl.cdiv(seq, tm)),
        in_specs=parts + parts + [
            _layer_spec((d, d), wo_layer),
            _const_spec((1, d)),
            _layer_spec((d, D_FF), layer),
            _layer_spec((D_FF, d), layer),
        ],
        out_specs=pl.BlockSpec((1, tm, d), lambda bi, i: (bi, i, 0)),
        out_shape=jax.ShapeDtypeStruct((b, seq, d), jnp.float32),
        compiler_params=pltpu.CompilerParams(
            dimension_semantics=("arbitrary", "arbitrary"), vmem_limit_bytes=VMEM_LIMIT),
        name="mix_out_mlp_last",
    )(*([h] * REAL_PARTS + [y] * REAL_PARTS), wo, gmlp, wup, wdn)


def _pad_heads(w, heads, dim):
    k = w.shape[0]
    w = w.reshape(k, heads, dim)
    w = jnp.pad(w, ((0, 0), (0, 0), (0, LANE - dim)))
    return w.reshape(k, heads * LANE)


def _lane_vec(v, offset=0):
    return jnp.zeros((LANE,), jnp.float32).at[offset:offset + v.shape[0]].set(v)


def _attn_params(w_in, g_cq, w_uq, g_ckv, w_ukv, g_q_mla, g_k_mla, g_q_fox, g_k_fox, b_forget):
    bf = jnp.bfloat16
    o1 = Q_LORA
    o2 = o1 + KV_LORA
    o3 = o2 + MLA_ROPE
    o4 = o3 + FOX_HEADS * FOX_DIM
    o5 = o4 + FOX_HEADS * FOX_DIM
    o6 = o5 + FOX_HEADS * FOX_DIM
    misc = jnp.zeros((D_MODEL, LANE), jnp.float32)
    misc = misc.at[:, MISC_GATE:MISC_GATE + FOX_HEADS].set(w_in[:, o6:])
    misc = misc.at[:, MISC_ROPE:MISC_ROPE + MLA_ROPE].set(w_in[:, o2:o3])
    wcat = jnp.concatenate([w_in[:, :o1], misc, w_in[:, o1:o2], w_in[:, o3:o6]], axis=1).astype(bf)
    kv = w_ukv.reshape(KV_LORA, MLA_HEADS, MLA_NOPE + MLA_V)
    wkn = _pad_heads(kv[:, :, :MLA_NOPE].reshape(KV_LORA, -1), MLA_HEADS, MLA_NOPE).astype(bf)
    wv = jnp.pad(kv[:, :, MLA_NOPE:].reshape(KV_LORA, MLA_HEADS // 2, 2, MLA_V),
                 ((0, 0), (0, 0), (0, 0), (0, LANE - MLA_V)))
    wv = jnp.concatenate([wv[:, :, 0], jnp.roll(wv[:, :, 1], MLA_V, axis=-1)], axis=-1)
    wv = wv.reshape(KV_LORA, MLA_HEADS * LANE).astype(bf)
    lo, mid, hi = MLA_NOPE, MLA_NOPE + HALF_ROPE, MLA_NOPE + MLA_ROPE
    uq = w_uq.reshape(Q_LORA, MLA_HEADS, MLA_QK)
    uq_sw = jnp.zeros((Q_LORA, MLA_HEADS, LANE), jnp.float32)
    uq_sw = uq_sw.at[:, :, lo:mid].set(uq[:, :, mid:hi]).at[:, :, mid:hi].set(uq[:, :, lo:mid])
    wuq = jnp.concatenate([_pad_heads(w_uq, MLA_HEADS, MLA_QK),
                           uq_sw.reshape(Q_LORA, MLA_HEADS * LANE)], axis=1).astype(bf)

    def swapped(g):
        return jnp.zeros((LANE,), jnp.float32).at[lo:mid].set(g[mid:hi]).at[mid:hi].set(g[lo:mid])

    zero = jnp.zeros((LANE,), jnp.float32)
    q_scale_mla = MLA_QK ** -0.5 * LOG2E
    rows = [zero] * VEC_ROWS
    rows[V_GQ_MLA] = _lane_vec(g_q_mla) * q_scale_mla
    rows[V_GQ_MLA_SW] = swapped(g_q_mla) * q_scale_mla
    rows[V_GK_MLA] = _lane_vec(g_k_mla)
    rows[V_GK_MLA_SW] = swapped(g_k_mla)
    for par in range(2):
        feat, extra = FEATURE_BASE[par], EXTRA_BASE[par]
        rows[V_GQ_FOX + par] = _lane_vec(g_q_fox, feat) * (FOX_DIM ** -0.5 * LOG2E)
        rows[V_GK_FOX + par] = _lane_vec(g_k_fox, feat)
        rows[V_ADD_Q_FOX + par] = (zero.at[extra + N_SPLIT:extra + 2 * N_SPLIT].set(1.0)
                                   .at[extra + FLAG_FOX_OFF].set(1.0))
        rows[V_ONES_K_FOX + par] = zero.at[extra:extra + N_SPLIT].set(1.0)
        rows[V_ONES_V + par] = zero.at[extra:extra + FOX_DIM].set(1.0)
    rows[V_B_FORGET] = _lane_vec(b_forget, MISC_GATE)
    rows[V_ADD_Q_MLA] = zero.at[FLAG_MLA].set(1.0)
    vec = jnp.stack(rows)
    return dict(wcat=wcat, gcq=g_cq[None], wuq=wuq, gckv=g_ckv[None], wkn=wkn, wv=wv, vec=vec)


def _gate_selectors():
    selq = np.zeros((LANE, FOX_HEADS * LANE), np.float32)
    selk = np.zeros((LANE, FOX_HEADS * LANE), np.float32)
    for part in range(N_SPLIT):
        for hd in range(FOX_HEADS):
            extra = hd * LANE + EXTRA_BASE[hd % 2]
            selq[part * FOX_HEADS + hd, extra + part] = 1.0
            selk[part * FOX_HEADS + hd, extra + N_SPLIT + part] = -1.0
    return jnp.asarray(selq, jnp.bfloat16), jnp.asarray(selk, jnp.bfloat16)


def _rope_table(lp):
    lane = jnp.arange(LANE, dtype=jnp.int32)
    rotary = (lane >= MLA_NOPE) & (lane < MLA_NOPE + MLA_ROPE)
    first_half = rotary & (lane < MLA_NOPE + HALF_ROPE)
    pair = ((lane - MLA_NOPE) % HALF_ROPE).astype(jnp.float32)
    inv_freq = ROPE_BASE ** (-(2.0 * pair) / MLA_ROPE)
    pos = (jnp.arange(lp, dtype=jnp.int32) - PAD).astype(jnp.float32)
    ang = pos[:, None] * inv_freq[None, :]
    cos_t = jnp.where(lane < MLA_NOPE, 1.0, jnp.where(rotary, jnp.cos(ang), 0.0))
    sin_sw = jnp.where(rotary, jnp.where(first_half, -jnp.sin(ang), jnp.sin(ang)), 0.0)
    return jnp.concatenate([cos_t, sin_sw], axis=1)


def _pruning_tables(gate_end, g_q, g_k):
    b, nt = gate_end.shape[:2]
    fox = jnp.transpose(gate_end[:, :, 0, MISC_GATE:MISC_GATE + FOX_HEADS] * LOG2E, (0, 2, 1))
    table = jnp.concatenate([jnp.zeros((b, MLA_HEADS, nt), jnp.float32), fox], axis=1)
    bound = 1.02 * FOX_DIM * (FOX_DIM ** -0.5 * LOG2E) * jnp.max(jnp.abs(g_q)) * jnp.max(jnp.abs(g_k))
    slack_fox = -(2.0 * bound + UNDERFLOW_LOG2 + 4.0)
    slack = jnp.concatenate([jnp.full((MLA_HEADS,), NEG, jnp.float32),
                             jnp.full((FOX_HEADS,), slack_fox, jnp.float32)])
    return table, slack


def _token_tile(lp):
    if lp % FLASH_TQ:
        raise ValueError(f"padded length {lp} is not a multiple of {FLASH_TQ}")
    return FLASH_TQ


def kernel(x, meta_tokens, g_mix, g_mlp, w_in_attn, g_cq, w_uq, g_ckv, w_ukv, g_q_mla, g_k_mla,
           g_q_fox, g_k_fox, b_forget, w_out_attn, w_in_conv, conv_w, w_out_conv, w_mlp_up,
           w_mlp_down):
    b, seq, d = x.shape
    assert d == D_MODEL and (PAD + N_META + seq) % BLOCK == 0
    lp = PAD + N_META + seq
    tm = _token_tile(lp)
    bf = jnp.bfloat16

    meta = meta_tokens.astype(x.dtype)
    h = None

    rope_tab = _rope_table(lp)
    tri = (jnp.arange(tm)[:, None] >= jnp.arange(tm)[None, :]).astype(bf)
    selq, selk = _gate_selectors()

    wo_attn, wo_conv, w_conv = w_out_attn.astype(bf), w_out_conv.astype(bf), w_in_conv.astype(bf)
    w_up, w_down = w_mlp_up.astype(bf), w_mlp_down.astype(bf)
    for layer in range(DEPTH):
        j = layer // 2
        gmix = g_mix[layer][None]
        if layer % 2 == 0:
            p = _attn_params(w_in_attn[j], g_cq[j], w_uq[j], g_ckv[j], w_ukv[j], g_q_mla[j],
                             g_k_mla[j], g_q_fox[j], g_k_fox[j], b_forget[j])
            stream = (x, meta) if layer == 0 else (h,)
            q, k, v, gate_end = _attn_in(stream, lp, gmix, p, rope_tab, tri, selq, selk, tm)
            y = _flash(*_pruning_tables(gate_end, g_q_fox[j], g_k_fox[j]), q, k, v)
            wo = wo_attn
        else:
            cw = jnp.zeros((8, d), jnp.float32).at[0:3].set(conv_w[j])
            y = _conv_in(h, gmix, w_conv, j, cw, tm)
            wo = wo_conv
        gmlp = g_mlp[layer][None]
        if layer == 0:
            h = _mix_out_mlp_first(x, meta, y, wo, j, gmlp, w_up, w_down, layer, tm)
        elif layer < DEPTH - 1:
            h = _mix_out_mlp(h.reshape(b * lp, d), y.reshape(b * lp, d), wo, j, gmlp,
                             w_up, w_down, layer, tm).reshape(b, lp, d)
        else:
            return _mix_out_mlp_last(h, y, wo, j, gmlp, w_up, w_down, layer, tm, seq)
```

```python
import functools

import numpy as np
import jax
import jax.numpy as jnp
from jax import lax
from jax.experimental import pallas as pl
from jax.experimental.pallas import tpu as pltpu

D_MODEL = 1024
DEPTH = 4
N_META = 16
BLOCK = 128
PAD = 2 * BLOCK - N_META
REAL_START = PAD + N_META
REAL_PARTS = 3
MLA_HEADS = 8
MLA_NOPE = 64
MLA_ROPE = 32
MLA_QK = MLA_NOPE + MLA_ROPE
MLA_V = 64
Q_LORA = 384
KV_LORA = 256
ROPE_BASE = 10000.0
FOX_HEADS = 8
FOX_DIM = 64
D_FF = 4 * D_MODEL
EPS = 1e-6
NEG = -1e30

LANE = 128
HEADS = MLA_HEADS + FOX_HEADS
HALF_ROPE = MLA_ROPE // 2
FEATURE_BASE = (0, FOX_DIM)
EXTRA_BASE = (FOX_DIM, 0)
N_SPLIT = 3
FLAG_FOX_OFF = 2 * N_SPLIT
FLAG_MLA = MLA_QK
PAD_KEY = NEG
LOG2E = 1.4426950408889634
MISC_GATE = 0
MISC_ROPE = MLA_NOPE

OFF_CQ = 0
OFF_MISC = OFF_CQ + Q_LORA
OFF_CKV = OFF_MISC + LANE
OFF_FQ = OFF_CKV + KV_LORA
OFF_FK = OFF_FQ + FOX_HEADS * FOX_DIM
OFF_FV = OFF_FK + FOX_HEADS * FOX_DIM
W_CAT = OFF_FV + FOX_HEADS * FOX_DIM

(V_GQ_MLA, V_GQ_MLA_SW, V_GK_MLA, V_GK_MLA_SW, V_ADD_Q_MLA, V_B_FORGET) = range(6)
V_GQ_FOX, V_GK_FOX, V_ADD_Q_FOX, V_ONES_K_FOX, V_ONES_V = 6, 8, 10, 12, 14
VEC_ROWS = 16
PAIR = 2 * LANE

FF_CHUNK = 1024
UNDERFLOW_LOG2 = 150.0
FLASH_TQ = 768
FLASH_TK = 256
VMEM_LIMIT = 56 * 1024 * 1024


def _const_spec(shape):
    nd = len(shape)
    return pl.BlockSpec(shape, lambda *_: (0,) * nd, pipeline_mode=pl.Buffered(1))


def _layer_spec(shape, layer):
    nd = len(shape)
    return pl.BlockSpec((1,) + shape, lambda *_: (layer,) + (0,) * nd,
                        pipeline_mode=pl.Buffered(1))


def _input_specs(tm):
    part = tm // REAL_PARTS
    assert REAL_START == part

    def part_spec(k):
        return pl.BlockSpec((1, part, D_MODEL),
                            lambda bi, i: (bi, jnp.maximum(REAL_PARTS * i + k - 1, 0), 0))

    return [part_spec(k) for k in range(REAL_PARTS)] + [_const_spec((N_META, D_MODEL))]


def _input_tile(i, x_parts, meta_ref):
    lead = jnp.concatenate([jnp.zeros((PAD, D_MODEL), jnp.float32), meta_ref[...]], axis=0)
    first = jnp.where(i == 0, lead, x_parts[0][0])
    return jnp.concatenate([first] + [r[0] for r in x_parts[1:]], axis=0)


def _rms(x, g, n):
    ms = jnp.sum(x * x, axis=-1, keepdims=True) * (1.0 / n)
    return x * lax.rsqrt(ms + EPS) * g


def _split3(x):
    hi = x.astype(jnp.bfloat16).astype(jnp.float32)
    r1 = x - hi
    mid = r1.astype(jnp.bfloat16).astype(jnp.float32)
    lo = r1 - mid
    packed = hi + pltpu.roll(mid, FOX_HEADS, 1) + pltpu.roll(lo, 2 * FOX_HEADS, 1)
    return packed.astype(jnp.bfloat16)


def _dot(a, b):
    return jnp.dot(a, b, preferred_element_type=jnp.float32)


def _attn_in_kernel(*refs, tm, from_x):
    n_stream = REAL_PARTS + 1 if from_x else 1
    stream = refs[:n_stream]
    (gmix_ref, wcat_ref, gcq_ref, wuq_ref, gckv_ref, wkn_ref, wv_ref, vec_ref, rope_ref, tri_ref,
     selq_ref, selk_ref, q_ref, k_ref, v_ref, gate_end_ref, carry_ref) = refs[n_stream:]
    i = pl.program_id(1)

    @pl.when(i == 0)
    def _():
        carry_ref[...] = jnp.zeros_like(carry_ref)

    x = _input_tile(i, stream[:-1], stream[-1]) if from_x else stream[0][0]
    hn = _rms(x, gmix_ref[...], D_MODEL).astype(jnp.bfloat16)

    def seg(lo, width):
        return _dot(hn, wcat_ref[:, lo:lo + width])

    def vec(r):
        return vec_ref[r:r + 1, :]

    cos_t = rope_ref[:, 0:LANE]
    sin_sw = rope_ref[:, LANE:2 * LANE]
    gc_q, gs_q = vec(V_GQ_MLA) * cos_t, vec(V_GQ_MLA_SW) * sin_sw
    gc_k, gs_k = vec(V_GK_MLA) * cos_t, vec(V_GK_MLA_SW) * sin_sw
    add_q_mla = vec(V_ADD_Q_MLA)

    lane = lax.broadcasted_iota(jnp.int32, (tm, LANE), 1)
    row = lax.broadcasted_iota(jnp.int32, (tm, LANE), 0)
    valid = (i * tm + row) >= PAD
    pad_key = jnp.where(valid, 0.0, PAD_KEY)
    add_k_mla = jnp.where(lane == FLAG_MLA, pad_key, 0.0)
    halves = (lane < FOX_DIM, lane >= FOX_DIM)
    add_k_fox = [vec(V_ONES_K_FOX + par)
                 + jnp.where(lane == EXTRA_BASE[par] + FLAG_FOX_OFF, pad_key, 0.0)
                 for par in range(2)]

    cq_misc = seg(OFF_CQ, Q_LORA + LANE)
    misc = cq_misc[:, Q_LORA:]
    kpe = jnp.where((lane >= MISC_ROPE) & (lane < MISC_ROPE + MLA_ROPE), misc, 0.0)
    k_rot = jnp.where(lane < MISC_ROPE + HALF_ROPE, pltpu.roll(kpe, LANE - HALF_ROPE, 1),
                      pltpu.roll(kpe, HALF_ROPE, 1)) * gs_k
    xl = misc + vec(V_B_FORGET)
    logf = jnp.minimum(xl, 0.0) - jnp.log1p(jnp.exp(-jnp.abs(xl)))
    logf = jnp.where(valid & (lane >= MISC_GATE) & (lane < MISC_GATE + FOX_HEADS), logf, 0.0)
    cs = _dot(tri_ref[...], _split3(logf))
    cs = (cs + pltpu.roll(cs, LANE - FOX_HEADS, 1)) + pltpu.roll(cs, LANE - 2 * FOX_HEADS, 1)
    cum = jnp.where(lane < FOX_HEADS, cs, 0.0) + carry_ref[0:1, :]
    carry_ref[0:1, :] = cum[tm - 1:tm, :]
    gate_end_ref[0, 0] = carry_ref[...]
    cum3 = _split3(cum * LOG2E)
    gate_q = _dot(cum3, selq_ref[...])
    gate_k = _dot(cum3, selk_ref[...])

    def inv_norm(sq, n):
        return lax.rsqrt(jnp.sum(sq, axis=-1, keepdims=True) + n * EPS)

    def fox_group(g):
        xq4 = seg(OFF_FQ + g * PAIR, PAIR)
        xk4 = seg(OFF_FK + g * PAIR, PAIR)
        xv4 = seg(OFF_FV + g * PAIR, PAIR)
        for e in range(4):
            hd, par = 4 * g + e, e % 2
            sl = slice((e // 2) * LANE, (e // 2 + 1) * LANE)
            gl = slice(hd * LANE, (hd + 1) * LANE)
            xq, xk = xq4[:, sl], xk4[:, sl]
            rq = inv_norm(jnp.where(halves[par], xq * xq, 0.0), FOX_DIM)
            rk = inv_norm(jnp.where(halves[par], xk * xk, 0.0), FOX_DIM)
            q_ref[0, MLA_HEADS + hd] = (xq * vec(V_GQ_FOX + par) * rq + gate_q[:, gl]
                                        + vec(V_ADD_Q_FOX + par)).astype(jnp.bfloat16)
            k_ref[0, MLA_HEADS + hd] = (xk * vec(V_GK_FOX + par) * rk + gate_k[:, gl]
                                        + add_k_fox[par]).astype(jnp.bfloat16)
            v_ref[0, MLA_HEADS + hd] = (jnp.where(halves[par], xv4[:, sl], 0.0)
                                        + vec(V_ONES_V + par)).astype(jnp.bfloat16)

    cqn = _rms(cq_misc[:, :Q_LORA], gcq_ref[...], Q_LORA).astype(jnp.bfloat16)
    ckvn = _rms(seg(OFF_CKV, KV_LORA), gckv_ref[...], KV_LORA).astype(jnp.bfloat16)
    def mla_pair(g):
        cols = slice(g * PAIR, (g + 1) * PAIR)
        cols_sw = slice(MLA_HEADS * LANE + g * PAIR, MLA_HEADS * LANE + (g + 1) * PAIR)
        xq2 = _dot(cqn, wuq_ref[:, cols])
        xq2_sw = _dot(cqn, wuq_ref[:, cols_sw])
        xk2 = _dot(ckvn, wkn_ref[:, cols])
        xv2 = _dot(ckvn, wv_ref[:, cols])
        for e in range(2):
            hd, sl = 2 * g + e, slice(e * LANE, (e + 1) * LANE)
            xq = xq2[:, sl]
            q_ref[0, hd] = ((xq * gc_q + xq2_sw[:, sl] * gs_q) * inv_norm(xq * xq, MLA_QK)
                            + add_q_mla).astype(jnp.bfloat16)
            xk = xk2[:, sl] + kpe
            k_ref[0, hd] = ((xk * gc_k + k_rot) * inv_norm(xk * xk, MLA_QK) + add_k_mla
                            ).astype(jnp.bfloat16)
            v_ref[0, hd] = (xv2[:, sl] + vec(V_ONES_V + e)).astype(jnp.bfloat16)

    fox_group(0)
    mla_pair(0)
    mla_pair(1)
    fox_group(1)
    mla_pair(2)
    mla_pair(3)


def _attn_in(stream, lp, gmix, p, rope_tab, tri, selq, selk, tm):
    from_x = len(stream) == 2
    b, d = stream[0].shape[0], D_MODEL
    nt = lp // tm
    kern = functools.partial(_attn_in_kernel, tm=tm, from_x=from_x)
    stream_specs = (_input_specs(tm) if from_x
                    else [pl.BlockSpec((1, tm, d), lambda bi, i: (bi, i, 0))])
    stream_args = [stream[0]] * REAL_PARTS + [stream[1]] if from_x else [stream[0]]
    qk_shape = jax.ShapeDtypeStruct((b, HEADS, lp, LANE), jnp.bfloat16)
    qk_spec = pl.BlockSpec((1, HEADS, tm, LANE), lambda bi, i: (bi, 0, i, 0))
    return pl.pallas_call(
        kern,
        grid=(b, nt),
        in_specs=stream_specs + [
            _const_spec((1, d)),
            _const_spec((d, W_CAT)),
            _const_spec((1, Q_LORA)),
            _const_spec((Q_LORA, 2 * MLA_HEADS * LANE)),
            _const_spec((1, KV_LORA)),
            _const_spec((KV_LORA, MLA_HEADS * LANE)),
            _const_spec((KV_LORA, MLA_HEADS * LANE)),
            _const_spec((VEC_ROWS, LANE)),
            pl.BlockSpec((tm, 2 * LANE), lambda bi, i: (i, 0)),
            _const_spec((tm, tm)),
            _const_spec((LANE, FOX_HEADS * LANE)),
            _const_spec((LANE, FOX_HEADS * LANE)),
        ],
        out_specs=[qk_spec, qk_spec, qk_spec,
                   pl.BlockSpec((1, 1, 8, LANE), lambda bi, i: (bi, i, 0, 0))],
        out_shape=[qk_shape, qk_shape, qk_shape,
                   jax.ShapeDtypeStruct((b, nt, 8, LANE), jnp.float32)],
        scratch_shapes=[pltpu.VMEM((8, LANE), jnp.float32)],
        compiler_params=pltpu.CompilerParams(
            dimension_semantics=("arbitrary", "arbitrary"), vmem_limit_bytes=VMEM_LIMIT),
        name="attn_in",
    )(*stream_args, gmix, p["wcat"], p["gcq"], p["wuq"], p["gckv"], p["wkn"], p["wv"], p["vec"],
      rope_tab, tri, selq, selk)


def _flash_kernel(gate_end_ref, slack_ref, q_ref, k_ref, v_ref, o_ref, m_ref, acc_ref, al_ref,
                  p_ref, *, tq, tk, nq):
    chunks = tq // tk
    bi, hp = pl.program_id(0), pl.program_id(1)

    def query_block(qi, carry):
        qbase = qi * tq

        def softmax(j, u, base, diagonal, first=False):
            r0 = u * tk if diagonal else 0
            rows = slice(r0, tq)
            start = pl.multiple_of(base + u * tk, tk)
            q_rows = pl.ds(pl.multiple_of(qbase + r0, tk), tq - r0)
            s = lax.dot_general(q_ref[0, j, q_rows, :], k_ref[0, j, pl.ds(start, tk), :],
                                (((1,), (1,)), ((), ())), preferred_element_type=jnp.float32)
            if diagonal:
                row = lax.broadcasted_iota(jnp.int32, (tq - r0, tk), 0)
                col = lax.broadcasted_iota(jnp.int32, (tq - r0, tk), 1)
                s = jnp.where(col <= row, s, NEG)
            if first:
                m_next = jnp.broadcast_to(jnp.max(s, axis=1, keepdims=True), (tq - r0, LANE))
            else:
                m_prev = m_ref[j, rows, :]
                m_next = jnp.maximum(m_prev, jnp.max(s, axis=1, keepdims=True))
                al_ref[j, u, rows, :] = jnp.exp2(m_prev - m_next)
            p = jnp.exp2(s - jnp.concatenate([m_next] * (tk // LANE), axis=1))
            p_ref[j, u, rows, :] = p.astype(jnp.bfloat16)
            m_ref[j, rows, :] = m_next

        def pv(j, u, base, diagonal, first=False):
            r0 = u * tk if diagonal else 0
            rows = slice(r0, tq)
            start = pl.multiple_of(base + u * tk, tk)
            new = _dot(p_ref[j, u, rows, :], v_ref[0, j, pl.ds(start, tk), :])
            if first:
                acc_ref[j, rows, :] = new
            else:
                acc_ref[j, rows, :] = acc_ref[j, rows, :] * al_ref[j, u, rows, :] + new

        def diagonal_block():
            for u in range(chunks):
                softmax(0, u, qbase, True, first=(u == 0))
                if u > 0:
                    pv(1, u - 1, qbase, True, first=(u == 1))
                softmax(1, u, qbase, True, first=(u == 0))
                pv(0, u, qbase, True, first=(u == 0))
            pv(1, chunks - 1, qbase, True, first=(chunks == 1))

        def block(kb, diagonal):
            base = kb * tq
            for u in range(chunks):
                softmax(0, u, base, diagonal)
                if u == 0:
                    pv(1, chunks - 1, jnp.maximum(kb - 1, 0) * tq, False)
                else:
                    pv(1, u - 1, base, diagonal)
                softmax(1, u, base, diagonal)
                pv(0, u, base, diagonal)

        def blocks_needed(j):
            hd = 2 * hp + j
            gate_q = gate_end_ref[bi, hd, jnp.maximum(qi - 1, 0)]
            count = jnp.int32(0)
            for kb in range(nq - 1):
                keep = (kb < qi) & (gate_q - gate_end_ref[bi, hd, kb] >= slack_ref[hd])
                count = count + keep.astype(jnp.int32)
            return count

        diagonal_block()
        al_ref[1, chunks - 1] = jnp.ones((tq, LANE), jnp.float32)
        p_ref[1, chunks - 1] = jnp.zeros((tq, tk), jnp.bfloat16)
        n_off = jnp.maximum(blocks_needed(0), blocks_needed(1))
        first = qi - n_off
        odd = n_off & 1

        @pl.when(odd == 1)
        def _():
            block(first, False)

        def body(pair, c):
            kb = first + odd + 2 * pair
            block(kb, False)
            block(kb + 1, False)
            return c

        lax.fori_loop(0, lax.shift_right_logical(n_off, 1), body, 0)
        pv(1, chunks - 1, jnp.maximum(qi - 1, 0) * tq, False)

        o0 = acc_ref[0]
        o1 = acc_ref[1]
        o0 = o0 / pltpu.roll(o0, FOX_DIM, 1)
        o1 = o1 / pltpu.roll(o1, FOX_DIM, 1)
        lane = lax.broadcasted_iota(jnp.int32, (tq, LANE), 1)
        o = jnp.where(lane < MLA_V, o0, o1)
        o_ref[0, pl.ds(pl.multiple_of(qbase, tq), tq), :] = o.astype(jnp.bfloat16)
        return carry

    lax.fori_loop(0, nq, query_block, 0)


def _flash(gate_end, slack, q, k, v):
    b, _, lp, _ = q.shape
    tq, tk = FLASH_TQ, FLASH_TK
    kern = functools.partial(_flash_kernel, tq=tq, tk=tk, nq=lp // tq)
    qkv_spec = pl.BlockSpec((1, 2, lp, LANE), lambda bi, hp: (bi, hp, 0, 0))
    return pl.pallas_call(
        kern,
        grid=(b, HEADS // 2),
        in_specs=[pl.BlockSpec(memory_space=pltpu.SMEM), pl.BlockSpec(memory_space=pltpu.SMEM),
                  qkv_spec, qkv_spec, qkv_spec],
        out_specs=pl.BlockSpec((1, lp, LANE), lambda bi, hp: (bi, 0, hp)),
        out_shape=jax.ShapeDtypeStruct((b, lp, HEADS * MLA_V), jnp.bfloat16),
        scratch_shapes=[pltpu.VMEM((2, tq, LANE), jnp.float32)] * 2
        + [pltpu.VMEM((2, tq // tk, tq, LANE), jnp.float32),
           pltpu.VMEM((2, tq // tk, tq, tk), jnp.bfloat16)],
        compiler_params=pltpu.CompilerParams(
            dimension_semantics=("arbitrary", "arbitrary"), vmem_limit_bytes=VMEM_LIMIT),
        name="flash",
    )(gate_end, slack, q, k, v)


def _conv_in_kernel(h_ref, gmix_ref, win_ref, cw_ref, y_ref, gs_ref, *, tm):
    i = pl.program_id(1)

    @pl.when(i == 0)
    def _():
        gs_ref[0:8, :] = jnp.zeros((8, D_MODEL), jnp.float32)

    x = h_ref[0]
    hn = _rms(x, gmix_ref[...], D_MODEL).astype(jnp.bfloat16)
    gate_c = _dot(hn, win_ref[0, :, D_MODEL:2 * D_MODEL])
    u = _dot(hn, win_ref[0, :, 2 * D_MODEL:3 * D_MODEL])
    row = lax.broadcasted_iota(jnp.int32, (tm, D_MODEL), 0)
    g = jnp.where((i * tm + row) >= PAD, gate_c * u, 0.0)
    gs_ref[8:tm + 8, :] = g
    y = (cw_ref[0:1, :] * gs_ref[6:tm + 6, :] + cw_ref[1:2, :] * gs_ref[7:tm + 7, :]
         + cw_ref[2:3, :] * g)
    gs_ref[0:8, :] = gs_ref[tm:tm + 8, :]
    gate_b = _dot(hn, win_ref[0, :, 0:D_MODEL])
    y_ref[0] = (gate_b * y).astype(jnp.bfloat16)


def _conv_in(h, gmix, win, layer, cw, tm):
    b, lp, d = h.shape
    kern = functools.partial(_conv_in_kernel, tm=tm)
    return pl.pallas_call(
        kern,
        grid=(b, lp // tm),
        in_specs=[
            pl.BlockSpec((1, tm, d), lambda bi, i: (bi, i, 0)),
            _const_spec((1, d)),
            _layer_spec((d, 3 * d), layer),
            _const_spec((8, d)),
        ],
        out_specs=pl.BlockSpec((1, tm, d), lambda bi, i: (bi, i, 0)),
        out_shape=jax.ShapeDtypeStruct((b, lp, d), jnp.bfloat16),
        scratch_shapes=[pltpu.VMEM((tm + 8, d), jnp.float32)],
        compiler_params=pltpu.CompilerParams(
            dimension_semantics=("arbitrary", "arbitrary"), vmem_limit_bytes=VMEM_LIMIT),
        name="conv_in",
    )(h, gmix, win, cw)


def _mlp_tile(h, y, wo_ref, gmlp_ref, wup_ref, wdn_ref):
    h1 = h + _dot(y, wo_ref[0])
    n = _rms(h1, gmlp_ref[...], D_MODEL).astype(jnp.bfloat16)
    acc = h1
    for c in range(D_FF // FF_CHUNK):
        sl = slice(c * FF_CHUNK, (c + 1) * FF_CHUNK)
        a = jnp.maximum(_dot(n, wup_ref[0, :, sl]), 0.0)
        acc = acc + _dot((a * a).astype(jnp.bfloat16), wdn_ref[0, sl, :])
    return acc


def _mix_out_mlp_kernel(h_ref, y_ref, wo_ref, gmlp_ref, wup_ref, wdn_ref, out_ref):
    out_ref[...] = _mlp_tile(h_ref[...], y_ref[...], wo_ref, gmlp_ref, wup_ref, wdn_ref)


def _mix_out_mlp(h, y, wo, wo_layer, gmlp, wup, wdn, layer, tm):
    r, d = h.shape
    return pl.pallas_call(
        _mix_out_mlp_kernel,
        grid=(r // tm,),
        in_specs=[
            pl.BlockSpec((tm, d), lambda i: (i, 0)),
            pl.BlockSpec((tm, d), lambda i: (i, 0)),
            _layer_spec((d, d), wo_layer),
            _const_spec((1, d)),
            _layer_spec((d, D_FF), layer),
            _layer_spec((D_FF, d), layer),
        ],
        out_specs=pl.BlockSpec((tm, d), lambda i: (i, 0)),
        out_shape=jax.ShapeDtypeStruct((r, d), jnp.float32),
        compiler_params=pltpu.CompilerParams(
            dimension_semantics=("arbitrary",), vmem_limit_bytes=VMEM_LIMIT),
        name="mix_out_mlp",
    )(h, y, wo, gmlp, wup, wdn)


def _mix_out_mlp_first_kernel(*refs):
    x_parts, meta_ref = refs[0:REAL_PARTS], refs[REAL_PARTS]
    y_ref, wo_ref, gmlp_ref, wup_ref, wdn_ref, out_ref = refs[REAL_PARTS + 1:]
    h = _input_tile(pl.program_id(1), x_parts, meta_ref)
    out_ref[0] = _mlp_tile(h, y_ref[0], wo_ref, gmlp_ref, wup_ref, wdn_ref)


def _mix_out_mlp_first(x, meta, y, wo, wo_layer, gmlp, wup, wdn, layer, tm):
    b, lp, d = y.shape
    tile = pl.BlockSpec((1, tm, d), lambda bi, i: (bi, i, 0))
    return pl.pallas_call(
        _mix_out_mlp_first_kernel,
        grid=(b, lp // tm),
        in_specs=_input_specs(tm) + [
            tile,
            _layer_spec((d, d), wo_layer),
            _const_spec((1, d)),
            _layer_spec((d, D_FF), layer),
            _layer_spec((D_FF, d), layer),
        ],
        out_specs=tile,
        out_shape=jax.ShapeDtypeStruct((b, lp, d), jnp.float32),
        compiler_params=pltpu.CompilerParams(
            dimension_semantics=("arbitrary", "arbitrary"), vmem_limit_bytes=VMEM_LIMIT),
        name="mix_out_mlp_first",
    )(*([x] * REAL_PARTS), meta, y, wo, gmlp, wup, wdn)


def _mix_out_mlp_last_kernel(*refs):
    h_parts, y_parts = refs[0:REAL_PARTS], refs[REAL_PARTS:2 * REAL_PARTS]
    wo_ref, gmlp_ref, wup_ref, wdn_ref, out_ref = refs[2 * REAL_PARTS:]
    h = jnp.concatenate([r[0] for r in h_parts], axis=0)
    y = jnp.concatenate([r[0] for r in y_parts], axis=0)
    out_ref[0] = _mlp_tile(h, y, wo_ref, gmlp_ref, wup_ref, wdn_ref)


def _mix_out_mlp_last(h, y, wo, wo_layer, gmlp, wup, wdn, layer, tm, seq):
    b, lp, d = h.shape
    part = tm // REAL_PARTS
    last_part = lp // part - 1

    def part_spec(k):
        return pl.BlockSpec(
            (1, part, d),
            lambda bi, i: (bi, jnp.minimum(REAL_START // part + REAL_PARTS * i + k, last_part), 0))

    parts = [part_spec(k) for k in range(REAL_PARTS)]
    return pl.pallas_call(
        _mix_out_mlp_last_kernel,
        grid=(b, pl.cdiv(seq, tm)),
        in_specs=parts + parts + [
            _layer_spec((d, d), wo_layer),
            _const_spec((1, d)),
            _layer_spec((d, D_FF), layer),
            _layer_spec((D_FF, d), layer),
        ],
        out_specs=pl.BlockSpec((1, tm, d), lambda bi, i: (bi, i, 0)),
        out_shape=jax.ShapeDtypeStruct((b, seq, d), jnp.float32),
        compiler_params=pltpu.CompilerParams(
            dimension_semantics=("arbitrary", "arbitrary"), vmem_limit_bytes=VMEM_LIMIT),
        name="mix_out_mlp_last",
    )(*([h] * REAL_PARTS + [y] * REAL_PARTS), wo, gmlp, wup, wdn)


def _pad_heads(w, heads, dim):
    k = w.shape[0]
    w = w.reshape(k, heads, dim)
    w = jnp.pad(w, ((0, 0), (0, 0), (0, LANE - dim)))
    return w.reshape(k, heads * LANE)


def _lane_vec(v, offset=0):
    return jnp.zeros((LANE,), jnp.float32).at[offset:offset + v.shape[0]].set(v)


def _attn_params(w_in, g_cq, w_uq, g_ckv, w_ukv, g_q_mla, g_k_mla, g_q_fox, g_k_fox, b_forget):
    bf = jnp.bfloat16
    o1 = Q_LORA
    o2 = o1 + KV_LORA
    o3 = o2 + MLA_ROPE
    o4 = o3 + FOX_HEADS * FOX_DIM
    o5 = o4 + FOX_HEADS * FOX_DIM
    o6 = o5 + FOX_HEADS * FOX_DIM
    misc = jnp.zeros((D_MODEL, LANE), jnp.float32)
    misc = misc.at[:, MISC_GATE:MISC_GATE + FOX_HEADS].set(w_in[:, o6:])
    misc = misc.at[:, MISC_ROPE:MISC_ROPE + MLA_ROPE].set(w_in[:, o2:o3])
    wcat = jnp.concatenate([w_in[:, :o1], misc, w_in[:, o1:o2], w_in[:, o3:o6]], axis=1).astype(bf)
    kv = w_ukv.reshape(KV_LORA, MLA_HEADS, MLA_NOPE + MLA_V)
    wkn = _pad_heads(kv[:, :, :MLA_NOPE].reshape(KV_LORA, -1), MLA_HEADS, MLA_NOPE).astype(bf)
    wv = jnp.pad(kv[:, :, MLA_NOPE:].reshape(KV_LORA, MLA_HEADS // 2, 2, MLA_V),
                 ((0, 0), (0, 0), (0, 0), (0, LANE - MLA_V)))
    wv = jnp.concatenate([wv[:, :, 0], jnp.roll(wv[:, :, 1], MLA_V, axis=-1)], axis=-1)
    wv = wv.reshape(KV_LORA, MLA_HEADS * LANE).astype(bf)
    lo, mid, hi = MLA_NOPE, MLA_NOPE + HALF_ROPE, MLA_NOPE + MLA_ROPE
    uq = w_uq.reshape(Q_LORA, MLA_HEADS, MLA_QK)
    uq_sw = jnp.zeros((Q_LORA, MLA_HEADS, LANE), jnp.float32)
    uq_sw = uq_sw.at[:, :, lo:mid].set(uq[:, :, mid:hi]).at[:, :, mid:hi].set(uq[:, :, lo:mid])
    wuq = jnp.concatenate([_pad_heads(w_uq, MLA_HEADS, MLA_QK),
                           uq_sw.reshape(Q_LORA, MLA_HEADS * LANE)], axis=1).astype(bf)

    def swapped(g):
        return jnp.zeros((LANE,), jnp.float32).at[lo:mid].set(g[mid:hi]).at[mid:hi].set(g[lo:mid])

    zero = jnp.zeros((LANE,), jnp.float32)
    rows = [zero] * VEC_ROWS
    rows[V_GQ_MLA] = _lane_vec(g_q_mla) * LOG2E
    rows[V_GQ_MLA_SW] = swapped(g_q_mla) * LOG2E
    rows[V_GK_MLA] = _lane_vec(g_k_mla) * MLA_QK ** 0.5
    rows[V_GK_MLA_SW] = swapped(g_k_mla) * MLA_QK ** 0.5
    for par in range(2):
        feat, extra = FEATURE_BASE[par], EXTRA_BASE[par]
        rows[V_GQ_FOX + par] = _lane_vec(g_q_fox, feat) * LOG2E
        rows[V_GK_FOX + par] = _lane_vec(g_k_fox, feat) * FOX_DIM ** 0.5
        rows[V_ADD_Q_FOX + par] = (zero.at[extra + N_SPLIT:extra + 2 * N_SPLIT].set(1.0)
                                   .at[extra + FLAG_FOX_OFF].set(1.0))
        rows[V_ONES_K_FOX + par] = zero.at[extra:extra + N_SPLIT].set(1.0)
        rows[V_ONES_V + par] = zero.at[extra:extra + FOX_DIM].set(1.0)
    rows[V_B_FORGET] = _lane_vec(b_forget, MISC_GATE)
    rows[V_ADD_Q_MLA] = zero.at[FLAG_MLA].set(1.0)
    vec = jnp.stack(rows)
    return dict(wcat=wcat, gcq=g_cq[None], wuq=wuq, gckv=g_ckv[None], wkn=wkn, wv=wv, vec=vec)


def _gate_selectors():
    selq = np.zeros((LANE, FOX_HEADS * LANE), np.float32)
    selk = np.zeros((LANE, FOX_HEADS * LANE), np.float32)
    for part in range(N_SPLIT):
        for hd in range(FOX_HEADS):
            extra = hd * LANE + EXTRA_BASE[hd % 2]
            selq[part * FOX_HEADS + hd, extra + part] = 1.0
            selk[part * FOX_HEADS + hd, extra + N_SPLIT + part] = -1.0
    return jnp.asarray(selq, jnp.bfloat16), jnp.asarray(selk, jnp.bfloat16)


def _rope_table(lp):
    lane = jnp.arange(LANE, dtype=jnp.int32)
    rotary = (lane >= MLA_NOPE) & (lane < MLA_NOPE + MLA_ROPE)
    first_half = rotary & (lane < MLA_NOPE + HALF_ROPE)
    pair = ((lane - MLA_NOPE) % HALF_ROPE).astype(jnp.float32)
    inv_freq = ROPE_BASE ** (-(2.0 * pair) / MLA_ROPE)
    pos = (jnp.arange(lp, dtype=jnp.int32) - PAD).astype(jnp.float32)
    ang = pos[:, None] * inv_freq[None, :]
    cos_t = jnp.where(lane < MLA_NOPE, 1.0, jnp.where(rotary, jnp.cos(ang), 0.0))
    sin_sw = jnp.where(rotary, jnp.where(first_half, -jnp.sin(ang), jnp.sin(ang)), 0.0)
    return jnp.concatenate([cos_t, sin_sw], axis=1)


def _pruning_tables(gate_end, g_q, g_k):
    b, nt = gate_end.shape[:2]
    fox = jnp.transpose(gate_end[:, :, 0, MISC_GATE:MISC_GATE + FOX_HEADS] * LOG2E, (0, 2, 1))
    table = jnp.concatenate([jnp.zeros((b, MLA_HEADS, nt), jnp.float32), fox], axis=1)
    bound = 1.02 * FOX_DIM * (FOX_DIM ** -0.5 * LOG2E) * jnp.max(jnp.abs(g_q)) * jnp.max(jnp.abs(g_k))
    slack_fox = -(2.0 * bound + UNDERFLOW_LOG2 + 4.0)
    slack = jnp.concatenate([jnp.full((MLA_HEADS,), NEG, jnp.float32),
                             jnp.full((FOX_HEADS,), slack_fox, jnp.float32)])
    return table, slack


def _token_tile(lp):
    if lp % FLASH_TQ:
        raise ValueError(f"padded length {lp} is not a multiple of {FLASH_TQ}")
    return FLASH_TQ


def kernel(x, meta_tokens, g_mix, g_mlp, w_in_attn, g_cq, w_uq, g_ckv, w_ukv, g_q_mla, g_k_mla,
           g_q_fox, g_k_fox, b_forget, w_out_attn, w_in_conv, conv_w, w_out_conv, w_mlp_up,
           w_mlp_down):
    b, seq, d = x.shape
    assert d == D_MODEL and (PAD + N_META + seq) % BLOCK == 0
    lp = PAD + N_META + seq
    tm = _token_tile(lp)
    bf = jnp.bfloat16

    meta = meta_tokens.astype(x.dtype)
    h = None

    rope_tab = _rope_table(lp)
    tri = (jnp.arange(tm)[:, None] >= jnp.arange(tm)[None, :]).astype(bf)
    selq, selk = _gate_selectors()

    wo_attn, wo_conv, w_conv = w_out_attn.astype(bf), w_out_conv.astype(bf), w_in_conv.astype(bf)
    w_up, w_down = w_mlp_up.astype(bf), w_mlp_down.astype(bf)
    for layer in range(DEPTH):
        j = layer // 2
        gmix = g_mix[layer][None]
        if layer % 2 == 0:
            p = _attn_params(w_in_attn[j], g_cq[j], w_uq[j], g_ckv[j], w_ukv[j], g_q_mla[j],
                             g_k_mla[j], g_q_fox[j], g_k_fox[j], b_forget[j])
            stream = (x, meta) if layer == 0 else (h,)
            q, k, v, gate_end = _attn_in(stream, lp, gmix, p, rope_tab, tri, selq, selk, tm)
            y = _flash(*_pruning_tables(gate_end, g_q_fox[j], g_k_fox[j]), q, k, v)
            wo = wo_attn
        else:
            cw = jnp.zeros((8, d), jnp.float32).at[0:3].set(conv_w[j])
            y = _conv_in(h, gmix, w_conv, j, cw, tm)
            wo = wo_conv
        gmlp = g_mlp[layer][None]
        if layer == 0:
            h = _mix_out_mlp_first(x, meta, y, wo, j, gmlp, w_up, w_down, layer, tm)
        elif layer < DEPTH - 1:
            h = _mix_out_mlp(h.reshape(b * lp, d), y.reshape(b * lp, d), wo, j, gmlp,
                             w_up, w_down, layer, tm).reshape(b, lp, d)
        else:
            return _mix_out_mlp_last(h, y, wo, j, gmlp, w_up, w_down, layer, tm, seq)
```

```python
import functools

import numpy as np
import jax
import jax.numpy as jnp
from jax import lax
from jax.experimental import pallas as pl
from jax.experimental.pallas import tpu as pltpu

D_MODEL = 1024
DEPTH = 4
N_META = 16
BLOCK = 128
PAD = 2 * BLOCK - N_META
REAL_START = PAD + N_META
REAL_PARTS = 3
MLA_HEADS = 8
MLA_NOPE = 64
MLA_ROPE = 32
MLA_QK = MLA_NOPE + MLA_ROPE
MLA_V = 64
Q_LORA = 384
KV_LORA = 256
ROPE_BASE = 10000.0
FOX_HEADS = 8
FOX_DIM = 64
D_FF = 4 * D_MODEL
EPS = 1e-6
NEG = -1e30

LANE = 128
HEADS = MLA_HEADS + FOX_HEADS
HALF_ROPE = MLA_ROPE // 2
FEATURE_BASE = (0, FOX_DIM)
EXTRA_BASE = (FOX_DIM, 0)
N_SPLIT = 3
FLAG_FOX_OFF = 2 * N_SPLIT
FLAG_MLA = MLA_QK
PAD_KEY = NEG
LOG2E = 1.4426950408889634
MISC_GATE = 0
MISC_ROPE = MLA_NOPE

OFF_CQ = 0
OFF_MISC = OFF_CQ + Q_LORA
OFF_CKV = OFF_MISC + LANE
OFF_FQ = OFF_CKV + KV_LORA
OFF_FK = OFF_FQ + FOX_HEADS * FOX_DIM
OFF_FV = OFF_FK + FOX_HEADS * FOX_DIM
W_CAT = OFF_FV + FOX_HEADS * FOX_DIM

(V_GQ_MLA, V_GQ_MLA_SW, V_GK_MLA, V_GK_MLA_SW, V_ADD_Q_MLA, V_B_FORGET) = range(6)
V_GQ_FOX, V_GK_FOX, V_ADD_Q_FOX, V_ONES_K_FOX, V_ONES_V = 6, 8, 10, 12, 14
VEC_ROWS = 16
PAIR = 2 * LANE

FF_CHUNK = 1024
UNDERFLOW_LOG2 = 150.0
FLASH_TQ = 768
FLASH_TK = 256
VMEM_LIMIT = 56 * 1024 * 1024


def _const_spec(shape):
    nd = len(shape)
    return pl.BlockSpec(shape, lambda *_: (0,) * nd, pipeline_mode=pl.Buffered(1))


def _layer_spec(shape, layer):
    nd = len(shape)
    return pl.BlockSpec((1,) + shape, lambda *_: (layer,) + (0,) * nd,
                        pipeline_mode=pl.Buffered(1))


def _input_specs(tm):
    part = tm // REAL_PARTS
    assert REAL_START == part

    def part_spec(k):
        return pl.BlockSpec((1, part, D_MODEL),
                            lambda bi, i: (bi, jnp.maximum(REAL_PARTS * i + k - 1, 0), 0))

    return [part_spec(k) for k in range(REAL_PARTS)] + [_const_spec((N_META, D_MODEL))]


def _input_tile(i, x_parts, meta_ref):
    lead = jnp.concatenate([jnp.zeros((PAD, D_MODEL), jnp.float32), meta_ref[...]], axis=0)
    first = jnp.where(i == 0, lead, x_parts[0][0])
    return jnp.concatenate([first] + [r[0] for r in x_parts[1:]], axis=0)


def _rms(x, g, n):
    ms = jnp.sum(x * x, axis=-1, keepdims=True) * (1.0 / n)
    return x * lax.rsqrt(ms + EPS) * g


def _split3(x):
    hi = x.astype(jnp.bfloat16).astype(jnp.float32)
    r1 = x - hi
    mid = r1.astype(jnp.bfloat16).astype(jnp.float32)
    lo = r1 - mid
    packed = hi + pltpu.roll(mid, FOX_HEADS, 1) + pltpu.roll(lo, 2 * FOX_HEADS, 1)
    return packed.astype(jnp.bfloat16)


def _dot(a, b):
    return jnp.dot(a, b, preferred_element_type=jnp.float32)


def _attn_in_kernel(*refs, tm, from_x):
    n_stream = REAL_PARTS + 1 if from_x else 1
    stream = refs[:n_stream]
    (gmix_ref, wcat_ref, gcq_ref, wuq_ref, gckv_ref, wkn_ref, wv_ref, vec_ref, rope_ref, tri_ref,
     selq_ref, selk_ref, q_ref, k_ref, v_ref, gate_end_ref, carry_ref) = refs[n_stream:]
    i = pl.program_id(1)

    @pl.when(i == 0)
    def _():
        carry_ref[...] = jnp.zeros_like(carry_ref)

    x = _input_tile(i, stream[:-1], stream[-1]) if from_x else stream[0][0]
    hn = _rms(x, gmix_ref[...], D_MODEL).astype(jnp.bfloat16)

    def seg(lo, width):
        return _dot(hn, wcat_ref[:, lo:lo + width])

    def vec(r):
        return vec_ref[r:r + 1, :]

    cos_t = rope_ref[:, 0:LANE]
    sin_sw = rope_ref[:, LANE:2 * LANE]
    gc_q, gs_q = vec(V_GQ_MLA) * cos_t, vec(V_GQ_MLA_SW) * sin_sw
    gc_k, gs_k = vec(V_GK_MLA) * cos_t, vec(V_GK_MLA_SW) * sin_sw
    add_q_mla = vec(V_ADD_Q_MLA)

    lane = lax.broadcasted_iota(jnp.int32, (tm, LANE), 1)
    row = lax.broadcasted_iota(jnp.int32, (tm, LANE), 0)
    valid = (i * tm + row) >= PAD
    pad_key = jnp.where(valid, 0.0, PAD_KEY)
    add_k_mla = jnp.where(lane == FLAG_MLA, pad_key, 0.0)
    halves = (lane < FOX_DIM, lane >= FOX_DIM)
    add_k_fox = [vec(V_ONES_K_FOX + par)
                 + jnp.where(lane == EXTRA_BASE[par] + FLAG_FOX_OFF, pad_key, 0.0)
                 for par in range(2)]

    cq_misc = seg(OFF_CQ, Q_LORA + LANE)
    misc = cq_misc[:, Q_LORA:]
    kpe = jnp.where((lane >= MISC_ROPE) & (lane < MISC_ROPE + MLA_ROPE), misc, 0.0)
    k_rot = jnp.where(lane < MISC_ROPE + HALF_ROPE, pltpu.roll(kpe, LANE - HALF_ROPE, 1),
                      pltpu.roll(kpe, HALF_ROPE, 1)) * gs_k
    xl = misc + vec(V_B_FORGET)
    logf = jnp.minimum(xl, 0.0) - jnp.log1p(jnp.exp(-jnp.abs(xl)))
    logf = jnp.where(valid & (lane >= MISC_GATE) & (lane < MISC_GATE + FOX_HEADS), logf, 0.0)
    cs = _dot(tri_ref[...], _split3(logf))
    cs = (cs + pltpu.roll(cs, LANE - FOX_HEADS, 1)) + pltpu.roll(cs, LANE - 2 * FOX_HEADS, 1)
    cum = jnp.where(lane < FOX_HEADS, cs, 0.0) + carry_ref[0:1, :]
    carry_ref[0:1, :] = cum[tm - 1:tm, :]
    gate_end_ref[0, 0] = jnp.zeros((8, LANE), jnp.float32)
    for c in range(tm // FLASH_TK):
        gate_end_ref[0, 0, c:c + 1, :] = cum[(c + 1) * FLASH_TK - 1:(c + 1) * FLASH_TK, :]
    cum3 = _split3(cum * LOG2E)
    gate_q = _dot(cum3, selq_ref[...])
    gate_k = _dot(cum3, selk_ref[...])

    def inv_norm(sq, n):
        return lax.rsqrt(jnp.sum(sq, axis=-1, keepdims=True) + n * EPS)

    def fox_group(g):
        xq4 = seg(OFF_FQ + g * PAIR, PAIR)
        xk4 = seg(OFF_FK + g * PAIR, PAIR)
        xv4 = seg(OFF_FV + g * PAIR, PAIR)
        for e in range(4):
            hd, par = 4 * g + e, e % 2
            sl = slice((e // 2) * LANE, (e // 2 + 1) * LANE)
            gl = slice(hd * LANE, (hd + 1) * LANE)
            xq, xk = xq4[:, sl], xk4[:, sl]
            rq = inv_norm(jnp.where(halves[par], xq * xq, 0.0), FOX_DIM)
            rk = inv_norm(jnp.where(halves[par], xk * xk, 0.0), FOX_DIM)
            q_ref[0, MLA_HEADS + hd] = (xq * vec(V_GQ_FOX + par) * rq + gate_q[:, gl]
                                        + vec(V_ADD_Q_FOX + par)).astype(jnp.bfloat16)
            k_ref[0, MLA_HEADS + hd] = (xk * vec(V_GK_FOX + par) * rk + gate_k[:, gl]
                                        + add_k_fox[par]).astype(jnp.bfloat16)
            v_ref[0, MLA_HEADS + hd] = (jnp.where(halves[par], xv4[:, sl], 0.0)
                                        + vec(V_ONES_V + par)).astype(jnp.bfloat16)

    cqn = _rms(cq_misc[:, :Q_LORA], gcq_ref[...], Q_LORA).astype(jnp.bfloat16)
    ckvn = _rms(seg(OFF_CKV, KV_LORA), gckv_ref[...], KV_LORA).astype(jnp.bfloat16)
    def mla_pair(g):
        cols = slice(g * PAIR, (g + 1) * PAIR)
        cols_sw = slice(MLA_HEADS * LANE + g * PAIR, MLA_HEADS * LANE + (g + 1) * PAIR)
        xq2 = _dot(cqn, wuq_ref[:, cols])
        xq2_sw = _dot(cqn, wuq_ref[:, cols_sw])
        xk2 = _dot(ckvn, wkn_ref[:, cols])
        xv2 = _dot(ckvn, wv_ref[:, cols])
        for e in range(2):
            hd, sl = 2 * g + e, slice(e * LANE, (e + 1) * LANE)
            xq = xq2[:, sl]
            q_ref[0, hd] = ((xq * gc_q + xq2_sw[:, sl] * gs_q) * inv_norm(xq * xq, MLA_QK)
                            + add_q_mla).astype(jnp.bfloat16)
            xk = xk2[:, sl] + kpe
            k_ref[0, hd] = ((xk * gc_k + k_rot) * inv_norm(xk * xk, MLA_QK) + add_k_mla
                            ).astype(jnp.bfloat16)
            v_ref[0, hd] = (xv2[:, sl] + vec(V_ONES_V + e)).astype(jnp.bfloat16)

    fox_group(0)
    mla_pair(0)
    mla_pair(1)
    fox_group(1)
    mla_pair(2)
    mla_pair(3)


def _attn_in(stream, lp, gmix, p, rope_tab, tri, selq, selk, tm):
    from_x = len(stream) == 2
    b, d = stream[0].shape[0], D_MODEL
    nt = lp // tm
    kern = functools.partial(_attn_in_kernel, tm=tm, from_x=from_x)
    stream_specs = (_input_specs(tm) if from_x
                    else [pl.BlockSpec((1, tm, d), lambda bi, i: (bi, i, 0))])
    stream_args = [stream[0]] * REAL_PARTS + [stream[1]] if from_x else [stream[0]]
    qk_shape = jax.ShapeDtypeStruct((b, HEADS, lp, LANE), jnp.bfloat16)
    qk_spec = pl.BlockSpec((1, HEADS, tm, LANE), lambda bi, i: (bi, 0, i, 0))
    return pl.pallas_call(
        kern,
        grid=(b, nt),
        in_specs=stream_specs + [
            _const_spec((1, d)),
            _const_spec((d, W_CAT)),
            _const_spec((1, Q_LORA)),
            _const_spec((Q_LORA, 2 * MLA_HEADS * LANE)),
            _const_spec((1, KV_LORA)),
            _const_spec((KV_LORA, MLA_HEADS * LANE)),
            _const_spec((KV_LORA, MLA_HEADS * LANE)),
            _const_spec((VEC_ROWS, LANE)),
            pl.BlockSpec((tm, 2 * LANE), lambda bi, i: (i, 0)),
            _const_spec((tm, tm)),
            _const_spec((LANE, FOX_HEADS * LANE)),
            _const_spec((LANE, FOX_HEADS * LANE)),
        ],
        out_specs=[qk_spec, qk_spec, qk_spec,
                   pl.BlockSpec((1, 1, 8, LANE), lambda bi, i: (bi, i, 0, 0))],
        out_shape=[qk_shape, qk_shape, qk_shape,
                   jax.ShapeDtypeStruct((b, nt, 8, LANE), jnp.float32)],
        scratch_shapes=[pltpu.VMEM((8, LANE), jnp.float32)],
        compiler_params=pltpu.CompilerParams(
            dimension_semantics=("arbitrary", "arbitrary"), vmem_limit_bytes=VMEM_LIMIT),
        name="attn_in",
    )(*stream_args, gmix, p["wcat"], p["gcq"], p["wuq"], p["gckv"], p["wkn"], p["wv"], p["vec"],
      rope_tab, tri, selq, selk)


def _flash_kernel(gate_end_ref, slack_ref, q_ref, k_ref, v_ref, o_ref, m_ref, acc_ref, al_ref,
                  p_ref, *, tq, tk, nq):
    chunks = tq // tk
    bi, hp = pl.program_id(0), pl.program_id(1)

    def query_block(qi, carry):
        qbase = qi * tq

        def softmax(j, u, base, diagonal, first=False):
            r0 = u * tk if diagonal else 0
            rows = slice(r0, tq)
            start = pl.multiple_of(base + u * tk, tk)
            q_rows = pl.ds(pl.multiple_of(qbase + r0, tk), tq - r0)
            s = lax.dot_general(q_ref[0, j, q_rows, :], k_ref[0, j, pl.ds(start, tk), :],
                                (((1,), (1,)), ((), ())), preferred_element_type=jnp.float32)
            if diagonal:
                row = lax.broadcasted_iota(jnp.int32, (tq - r0, tk), 0)
                col = lax.broadcasted_iota(jnp.int32, (tq - r0, tk), 1)
                s = jnp.where(col <= row, s, NEG)
            if first:
                m_next = jnp.broadcast_to(jnp.max(s, axis=1, keepdims=True), (tq - r0, LANE))
            else:
                m_prev = m_ref[j, rows, :]
                m_next = jnp.maximum(m_prev, jnp.max(s, axis=1, keepdims=True))
                al_ref[j, u, rows, :] = jnp.exp2(m_prev - m_next)
            p = jnp.exp2(s - jnp.concatenate([m_next] * (tk // LANE), axis=1))
            p_ref[j, u, rows, :] = p.astype(jnp.bfloat16)
            m_ref[j, rows, :] = m_next

        def pv(j, u, base, diagonal, first=False):
            r0 = u * tk if diagonal else 0
            rows = slice(r0, tq)
            start = pl.multiple_of(base + u * tk, tk)
            new = _dot(p_ref[j, u, rows, :], v_ref[0, j, pl.ds(start, tk), :])
            if first:
                acc_ref[j, rows, :] = new
            else:
                acc_ref[j, rows, :] = acc_ref[j, rows, :] * al_ref[j, u, rows, :] + new

        def diagonal_block():
            for u in range(chunks):
                softmax(0, u, qbase, True, first=(u == 0))
                if u > 0:
                    pv(1, u - 1, qbase, True, first=(u == 1))
                softmax(1, u, qbase, True, first=(u == 0))
                pv(0, u, qbase, True, first=(u == 0))
            pv(1, chunks - 1, qbase, True, first=(chunks == 1))

        def block(kb, diagonal, start=0):
            base = kb * tq
            for u in range(start, chunks):
                softmax(0, u, base, diagonal)
                if u == start:
                    pv(1, chunks - 1, jnp.maximum(kb - 1, 0) * tq, False)
                else:
                    pv(1, u - 1, base, diagonal)
                softmax(1, u, base, diagonal)
                pv(0, u, base, diagonal)

        def chunks_needed(j):
            hd = 2 * hp + j
            gate_q = gate_end_ref[bi, hd, jnp.maximum(chunks * qi - 1, 0)]
            count = jnp.int32(0)
            for c in range(chunks * (nq - 1)):
                keep = (c < chunks * qi) & (gate_q - gate_end_ref[bi, hd, c] >= slack_ref[hd])
                count = count + keep.astype(jnp.int32)
            return count

        diagonal_block()
        al_ref[1, chunks - 1] = jnp.ones((tq, LANE), jnp.float32)
        p_ref[1, chunks - 1] = jnp.zeros((tq, tk), jnp.bfloat16)
        n_chunks = jnp.maximum(chunks_needed(0), chunks_needed(1))
        partial = n_chunks % chunks
        n_full = n_chunks // chunks
        first = qi - n_full
        for start in range(1, chunks):

            @pl.when(partial == chunks - start)
            def _():
                block(first - 1, False, start)

        odd = n_full & 1

        @pl.when(odd == 1)
        def _():
            block(first, False)

        def body(pair, c):
            kb = first + odd + 2 * pair
            block(kb, False)
            block(kb + 1, False)
            return c

        lax.fori_loop(0, lax.shift_right_logical(n_full, 1), body, 0)
        pv(1, chunks - 1, jnp.maximum(qi - 1, 0) * tq, False)

        o0 = acc_ref[0]
        o1 = acc_ref[1]
        o0 = o0 / pltpu.roll(o0, FOX_DIM, 1)
        o1 = o1 / pltpu.roll(o1, FOX_DIM, 1)
        lane = lax.broadcasted_iota(jnp.int32, (tq, LANE), 1)
        o = jnp.where(lane < MLA_V, o0, o1)
        o_ref[0, pl.ds(pl.multiple_of(qbase, tq), tq), :] = o.astype(jnp.bfloat16)
        return carry

    lax.fori_loop(0, nq, query_block, 0)


def _flash(gate_end, slack, q, k, v):
    b, _, lp, _ = q.shape
    tq, tk = FLASH_TQ, FLASH_TK
    kern = functools.partial(_flash_kernel, tq=tq, tk=tk, nq=lp // tq)
    qkv_spec = pl.BlockSpec((1, 2, lp, LANE), lambda bi, hp: (bi, hp, 0, 0))
    return pl.pallas_call(
        kern,
        grid=(b, HEADS // 2),
        in_specs=[pl.BlockSpec(memory_space=pltpu.SMEM), pl.BlockSpec(memory_space=pltpu.SMEM),
                  qkv_spec, qkv_spec, qkv_spec],
        out_specs=pl.BlockSpec((1, lp, LANE), lambda bi, hp: (bi, 0, hp)),
        out_shape=jax.ShapeDtypeStruct((b, lp, HEADS * MLA_V), jnp.bfloat16),
        scratch_shapes=[pltpu.VMEM((2, tq, LANE), jnp.float32)] * 2
        + [pltpu.VMEM((2, tq // tk, tq, LANE), jnp.float32),
           pltpu.VMEM((2, tq // tk, tq, tk), jnp.bfloat16)],
        compiler_params=pltpu.CompilerParams(
            dimension_semantics=("arbitrary", "arbitrary"), vmem_limit_bytes=VMEM_LIMIT),
        name="flash",
    )(gate_end, slack, q, k, v)


def _conv_in_kernel(h_ref, gmix_ref, win_ref, cw_ref, y_ref, gs_ref, *, tm):
    i = pl.program_id(1)

    @pl.when(i == 0)
    def _():
        gs_ref[0:8, :] = jnp.zeros((8, D_MODEL), jnp.float32)

    x = h_ref[0]
    hn = _rms(x, gmix_ref[...], D_MODEL).astype(jnp.bfloat16)
    gate_c = _dot(hn, win_ref[0, :, D_MODEL:2 * D_MODEL])
    u = _dot(hn, win_ref[0, :, 2 * D_MODEL:3 * D_MODEL])
    row = lax.broadcasted_iota(jnp.int32, (tm, D_MODEL), 0)
    g = jnp.where((i * tm + row) >= PAD, gate_c * u, 0.0)
    gs_ref[8:tm + 8, :] = g
    y = (cw_ref[0:1, :] * gs_ref[6:tm + 6, :] + cw_ref[1:2, :] * gs_ref[7:tm + 7, :]
         + cw_ref[2:3, :] * g)
    gs_ref[0:8, :] = gs_ref[tm:tm + 8, :]
    gate_b = _dot(hn, win_ref[0, :, 0:D_MODEL])
    y_ref[0] = (gate_b * y).astype(jnp.bfloat16)


def _conv_in(h, gmix, win, layer, cw, tm):
    b, lp, d = h.shape
    kern = functools.partial(_conv_in_kernel, tm=tm)
    return pl.pallas_call(
        kern,
        grid=(b, lp // tm),
        in_specs=[
            pl.BlockSpec((1, tm, d), lambda bi, i: (bi, i, 0)),
            _const_spec((1, d)),
            _layer_spec((d, 3 * d), layer),
            _const_spec((8, d)),
        ],
        out_specs=pl.BlockSpec((1, tm, d), lambda bi, i: (bi, i, 0)),
        out_shape=jax.ShapeDtypeStruct((b, lp, d), jnp.bfloat16),
        scratch_shapes=[pltpu.VMEM((tm + 8, d), jnp.float32)],
        compiler_params=pltpu.CompilerParams(
            dimension_semantics=("arbitrary", "arbitrary"), vmem_limit_bytes=VMEM_LIMIT),
        name="conv_in",
    )(h, gmix, win, cw)


def _mlp_tile(h, y, wo_ref, gmlp_ref, wup_ref, wdn_ref):
    h1 = h + _dot(y, wo_ref[0])
    n = _rms(h1, gmlp_ref[...], D_MODEL).astype(jnp.bfloat16)
    acc = h1
    for c in range(D_FF // FF_CHUNK):
        sl = slice(c * FF_CHUNK, (c + 1) * FF_CHUNK)
        a = jnp.maximum(_dot(n, wup_ref[0, :, sl]), 0.0)
        acc = acc + _dot((a * a).astype(jnp.bfloat16), wdn_ref[0, sl, :])
    return acc


def _mix_out_mlp_kernel(h_ref, y_ref, wo_ref, gmlp_ref, wup_ref, wdn_ref, out_ref):
    out_ref[...] = _mlp_tile(h_ref[...], y_ref[...], wo_ref, gmlp_ref, wup_ref, wdn_ref)


def _mix_out_mlp(h, y, wo, wo_layer, gmlp, wup, wdn, layer, tm):
    r, d = h.shape
    return pl.pallas_call(
        _mix_out_mlp_kernel,
        grid=(r // tm,),
        in_specs=[
            pl.BlockSpec((tm, d), lambda i: (i, 0)),
            pl.BlockSpec((tm, d), lambda i: (i, 0)),
            _layer_spec((d, d), wo_layer),
            _const_spec((1, d)),
            _layer_spec((d, D_FF), layer),
            _layer_spec((D_FF, d), layer),
        ],
        out_specs=pl.BlockSpec((tm, d), lambda i: (i, 0)),
        out_shape=jax.ShapeDtypeStruct((r, d), jnp.float32),
        compiler_params=pltpu.CompilerParams(
            dimension_semantics=("arbitrary",), vmem_limit_bytes=VMEM_LIMIT),
        name="mix_out_mlp",
    )(h, y, wo, gmlp, wup, wdn)


def _mix_out_mlp_first_kernel(*refs):
    x_parts, meta_ref = refs[0:REAL_PARTS], refs[REAL_PARTS]
    y_ref, wo_ref, gmlp_ref, wup_ref, wdn_ref, out_ref = refs[REAL_PARTS + 1:]
    h = _input_tile(pl.program_id(1), x_parts, meta_ref)
    out_ref[0] = _mlp_tile(h, y_ref[0], wo_ref, gmlp_ref, wup_ref, wdn_ref)


def _mix_out_mlp_first(x, meta, y, wo, wo_layer, gmlp, wup, wdn, layer, tm):
    b, lp, d = y.shape
    tile = pl.BlockSpec((1, tm, d), lambda bi, i: (bi, i, 0))
    return pl.pallas_call(
        _mix_out_mlp_first_kernel,
        grid=(b, lp // tm),
        in_specs=_input_specs(tm) + [
            tile,
            _layer_spec((d, d), wo_layer),
            _const_spec((1, d)),
            _layer_spec((d, D_FF), layer),
            _layer_spec((D_FF, d), layer),
        ],
        out_specs=tile,
        out_shape=jax.ShapeDtypeStruct((b, lp, d), jnp.float32),
        compiler_params=pltpu.CompilerParams(
            dimension_semantics=("arbitrary", "arbitrary"), vmem_limit_bytes=VMEM_LIMIT),
        name="mix_out_mlp_first",
    )(*([x] * REAL_PARTS), meta, y, wo, gmlp, wup, wdn)


def _mix_out_mlp_last_kernel(*refs):
    h_parts, y_parts = refs[0:REAL_PARTS], refs[REAL_PARTS:2 * REAL_PARTS]
    wo_ref, gmlp_ref, wup_ref, wdn_ref, out_ref = refs[2 * REAL_PARTS:]
    h = jnp.concatenate([r[0] for r in h_parts], axis=0)
    y = jnp.concatenate([r[0] for r in y_parts], axis=0)
    out_ref[0] = _mlp_tile(h, y, wo_ref, gmlp_ref, wup_ref, wdn_ref)


def _mix_out_mlp_last(h, y, wo, wo_layer, gmlp, wup, wdn, layer, tm, seq):
    b, lp, d = h.shape
    part = tm // REAL_PARTS
    last_part = lp // part - 1

    def part_spec(k):
        return pl.BlockSpec(
            (1, part, d),
            lambda bi, i: (bi, jnp.minimum(REAL_START // part + REAL_PARTS * i + k, last_part), 0))

    parts = [part_spec(k) for k in range(REAL_PARTS)]
    return pl.pallas_call(
        _mix_out_mlp_last_kernel,
        grid=(b, pl.cdiv(seq, tm)),
        in_specs=parts + parts + [
            _layer_spec((d, d), wo_layer),
            _const_spec((1, d)),
            _layer_spec((d, D_FF), layer),
            _layer_spec((D_FF, d), layer),
        ],
        out_specs=pl.BlockSpec((1, tm, d), lambda bi, i: (bi, i, 0)),
        out_shape=jax.ShapeDtypeStruct((b, seq, d), jnp.float32),
        compiler_params=pltpu.CompilerParams(
            dimension_semantics=("arbitrary", "arbitrary"), vmem_limit_bytes=VMEM_LIMIT),
        name="mix_out_mlp_last",
    )(*([h] * REAL_PARTS + [y] * REAL_PARTS), wo, gmlp, wup, wdn)


def _pad_heads(w, heads, dim):
    k = w.shape[0]
    w = w.reshape(k, heads, dim)
    w = jnp.pad(w, ((0, 0), (0, 0), (0, LANE - dim)))
    return w.reshape(k, heads * LANE)


def _lane_vec(v, offset=0):
    return jnp.zeros((LANE,), jnp.float32).at[offset:offset + v.shape[0]].set(v)


def _attn_params(w_in, g_cq, w_uq, g_ckv, w_ukv, g_q_mla, g_k_mla, g_q_fox, g_k_fox, b_forget):
    bf = jnp.bfloat16
    o1 = Q_LORA
    o2 = o1 + KV_LORA
    o3 = o2 + MLA_ROPE
    o4 = o3 + FOX_HEADS * FOX_DIM
    o5 = o4 + FOX_HEADS * FOX_DIM
    o6 = o5 + FOX_HEADS * FOX_DIM
    misc = jnp.zeros((D_MODEL, LANE), jnp.float32)
    misc = misc.at[:, MISC_GATE:MISC_GATE + FOX_HEADS].set(w_in[:, o6:])
    misc = misc.at[:, MISC_ROPE:MISC_ROPE + MLA_ROPE].set(w_in[:, o2:o3])
    wcat = jnp.concatenate([w_in[:, :o1], misc, w_in[:, o1:o2], w_in[:, o3:o6]], axis=1).astype(bf)
    kv = w_ukv.reshape(KV_LORA, MLA_HEADS, MLA_NOPE + MLA_V)
    wkn = _pad_heads(kv[:, :, :MLA_NOPE].reshape(KV_LORA, -1), MLA_HEADS, MLA_NOPE).astype(bf)
    wv = jnp.pad(kv[:, :, MLA_NOPE:].reshape(KV_LORA, MLA_HEADS // 2, 2, MLA_V),
                 ((0, 0), (0, 0), (0, 0), (0, LANE - MLA_V)))
    wv = jnp.concatenate([wv[:, :, 0], jnp.roll(wv[:, :, 1], MLA_V, axis=-1)], axis=-1)
    wv = wv.reshape(KV_LORA, MLA_HEADS * LANE).astype(bf)
    lo, mid, hi = MLA_NOPE, MLA_NOPE + HALF_ROPE, MLA_NOPE + MLA_ROPE
    uq = w_uq.reshape(Q_LORA, MLA_HEADS, MLA_QK)
    uq_sw = jnp.zeros((Q_LORA, MLA_HEADS, LANE), jnp.float32)
    uq_sw = uq_sw.at[:, :, lo:mid].set(uq[:, :, mid:hi]).at[:, :, mid:hi].set(uq[:, :, lo:mid])
    wuq = jnp.concatenate([_pad_heads(w_uq, MLA_HEADS, MLA_QK),
                           uq_sw.reshape(Q_LORA, MLA_HEADS * LANE)], axis=1).astype(bf)

    def swapped(g):
        return jnp.zeros((LANE,), jnp.float32).at[lo:mid].set(g[mid:hi]).at[mid:hi].set(g[lo:mid])

    zero = jnp.zeros((LANE,), jnp.float32)
    rows = [zero] * VEC_ROWS
    rows[V_GQ_MLA] = _lane_vec(g_q_mla) * LOG2E
    rows[V_GQ_MLA_SW] = swapped(g_q_mla) * LOG2E
    rows[V_GK_MLA] = _lane_vec(g_k_mla) * MLA_QK ** 0.5
    rows[V_GK_MLA_SW] = swapped(g_k_mla) * MLA_QK ** 0.5
    for par in range(2):
        feat, extra = FEATURE_BASE[par], EXTRA_BASE[par]
        rows[V_GQ_FOX + par] = _lane_vec(g_q_fox, feat) * LOG2E
        rows[V_GK_FOX + par] = _lane_vec(g_k_fox, feat) * FOX_DIM ** 0.5
        rows[V_ADD_Q_FOX + par] = (zero.at[extra + N_SPLIT:extra + 2 * N_SPLIT].set(1.0)
                                   .at[extra + FLAG_FOX_OFF].set(1.0))
        rows[V_ONES_K_FOX + par] = zero.at[extra:extra + N_SPLIT].set(1.0)
        rows[V_ONES_V + par] = zero.at[extra:extra + FOX_DIM].set(1.0)
    rows[V_B_FORGET] = _lane_vec(b_forget, MISC_GATE)
    rows[V_ADD_Q_MLA] = zero.at[FLAG_MLA].set(1.0)
    vec = jnp.stack(rows)
    return dict(wcat=wcat, gcq=g_cq[None], wuq=wuq, gckv=g_ckv[None], wkn=wkn, wv=wv, vec=vec)


def _gate_selectors():
    selq = np.zeros((LANE, FOX_HEADS * LANE), np.float32)
    selk = np.zeros((LANE, FOX_HEADS * LANE), np.float32)
    for part in range(N_SPLIT):
        for hd in range(FOX_HEADS):
            extra = hd * LANE + EXTRA_BASE[hd % 2]
            selq[part * FOX_HEADS + hd, extra + part] = 1.0
            selk[part * FOX_HEADS + hd, extra + N_SPLIT + part] = -1.0
    return jnp.asarray(selq, jnp.bfloat16), jnp.asarray(selk, jnp.bfloat16)


def _rope_table(lp):
    lane = jnp.arange(LANE, dtype=jnp.int32)
    rotary = (lane >= MLA_NOPE) & (lane < MLA_NOPE + MLA_ROPE)
    first_half = rotary & (lane < MLA_NOPE + HALF_ROPE)
    pair = ((lane - MLA_NOPE) % HALF_ROPE).astype(jnp.float32)
    inv_freq = ROPE_BASE ** (-(2.0 * pair) / MLA_ROPE)
    pos = (jnp.arange(lp, dtype=jnp.int32) - PAD).astype(jnp.float32)
    ang = pos[:, None] * inv_freq[None, :]
    cos_t = jnp.where(lane < MLA_NOPE, 1.0, jnp.where(rotary, jnp.cos(ang), 0.0))
    sin_sw = jnp.where(rotary, jnp.where(first_half, -jnp.sin(ang), jnp.sin(ang)), 0.0)
    return jnp.concatenate([cos_t, sin_sw], axis=1)


def _pruning_tables(gate_end, g_q, g_k):
    b, nt = gate_end.shape[:2]
    per_tile = FLASH_TQ // FLASH_TK
    fox = gate_end[:, :, 0:per_tile, MISC_GATE:MISC_GATE + FOX_HEADS] * LOG2E
    fox = jnp.transpose(fox.reshape(b, nt * per_tile, FOX_HEADS), (0, 2, 1))
    table = jnp.concatenate([jnp.zeros((b, MLA_HEADS, nt * per_tile), jnp.float32), fox], axis=1)
    bound = 1.02 * FOX_DIM * (FOX_DIM ** -0.5 * LOG2E) * jnp.max(jnp.abs(g_q)) * jnp.max(jnp.abs(g_k))
    slack_fox = -(2.0 * bound + UNDERFLOW_LOG2 + 4.0)
    slack = jnp.concatenate([jnp.full((MLA_HEADS,), NEG, jnp.float32),
                             jnp.full((FOX_HEADS,), slack_fox, jnp.float32)])
    return table, slack


def _token_tile(lp):
    if lp % FLASH_TQ:
        raise ValueError(f"padded length {lp} is not a multiple of {FLASH_TQ}")
    return FLASH_TQ


def kernel(x, meta_tokens, g_mix, g_mlp, w_in_attn, g_cq, w_uq, g_ckv, w_ukv, g_q_mla, g_k_mla,
           g_q_fox, g_k_fox, b_forget, w_out_attn, w_in_conv, conv_w, w_out_conv, w_mlp_up,
           w_mlp_down):
    b, seq, d = x.shape
    assert d == D_MODEL and (PAD + N_META + seq) % BLOCK == 0
    lp = PAD + N_META + seq
    tm = _token_tile(lp)
    bf = jnp.bfloat16

    meta = meta_tokens.astype(x.dtype)
    h = None

    rope_tab = _rope_table(lp)
    tri = (jnp.arange(tm)[:, None] >= jnp.arange(tm)[None, :]).astype(bf)
    selq, selk = _gate_selectors()

    wo_attn, wo_conv, w_conv = w_out_attn.astype(bf), w_out_conv.astype(bf), w_in_conv.astype(bf)
    w_up, w_down = w_mlp_up.astype(bf), w_mlp_down.astype(bf)
    for layer in range(DEPTH):
        j = layer // 2
        gmix = g_mix[layer][None]
        if layer % 2 == 0:
            p = _attn_params(w_in_attn[j], g_cq[j], w_uq[j], g_ckv[j], w_ukv[j], g_q_mla[j],
                             g_k_mla[j], g_q_fox[j], g_k_fox[j], b_forget[j])
            stream = (x, meta) if layer == 0 else (h,)
            q, k, v, gate_end = _attn_in(stream, lp, gmix, p, rope_tab, tri, selq, selk, tm)
            y = _flash(*_pruning_tables(gate_end, g_q_fox[j], g_k_fox[j]), q, k, v)
            wo = wo_attn
        else:
            cw = jnp.zeros((8, d), jnp.float32).at[0:3].set(conv_w[j])
            y = _conv_in(h, gmix, w_conv, j, cw, tm)
            wo = wo_conv
        gmlp = g_mlp[layer][None]
        if layer == 0:
            h = _mix_out_mlp_first(x, meta, y, wo, j, gmlp, w_up, w_down, layer, tm)
        elif layer < DEPTH - 1:
            h = _mix_out_mlp(h.reshape(b * lp, d), y.reshape(b * lp, d), wo, j, gmlp,
                             w_up, w_down, layer, tm).reshape(b, lp, d)
        else:
            return _mix_out_mlp_last(h, y, wo, j, gmlp, w_up, w_down, layer, tm, seq)
```

```python
import functools

import numpy as np
import jax
import jax.numpy as jnp
from jax import lax
from jax.experimental import pallas as pl
from jax.experimental.pallas import tpu as pltpu

D_MODEL = 1024
DEPTH = 4
N_META = 16
BLOCK = 128
PAD = 2 * BLOCK - N_META
REAL_START = PAD + N_META
REAL_PARTS = 3
MLA_HEADS = 8
MLA_NOPE = 64
MLA_ROPE = 32
MLA_QK = MLA_NOPE + MLA_ROPE
MLA_V = 64
Q_LORA = 384
KV_LORA = 256
ROPE_BASE = 10000.0
FOX_HEADS = 8
FOX_DIM = 64
D_FF = 4 * D_MODEL
EPS = 1e-6
NEG = -1e30

LANE = 128
HEADS = MLA_HEADS + FOX_HEADS
HALF_ROPE = MLA_ROPE // 2
FEATURE_BASE = (0, FOX_DIM)
EXTRA_BASE = (FOX_DIM, 0)
N_SPLIT = 3
FLAG_FOX_OFF = 2 * N_SPLIT
FLAG_MLA = MLA_QK
PAD_KEY = NEG
LOG2E = 1.4426950408889634
MISC_GATE = 0
MISC_ROPE = MLA_NOPE

OFF_CQ = 0
OFF_MISC = OFF_CQ + Q_LORA
OFF_CKV = OFF_MISC + LANE
OFF_FQ = OFF_CKV + KV_LORA
OFF_FK = OFF_FQ + FOX_HEADS * FOX_DIM
OFF_FV = OFF_FK + FOX_HEADS * FOX_DIM
W_CAT = OFF_FV + FOX_HEADS * FOX_DIM

(V_GQ_MLA, V_GQ_MLA_SW, V_GK_MLA, V_GK_MLA_SW, V_ADD_Q_MLA, V_B_FORGET) = range(6)
V_GQ_FOX, V_GK_FOX, V_ADD_Q_FOX, V_ONES_K_FOX, V_ONES_V = 6, 8, 10, 12, 14
VEC_ROWS = 16
PAIR = 2 * LANE

FF_CHUNK = 1024
UNDERFLOW_LOG2 = 150.0
FLASH_TQ = 768
FLASH_TK = 256
VMEM_LIMIT = 56 * 1024 * 1024


def _const_spec(shape):
    nd = len(shape)
    return pl.BlockSpec(shape, lambda *_: (0,) * nd, pipeline_mode=pl.Buffered(1))


def _layer_spec(shape, layer):
    nd = len(shape)
    return pl.BlockSpec((1,) + shape, lambda *_: (layer,) + (0,) * nd,
                        pipeline_mode=pl.Buffered(1))


def _input_specs(tm):
    part = tm // REAL_PARTS
    assert REAL_START == part

    def part_spec(k):
        return pl.BlockSpec((1, part, D_MODEL),
                            lambda bi, i: (bi, jnp.maximum(REAL_PARTS * i + k - 1, 0), 0))

    return [part_spec(k) for k in range(REAL_PARTS)] + [_const_spec((N_META, D_MODEL))]


def _input_tile(i, x_parts, meta_ref):
    lead = jnp.concatenate([jnp.zeros((PAD, D_MODEL), jnp.float32), meta_ref[...]], axis=0)
    first = jnp.where(i == 0, lead, x_parts[0][0])
    return jnp.concatenate([first] + [r[0] for r in x_parts[1:]], axis=0)


def _rms(x, g, n):
    ms = jnp.sum(x * x, axis=-1, keepdims=True) * (1.0 / n)
    return x * lax.rsqrt(ms + EPS) * g


def _split3(x):
    hi = x.astype(jnp.bfloat16).astype(jnp.float32)
    r1 = x - hi
    mid = r1.astype(jnp.bfloat16).astype(jnp.float32)
    lo = r1 - mid
    packed = hi + pltpu.roll(mid, FOX_HEADS, 1) + pltpu.roll(lo, 2 * FOX_HEADS, 1)
    return packed.astype(jnp.bfloat16)


def _dot(a, b):
    return jnp.dot(a, b, preferred_element_type=jnp.float32)


def _attn_in_kernel(*refs, tm, from_x):
    n_stream = REAL_PARTS + 1 if from_x else 1
    stream = refs[:n_stream]
    (gmix_ref, wcat_ref, gcq_ref, wuq_ref, gckv_ref, wkn_ref, wv_ref, vec_ref, rope_ref, tri_ref,
     selq_ref, selk_ref, q_ref, k_ref, v_ref, gate_end_ref, carry_ref) = refs[n_stream:]
    i = pl.program_id(1)

    @pl.when(i == 0)
    def _():
        carry_ref[...] = jnp.zeros_like(carry_ref)

    if from_x:
        x = _input_tile(i, stream[:-1], stream[-1])
        hn = _rms(x, gmix_ref[...], D_MODEL).astype(jnp.bfloat16)
    else:
        hn = stream[0][0]

    def seg(lo, width):
        return _dot(hn, wcat_ref[:, lo:lo + width])

    def vec(r):
        return vec_ref[r:r + 1, :]

    cos_t = rope_ref[:, 0:LANE]
    sin_sw = rope_ref[:, LANE:2 * LANE]
    gc_q, gs_q = vec(V_GQ_MLA) * cos_t, vec(V_GQ_MLA_SW) * sin_sw
    gc_k, gs_k = vec(V_GK_MLA) * cos_t, vec(V_GK_MLA_SW) * sin_sw
    add_q_mla = vec(V_ADD_Q_MLA)

    lane = lax.broadcasted_iota(jnp.int32, (tm, LANE), 1)
    row = lax.broadcasted_iota(jnp.int32, (tm, LANE), 0)
    valid = (i * tm + row) >= PAD
    pad_key = jnp.where(valid, 0.0, PAD_KEY)
    add_k_mla = jnp.where(lane == FLAG_MLA, pad_key, 0.0)
    halves = (lane < FOX_DIM, lane >= FOX_DIM)
    add_k_fox = [vec(V_ONES_K_FOX + par)
                 + jnp.where(lane == EXTRA_BASE[par] + FLAG_FOX_OFF, pad_key, 0.0)
                 for par in range(2)]

    cq_misc = seg(OFF_CQ, Q_LORA + LANE)
    misc = cq_misc[:, Q_LORA:]
    kpe = jnp.where((lane >= MISC_ROPE) & (lane < MISC_ROPE + MLA_ROPE), misc, 0.0)
    k_rot = jnp.where(lane < MISC_ROPE + HALF_ROPE, pltpu.roll(kpe, LANE - HALF_ROPE, 1),
                      pltpu.roll(kpe, HALF_ROPE, 1)) * gs_k
    xl = misc + vec(V_B_FORGET)
    logf = jnp.minimum(xl, 0.0) - jnp.log1p(jnp.exp(-jnp.abs(xl)))
    logf = jnp.where(valid & (lane >= MISC_GATE) & (lane < MISC_GATE + FOX_HEADS), logf, 0.0)
    cs = _dot(tri_ref[...], _split3(logf))
    cs = (cs + pltpu.roll(cs, LANE - FOX_HEADS, 1)) + pltpu.roll(cs, LANE - 2 * FOX_HEADS, 1)
    cum = jnp.where(lane < FOX_HEADS, cs, 0.0) + carry_ref[0:1, :]
    carry_ref[0:1, :] = cum[tm - 1:tm, :]
    gate_end_ref[0, 0] = jnp.zeros((8, LANE), jnp.float32)
    for c in range(tm // FLASH_TK):
        gate_end_ref[0, 0, c:c + 1, :] = cum[(c + 1) * FLASH_TK - 1:(c + 1) * FLASH_TK, :]
    cum3 = _split3(cum * LOG2E)
    gate_q = _dot(cum3, selq_ref[...])
    gate_k = _dot(cum3, selk_ref[...])

    def inv_norm(sq, n):
        return lax.rsqrt(jnp.sum(sq, axis=-1, keepdims=True) + n * EPS)

    def fox_group(g):
        xq4 = seg(OFF_FQ + g * PAIR, PAIR)
        xk4 = seg(OFF_FK + g * PAIR, PAIR)
        xv4 = seg(OFF_FV + g * PAIR, PAIR)
        for e in range(4):
            hd, par = 4 * g + e, e % 2
            sl = slice((e // 2) * LANE, (e // 2 + 1) * LANE)
            gl = slice(hd * LANE, (hd + 1) * LANE)
            xq, xk = xq4[:, sl], xk4[:, sl]
            rq = inv_norm(jnp.where(halves[par], xq * xq, 0.0), FOX_DIM)
            rk = inv_norm(jnp.where(halves[par], xk * xk, 0.0), FOX_DIM)
            q_ref[0, MLA_HEADS + hd] = (xq * vec(V_GQ_FOX + par) * rq + gate_q[:, gl]
                                        + vec(V_ADD_Q_FOX + par)).astype(jnp.bfloat16)
            k_ref[0, MLA_HEADS + hd] = (xk * vec(V_GK_FOX + par) * rk + gate_k[:, gl]
                                        + add_k_fox[par]).astype(jnp.bfloat16)
            v_ref[0, MLA_HEADS + hd] = (jnp.where(halves[par], xv4[:, sl], 0.0)
                                        + vec(V_ONES_V + par)).astype(jnp.bfloat16)

    cqn = _rms(cq_misc[:, :Q_LORA], gcq_ref[...], Q_LORA).astype(jnp.bfloat16)
    ckvn = _rms(seg(OFF_CKV, KV_LORA), gckv_ref[...], KV_LORA).astype(jnp.bfloat16)
    def mla_pair(g):
        cols = slice(g * PAIR, (g + 1) * PAIR)
        cols_sw = slice(MLA_HEADS * LANE + g * PAIR, MLA_HEADS * LANE + (g + 1) * PAIR)
        xq2 = _dot(cqn, wuq_ref[:, cols])
        xq2_sw = _dot(cqn, wuq_ref[:, cols_sw])
        xk2 = _dot(ckvn, wkn_ref[:, cols])
        xv2 = _dot(ckvn, wv_ref[:, cols])
        for e in range(2):
            hd, sl = 2 * g + e, slice(e * LANE, (e + 1) * LANE)
            xq = xq2[:, sl]
            q_ref[0, hd] = ((xq * gc_q + xq2_sw[:, sl] * gs_q) * inv_norm(xq * xq, MLA_QK)
                            + add_q_mla).astype(jnp.bfloat16)
            xk = xk2[:, sl] + kpe
            k_ref[0, hd] = ((xk * gc_k + k_rot) * inv_norm(xk * xk, MLA_QK) + add_k_mla
                            ).astype(jnp.bfloat16)
            v_ref[0, hd] = (xv2[:, sl] + vec(V_ONES_V + e)).astype(jnp.bfloat16)

    fox_group(0)
    mla_pair(0)
    mla_pair(1)
    fox_group(1)
    mla_pair(2)
    mla_pair(3)


def _attn_in(stream, lp, gmix, p, rope_tab, tri, selq, selk, tm):
    from_x = len(stream) == 2
    b, d = stream[0].shape[0], D_MODEL
    nt = lp // tm
    kern = functools.partial(_attn_in_kernel, tm=tm, from_x=from_x)
    stream_specs = (_input_specs(tm) if from_x
                    else [pl.BlockSpec((1, tm, d), lambda bi, i: (bi, i, 0))])
    stream_args = [stream[0]] * REAL_PARTS + [stream[1]] if from_x else [stream[0]]
    qk_shape = jax.ShapeDtypeStruct((b, HEADS, lp, LANE), jnp.bfloat16)
    qk_spec = pl.BlockSpec((1, HEADS, tm, LANE), lambda bi, i: (bi, 0, i, 0))
    return pl.pallas_call(
        kern,
        grid=(b, nt),
        in_specs=stream_specs + [
            _const_spec((1, d)),
            _const_spec((d, W_CAT)),
            _const_spec((1, Q_LORA)),
            _const_spec((Q_LORA, 2 * MLA_HEADS * LANE)),
            _const_spec((1, KV_LORA)),
            _const_spec((KV_LORA, MLA_HEADS * LANE)),
            _const_spec((KV_LORA, MLA_HEADS * LANE)),
            _const_spec((VEC_ROWS, LANE)),
            pl.BlockSpec((tm, 2 * LANE), lambda bi, i: (i, 0)),
            _const_spec((tm, tm)),
            _const_spec((LANE, FOX_HEADS * LANE)),
            _const_spec((LANE, FOX_HEADS * LANE)),
        ],
        out_specs=[qk_spec, qk_spec, qk_spec,
                   pl.BlockSpec((1, 1, 8, LANE), lambda bi, i: (bi, i, 0, 0))],
        out_shape=[qk_shape, qk_shape, qk_shape,
                   jax.ShapeDtypeStruct((b, nt, 8, LANE), jnp.float32)],
        scratch_shapes=[pltpu.VMEM((8, LANE), jnp.float32)],
        compiler_params=pltpu.CompilerParams(
            dimension_semantics=("arbitrary", "arbitrary"), vmem_limit_bytes=VMEM_LIMIT),
        name="attn_in",
    )(*stream_args, gmix, p["wcat"], p["gcq"], p["wuq"], p["gckv"], p["wkn"], p["wv"], p["vec"],
      rope_tab, tri, selq, selk)


def _flash_kernel(gate_end_ref, slack_ref, q_ref, k_ref, v_ref, o_ref, m_ref, acc_ref, al_ref,
                  p_ref, *, tq, tk, nq):
    chunks = tq // tk
    bi, hp = pl.program_id(0), pl.program_id(1)

    def query_block(qi, carry):
        qbase = qi * tq

        def softmax(j, u, base, diagonal, first=False):
            r0 = u * tk if diagonal else 0
            rows = slice(r0, tq)
            start = pl.multiple_of(base + u * tk, tk)
            q_rows = pl.ds(pl.multiple_of(qbase + r0, tk), tq - r0)
            s = lax.dot_general(q_ref[0, j, q_rows, :], k_ref[0, j, pl.ds(start, tk), :],
                                (((1,), (1,)), ((), ())), preferred_element_type=jnp.float32)
            if diagonal:
                row = lax.broadcasted_iota(jnp.int32, (tq - r0, tk), 0)
                col = lax.broadcasted_iota(jnp.int32, (tq - r0, tk), 1)
                s = jnp.where(col <= row, s, NEG)
            if first:
                m_next = jnp.broadcast_to(jnp.max(s, axis=1, keepdims=True), (tq - r0, LANE))
            else:
                m_prev = m_ref[j, rows, :]
                m_next = jnp.maximum(m_prev, jnp.max(s, axis=1, keepdims=True))
                al_ref[j, u, rows, :] = jnp.exp2(m_prev - m_next)
            p = jnp.exp2(s - jnp.concatenate([m_next] * (tk // LANE), axis=1))
            p_ref[j, u, rows, :] = p.astype(jnp.bfloat16)
            m_ref[j, rows, :] = m_next

        def pv(j, u, base, diagonal, first=False):
            r0 = u * tk if diagonal else 0
            rows = slice(r0, tq)
            start = pl.multiple_of(base + u * tk, tk)
            new = _dot(p_ref[j, u, rows, :], v_ref[0, j, pl.ds(start, tk), :])
            if first:
                acc_ref[j, rows, :] = new
            else:
                acc_ref[j, rows, :] = acc_ref[j, rows, :] * al_ref[j, u, rows, :] + new

        def diagonal_block():
            for u in range(chunks):
                softmax(0, u, qbase, True, first=(u == 0))
                if u > 0:
                    pv(1, u - 1, qbase, True, first=(u == 1))
                softmax(1, u, qbase, True, first=(u == 0))
                pv(0, u, qbase, True, first=(u == 0))
            pv(1, chunks - 1, qbase, True, first=(chunks == 1))

        def block(kb, diagonal, start=0):
            base = kb * tq
            for u in range(start, chunks):
                softmax(0, u, base, diagonal)
                if u == start:
                    pv(1, chunks - 1, jnp.maximum(kb - 1, 0) * tq, False)
                else:
                    pv(1, u - 1, base, diagonal)
                softmax(1, u, base, diagonal)
                pv(0, u, base, diagonal)

        def chunks_needed(j):
            hd = 2 * hp + j
            gate_q = gate_end_ref[bi, hd, jnp.maximum(chunks * qi - 1, 0)]
            count = jnp.int32(0)
            for c in range(chunks * (nq - 1)):
                keep = (c < chunks * qi) & (gate_q - gate_end_ref[bi, hd, c] >= slack_ref[hd])
                count = count + keep.astype(jnp.int32)
            return count

        diagonal_block()
        al_ref[1, chunks - 1] = jnp.ones((tq, LANE), jnp.float32)
        p_ref[1, chunks - 1] = jnp.zeros((tq, tk), jnp.bfloat16)
        n_chunks = jnp.maximum(chunks_needed(0), chunks_needed(1))
        partial = n_chunks % chunks
        n_full = n_chunks // chunks
        first = qi - n_full
        for start in range(1, chunks):

            @pl.when(partial == chunks - start)
            def _():
                block(first - 1, False, start)

        odd = n_full & 1

        @pl.when(odd == 1)
        def _():
            block(first, False)

        def body(pair, c):
            kb = first + odd + 2 * pair
            block(kb, False)
            block(kb + 1, False)
            return c

        lax.fori_loop(0, lax.shift_right_logical(n_full, 1), body, 0)
        pv(1, chunks - 1, jnp.maximum(qi - 1, 0) * tq, False)

        o0 = acc_ref[0]
        o1 = acc_ref[1]
        o0 = o0 / pltpu.roll(o0, FOX_DIM, 1)
        o1 = o1 / pltpu.roll(o1, FOX_DIM, 1)
        lane = lax.broadcasted_iota(jnp.int32, (tq, LANE), 1)
        o = jnp.where(lane < MLA_V, o0, o1)
        o_ref[0, pl.ds(pl.multiple_of(qbase, tq), tq), :] = o.astype(jnp.bfloat16)
        return carry

    lax.fori_loop(0, nq, query_block, 0)


def _flash(gate_end, slack, q, k, v):
    b, _, lp, _ = q.shape
    tq, tk = FLASH_TQ, FLASH_TK
    kern = functools.partial(_flash_kernel, tq=tq, tk=tk, nq=lp // tq)
    qkv_spec = pl.BlockSpec((1, 2, lp, LANE), lambda bi, hp: (bi, hp, 0, 0))
    return pl.pallas_call(
        kern,
        grid=(b, HEADS // 2),
        in_specs=[pl.BlockSpec(memory_space=pltpu.SMEM), pl.BlockSpec(memory_space=pltpu.SMEM),
                  qkv_spec, qkv_spec, qkv_spec],
        out_specs=pl.BlockSpec((1, lp, LANE), lambda bi, hp: (bi, 0, hp)),
        out_shape=jax.ShapeDtypeStruct((b, lp, HEADS * MLA_V), jnp.bfloat16),
        scratch_shapes=[pltpu.VMEM((2, tq, LANE), jnp.float32)] * 2
        + [pltpu.VMEM((2, tq // tk, tq, LANE), jnp.float32),
           pltpu.VMEM((2, tq // tk, tq, tk), jnp.bfloat16)],
        compiler_params=pltpu.CompilerParams(
            dimension_semantics=("arbitrary", "arbitrary"), vmem_limit_bytes=VMEM_LIMIT),
        name="flash",
    )(gate_end, slack, q, k, v)


def _conv_in_kernel(hn_ref, win_ref, cw_ref, y_ref, gs_ref, *, tm):
    i = pl.program_id(1)

    @pl.when(i == 0)
    def _():
        gs_ref[0:8, :] = jnp.zeros((8, D_MODEL), jnp.float32)

    hn = hn_ref[0]
    gate_c = _dot(hn, win_ref[0, :, D_MODEL:2 * D_MODEL])
    u = _dot(hn, win_ref[0, :, 2 * D_MODEL:3 * D_MODEL])
    row = lax.broadcasted_iota(jnp.int32, (tm, D_MODEL), 0)
    g = jnp.where((i * tm + row) >= PAD, gate_c * u, 0.0)
    gs_ref[8:tm + 8, :] = g
    y = (cw_ref[0:1, :] * gs_ref[6:tm + 6, :] + cw_ref[1:2, :] * gs_ref[7:tm + 7, :]
         + cw_ref[2:3, :] * g)
    gs_ref[0:8, :] = gs_ref[tm:tm + 8, :]
    gate_b = _dot(hn, win_ref[0, :, 0:D_MODEL])
    y_ref[0] = (gate_b * y).astype(jnp.bfloat16)


def _conv_in(hn, win, layer, cw, tm):
    b, lp, d = hn.shape
    kern = functools.partial(_conv_in_kernel, tm=tm)
    return pl.pallas_call(
        kern,
        grid=(b, lp // tm),
        in_specs=[
            pl.BlockSpec((1, tm, d), lambda bi, i: (bi, i, 0)),
            _layer_spec((d, 3 * d), layer),
            _const_spec((8, d)),
        ],
        out_specs=pl.BlockSpec((1, tm, d), lambda bi, i: (bi, i, 0)),
        out_shape=jax.ShapeDtypeStruct((b, lp, d), jnp.bfloat16),
        scratch_shapes=[pltpu.VMEM((tm + 8, d), jnp.float32)],
        compiler_params=pltpu.CompilerParams(
            dimension_semantics=("arbitrary", "arbitrary"), vmem_limit_bytes=VMEM_LIMIT),
        name="conv_in",
    )(hn, win, cw)


def _mlp_tile(h, y, wo_ref, gmlp_ref, wup_ref, wdn_ref):
    h1 = h + _dot(y, wo_ref[0])
    n = _rms(h1, gmlp_ref[...], D_MODEL).astype(jnp.bfloat16)
    acc = h1
    for c in range(D_FF // FF_CHUNK):
        sl = slice(c * FF_CHUNK, (c + 1) * FF_CHUNK)
        a = jnp.maximum(_dot(n, wup_ref[0, :, sl]), 0.0)
        acc = acc + _dot((a * a).astype(jnp.bfloat16), wdn_ref[0, sl, :])
    return acc


def _mix_out_mlp_kernel(h_ref, y_ref, wo_ref, gmlp_ref, gnext_ref, wup_ref, wdn_ref,
                        out_ref, hn_ref):
    out = _mlp_tile(h_ref[...], y_ref[...], wo_ref, gmlp_ref, wup_ref, wdn_ref)
    out_ref[...] = out
    hn_ref[...] = _rms(out, gnext_ref[...], D_MODEL).astype(jnp.bfloat16)


def _mix_out_mlp(h, y, wo, wo_layer, gmlp, gnext, wup, wdn, layer, tm):
    r, d = h.shape
    return pl.pallas_call(
        _mix_out_mlp_kernel,
        grid=(r // tm,),
        in_specs=[
            pl.BlockSpec((tm, d), lambda i: (i, 0)),
            pl.BlockSpec((tm, d), lambda i: (i, 0)),
            _layer_spec((d, d), wo_layer),
            _const_spec((1, d)),
            _const_spec((1, d)),
            _layer_spec((d, D_FF), layer),
            _layer_spec((D_FF, d), layer),
        ],
        out_specs=[pl.BlockSpec((tm, d), lambda i: (i, 0))] * 2,
        out_shape=[jax.ShapeDtypeStruct((r, d), jnp.float32),
                   jax.ShapeDtypeStruct((r, d), jnp.bfloat16)],
        compiler_params=pltpu.CompilerParams(
            dimension_semantics=("arbitrary",), vmem_limit_bytes=VMEM_LIMIT),
        name="mix_out_mlp",
    )(h, y, wo, gmlp, gnext, wup, wdn)


def _mix_out_mlp_first_kernel(*refs):
    x_parts, meta_ref = refs[0:REAL_PARTS], refs[REAL_PARTS]
    y_ref, wo_ref, gmlp_ref, gnext_ref, wup_ref, wdn_ref, out_ref, hn_ref = refs[REAL_PARTS + 1:]
    h = _input_tile(pl.program_id(1), x_parts, meta_ref)
    out = _mlp_tile(h, y_ref[0], wo_ref, gmlp_ref, wup_ref, wdn_ref)
    out_ref[0] = out
    hn_ref[0] = _rms(out, gnext_ref[...], D_MODEL).astype(jnp.bfloat16)


def _mix_out_mlp_first(x, meta, y, wo, wo_layer, gmlp, gnext, wup, wdn, layer, tm):
    b, lp, d = y.shape
    tile = pl.BlockSpec((1, tm, d), lambda bi, i: (bi, i, 0))
    return pl.pallas_call(
        _mix_out_mlp_first_kernel,
        grid=(b, lp // tm),
        in_specs=_input_specs(tm) + [
            tile,
            _layer_spec((d, d), wo_layer),
            _const_spec((1, d)),
            _const_spec((1, d)),
            _layer_spec((d, D_FF), layer),
            _layer_spec((D_FF, d), layer),
        ],
        out_specs=[tile, tile],
        out_shape=[jax.ShapeDtypeStruct((b, lp, d), jnp.float32),
                   jax.ShapeDtypeStruct((b, lp, d), jnp.bfloat16)],
        compiler_params=pltpu.CompilerParams(
            dimension_semantics=("arbitrary", "arbitrary"), vmem_limit_bytes=VMEM_LIMIT),
        name="mix_out_mlp_first",
    )(*([x] * REAL_PARTS), meta, y, wo, gmlp, gnext, wup, wdn)


def _mix_out_mlp_last_kernel(*refs):
    h_parts, y_parts = refs[0:REAL_PARTS], refs[REAL_PARTS:2 * REAL_PARTS]
    wo_ref, gmlp_ref, wup_ref, wdn_ref, out_ref = refs[2 * REAL_PARTS:]
    h = jnp.concatenate([r[0] for r in h_parts], axis=0)
    y = jnp.concatenate([r[0] for r in y_parts], axis=0)
    out_ref[0] = _mlp_tile(h, y, wo_ref, gmlp_ref, wup_ref, wdn_ref)


def _mix_out_mlp_last(h, y, wo, wo_layer, gmlp, wup, wdn, layer, tm, seq):
    b, lp, d = h.shape
    part = tm // REAL_PARTS
    last_part = lp // part - 1

    def part_spec(k):
        return pl.BlockSpec(
            (1, part, d),
            lambda bi, i: (bi, jnp.minimum(REAL_START // part + REAL_PARTS * i + k, last_part), 0))

    parts = [part_spec(k) for k in range(REAL_PARTS)]
    return pl.pallas_call(
        _mix_out_mlp_last_kernel,
        grid=(b, pl.cdiv(seq, tm)),
        in_specs=parts + parts + [
            _layer_spec((d, d), wo_layer),
            _const_spec((1, d)),
            _layer_spec((d, D_FF), layer),
            _layer_spec((D_FF, d), layer),
        ],
        out_specs=pl.BlockSpec((1, tm, d), lambda bi, i: (bi, i, 0)),
        out_shape=jax.ShapeDtypeStruct((b, seq, d), jnp.float32),
        compiler_params=pltpu.CompilerParams(
            dimension_semantics=("arbitrary", "arbitrary"), vmem_limit_bytes=VMEM_LIMIT),
        name="mix_out_mlp_last",
    )(*([h] * REAL_PARTS + [y] * REAL_PARTS), wo, gmlp, wup, wdn)


def _pad_heads(w, heads, dim):
    k = w.shape[0]
    w = w.reshape(k, heads, dim)
    w = jnp.pad(w, ((0, 0), (0, 0), (0, LANE - dim)))
    return w.reshape(k, heads * LANE)


def _lane_vec(v, offset=0):
    return jnp.zeros((LANE,), jnp.float32).at[offset:offset + v.shape[0]].set(v)


def _attn_params(w_in, g_cq, w_uq, g_ckv, w_ukv, g_q_mla, g_k_mla, g_q_fox, g_k_fox, b_forget):
    bf = jnp.bfloat16
    o1 = Q_LORA
    o2 = o1 + KV_LORA
    o3 = o2 + MLA_ROPE
    o4 = o3 + FOX_HEADS * FOX_DIM
    o5 = o4 + FOX_HEADS * FOX_DIM
    o6 = o5 + FOX_HEADS * FOX_DIM
    misc = jnp.zeros((D_MODEL, LANE), jnp.float32)
    misc = misc.at[:, MISC_GATE:MISC_GATE + FOX_HEADS].set(w_in[:, o6:])
    misc = misc.at[:, MISC_ROPE:MISC_ROPE + MLA_ROPE].set(w_in[:, o2:o3])
    wcat = jnp.concatenate([w_in[:, :o1], misc, w_in[:, o1:o2], w_in[:, o3:o6]], axis=1).astype(bf)
    kv = w_ukv.reshape(KV_LORA, MLA_HEADS, MLA_NOPE + MLA_V)
    wkn = _pad_heads(kv[:, :, :MLA_NOPE].reshape(KV_LORA, -1), MLA_HEADS, MLA_NOPE).astype(bf)
    wv = jnp.pad(kv[:, :, MLA_NOPE:].reshape(KV_LORA, MLA_HEADS // 2, 2, MLA_V),
                 ((0, 0), (0, 0), (0, 0), (0, LANE - MLA_V)))
    wv = jnp.concatenate([wv[:, :, 0], jnp.roll(wv[:, :, 1], MLA_V, axis=-1)], axis=-1)
    wv = wv.reshape(KV_LORA, MLA_HEADS * LANE).astype(bf)
    lo, mid, hi = MLA_NOPE, MLA_NOPE + HALF_ROPE, MLA_NOPE + MLA_ROPE
    uq = w_uq.reshape(Q_LORA, MLA_HEADS, MLA_QK)
    uq_sw = jnp.zeros((Q_LORA, MLA_HEADS, LANE), jnp.float32)
    uq_sw = uq_sw.at[:, :, lo:mid].set(uq[:, :, mid:hi]).at[:, :, mid:hi].set(uq[:, :, lo:mid])
    wuq = jnp.concatenate([_pad_heads(w_uq, MLA_HEADS, MLA_QK),
                           uq_sw.reshape(Q_LORA, MLA_HEADS * LANE)], axis=1).astype(bf)

    def swapped(g):
        return jnp.zeros((LANE,), jnp.float32).at[lo:mid].set(g[mid:hi]).at[mid:hi].set(g[lo:mid])

    zero = jnp.zeros((LANE,), jnp.float32)
    rows = [zero] * VEC_ROWS
    rows[V_GQ_MLA] = _lane_vec(g_q_mla) * LOG2E
    rows[V_GQ_MLA_SW] = swapped(g_q_mla) * LOG2E
    rows[V_GK_MLA] = _lane_vec(g_k_mla) * MLA_QK ** 0.5
    rows[V_GK_MLA_SW] = swapped(g_k_mla) * MLA_QK ** 0.5
    for par in range(2):
        feat, extra = FEATURE_BASE[par], EXTRA_BASE[par]
        rows[V_GQ_FOX + par] = _lane_vec(g_q_fox, feat) * LOG2E
        rows[V_GK_FOX + par] = _lane_vec(g_k_fox, feat) * FOX_DIM ** 0.5
        rows[V_ADD_Q_FOX + par] = (zero.at[extra + N_SPLIT:extra + 2 * N_SPLIT].set(1.0)
                                   .at[extra + FLAG_FOX_OFF].set(1.0))
        rows[V_ONES_K_FOX + par] = zero.at[extra:extra + N_SPLIT].set(1.0)
        rows[V_ONES_V + par] = zero.at[extra:extra + FOX_DIM].set(1.0)
    rows[V_B_FORGET] = _lane_vec(b_forget, MISC_GATE)
    rows[V_ADD_Q_MLA] = zero.at[FLAG_MLA].set(1.0)
    vec = jnp.stack(rows)
    return dict(wcat=wcat, gcq=g_cq[None], wuq=wuq, gckv=g_ckv[None], wkn=wkn, wv=wv, vec=vec)


def _gate_selectors():
    selq = np.zeros((LANE, FOX_HEADS * LANE), np.float32)
    selk = np.zeros((LANE, FOX_HEADS * LANE), np.float32)
    for part in range(N_SPLIT):
        for hd in range(FOX_HEADS):
            extra = hd * LANE + EXTRA_BASE[hd % 2]
            selq[part * FOX_HEADS + hd, extra + part] = 1.0
            selk[part * FOX_HEADS + hd, extra + N_SPLIT + part] = -1.0
    return jnp.asarray(selq, jnp.bfloat16), jnp.asarray(selk, jnp.bfloat16)


def _rope_table(lp):
    lane = jnp.arange(LANE, dtype=jnp.int32)
    rotary = (lane >= MLA_NOPE) & (lane < MLA_NOPE + MLA_ROPE)
    first_half = rotary & (lane < MLA_NOPE + HALF_ROPE)
    pair = ((lane - MLA_NOPE) % HALF_ROPE).astype(jnp.float32)
    inv_freq = ROPE_BASE ** (-(2.0 * pair) / MLA_ROPE)
    pos = (jnp.arange(lp, dtype=jnp.int32) - PAD).astype(jnp.float32)
    ang = pos[:, None] * inv_freq[None, :]
    cos_t = jnp.where(lane < MLA_NOPE, 1.0, jnp.where(rotary, jnp.cos(ang), 0.0))
    sin_sw = jnp.where(rotary, jnp.where(first_half, -jnp.sin(ang), jnp.sin(ang)), 0.0)
    return jnp.concatenate([cos_t, sin_sw], axis=1)


def _pruning_tables(gate_end, g_q, g_k):
    b, nt = gate_end.shape[:2]
    per_tile = FLASH_TQ // FLASH_TK
    fox = gate_end[:, :, 0:per_tile, MISC_GATE:MISC_GATE + FOX_HEADS] * LOG2E
    fox = jnp.transpose(fox.reshape(b, nt * per_tile, FOX_HEADS), (0, 2, 1))
    table = jnp.concatenate([jnp.zeros((b, MLA_HEADS, nt * per_tile), jnp.float32), fox], axis=1)
    bound = 1.02 * FOX_DIM * (FOX_DIM ** -0.5 * LOG2E) * jnp.max(jnp.abs(g_q)) * jnp.max(jnp.abs(g_k))
    slack_fox = -(2.0 * bound + UNDERFLOW_LOG2 + 4.0)
    slack = jnp.concatenate([jnp.full((MLA_HEADS,), NEG, jnp.float32),
                             jnp.full((FOX_HEADS,), slack_fox, jnp.float32)])
    return table, slack


def _token_tile(lp):
    if lp % FLASH_TQ:
        raise ValueError(f"padded length {lp} is not a multiple of {FLASH_TQ}")
    return FLASH_TQ


def kernel(x, meta_tokens, g_mix, g_mlp, w_in_attn, g_cq, w_uq, g_ckv, w_ukv, g_q_mla, g_k_mla,
           g_q_fox, g_k_fox, b_forget, w_out_attn, w_in_conv, conv_w, w_out_conv, w_mlp_up,
           w_mlp_down):
    b, seq, d = x.shape
    assert d == D_MODEL and (PAD + N_META + seq) % BLOCK == 0
    lp = PAD + N_META + seq
    tm = _token_tile(lp)
    bf = jnp.bfloat16

    meta = meta_tokens.astype(x.dtype)
    h = hn = None

    rope_tab = _rope_table(lp)
    tri = (jnp.arange(tm)[:, None] >= jnp.arange(tm)[None, :]).astype(bf)
    selq, selk = _gate_selectors()

    wo_attn, wo_conv, w_conv = w_out_attn.astype(bf), w_out_conv.astype(bf), w_in_conv.astype(bf)
    w_up, w_down = w_mlp_up.astype(bf), w_mlp_down.astype(bf)
    for layer in range(DEPTH):
        j = layer // 2
        gmix = g_mix[layer][None]
        if layer % 2 == 0:
            p = _attn_params(w_in_attn[j], g_cq[j], w_uq[j], g_ckv[j], w_ukv[j], g_q_mla[j],
                             g_k_mla[j], g_q_fox[j], g_k_fox[j], b_forget[j])
            stream = (x, meta) if layer == 0 else (hn,)
            q, k, v, gate_end = _attn_in(stream, lp, gmix, p, rope_tab, tri, selq, selk, tm)
            y = _flash(*_pruning_tables(gate_end, g_q_fox[j], g_k_fox[j]), q, k, v)
            wo = wo_attn
        else:
            cw = jnp.zeros((8, d), jnp.float32).at[0:3].set(conv_w[j])
            y = _conv_in(hn, w_conv, j, cw, tm)
            wo = wo_conv
        gmlp = g_mlp[layer][None]
        gnext = g_mix[min(layer + 1, DEPTH - 1)][None]
        if layer == 0:
            h, hn = _mix_out_mlp_first(x, meta, y, wo, j, gmlp, gnext, w_up, w_down, layer, tm)
        elif layer < DEPTH - 1:
            h, hn = _mix_out_mlp(h.reshape(b * lp, d), y.reshape(b * lp, d), wo, j, gmlp, gnext,
                                 w_up, w_down, layer, tm)
            h, hn = h.reshape(b, lp, d), hn.reshape(b, lp, d)
        else:
            return _mix_out_mlp_last(h, y, wo, j, gmlp, w_up, w_down, layer, tm, seq)
```

```python
import functools

import numpy as np
import jax
import jax.numpy as jnp
from jax import lax
from jax.experimental import pallas as pl
from jax.experimental.pallas import tpu as pltpu

D_MODEL = 1024
DEPTH = 4
N_META = 16
BLOCK = 128
PAD = 2 * BLOCK - N_META
REAL_START = PAD + N_META
REAL_PARTS = 3
MLA_HEADS = 8
MLA_NOPE = 64
MLA_ROPE = 32
MLA_QK = MLA_NOPE + MLA_ROPE
MLA_V = 64
Q_LORA = 384
KV_LORA = 256
ROPE_BASE = 10000.0
FOX_HEADS = 8
FOX_DIM = 64
D_FF = 4 * D_MODEL
EPS = 1e-6
NEG = -1e30

LANE = 128
HEADS = MLA_HEADS + FOX_HEADS
HALF_ROPE = MLA_ROPE // 2
FEATURE_BASE = (0, FOX_DIM)
EXTRA_BASE = (FOX_DIM, 0)
N_SPLIT = 3
FLAG_FOX_OFF = 2 * N_SPLIT
FLAG_MLA = MLA_QK
PAD_KEY = NEG
LOG2E = 1.4426950408889634
MISC_GATE = 0
MISC_ROPE = MLA_NOPE

OFF_CQ = 0
OFF_MISC = OFF_CQ + Q_LORA
OFF_CKV = OFF_MISC + LANE
OFF_FQ = OFF_CKV + KV_LORA
OFF_FK = OFF_FQ + FOX_HEADS * FOX_DIM
OFF_FV = OFF_FK + FOX_HEADS * FOX_DIM
W_CAT = OFF_FV + FOX_HEADS * FOX_DIM

(V_GQ_MLA, V_GQ_MLA_SW, V_GK_MLA, V_GK_MLA_SW, V_ADD_Q_MLA, V_B_FORGET) = range(6)
V_GQ_FOX, V_GK_FOX, V_ADD_Q_FOX, V_ONES_K_FOX, V_ONES_V = 6, 8, 10, 12, 14
VEC_ROWS = 16
PAIR = 2 * LANE

FF_CHUNK = 1024
UNDERFLOW_LOG2 = 150.0
FLASH_TQ = 768
FLASH_TK = 256
BLOCKS_PER_BODY = 4
VMEM_LIMIT = 56 * 1024 * 1024


def _const_spec(shape):
    nd = len(shape)
    return pl.BlockSpec(shape, lambda *_: (0,) * nd, pipeline_mode=pl.Buffered(1))


def _layer_spec(shape, layer):
    nd = len(shape)
    return pl.BlockSpec((1,) + shape, lambda *_: (layer,) + (0,) * nd,
                        pipeline_mode=pl.Buffered(1))


def _input_specs(tm):
    part = tm // REAL_PARTS
    assert REAL_START == part

    def part_spec(k):
        return pl.BlockSpec((1, part, D_MODEL),
                            lambda bi, i: (bi, jnp.maximum(REAL_PARTS * i + k - 1, 0), 0))

    return [part_spec(k) for k in range(REAL_PARTS)] + [_const_spec((N_META, D_MODEL))]


def _input_tile(i, x_parts, meta_ref):
    lead = jnp.concatenate([jnp.zeros((PAD, D_MODEL), jnp.float32), meta_ref[...]], axis=0)
    first = jnp.where(i == 0, lead, x_parts[0][0])
    return jnp.concatenate([first] + [r[0] for r in x_parts[1:]], axis=0)


def _rms(x, g, n):
    ms = jnp.sum(x * x, axis=-1, keepdims=True) * (1.0 / n)
    return x * lax.rsqrt(ms + EPS) * g


def _split3(x):
    hi = x.astype(jnp.bfloat16).astype(jnp.float32)
    r1 = x - hi
    mid = r1.astype(jnp.bfloat16).astype(jnp.float32)
    lo = r1 - mid
    packed = hi + pltpu.roll(mid, FOX_HEADS, 1) + pltpu.roll(lo, 2 * FOX_HEADS, 1)
    return packed.astype(jnp.bfloat16)


def _dot(a, b):
    return jnp.dot(a, b, preferred_element_type=jnp.float32)


def _attn_in_kernel(*refs, tm, from_x):
    n_stream = REAL_PARTS + 1 if from_x else 1
    stream = refs[:n_stream]
    (gmix_ref, wcat_ref, gcq_ref, wuq_ref, gckv_ref, wkn_ref, wv_ref, vec_ref, rope_ref, tri_ref,
     selq_ref, selk_ref, q_ref, k_ref, v_ref, gate_end_ref, carry_ref) = refs[n_stream:]
    i = pl.program_id(1)

    @pl.when(i == 0)
    def _():
        carry_ref[...] = jnp.zeros_like(carry_ref)

    if from_x:
        x = _input_tile(i, stream[:-1], stream[-1])
        hn = _rms(x, gmix_ref[...], D_MODEL).astype(jnp.bfloat16)
    else:
        hn = stream[0][0]

    def seg(lo, width):
        return _dot(hn, wcat_ref[:, lo:lo + width])

    def vec(r):
        return vec_ref[r:r + 1, :]

    cos_t = rope_ref[:, 0:LANE]
    sin_sw = rope_ref[:, LANE:2 * LANE]
    gc_q, gs_q = vec(V_GQ_MLA) * cos_t, vec(V_GQ_MLA_SW) * sin_sw
    gc_k, gs_k = vec(V_GK_MLA) * cos_t, vec(V_GK_MLA_SW) * sin_sw
    add_q_mla = vec(V_ADD_Q_MLA)

    lane = lax.broadcasted_iota(jnp.int32, (tm, LANE), 1)
    row = lax.broadcasted_iota(jnp.int32, (tm, LANE), 0)
    valid = (i * tm + row) >= PAD
    pad_key = jnp.where(valid, 0.0, PAD_KEY)
    add_k_mla = jnp.where(lane == FLAG_MLA, pad_key, 0.0)
    halves = (lane < FOX_DIM, lane >= FOX_DIM)
    add_k_fox = [vec(V_ONES_K_FOX + par)
                 + jnp.where(lane == EXTRA_BASE[par] + FLAG_FOX_OFF, pad_key, 0.0)
                 for par in range(2)]

    cq_misc = seg(OFF_CQ, Q_LORA + LANE)
    misc = cq_misc[:, Q_LORA:]
    kpe = jnp.where((lane >= MISC_ROPE) & (lane < MISC_ROPE + MLA_ROPE), misc, 0.0)
    k_rot = jnp.where(lane < MISC_ROPE + HALF_ROPE, pltpu.roll(kpe, LANE - HALF_ROPE, 1),
                      pltpu.roll(kpe, HALF_ROPE, 1)) * gs_k
    xl = misc + vec(V_B_FORGET)
    logf = jnp.minimum(xl, 0.0) - jnp.log1p(jnp.exp(-jnp.abs(xl)))
    logf = jnp.where(valid & (lane >= MISC_GATE) & (lane < MISC_GATE + FOX_HEADS), logf, 0.0)
    cs = _dot(tri_ref[...], _split3(logf))
    cs = (cs + pltpu.roll(cs, LANE - FOX_HEADS, 1)) + pltpu.roll(cs, LANE - 2 * FOX_HEADS, 1)
    cum = jnp.where(lane < FOX_HEADS, cs, 0.0) + carry_ref[0:1, :]
    carry_ref[0:1, :] = cum[tm - 1:tm, :]
    gate_end_ref[0, 0] = jnp.zeros((8, LANE), jnp.float32)
    for c in range(tm // FLASH_TK):
        gate_end_ref[0, 0, c:c + 1, :] = cum[(c + 1) * FLASH_TK - 1:(c + 1) * FLASH_TK, :]
    cum3 = _split3(cum * LOG2E)
    gate_q = _dot(cum3, selq_ref[...])
    gate_k = _dot(cum3, selk_ref[...])

    def inv_norm(sq, n):
        return lax.rsqrt(jnp.sum(sq, axis=-1, keepdims=True) + n * EPS)

    def fox_group(g):
        xq4 = seg(OFF_FQ + g * PAIR, PAIR)
        xk4 = seg(OFF_FK + g * PAIR, PAIR)
        xv4 = seg(OFF_FV + g * PAIR, PAIR)
        for e in range(4):
            hd, par = 4 * g + e, e % 2
            sl = slice((e // 2) * LANE, (e // 2 + 1) * LANE)
            gl = slice(hd * LANE, (hd + 1) * LANE)
            xq, xk = xq4[:, sl], xk4[:, sl]
            rq = inv_norm(jnp.where(halves[par], xq * xq, 0.0), FOX_DIM)
            rk = inv_norm(jnp.where(halves[par], xk * xk, 0.0), FOX_DIM)
            q_ref[0, MLA_HEADS + hd] = (xq * vec(V_GQ_FOX + par) * rq + gate_q[:, gl]
                                        + vec(V_ADD_Q_FOX + par)).astype(jnp.bfloat16)
            k_ref[0, MLA_HEADS + hd] = (xk * vec(V_GK_FOX + par) * rk + gate_k[:, gl]
                                        + add_k_fox[par]).astype(jnp.bfloat16)
            v_ref[0, MLA_HEADS + hd] = (jnp.where(halves[par], xv4[:, sl], 0.0)
                                        + vec(V_ONES_V + par)).astype(jnp.bfloat16)

    cqn = _rms(cq_misc[:, :Q_LORA], gcq_ref[...], Q_LORA).astype(jnp.bfloat16)
    ckvn = _rms(seg(OFF_CKV, KV_LORA), gckv_ref[...], KV_LORA).astype(jnp.bfloat16)
    def mla_pair(g):
        cols = slice(g * PAIR, (g + 1) * PAIR)
        cols_sw = slice(MLA_HEADS * LANE + g * PAIR, MLA_HEADS * LANE + (g + 1) * PAIR)
        xq2 = _dot(cqn, wuq_ref[:, cols])
        xq2_sw = _dot(cqn, wuq_ref[:, cols_sw])
        xk2 = _dot(ckvn, wkn_ref[:, cols])
        xv2 = _dot(ckvn, wv_ref[:, cols])
        for e in range(2):
            hd, sl = 2 * g + e, slice(e * LANE, (e + 1) * LANE)
            xq = xq2[:, sl]
            q_ref[0, hd] = ((xq * gc_q + xq2_sw[:, sl] * gs_q) * inv_norm(xq * xq, MLA_QK)
                            + add_q_mla).astype(jnp.bfloat16)
            xk = xk2[:, sl] + kpe
            k_ref[0, hd] = ((xk * gc_k + k_rot) * inv_norm(xk * xk, MLA_QK) + add_k_mla
                            ).astype(jnp.bfloat16)
            v_ref[0, hd] = (xv2[:, sl] + vec(V_ONES_V + e)).astype(jnp.bfloat16)

    fox_group(0)
    mla_pair(0)
    mla_pair(1)
    fox_group(1)
    mla_pair(2)
    mla_pair(3)


def _attn_in(stream, lp, gmix, p, rope_tab, tri, selq, selk, tm):
    from_x = len(stream) == 2
    b, d = stream[0].shape[0], D_MODEL
    nt = lp // tm
    kern = functools.partial(_attn_in_kernel, tm=tm, from_x=from_x)
    stream_specs = (_input_specs(tm) if from_x
                    else [pl.BlockSpec((1, tm, d), lambda bi, i: (bi, i, 0))])
    stream_args = [stream[0]] * REAL_PARTS + [stream[1]] if from_x else [stream[0]]
    qk_shape = jax.ShapeDtypeStruct((b, HEADS, lp, LANE), jnp.bfloat16)
    qk_spec = pl.BlockSpec((1, HEADS, tm, LANE), lambda bi, i: (bi, 0, i, 0))
    return pl.pallas_call(
        kern,
        grid=(b, nt),
        in_specs=stream_specs + [
            _const_spec((1, d)),
            _const_spec((d, W_CAT)),
            _const_spec((1, Q_LORA)),
            _const_spec((Q_LORA, 2 * MLA_HEADS * LANE)),
            _const_spec((1, KV_LORA)),
            _const_spec((KV_LORA, MLA_HEADS * LANE)),
            _const_spec((KV_LORA, MLA_HEADS * LANE)),
            _const_spec((VEC_ROWS, LANE)),
            pl.BlockSpec((tm, 2 * LANE), lambda bi, i: (i, 0)),
            _const_spec((tm, tm)),
            _const_spec((LANE, FOX_HEADS * LANE)),
            _const_spec((LANE, FOX_HEADS * LANE)),
        ],
        out_specs=[qk_spec, qk_spec, qk_spec,
                   pl.BlockSpec((1, 1, 8, LANE), lambda bi, i: (bi, i, 0, 0))],
        out_shape=[qk_shape, qk_shape, qk_shape,
                   jax.ShapeDtypeStruct((b, nt, 8, LANE), jnp.float32)],
        scratch_shapes=[pltpu.VMEM((8, LANE), jnp.float32)],
        compiler_params=pltpu.CompilerParams(
            dimension_semantics=("arbitrary", "arbitrary"), vmem_limit_bytes=VMEM_LIMIT),
        name="attn_in",
    )(*stream_args, gmix, p["wcat"], p["gcq"], p["wuq"], p["gckv"], p["wkn"], p["wv"], p["vec"],
      rope_tab, tri, selq, selk)


def _flash_kernel(gate_end_ref, slack_ref, q_ref, k_ref, v_ref, o_ref, m_ref, acc_ref, al_ref,
                  p_ref, *, tq, tk, nq):
    chunks = tq // tk
    bi, hp = pl.program_id(0), pl.program_id(1)

    def query_block(qi, carry):
        qbase = qi * tq

        def softmax(j, u, base, diagonal, first=False):
            r0 = u * tk if diagonal else 0
            rows = slice(r0, tq)
            start = pl.multiple_of(base + u * tk, tk)
            q_rows = pl.ds(pl.multiple_of(qbase + r0, tk), tq - r0)
            s = lax.dot_general(q_ref[0, j, q_rows, :], k_ref[0, j, pl.ds(start, tk), :],
                                (((1,), (1,)), ((), ())), preferred_element_type=jnp.float32)
            if diagonal:
                row = lax.broadcasted_iota(jnp.int32, (tq - r0, tk), 0)
                col = lax.broadcasted_iota(jnp.int32, (tq - r0, tk), 1)
                s = jnp.where(col <= row, s, NEG)
            if first:
                m_next = jnp.broadcast_to(jnp.max(s, axis=1, keepdims=True), (tq - r0, LANE))
            else:
                m_prev = m_ref[j, rows, :]
                m_next = jnp.maximum(m_prev, jnp.max(s, axis=1, keepdims=True))
                al_ref[j, u, rows, :] = jnp.exp2(m_prev - m_next)
            p = jnp.exp2(s - jnp.concatenate([m_next] * (tk // LANE), axis=1))
            p_ref[j, u, rows, :] = p.astype(jnp.bfloat16)
            m_ref[j, rows, :] = m_next

        def pv(j, u, base, diagonal, first=False):
            r0 = u * tk if diagonal else 0
            rows = slice(r0, tq)
            start = pl.multiple_of(base + u * tk, tk)
            new = _dot(p_ref[j, u, rows, :], v_ref[0, j, pl.ds(start, tk), :])
            if first:
                acc_ref[j, rows, :] = new
            else:
                acc_ref[j, rows, :] = acc_ref[j, rows, :] * al_ref[j, u, rows, :] + new

        def diagonal_block():
            for u in range(chunks):
                softmax(0, u, qbase, True, first=(u == 0))
                if u > 0:
                    pv(1, u - 1, qbase, True, first=(u == 1))
                softmax(1, u, qbase, True, first=(u == 0))
                pv(0, u, qbase, True, first=(u == 0))
            pv(1, chunks - 1, qbase, True, first=(chunks == 1))

        def block(kb, diagonal, start=0):
            base = kb * tq
            for u in range(start, chunks):
                softmax(0, u, base, diagonal)
                if u == start:
                    pv(1, chunks - 1, jnp.maximum(kb - 1, 0) * tq, False)
                else:
                    pv(1, u - 1, base, diagonal)
                softmax(1, u, base, diagonal)
                pv(0, u, base, diagonal)

        def chunks_needed(j):
            hd = 2 * hp + j
            gate_q = gate_end_ref[bi, hd, jnp.maximum(chunks * qi - 1, 0)]
            count = jnp.int32(0)
            for c in range(chunks * (nq - 1)):
                keep = (c < chunks * qi) & (gate_q - gate_end_ref[bi, hd, c] >= slack_ref[hd])
                count = count + keep.astype(jnp.int32)
            return count

        diagonal_block()
        al_ref[1, chunks - 1] = jnp.ones((tq, LANE), jnp.float32)
        p_ref[1, chunks - 1] = jnp.zeros((tq, tk), jnp.bfloat16)
        n_chunks = jnp.maximum(chunks_needed(0), chunks_needed(1))
        partial = n_chunks % chunks
        n_full = n_chunks // chunks
        first = qi - n_full
        for start in range(1, chunks):

            @pl.when(partial == chunks - start)
            def _():
                block(first - 1, False, start)

        rem = n_full & (BLOCKS_PER_BODY - 1)

        @pl.when((rem & 1) == 1)
        def _():
            block(first, False)

        @pl.when((rem & 2) == 2)
        def _():
            kb = first + (rem & 1)
            block(kb, False)
            block(kb + 1, False)

        def body(group, c):
            kb = first + rem + BLOCKS_PER_BODY * group
            for d in range(BLOCKS_PER_BODY):
                block(kb + d, False)
            return c

        lax.fori_loop(0, n_full // BLOCKS_PER_BODY, body, 0)
        pv(1, chunks - 1, jnp.maximum(qi - 1, 0) * tq, False)

        o0 = acc_ref[0]
        o1 = acc_ref[1]
        o0 = o0 / pltpu.roll(o0, FOX_DIM, 1)
        o1 = o1 / pltpu.roll(o1, FOX_DIM, 1)
        lane = lax.broadcasted_iota(jnp.int32, (tq, LANE), 1)
        o = jnp.where(lane < MLA_V, o0, o1)
        o_ref[0, pl.ds(pl.multiple_of(qbase, tq), tq), :] = o.astype(jnp.bfloat16)
        return carry

    lax.fori_loop(0, nq, query_block, 0)


def _flash(gate_end, slack, q, k, v):
    b, _, lp, _ = q.shape
    tq, tk = FLASH_TQ, FLASH_TK
    kern = functools.partial(_flash_kernel, tq=tq, tk=tk, nq=lp // tq)
    qkv_spec = pl.BlockSpec((1, 2, lp, LANE), lambda bi, hp: (bi, hp, 0, 0))
    return pl.pallas_call(
        kern,
        grid=(b, HEADS // 2),
        in_specs=[pl.BlockSpec(memory_space=pltpu.SMEM), pl.BlockSpec(memory_space=pltpu.SMEM),
                  qkv_spec, qkv_spec, qkv_spec],
        out_specs=pl.BlockSpec((1, lp, LANE), lambda bi, hp: (bi, 0, hp)),
        out_shape=jax.ShapeDtypeStruct((b, lp, HEADS * MLA_V), jnp.bfloat16),
        scratch_shapes=[pltpu.VMEM((2, tq, LANE), jnp.float32)] * 2
        + [pltpu.VMEM((2, tq // tk, tq, LANE), jnp.float32),
           pltpu.VMEM((2, tq // tk, tq, tk), jnp.bfloat16)],
        compiler_params=pltpu.CompilerParams(
            dimension_semantics=("arbitrary", "arbitrary"), vmem_limit_bytes=VMEM_LIMIT),
        name="flash",
    )(gate_end, slack, q, k, v)


def _conv_in_kernel(hn_ref, win_ref, cw_ref, y_ref, gs_ref, *, tm):
    i = pl.program_id(1)

    @pl.when(i == 0)
    def _():
        gs_ref[0:8, :] = jnp.zeros((8, D_MODEL), jnp.float32)

    hn = hn_ref[0]
    gate_c = _dot(hn, win_ref[0, :, D_MODEL:2 * D_MODEL])
    u = _dot(hn, win_ref[0, :, 2 * D_MODEL:3 * D_MODEL])
    row = lax.broadcasted_iota(jnp.int32, (tm, D_MODEL), 0)
    g = jnp.where((i * tm + row) >= PAD, gate_c * u, 0.0)
    gs_ref[8:tm + 8, :] = g
    y = (cw_ref[0:1, :] * gs_ref[6:tm + 6, :] + cw_ref[1:2, :] * gs_ref[7:tm + 7, :]
         + cw_ref[2:3, :] * g)
    gs_ref[0:8, :] = gs_ref[tm:tm + 8, :]
    gate_b = _dot(hn, win_ref[0, :, 0:D_MODEL])
    y_ref[0] = (gate_b * y).astype(jnp.bfloat16)


def _conv_in(hn, win, layer, cw, tm):
    b, lp, d = hn.shape
    kern = functools.partial(_conv_in_kernel, tm=tm)
    return pl.pallas_call(
        kern,
        grid=(b, lp // tm),
        in_specs=[
            pl.BlockSpec((1, tm, d), lambda bi, i: (bi, i, 0)),
            _layer_spec((d, 3 * d), layer),
            _const_spec((8, d)),
        ],
        out_specs=pl.BlockSpec((1, tm, d), lambda bi, i: (bi, i, 0)),
        out_shape=jax.ShapeDtypeStruct((b, lp, d), jnp.bfloat16),
        scratch_shapes=[pltpu.VMEM((tm + 8, d), jnp.float32)],
        compiler_params=pltpu.CompilerParams(
            dimension_semantics=("arbitrary", "arbitrary"), vmem_limit_bytes=VMEM_LIMIT),
        name="conv_in",
    )(hn, win, cw)


def _mlp_tile(h, y, wo_ref, gmlp_ref, wup_ref, wdn_ref):
    h1 = h + _dot(y, wo_ref[0])
    n = _rms(h1, gmlp_ref[...], D_MODEL).astype(jnp.bfloat16)
    acc = h1
    for c in range(D_FF // FF_CHUNK):
        sl = slice(c * FF_CHUNK, (c + 1) * FF_CHUNK)
        a = jnp.maximum(_dot(n, wup_ref[0, :, sl]), 0.0)
        acc = acc + _dot((a * a).astype(jnp.bfloat16), wdn_ref[0, sl, :])
    return acc


def _mix_out_mlp_kernel(h_ref, y_ref, wo_ref, gmlp_ref, gnext_ref, wup_ref, wdn_ref,
                        out_ref, hn_ref):
    out = _mlp_tile(h_ref[...], y_ref[...], wo_ref, gmlp_ref, wup_ref, wdn_ref)
    out_ref[...] = out
    hn_ref[...] = _rms(out, gnext_ref[...], D_MODEL).astype(jnp.bfloat16)


def _mix_out_mlp(h, y, wo, wo_layer, gmlp, gnext, wup, wdn, layer, tm):
    r, d = h.shape
    return pl.pallas_call(
        _mix_out_mlp_kernel,
        grid=(r // tm,),
        in_specs=[
            pl.BlockSpec((tm, d), lambda i: (i, 0)),
            pl.BlockSpec((tm, d), lambda i: (i, 0)),
            _layer_spec((d, d), wo_layer),
            _const_spec((1, d)),
            _const_spec((1, d)),
            _layer_spec((d, D_FF), layer),
            _layer_spec((D_FF, d), layer),
        ],
        out_specs=[pl.BlockSpec((tm, d), lambda i: (i, 0))] * 2,
        out_shape=[jax.ShapeDtypeStruct((r, d), jnp.float32),
                   jax.ShapeDtypeStruct((r, d), jnp.bfloat16)],
        compiler_params=pltpu.CompilerParams(
            dimension_semantics=("arbitrary",), vmem_limit_bytes=VMEM_LIMIT),
        name="mix_out_mlp",
    )(h, y, wo, gmlp, gnext, wup, wdn)


def _mix_out_mlp_first_kernel(*refs):
    x_parts, meta_ref = refs[0:REAL_PARTS], refs[REAL_PARTS]
    y_ref, wo_ref, gmlp_ref, gnext_ref, wup_ref, wdn_ref, out_ref, hn_ref = refs[REAL_PARTS + 1:]
    h = _input_tile(pl.program_id(1), x_parts, meta_ref)
    out = _mlp_tile(h, y_ref[0], wo_ref, gmlp_ref, wup_ref, wdn_ref)
    out_ref[0] = out
    hn_ref[0] = _rms(out, gnext_ref[...], D_MODEL).astype(jnp.bfloat16)


def _mix_out_mlp_first(x, meta, y, wo, wo_layer, gmlp, gnext, wup, wdn, layer, tm):
    b, lp, d = y.shape
    tile = pl.BlockSpec((1, tm, d), lambda bi, i: (bi, i, 0))
    return pl.pallas_call(
        _mix_out_mlp_first_kernel,
        grid=(b, lp // tm),
        in_specs=_input_specs(tm) + [
            tile,
            _layer_spec((d, d), wo_layer),
            _const_spec((1, d)),
            _const_spec((1, d)),
            _layer_spec((d, D_FF), layer),
            _layer_spec((D_FF, d), layer),
        ],
        out_specs=[tile, tile],
        out_shape=[jax.ShapeDtypeStruct((b, lp, d), jnp.float32),
                   jax.ShapeDtypeStruct((b, lp, d), jnp.bfloat16)],
        compiler_params=pltpu.CompilerParams(
            dimension_semantics=("arbitrary", "arbitrary"), vmem_limit_bytes=VMEM_LIMIT),
        name="mix_out_mlp_first",
    )(*([x] * REAL_PARTS), meta, y, wo, gmlp, gnext, wup, wdn)


def _mix_out_mlp_last_kernel(*refs):
    h_parts, y_parts = refs[0:REAL_PARTS], refs[REAL_PARTS:2 * REAL_PARTS]
    wo_ref, gmlp_ref, wup_ref, wdn_ref, out_ref = refs[2 * REAL_PARTS:]
    h = jnp.concatenate([r[0] for r in h_parts], axis=0)
    y = jnp.concatenate([r[0] for r in y_parts], axis=0)
    out_ref[0] = _mlp_tile(h, y, wo_ref, gmlp_ref, wup_ref, wdn_ref)


def _mix_out_mlp_last(h, y, wo, wo_layer, gmlp, wup, wdn, layer, tm, seq):
    b, lp, d = h.shape
    part = tm // REAL_PARTS
    last_part = lp // part - 1

    def part_spec(k):
        return pl.BlockSpec(
            (1, part, d),
            lambda bi, i: (bi, jnp.minimum(REAL_START // part + REAL_PARTS * i + k, last_part), 0))

    parts = [part_spec(k) for k in range(REAL_PARTS)]
    return pl.pallas_call(
        _mix_out_mlp_last_kernel,
        grid=(b, pl.cdiv(seq, tm)),
        in_specs=parts + parts + [
            _layer_spec((d, d), wo_layer),
            _const_spec((1, d)),
            _layer_spec((d, D_FF), layer),
            _layer_spec((D_FF, d), layer),
        ],
        out_specs=pl.BlockSpec((1, tm, d), lambda bi, i: (bi, i, 0)),
        out_shape=jax.ShapeDtypeStruct((b, seq, d), jnp.float32),
        compiler_params=pltpu.CompilerParams(
            dimension_semantics=("arbitrary", "arbitrary"), vmem_limit_bytes=VMEM_LIMIT),
        name="mix_out_mlp_last",
    )(*([h] * REAL_PARTS + [y] * REAL_PARTS), wo, gmlp, wup, wdn)


def _pad_heads(w, heads, dim):
    k = w.shape[0]
    w = w.reshape(k, heads, dim)
    w = jnp.pad(w, ((0, 0), (0, 0), (0, LANE - dim)))
    return w.reshape(k, heads * LANE)


def _lane_vec(v, offset=0):
    return jnp.zeros((LANE,), jnp.float32).at[offset:offset + v.shape[0]].set(v)


def _attn_params(w_in, g_cq, w_uq, g_ckv, w_ukv, g_q_mla, g_k_mla, g_q_fox, g_k_fox, b_forget):
    bf = jnp.bfloat16
    o1 = Q_LORA
    o2 = o1 + KV_LORA
    o3 = o2 + MLA_ROPE
    o4 = o3 + FOX_HEADS * FOX_DIM
    o5 = o4 + FOX_HEADS * FOX_DIM
    o6 = o5 + FOX_HEADS * FOX_DIM
    misc = jnp.zeros((D_MODEL, LANE), jnp.float32)
    misc = misc.at[:, MISC_GATE:MISC_GATE + FOX_HEADS].set(w_in[:, o6:])
    misc = misc.at[:, MISC_ROPE:MISC_ROPE + MLA_ROPE].set(w_in[:, o2:o3])
    wcat = jnp.concatenate([w_in[:, :o1], misc, w_in[:, o1:o2], w_in[:, o3:o6]], axis=1).astype(bf)
    kv = w_ukv.reshape(KV_LORA, MLA_HEADS, MLA_NOPE + MLA_V)
    wkn = _pad_heads(kv[:, :, :MLA_NOPE].reshape(KV_LORA, -1), MLA_HEADS, MLA_NOPE).astype(bf)
    wv = jnp.pad(kv[:, :, MLA_NOPE:].reshape(KV_LORA, MLA_HEADS // 2, 2, MLA_V),
                 ((0, 0), (0, 0), (0, 0), (0, LANE - MLA_V)))
    wv = jnp.concatenate([wv[:, :, 0], jnp.roll(wv[:, :, 1], MLA_V, axis=-1)], axis=-1)
    wv = wv.reshape(KV_LORA, MLA_HEADS * LANE).astype(bf)
    lo, mid, hi = MLA_NOPE, MLA_NOPE + HALF_ROPE, MLA_NOPE + MLA_ROPE
    uq = w_uq.reshape(Q_LORA, MLA_HEADS, MLA_QK)
    uq_sw = jnp.zeros((Q_LORA, MLA_HEADS, LANE), jnp.float32)
    uq_sw = uq_sw.at[:, :, lo:mid].set(uq[:, :, mid:hi]).at[:, :, mid:hi].set(uq[:, :, lo:mid])
    wuq = jnp.concatenate([_pad_heads(w_uq, MLA_HEADS, MLA_QK),
                           uq_sw.reshape(Q_LORA, MLA_HEADS * LANE)], axis=1).astype(bf)

    def swapped(g):
        return jnp.zeros((LANE,), jnp.float32).at[lo:mid].set(g[mid:hi]).at[mid:hi].set(g[lo:mid])

    zero = jnp.zeros((LANE,), jnp.float32)
    rows = [zero] * VEC_ROWS
    rows[V_GQ_MLA] = _lane_vec(g_q_mla) * LOG2E
    rows[V_GQ_MLA_SW] = swapped(g_q_mla) * LOG2E
    rows[V_GK_MLA] = _lane_vec(g_k_mla) * MLA_QK ** 0.5
    rows[V_GK_MLA_SW] = swapped(g_k_mla) * MLA_QK ** 0.5
    for par in range(2):
        feat, extra = FEATURE_BASE[par], EXTRA_BASE[par]
        rows[V_GQ_FOX + par] = _lane_vec(g_q_fox, feat) * LOG2E
        rows[V_GK_FOX + par] = _lane_vec(g_k_fox, feat) * FOX_DIM ** 0.5
        rows[V_ADD_Q_FOX + par] = (zero.at[extra + N_SPLIT:extra + 2 * N_SPLIT].set(1.0)
                                   .at[extra + FLAG_FOX_OFF].set(1.0))
        rows[V_ONES_K_FOX + par] = zero.at[extra:extra + N_SPLIT].set(1.0)
        rows[V_ONES_V + par] = zero.at[extra:extra + FOX_DIM].set(1.0)
    rows[V_B_FORGET] = _lane_vec(b_forget, MISC_GATE)
    rows[V_ADD_Q_MLA] = zero.at[FLAG_MLA].set(1.0)
    vec = jnp.stack(rows)
    return dict(wcat=wcat, gcq=g_cq[None], wuq=wuq, gckv=g_ckv[None], wkn=wkn, wv=wv, vec=vec)


def _gate_selectors():
    selq = np.zeros((LANE, FOX_HEADS * LANE), np.float32)
    selk = np.zeros((LANE, FOX_HEADS * LANE), np.float32)
    for part in range(N_SPLIT):
        for hd in range(FOX_HEADS):
            extra = hd * LANE + EXTRA_BASE[hd % 2]
            selq[part * FOX_HEADS + hd, extra + part] = 1.0
            selk[part * FOX_HEADS + hd, extra + N_SPLIT + part] = -1.0
    return jnp.asarray(selq, jnp.bfloat16), jnp.asarray(selk, jnp.bfloat16)


def _rope_table(lp):
    lane = jnp.arange(LANE, dtype=jnp.int32)
    rotary = (lane >= MLA_NOPE) & (lane < MLA_NOPE + MLA_ROPE)
    first_half = rotary & (lane < MLA_NOPE + HALF_ROPE)
    pair = ((lane - MLA_NOPE) % HALF_ROPE).astype(jnp.float32)
    inv_freq = ROPE_BASE ** (-(2.0 * pair) / MLA_ROPE)
    pos = (jnp.arange(lp, dtype=jnp.int32) - PAD).astype(jnp.float32)
    ang = pos[:, None] * inv_freq[None, :]
    cos_t = jnp.where(lane < MLA_NOPE, 1.0, jnp.where(rotary, jnp.cos(ang), 0.0))
    sin_sw = jnp.where(rotary, jnp.where(first_half, -jnp.sin(ang), jnp.sin(ang)), 0.0)
    return jnp.concatenate([cos_t, sin_sw], axis=1)


def _pruning_tables(gate_end, g_q, g_k):
    b, nt = gate_end.shape[:2]
    per_tile = FLASH_TQ // FLASH_TK
    fox = gate_end[:, :, 0:per_tile, MISC_GATE:MISC_GATE + FOX_HEADS] * LOG2E
    fox = jnp.transpose(fox.reshape(b, nt * per_tile, FOX_HEADS), (0, 2, 1))
    table = jnp.concatenate([jnp.zeros((b, MLA_HEADS, nt * per_tile), jnp.float32), fox], axis=1)
    bound = 1.02 * FOX_DIM * (FOX_DIM ** -0.5 * LOG2E) * jnp.max(jnp.abs(g_q)) * jnp.max(jnp.abs(g_k))
    slack_fox = -(2.0 * bound + UNDERFLOW_LOG2 + 4.0)
    slack = jnp.concatenate([jnp.full((MLA_HEADS,), NEG, jnp.float32),
                             jnp.full((FOX_HEADS,), slack_fox, jnp.float32)])
    return table, slack


def _token_tile(lp):
    if lp % FLASH_TQ:
        raise ValueError(f"padded length {lp} is not a multiple of {FLASH_TQ}")
    return FLASH_TQ


def kernel(x, meta_tokens, g_mix, g_mlp, w_in_attn, g_cq, w_uq, g_ckv, w_ukv, g_q_mla, g_k_mla,
           g_q_fox, g_k_fox, b_forget, w_out_attn, w_in_conv, conv_w, w_out_conv, w_mlp_up,
           w_mlp_down):
    b, seq, d = x.shape
    assert d == D_MODEL and (PAD + N_META + seq) % BLOCK == 0
    lp = PAD + N_META + seq
    tm = _token_tile(lp)
    bf = jnp.bfloat16

    meta = meta_tokens.astype(x.dtype)
    h = hn = None

    rope_tab = _rope_table(lp)
    tri = (jnp.arange(tm)[:, None] >= jnp.arange(tm)[None, :]).astype(bf)
    selq, selk = _gate_selectors()

    wo_attn, wo_conv, w_conv = w_out_attn.astype(bf), w_out_conv.astype(bf), w_in_conv.astype(bf)
    w_up, w_down = w_mlp_up.astype(bf), w_mlp_down.astype(bf)
    for layer in range(DEPTH):
        j = layer // 2
        gmix = g_mix[layer][None]
        if layer % 2 == 0:
            p = _attn_params(w_in_attn[j], g_cq[j], w_uq[j], g_ckv[j], w_ukv[j], g_q_mla[j],
                             g_k_mla[j], g_q_fox[j], g_k_fox[j], b_forget[j])
            stream = (x, meta) if layer == 0 else (hn,)
            q, k, v, gate_end = _attn_in(stream, lp, gmix, p, rope_tab, tri, selq, selk, tm)
            y = _flash(*_pruning_tables(gate_end, g_q_fox[j], g_k_fox[j]), q, k, v)
            wo = wo_attn
        else:
            cw = jnp.zeros((8, d), jnp.float32).at[0:3].set(conv_w[j])
            y = _conv_in(hn, w_conv, j, cw, tm)
            wo = wo_conv
        gmlp = g_mlp[layer][None]
        gnext = g_mix[min(layer + 1, DEPTH - 1)][None]
        if layer == 0:
            h, hn = _mix_out_mlp_first(x, meta, y, wo, j, gmlp, gnext, w_up, w_down, layer, tm)
        elif layer < DEPTH - 1:
            h, hn = _mix_out_mlp(h.reshape(b * lp, d), y.reshape(b * lp, d), wo, j, gmlp, gnext,
                                 w_up, w_down, layer, tm)
            h, hn = h.reshape(b, lp, d), hn.reshape(b, lp, d)
        else:
            return _mix_out_mlp_last(h, y, wo, j, gmlp, w_up, w_down, layer, tm, seq)
```

```python
import functools

import numpy as np
import jax
import jax.numpy as jnp
from jax import lax
from jax.experimental import pallas as pl
from jax.experimental.pallas import tpu as pltpu

D_MODEL = 1024
DEPTH = 4
N_META = 16
BLOCK = 128
PAD = 2 * BLOCK - N_META
REAL_START = PAD + N_META
REAL_PARTS = 3
MLA_HEADS = 8
MLA_NOPE = 64
MLA_ROPE = 32
MLA_QK = MLA_NOPE + MLA_ROPE
MLA_V = 64
Q_LORA = 384
KV_LORA = 256
ROPE_BASE = 10000.0
FOX_HEADS = 8
FOX_DIM = 64
D_FF = 4 * D_MODEL
EPS = 1e-6
NEG = -1e30

LANE = 128
HEADS = MLA_HEADS + FOX_HEADS
HALF_ROPE = MLA_ROPE // 2
FEATURE_BASE = (0, FOX_DIM)
EXTRA_BASE = (FOX_DIM, 0)
N_SPLIT = 3
FLAG_FOX_OFF = 2 * N_SPLIT
FLAG_MLA = MLA_QK
PAD_KEY = NEG
LOG2E = 1.4426950408889634
MISC_GATE = 0
MISC_ROPE = MLA_NOPE

OFF_CQ = 0
OFF_MISC = OFF_CQ + Q_LORA
OFF_CKV = OFF_MISC + LANE
OFF_FQ = OFF_CKV + KV_LORA
OFF_FK = OFF_FQ + FOX_HEADS * FOX_DIM
OFF_FV = OFF_FK + FOX_HEADS * FOX_DIM
W_CAT = OFF_FV + FOX_HEADS * FOX_DIM

(V_GQ_MLA, V_GQ_MLA_SW, V_GK_MLA, V_GK_MLA_SW, V_ADD_Q_MLA, V_B_FORGET) = range(6)
V_GQ_FOX, V_GK_FOX, V_ADD_Q_FOX, V_ONES_K_FOX, V_ONES_V = 6, 8, 10, 12, 14
VEC_ROWS = 16
PAIR = 2 * LANE

FF_CHUNK = 1024
UNDERFLOW_LOG2 = 150.0
FLASH_TQ = 768
FLASH_TK = 256
BLOCKS_PER_BODY = 4
VMEM_LIMIT = 56 * 1024 * 1024


def _const_spec(shape):
    nd = len(shape)
    return pl.BlockSpec(shape, lambda *_: (0,) * nd, pipeline_mode=pl.Buffered(1))


def _layer_spec(shape, layer):
    nd = len(shape)
    return pl.BlockSpec((1,) + shape, lambda *_: (layer,) + (0,) * nd,
                        pipeline_mode=pl.Buffered(1))


def _input_specs(tm):
    part = tm // REAL_PARTS
    assert REAL_START == part

    def part_spec(k):
        return pl.BlockSpec((1, part, D_MODEL),
                            lambda bi, i: (bi, jnp.maximum(REAL_PARTS * i + k - 1, 0), 0))

    return [part_spec(k) for k in range(REAL_PARTS)] + [_const_spec((N_META, D_MODEL))]


def _input_tile(i, x_parts, meta_ref):
    lead = jnp.concatenate([jnp.zeros((PAD, D_MODEL), jnp.float32), meta_ref[...]], axis=0)
    first = jnp.where(i == 0, lead, x_parts[0][0])
    return jnp.concatenate([first] + [r[0] for r in x_parts[1:]], axis=0)


def _rms(x, g, n):
    ms = jnp.sum(x * x, axis=-1, keepdims=True) * (1.0 / n)
    return x * lax.rsqrt(ms + EPS) * g


def _split3(x):
    hi = x.astype(jnp.bfloat16).astype(jnp.float32)
    r1 = x - hi
    mid = r1.astype(jnp.bfloat16).astype(jnp.float32)
    lo = r1 - mid
    packed = hi + pltpu.roll(mid, FOX_HEADS, 1) + pltpu.roll(lo, 2 * FOX_HEADS, 1)
    return packed.astype(jnp.bfloat16)


def _dot(a, b):
    return jnp.dot(a, b, preferred_element_type=jnp.float32)


def _attn_in_kernel(*refs, tm, from_x):
    n_stream = REAL_PARTS + 1 if from_x else 1
    stream = refs[:n_stream]
    (gmix_ref, wcat_ref, gcq_ref, wuq_ref, gckv_ref, wkn_ref, wv_ref, vec_ref, rope_ref, tri_ref,
     selq_ref, selk_ref, q_ref, k_ref, v_ref, gate_end_ref, carry_ref) = refs[n_stream:]
    i = pl.program_id(1)

    @pl.when(i == 0)
    def _():
        carry_ref[...] = jnp.zeros_like(carry_ref)

    if from_x:
        x = _input_tile(i, stream[:-1], stream[-1])
        hn = _rms(x, gmix_ref[...], D_MODEL).astype(jnp.bfloat16)
    else:
        hn = stream[0][0]

    def seg(lo, width):
        return _dot(hn, wcat_ref[:, lo:lo + width])

    def vec(r):
        return vec_ref[r:r + 1, :]

    cos_t = rope_ref[:, 0:LANE]
    sin_sw = rope_ref[:, LANE:2 * LANE]
    gc_q, gs_q = vec(V_GQ_MLA) * cos_t, vec(V_GQ_MLA_SW) * sin_sw
    gc_k, gs_k = vec(V_GK_MLA) * cos_t, vec(V_GK_MLA_SW) * sin_sw
    add_q_mla = vec(V_ADD_Q_MLA)

    lane = lax.broadcasted_iota(jnp.int32, (tm, LANE), 1)
    row = lax.broadcasted_iota(jnp.int32, (tm, LANE), 0)
    valid = (i * tm + row) >= PAD
    pad_key = jnp.where(valid, 0.0, PAD_KEY)
    add_k_mla = jnp.where(lane == FLAG_MLA, pad_key, 0.0)
    halves = (lane < FOX_DIM, lane >= FOX_DIM)
    add_k_fox = [vec(V_ONES_K_FOX + par)
                 + jnp.where(lane == EXTRA_BASE[par] + FLAG_FOX_OFF, pad_key, 0.0)
                 for par in range(2)]

    cq_misc = seg(OFF_CQ, Q_LORA + LANE)
    misc = cq_misc[:, Q_LORA:]
    kpe = jnp.where((lane >= MISC_ROPE) & (lane < MISC_ROPE + MLA_ROPE), misc, 0.0)
    k_rot = jnp.where(lane < MISC_ROPE + HALF_ROPE, pltpu.roll(kpe, LANE - HALF_ROPE, 1),
                      pltpu.roll(kpe, HALF_ROPE, 1)) * gs_k
    xl = misc + vec(V_B_FORGET)
    logf = jnp.minimum(xl, 0.0) - jnp.log1p(jnp.exp(-jnp.abs(xl)))
    logf = jnp.where(valid & (lane >= MISC_GATE) & (lane < MISC_GATE + FOX_HEADS), logf, 0.0)
    cs = _dot(tri_ref[...], _split3(logf))
    cs = (cs + pltpu.roll(cs, LANE - FOX_HEADS, 1)) + pltpu.roll(cs, LANE - 2 * FOX_HEADS, 1)
    cum = jnp.where(lane < FOX_HEADS, cs, 0.0) + carry_ref[0:1, :]
    carry_ref[0:1, :] = cum[tm - 1:tm, :]
    gate_end_ref[0, 0] = jnp.zeros((8, LANE), jnp.float32)
    for c in range(tm // FLASH_TK):
        gate_end_ref[0, 0, c:c + 1, :] = cum[(c + 1) * FLASH_TK - 1:(c + 1) * FLASH_TK, :]
    cum3 = _split3(cum * LOG2E)
    gate_q = _dot(cum3, selq_ref[...])
    gate_k = _dot(cum3, selk_ref[...])

    def inv_norm(sq, n):
        return lax.rsqrt(jnp.sum(sq, axis=-1, keepdims=True) + n * EPS)

    def fox_group(g):
        xq4 = seg(OFF_FQ + g * PAIR, PAIR)
        xk4 = seg(OFF_FK + g * PAIR, PAIR)
        xv4 = seg(OFF_FV + g * PAIR, PAIR)
        for e in range(4):
            hd, par = 4 * g + e, e % 2
            sl = slice((e // 2) * LANE, (e // 2 + 1) * LANE)
            gl = slice(hd * LANE, (hd + 1) * LANE)
            xq, xk = xq4[:, sl], xk4[:, sl]
            rq = inv_norm(jnp.where(halves[par], xq * xq, 0.0), FOX_DIM)
            rk = inv_norm(jnp.where(halves[par], xk * xk, 0.0), FOX_DIM)
            q_ref[0, MLA_HEADS + hd] = (xq * vec(V_GQ_FOX + par) * rq + gate_q[:, gl]
                                        + vec(V_ADD_Q_FOX + par)).astype(jnp.bfloat16)
            k_ref[0, MLA_HEADS + hd] = (xk * vec(V_GK_FOX + par) * rk + gate_k[:, gl]
                                        + add_k_fox[par]).astype(jnp.bfloat16)
            v_ref[0, MLA_HEADS + hd] = jnp.where(halves[par], xv4[:, sl], 1.0
                                                 ).astype(jnp.bfloat16)

    cqn = _rms(cq_misc[:, :Q_LORA], gcq_ref[...], Q_LORA).astype(jnp.bfloat16)
    ckvn = _rms(seg(OFF_CKV, KV_LORA), gckv_ref[...], KV_LORA).astype(jnp.bfloat16)
    def mla_pair(g):
        cols = slice(g * PAIR, (g + 1) * PAIR)
        cols_sw = slice(MLA_HEADS * LANE + g * PAIR, MLA_HEADS * LANE + (g + 1) * PAIR)
        xq2 = _dot(cqn, wuq_ref[:, cols])
        xq2_sw = _dot(cqn, wuq_ref[:, cols_sw])
        xk2 = _dot(ckvn, wkn_ref[:, cols])
        xv2 = _dot(ckvn, wv_ref[:, cols])
        for e in range(2):
            hd, sl = 2 * g + e, slice(e * LANE, (e + 1) * LANE)
            xq = xq2[:, sl]
            q_ref[0, hd] = ((xq * gc_q + xq2_sw[:, sl] * gs_q) * inv_norm(xq * xq, MLA_QK)
                            + add_q_mla).astype(jnp.bfloat16)
            xk = xk2[:, sl] + kpe
            k_ref[0, hd] = ((xk * gc_k + k_rot) * inv_norm(xk * xk, MLA_QK) + add_k_mla
                            ).astype(jnp.bfloat16)
            v_ref[0, hd] = (xv2[:, sl] + vec(V_ONES_V + e)).astype(jnp.bfloat16)

    fox_group(0)
    mla_pair(0)
    mla_pair(1)
    fox_group(1)
    mla_pair(2)
    mla_pair(3)


def _attn_in(stream, lp, gmix, p, rope_tab, tri, selq, selk, tm):
    from_x = len(stream) == 2
    b, d = stream[0].shape[0], D_MODEL
    nt = lp // tm
    kern = functools.partial(_attn_in_kernel, tm=tm, from_x=from_x)
    stream_specs = (_input_specs(tm) if from_x
                    else [pl.BlockSpec((1, tm, d), lambda bi, i: (bi, i, 0))])
    stream_args = [stream[0]] * REAL_PARTS + [stream[1]] if from_x else [stream[0]]
    qk_shape = jax.ShapeDtypeStruct((b, HEADS, lp, LANE), jnp.bfloat16)
    qk_spec = pl.BlockSpec((1, HEADS, tm, LANE), lambda bi, i: (bi, 0, i, 0))
    return pl.pallas_call(
        kern,
        grid=(b, nt),
        in_specs=stream_specs + [
            _const_spec((1, d)),
            _const_spec((d, W_CAT)),
            _const_spec((1, Q_LORA)),
            _const_spec((Q_LORA, 2 * MLA_HEADS * LANE)),
            _const_spec((1, KV_LORA)),
            _const_spec((KV_LORA, MLA_HEADS * LANE)),
            _const_spec((KV_LORA, MLA_HEADS * LANE)),
            _const_spec((VEC_ROWS, LANE)),
            pl.BlockSpec((tm, 2 * LANE), lambda bi, i: (i, 0)),
            _const_spec((tm, tm)),
            _const_spec((LANE, FOX_HEADS * LANE)),
            _const_spec((LANE, FOX_HEADS * LANE)),
        ],
        out_specs=[qk_spec, qk_spec, qk_spec,
                   pl.BlockSpec((1, 1, 8, LANE), lambda bi, i: (bi, i, 0, 0))],
        out_shape=[qk_shape, qk_shape, qk_shape,
                   jax.ShapeDtypeStruct((b, nt, 8, LANE), jnp.float32)],
        scratch_shapes=[pltpu.VMEM((8, LANE), jnp.float32)],
        compiler_params=pltpu.CompilerParams(
            dimension_semantics=("arbitrary", "arbitrary"), vmem_limit_bytes=VMEM_LIMIT),
        name="attn_in",
    )(*stream_args, gmix, p["wcat"], p["gcq"], p["wuq"], p["gckv"], p["wkn"], p["wv"], p["vec"],
      rope_tab, tri, selq, selk)


def _flash_kernel(gate_end_ref, slack_ref, q_ref, k_ref, v_ref, o_ref, m_ref, acc_ref, al_ref,
                  p_ref, *, tq, tk, nq):
    chunks = tq // tk
    bi, hp = pl.program_id(0), pl.program_id(1)

    def query_block(qi, carry):
        qbase = qi * tq

        def softmax(j, u, base, diagonal, first=False):
            r0 = u * tk if diagonal else 0
            rows = slice(r0, tq)
            start = pl.multiple_of(base + u * tk, tk)
            q_rows = pl.ds(pl.multiple_of(qbase + r0, tk), tq - r0)
            s = lax.dot_general(q_ref[0, j, q_rows, :], k_ref[0, j, pl.ds(start, tk), :],
                                (((1,), (1,)), ((), ())), preferred_element_type=jnp.float32)
            if diagonal:
                row = lax.broadcasted_iota(jnp.int32, (tq - r0, tk), 0)
                col = lax.broadcasted_iota(jnp.int32, (tq - r0, tk), 1)
                s = jnp.where(col <= row, s, NEG)
            if first:
                m_next = jnp.broadcast_to(jnp.max(s, axis=1, keepdims=True), (tq - r0, LANE))
            else:
                m_prev = m_ref[j, rows, :]
                m_next = jnp.maximum(m_prev, jnp.max(s, axis=1, keepdims=True))
                al_ref[j, u, rows, :] = jnp.exp2(m_prev - m_next)
            p = jnp.exp2(s - jnp.concatenate([m_next] * (tk // LANE), axis=1))
            p_ref[j, u, rows, :] = p.astype(jnp.bfloat16)
            m_ref[j, rows, :] = m_next

        def pv(j, u, base, diagonal, first=False):
            r0 = u * tk if diagonal else 0
            rows = slice(r0, tq)
            start = pl.multiple_of(base + u * tk, tk)
            new = _dot(p_ref[j, u, rows, :], v_ref[0, j, pl.ds(start, tk), :])
            if first:
                acc_ref[j, rows, :] = new
            else:
                acc_ref[j, rows, :] = acc_ref[j, rows, :] * al_ref[j, u, rows, :] + new

        def diagonal_block():
            for u in range(chunks):
                softmax(0, u, qbase, True, first=(u == 0))
                if u > 0:
                    pv(1, u - 1, qbase, True, first=(u == 1))
                softmax(1, u, qbase, True, first=(u == 0))
                pv(0, u, qbase, True, first=(u == 0))
            pv(1, chunks - 1, qbase, True, first=(chunks == 1))

        def block(kb, diagonal, start=0):
            base = kb * tq
            for u in range(start, chunks):
                softmax(0, u, base, diagonal)
                if u == start:
                    pv(1, chunks - 1, jnp.maximum(kb - 1, 0) * tq, False)
                else:
                    pv(1, u - 1, base, diagonal)
                softmax(1, u, base, diagonal)
                pv(0, u, base, diagonal)

        def chunks_needed(j):
            hd = 2 * hp + j
            gate_q = gate_end_ref[bi, hd, jnp.maximum(chunks * qi - 1, 0)]
            count = jnp.int32(0)
            for c in range(chunks * (nq - 1)):
                keep = (c < chunks * qi) & (gate_q - gate_end_ref[bi, hd, c] >= slack_ref[hd])
                count = count + keep.astype(jnp.int32)
            return count

        diagonal_block()
        al_ref[1, chunks - 1] = jnp.ones((tq, LANE), jnp.float32)
        p_ref[1, chunks - 1] = jnp.zeros((tq, tk), jnp.bfloat16)
        n_chunks = jnp.maximum(chunks_needed(0), chunks_needed(1))
        partial = n_chunks % chunks
        n_full = n_chunks // chunks
        first = qi - n_full
        for start in range(1, chunks):

            @pl.when(partial == chunks - start)
            def _():
                block(first - 1, False, start)

        rem = n_full & (BLOCKS_PER_BODY - 1)

        @pl.when((rem & 1) == 1)
        def _():
            block(first, False)

        @pl.when((rem & 2) == 2)
        def _():
            kb = first + (rem & 1)
            block(kb, False)
            block(kb + 1, False)

        def body(group, c):
            kb = first + rem + BLOCKS_PER_BODY * group
            for d in range(BLOCKS_PER_BODY):
                block(kb + d, False)
            return c

        lax.fori_loop(0, n_full // BLOCKS_PER_BODY, body, 0)
        pv(1, chunks - 1, jnp.maximum(qi - 1, 0) * tq, False)

        o0 = acc_ref[0]
        o1 = acc_ref[1]
        o0 = o0 / pltpu.roll(o0, FOX_DIM, 1)
        o1 = o1 / pltpu.roll(o1, FOX_DIM, 1)
        lane = lax.broadcasted_iota(jnp.int32, (tq, LANE), 1)
        o = jnp.where(lane < MLA_V, o0, o1)
        o_ref[0, pl.ds(pl.multiple_of(qbase, tq), tq), :] = o.astype(jnp.bfloat16)
        return carry

    lax.fori_loop(0, nq, query_block, 0)


def _flash(gate_end, slack, q, k, v):
    b, _, lp, _ = q.shape
    tq, tk = FLASH_TQ, FLASH_TK
    kern = functools.partial(_flash_kernel, tq=tq, tk=tk, nq=lp // tq)
    qkv_spec = pl.BlockSpec((1, 2, lp, LANE), lambda bi, hp: (bi, hp, 0, 0))
    return pl.pallas_call(
        kern,
        grid=(b, HEADS // 2),
        in_specs=[pl.BlockSpec(memory_space=pltpu.SMEM), pl.BlockSpec(memory_space=pltpu.SMEM),
                  qkv_spec, qkv_spec, qkv_spec],
        out_specs=pl.BlockSpec((1, lp, LANE), lambda bi, hp: (bi, 0, hp)),
        out_shape=jax.ShapeDtypeStruct((b, lp, HEADS * MLA_V), jnp.bfloat16),
        scratch_shapes=[pltpu.VMEM((2, tq, LANE), jnp.float32)] * 2
        + [pltpu.VMEM((2, tq // tk, tq, LANE), jnp.float32),
           pltpu.VMEM((2, tq // tk, tq, tk), jnp.bfloat16)],
        compiler_params=pltpu.CompilerParams(
            dimension_semantics=("arbitrary", "arbitrary"), vmem_limit_bytes=VMEM_LIMIT),
        name="flash",
    )(gate_end, slack, q, k, v)


def _conv_in_kernel(hn_ref, win_ref, cw_ref, y_ref, gs_ref, *, tm):
    i = pl.program_id(1)

    @pl.when(i == 0)
    def _():
        gs_ref[0:8, :] = jnp.zeros((8, D_MODEL), jnp.float32)

    hn = hn_ref[0]
    gate_c = _dot(hn, win_ref[0, :, D_MODEL:2 * D_MODEL])
    u = _dot(hn, win_ref[0, :, 2 * D_MODEL:3 * D_MODEL])
    row = lax.broadcasted_iota(jnp.int32, (tm, D_MODEL), 0)
    g = jnp.where((i * tm + row) >= PAD, gate_c * u, 0.0)
    gs_ref[8:tm + 8, :] = g
    y = (cw_ref[0:1, :] * gs_ref[6:tm + 6, :] + cw_ref[1:2, :] * gs_ref[7:tm + 7, :]
         + cw_ref[2:3, :] * g)
    gs_ref[0:8, :] = gs_ref[tm:tm + 8, :]
    gate_b = _dot(hn, win_ref[0, :, 0:D_MODEL])
    y_ref[0] = (gate_b * y).astype(jnp.bfloat16)


def _conv_in(hn, win, layer, cw, tm):
    b, lp, d = hn.shape
    kern = functools.partial(_conv_in_kernel, tm=tm)
    return pl.pallas_call(
        kern,
        grid=(b, lp // tm),
        in_specs=[
            pl.BlockSpec((1, tm, d), lambda bi, i: (bi, i, 0)),
            _layer_spec((d, 3 * d), layer),
            _const_spec((8, d)),
        ],
        out_specs=pl.BlockSpec((1, tm, d), lambda bi, i: (bi, i, 0)),
        out_shape=jax.ShapeDtypeStruct((b, lp, d), jnp.bfloat16),
        scratch_shapes=[pltpu.VMEM((tm + 8, d), jnp.float32)],
        compiler_params=pltpu.CompilerParams(
            dimension_semantics=("arbitrary", "arbitrary"), vmem_limit_bytes=VMEM_LIMIT),
        name="conv_in",
    )(hn, win, cw)


def _mlp_tile(h, y, wo_ref, gmlp_ref, wup_ref, wdn_ref):
    h1 = h + _dot(y, wo_ref[0])
    n = _rms(h1, gmlp_ref[...], D_MODEL).astype(jnp.bfloat16)
    acc = h1
    for c in range(D_FF // FF_CHUNK):
        sl = slice(c * FF_CHUNK, (c + 1) * FF_CHUNK)
        a = jnp.maximum(_dot(n, wup_ref[0, :, sl]), 0.0)
        acc = acc + _dot((a * a).astype(jnp.bfloat16), wdn_ref[0, sl, :])
    return acc


def _mix_out_mlp_kernel(h_ref, y_ref, wo_ref, gmlp_ref, gnext_ref, wup_ref, wdn_ref,
                        out_ref, hn_ref):
    out = _mlp_tile(h_ref[...], y_ref[...], wo_ref, gmlp_ref, wup_ref, wdn_ref)
    out_ref[...] = out
    hn_ref[...] = _rms(out, gnext_ref[...], D_MODEL).astype(jnp.bfloat16)


def _mix_out_mlp(h, y, wo, wo_layer, gmlp, gnext, wup, wdn, layer, tm):
    r, d = h.shape
    return pl.pallas_call(
        _mix_out_mlp_kernel,
        grid=(r // tm,),
        in_specs=[
            pl.BlockSpec((tm, d), lambda i: (i, 0)),
            pl.BlockSpec((tm, d), lambda i: (i, 0)),
            _layer_spec((d, d), wo_layer),
            _const_spec((1, d)),
            _const_spec((1, d)),
            _layer_spec((d, D_FF), layer),
            _layer_spec((D_FF, d), layer),
        ],
        out_specs=[pl.BlockSpec((tm, d), lambda i: (i, 0))] * 2,
        out_shape=[jax.ShapeDtypeStruct((r, d), jnp.float32),
                   jax.ShapeDtypeStruct((r, d), jnp.bfloat16)],
        compiler_params=pltpu.CompilerParams(
            dimension_semantics=("arbitrary",), vmem_limit_bytes=VMEM_LIMIT),
        name="mix_out_mlp",
    )(h, y, wo, gmlp, gnext, wup, wdn)


def _mix_out_mlp_first_kernel(*refs):
    x_parts, meta_ref = refs[0:REAL_PARTS], refs[REAL_PARTS]
    y_ref, wo_ref, gmlp_ref, gnext_ref, wup_ref, wdn_ref, out_ref, hn_ref = refs[REAL_PARTS + 1:]
    h = _input_tile(pl.program_id(1), x_parts, meta_ref)
    out = _mlp_tile(h, y_ref[0], wo_ref, gmlp_ref, wup_ref, wdn_ref)
    out_ref[0] = out
    hn_ref[0] = _rms(out, gnext_ref[...], D_MODEL).astype(jnp.bfloat16)


def _mix_out_mlp_first(x, meta, y, wo, wo_layer, gmlp, gnext, wup, wdn, layer, tm):
    b, lp, d = y.shape
    tile = pl.BlockSpec((1, tm, d), lambda bi, i: (bi, i, 0))
    return pl.pallas_call(
        _mix_out_mlp_first_kernel,
        grid=(b, lp // tm),
        in_specs=_input_specs(tm) + [
            tile,
            _layer_spec((d, d), wo_layer),
            _const_spec((1, d)),
            _const_spec((1, d)),
            _layer_spec((d, D_FF), layer),
            _layer_spec((D_FF, d), layer),
        ],
        out_specs=[tile, tile],
        out_shape=[jax.ShapeDtypeStruct((b, lp, d), jnp.float32),
                   jax.ShapeDtypeStruct((b, lp, d), jnp.bfloat16)],
        compiler_params=pltpu.CompilerParams(
            dimension_semantics=("arbitrary", "arbitrary"), vmem_limit_bytes=VMEM_LIMIT),
        name="mix_out_mlp_first",
    )(*([x] * REAL_PARTS), meta, y, wo, gmlp, gnext, wup, wdn)


def _mix_out_mlp_last_kernel(*refs):
    h_parts, y_parts = refs[0:REAL_PARTS], refs[REAL_PARTS:2 * REAL_PARTS]
    wo_ref, gmlp_ref, wup_ref, wdn_ref, out_ref = refs[2 * REAL_PARTS:]
    h = jnp.concatenate([r[0] for r in h_parts], axis=0)
    y = jnp.concatenate([r[0] for r in y_parts], axis=0)
    out_ref[0] = _mlp_tile(h, y, wo_ref, gmlp_ref, wup_ref, wdn_ref)


def _mix_out_mlp_last(h, y, wo, wo_layer, gmlp, wup, wdn, layer, tm, seq):
    b, lp, d = h.shape
    part = tm // REAL_PARTS
    last_part = lp // part - 1

    def part_spec(k):
        return pl.BlockSpec(
            (1, part, d),
            lambda bi, i: (bi, jnp.minimum(REAL_START // part + REAL_PARTS * i + k, last_part), 0))

    parts = [part_spec(k) for k in range(REAL_PARTS)]
    return pl.pallas_call(
        _mix_out_mlp_last_kernel,
        grid=(b, pl.cdiv(seq, tm)),
        in_specs=parts + parts + [
            _layer_spec((d, d), wo_layer),
            _const_spec((1, d)),
            _layer_spec((d, D_FF), layer),
            _layer_spec((D_FF, d), layer),
        ],
        out_specs=pl.BlockSpec((1, tm, d), lambda bi, i: (bi, i, 0)),
        out_shape=jax.ShapeDtypeStruct((b, seq, d), jnp.float32),
        compiler_params=pltpu.CompilerParams(
            dimension_semantics=("arbitrary", "arbitrary"), vmem_limit_bytes=VMEM_LIMIT),
        name="mix_out_mlp_last",
    )(*([h] * REAL_PARTS + [y] * REAL_PARTS), wo, gmlp, wup, wdn)


def _pad_heads(w, heads, dim):
    k = w.shape[0]
    w = w.reshape(k, heads, dim)
    w = jnp.pad(w, ((0, 0), (0, 0), (0, LANE - dim)))
    return w.reshape(k, heads * LANE)


def _lane_vec(v, offset=0):
    return jnp.zeros((LANE,), jnp.float32).at[offset:offset + v.shape[0]].set(v)


def _attn_params(w_in, g_cq, w_uq, g_ckv, w_ukv, g_q_mla, g_k_mla, g_q_fox, g_k_fox, b_forget):
    bf = jnp.bfloat16
    o1 = Q_LORA
    o2 = o1 + KV_LORA
    o3 = o2 + MLA_ROPE
    o4 = o3 + FOX_HEADS * FOX_DIM
    o5 = o4 + FOX_HEADS * FOX_DIM
    o6 = o5 + FOX_HEADS * FOX_DIM
    misc = jnp.zeros((D_MODEL, LANE), jnp.float32)
    misc = misc.at[:, MISC_GATE:MISC_GATE + FOX_HEADS].set(w_in[:, o6:])
    misc = misc.at[:, MISC_ROPE:MISC_ROPE + MLA_ROPE].set(w_in[:, o2:o3])
    wcat = jnp.concatenate([w_in[:, :o1], misc, w_in[:, o1:o2], w_in[:, o3:o6]], axis=1).astype(bf)
    kv = w_ukv.reshape(KV_LORA, MLA_HEADS, MLA_NOPE + MLA_V)
    wkn = _pad_heads(kv[:, :, :MLA_NOPE].reshape(KV_LORA, -1), MLA_HEADS, MLA_NOPE).astype(bf)
    wv = jnp.pad(kv[:, :, MLA_NOPE:].reshape(KV_LORA, MLA_HEADS // 2, 2, MLA_V),
                 ((0, 0), (0, 0), (0, 0), (0, LANE - MLA_V)))
    wv = jnp.concatenate([wv[:, :, 0], jnp.roll(wv[:, :, 1], MLA_V, axis=-1)], axis=-1)
    wv = wv.reshape(KV_LORA, MLA_HEADS * LANE).astype(bf)
    lo, mid, hi = MLA_NOPE, MLA_NOPE + HALF_ROPE, MLA_NOPE + MLA_ROPE
    uq = w_uq.reshape(Q_LORA, MLA_HEADS, MLA_QK)
    uq_sw = jnp.zeros((Q_LORA, MLA_HEADS, LANE), jnp.float32)
    uq_sw = uq_sw.at[:, :, lo:mid].set(uq[:, :, mid:hi]).at[:, :, mid:hi].set(uq[:, :, lo:mid])
    wuq = jnp.concatenate([_pad_heads(w_uq, MLA_HEADS, MLA_QK),
                           uq_sw.reshape(Q_LORA, MLA_HEADS * LANE)], axis=1).astype(bf)

    def swapped(g):
        return jnp.zeros((LANE,), jnp.float32).at[lo:mid].set(g[mid:hi]).at[mid:hi].set(g[lo:mid])

    zero = jnp.zeros((LANE,), jnp.float32)
    rows = [zero] * VEC_ROWS
    rows[V_GQ_MLA] = _lane_vec(g_q_mla) * LOG2E
    rows[V_GQ_MLA_SW] = swapped(g_q_mla) * LOG2E
    rows[V_GK_MLA] = _lane_vec(g_k_mla) * MLA_QK ** 0.5
    rows[V_GK_MLA_SW] = swapped(g_k_mla) * MLA_QK ** 0.5
    for par in range(2):
        feat, extra = FEATURE_BASE[par], EXTRA_BASE[par]
        rows[V_GQ_FOX + par] = _lane_vec(g_q_fox, feat) * LOG2E
        rows[V_GK_FOX + par] = _lane_vec(g_k_fox, feat) * FOX_DIM ** 0.5
        rows[V_ADD_Q_FOX + par] = (zero.at[extra + N_SPLIT:extra + 2 * N_SPLIT].set(1.0)
                                   .at[extra + FLAG_FOX_OFF].set(1.0))
        rows[V_ONES_K_FOX + par] = zero.at[extra:extra + N_SPLIT].set(1.0)
        rows[V_ONES_V + par] = zero.at[extra:extra + FOX_DIM].set(1.0)
    rows[V_B_FORGET] = _lane_vec(b_forget, MISC_GATE)
    rows[V_ADD_Q_MLA] = zero.at[FLAG_MLA].set(1.0)
    vec = jnp.stack(rows)
    return dict(wcat=wcat, gcq=g_cq[None], wuq=wuq, gckv=g_ckv[None], wkn=wkn, wv=wv, vec=vec)


def _gate_selectors():
    selq = np.zeros((LANE, FOX_HEADS * LANE), np.float32)
    selk = np.zeros((LANE, FOX_HEADS * LANE), np.float32)
    for part in range(N_SPLIT):
        for hd in range(FOX_HEADS):
            extra = hd * LANE + EXTRA_BASE[hd % 2]
            selq[part * FOX_HEADS + hd, extra + part] = 1.0
            selk[part * FOX_HEADS + hd, extra + N_SPLIT + part] = -1.0
    return jnp.asarray(selq, jnp.bfloat16), jnp.asarray(selk, jnp.bfloat16)


def _rope_table(lp):
    lane = jnp.arange(LANE, dtype=jnp.int32)
    rotary = (lane >= MLA_NOPE) & (lane < MLA_NOPE + MLA_ROPE)
    first_half = rotary & (lane < MLA_NOPE + HALF_ROPE)
    pair = ((lane - MLA_NOPE) % HALF_ROPE).astype(jnp.float32)
    inv_freq = ROPE_BASE ** (-(2.0 * pair) / MLA_ROPE)
    pos = (jnp.arange(lp, dtype=jnp.int32) - PAD).astype(jnp.float32)
    ang = pos[:, None] * inv_freq[None, :]
    cos_t = jnp.where(lane < MLA_NOPE, 1.0, jnp.where(rotary, jnp.cos(ang), 0.0))
    sin_sw = jnp.where(rotary, jnp.where(first_half, -jnp.sin(ang), jnp.sin(ang)), 0.0)
    return jnp.concatenate([cos_t, sin_sw], axis=1)


def _pruning_tables(gate_end, g_q, g_k):
    b, nt = gate_end.shape[:2]
    per_tile = FLASH_TQ // FLASH_TK
    fox = gate_end[:, :, 0:per_tile, MISC_GATE:MISC_GATE + FOX_HEADS] * LOG2E
    fox = jnp.transpose(fox.reshape(b, nt * per_tile, FOX_HEADS), (0, 2, 1))
    table = jnp.concatenate([jnp.zeros((b, MLA_HEADS, nt * per_tile), jnp.float32), fox], axis=1)
    bound = 1.02 * FOX_DIM * (FOX_DIM ** -0.5 * LOG2E) * jnp.max(jnp.abs(g_q)) * jnp.max(jnp.abs(g_k))
    slack_fox = -(2.0 * bound + UNDERFLOW_LOG2 + 4.0)
    slack = jnp.concatenate([jnp.full((MLA_HEADS,), NEG, jnp.float32),
                             jnp.full((FOX_HEADS,), slack_fox, jnp.float32)])
    return table, slack


def _token_tile(lp):
    if lp % FLASH_TQ:
        raise ValueError(f"padded length {lp} is not a multiple of {FLASH_TQ}")
    return FLASH_TQ


def kernel(x, meta_tokens, g_mix, g_mlp, w_in_attn, g_cq, w_uq, g_ckv, w_ukv, g_q_mla, g_k_mla,
           g_q_fox, g_k_fox, b_forget, w_out_attn, w_in_conv, conv_w, w_out_conv, w_mlp_up,
           w_mlp_down):
    b, seq, d = x.shape
    assert d == D_MODEL and (PAD + N_META + seq) % BLOCK == 0
    lp = PAD + N_META + seq
    tm = _token_tile(lp)
    bf = jnp.bfloat16

    meta = meta_tokens.astype(x.dtype)
    h = hn = None

    rope_tab = _rope_table(lp)
    tri = (jnp.arange(tm)[:, None] >= jnp.arange(tm)[None, :]).astype(bf)
    selq, selk = _gate_selectors()

    wo_attn, wo_conv, w_conv = w_out_attn.astype(bf), w_out_conv.astype(bf), w_in_conv.astype(bf)
    w_up, w_down = w_mlp_up.astype(bf), w_mlp_down.astype(bf)
    for layer in range(DEPTH):
        j = layer // 2
        gmix = g_mix[layer][None]
        if layer % 2 == 0:
            p = _attn_params(w_in_attn[j], g_cq[j], w_uq[j], g_ckv[j], w_ukv[j], g_q_mla[j],
                             g_k_mla[j], g_q_fox[j], g_k_fox[j], b_forget[j])
            stream = (x, meta) if layer == 0 else (hn,)
            q, k, v, gate_end = _attn_in(stream, lp, gmix, p, rope_tab, tri, selq, selk, tm)
            y = _flash(*_pruning_tables(gate_end, g_q_fox[j], g_k_fox[j]), q, k, v)
            wo = wo_attn
        else:
            cw = jnp.zeros((8, d), jnp.float32).at[0:3].set(conv_w[j])
            y = _conv_in(hn, w_conv, j, cw, tm)
            wo = wo_conv
        gmlp = g_mlp[layer][None]
        gnext = g_mix[min(layer + 1, DEPTH - 1)][None]
        if layer == 0:
            h, hn = _mix_out_mlp_first(x, meta, y, wo, j, gmlp, gnext, w_up, w_down, layer, tm)
        elif layer < DEPTH - 1:
            h, hn = _mix_out_mlp(h.reshape(b * lp, d), y.reshape(b * lp, d), wo, j, gmlp, gnext,
                                 w_up, w_down, layer, tm)
            h, hn = h.reshape(b, lp, d), hn.reshape(b, lp, d)
        else:
            return _mix_out_mlp_last(h, y, wo, j, gmlp, w_up, w_down, layer, tm, seq)
```

```python
import functools

import numpy as np
import jax
import jax.numpy as jnp
from jax import lax
from jax.experimental import pallas as pl
from jax.experimental.pallas import tpu as pltpu

D_MODEL = 1024
DEPTH = 4
N_META = 16
BLOCK = 128
PAD = 2 * BLOCK - N_META
REAL_START = PAD + N_META
REAL_PARTS = 3
MLA_HEADS = 8
MLA_NOPE = 64
MLA_ROPE = 32
MLA_QK = MLA_NOPE + MLA_ROPE
MLA_V = 64
Q_LORA = 384
KV_LORA = 256
ROPE_BASE = 10000.0
FOX_HEADS = 8
FOX_DIM = 64
D_FF = 4 * D_MODEL
EPS = 1e-6
NEG = -1e30

LANE = 128
HEADS = MLA_HEADS + FOX_HEADS
HALF_ROPE = MLA_ROPE // 2
FEATURE_BASE = (0, FOX_DIM)
EXTRA_BASE = (FOX_DIM, 0)
N_SPLIT = 3
ONE_LANE = N_SPLIT * FOX_HEADS
PADROW_LANE = ONE_LANE + 1
FLAG_FOX_OFF = 2 * N_SPLIT
FLAG_MLA = MLA_QK
PAD_KEY = NEG
LOG2E = 1.4426950408889634
MISC_GATE = 0
MISC_ROPE = MLA_NOPE

OFF_CQ = 0
OFF_MISC = OFF_CQ + Q_LORA
OFF_CKV = OFF_MISC + LANE
OFF_FQ = OFF_CKV + KV_LORA
OFF_FK = OFF_FQ + FOX_HEADS * FOX_DIM
OFF_FV = OFF_FK + FOX_HEADS * FOX_DIM
W_CAT = OFF_FV + FOX_HEADS * FOX_DIM

(V_GQ_MLA, V_GQ_MLA_SW, V_GK_MLA, V_GK_MLA_SW, V_ADD_Q_MLA, V_B_FORGET) = range(6)
V_GQ_FOX, V_GK_FOX, V_ONES_V = 6, 8, 10
VEC_ROWS = 16
PAIR = 2 * LANE

FF_CHUNK = 1024
UNDERFLOW_LOG2 = 150.0
FLASH_TQ = 768
FLASH_TK = 256
BLOCKS_PER_BODY = 4
VMEM_LIMIT = 56 * 1024 * 1024


def _const_spec(shape):
    nd = len(shape)
    return pl.BlockSpec(shape, lambda *_: (0,) * nd, pipeline_mode=pl.Buffered(1))


def _layer_spec(shape, layer):
    nd = len(shape)
    return pl.BlockSpec((1,) + shape, lambda *_: (layer,) + (0,) * nd,
                        pipeline_mode=pl.Buffered(1))


def _input_specs(tm):
    part = tm // REAL_PARTS
    assert REAL_START == part

    def part_spec(k):
        return pl.BlockSpec((1, part, D_MODEL),
                            lambda bi, i: (bi, jnp.maximum(REAL_PARTS * i + k - 1, 0), 0))

    return [part_spec(k) for k in range(REAL_PARTS)] + [_const_spec((N_META, D_MODEL))]


def _input_tile(i, x_parts, meta_ref):
    lead = jnp.concatenate([jnp.zeros((PAD, D_MODEL), jnp.float32), meta_ref[...]], axis=0)
    first = jnp.where(i == 0, lead, x_parts[0][0])
    return jnp.concatenate([first] + [r[0] for r in x_parts[1:]], axis=0)


def _rms(x, g, n):
    ms = jnp.sum(x * x, axis=-1, keepdims=True) * (1.0 / n)
    return x * lax.rsqrt(ms + EPS) * g


def _split3(x, extra=0.0):
    hi = x.astype(jnp.bfloat16).astype(jnp.float32)
    r1 = x - hi
    mid = r1.astype(jnp.bfloat16).astype(jnp.float32)
    lo = r1 - mid
    packed = hi + pltpu.roll(mid, FOX_HEADS, 1) + pltpu.roll(lo, 2 * FOX_HEADS, 1)
    return (packed + extra).astype(jnp.bfloat16)


def _dot(a, b):
    return jnp.dot(a, b, preferred_element_type=jnp.float32)


def _attn_in_kernel(*refs, tm, from_x):
    n_stream = REAL_PARTS + 1 if from_x else 1
    stream = refs[:n_stream]
    (gmix_ref, wcat_ref, gcq_ref, wuq_ref, gckv_ref, wkn_ref, wv_ref, vec_ref, rope_ref, tri_ref,
     selq_ref, selk_ref, q_ref, k_ref, v_ref, gate_end_ref, carry_ref) = refs[n_stream:]
    i = pl.program_id(1)

    @pl.when(i == 0)
    def _():
        carry_ref[...] = jnp.zeros_like(carry_ref)

    if from_x:
        x = _input_tile(i, stream[:-1], stream[-1])
        hn = _rms(x, gmix_ref[...], D_MODEL).astype(jnp.bfloat16)
    else:
        hn = stream[0][0]

    def seg(lo, width):
        return _dot(hn, wcat_ref[:, lo:lo + width])

    def vec(r):
        return vec_ref[r:r + 1, :]

    cos_t = rope_ref[:, 0:LANE]
    sin_sw = rope_ref[:, LANE:2 * LANE]
    gc_q, gs_q = vec(V_GQ_MLA) * cos_t, vec(V_GQ_MLA_SW) * sin_sw
    gc_k, gs_k = vec(V_GK_MLA) * cos_t, vec(V_GK_MLA_SW) * sin_sw
    add_q_mla = vec(V_ADD_Q_MLA)

    lane = lax.broadcasted_iota(jnp.int32, (tm, LANE), 1)
    row = lax.broadcasted_iota(jnp.int32, (tm, LANE), 0)
    valid = (i * tm + row) >= PAD
    pad_key = jnp.where(valid, 0.0, PAD_KEY)
    add_k_mla = jnp.where(lane == FLAG_MLA, pad_key, 0.0)
    halves = (lane < FOX_DIM, lane >= FOX_DIM)
    sel_extra = jnp.where(lane == ONE_LANE, 1.0,
                          jnp.where((lane == PADROW_LANE) & ~valid, 1.0, 0.0))

    cq_misc = seg(OFF_CQ, Q_LORA + LANE)
    misc = cq_misc[:, Q_LORA:]
    kpe = jnp.where((lane >= MISC_ROPE) & (lane < MISC_ROPE + MLA_ROPE), misc, 0.0)
    k_rot = jnp.where(lane < MISC_ROPE + HALF_ROPE, pltpu.roll(kpe, LANE - HALF_ROPE, 1),
                      pltpu.roll(kpe, HALF_ROPE, 1)) * gs_k
    xl = misc + vec(V_B_FORGET)
    logf = jnp.minimum(xl, 0.0) - jnp.log1p(jnp.exp(-jnp.abs(xl)))
    logf = jnp.where(valid & (lane >= MISC_GATE) & (lane < MISC_GATE + FOX_HEADS), logf, 0.0)
    cs = _dot(tri_ref[...], _split3(logf))
    cs = (cs + pltpu.roll(cs, LANE - FOX_HEADS, 1)) + pltpu.roll(cs, LANE - 2 * FOX_HEADS, 1)
    cum = jnp.where(lane < FOX_HEADS, cs, 0.0) + carry_ref[0:1, :]
    carry_ref[0:1, :] = cum[tm - 1:tm, :]
    gate_end_ref[0, 0] = jnp.zeros((8, LANE), jnp.float32)
    for c in range(tm // FLASH_TK):
        gate_end_ref[0, 0, c:c + 1, :] = cum[(c + 1) * FLASH_TK - 1:(c + 1) * FLASH_TK, :]
    cum3 = _split3(cum * LOG2E, sel_extra)
    gate_q = _dot(cum3, selq_ref[...])
    gate_k = _dot(cum3, selk_ref[...])

    def inv_norm(sq, n):
        return lax.rsqrt(jnp.sum(sq, axis=-1, keepdims=True) + n * EPS)

    def fox_group(g):
        xq4 = seg(OFF_FQ + g * PAIR, PAIR)
        xk4 = seg(OFF_FK + g * PAIR, PAIR)
        xv4 = seg(OFF_FV + g * PAIR, PAIR)
        for e in range(4):
            hd, par = 4 * g + e, e % 2
            sl = slice((e // 2) * LANE, (e // 2 + 1) * LANE)
            gl = slice(hd * LANE, (hd + 1) * LANE)
            xq, xk = xq4[:, sl], xk4[:, sl]
            rq = inv_norm(jnp.where(halves[par], xq * xq, 0.0), FOX_DIM)
            rk = inv_norm(jnp.where(halves[par], xk * xk, 0.0), FOX_DIM)
            q_ref[0, MLA_HEADS + hd] = (xq * vec(V_GQ_FOX + par) * rq + gate_q[:, gl]
                                        ).astype(jnp.bfloat16)
            k_ref[0, MLA_HEADS + hd] = (xk * vec(V_GK_FOX + par) * rk + gate_k[:, gl]
                                        ).astype(jnp.bfloat16)
            v_ref[0, MLA_HEADS + hd] = jnp.where(halves[par], xv4[:, sl], 1.0
                                                 ).astype(jnp.bfloat16)

    cqn = _rms(cq_misc[:, :Q_LORA], gcq_ref[...], Q_LORA).astype(jnp.bfloat16)
    ckvn = _rms(seg(OFF_CKV, KV_LORA), gckv_ref[...], KV_LORA).astype(jnp.bfloat16)
    def mla_pair(g):
        cols = slice(g * PAIR, (g + 1) * PAIR)
        cols_sw = slice(MLA_HEADS * LANE + g * PAIR, MLA_HEADS * LANE + (g + 1) * PAIR)
        xq2 = _dot(cqn, wuq_ref[:, cols])
        xq2_sw = _dot(cqn, wuq_ref[:, cols_sw])
        xk2 = _dot(ckvn, wkn_ref[:, cols])
        xv2 = _dot(ckvn, wv_ref[:, cols])
        for e in range(2):
            hd, sl = 2 * g + e, slice(e * LANE, (e + 1) * LANE)
            xq = xq2[:, sl]
            q_ref[0, hd] = ((xq * gc_q + xq2_sw[:, sl] * gs_q) * inv_norm(xq * xq, MLA_QK)
                            + add_q_mla).astype(jnp.bfloat16)
            xk = xk2[:, sl] + kpe
            k_ref[0, hd] = ((xk * gc_k + k_rot) * inv_norm(xk * xk, MLA_QK) + add_k_mla
                            ).astype(jnp.bfloat16)
            v_ref[0, hd] = (xv2[:, sl] + vec(V_ONES_V + e)).astype(jnp.bfloat16)

    fox_group(0)
    mla_pair(0)
    mla_pair(1)
    fox_group(1)
    mla_pair(2)
    mla_pair(3)


def _attn_in(stream, lp, gmix, p, rope_tab, tri, selq, selk, tm):
    from_x = len(stream) == 2
    b, d = stream[0].shape[0], D_MODEL
    nt = lp // tm
    kern = functools.partial(_attn_in_kernel, tm=tm, from_x=from_x)
    stream_specs = (_input_specs(tm) if from_x
                    else [pl.BlockSpec((1, tm, d), lambda bi, i: (bi, i, 0))])
    stream_args = [stream[0]] * REAL_PARTS + [stream[1]] if from_x else [stream[0]]
    qk_shape = jax.ShapeDtypeStruct((b, HEADS, lp, LANE), jnp.bfloat16)
    qk_spec = pl.BlockSpec((1, HEADS, tm, LANE), lambda bi, i: (bi, 0, i, 0))
    return pl.pallas_call(
        kern,
        grid=(b, nt),
        in_specs=stream_specs + [
            _const_spec((1, d)),
            _const_spec((d, W_CAT)),
            _const_spec((1, Q_LORA)),
            _const_spec((Q_LORA, 2 * MLA_HEADS * LANE)),
            _const_spec((1, KV_LORA)),
            _const_spec((KV_LORA, MLA_HEADS * LANE)),
            _const_spec((KV_LORA, MLA_HEADS * LANE)),
            _const_spec((VEC_ROWS, LANE)),
            pl.BlockSpec((tm, 2 * LANE), lambda bi, i: (i, 0)),
            _const_spec((tm, tm)),
            _const_spec((LANE, FOX_HEADS * LANE)),
            _const_spec((LANE, FOX_HEADS * LANE)),
        ],
        out_specs=[qk_spec, qk_spec, qk_spec,
                   pl.BlockSpec((1, 1, 8, LANE), lambda bi, i: (bi, i, 0, 0))],
        out_shape=[qk_shape, qk_shape, qk_shape,
                   jax.ShapeDtypeStruct((b, nt, 8, LANE), jnp.float32)],
        scratch_shapes=[pltpu.VMEM((8, LANE), jnp.float32)],
        compiler_params=pltpu.CompilerParams(
            dimension_semantics=("arbitrary", "arbitrary"), vmem_limit_bytes=VMEM_LIMIT),
        name="attn_in",
    )(*stream_args, gmix, p["wcat"], p["gcq"], p["wuq"], p["gckv"], p["wkn"], p["wv"], p["vec"],
      rope_tab, tri, selq, selk)


def _flash_kernel(gate_end_ref, slack_ref, q_ref, k_ref, v_ref, o_ref, m_ref, acc_ref, al_ref,
                  p_ref, *, tq, tk, nq):
    chunks = tq // tk
    bi, hp = pl.program_id(0), pl.program_id(1)

    def query_block(qi, carry):
        qbase = qi * tq

        def softmax(j, u, base, diagonal, first=False):
            r0 = u * tk if diagonal else 0
            rows = slice(r0, tq)
            start = pl.multiple_of(base + u * tk, tk)
            q_rows = pl.ds(pl.multiple_of(qbase + r0, tk), tq - r0)
            s = lax.dot_general(q_ref[0, j, q_rows, :], k_ref[0, j, pl.ds(start, tk), :],
                                (((1,), (1,)), ((), ())), preferred_element_type=jnp.float32)
            if diagonal:
                row = lax.broadcasted_iota(jnp.int32, (tq - r0, tk), 0)
                col = lax.broadcasted_iota(jnp.int32, (tq - r0, tk), 1)
                s = jnp.where(col <= row, s, NEG)
            if first:
                m_next = jnp.broadcast_to(jnp.max(s, axis=1, keepdims=True), (tq - r0, LANE))
            else:
                m_prev = m_ref[j, rows, :]
                m_next = jnp.maximum(m_prev, jnp.max(s, axis=1, keepdims=True))
                al_ref[j, u, rows, :] = jnp.exp2(m_prev - m_next)
            p = jnp.exp2(s - jnp.concatenate([m_next] * (tk // LANE), axis=1))
            p_ref[j, u, rows, :] = p.astype(jnp.bfloat16)
            m_ref[j, rows, :] = m_next

        def pv(j, u, base, diagonal, first=False):
            r0 = u * tk if diagonal else 0
            rows = slice(r0, tq)
            start = pl.multiple_of(base + u * tk, tk)
            new = _dot(p_ref[j, u, rows, :], v_ref[0, j, pl.ds(start, tk), :])
            if first:
                acc_ref[j, rows, :] = new
            else:
                acc_ref[j, rows, :] = acc_ref[j, rows, :] * al_ref[j, u, rows, :] + new

        def diagonal_block():
            for u in range(chunks):
                softmax(0, u, qbase, True, first=(u == 0))
                if u > 0:
                    pv(1, u - 1, qbase, True, first=(u == 1))
                softmax(1, u, qbase, True, first=(u == 0))
                pv(0, u, qbase, True, first=(u == 0))
            pv(1, chunks - 1, qbase, True, first=(chunks == 1))

        def block(kb, diagonal, start=0):
            base = kb * tq
            for u in range(start, chunks):
                softmax(0, u, base, diagonal)
                if u == start:
                    pv(1, chunks - 1, jnp.maximum(kb - 1, 0) * tq, False)
                else:
                    pv(1, u - 1, base, diagonal)
                softmax(1, u, base, diagonal)
                pv(0, u, base, diagonal)

        def chunks_needed(j):
            hd = 2 * hp + j
            gate_q = gate_end_ref[bi, hd, jnp.maximum(chunks * qi - 1, 0)]
            count = jnp.int32(0)
            for c in range(chunks * (nq - 1)):
                keep = (c < chunks * qi) & (gate_q - gate_end_ref[bi, hd, c] >= slack_ref[hd])
                count = count + keep.astype(jnp.int32)
            return count

        diagonal_block()
        al_ref[1, chunks - 1] = jnp.ones((tq, LANE), jnp.float32)
        p_ref[1, chunks - 1] = jnp.zeros((tq, tk), jnp.bfloat16)
        n_chunks = jnp.maximum(chunks_needed(0), chunks_needed(1))
        partial = n_chunks % chunks
        n_full = n_chunks // chunks
        first = qi - n_full
        for start in range(1, chunks):

            @pl.when(partial == chunks - start)
            def _():
                block(first - 1, False, start)

        rem = n_full & (BLOCKS_PER_BODY - 1)

        @pl.when((rem & 1) == 1)
        def _():
            block(first, False)

        @pl.when((rem & 2) == 2)
        def _():
            kb = first + (rem & 1)
            block(kb, False)
            block(kb + 1, False)

        def body(group, c):
            kb = first + rem + BLOCKS_PER_BODY * group
            for d in range(BLOCKS_PER_BODY):
                block(kb + d, False)
            return c

        lax.fori_loop(0, n_full // BLOCKS_PER_BODY, body, 0)
        pv(1, chunks - 1, jnp.maximum(qi - 1, 0) * tq, False)

        o0 = acc_ref[0]
        o1 = acc_ref[1]
        o0 = o0 / pltpu.roll(o0, FOX_DIM, 1)
        o1 = o1 / pltpu.roll(o1, FOX_DIM, 1)
        lane = lax.broadcasted_iota(jnp.int32, (tq, LANE), 1)
        o = jnp.where(lane < MLA_V, o0, o1)
        o_ref[0, pl.ds(pl.multiple_of(qbase, tq), tq), :] = o.astype(jnp.bfloat16)
        return carry

    lax.fori_loop(0, nq, query_block, 0)


def _flash(gate_end, slack, q, k, v):
    b, _, lp, _ = q.shape
    tq, tk = FLASH_TQ, FLASH_TK
    kern = functools.partial(_flash_kernel, tq=tq, tk=tk, nq=lp // tq)
    qkv_spec = pl.BlockSpec((1, 2, lp, LANE), lambda bi, hp: (bi, hp, 0, 0))
    return pl.pallas_call(
        kern,
        grid=(b, HEADS // 2),
        in_specs=[pl.BlockSpec(memory_space=pltpu.SMEM), pl.BlockSpec(memory_space=pltpu.SMEM),
                  qkv_spec, qkv_spec, qkv_spec],
        out_specs=pl.BlockSpec((1, lp, LANE), lambda bi, hp: (bi, 0, hp)),
        out_shape=jax.ShapeDtypeStruct((b, lp, HEADS * MLA_V), jnp.bfloat16),
        scratch_shapes=[pltpu.VMEM((2, tq, LANE), jnp.float32)] * 2
        + [pltpu.VMEM((2, tq // tk, tq, LANE), jnp.float32),
           pltpu.VMEM((2, tq // tk, tq, tk), jnp.bfloat16)],
        compiler_params=pltpu.CompilerParams(
            dimension_semantics=("arbitrary", "arbitrary"), vmem_limit_bytes=VMEM_LIMIT),
        name="flash",
    )(gate_end, slack, q, k, v)


def _conv_in_kernel(hn_ref, win_ref, cw_ref, y_ref, gs_ref, *, tm):
    i = pl.program_id(1)

    @pl.when(i == 0)
    def _():
        gs_ref[0:8, :] = jnp.zeros((8, D_MODEL), jnp.float32)

    hn = hn_ref[0]
    gate_c = _dot(hn, win_ref[0, :, D_MODEL:2 * D_MODEL])
    u = _dot(hn, win_ref[0, :, 2 * D_MODEL:3 * D_MODEL])
    row = lax.broadcasted_iota(jnp.int32, (tm, D_MODEL), 0)
    g = jnp.where((i * tm + row) >= PAD, gate_c * u, 0.0)
    gs_ref[8:tm + 8, :] = g
    y = (cw_ref[0:1, :] * gs_ref[6:tm + 6, :] + cw_ref[1:2, :] * gs_ref[7:tm + 7, :]
         + cw_ref[2:3, :] * g)
    gs_ref[0:8, :] = gs_ref[tm:tm + 8, :]
    gate_b = _dot(hn, win_ref[0, :, 0:D_MODEL])
    y_ref[0] = (gate_b * y).astype(jnp.bfloat16)


def _conv_in(hn, win, layer, cw, tm):
    b, lp, d = hn.shape
    kern = functools.partial(_conv_in_kernel, tm=tm)
    return pl.pallas_call(
        kern,
        grid=(b, lp // tm),
        in_specs=[
            pl.BlockSpec((1, tm, d), lambda bi, i: (bi, i, 0)),
            _layer_spec((d, 3 * d), layer),
            _const_spec((8, d)),
        ],
        out_specs=pl.BlockSpec((1, tm, d), lambda bi, i: (bi, i, 0)),
        out_shape=jax.ShapeDtypeStruct((b, lp, d), jnp.bfloat16),
        scratch_shapes=[pltpu.VMEM((tm + 8, d), jnp.float32)],
        compiler_params=pltpu.CompilerParams(
            dimension_semantics=("arbitrary", "arbitrary"), vmem_limit_bytes=VMEM_LIMIT),
        name="conv_in",
    )(hn, win, cw)


def _mlp_tile(h, y, wo_ref, gmlp_ref, wup_ref, wdn_ref):
    h1 = h + _dot(y, wo_ref[0])
    n = _rms(h1, gmlp_ref[...], D_MODEL).astype(jnp.bfloat16)
    acc = h1
    for c in range(D_FF // FF_CHUNK):
        sl = slice(c * FF_CHUNK, (c + 1) * FF_CHUNK)
        a = jnp.maximum(_dot(n, wup_ref[0, :, sl]), 0.0)
        acc = acc + _dot((a * a).astype(jnp.bfloat16), wdn_ref[0, sl, :])
    return acc


def _mix_out_mlp_kernel(h_ref, y_ref, wo_ref, gmlp_ref, gnext_ref, wup_ref, wdn_ref,
                        out_ref, hn_ref):
    out = _mlp_tile(h_ref[...], y_ref[...], wo_ref, gmlp_ref, wup_ref, wdn_ref)
    out_ref[...] = out
    hn_ref[...] = _rms(out, gnext_ref[...], D_MODEL).astype(jnp.bfloat16)


def _mix_out_mlp(h, y, wo, wo_layer, gmlp, gnext, wup, wdn, layer, tm):
    r, d = h.shape
    return pl.pallas_call(
        _mix_out_mlp_kernel,
        grid=(r // tm,),
        in_specs=[
            pl.BlockSpec((tm, d), lambda i: (i, 0)),
            pl.BlockSpec((tm, d), lambda i: (i, 0)),
            _layer_spec((d, d), wo_layer),
            _const_spec((1, d)),
            _const_spec((1, d)),
            _layer_spec((d, D_FF), layer),
            _layer_spec((D_FF, d), layer),
        ],
        out_specs=[pl.BlockSpec((tm, d), lambda i: (i, 0))] * 2,
        out_shape=[jax.ShapeDtypeStruct((r, d), jnp.float32),
                   jax.ShapeDtypeStruct((r, d), jnp.bfloat16)],
        compiler_params=pltpu.CompilerParams(
            dimension_semantics=("arbitrary",), vmem_limit_bytes=VMEM_LIMIT),
        name="mix_out_mlp",
    )(h, y, wo, gmlp, gnext, wup, wdn)


def _mix_out_mlp_first_kernel(*refs):
    x_parts, meta_ref = refs[0:REAL_PARTS], refs[REAL_PARTS]
    y_ref, wo_ref, gmlp_ref, gnext_ref, wup_ref, wdn_ref, out_ref, hn_ref = refs[REAL_PARTS + 1:]
    h = _input_tile(pl.program_id(1), x_parts, meta_ref)
    out = _mlp_tile(h, y_ref[0], wo_ref, gmlp_ref, wup_ref, wdn_ref)
    out_ref[0] = out
    hn_ref[0] = _rms(out, gnext_ref[...], D_MODEL).astype(jnp.bfloat16)


def _mix_out_mlp_first(x, meta, y, wo, wo_layer, gmlp, gnext, wup, wdn, layer, tm):
    b, lp, d = y.shape
    tile = pl.BlockSpec((1, tm, d), lambda bi, i: (bi, i, 0))
    return pl.pallas_call(
        _mix_out_mlp_first_kernel,
        grid=(b, lp // tm),
        in_specs=_input_specs(tm) + [
            tile,
            _layer_spec((d, d), wo_layer),
            _const_spec((1, d)),
            _const_spec((1, d)),
            _layer_spec((d, D_FF), layer),
            _layer_spec((D_FF, d), layer),
        ],
        out_specs=[tile, tile],
        out_shape=[jax.ShapeDtypeStruct((b, lp, d), jnp.float32),
                   jax.ShapeDtypeStruct((b, lp, d), jnp.bfloat16)],
        compiler_params=pltpu.CompilerParams(
            dimension_semantics=("arbitrary", "arbitrary"), vmem_limit_bytes=VMEM_LIMIT),
        name="mix_out_mlp_first",
    )(*([x] * REAL_PARTS), meta, y, wo, gmlp, gnext, wup, wdn)


def _mix_out_mlp_last_kernel(*refs):
    h_parts, y_parts = refs[0:REAL_PARTS], refs[REAL_PARTS:2 * REAL_PARTS]
    wo_ref, gmlp_ref, wup_ref, wdn_ref, out_ref = refs[2 * REAL_PARTS:]
    h = jnp.concatenate([r[0] for r in h_parts], axis=0)
    y = jnp.concatenate([r[0] for r in y_parts], axis=0)
    out_ref[0] = _mlp_tile(h, y, wo_ref, gmlp_ref, wup_ref, wdn_ref)


def _mix_out_mlp_last(h, y, wo, wo_layer, gmlp, wup, wdn, layer, tm, seq):
    b, lp, d = h.shape
    part = tm // REAL_PARTS
    last_part = lp // part - 1

    def part_spec(k):
        return pl.BlockSpec(
            (1, part, d),
            lambda bi, i: (bi, jnp.minimum(REAL_START // part + REAL_PARTS * i + k, last_part), 0))

    parts = [part_spec(k) for k in range(REAL_PARTS)]
    return pl.pallas_call(
        _mix_out_mlp_last_kernel,
        grid=(b, pl.cdiv(seq, tm)),
        in_specs=parts + parts + [
            _layer_spec((d, d), wo_layer),
            _const_spec((1, d)),
            _layer_spec((d, D_FF), layer),
            _layer_spec((D_FF, d), layer),
        ],
        out_specs=pl.BlockSpec((1, tm, d), lambda bi, i: (bi, i, 0)),
        out_shape=jax.ShapeDtypeStruct((b, seq, d), jnp.float32),
        compiler_params=pltpu.CompilerParams(
            dimension_semantics=("arbitrary", "arbitrary"), vmem_limit_bytes=VMEM_LIMIT),
        name="mix_out_mlp_last",
    )(*([h] * REAL_PARTS + [y] * REAL_PARTS), wo, gmlp, wup, wdn)


def _pad_heads(w, heads, dim):
    k = w.shape[0]
    w = w.reshape(k, heads, dim)
    w = jnp.pad(w, ((0, 0), (0, 0), (0, LANE - dim)))
    return w.reshape(k, heads * LANE)


def _lane_vec(v, offset=0):
    return jnp.zeros((LANE,), jnp.float32).at[offset:offset + v.shape[0]].set(v)


def _attn_params(w_in, g_cq, w_uq, g_ckv, w_ukv, g_q_mla, g_k_mla, g_q_fox, g_k_fox, b_forget):
    bf = jnp.bfloat16
    o1 = Q_LORA
    o2 = o1 + KV_LORA
    o3 = o2 + MLA_ROPE
    o4 = o3 + FOX_HEADS * FOX_DIM
    o5 = o4 + FOX_HEADS * FOX_DIM
    o6 = o5 + FOX_HEADS * FOX_DIM
    misc = jnp.zeros((D_MODEL, LANE), jnp.float32)
    misc = misc.at[:, MISC_GATE:MISC_GATE + FOX_HEADS].set(w_in[:, o6:])
    misc = misc.at[:, MISC_ROPE:MISC_ROPE + MLA_ROPE].set(w_in[:, o2:o3])
    wcat = jnp.concatenate([w_in[:, :o1], misc, w_in[:, o1:o2], w_in[:, o3:o6]], axis=1).astype(bf)
    kv = w_ukv.reshape(KV_LORA, MLA_HEADS, MLA_NOPE + MLA_V)
    wkn = _pad_heads(kv[:, :, :MLA_NOPE].reshape(KV_LORA, -1), MLA_HEADS, MLA_NOPE).astype(bf)
    wv = jnp.pad(kv[:, :, MLA_NOPE:].reshape(KV_LORA, MLA_HEADS // 2, 2, MLA_V),
                 ((0, 0), (0, 0), (0, 0), (0, LANE - MLA_V)))
    wv = jnp.concatenate([wv[:, :, 0], jnp.roll(wv[:, :, 1], MLA_V, axis=-1)], axis=-1)
    wv = wv.reshape(KV_LORA, MLA_HEADS * LANE).astype(bf)
    lo, mid, hi = MLA_NOPE, MLA_NOPE + HALF_ROPE, MLA_NOPE + MLA_ROPE
    uq = w_uq.reshape(Q_LORA, MLA_HEADS, MLA_QK)
    uq_sw = jnp.zeros((Q_LORA, MLA_HEADS, LANE), jnp.float32)
    uq_sw = uq_sw.at[:, :, lo:mid].set(uq[:, :, mid:hi]).at[:, :, mid:hi].set(uq[:, :, lo:mid])
    wuq = jnp.concatenate([_pad_heads(w_uq, MLA_HEADS, MLA_QK),
                           uq_sw.reshape(Q_LORA, MLA_HEADS * LANE)], axis=1).astype(bf)

    def swapped(g):
        return jnp.zeros((LANE,), jnp.float32).at[lo:mid].set(g[mid:hi]).at[mid:hi].set(g[lo:mid])

    zero = jnp.zeros((LANE,), jnp.float32)
    rows = [zero] * VEC_ROWS
    rows[V_GQ_MLA] = _lane_vec(g_q_mla) * LOG2E
    rows[V_GQ_MLA_SW] = swapped(g_q_mla) * LOG2E
    rows[V_GK_MLA] = _lane_vec(g_k_mla) * MLA_QK ** 0.5
    rows[V_GK_MLA_SW] = swapped(g_k_mla) * MLA_QK ** 0.5
    for par in range(2):
        feat, extra = FEATURE_BASE[par], EXTRA_BASE[par]
        rows[V_GQ_FOX + par] = _lane_vec(g_q_fox, feat) * LOG2E
        rows[V_GK_FOX + par] = _lane_vec(g_k_fox, feat) * FOX_DIM ** 0.5
        rows[V_ONES_V + par] = zero.at[extra:extra + FOX_DIM].set(1.0)
    rows[V_B_FORGET] = _lane_vec(b_forget, MISC_GATE)
    rows[V_ADD_Q_MLA] = zero.at[FLAG_MLA].set(1.0)
    vec = jnp.stack(rows)
    return dict(wcat=wcat, gcq=g_cq[None], wuq=wuq, gckv=g_ckv[None], wkn=wkn, wv=wv, vec=vec)


def _gate_selectors():
    selq = np.zeros((LANE, FOX_HEADS * LANE), np.float32)
    selk = np.zeros((LANE, FOX_HEADS * LANE), np.float32)
    for hd in range(FOX_HEADS):
        extra = hd * LANE + EXTRA_BASE[hd % 2]
        for part in range(N_SPLIT):
            selq[part * FOX_HEADS + hd, extra + part] = 1.0
            selk[part * FOX_HEADS + hd, extra + N_SPLIT + part] = -1.0
            selq[ONE_LANE, extra + N_SPLIT + part] = 1.0
            selk[ONE_LANE, extra + part] = 1.0
        selq[ONE_LANE, extra + FLAG_FOX_OFF] = 1.0
        selk[PADROW_LANE, extra + FLAG_FOX_OFF] = PAD_KEY
    return jnp.asarray(selq, jnp.bfloat16), jnp.asarray(selk, jnp.bfloat16)


def _rope_table(lp):
    lane = jnp.arange(LANE, dtype=jnp.int32)
    rotary = (lane >= MLA_NOPE) & (lane < MLA_NOPE + MLA_ROPE)
    first_half = rotary & (lane < MLA_NOPE + HALF_ROPE)
    pair = ((lane - MLA_NOPE) % HALF_ROPE).astype(jnp.float32)
    inv_freq = ROPE_BASE ** (-(2.0 * pair) / MLA_ROPE)
    pos = (jnp.arange(lp, dtype=jnp.int32) - PAD).astype(jnp.float32)
    ang = pos[:, None] * inv_freq[None, :]
    cos_t = jnp.where(lane < MLA_NOPE, 1.0, jnp.where(rotary, jnp.cos(ang), 0.0))
    sin_sw = jnp.where(rotary, jnp.where(first_half, -jnp.sin(ang), jnp.sin(ang)), 0.0)
    return jnp.concatenate([cos_t, sin_sw], axis=1)


def _pruning_tables(gate_end, g_q, g_k):
    b, nt = gate_end.shape[:2]
    per_tile = FLASH_TQ // FLASH_TK
    fox = gate_end[:, :, 0:per_tile, MISC_GATE:MISC_GATE + FOX_HEADS] * LOG2E
    fox = jnp.transpose(fox.reshape(b, nt * per_tile, FOX_HEADS), (0, 2, 1))
    table = jnp.concatenate([jnp.zeros((b, MLA_HEADS, nt * per_tile), jnp.float32), fox], axis=1)
    bound = 1.02 * FOX_DIM * (FOX_DIM ** -0.5 * LOG2E) * jnp.max(jnp.abs(g_q)) * jnp.max(jnp.abs(g_k))
    slack_fox = -(2.0 * bound + UNDERFLOW_LOG2 + 4.0)
    slack = jnp.concatenate([jnp.full((MLA_HEADS,), NEG, jnp.float32),
                             jnp.full((FOX_HEADS,), slack_fox, jnp.float32)])
    return table, slack


def _token_tile(lp):
    if lp % FLASH_TQ:
        raise ValueError(f"padded length {lp} is not a multiple of {FLASH_TQ}")
    return FLASH_TQ


def kernel(x, meta_tokens, g_mix, g_mlp, w_in_attn, g_cq, w_uq, g_ckv, w_ukv, g_q_mla, g_k_mla,
           g_q_fox, g_k_fox, b_forget, w_out_attn, w_in_conv, conv_w, w_out_conv, w_mlp_up,
           w_mlp_down):
    b, seq, d = x.shape
    assert d == D_MODEL and (PAD + N_META + seq) % BLOCK == 0
    lp = PAD + N_META + seq
    tm = _token_tile(lp)
    bf = jnp.bfloat16

    meta = meta_tokens.astype(x.dtype)
    h = hn = None

    rope_tab = _rope_table(lp)
    tri = (jnp.arange(tm)[:, None] >= jnp.arange(tm)[None, :]).astype(bf)
    selq, selk = _gate_selectors()

    wo_attn, wo_conv, w_conv = w_out_attn.astype(bf), w_out_conv.astype(bf), w_in_conv.astype(bf)
    w_up, w_down = w_mlp_up.astype(bf), w_mlp_down.astype(bf)
    for layer in range(DEPTH):
        j = layer // 2
        gmix = g_mix[layer][None]
        if layer % 2 == 0:
            p = _attn_params(w_in_attn[j], g_cq[j], w_uq[j], g_ckv[j], w_ukv[j], g_q_mla[j],
                             g_k_mla[j], g_q_fox[j], g_k_fox[j], b_forget[j])
            stream = (x, meta) if layer == 0 else (hn,)
            q, k, v, gate_end = _attn_in(stream, lp, gmix, p, rope_tab, tri, selq, selk, tm)
            y = _flash(*_pruning_tables(gate_end, g_q_fox[j], g_k_fox[j]), q, k, v)
            wo = wo_attn
        else:
            cw = jnp.zeros((8, d), jnp.float32).at[0:3].set(conv_w[j])
            y = _conv_in(hn, w_conv, j, cw, tm)
            wo = wo_conv
        gmlp = g_mlp[layer][None]
        gnext = g_mix[min(layer + 1, DEPTH - 1)][None]
        if layer == 0:
            h, hn = _mix_out_mlp_first(x, meta, y, wo, j, gmlp, gnext, w_up, w_down, layer, tm)
        elif layer < DEPTH - 1:
            h, hn = _mix_out_mlp(h.reshape(b * lp, d), y.reshape(b * lp, d), wo, j, gmlp, gnext,
                                 w_up, w_down, layer, tm)
            h, hn = h.reshape(b, lp, d), hn.reshape(b, lp, d)
        else:
            return _mix_out_mlp_last(h, y, wo, j, gmlp, w_up, w_down, layer, tm, seq)
```

```python
import functools

import numpy as np
import jax
import jax.numpy as jnp
from jax import lax
from jax.experimental import pallas as pl
from jax.experimental.pallas import tpu as pltpu

D_MODEL = 1024
DEPTH = 4
N_META = 16
BLOCK = 128
PAD = 2 * BLOCK - N_META
REAL_START = PAD + N_META
REAL_PARTS = 3
MLA_HEADS = 8
MLA_NOPE = 64
MLA_ROPE = 32
MLA_QK = MLA_NOPE + MLA_ROPE
MLA_V = 64
Q_LORA = 384
KV_LORA = 256
ROPE_BASE = 10000.0
FOX_HEADS = 8
FOX_DIM = 64
D_FF = 4 * D_MODEL
EPS = 1e-6
NEG = -1e30

LANE = 128
HEADS = MLA_HEADS + FOX_HEADS
HALF_ROPE = MLA_ROPE // 2
FEATURE_BASE = (0, FOX_DIM)
EXTRA_BASE = (FOX_DIM, 0)
N_SPLIT = 3
ONE_LANE = N_SPLIT * FOX_HEADS
PADROW_LANE = ONE_LANE + 1
FLAG_FOX_OFF = 2 * N_SPLIT
FLAG_MLA = MLA_QK
PAD_KEY = NEG
LOG2E = 1.4426950408889634
MISC_GATE = 0
MISC_ROPE = MLA_NOPE

OFF_CQ = 0
OFF_MISC = OFF_CQ + Q_LORA
OFF_CKV = OFF_MISC + LANE
OFF_FQ = OFF_CKV + KV_LORA
OFF_FK = OFF_FQ + FOX_HEADS * FOX_DIM
OFF_FV = OFF_FK + FOX_HEADS * FOX_DIM
W_CAT = OFF_FV + FOX_HEADS * FOX_DIM

(V_GQ_MLA, V_GQ_MLA_SW, V_GK_MLA, V_GK_MLA_SW, V_ADD_Q_MLA, V_B_FORGET) = range(6)
V_GQ_FOX, V_GK_FOX, V_ONES_V = 6, 8, 10
VEC_ROWS = 16
PAIR = 2 * LANE

FF_CHUNK = 1024
UNDERFLOW_LOG2 = 150.0
FLASH_TQ = 768
FLASH_TK = 256
BLOCKS_PER_BODY = 4
VMEM_LIMIT = 56 * 1024 * 1024


def _const_spec(shape):
    nd = len(shape)
    return pl.BlockSpec(shape, lambda *_: (0,) * nd, pipeline_mode=pl.Buffered(1))


def _layer_spec(shape, layer):
    nd = len(shape)
    return pl.BlockSpec((1,) + shape, lambda *_: (layer,) + (0,) * nd,
                        pipeline_mode=pl.Buffered(1))


def _input_specs(tm):
    part = tm // REAL_PARTS
    assert REAL_START == part

    def part_spec(k):
        return pl.BlockSpec((1, part, D_MODEL),
                            lambda bi, i: (bi, jnp.maximum(REAL_PARTS * i + k - 1, 0), 0))

    return [part_spec(k) for k in range(REAL_PARTS)] + [_const_spec((N_META, D_MODEL))]


def _input_tile(i, x_parts, meta_ref):
    lead = jnp.concatenate([jnp.zeros((PAD, D_MODEL), jnp.float32), meta_ref[...]], axis=0)
    first = jnp.where(i == 0, lead, x_parts[0][0])
    return jnp.concatenate([first] + [r[0] for r in x_parts[1:]], axis=0)


def _rms(x, g, n):
    ms = jnp.sum(x * x, axis=-1, keepdims=True) * (1.0 / n)
    return x * lax.rsqrt(ms + EPS) * g


def _split3(x, extra=0.0):
    hi = x.astype(jnp.bfloat16).astype(jnp.float32)
    r1 = x - hi
    mid = r1.astype(jnp.bfloat16).astype(jnp.float32)
    lo = r1 - mid
    packed = hi + pltpu.roll(mid, FOX_HEADS, 1) + pltpu.roll(lo, 2 * FOX_HEADS, 1)
    return (packed + extra).astype(jnp.bfloat16)


def _dot(a, b):
    return jnp.dot(a, b, preferred_element_type=jnp.float32)


def _attn_in_kernel(*refs, tm, from_x):
    n_stream = REAL_PARTS + 1 if from_x else 1
    stream = refs[:n_stream]
    (gmix_ref, wcat_ref, gcq_ref, wuq_ref, gckv_ref, wkn_ref, wv_ref, vec_ref, rope_ref, tri_ref,
     selq_ref, selk_ref, q_ref, k_ref, v_ref, gate_end_ref, carry_ref) = refs[n_stream:]
    i = pl.program_id(1)

    @pl.when(i == 0)
    def _():
        carry_ref[...] = jnp.zeros_like(carry_ref)

    if from_x:
        x = _input_tile(i, stream[:-1], stream[-1])
        hn = _rms(x, gmix_ref[...], D_MODEL).astype(jnp.bfloat16)
    else:
        hn = stream[0][0]

    def seg(lo, width):
        return _dot(hn, wcat_ref[:, lo:lo + width])

    def vec(r):
        return vec_ref[r:r + 1, :]

    cos_t = rope_ref[:, 0:LANE]
    sin_sw = rope_ref[:, LANE:2 * LANE]
    gc_q, gs_q = vec(V_GQ_MLA) * cos_t, vec(V_GQ_MLA_SW) * sin_sw
    gc_k, gs_k = vec(V_GK_MLA) * cos_t, vec(V_GK_MLA_SW) * sin_sw
    add_q_mla = vec(V_ADD_Q_MLA)

    lane = lax.broadcasted_iota(jnp.int32, (tm, LANE), 1)
    row = lax.broadcasted_iota(jnp.int32, (tm, LANE), 0)
    valid = (i * tm + row) >= PAD
    pad_key = jnp.where(valid, 0.0, PAD_KEY)
    add_k_mla = jnp.where(lane == FLAG_MLA, pad_key, 0.0)
    halves = (lane < FOX_DIM, lane >= FOX_DIM)
    sel_extra = jnp.where(lane == ONE_LANE, 1.0,
                          jnp.where((lane == PADROW_LANE) & ~valid, 1.0, 0.0))

    cq_misc = seg(OFF_CQ, Q_LORA + LANE)
    misc = cq_misc[:, Q_LORA:]
    kpe = jnp.where((lane >= MISC_ROPE) & (lane < MISC_ROPE + MLA_ROPE), misc, 0.0)
    k_rot = jnp.where(lane < MISC_ROPE + HALF_ROPE, pltpu.roll(kpe, LANE - HALF_ROPE, 1),
                      pltpu.roll(kpe, HALF_ROPE, 1)) * gs_k
    xl = misc + vec(V_B_FORGET)
    logf = jnp.minimum(xl, 0.0) - jnp.log1p(jnp.exp(-jnp.abs(xl)))
    logf = jnp.where(valid & (lane >= MISC_GATE) & (lane < MISC_GATE + FOX_HEADS), logf, 0.0)
    cs = _dot(tri_ref[...], _split3(logf))
    cs = (cs + pltpu.roll(cs, LANE - FOX_HEADS, 1)) + pltpu.roll(cs, LANE - 2 * FOX_HEADS, 1)
    cum = jnp.where(lane < FOX_HEADS, cs, 0.0) + carry_ref[0:1, :]
    carry_ref[0:1, :] = cum[tm - 1:tm, :]
    gate_end_ref[0, 0] = jnp.zeros((8, LANE), jnp.float32)
    for c in range(tm // FLASH_TK):
        gate_end_ref[0, 0, c:c + 1, :] = cum[(c + 1) * FLASH_TK - 1:(c + 1) * FLASH_TK, :]
    cum3 = _split3(cum * LOG2E, sel_extra)
    gate_q = _dot(cum3, selq_ref[...])
    gate_k = _dot(cum3, selk_ref[...])

    def inv_norm(sq, n):
        return lax.rsqrt(jnp.sum(sq, axis=-1, keepdims=True) + n * EPS)

    def fox_group(g):
        xq4 = seg(OFF_FQ + g * PAIR, PAIR)
        xk4 = seg(OFF_FK + g * PAIR, PAIR)
        xv4 = seg(OFF_FV + g * PAIR, PAIR)
        for e in range(4):
            hd, par = 4 * g + e, e % 2
            sl = slice((e // 2) * LANE, (e // 2 + 1) * LANE)
            gl = slice(hd * LANE, (hd + 1) * LANE)
            xq, xk = xq4[:, sl], xk4[:, sl]
            rq = inv_norm(jnp.where(halves[par], xq * xq, 0.0), FOX_DIM)
            rk = inv_norm(jnp.where(halves[par], xk * xk, 0.0), FOX_DIM)
            q_ref[0, MLA_HEADS + hd] = (xq * vec(V_GQ_FOX + par) * rq + gate_q[:, gl]
                                        ).astype(jnp.bfloat16)
            k_ref[0, MLA_HEADS + hd] = (xk * vec(V_GK_FOX + par) * rk + gate_k[:, gl]
                                        ).astype(jnp.bfloat16)
            v_ref[0, MLA_HEADS + hd] = jnp.where(halves[par], xv4[:, sl], 1.0
                                                 ).astype(jnp.bfloat16)

    cqn = _rms(cq_misc[:, :Q_LORA], gcq_ref[...], Q_LORA).astype(jnp.bfloat16)
    ckvn = _rms(seg(OFF_CKV, KV_LORA), gckv_ref[...], KV_LORA).astype(jnp.bfloat16)
    def mla_pair(g):
        cols = slice(g * PAIR, (g + 1) * PAIR)
        cols_sw = slice(MLA_HEADS * LANE + g * PAIR, MLA_HEADS * LANE + (g + 1) * PAIR)
        xq2 = _dot(cqn, wuq_ref[:, cols])
        xq2_sw = _dot(cqn, wuq_ref[:, cols_sw])
        xk2 = _dot(ckvn, wkn_ref[:, cols])
        xv2 = _dot(ckvn, wv_ref[:, cols])
        for e in range(2):
            hd, sl = 2 * g + e, slice(e * LANE, (e + 1) * LANE)
            xq = xq2[:, sl]
            q_ref[0, hd] = ((xq * gc_q + xq2_sw[:, sl] * gs_q) * inv_norm(xq * xq, MLA_QK)
                            + add_q_mla).astype(jnp.bfloat16)
            xk = xk2[:, sl] + kpe
            k_ref[0, hd] = ((xk * gc_k + k_rot) * inv_norm(xk * xk, MLA_QK) + add_k_mla
                            ).astype(jnp.bfloat16)
            v_ref[0, hd] = (xv2[:, sl] + vec(V_ONES_V + e)).astype(jnp.bfloat16)

    fox_group(0)
    mla_pair(0)
    mla_pair(1)
    fox_group(1)
    mla_pair(2)
    mla_pair(3)


def _attn_in(stream, lp, gmix, p, rope_tab, tri, selq, selk, tm):
    from_x = len(stream) == 2
    b, d = stream[0].shape[0], D_MODEL
    nt = lp // tm
    kern = functools.partial(_attn_in_kernel, tm=tm, from_x=from_x)
    stream_specs = (_input_specs(tm) if from_x
                    else [pl.BlockSpec((1, tm, d), lambda bi, i: (bi, i, 0))])
    stream_args = [stream[0]] * REAL_PARTS + [stream[1]] if from_x else [stream[0]]
    qk_shape = jax.ShapeDtypeStruct((b, HEADS, lp, LANE), jnp.bfloat16)
    qk_spec = pl.BlockSpec((1, HEADS, tm, LANE), lambda bi, i: (bi, 0, i, 0))
    return pl.pallas_call(
        kern,
        grid=(b, nt),
        in_specs=stream_specs + [
            _const_spec((1, d)),
            _const_spec((d, W_CAT)),
            _const_spec((1, Q_LORA)),
            _const_spec((Q_LORA, 2 * MLA_HEADS * LANE)),
            _const_spec((1, KV_LORA)),
            _const_spec((KV_LORA, MLA_HEADS * LANE)),
            _const_spec((KV_LORA, MLA_HEADS * LANE)),
            _const_spec((VEC_ROWS, LANE)),
            pl.BlockSpec((tm, 2 * LANE), lambda bi, i: (i, 0)),
            _const_spec((tm, tm)),
            _const_spec((LANE, FOX_HEADS * LANE)),
            _const_spec((LANE, FOX_HEADS * LANE)),
        ],
        out_specs=[qk_spec, qk_spec, qk_spec,
                   pl.BlockSpec((1, 1, 8, LANE), lambda bi, i: (bi, i, 0, 0))],
        out_shape=[qk_shape, qk_shape, qk_shape,
                   jax.ShapeDtypeStruct((b, nt, 8, LANE), jnp.float32)],
        scratch_shapes=[pltpu.VMEM((8, LANE), jnp.float32)],
        compiler_params=pltpu.CompilerParams(
            dimension_semantics=("arbitrary", "arbitrary"), vmem_limit_bytes=VMEM_LIMIT),
        name="attn_in",
    )(*stream_args, gmix, p["wcat"], p["gcq"], p["wuq"], p["gckv"], p["wkn"], p["wv"], p["vec"],
      rope_tab, tri, selq, selk)


def _flash_kernel(gate_end_ref, slack_ref, q_ref, k_ref, v_ref, o_ref, m_ref, acc_ref, al_ref,
                  p_ref, *, tq, tk, nq):
    chunks = tq // tk
    assert chunks >= 2
    bi, hp = pl.program_id(0), pl.program_id(1)

    def query_block(qi, carry):
        qbase = qi * tq

        def softmax(j, u, base, diagonal, first=False):
            r0 = u * tk if diagonal else 0
            rows = slice(r0, tq)
            start = pl.multiple_of(base + u * tk, tk)
            q_rows = pl.ds(pl.multiple_of(qbase + r0, tk), tq - r0)
            s = lax.dot_general(q_ref[0, j, q_rows, :], k_ref[0, j, pl.ds(start, tk), :],
                                (((1,), (1,)), ((), ())), preferred_element_type=jnp.float32)
            if diagonal:
                row = lax.broadcasted_iota(jnp.int32, (tq - r0, tk), 0)
                col = lax.broadcasted_iota(jnp.int32, (tq - r0, tk), 1)
                s = jnp.where(col <= row, s, NEG)
            if first:
                m_next = jnp.broadcast_to(jnp.max(s, axis=1, keepdims=True), (tq - r0, LANE))
            else:
                m_prev = m_ref[j, rows, :]
                m_next = jnp.maximum(m_prev, jnp.max(s, axis=1, keepdims=True))
                al_ref[j, u, rows, :] = jnp.exp2(m_prev - m_next)
            p = jnp.exp2(s - jnp.concatenate([m_next] * (tk // LANE), axis=1))
            p_ref[j, u, rows, :] = p.astype(jnp.bfloat16)
            m_ref[j, rows, :] = m_next

        def pv(j, u, base, diagonal, first=False):
            r0 = u * tk if diagonal else 0
            rows = slice(r0, tq)
            start = pl.multiple_of(base + u * tk, tk)
            new = _dot(p_ref[j, u, rows, :], v_ref[0, j, pl.ds(start, tk), :])
            if first:
                acc_ref[j, rows, :] = new
            else:
                acc_ref[j, rows, :] = acc_ref[j, rows, :] * al_ref[j, u, rows, :] + new

        def diagonal_block():
            for u in range(chunks):
                softmax(0, u, qbase, True, first=(u == 0))
                if u > 0:
                    pv(1, u - 1, qbase, True, first=(u == 1))
                softmax(1, u, qbase, True, first=(u == 0))
                pv(0, u, qbase, True, first=(u == 0))
            r0 = (chunks - 1) * tk
            al_ref[1, chunks - 1, 0:r0, :] = jnp.ones((r0, LANE), jnp.float32)
            p_ref[1, chunks - 1, 0:r0, :] = jnp.zeros((r0, tk), jnp.bfloat16)

        def block(kb, diagonal, start=0):
            base = kb * tq
            for u in range(start, chunks):
                softmax(0, u, base, diagonal)
                if u == start:
                    pv(1, chunks - 1, jnp.where(kb == first_kb, qbase, (kb - 1) * tq), False)
                else:
                    pv(1, u - 1, base, diagonal)
                softmax(1, u, base, diagonal)
                pv(0, u, base, diagonal)

        def chunks_needed(j):
            hd = 2 * hp + j
            gate_q = gate_end_ref[bi, hd, jnp.maximum(chunks * qi - 1, 0)]
            count = jnp.int32(0)
            for c in range(chunks * (nq - 1)):
                keep = (c < chunks * qi) & (gate_q - gate_end_ref[bi, hd, c] >= slack_ref[hd])
                count = count + keep.astype(jnp.int32)
            return count

        diagonal_block()
        n_chunks = jnp.maximum(chunks_needed(0), chunks_needed(1))
        partial = n_chunks % chunks
        n_full = n_chunks // chunks
        first = qi - n_full
        first_kb = first - (partial != 0).astype(jnp.int32)
        for start in range(1, chunks):

            @pl.when(partial == chunks - start)
            def _():
                block(first - 1, False, start)

        rem = n_full & (BLOCKS_PER_BODY - 1)

        @pl.when((rem & 1) == 1)
        def _():
            block(first, False)

        @pl.when((rem & 2) == 2)
        def _():
            kb = first + (rem & 1)
            block(kb, False)
            block(kb + 1, False)

        def body(group, c):
            kb = first + rem + BLOCKS_PER_BODY * group
            for d in range(BLOCKS_PER_BODY):
                block(kb + d, False)
            return c

        lax.fori_loop(0, n_full // BLOCKS_PER_BODY, body, 0)
        pv(1, chunks - 1, jnp.where(n_chunks == 0, qbase, (qi - 1) * tq), False)

        o0 = acc_ref[0]
        o1 = acc_ref[1]
        o0 = o0 / pltpu.roll(o0, FOX_DIM, 1)
        o1 = o1 / pltpu.roll(o1, FOX_DIM, 1)
        lane = lax.broadcasted_iota(jnp.int32, (tq, LANE), 1)
        o = jnp.where(lane < MLA_V, o0, o1)
        o_ref[0, pl.ds(pl.multiple_of(qbase, tq), tq), :] = o.astype(jnp.bfloat16)
        return carry

    lax.fori_loop(0, nq, query_block, 0)


def _flash(gate_end, slack, q, k, v):
    b, _, lp, _ = q.shape
    tq, tk = FLASH_TQ, FLASH_TK
    kern = functools.partial(_flash_kernel, tq=tq, tk=tk, nq=lp // tq)
    qkv_spec = pl.BlockSpec((1, 2, lp, LANE), lambda bi, hp: (bi, hp, 0, 0))
    return pl.pallas_call(
        kern,
        grid=(b, HEADS // 2),
        in_specs=[pl.BlockSpec(memory_space=pltpu.SMEM), pl.BlockSpec(memory_space=pltpu.SMEM),
                  qkv_spec, qkv_spec, qkv_spec],
        out_specs=pl.BlockSpec((1, lp, LANE), lambda bi, hp: (bi, 0, hp)),
        out_shape=jax.ShapeDtypeStruct((b, lp, HEADS * MLA_V), jnp.bfloat16),
        scratch_shapes=[pltpu.VMEM((2, tq, LANE), jnp.float32)] * 2
        + [pltpu.VMEM((2, tq // tk, tq, LANE), jnp.float32),
           pltpu.VMEM((2, tq // tk, tq, tk), jnp.bfloat16)],
        compiler_params=pltpu.CompilerParams(
            dimension_semantics=("arbitrary", "arbitrary"), vmem_limit_bytes=VMEM_LIMIT),
        name="flash",
    )(gate_end, slack, q, k, v)


def _conv_in_kernel(hn_ref, win_ref, cw_ref, y_ref, gs_ref, *, tm):
    i = pl.program_id(1)

    @pl.when(i == 0)
    def _():
        gs_ref[0:8, :] = jnp.zeros((8, D_MODEL), jnp.float32)

    hn = hn_ref[0]
    gate_c = _dot(hn, win_ref[0, :, D_MODEL:2 * D_MODEL])
    u = _dot(hn, win_ref[0, :, 2 * D_MODEL:3 * D_MODEL])
    row = lax.broadcasted_iota(jnp.int32, (tm, D_MODEL), 0)
    g = jnp.where((i * tm + row) >= PAD, gate_c * u, 0.0)
    gs_ref[8:tm + 8, :] = g
    y = (cw_ref[0:1, :] * gs_ref[6:tm + 6, :] + cw_ref[1:2, :] * gs_ref[7:tm + 7, :]
         + cw_ref[2:3, :] * g)
    gs_ref[0:8, :] = gs_ref[tm:tm + 8, :]
    gate_b = _dot(hn, win_ref[0, :, 0:D_MODEL])
    y_ref[0] = (gate_b * y).astype(jnp.bfloat16)


def _conv_in(hn, win, layer, cw, tm):
    b, lp, d = hn.shape
    kern = functools.partial(_conv_in_kernel, tm=tm)
    return pl.pallas_call(
        kern,
        grid=(b, lp // tm),
        in_specs=[
            pl.BlockSpec((1, tm, d), lambda bi, i: (bi, i, 0)),
            _layer_spec((d, 3 * d), layer),
            _const_spec((8, d)),
        ],
        out_specs=pl.BlockSpec((1, tm, d), lambda bi, i: (bi, i, 0)),
        out_shape=jax.ShapeDtypeStruct((b, lp, d), jnp.bfloat16),
        scratch_shapes=[pltpu.VMEM((tm + 8, d), jnp.float32)],
        compiler_params=pltpu.CompilerParams(
            dimension_semantics=("arbitrary", "arbitrary"), vmem_limit_bytes=VMEM_LIMIT),
        name="conv_in",
    )(hn, win, cw)


def _mlp_tile(h, y, wo_ref, gmlp_ref, wup_ref, wdn_ref):
    h1 = h + _dot(y, wo_ref[0])
    n = _rms(h1, gmlp_ref[...], D_MODEL).astype(jnp.bfloat16)
    acc = h1
    for c in range(D_FF // FF_CHUNK):
        sl = slice(c * FF_CHUNK, (c + 1) * FF_CHUNK)
        a = jnp.maximum(_dot(n, wup_ref[0, :, sl]), 0.0)
        acc = acc + _dot((a * a).astype(jnp.bfloat16), wdn_ref[0, sl, :])
    return acc


def _mix_out_mlp_kernel(h_ref, y_ref, wo_ref, gmlp_ref, gnext_ref, wup_ref, wdn_ref,
                        out_ref, hn_ref):
    out = _mlp_tile(h_ref[...], y_ref[...], wo_ref, gmlp_ref, wup_ref, wdn_ref)
    out_ref[...] = out
    hn_ref[...] = _rms(out, gnext_ref[...], D_MODEL).astype(jnp.bfloat16)


def _mix_out_mlp(h, y, wo, wo_layer, gmlp, gnext, wup, wdn, layer, tm):
    r, d = h.shape
    return pl.pallas_call(
        _mix_out_mlp_kernel,
        grid=(r // tm,),
        in_specs=[
            pl.BlockSpec((tm, d), lambda i: (i, 0)),
            pl.BlockSpec((tm, d), lambda i: (i, 0)),
            _layer_spec((d, d), wo_layer),
            _const_spec((1, d)),
            _const_spec((1, d)),
            _layer_spec((d, D_FF), layer),
            _layer_spec((D_FF, d), layer),
        ],
        out_specs=[pl.BlockSpec((tm, d), lambda i: (i, 0))] * 2,
        out_shape=[jax.ShapeDtypeStruct((r, d), jnp.float32),
                   jax.ShapeDtypeStruct((r, d), jnp.bfloat16)],
        compiler_params=pltpu.CompilerParams(
            dimension_semantics=("arbitrary",), vmem_limit_bytes=VMEM_LIMIT),
        name="mix_out_mlp",
    )(h, y, wo, gmlp, gnext, wup, wdn)


def _mix_out_mlp_first_kernel(*refs):
    x_parts, meta_ref = refs[0:REAL_PARTS], refs[REAL_PARTS]
    y_ref, wo_ref, gmlp_ref, gnext_ref, wup_ref, wdn_ref, out_ref, hn_ref = refs[REAL_PARTS + 1:]
    h = _input_tile(pl.program_id(1), x_parts, meta_ref)
    out = _mlp_tile(h, y_ref[0], wo_ref, gmlp_ref, wup_ref, wdn_ref)
    out_ref[0] = out
    hn_ref[0] = _rms(out, gnext_ref[...], D_MODEL).astype(jnp.bfloat16)


def _mix_out_mlp_first(x, meta, y, wo, wo_layer, gmlp, gnext, wup, wdn, layer, tm):
    b, lp, d = y.shape
    tile = pl.BlockSpec((1, tm, d), lambda bi, i: (bi, i, 0))
    return pl.pallas_call(
        _mix_out_mlp_first_kernel,
        grid=(b, lp // tm),
        in_specs=_input_specs(tm) + [
            tile,
            _layer_spec((d, d), wo_layer),
            _const_spec((1, d)),
            _const_spec((1, d)),
            _layer_spec((d, D_FF), layer),
            _layer_spec((D_FF, d), layer),
        ],
        out_specs=[tile, tile],
        out_shape=[jax.ShapeDtypeStruct((b, lp, d), jnp.float32),
                   jax.ShapeDtypeStruct((b, lp, d), jnp.bfloat16)],
        compiler_params=pltpu.CompilerParams(
            dimension_semantics=("arbitrary", "arbitrary"), vmem_limit_bytes=VMEM_LIMIT),
        name="mix_out_mlp_first",
    )(*([x] * REAL_PARTS), meta, y, wo, gmlp, gnext, wup, wdn)


def _mix_out_mlp_last_kernel(*refs):
    h_parts, y_parts = refs[0:REAL_PARTS], refs[REAL_PARTS:2 * REAL_PARTS]
    wo_ref, gmlp_ref, wup_ref, wdn_ref, out_ref = refs[2 * REAL_PARTS:]
    h = jnp.concatenate([r[0] for r in h_parts], axis=0)
    y = jnp.concatenate([r[0] for r in y_parts], axis=0)
    out_ref[0] = _mlp_tile(h, y, wo_ref, gmlp_ref, wup_ref, wdn_ref)


def _mix_out_mlp_last(h, y, wo, wo_layer, gmlp, wup, wdn, layer, tm, seq):
    b, lp, d = h.shape
    part = tm // REAL_PARTS
    last_part = lp // part - 1

    def part_spec(k):
        return pl.BlockSpec(
            (1, part, d),
            lambda bi, i: (bi, jnp.minimum(REAL_START // part + REAL_PARTS * i + k, last_part), 0))

    parts = [part_spec(k) for k in range(REAL_PARTS)]
    return pl.pallas_call(
        _mix_out_mlp_last_kernel,
        grid=(b, pl.cdiv(seq, tm)),
        in_specs=parts + parts + [
            _layer_spec((d, d), wo_layer),
            _const_spec((1, d)),
            _layer_spec((d, D_FF), layer),
            _layer_spec((D_FF, d), layer),
        ],
        out_specs=pl.BlockSpec((1, tm, d), lambda bi, i: (bi, i, 0)),
        out_shape=jax.ShapeDtypeStruct((b, seq, d), jnp.float32),
        compiler_params=pltpu.CompilerParams(
            dimension_semantics=("arbitrary", "arbitrary"), vmem_limit_bytes=VMEM_LIMIT),
        name="mix_out_mlp_last",
    )(*([h] * REAL_PARTS + [y] * REAL_PARTS), wo, gmlp, wup, wdn)


def _pad_heads(w, heads, dim):
    k = w.shape[0]
    w = w.reshape(k, heads, dim)
    w = jnp.pad(w, ((0, 0), (0, 0), (0, LANE - dim)))
    return w.reshape(k, heads * LANE)


def _lane_vec(v, offset=0):
    return jnp.zeros((LANE,), jnp.float32).at[offset:offset + v.shape[0]].set(v)


def _attn_params(w_in, g_cq, w_uq, g_ckv, w_ukv, g_q_mla, g_k_mla, g_q_fox, g_k_fox, b_forget):
    bf = jnp.bfloat16
    o1 = Q_LORA
    o2 = o1 + KV_LORA
    o3 = o2 + MLA_ROPE
    o4 = o3 + FOX_HEADS * FOX_DIM
    o5 = o4 + FOX_HEADS * FOX_DIM
    o6 = o5 + FOX_HEADS * FOX_DIM
    misc = jnp.zeros((D_MODEL, LANE), jnp.float32)
    misc = misc.at[:, MISC_GATE:MISC_GATE + FOX_HEADS].set(w_in[:, o6:])
    misc = misc.at[:, MISC_ROPE:MISC_ROPE + MLA_ROPE].set(w_in[:, o2:o3])
    wcat = jnp.concatenate([w_in[:, :o1], misc, w_in[:, o1:o2], w_in[:, o3:o6]], axis=1).astype(bf)
    kv = w_ukv.reshape(KV_LORA, MLA_HEADS, MLA_NOPE + MLA_V)
    wkn = _pad_heads(kv[:, :, :MLA_NOPE].reshape(KV_LORA, -1), MLA_HEADS, MLA_NOPE).astype(bf)
    wv = jnp.pad(kv[:, :, MLA_NOPE:].reshape(KV_LORA, MLA_HEADS // 2, 2, MLA_V),
                 ((0, 0), (0, 0), (0, 0), (0, LANE - MLA_V)))
    wv = jnp.concatenate([wv[:, :, 0], jnp.roll(wv[:, :, 1], MLA_V, axis=-1)], axis=-1)
    wv = wv.reshape(KV_LORA, MLA_HEADS * LANE).astype(bf)
    lo, mid, hi = MLA_NOPE, MLA_NOPE + HALF_ROPE, MLA_NOPE + MLA_ROPE
    uq = w_uq.reshape(Q_LORA, MLA_HEADS, MLA_QK)
    uq_sw = jnp.zeros((Q_LORA, MLA_HEADS, LANE), jnp.float32)
    uq_sw = uq_sw.at[:, :, lo:mid].set(uq[:, :, mid:hi]).at[:, :, mid:hi].set(uq[:, :, lo:mid])
    wuq = jnp.concatenate([_pad_heads(w_uq, MLA_HEADS, MLA_QK),
                           uq_sw.reshape(Q_LORA, MLA_HEADS * LANE)], axis=1).astype(bf)

    def swapped(g):
        return jnp.zeros((LANE,), jnp.float32).at[lo:mid].set(g[mid:hi]).at[mid:hi].set(g[lo:mid])

    zero = jnp.zeros((LANE,), jnp.float32)
    rows = [zero] * VEC_ROWS
    rows[V_GQ_MLA] = _lane_vec(g_q_mla) * LOG2E
    rows[V_GQ_MLA_SW] = swapped(g_q_mla) * LOG2E
    rows[V_GK_MLA] = _lane_vec(g_k_mla) * MLA_QK ** 0.5
    rows[V_GK_MLA_SW] = swapped(g_k_mla) * MLA_QK ** 0.5
    for par in range(2):
        feat, extra = FEATURE_BASE[par], EXTRA_BASE[par]
        rows[V_GQ_FOX + par] = _lane_vec(g_q_fox, feat) * LOG2E
        rows[V_GK_FOX + par] = _lane_vec(g_k_fox, feat) * FOX_DIM ** 0.5
        rows[V_ONES_V + par] = zero.at[extra:extra + FOX_DIM].set(1.0)
    rows[V_B_FORGET] = _lane_vec(b_forget, MISC_GATE)
    rows[V_ADD_Q_MLA] = zero.at[FLAG_MLA].set(1.0)
    vec = jnp.stack(rows)
    return dict(wcat=wcat, gcq=g_cq[None], wuq=wuq, gckv=g_ckv[None], wkn=wkn, wv=wv, vec=vec)


def _gate_selectors():
    selq = np.zeros((LANE, FOX_HEADS * LANE), np.float32)
    selk = np.zeros((LANE, FOX_HEADS * LANE), np.float32)
    for hd in range(FOX_HEADS):
        extra = hd * LANE + EXTRA_BASE[hd % 2]
        for part in range(N_SPLIT):
            selq[part * FOX_HEADS + hd, extra + part] = 1.0
            selk[part * FOX_HEADS + hd, extra + N_SPLIT + part] = -1.0
            selq[ONE_LANE, extra + N_SPLIT + part] = 1.0
            selk[ONE_LANE, extra + part] = 1.0
        selq[ONE_LANE, extra + FLAG_FOX_OFF] = 1.0
        selk[PADROW_LANE, extra + FLAG_FOX_OFF] = PAD_KEY
    return jnp.asarray(selq, jnp.bfloat16), jnp.asarray(selk, jnp.bfloat16)


def _rope_table(lp):
    lane = jnp.arange(LANE, dtype=jnp.int32)
    rotary = (lane >= MLA_NOPE) & (lane < MLA_NOPE + MLA_ROPE)
    first_half = rotary & (lane < MLA_NOPE + HALF_ROPE)
    pair = ((lane - MLA_NOPE) % HALF_ROPE).astype(jnp.float32)
    inv_freq = ROPE_BASE ** (-(2.0 * pair) / MLA_ROPE)
    pos = (jnp.arange(lp, dtype=jnp.int32) - PAD).astype(jnp.float32)
    ang = pos[:, None] * inv_freq[None, :]
    cos_t = jnp.where(lane < MLA_NOPE, 1.0, jnp.where(rotary, jnp.cos(ang), 0.0))
    sin_sw = jnp.where(rotary, jnp.where(first_half, -jnp.sin(ang), jnp.sin(ang)), 0.0)
    return jnp.concatenate([cos_t, sin_sw], axis=1)


def _pruning_tables(gate_end, g_q, g_k):
    b, nt = gate_end.shape[:2]
    per_tile = FLASH_TQ // FLASH_TK
    fox = gate_end[:, :, 0:per_tile, MISC_GATE:MISC_GATE + FOX_HEADS] * LOG2E
    fox = jnp.transpose(fox.reshape(b, nt * per_tile, FOX_HEADS), (0, 2, 1))
    table = jnp.concatenate([jnp.zeros((b, MLA_HEADS, nt * per_tile), jnp.float32), fox], axis=1)
    bound = 1.02 * FOX_DIM * (FOX_DIM ** -0.5 * LOG2E) * jnp.max(jnp.abs(g_q)) * jnp.max(jnp.abs(g_k))
    slack_fox = -(2.0 * bound + UNDERFLOW_LOG2 + 4.0)
    slack = jnp.concatenate([jnp.full((MLA_HEADS,), NEG, jnp.float32),
                             jnp.full((FOX_HEADS,), slack_fox, jnp.float32)])
    return table, slack


def _token_tile(lp):
    if lp % FLASH_TQ:
        raise ValueError(f"padded length {lp} is not a multiple of {FLASH_TQ}")
    return FLASH_TQ


def kernel(x, meta_tokens, g_mix, g_mlp, w_in_attn, g_cq, w_uq, g_ckv, w_ukv, g_q_mla, g_k_mla,
           g_q_fox, g_k_fox, b_forget, w_out_attn, w_in_conv, conv_w, w_out_conv, w_mlp_up,
           w_mlp_down):
    b, seq, d = x.shape
    assert d == D_MODEL and (PAD + N_META + seq) % BLOCK == 0
    lp = PAD + N_META + seq
    tm = _token_tile(lp)
    bf = jnp.bfloat16

    meta = meta_tokens.astype(x.dtype)
    h = hn = None

    rope_tab = _rope_table(lp)
    tri = (jnp.arange(tm)[:, None] >= jnp.arange(tm)[None, :]).astype(bf)
    selq, selk = _gate_selectors()

    wo_attn, wo_conv, w_conv = w_out_attn.astype(bf), w_out_conv.astype(bf), w_in_conv.astype(bf)
    w_up, w_down = w_mlp_up.astype(bf), w_mlp_down.astype(bf)
    for layer in range(DEPTH):
        j = layer // 2
        gmix = g_mix[layer][None]
        if layer % 2 == 0:
            p = _attn_params(w_in_attn[j], g_cq[j], w_uq[j], g_ckv[j], w_ukv[j], g_q_mla[j],
                             g_k_mla[j], g_q_fox[j], g_k_fox[j], b_forget[j])
            stream = (x, meta) if layer == 0 else (hn,)
            q, k, v, gate_end = _attn_in(stream, lp, gmix, p, rope_tab, tri, selq, selk, tm)
            y = _flash(*_pruning_tables(gate_end, g_q_fox[j], g_k_fox[j]), q, k, v)
            wo = wo_attn
        else:
            cw = jnp.zeros((8, d), jnp.float32).at[0:3].set(conv_w[j])
            y = _conv_in(hn, w_conv, j, cw, tm)
            wo = wo_conv
        gmlp = g_mlp[layer][None]
        gnext = g_mix[min(layer + 1, DEPTH - 1)][None]
        if layer == 0:
            h, hn = _mix_out_mlp_first(x, meta, y, wo, j, gmlp, gnext, w_up, w_down, layer, tm)
        elif layer < DEPTH - 1:
            h, hn = _mix_out_mlp(h.reshape(b * lp, d), y.reshape(b * lp, d), wo, j, gmlp, gnext,
                                 w_up, w_down, layer, tm)
            h, hn = h.reshape(b, lp, d), hn.reshape(b, lp, d)
        else:
            return _mix_out_mlp_last(h, y, wo, j, gmlp, w_up, w_down, layer, tm, seq)
```

```python
import functools

import numpy as np
import jax
import jax.numpy as jnp
from jax import lax
from jax.experimental import pallas as pl
from jax.experimental.pallas import tpu as pltpu

D_MODEL = 1024
DEPTH = 4
N_META = 16
BLOCK = 128
PAD = 2 * BLOCK - N_META
REAL_START = PAD + N_META
REAL_PARTS = 3
MLA_HEADS = 8
MLA_NOPE = 64
MLA_ROPE = 32
MLA_QK = MLA_NOPE + MLA_ROPE
MLA_V = 64
Q_LORA = 384
KV_LORA = 256
ROPE_BASE = 10000.0
FOX_HEADS = 8
FOX_DIM = 64
D_FF = 4 * D_MODEL
EPS = 1e-6
NEG = -1e30

LANE = 128
HEADS = MLA_HEADS + FOX_HEADS
HALF_ROPE = MLA_ROPE // 2
FEATURE_BASE = (0, FOX_DIM)
EXTRA_BASE = (FOX_DIM, 0)
N_SPLIT = 3
ONE_LANE = N_SPLIT * FOX_HEADS
PADROW_LANE = ONE_LANE + 1
FLAG_FOX_OFF = 2 * N_SPLIT
FLAG_MLA = MLA_QK
PAD_KEY = NEG
LOG2E = 1.4426950408889634
MISC_GATE = 0
MISC_ROPE = MLA_NOPE

OFF_CQ = 0
OFF_MISC = OFF_CQ + Q_LORA
OFF_CKV = OFF_MISC + LANE
OFF_FQ = OFF_CKV + KV_LORA
OFF_FK = OFF_FQ + FOX_HEADS * FOX_DIM
OFF_FV = OFF_FK + FOX_HEADS * FOX_DIM
W_CAT = OFF_FV + FOX_HEADS * FOX_DIM

(V_GQ_MLA, V_GQ_MLA_SW, V_GK_MLA, V_GK_MLA_SW, V_ADD_Q_MLA, V_B_FORGET) = range(6)
V_GQ_FOX, V_GK_FOX, V_ONES_V = 6, 8, 10
VEC_ROWS = 16
PAIR = 2 * LANE

FF_CHUNK = 1024
UNDERFLOW_LOG2 = 150.0
FLASH_TQ = 768
FLASH_TK = 256
BLOCKS_PER_BODY = 4
VMEM_LIMIT = 56 * 1024 * 1024


def _const_spec(shape):
    nd = len(shape)
    return pl.BlockSpec(shape, lambda *_: (0,) * nd, pipeline_mode=pl.Buffered(1))


def _layer_spec(shape, layer):
    nd = len(shape)
    return pl.BlockSpec((1,) + shape, lambda *_: (layer,) + (0,) * nd,
                        pipeline_mode=pl.Buffered(1))


def _input_specs(tm):
    part = tm // REAL_PARTS
    assert REAL_START == part

    def part_spec(k):
        return pl.BlockSpec((1, part, D_MODEL),
                            lambda bi, i: (bi, jnp.maximum(REAL_PARTS * i + k - 1, 0), 0))

    return [part_spec(k) for k in range(REAL_PARTS)] + [_const_spec((N_META, D_MODEL))]


def _input_tile(i, x_parts, meta_ref):
    lead = jnp.concatenate([jnp.zeros((PAD, D_MODEL), jnp.float32), meta_ref[...]], axis=0)
    first = jnp.where(i == 0, lead, x_parts[0][0])
    return jnp.concatenate([first] + [r[0] for r in x_parts[1:]], axis=0)


def _rms(x, g, n):
    ms = jnp.sum(x * x, axis=-1, keepdims=True) * (1.0 / n)
    return x * lax.rsqrt(ms + EPS) * g


def _split3(x, extra=0.0):
    hi = x.astype(jnp.bfloat16).astype(jnp.float32)
    r1 = x - hi
    mid = r1.astype(jnp.bfloat16).astype(jnp.float32)
    lo = r1 - mid
    packed = hi + pltpu.roll(mid, FOX_HEADS, 1) + pltpu.roll(lo, 2 * FOX_HEADS, 1)
    return (packed + extra).astype(jnp.bfloat16)


def _dot(a, b):
    return jnp.dot(a, b, preferred_element_type=jnp.float32)


def _attn_in_kernel(*refs, tm, from_x):
    n_stream = REAL_PARTS + 1 if from_x else 1
    stream = refs[:n_stream]
    (gmix_ref, wcat_ref, gcq_ref, wuq_ref, gckv_ref, wkn_ref, wv_ref, vec_ref, rope_ref, tri_ref,
     selq_ref, selk_ref, q_ref, k_ref, v_ref, gate_end_ref, carry_ref) = refs[n_stream:]
    i = pl.program_id(1)

    @pl.when(i == 0)
    def _():
        carry_ref[...] = jnp.zeros_like(carry_ref)

    if from_x:
        x = _input_tile(i, stream[:-1], stream[-1])
        hn = _rms(x, gmix_ref[...], D_MODEL).astype(jnp.bfloat16)
    else:
        hn = stream[0][0]

    def seg(lo, width):
        return _dot(hn, wcat_ref[:, lo:lo + width])

    def vec(r):
        return vec_ref[r:r + 1, :]

    cos_t = rope_ref[:, 0:LANE]
    sin_sw = rope_ref[:, LANE:2 * LANE]
    gc_q, gs_q = vec(V_GQ_MLA) * cos_t, vec(V_GQ_MLA_SW) * sin_sw
    gc_k, gs_k = vec(V_GK_MLA) * cos_t, vec(V_GK_MLA_SW) * sin_sw
    add_q_mla = vec(V_ADD_Q_MLA)

    lane = lax.broadcasted_iota(jnp.int32, (tm, LANE), 1)
    row = lax.broadcasted_iota(jnp.int32, (tm, LANE), 0)
    valid = (i * tm + row) >= PAD
    pad_key = jnp.where(valid, 0.0, PAD_KEY)
    add_k_mla = jnp.where(lane == FLAG_MLA, pad_key, 0.0)
    halves = (lane < FOX_DIM, lane >= FOX_DIM)
    sel_extra = jnp.where(lane == ONE_LANE, 1.0,
                          jnp.where((lane == PADROW_LANE) & ~valid, 1.0, 0.0))

    cq_misc = seg(OFF_CQ, Q_LORA + LANE)
    misc = cq_misc[:, Q_LORA:]
    kpe = jnp.where((lane >= MISC_ROPE) & (lane < MISC_ROPE + MLA_ROPE), misc, 0.0)
    k_rot = jnp.where(lane < MISC_ROPE + HALF_ROPE, pltpu.roll(kpe, LANE - HALF_ROPE, 1),
                      pltpu.roll(kpe, HALF_ROPE, 1)) * gs_k
    xl = misc + vec(V_B_FORGET)
    logf = jnp.minimum(xl, 0.0) - jnp.log1p(jnp.exp(-jnp.abs(xl)))
    logf = jnp.where(valid & (lane >= MISC_GATE) & (lane < MISC_GATE + FOX_HEADS), logf, 0.0)
    cs = _dot(tri_ref[...], _split3(logf))
    cs = (cs + pltpu.roll(cs, LANE - FOX_HEADS, 1)) + pltpu.roll(cs, LANE - 2 * FOX_HEADS, 1)
    cum = jnp.where(lane < FOX_HEADS, cs, 0.0) + carry_ref[0:1, :]
    carry_ref[0:1, :] = cum[tm - 1:tm, :]
    gate_end_ref[0, 0] = jnp.zeros((8, LANE), jnp.float32)
    for c in range(tm // FLASH_TK):
        gate_end_ref[0, 0, c:c + 1, :] = cum[(c + 1) * FLASH_TK - 1:(c + 1) * FLASH_TK, :]
    cum3 = _split3(cum * LOG2E, sel_extra)
    gate_q = _dot(cum3, selq_ref[...])
    gate_k = _dot(cum3, selk_ref[...])

    def inv_norm(sq, n):
        return lax.rsqrt(jnp.sum(sq, axis=-1, keepdims=True) + n * EPS)

    def fox_group(g):
        xq4 = seg(OFF_FQ + g * PAIR, PAIR)
        xk4 = seg(OFF_FK + g * PAIR, PAIR)
        xv4 = seg(OFF_FV + g * PAIR, PAIR)
        for e in range(4):
            hd, par = 4 * g + e, e % 2
            sl = slice((e // 2) * LANE, (e // 2 + 1) * LANE)
            gl = slice(hd * LANE, (hd + 1) * LANE)
            xq, xk = xq4[:, sl], xk4[:, sl]
            rq = inv_norm(jnp.where(halves[par], xq * xq, 0.0), FOX_DIM)
            rk = inv_norm(jnp.where(halves[par], xk * xk, 0.0), FOX_DIM)
            q_ref[0, MLA_HEADS + hd] = (xq * vec(V_GQ_FOX + par) * rq + gate_q[:, gl]
                                        ).astype(jnp.bfloat16)
            k_ref[0, MLA_HEADS + hd] = (xk * vec(V_GK_FOX + par) * rk + gate_k[:, gl]
                                        ).astype(jnp.bfloat16)
            v_ref[0, MLA_HEADS + hd] = jnp.where(halves[par], xv4[:, sl], 1.0
                                                 ).astype(jnp.bfloat16)

    cqn = _rms(cq_misc[:, :Q_LORA], gcq_ref[...], Q_LORA).astype(jnp.bfloat16)
    ckvn = _rms(seg(OFF_CKV, KV_LORA), gckv_ref[...], KV_LORA).astype(jnp.bfloat16)
    def mla_pair(g):
        cols = slice(g * PAIR, (g + 1) * PAIR)
        cols_sw = slice(MLA_HEADS * LANE + g * PAIR, MLA_HEADS * LANE + (g + 1) * PAIR)
        xq2 = _dot(cqn, wuq_ref[:, cols])
        xq2_sw = _dot(cqn, wuq_ref[:, cols_sw])
        xk2 = _dot(ckvn, wkn_ref[:, cols])
        xv2 = _dot(ckvn, wv_ref[:, cols])
        for e in range(2):
            hd, sl = 2 * g + e, slice(e * LANE, (e + 1) * LANE)
            xq = xq2[:, sl]
            q_ref[0, hd] = ((xq * gc_q + xq2_sw[:, sl] * gs_q) * inv_norm(xq * xq, MLA_QK)
                            + add_q_mla).astype(jnp.bfloat16)
            xk = xk2[:, sl] + kpe
            k_ref[0, hd] = ((xk * gc_k + k_rot) * inv_norm(xk * xk, MLA_QK) + add_k_mla
                            ).astype(jnp.bfloat16)
            v_ref[0, hd] = (xv2[:, sl] + vec(V_ONES_V + e)).astype(jnp.bfloat16)

    fox_group(0)
    mla_pair(0)
    mla_pair(1)
    fox_group(1)
    mla_pair(2)
    mla_pair(3)


def _attn_in(stream, lp, gmix, p, rope_tab, tri, selq, selk, tm):
    from_x = len(stream) == 2
    b, d = stream[0].shape[0], D_MODEL
    nt = lp // tm
    kern = functools.partial(_attn_in_kernel, tm=tm, from_x=from_x)
    stream_specs = (_input_specs(tm) if from_x
                    else [pl.BlockSpec((1, tm, d), lambda bi, i: (bi, i, 0))])
    stream_args = [stream[0]] * REAL_PARTS + [stream[1]] if from_x else [stream[0]]
    qk_shape = jax.ShapeDtypeStruct((b, HEADS, lp, LANE), jnp.bfloat16)
    qk_spec = pl.BlockSpec((1, HEADS, tm, LANE), lambda bi, i: (bi, 0, i, 0))
    return pl.pallas_call(
        kern,
        grid=(b, nt),
        in_specs=stream_specs + [
            _const_spec((1, d)),
            _const_spec((d, W_CAT)),
            _const_spec((1, Q_LORA)),
            _const_spec((Q_LORA, 2 * MLA_HEADS * LANE)),
            _const_spec((1, KV_LORA)),
            _const_spec((KV_LORA, MLA_HEADS * LANE)),
            _const_spec((KV_LORA, MLA_HEADS * LANE)),
            _const_spec((VEC_ROWS, LANE)),
            pl.BlockSpec((tm, 2 * LANE), lambda bi, i: (i, 0)),
            _const_spec((tm, tm)),
            _const_spec((LANE, FOX_HEADS * LANE)),
            _const_spec((LANE, FOX_HEADS * LANE)),
        ],
        out_specs=[qk_spec, qk_spec, qk_spec,
                   pl.BlockSpec((1, 1, 8, LANE), lambda bi, i: (bi, i, 0, 0))],
        out_shape=[qk_shape, qk_shape, qk_shape,
                   jax.ShapeDtypeStruct((b, nt, 8, LANE), jnp.float32)],
        scratch_shapes=[pltpu.VMEM((8, LANE), jnp.float32)],
        compiler_params=pltpu.CompilerParams(
            dimension_semantics=("arbitrary", "arbitrary"), vmem_limit_bytes=VMEM_LIMIT),
        name="attn_in",
    )(*stream_args, gmix, p["wcat"], p["gcq"], p["wuq"], p["gckv"], p["wkn"], p["wv"], p["vec"],
      rope_tab, tri, selq, selk)


def _flash_kernel(gate_end_ref, slack_ref, q_ref, k_ref, v_ref, o_ref, m_ref, acc_ref, al_ref,
                  p_ref, *, tq, tk, nq):
    chunks = tq // tk
    assert chunks >= 2
    bi, hp = pl.program_id(0), pl.program_id(1)

    def finalize(qb):
        o0 = acc_ref[0]
        o1 = acc_ref[1]
        o0 = o0 / pltpu.roll(o0, FOX_DIM, 1)
        o1 = o1 / pltpu.roll(o1, FOX_DIM, 1)
        lane = lax.broadcasted_iota(jnp.int32, (tq, LANE), 1)
        o = jnp.where(lane < MLA_V, o0, o1)
        o_ref[0, pl.ds(pl.multiple_of(qb * tq, tq), tq), :] = o.astype(jnp.bfloat16)

    def query_block(qi, carry):
        qbase = qi * tq
        finalize(jnp.maximum(qi - 1, 0))

        def softmax(j, u, base, diagonal, first=False):
            r0 = u * tk if diagonal else 0
            rows = slice(r0, tq)
            start = pl.multiple_of(base + u * tk, tk)
            q_rows = pl.ds(pl.multiple_of(qbase + r0, tk), tq - r0)
            s = lax.dot_general(q_ref[0, j, q_rows, :], k_ref[0, j, pl.ds(start, tk), :],
                                (((1,), (1,)), ((), ())), preferred_element_type=jnp.float32)
            if diagonal:
                row = lax.broadcasted_iota(jnp.int32, (tq - r0, tk), 0)
                col = lax.broadcasted_iota(jnp.int32, (tq - r0, tk), 1)
                s = jnp.where(col <= row, s, NEG)
            if first:
                m_next = jnp.broadcast_to(jnp.max(s, axis=1, keepdims=True), (tq - r0, LANE))
            else:
                m_prev = m_ref[j, rows, :]
                m_next = jnp.maximum(m_prev, jnp.max(s, axis=1, keepdims=True))
                al_ref[j, u, rows, :] = jnp.exp2(m_prev - m_next)
            p = jnp.exp2(s - jnp.concatenate([m_next] * (tk // LANE), axis=1))
            p_ref[j, u, rows, :] = p.astype(jnp.bfloat16)
            m_ref[j, rows, :] = m_next

        def pv(j, u, base, diagonal, first=False):
            r0 = u * tk if diagonal else 0
            rows = slice(r0, tq)
            start = pl.multiple_of(base + u * tk, tk)
            new = _dot(p_ref[j, u, rows, :], v_ref[0, j, pl.ds(start, tk), :])
            if first:
                acc_ref[j, rows, :] = new
            else:
                acc_ref[j, rows, :] = acc_ref[j, rows, :] * al_ref[j, u, rows, :] + new

        def diagonal_block():
            for u in range(chunks):
                softmax(0, u, qbase, True, first=(u == 0))
                if u > 0:
                    pv(1, u - 1, qbase, True, first=(u == 1))
                softmax(1, u, qbase, True, first=(u == 0))
                pv(0, u, qbase, True, first=(u == 0))
            r0 = (chunks - 1) * tk
            al_ref[1, chunks - 1, 0:r0, :] = jnp.ones((r0, LANE), jnp.float32)
            p_ref[1, chunks - 1, 0:r0, :] = jnp.zeros((r0, tk), jnp.bfloat16)

        def block(kb, diagonal, start=0):
            base = kb * tq
            for u in range(start, chunks):
                softmax(0, u, base, diagonal)
                if u == start:
                    pv(1, chunks - 1, jnp.where(kb == first_kb, qbase, (kb - 1) * tq), False)
                else:
                    pv(1, u - 1, base, diagonal)
                softmax(1, u, base, diagonal)
                pv(0, u, base, diagonal)

        def chunks_needed(j):
            hd = 2 * hp + j
            gate_q = gate_end_ref[bi, hd, jnp.maximum(chunks * qi - 1, 0)]
            count = jnp.int32(0)
            for c in range(chunks * (nq - 1)):
                keep = (c < chunks * qi) & (gate_q - gate_end_ref[bi, hd, c] >= slack_ref[hd])
                count = count + keep.astype(jnp.int32)
            return count

        diagonal_block()
        n_chunks = jnp.maximum(chunks_needed(0), chunks_needed(1))
        partial = n_chunks % chunks
        n_full = n_chunks // chunks
        first = qi - n_full
        first_kb = first - (partial != 0).astype(jnp.int32)
        for start in range(1, chunks):

            @pl.when(partial == chunks - start)
            def _():
                block(first - 1, False, start)

        rem = n_full & (BLOCKS_PER_BODY - 1)

        @pl.when((rem & 1) == 1)
        def _():
            block(first, False)

        @pl.when((rem & 2) == 2)
        def _():
            kb = first + (rem & 1)
            block(kb, False)
            block(kb + 1, False)

        def body(group, c):
            kb = first + rem + BLOCKS_PER_BODY * group
            for d in range(BLOCKS_PER_BODY):
                block(kb + d, False)
            return c

        lax.fori_loop(0, n_full // BLOCKS_PER_BODY, body, 0)
        pv(1, chunks - 1, jnp.where(n_chunks == 0, qbase, (qi - 1) * tq), False)

        return carry

    acc_ref[...] = jnp.ones(acc_ref.shape, jnp.float32)
    lax.fori_loop(0, nq, query_block, 0)
    finalize(nq - 1)


def _flash(gate_end, slack, q, k, v):
    b, _, lp, _ = q.shape
    tq, tk = FLASH_TQ, FLASH_TK
    kern = functools.partial(_flash_kernel, tq=tq, tk=tk, nq=lp // tq)
    qkv_spec = pl.BlockSpec((1, 2, lp, LANE), lambda bi, hp: (bi, hp, 0, 0))
    return pl.pallas_call(
        kern,
        grid=(b, HEADS // 2),
        in_specs=[pl.BlockSpec(memory_space=pltpu.SMEM), pl.BlockSpec(memory_space=pltpu.SMEM),
                  qkv_spec, qkv_spec, qkv_spec],
        out_specs=pl.BlockSpec((1, lp, LANE), lambda bi, hp: (bi, 0, hp)),
        out_shape=jax.ShapeDtypeStruct((b, lp, HEADS * MLA_V), jnp.bfloat16),
        scratch_shapes=[pltpu.VMEM((2, tq, LANE), jnp.float32)] * 2
        + [pltpu.VMEM((2, tq // tk, tq, LANE), jnp.float32),
           pltpu.VMEM((2, tq // tk, tq, tk), jnp.bfloat16)],
        compiler_params=pltpu.CompilerParams(
            dimension_semantics=("arbitrary", "arbitrary"), vmem_limit_bytes=VMEM_LIMIT),
        name="flash",
    )(gate_end, slack, q, k, v)


def _conv_in_kernel(hn_ref, win_ref, cw_ref, y_ref, gs_ref, *, tm):
    i = pl.program_id(1)

    @pl.when(i == 0)
    def _():
        gs_ref[0:8, :] = jnp.zeros((8, D_MODEL), jnp.float32)

    hn = hn_ref[0]
    gate_c = _dot(hn, win_ref[0, :, D_MODEL:2 * D_MODEL])
    u = _dot(hn, win_ref[0, :, 2 * D_MODEL:3 * D_MODEL])
    row = lax.broadcasted_iota(jnp.int32, (tm, D_MODEL), 0)
    g = jnp.where((i * tm + row) >= PAD, gate_c * u, 0.0)
    gs_ref[8:tm + 8, :] = g
    y = (cw_ref[0:1, :] * gs_ref[6:tm + 6, :] + cw_ref[1:2, :] * gs_ref[7:tm + 7, :]
         + cw_ref[2:3, :] * g)
    gs_ref[0:8, :] = gs_ref[tm:tm + 8, :]
    gate_b = _dot(hn, win_ref[0, :, 0:D_MODEL])
    y_ref[0] = (gate_b * y).astype(jnp.bfloat16)


def _conv_in(hn, win, layer, cw, tm):
    b, lp, d = hn.shape
    kern = functools.partial(_conv_in_kernel, tm=tm)
    return pl.pallas_call(
        kern,
        grid=(b, lp // tm),
        in_specs=[
            pl.BlockSpec((1, tm, d), lambda bi, i: (bi, i, 0)),
            _layer_spec((d, 3 * d), layer),
            _const_spec((8, d)),
        ],
        out_specs=pl.BlockSpec((1, tm, d), lambda bi, i: (bi, i, 0)),
        out_shape=jax.ShapeDtypeStruct((b, lp, d), jnp.bfloat16),
        scratch_shapes=[pltpu.VMEM((tm + 8, d), jnp.float32)],
        compiler_params=pltpu.CompilerParams(
            dimension_semantics=("arbitrary", "arbitrary"), vmem_limit_bytes=VMEM_LIMIT),
        name="conv_in",
    )(hn, win, cw)


def _mlp_tile(h, y, wo_ref, gmlp_ref, wup_ref, wdn_ref):
    h1 = h + _dot(y, wo_ref[0])
    n = _rms(h1, gmlp_ref[...], D_MODEL).astype(jnp.bfloat16)
    acc = h1
    for c in range(D_FF // FF_CHUNK):
        sl = slice(c * FF_CHUNK, (c + 1) * FF_CHUNK)
        a = jnp.maximum(_dot(n, wup_ref[0, :, sl]), 0.0)
        acc = acc + _dot((a * a).astype(jnp.bfloat16), wdn_ref[0, sl, :])
    return acc


def _mix_out_mlp_kernel(h_ref, y_ref, wo_ref, gmlp_ref, gnext_ref, wup_ref, wdn_ref,
                        out_ref, hn_ref):
    out = _mlp_tile(h_ref[...], y_ref[...], wo_ref, gmlp_ref, wup_ref, wdn_ref)
    out_ref[...] = out
    hn_ref[...] = _rms(out, gnext_ref[...], D_MODEL).astype(jnp.bfloat16)


def _mix_out_mlp(h, y, wo, wo_layer, gmlp, gnext, wup, wdn, layer, tm):
    r, d = h.shape
    return pl.pallas_call(
        _mix_out_mlp_kernel,
        grid=(r // tm,),
        in_specs=[
            pl.BlockSpec((tm, d), lambda i: (i, 0)),
            pl.BlockSpec((tm, d), lambda i: (i, 0)),
            _layer_spec((d, d), wo_layer),
            _const_spec((1, d)),
            _const_spec((1, d)),
            _layer_spec((d, D_FF), layer),
            _layer_spec((D_FF, d), layer),
        ],
        out_specs=[pl.BlockSpec((tm, d), lambda i: (i, 0))] * 2,
        out_shape=[jax.ShapeDtypeStruct((r, d), jnp.float32),
                   jax.ShapeDtypeStruct((r, d), jnp.bfloat16)],
        compiler_params=pltpu.CompilerParams(
            dimension_semantics=("arbitrary",), vmem_limit_bytes=VMEM_LIMIT),
        name="mix_out_mlp",
    )(h, y, wo, gmlp, gnext, wup, wdn)


def _mix_out_mlp_first_kernel(*refs):
    x_parts, meta_ref = refs[0:REAL_PARTS], refs[REAL_PARTS]
    y_ref, wo_ref, gmlp_ref, gnext_ref, wup_ref, wdn_ref, out_ref, hn_ref = refs[REAL_PARTS + 1:]
    h = _input_tile(pl.program_id(1), x_parts, meta_ref)
    out = _mlp_tile(h, y_ref[0], wo_ref, gmlp_ref, wup_ref, wdn_ref)
    out_ref[0] = out
    hn_ref[0] = _rms(out, gnext_ref[...], D_MODEL).astype(jnp.bfloat16)


def _mix_out_mlp_first(x, meta, y, wo, wo_layer, gmlp, gnext, wup, wdn, layer, tm):
    b, lp, d = y.shape
    tile = pl.BlockSpec((1, tm, d), lambda bi, i: (bi, i, 0))
    return pl.pallas_call(
        _mix_out_mlp_first_kernel,
        grid=(b, lp // tm),
        in_specs=_input_specs(tm) + [
            tile,
            _layer_spec((d, d), wo_layer),
            _const_spec((1, d)),
            _const_spec((1, d)),
            _layer_spec((d, D_FF), layer),
            _layer_spec((D_FF, d), layer),
        ],
        out_specs=[tile, tile],
        out_shape=[jax.ShapeDtypeStruct((b, lp, d), jnp.float32),
                   jax.ShapeDtypeStruct((b, lp, d), jnp.bfloat16)],
        compiler_params=pltpu.CompilerParams(
            dimension_semantics=("arbitrary", "arbitrary"), vmem_limit_bytes=VMEM_LIMIT),
        name="mix_out_mlp_first",
    )(*([x] * REAL_PARTS), meta, y, wo, gmlp, gnext, wup, wdn)


def _mix_out_mlp_last_kernel(*refs):
    h_parts, y_parts = refs[0:REAL_PARTS], refs[REAL_PARTS:2 * REAL_PARTS]
    wo_ref, gmlp_ref, wup_ref, wdn_ref, out_ref = refs[2 * REAL_PARTS:]
    h = jnp.concatenate([r[0] for r in h_parts], axis=0)
    y = jnp.concatenate([r[0] for r in y_parts], axis=0)
    out_ref[0] = _mlp_tile(h, y, wo_ref, gmlp_ref, wup_ref, wdn_ref)


def _mix_out_mlp_last(h, y, wo, wo_layer, gmlp, wup, wdn, layer, tm, seq):
    b, lp, d = h.shape
    part = tm // REAL_PARTS
    last_part = lp // part - 1

    def part_spec(k):
        return pl.BlockSpec(
            (1, part, d),
            lambda bi, i: (bi, jnp.minimum(REAL_START // part + REAL_PARTS * i + k, last_part), 0))

    parts = [part_spec(k) for k in range(REAL_PARTS)]
    return pl.pallas_call(
        _mix_out_mlp_last_kernel,
        grid=(b, pl.cdiv(seq, tm)),
        in_specs=parts + parts + [
            _layer_spec((d, d), wo_layer),
            _const_spec((1, d)),
            _layer_spec((d, D_FF), layer),
            _layer_spec((D_FF, d), layer),
        ],
        out_specs=pl.BlockSpec((1, tm, d), lambda bi, i: (bi, i, 0)),
        out_shape=jax.ShapeDtypeStruct((b, seq, d), jnp.float32),
        compiler_params=pltpu.CompilerParams(
            dimension_semantics=("arbitrary", "arbitrary"), vmem_limit_bytes=VMEM_LIMIT),
        name="mix_out_mlp_last",
    )(*([h] * REAL_PARTS + [y] * REAL_PARTS), wo, gmlp, wup, wdn)


def _pad_heads(w, heads, dim):
    k = w.shape[0]
    w = w.reshape(k, heads, dim)
    w = jnp.pad(w, ((0, 0), (0, 0), (0, LANE - dim)))
    return w.reshape(k, heads * LANE)


def _lane_vec(v, offset=0):
    return jnp.zeros((LANE,), jnp.float32).at[offset:offset + v.shape[0]].set(v)


def _attn_params(w_in, g_cq, w_uq, g_ckv, w_ukv, g_q_mla, g_k_mla, g_q_fox, g_k_fox, b_forget):
    bf = jnp.bfloat16
    o1 = Q_LORA
    o2 = o1 + KV_LORA
    o3 = o2 + MLA_ROPE
    o4 = o3 + FOX_HEADS * FOX_DIM
    o5 = o4 + FOX_HEADS * FOX_DIM
    o6 = o5 + FOX_HEADS * FOX_DIM
    misc = jnp.zeros((D_MODEL, LANE), jnp.float32)
    misc = misc.at[:, MISC_GATE:MISC_GATE + FOX_HEADS].set(w_in[:, o6:])
    misc = misc.at[:, MISC_ROPE:MISC_ROPE + MLA_ROPE].set(w_in[:, o2:o3])
    wcat = jnp.concatenate([w_in[:, :o1], misc, w_in[:, o1:o2], w_in[:, o3:o6]], axis=1).astype(bf)
    kv = w_ukv.reshape(KV_LORA, MLA_HEADS, MLA_NOPE + MLA_V)
    wkn = _pad_heads(kv[:, :, :MLA_NOPE].reshape(KV_LORA, -1), MLA_HEADS, MLA_NOPE).astype(bf)
    wv = jnp.pad(kv[:, :, MLA_NOPE:].reshape(KV_LORA, MLA_HEADS // 2, 2, MLA_V),
                 ((0, 0), (0, 0), (0, 0), (0, LANE - MLA_V)))
    wv = jnp.concatenate([wv[:, :, 0], jnp.roll(wv[:, :, 1], MLA_V, axis=-1)], axis=-1)
    wv = wv.reshape(KV_LORA, MLA_HEADS * LANE).astype(bf)
    lo, mid, hi = MLA_NOPE, MLA_NOPE + HALF_ROPE, MLA_NOPE + MLA_ROPE
    uq = w_uq.reshape(Q_LORA, MLA_HEADS, MLA_QK)
    uq_sw = jnp.zeros((Q_LORA, MLA_HEADS, LANE), jnp.float32)
    uq_sw = uq_sw.at[:, :, lo:mid].set(uq[:, :, mid:hi]).at[:, :, mid:hi].set(uq[:, :, lo:mid])
    wuq = jnp.concatenate([_pad_heads(w_uq, MLA_HEADS, MLA_QK),
                           uq_sw.reshape(Q_LORA, MLA_HEADS * LANE)], axis=1).astype(bf)

    def swapped(g):
        return jnp.zeros((LANE,), jnp.float32).at[lo:mid].set(g[mid:hi]).at[mid:hi].set(g[lo:mid])

    zero = jnp.zeros((LANE,), jnp.float32)
    rows = [zero] * VEC_ROWS
    rows[V_GQ_MLA] = _lane_vec(g_q_mla) * LOG2E
    rows[V_GQ_MLA_SW] = swapped(g_q_mla) * LOG2E
    rows[V_GK_MLA] = _lane_vec(g_k_mla) * MLA_QK ** 0.5
    rows[V_GK_MLA_SW] = swapped(g_k_mla) * MLA_QK ** 0.5
    for par in range(2):
        feat, extra = FEATURE_BASE[par], EXTRA_BASE[par]
        rows[V_GQ_FOX + par] = _lane_vec(g_q_fox, feat) * LOG2E
        rows[V_GK_FOX + par] = _lane_vec(g_k_fox, feat) * FOX_DIM ** 0.5
        rows[V_ONES_V + par] = zero.at[extra:extra + FOX_DIM].set(1.0)
    rows[V_B_FORGET] = _lane_vec(b_forget, MISC_GATE)
    rows[V_ADD_Q_MLA] = zero.at[FLAG_MLA].set(1.0)
    vec = jnp.stack(rows)
    return dict(wcat=wcat, gcq=g_cq[None], wuq=wuq, gckv=g_ckv[None], wkn=wkn, wv=wv, vec=vec)


def _gate_selectors():
    selq = np.zeros((LANE, FOX_HEADS * LANE), np.float32)
    selk = np.zeros((LANE, FOX_HEADS * LANE), np.float32)
    for hd in range(FOX_HEADS):
        extra = hd * LANE + EXTRA_BASE[hd % 2]
        for part in range(N_SPLIT):
            selq[part * FOX_HEADS + hd, extra + part] = 1.0
            selk[part * FOX_HEADS + hd, extra + N_SPLIT + part] = -1.0
            selq[ONE_LANE, extra + N_SPLIT + part] = 1.0
            selk[ONE_LANE, extra + part] = 1.0
        selq[ONE_LANE, extra + FLAG_FOX_OFF] = 1.0
        selk[PADROW_LANE, extra + FLAG_FOX_OFF] = PAD_KEY
    return jnp.asarray(selq, jnp.bfloat16), jnp.asarray(selk, jnp.bfloat16)


def _rope_table(lp):
    lane = jnp.arange(LANE, dtype=jnp.int32)
    rotary = (lane >= MLA_NOPE) & (lane < MLA_NOPE + MLA_ROPE)
    first_half = rotary & (lane < MLA_NOPE + HALF_ROPE)
    pair = ((lane - MLA_NOPE) % HALF_ROPE).astype(jnp.float32)
    inv_freq = ROPE_BASE ** (-(2.0 * pair) / MLA_ROPE)
    pos = (jnp.arange(lp, dtype=jnp.int32) - PAD).astype(jnp.float32)
    ang = pos[:, None] * inv_freq[None, :]
    cos_t = jnp.where(lane < MLA_NOPE, 1.0, jnp.where(rotary, jnp.cos(ang), 0.0))
    sin_sw = jnp.where(rotary, jnp.where(first_half, -jnp.sin(ang), jnp.sin(ang)), 0.0)
    return jnp.concatenate([cos_t, sin_sw], axis=1)


def _pruning_tables(gate_end, g_q, g_k):
    b, nt = gate_end.shape[:2]
    per_tile = FLASH_TQ // FLASH_TK
    fox = gate_end[:, :, 0:per_tile, MISC_GATE:MISC_GATE + FOX_HEADS] * LOG2E
    fox = jnp.transpose(fox.reshape(b, nt * per_tile, FOX_HEADS), (0, 2, 1))
    table = jnp.concatenate([jnp.zeros((b, MLA_HEADS, nt * per_tile), jnp.float32), fox], axis=1)
    bound = 1.02 * FOX_DIM * (FOX_DIM ** -0.5 * LOG2E) * jnp.max(jnp.abs(g_q)) * jnp.max(jnp.abs(g_k))
    slack_fox = -(2.0 * bound + UNDERFLOW_LOG2 + 4.0)
    slack = jnp.concatenate([jnp.full((MLA_HEADS,), NEG, jnp.float32),
                             jnp.full((FOX_HEADS,), slack_fox, jnp.float32)])
    return table, slack


def _token_tile(lp):
    if lp % FLASH_TQ:
        raise ValueError(f"padded length {lp} is not a multiple of {FLASH_TQ}")
    return FLASH_TQ


def kernel(x, meta_tokens, g_mix, g_mlp, w_in_attn, g_cq, w_uq, g_ckv, w_ukv, g_q_mla, g_k_mla,
           g_q_fox, g_k_fox, b_forget, w_out_attn, w_in_conv, conv_w, w_out_conv, w_mlp_up,
           w_mlp_down):
    b, seq, d = x.shape
    assert d == D_MODEL and (PAD + N_META + seq) % BLOCK == 0
    lp = PAD + N_META + seq
    tm = _token_tile(lp)
    bf = jnp.bfloat16

    meta = meta_tokens.astype(x.dtype)
    h = hn = None

    rope_tab = _rope_table(lp)
    tri = (jnp.arange(tm)[:, None] >= jnp.arange(tm)[None, :]).astype(bf)
    selq, selk = _gate_selectors()

    wo_attn, wo_conv, w_conv = w_out_attn.astype(bf), w_out_conv.astype(bf), w_in_conv.astype(bf)
    w_up, w_down = w_mlp_up.astype(bf), w_mlp_down.astype(bf)
    for layer in range(DEPTH):
        j = layer // 2
        gmix = g_mix[layer][None]
        if layer % 2 == 0:
            p = _attn_params(w_in_attn[j], g_cq[j], w_uq[j], g_ckv[j], w_ukv[j], g_q_mla[j],
                             g_k_mla[j], g_q_fox[j], g_k_fox[j], b_forget[j])
            stream = (x, meta) if layer == 0 else (hn,)
            q, k, v, gate_end = _attn_in(stream, lp, gmix, p, rope_tab, tri, selq, selk, tm)
            y = _flash(*_pruning_tables(gate_end, g_q_fox[j], g_k_fox[j]), q, k, v)
            wo = wo_attn
        else:
            cw = jnp.zeros((8, d), jnp.float32).at[0:3].set(conv_w[j])
            y = _conv_in(hn, w_conv, j, cw, tm)
            wo = wo_conv
        gmlp = g_mlp[layer][None]
        gnext = g_mix[min(layer + 1, DEPTH - 1)][None]
        if layer == 0:
            h, hn = _mix_out_mlp_first(x, meta, y, wo, j, gmlp, gnext, w_up, w_down, layer, tm)
        elif layer < DEPTH - 1:
            h, hn = _mix_out_mlp(h.reshape(b * lp, d), y.reshape(b * lp, d), wo, j, gmlp, gnext,
                                 w_up, w_down, layer, tm)
            h, hn = h.reshape(b, lp, d), hn.reshape(b, lp, d)
        else:
            return _mix_out_mlp_last(h, y, wo, j, gmlp, w_up, w_down, layer, tm, seq)
```

```python
import functools

import numpy as np
import jax
import jax.numpy as jnp
from jax import lax
from jax.experimental import pallas as pl
from jax.experimental.pallas import tpu as pltpu

D_MODEL = 1024
DEPTH = 4
N_META = 16
BLOCK = 128
PAD = 2 * BLOCK - N_META
REAL_START = PAD + N_META
REAL_PARTS = 3
MLA_HEADS = 8
MLA_NOPE = 64
MLA_ROPE = 32
MLA_QK = MLA_NOPE + MLA_ROPE
MLA_V = 64
Q_LORA = 384
KV_LORA = 256
ROPE_BASE = 10000.0
FOX_HEADS = 8
FOX_DIM = 64
D_FF = 4 * D_MODEL
EPS = 1e-6
NEG = -1e30

LANE = 128
HEADS = MLA_HEADS + FOX_HEADS
HALF_ROPE = MLA_ROPE // 2
FEATURE_BASE = (0, FOX_DIM)
EXTRA_BASE = (FOX_DIM, 0)
N_SPLIT = 3
ONE_LANE = N_SPLIT * FOX_HEADS
PADROW_LANE = ONE_LANE + 1
FLAG_FOX_OFF = 2 * N_SPLIT
FLAG_MLA = MLA_QK
PAD_KEY = NEG
LOG2E = 1.4426950408889634
MISC_GATE = 0
MISC_ROPE = MLA_NOPE

OFF_CQ = 0
OFF_MISC = OFF_CQ + Q_LORA
OFF_CKV = OFF_MISC + LANE
OFF_FQ = OFF_CKV + KV_LORA
OFF_FK = OFF_FQ + FOX_HEADS * FOX_DIM
OFF_FV = OFF_FK + FOX_HEADS * FOX_DIM
W_CAT = OFF_FV + FOX_HEADS * FOX_DIM

(V_GQ_MLA, V_GQ_MLA_SW, V_GK_MLA, V_GK_MLA_SW, V_ADD_Q_MLA, V_B_FORGET) = range(6)
V_GQ_FOX, V_GK_FOX, V_ONES_V = 6, 8, 10
VEC_ROWS = 16
PAIR = 2 * LANE

FF_CHUNK = 1024
UNDERFLOW_LOG2 = 150.0
FLASH_TQ = 768
FLASH_TK = 256
BLOCKS_PER_BODY = 4
VMEM_LIMIT = 56 * 1024 * 1024


def _const_spec(shape):
    nd = len(shape)
    return pl.BlockSpec(shape, lambda *_: (0,) * nd, pipeline_mode=pl.Buffered(1))


def _layer_spec(shape, layer):
    nd = len(shape)
    return pl.BlockSpec((1,) + shape, lambda *_: (layer,) + (0,) * nd,
                        pipeline_mode=pl.Buffered(1))


def _input_specs(tm):
    part = tm // REAL_PARTS
    assert REAL_START == part

    def part_spec(k):
        return pl.BlockSpec((1, part, D_MODEL),
                            lambda bi, i: (bi, jnp.maximum(REAL_PARTS * i + k - 1, 0), 0))

    return [part_spec(k) for k in range(REAL_PARTS)] + [_const_spec((N_META, D_MODEL))]


def _input_tile(i, x_parts, meta_ref):
    lead = jnp.concatenate([jnp.zeros((PAD, D_MODEL), jnp.float32), meta_ref[...]], axis=0)
    first = jnp.where(i == 0, lead, x_parts[0][0])
    return jnp.concatenate([first] + [r[0] for r in x_parts[1:]], axis=0)


def _rms(x, g, n):
    ms = jnp.sum(x * x, axis=-1, keepdims=True) * (1.0 / n)
    return x * lax.rsqrt(ms + EPS) * g


def _split3(x, extra=0.0):
    hi = x.astype(jnp.bfloat16).astype(jnp.float32)
    r1 = x - hi
    mid = r1.astype(jnp.bfloat16).astype(jnp.float32)
    lo = r1 - mid
    packed = hi + pltpu.roll(mid, FOX_HEADS, 1) + pltpu.roll(lo, 2 * FOX_HEADS, 1)
    return (packed + extra).astype(jnp.bfloat16)


def _dot(a, b):
    return jnp.dot(a, b, preferred_element_type=jnp.float32)


def _attn_in_kernel(*refs, tm, from_x):
    n_stream = REAL_PARTS + 1 if from_x else 1
    stream = refs[:n_stream]
    (gmix_ref, wcat_ref, gcq_ref, wuq_ref, gckv_ref, wkn_ref, wv_ref, vec_ref, rope_ref, tri_ref,
     selq_ref, selk_ref, q_ref, k_ref, v_ref, gate_end_ref, carry_ref) = refs[n_stream:]
    i = pl.program_id(1)

    @pl.when(i == 0)
    def _():
        carry_ref[...] = jnp.zeros_like(carry_ref)

    if from_x:
        x = _input_tile(i, stream[:-1], stream[-1])
        hn = _rms(x, gmix_ref[...], D_MODEL).astype(jnp.bfloat16)
    else:
        hn = stream[0][0]

    def seg(lo, width):
        return _dot(hn, wcat_ref[:, lo:lo + width])

    def vec(r):
        return vec_ref[r:r + 1, :]

    cos_t = rope_ref[:, 0:LANE]
    sin_sw = rope_ref[:, LANE:2 * LANE]
    gc_q, gs_q = vec(V_GQ_MLA) * cos_t, vec(V_GQ_MLA_SW) * sin_sw
    gc_k, gs_k = vec(V_GK_MLA) * cos_t, vec(V_GK_MLA_SW) * sin_sw
    add_q_mla = vec(V_ADD_Q_MLA)

    lane = lax.broadcasted_iota(jnp.int32, (tm, LANE), 1)
    row = lax.broadcasted_iota(jnp.int32, (tm, LANE), 0)
    valid = (i * tm + row) >= PAD
    pad_key = jnp.where(valid, 0.0, PAD_KEY)
    add_k_mla = jnp.where(lane == FLAG_MLA, pad_key, 0.0)
    halves = (lane < FOX_DIM, lane >= FOX_DIM)
    sel_extra = jnp.where(lane == ONE_LANE, 1.0,
                          jnp.where((lane == PADROW_LANE) & ~valid, 1.0, 0.0))

    cq_misc = seg(OFF_CQ, Q_LORA + LANE)
    misc = cq_misc[:, Q_LORA:]
    kpe = jnp.where((lane >= MISC_ROPE) & (lane < MISC_ROPE + MLA_ROPE), misc, 0.0)
    k_rot = jnp.where(lane < MISC_ROPE + HALF_ROPE, pltpu.roll(kpe, LANE - HALF_ROPE, 1),
                      pltpu.roll(kpe, HALF_ROPE, 1)) * gs_k
    xl = misc + vec(V_B_FORGET)
    logf = jnp.minimum(xl, 0.0) - jnp.log1p(jnp.exp(-jnp.abs(xl)))
    logf = jnp.where(valid & (lane >= MISC_GATE) & (lane < MISC_GATE + FOX_HEADS), logf, 0.0)
    cs = _dot(tri_ref[...], _split3(logf))
    cs = (cs + pltpu.roll(cs, LANE - FOX_HEADS, 1)) + pltpu.roll(cs, LANE - 2 * FOX_HEADS, 1)
    cum = jnp.where(lane < FOX_HEADS, cs, 0.0) + carry_ref[0:1, :]
    carry_ref[0:1, :] = cum[tm - 1:tm, :]
    gate_end_ref[0, 0] = jnp.zeros((8, LANE), jnp.float32)
    for c in range(tm // FLASH_TK):
        gate_end_ref[0, 0, c:c + 1, :] = cum[(c + 1) * FLASH_TK - 1:(c + 1) * FLASH_TK, :]
    cum3 = _split3(cum * LOG2E, sel_extra)
    gate_q = _dot(cum3, selq_ref[...])
    gate_k = _dot(cum3, selk_ref[...])

    def inv_norm(sq, n):
        return lax.rsqrt(jnp.sum(sq, axis=-1, keepdims=True) + n * EPS)

    def fox_group(g):
        xq4 = seg(OFF_FQ + g * PAIR, PAIR)
        xk4 = seg(OFF_FK + g * PAIR, PAIR)
        xv4 = seg(OFF_FV + g * PAIR, PAIR)
        for e in range(4):
            hd, par = 4 * g + e, e % 2
            sl = slice((e // 2) * LANE, (e // 2 + 1) * LANE)
            gl = slice(hd * LANE, (hd + 1) * LANE)
            xq, xk = xq4[:, sl], xk4[:, sl]
            rq = inv_norm(jnp.where(halves[par], xq * xq, 0.0), FOX_DIM)
            rk = inv_norm(jnp.where(halves[par], xk * xk, 0.0), FOX_DIM)
            q_ref[0, MLA_HEADS + hd] = (xq * vec(V_GQ_FOX + par) * rq + gate_q[:, gl]
                                        ).astype(jnp.bfloat16)
            k_ref[0, MLA_HEADS + hd] = (xk * vec(V_GK_FOX + par) * rk + gate_k[:, gl]
                                        ).astype(jnp.bfloat16)
            v_ref[0, MLA_HEADS + hd] = jnp.where(halves[par], xv4[:, sl], 1.0
                                                 ).astype(jnp.bfloat16)

    cqn = _rms(cq_misc[:, :Q_LORA], gcq_ref[...], Q_LORA).astype(jnp.bfloat16)
    ckvn = _rms(seg(OFF_CKV, KV_LORA), gckv_ref[...], KV_LORA).astype(jnp.bfloat16)
    def mla_pair(g):
        cols = slice(g * PAIR, (g + 1) * PAIR)
        cols_sw = slice(MLA_HEADS * LANE + g * PAIR, MLA_HEADS * LANE + (g + 1) * PAIR)
        xq2 = _dot(cqn, wuq_ref[:, cols])
        xq2_sw = _dot(cqn, wuq_ref[:, cols_sw])
        xk2 = _dot(ckvn, wkn_ref[:, cols])
        xv2 = _dot(ckvn, wv_ref[:, cols])
        for e in range(2):
            hd, sl = 2 * g + e, slice(e * LANE, (e + 1) * LANE)
            xq = xq2[:, sl]
            q_ref[0, hd] = ((xq * gc_q + xq2_sw[:, sl] * gs_q) * inv_norm(xq * xq, MLA_QK)
                            + add_q_mla).astype(jnp.bfloat16)
            xk = xk2[:, sl] + kpe
            k_ref[0, hd] = ((xk * gc_k + k_rot) * inv_norm(xk * xk, MLA_QK) + add_k_mla
                            ).astype(jnp.bfloat16)
            v_ref[0, hd] = (xv2[:, sl] + vec(V_ONES_V + e)).astype(jnp.bfloat16)

    fox_group(0)
    mla_pair(0)
    mla_pair(1)
    fox_group(1)
    mla_pair(2)
    mla_pair(3)


def _attn_in(stream, lp, gmix, p, rope_tab, tri, selq, selk, tm):
    from_x = len(stream) == 2
    b, d = stream[0].shape[0], D_MODEL
    nt = lp // tm
    kern = functools.partial(_attn_in_kernel, tm=tm, from_x=from_x)
    stream_specs = (_input_specs(tm) if from_x
                    else [pl.BlockSpec((1, tm, d), lambda bi, i: (bi, i, 0))])
    stream_args = [stream[0]] * REAL_PARTS + [stream[1]] if from_x else [stream[0]]
    qk_shape = jax.ShapeDtypeStruct((b, HEADS, lp, LANE), jnp.bfloat16)
    qk_spec = pl.BlockSpec((1, HEADS, tm, LANE), lambda bi, i: (bi, 0, i, 0))
    return pl.pallas_call(
        kern,
        grid=(b, nt),
        in_specs=stream_specs + [
            _const_spec((1, d)),
            _const_spec((d, W_CAT)),
            _const_spec((1, Q_LORA)),
            _const_spec((Q_LORA, 2 * MLA_HEADS * LANE)),
            _const_spec((1, KV_LORA)),
            _const_spec((KV_LORA, MLA_HEADS * LANE)),
            _const_spec((KV_LORA, MLA_HEADS * LANE)),
            _const_spec((VEC_ROWS, LANE)),
            pl.BlockSpec((tm, 2 * LANE), lambda bi, i: (i, 0)),
            _const_spec((tm, tm)),
            _const_spec((LANE, FOX_HEADS * LANE)),
            _const_spec((LANE, FOX_HEADS * LANE)),
        ],
        out_specs=[qk_spec, qk_spec, qk_spec,
                   pl.BlockSpec((1, 1, 8, LANE), lambda bi, i: (bi, i, 0, 0))],
        out_shape=[qk_shape, qk_shape, qk_shape,
                   jax.ShapeDtypeStruct((b, nt, 8, LANE), jnp.float32)],
        scratch_shapes=[pltpu.VMEM((8, LANE), jnp.float32)],
        compiler_params=pltpu.CompilerParams(
            dimension_semantics=("arbitrary", "arbitrary"), vmem_limit_bytes=VMEM_LIMIT),
        name="attn_in",
    )(*stream_args, gmix, p["wcat"], p["gcq"], p["wuq"], p["gckv"], p["wkn"], p["wv"], p["vec"],
      rope_tab, tri, selq, selk)


def _flash_kernel(need_ref, q_ref, k_ref, v_ref, o_ref, m_ref, acc_ref, al_ref, p_ref, *,
                  tq, tk, nq):
    chunks = tq // tk
    assert chunks >= 2
    bi, hp = pl.program_id(0), pl.program_id(1)

    def query_block(qi, carry):
        qbase = qi * tq

        def softmax(j, u, base, diagonal, first=False):
            r0 = u * tk if diagonal else 0
            rows = slice(r0, tq)
            start = pl.multiple_of(base + u * tk, tk)
            q_rows = pl.ds(pl.multiple_of(qbase + r0, tk), tq - r0)
            s = lax.dot_general(q_ref[0, j, q_rows, :], k_ref[0, j, pl.ds(start, tk), :],
                                (((1,), (1,)), ((), ())), preferred_element_type=jnp.float32)
            if diagonal:
                row = lax.broadcasted_iota(jnp.int32, (tq - r0, tk), 0)
                col = lax.broadcasted_iota(jnp.int32, (tq - r0, tk), 1)
                s = jnp.where(col <= row, s, NEG)
            if first:
                m_next = jnp.broadcast_to(jnp.max(s, axis=1, keepdims=True), (tq - r0, LANE))
            else:
                m_prev = m_ref[j, rows, :]
                m_next = jnp.maximum(m_prev, jnp.max(s, axis=1, keepdims=True))
                al_ref[j, u, rows, :] = jnp.exp2(m_prev - m_next)
            p = jnp.exp2(s - jnp.concatenate([m_next] * (tk // LANE), axis=1))
            p_ref[j, u, rows, :] = p.astype(jnp.bfloat16)
            m_ref[j, rows, :] = m_next

        def pv(j, u, base, diagonal, first=False):
            r0 = u * tk if diagonal else 0
            rows = slice(r0, tq)
            start = pl.multiple_of(base + u * tk, tk)
            new = _dot(p_ref[j, u, rows, :], v_ref[0, j, pl.ds(start, tk), :])
            if first:
                acc_ref[j, rows, :] = new
            else:
                acc_ref[j, rows, :] = acc_ref[j, rows, :] * al_ref[j, u, rows, :] + new

        def diagonal_block():
            for u in range(chunks):
                softmax(0, u, qbase, True, first=(u == 0))
                if u > 0:
                    pv(1, u - 1, qbase, True, first=(u == 1))
                softmax(1, u, qbase, True, first=(u == 0))
                pv(0, u, qbase, True, first=(u == 0))
            r0 = (chunks - 1) * tk
            al_ref[1, chunks - 1, 0:r0, :] = jnp.ones((r0, LANE), jnp.float32)
            p_ref[1, chunks - 1, 0:r0, :] = jnp.zeros((r0, tk), jnp.bfloat16)

        def block(kb, diagonal, start=0):
            base = kb * tq
            for u in range(start, chunks):
                softmax(0, u, base, diagonal)
                if u == start:
                    pv(1, chunks - 1, jnp.where(kb == first_kb, qbase, (kb - 1) * tq), False)
                else:
                    pv(1, u - 1, base, diagonal)
                softmax(1, u, base, diagonal)
                pv(0, u, base, diagonal)

        diagonal_block()
        n_chunks = need_ref[bi, hp, qi]
        partial = n_chunks % chunks
        n_full = n_chunks // chunks
        first = qi - n_full
        first_kb = first - (partial != 0).astype(jnp.int32)
        for start in range(1, chunks):

            @pl.when(partial == chunks - start)
            def _():
                block(first - 1, False, start)

        rem = n_full & (BLOCKS_PER_BODY - 1)

        @pl.when((rem & 1) == 1)
        def _():
            block(first, False)

        @pl.when((rem & 2) == 2)
        def _():
            kb = first + (rem & 1)
            block(kb, False)
            block(kb + 1, False)

        def body(group, c):
            kb = first + rem + BLOCKS_PER_BODY * group
            for d in range(BLOCKS_PER_BODY):
                block(kb + d, False)
            return c

        lax.fori_loop(0, n_full // BLOCKS_PER_BODY, body, 0)
        pv(1, chunks - 1, jnp.where(n_chunks == 0, qbase, (qi - 1) * tq), False)

        o0 = acc_ref[0]
        o1 = acc_ref[1]
        o0 = o0 / pltpu.roll(o0, FOX_DIM, 1)
        o1 = o1 / pltpu.roll(o1, FOX_DIM, 1)
        lane = lax.broadcasted_iota(jnp.int32, (tq, LANE), 1)
        o = jnp.where(lane < MLA_V, o0, o1)
        o_ref[0, pl.ds(pl.multiple_of(qbase, tq), tq), :] = o.astype(jnp.bfloat16)
        return carry

    lax.fori_loop(0, nq, query_block, 0)


def _flash(need, q, k, v):
    b, _, lp, _ = q.shape
    tq, tk = FLASH_TQ, FLASH_TK
    kern = functools.partial(_flash_kernel, tq=tq, tk=tk, nq=lp // tq)
    qkv_spec = pl.BlockSpec((1, 2, lp, LANE), lambda bi, hp: (bi, hp, 0, 0))
    return pl.pallas_call(
        kern,
        grid=(b, HEADS // 2),
        in_specs=[pl.BlockSpec(memory_space=pltpu.SMEM), qkv_spec, qkv_spec, qkv_spec],
        out_specs=pl.BlockSpec((1, lp, LANE), lambda bi, hp: (bi, 0, hp)),
        out_shape=jax.ShapeDtypeStruct((b, lp, HEADS * MLA_V), jnp.bfloat16),
        scratch_shapes=[pltpu.VMEM((2, tq, LANE), jnp.float32)] * 2
        + [pltpu.VMEM((2, tq // tk, tq, LANE), jnp.float32),
           pltpu.VMEM((2, tq // tk, tq, tk), jnp.bfloat16)],
        compiler_params=pltpu.CompilerParams(
            dimension_semantics=("arbitrary", "arbitrary"), vmem_limit_bytes=VMEM_LIMIT),
        name="flash",
    )(need, q, k, v)


def _conv_in_kernel(hn_ref, win_ref, cw_ref, y_ref, gs_ref, *, tm):
    i = pl.program_id(1)

    @pl.when(i == 0)
    def _():
        gs_ref[0:8, :] = jnp.zeros((8, D_MODEL), jnp.float32)

    hn = hn_ref[0]
    gate_c = _dot(hn, win_ref[0, :, D_MODEL:2 * D_MODEL])
    u = _dot(hn, win_ref[0, :, 2 * D_MODEL:3 * D_MODEL])
    row = lax.broadcasted_iota(jnp.int32, (tm, D_MODEL), 0)
    g = jnp.where((i * tm + row) >= PAD, gate_c * u, 0.0)
    gs_ref[8:tm + 8, :] = g
    y = (cw_ref[0:1, :] * gs_ref[6:tm + 6, :] + cw_ref[1:2, :] * gs_ref[7:tm + 7, :]
         + cw_ref[2:3, :] * g)
    gs_ref[0:8, :] = gs_ref[tm:tm + 8, :]
    gate_b = _dot(hn, win_ref[0, :, 0:D_MODEL])
    y_ref[0] = (gate_b * y).astype(jnp.bfloat16)


def _conv_in(hn, win, layer, cw, tm):
    b, lp, d = hn.shape
    kern = functools.partial(_conv_in_kernel, tm=tm)
    return pl.pallas_call(
        kern,
        grid=(b, lp // tm),
        in_specs=[
            pl.BlockSpec((1, tm, d), lambda bi, i: (bi, i, 0)),
            _layer_spec((d, 3 * d), layer),
            _const_spec((8, d)),
        ],
        out_specs=pl.BlockSpec((1, tm, d), lambda bi, i: (bi, i, 0)),
        out_shape=jax.ShapeDtypeStruct((b, lp, d), jnp.bfloat16),
        scratch_shapes=[pltpu.VMEM((tm + 8, d), jnp.float32)],
        compiler_params=pltpu.CompilerParams(
            dimension_semantics=("arbitrary", "arbitrary"), vmem_limit_bytes=VMEM_LIMIT),
        name="conv_in",
    )(hn, win, cw)


def _mlp_tile(h, y, wo_ref, gmlp_ref, wup_ref, wdn_ref):
    h1 = h + _dot(y, wo_ref[0])
    n = _rms(h1, gmlp_ref[...], D_MODEL).astype(jnp.bfloat16)
    acc = h1
    for c in range(D_FF // FF_CHUNK):
        sl = slice(c * FF_CHUNK, (c + 1) * FF_CHUNK)
        a = jnp.maximum(_dot(n, wup_ref[0, :, sl]), 0.0)
        acc = acc + _dot((a * a).astype(jnp.bfloat16), wdn_ref[0, sl, :])
    return acc


def _mix_out_mlp_kernel(h_ref, y_ref, wo_ref, gmlp_ref, gnext_ref, wup_ref, wdn_ref,
                        out_ref, hn_ref):
    out = _mlp_tile(h_ref[...], y_ref[...], wo_ref, gmlp_ref, wup_ref, wdn_ref)
    out_ref[...] = out
    hn_ref[...] = _rms(out, gnext_ref[...], D_MODEL).astype(jnp.bfloat16)


def _mix_out_mlp(h, y, wo, wo_layer, gmlp, gnext, wup, wdn, layer, tm):
    r, d = h.shape
    return pl.pallas_call(
        _mix_out_mlp_kernel,
        grid=(r // tm,),
        in_specs=[
            pl.BlockSpec((tm, d), lambda i: (i, 0)),
            pl.BlockSpec((tm, d), lambda i: (i, 0)),
            _layer_spec((d, d), wo_layer),
            _const_spec((1, d)),
            _const_spec((1, d)),
            _layer_spec((d, D_FF), layer),
            _layer_spec((D_FF, d), layer),
        ],
        out_specs=[pl.BlockSpec((tm, d), lambda i: (i, 0))] * 2,
        out_shape=[jax.ShapeDtypeStruct((r, d), jnp.float32),
                   jax.ShapeDtypeStruct((r, d), jnp.bfloat16)],
        compiler_params=pltpu.CompilerParams(
            dimension_semantics=("arbitrary",), vmem_limit_bytes=VMEM_LIMIT),
        name="mix_out_mlp",
    )(h, y, wo, gmlp, gnext, wup, wdn)


def _mix_out_mlp_first_kernel(*refs):
    x_parts, meta_ref = refs[0:REAL_PARTS], refs[REAL_PARTS]
    y_ref, wo_ref, gmlp_ref, gnext_ref, wup_ref, wdn_ref, out_ref, hn_ref = refs[REAL_PARTS + 1:]
    h = _input_tile(pl.program_id(1), x_parts, meta_ref)
    out = _mlp_tile(h, y_ref[0], wo_ref, gmlp_ref, wup_ref, wdn_ref)
    out_ref[0] = out
    hn_ref[0] = _rms(out, gnext_ref[...], D_MODEL).astype(jnp.bfloat16)


def _mix_out_mlp_first(x, meta, y, wo, wo_layer, gmlp, gnext, wup, wdn, layer, tm):
    b, lp, d = y.shape
    tile = pl.BlockSpec((1, tm, d), lambda bi, i: (bi, i, 0))
    return pl.pallas_call(
        _mix_out_mlp_first_kernel,
        grid=(b, lp // tm),
        in_specs=_input_specs(tm) + [
            tile,
            _layer_spec((d, d), wo_layer),
            _const_spec((1, d)),
            _const_spec((1, d)),
            _layer_spec((d, D_FF), layer),
            _layer_spec((D_FF, d), layer),
        ],
        out_specs=[tile, tile],
        out_shape=[jax.ShapeDtypeStruct((b, lp, d), jnp.float32),
                   jax.ShapeDtypeStruct((b, lp, d), jnp.bfloat16)],
        compiler_params=pltpu.CompilerParams(
            dimension_semantics=("arbitrary", "arbitrary"), vmem_limit_bytes=VMEM_LIMIT),
        name="mix_out_mlp_first",
    )(*([x] * REAL_PARTS), meta, y, wo, gmlp, gnext, wup, wdn)


def _mix_out_mlp_last_kernel(*refs):
    h_parts, y_parts = refs[0:REAL_PARTS], refs[REAL_PARTS:2 * REAL_PARTS]
    wo_ref, gmlp_ref, wup_ref, wdn_ref, out_ref = refs[2 * REAL_PARTS:]
    h = jnp.concatenate([r[0] for r in h_parts], axis=0)
    y = jnp.concatenate([r[0] for r in y_parts], axis=0)
    out_ref[0] = _mlp_tile(h, y, wo_ref, gmlp_ref, wup_ref, wdn_ref)


def _mix_out_mlp_last(h, y, wo, wo_layer, gmlp, wup, wdn, layer, tm, seq):
    b, lp, d = h.shape
    part = tm // REAL_PARTS
    last_part = lp // part - 1

    def part_spec(k):
        return pl.BlockSpec(
            (1, part, d),
            lambda bi, i: (bi, jnp.minimum(REAL_START // part + REAL_PARTS * i + k, last_part), 0))

    parts = [part_spec(k) for k in range(REAL_PARTS)]
    return pl.pallas_call(
        _mix_out_mlp_last_kernel,
        grid=(b, pl.cdiv(seq, tm)),
        in_specs=parts + parts + [
            _layer_spec((d, d), wo_layer),
            _const_spec((1, d)),
            _layer_spec((d, D_FF), layer),
            _layer_spec((D_FF, d), layer),
        ],
        out_specs=pl.BlockSpec((1, tm, d), lambda bi, i: (bi, i, 0)),
        out_shape=jax.ShapeDtypeStruct((b, seq, d), jnp.float32),
        compiler_params=pltpu.CompilerParams(
            dimension_semantics=("arbitrary", "arbitrary"), vmem_limit_bytes=VMEM_LIMIT),
        name="mix_out_mlp_last",
    )(*([h] * REAL_PARTS + [y] * REAL_PARTS), wo, gmlp, wup, wdn)


def _pad_heads(w, heads, dim):
    k = w.shape[0]
    w = w.reshape(k, heads, dim)
    w = jnp.pad(w, ((0, 0), (0, 0), (0, LANE - dim)))
    return w.reshape(k, heads * LANE)


def _lane_vec(v, offset=0):
    return jnp.zeros((LANE,), jnp.float32).at[offset:offset + v.shape[0]].set(v)


def _attn_params(w_in, g_cq, w_uq, g_ckv, w_ukv, g_q_mla, g_k_mla, g_q_fox, g_k_fox, b_forget):
    bf = jnp.bfloat16
    o1 = Q_LORA
    o2 = o1 + KV_LORA
    o3 = o2 + MLA_ROPE
    o4 = o3 + FOX_HEADS * FOX_DIM
    o5 = o4 + FOX_HEADS * FOX_DIM
    o6 = o5 + FOX_HEADS * FOX_DIM
    misc = jnp.zeros((D_MODEL, LANE), jnp.float32)
    misc = misc.at[:, MISC_GATE:MISC_GATE + FOX_HEADS].set(w_in[:, o6:])
    misc = misc.at[:, MISC_ROPE:MISC_ROPE + MLA_ROPE].set(w_in[:, o2:o3])
    wcat = jnp.concatenate([w_in[:, :o1], misc, w_in[:, o1:o2], w_in[:, o3:o6]], axis=1).astype(bf)
    kv = w_ukv.reshape(KV_LORA, MLA_HEADS, MLA_NOPE + MLA_V)
    wkn = _pad_heads(kv[:, :, :MLA_NOPE].reshape(KV_LORA, -1), MLA_HEADS, MLA_NOPE).astype(bf)
    wv = jnp.pad(kv[:, :, MLA_NOPE:].reshape(KV_LORA, MLA_HEADS // 2, 2, MLA_V),
                 ((0, 0), (0, 0), (0, 0), (0, LANE - MLA_V)))
    wv = jnp.concatenate([wv[:, :, 0], jnp.roll(wv[:, :, 1], MLA_V, axis=-1)], axis=-1)
    wv = wv.reshape(KV_LORA, MLA_HEADS * LANE).astype(bf)
    lo, mid, hi = MLA_NOPE, MLA_NOPE + HALF_ROPE, MLA_NOPE + MLA_ROPE
    uq = w_uq.reshape(Q_LORA, MLA_HEADS, MLA_QK)
    uq_sw = jnp.zeros((Q_LORA, MLA_HEADS, LANE), jnp.float32)
    uq_sw = uq_sw.at[:, :, lo:mid].set(uq[:, :, mid:hi]).at[:, :, mid:hi].set(uq[:, :, lo:mid])
    wuq = jnp.concatenate([_pad_heads(w_uq, MLA_HEADS, MLA_QK),
                           uq_sw.reshape(Q_LORA, MLA_HEADS * LANE)], axis=1).astype(bf)

    def swapped(g):
        return jnp.zeros((LANE,), jnp.float32).at[lo:mid].set(g[mid:hi]).at[mid:hi].set(g[lo:mid])

    zero = jnp.zeros((LANE,), jnp.float32)
    rows = [zero] * VEC_ROWS
    rows[V_GQ_MLA] = _lane_vec(g_q_mla) * LOG2E
    rows[V_GQ_MLA_SW] = swapped(g_q_mla) * LOG2E
    rows[V_GK_MLA] = _lane_vec(g_k_mla) * MLA_QK ** 0.5
    rows[V_GK_MLA_SW] = swapped(g_k_mla) * MLA_QK ** 0.5
    for par in range(2):
        feat, extra = FEATURE_BASE[par], EXTRA_BASE[par]
        rows[V_GQ_FOX + par] = _lane_vec(g_q_fox, feat) * LOG2E
        rows[V_GK_FOX + par] = _lane_vec(g_k_fox, feat) * FOX_DIM ** 0.5
        rows[V_ONES_V + par] = zero.at[extra:extra + FOX_DIM].set(1.0)
    rows[V_B_FORGET] = _lane_vec(b_forget, MISC_GATE)
    rows[V_ADD_Q_MLA] = zero.at[FLAG_MLA].set(1.0)
    vec = jnp.stack(rows)
    return dict(wcat=wcat, gcq=g_cq[None], wuq=wuq, gckv=g_ckv[None], wkn=wkn, wv=wv, vec=vec)


def _gate_selectors():
    selq = np.zeros((LANE, FOX_HEADS * LANE), np.float32)
    selk = np.zeros((LANE, FOX_HEADS * LANE), np.float32)
    for hd in range(FOX_HEADS):
        extra = hd * LANE + EXTRA_BASE[hd % 2]
        for part in range(N_SPLIT):
            selq[part * FOX_HEADS + hd, extra + part] = 1.0
            selk[part * FOX_HEADS + hd, extra + N_SPLIT + part] = -1.0
            selq[ONE_LANE, extra + N_SPLIT + part] = 1.0
            selk[ONE_LANE, extra + part] = 1.0
        selq[ONE_LANE, extra + FLAG_FOX_OFF] = 1.0
        selk[PADROW_LANE, extra + FLAG_FOX_OFF] = PAD_KEY
    return jnp.asarray(selq, jnp.bfloat16), jnp.asarray(selk, jnp.bfloat16)


def _rope_table(lp):
    lane = jnp.arange(LANE, dtype=jnp.int32)
    rotary = (lane >= MLA_NOPE) & (lane < MLA_NOPE + MLA_ROPE)
    first_half = rotary & (lane < MLA_NOPE + HALF_ROPE)
    pair = ((lane - MLA_NOPE) % HALF_ROPE).astype(jnp.float32)
    inv_freq = ROPE_BASE ** (-(2.0 * pair) / MLA_ROPE)
    pos = (jnp.arange(lp, dtype=jnp.int32) - PAD).astype(jnp.float32)
    ang = pos[:, None] * inv_freq[None, :]
    cos_t = jnp.where(lane < MLA_NOPE, 1.0, jnp.where(rotary, jnp.cos(ang), 0.0))
    sin_sw = jnp.where(rotary, jnp.where(first_half, -jnp.sin(ang), jnp.sin(ang)), 0.0)
    return jnp.concatenate([cos_t, sin_sw], axis=1)


def _pruning_tables(gate_end, g_q, g_k):
    b, nt = gate_end.shape[:2]
    per_tile = FLASH_TQ // FLASH_TK
    n_chunk = nt * per_tile
    fox = gate_end[:, :, 0:per_tile, MISC_GATE:MISC_GATE + FOX_HEADS] * LOG2E
    fox = jnp.transpose(fox.reshape(b, n_chunk, FOX_HEADS), (0, 2, 1))
    bound = 1.02 * FOX_DIM * (FOX_DIM ** -0.5 * LOG2E) * jnp.max(jnp.abs(g_q)) * jnp.max(jnp.abs(g_k))
    slack = -(2.0 * bound + UNDERFLOW_LOG2 + 4.0)
    before = per_tile * jnp.arange(nt, dtype=jnp.int32)
    gate_q = fox[:, :, jnp.maximum(before - 1, 0)]
    chunk = jnp.arange(n_chunk, dtype=jnp.int32)
    keep = ((chunk[None, :] < before[:, None])[None, None]
            & (gate_q[..., None] - fox[:, :, None, :] >= slack))
    need_fox = jnp.sum(keep, axis=-1, dtype=jnp.int32)
    need = jnp.concatenate(
        [jnp.broadcast_to(before[None, None], (b, MLA_HEADS, nt)), need_fox], axis=1)
    return jnp.max(need.reshape(b, HEADS // 2, 2, nt), axis=2)


def _token_tile(lp):
    if lp % FLASH_TQ:
        raise ValueError(f"padded length {lp} is not a multiple of {FLASH_TQ}")
    return FLASH_TQ


def kernel(x, meta_tokens, g_mix, g_mlp, w_in_attn, g_cq, w_uq, g_ckv, w_ukv, g_q_mla, g_k_mla,
           g_q_fox, g_k_fox, b_forget, w_out_attn, w_in_conv, conv_w, w_out_conv, w_mlp_up,
           w_mlp_down):
    b, seq, d = x.shape
    assert d == D_MODEL and (PAD + N_META + seq) % BLOCK == 0
    lp = PAD + N_META + seq
    tm = _token_tile(lp)
    bf = jnp.bfloat16

    meta = meta_tokens.astype(x.dtype)
    h = hn = None

    rope_tab = _rope_table(lp)
    tri = (jnp.arange(tm)[:, None] >= jnp.arange(tm)[None, :]).astype(bf)
    selq, selk = _gate_selectors()

    wo_attn, wo_conv, w_conv = w_out_attn.astype(bf), w_out_conv.astype(bf), w_in_conv.astype(bf)
    w_up, w_down = w_mlp_up.astype(bf), w_mlp_down.astype(bf)
    for layer in range(DEPTH):
        j = layer // 2
        gmix = g_mix[layer][None]
        if layer % 2 == 0:
            p = _attn_params(w_in_attn[j], g_cq[j], w_uq[j], g_ckv[j], w_ukv[j], g_q_mla[j],
                             g_k_mla[j], g_q_fox[j], g_k_fox[j], b_forget[j])
            stream = (x, meta) if layer == 0 else (hn,)
            q, k, v, gate_end = _attn_in(stream, lp, gmix, p, rope_tab, tri, selq, selk, tm)
            y = _flash(_pruning_tables(gate_end, g_q_fox[j], g_k_fox[j]), q, k, v)
            wo = wo_attn
        else:
            cw = jnp.zeros((8, d), jnp.float32).at[0:3].set(conv_w[j])
            y = _conv_in(hn, w_conv, j, cw, tm)
            wo = wo_conv
        gmlp = g_mlp[layer][None]
        gnext = g_mix[min(layer + 1, DEPTH - 1)][None]
        if layer == 0:
            h, hn = _mix_out_mlp_first(x, meta, y, wo, j, gmlp, gnext, w_up, w_down, layer, tm)
        elif layer < DEPTH - 1:
            h, hn = _mix_out_mlp(h.reshape(b * lp, d), y.reshape(b * lp, d), wo, j, gmlp, gnext,
                                 w_up, w_down, layer, tm)
            h, hn = h.reshape(b, lp, d), hn.reshape(b, lp, d)
        else:
            return _mix_out_mlp_last(h, y, wo, j, gmlp, w_up, w_down, layer, tm, seq)
```

```python
import functools

import numpy as np
import jax
import jax.numpy as jnp
from jax import lax
from jax.experimental import pallas as pl
from jax.experimental.pallas import tpu as pltpu

D_MODEL = 1024
DEPTH = 4
N_META = 16
BLOCK = 128
PAD = 2 * BLOCK - N_META
REAL_START = PAD + N_META
REAL_PARTS = 3
MLA_HEADS = 8
MLA_NOPE = 64
MLA_ROPE = 32
MLA_QK = MLA_NOPE + MLA_ROPE
MLA_V = 64
Q_LORA = 384
KV_LORA = 256
ROPE_BASE = 10000.0
FOX_HEADS = 8
FOX_DIM = 64
D_FF = 4 * D_MODEL
EPS = 1e-6
NEG = -1e30

LANE = 128
HEADS = MLA_HEADS + FOX_HEADS
HALF_ROPE = MLA_ROPE // 2
FEATURE_BASE = (0, FOX_DIM)
EXTRA_BASE = (FOX_DIM, 0)
N_SPLIT = 3
ONE_LANE = N_SPLIT * FOX_HEADS
PADROW_LANE = ONE_LANE + 1
FLAG_FOX_OFF = 2 * N_SPLIT
FLAG_MLA = MLA_QK
PAD_KEY = NEG
LOG2E = 1.4426950408889634
MISC_GATE = 0
MISC_ROPE = MLA_NOPE

OFF_CQ = 0
OFF_MISC = OFF_CQ + Q_LORA
OFF_CKV = OFF_MISC + LANE
OFF_FQ = OFF_CKV + KV_LORA
OFF_FK = OFF_FQ + FOX_HEADS * FOX_DIM
OFF_FV = OFF_FK + FOX_HEADS * FOX_DIM
W_CAT = OFF_FV + FOX_HEADS * FOX_DIM

(V_GQ_MLA, V_GQ_MLA_SW, V_GK_MLA, V_GK_MLA_SW, V_ADD_Q_MLA, V_B_FORGET) = range(6)
V_GQ_FOX, V_GK_FOX, V_ONES_V = 6, 8, 10
VEC_ROWS = 16
PAIR = 2 * LANE

FF_CHUNK = 1024
UNDERFLOW_LOG2 = 150.0
FLASH_TQ = 768
FLASH_TK = 256
BLOCKS_PER_BODY = 4
VMEM_LIMIT = 56 * 1024 * 1024


def _const_spec(shape):
    nd = len(shape)
    return pl.BlockSpec(shape, lambda *_: (0,) * nd, pipeline_mode=pl.Buffered(1))


def _layer_spec(shape, layer):
    nd = len(shape)
    return pl.BlockSpec((1,) + shape, lambda *_: (layer,) + (0,) * nd,
                        pipeline_mode=pl.Buffered(1))


def _input_specs(tm):
    part = tm // REAL_PARTS
    assert REAL_START == part

    def part_spec(k):
        return pl.BlockSpec((1, part, D_MODEL),
                            lambda bi, i: (bi, jnp.maximum(REAL_PARTS * i + k - 1, 0), 0))

    return [part_spec(k) for k in range(REAL_PARTS)] + [_const_spec((N_META, D_MODEL))]


def _input_tile(i, x_parts, meta_ref):
    lead = jnp.concatenate([jnp.zeros((PAD, D_MODEL), jnp.float32), meta_ref[...]], axis=0)
    first = jnp.where(i == 0, lead, x_parts[0][0])
    return jnp.concatenate([first] + [r[0] for r in x_parts[1:]], axis=0)


def _rms(x, g, n):
    ms = jnp.sum(x * x, axis=-1, keepdims=True) * (1.0 / n)
    return x * lax.rsqrt(ms + EPS) * g


def _split3(x, extra=0.0):
    hi = x.astype(jnp.bfloat16).astype(jnp.float32)
    r1 = x - hi
    mid = r1.astype(jnp.bfloat16).astype(jnp.float32)
    lo = r1 - mid
    packed = hi + pltpu.roll(mid, FOX_HEADS, 1) + pltpu.roll(lo, 2 * FOX_HEADS, 1)
    return (packed + extra).astype(jnp.bfloat16)


def _dot(a, b):
    return jnp.dot(a, b, preferred_element_type=jnp.float32)


def _attn_in_kernel(*refs, tm, from_x):
    n_stream = REAL_PARTS + 1 if from_x else 1
    stream = refs[:n_stream]
    (gmix_ref, wcat_ref, gcq_ref, wuq_ref, gckv_ref, wkn_ref, wv_ref, vec_ref, rope_ref, tri_ref,
     selq_ref, selk_ref, q_ref, k_ref, v_ref, gate_end_ref, carry_ref) = refs[n_stream:]
    i = pl.program_id(1)

    @pl.when(i == 0)
    def _():
        carry_ref[...] = jnp.zeros_like(carry_ref)

    if from_x:
        x = _input_tile(i, stream[:-1], stream[-1])
        hn = _rms(x, gmix_ref[...], D_MODEL).astype(jnp.bfloat16)
    else:
        hn = stream[0][0]

    def seg(lo, width):
        return _dot(hn, wcat_ref[:, lo:lo + width])

    def vec(r):
        return vec_ref[r:r + 1, :]

    cos_t = rope_ref[:, 0:LANE]
    sin_sw = rope_ref[:, LANE:2 * LANE]
    gc_q, gs_q = vec(V_GQ_MLA) * cos_t, vec(V_GQ_MLA_SW) * sin_sw
    gc_k, gs_k = vec(V_GK_MLA) * cos_t, vec(V_GK_MLA_SW) * sin_sw
    add_q_mla = vec(V_ADD_Q_MLA)

    lane = lax.broadcasted_iota(jnp.int32, (tm, LANE), 1)
    row = lax.broadcasted_iota(jnp.int32, (tm, LANE), 0)
    valid = (i * tm + row) >= PAD
    pad_key = jnp.where(valid, 0.0, PAD_KEY)
    add_k_mla = jnp.where(lane == FLAG_MLA, pad_key, 0.0)
    halves = (lane < FOX_DIM, lane >= FOX_DIM)
    sel_extra = jnp.where(lane == ONE_LANE, 1.0,
                          jnp.where((lane == PADROW_LANE) & ~valid, 1.0, 0.0))

    cq_misc = seg(OFF_CQ, Q_LORA + LANE)
    misc = cq_misc[:, Q_LORA:]
    kpe = jnp.where((lane >= MISC_ROPE) & (lane < MISC_ROPE + MLA_ROPE), misc, 0.0)
    k_rot = jnp.where(lane < MISC_ROPE + HALF_ROPE, pltpu.roll(kpe, LANE - HALF_ROPE, 1),
                      pltpu.roll(kpe, HALF_ROPE, 1)) * gs_k
    xl = misc + vec(V_B_FORGET)
    logf = jnp.minimum(xl, 0.0) - jnp.log1p(jnp.exp(-jnp.abs(xl)))
    logf = jnp.where(valid & (lane >= MISC_GATE) & (lane < MISC_GATE + FOX_HEADS), logf, 0.0)
    cs = _dot(tri_ref[...], _split3(logf))
    cs = (cs + pltpu.roll(cs, LANE - FOX_HEADS, 1)) + pltpu.roll(cs, LANE - 2 * FOX_HEADS, 1)
    cum = jnp.where(lane < FOX_HEADS, cs, 0.0) + carry_ref[0:1, :]
    carry_ref[0:1, :] = cum[tm - 1:tm, :]
    gate_end_ref[0, 0] = jnp.zeros((8, LANE), jnp.float32)
    for c in range(tm // FLASH_TK):
        gate_end_ref[0, 0, c:c + 1, :] = cum[(c + 1) * FLASH_TK - 1:(c + 1) * FLASH_TK, :]
    cum3 = _split3(cum * LOG2E, sel_extra)
    gate_q = _dot(cum3, selq_ref[...])
    gate_k = _dot(cum3, selk_ref[...])

    def inv_norm(sq, n):
        return lax.rsqrt(jnp.sum(sq, axis=-1, keepdims=True) + n * EPS)

    def fox_group(g):
        xq4 = seg(OFF_FQ + g * PAIR, PAIR)
        xk4 = seg(OFF_FK + g * PAIR, PAIR)
        xv4 = seg(OFF_FV + g * PAIR, PAIR)
        for e in range(4):
            hd, par = 4 * g + e, e % 2
            sl = slice((e // 2) * LANE, (e // 2 + 1) * LANE)
            gl = slice(hd * LANE, (hd + 1) * LANE)
            xq, xk = xq4[:, sl], xk4[:, sl]
            rq = inv_norm(jnp.where(halves[par], xq * xq, 0.0), FOX_DIM)
            rk = inv_norm(jnp.where(halves[par], xk * xk, 0.0), FOX_DIM)
            q_ref[0, MLA_HEADS + hd] = (xq * vec(V_GQ_FOX + par) * rq + gate_q[:, gl]
                                        ).astype(jnp.bfloat16)
            k_ref[0, MLA_HEADS + hd] = (xk * vec(V_GK_FOX + par) * rk + gate_k[:, gl]
                                        ).astype(jnp.bfloat16)
            v_ref[0, MLA_HEADS + hd] = jnp.where(halves[par], xv4[:, sl], 1.0
                                                 ).astype(jnp.bfloat16)

    cqn = _rms(cq_misc[:, :Q_LORA], gcq_ref[...], Q_LORA).astype(jnp.bfloat16)
    ckvn = _rms(seg(OFF_CKV, KV_LORA), gckv_ref[...], KV_LORA).astype(jnp.bfloat16)
    def mla_pair(g):
        cols = slice(g * PAIR, (g + 1) * PAIR)
        cols_sw = slice(MLA_HEADS * LANE + g * PAIR, MLA_HEADS * LANE + (g + 1) * PAIR)
        xq2 = _dot(cqn, wuq_ref[:, cols])
        xq2_sw = _dot(cqn, wuq_ref[:, cols_sw])
        xk2 = _dot(ckvn, wkn_ref[:, cols])
        xv2 = _dot(ckvn, wv_ref[:, cols])
        for e in range(2):
            hd, sl = 2 * g + e, slice(e * LANE, (e + 1) * LANE)
            xq = xq2[:, sl]
            q_ref[0, hd] = ((xq * gc_q + xq2_sw[:, sl] * gs_q) * inv_norm(xq * xq, MLA_QK)
                            + add_q_mla).astype(jnp.bfloat16)
            xk = xk2[:, sl] + kpe
            k_ref[0, hd] = ((xk * gc_k + k_rot) * inv_norm(xk * xk, MLA_QK) + add_k_mla
                            ).astype(jnp.bfloat16)
            v_ref[0, hd] = (xv2[:, sl] + vec(V_ONES_V + e)).astype(jnp.bfloat16)

    fox_group(0)
    mla_pair(0)
    mla_pair(1)
    fox_group(1)
    mla_pair(2)
    mla_pair(3)


def _attn_in(stream, lp, gmix, p, rope_tab, tri, selq, selk, tm):
    from_x = len(stream) == 2
    b, d = stream[0].shape[0], D_MODEL
    nt = lp // tm
    kern = functools.partial(_attn_in_kernel, tm=tm, from_x=from_x)
    stream_specs = (_input_specs(tm) if from_x
                    else [pl.BlockSpec((1, tm, d), lambda bi, i: (bi, i, 0))])
    stream_args = [stream[0]] * REAL_PARTS + [stream[1]] if from_x else [stream[0]]
    qk_shape = jax.ShapeDtypeStruct((b, HEADS, lp, LANE), jnp.bfloat16)
    qk_spec = pl.BlockSpec((1, HEADS, tm, LANE), lambda bi, i: (bi, 0, i, 0))
    return pl.pallas_call(
        kern,
        grid=(b, nt),
        in_specs=stream_specs + [
            _const_spec((1, d)),
            _const_spec((d, W_CAT)),
            _const_spec((1, Q_LORA)),
            _const_spec((Q_LORA, 2 * MLA_HEADS * LANE)),
            _const_spec((1, KV_LORA)),
            _const_spec((KV_LORA, MLA_HEADS * LANE)),
            _const_spec((KV_LORA, MLA_HEADS * LANE)),
            _const_spec((VEC_ROWS, LANE)),
            pl.BlockSpec((tm, 2 * LANE), lambda bi, i: (i, 0)),
            _const_spec((tm, tm)),
            _const_spec((LANE, FOX_HEADS * LANE)),
            _const_spec((LANE, FOX_HEADS * LANE)),
        ],
        out_specs=[qk_spec, qk_spec, qk_spec,
                   pl.BlockSpec((1, 1, 8, LANE), lambda bi, i: (bi, i, 0, 0))],
        out_shape=[qk_shape, qk_shape, qk_shape,
                   jax.ShapeDtypeStruct((b, nt, 8, LANE), jnp.float32)],
        scratch_shapes=[pltpu.VMEM((8, LANE), jnp.float32)],
        compiler_params=pltpu.CompilerParams(
            dimension_semantics=("arbitrary", "arbitrary"), vmem_limit_bytes=VMEM_LIMIT),
        name="attn_in",
    )(*stream_args, gmix, p["wcat"], p["gcq"], p["wuq"], p["gckv"], p["wkn"], p["wv"], p["vec"],
      rope_tab, tri, selq, selk)


def _flash_kernel(gate_end_ref, slack_ref, q_ref, k_ref, v_ref, o_ref, m_ref, acc_ref, al_ref,
                  p_ref, *, tq, tk, nq):
    chunks = tq // tk
    assert chunks >= 2
    bi, hp = pl.program_id(0), pl.program_id(1)

    def query_block(qi, carry):
        qbase = qi * tq

        def softmax(j, u, base, diagonal, first=False):
            r0 = u * tk if diagonal else 0
            rows = slice(r0, tq)
            start = pl.multiple_of(base + u * tk, tk)
            q_rows = pl.ds(pl.multiple_of(qbase + r0, tk), tq - r0)
            s = lax.dot_general(q_ref[0, j, q_rows, :], k_ref[0, j, pl.ds(start, tk), :],
                                (((1,), (1,)), ((), ())), preferred_element_type=jnp.float32)
            if diagonal:
                row = lax.broadcasted_iota(jnp.int32, (tq - r0, tk), 0)
                col = lax.broadcasted_iota(jnp.int32, (tq - r0, tk), 1)
                s = jnp.where(col <= row, s, NEG)
            if first:
                m_next = jnp.broadcast_to(jnp.max(s, axis=1, keepdims=True), (tq - r0, LANE))
            else:
                m_prev = m_ref[j, rows, :]
                m_next = jnp.maximum(m_prev, jnp.max(s, axis=1, keepdims=True))
                al_ref[j, u, rows, :] = jnp.exp2(m_prev - m_next)
            p = jnp.exp2(s - jnp.concatenate([m_next] * (tk // LANE), axis=1))
            p_ref[j, u, rows, :] = p.astype(jnp.bfloat16)
            m_ref[j, rows, :] = m_next

        def pv(j, u, base, diagonal, first=False):
            r0 = u * tk if diagonal else 0
            rows = slice(r0, tq)
            start = pl.multiple_of(base + u * tk, tk)
            new = _dot(p_ref[j, u, rows, :], v_ref[0, j, pl.ds(start, tk), :])
            if first:
                acc_ref[j, rows, :] = new
            else:
                acc_ref[j, rows, :] = acc_ref[j, rows, :] * al_ref[j, u, rows, :] + new

        def diagonal_block():
            for u in range(chunks):
                softmax(0, u, qbase, True, first=(u == 0))
                if u > 0:
                    pv(1, u - 1, qbase, True, first=(u == 1))
                softmax(1, u, qbase, True, first=(u == 0))
                pv(0, u, qbase, True, first=(u == 0))
            r0 = (chunks - 1) * tk
            al_ref[1, chunks - 1, 0:r0, :] = jnp.ones((r0, LANE), jnp.float32)
            p_ref[1, chunks - 1, 0:r0, :] = jnp.zeros((r0, tk), jnp.bfloat16)

        def block(kb, diagonal, start=0):
            base = kb * tq
            for u in range(start, chunks):
                softmax(0, u, base, diagonal)
                if u == start:
                    pv(1, chunks - 1, jnp.where(kb == first_kb, qbase, (kb - 1) * tq), False)
                else:
                    pv(1, u - 1, base, diagonal)
                softmax(1, u, base, diagonal)
                pv(0, u, base, diagonal)

        def chunks_needed(j):
            hd = 2 * hp + j
            gate_q = gate_end_ref[bi, hd, jnp.maximum(chunks * qi - 1, 0)]
            count = jnp.int32(0)
            for c in range(chunks * (nq - 1)):
                keep = (c < chunks * qi) & (gate_q - gate_end_ref[bi, hd, c] >= slack_ref[hd])
                count = count + keep.astype(jnp.int32)
            return count

        diagonal_block()
        n_chunks = jnp.maximum(chunks_needed(0), chunks_needed(1))
        partial = n_chunks % chunks
        n_full = n_chunks // chunks
        first = qi - n_full
        first_kb = first - (partial != 0).astype(jnp.int32)
        for start in range(1, chunks):

            @pl.when(partial == chunks - start)
            def _():
                block(first - 1, False, start)

        rem = n_full & (BLOCKS_PER_BODY - 1)

        @pl.when((rem & 1) == 1)
        def _():
            block(first, False)

        @pl.when((rem & 2) == 2)
        def _():
            kb = first + (rem & 1)
            block(kb, False)
            block(kb + 1, False)

        def body(group, c):
            kb = first + rem + BLOCKS_PER_BODY * group
            for d in range(BLOCKS_PER_BODY):
                block(kb + d, False)
            return c

        lax.fori_loop(0, n_full // BLOCKS_PER_BODY, body, 0)
        pv(1, chunks - 1, jnp.where(n_chunks == 0, qbase, (qi - 1) * tq), False)

        o0 = acc_ref[0]
        o1 = acc_ref[1]
        o0 = o0 * pl.reciprocal(pltpu.roll(o0, FOX_DIM, 1), approx=True)
        o1 = o1 * pl.reciprocal(pltpu.roll(o1, FOX_DIM, 1), approx=True)
        lane = lax.broadcasted_iota(jnp.int32, (tq, LANE), 1)
        o = jnp.where(lane < MLA_V, o0, o1)
        o_ref[0, pl.ds(pl.multiple_of(qbase, tq), tq), :] = o.astype(jnp.bfloat16)
        return carry

    lax.fori_loop(0, nq, query_block, 0)


def _flash(gate_end, slack, q, k, v):
    b, _, lp, _ = q.shape
    tq, tk = FLASH_TQ, FLASH_TK
    kern = functools.partial(_flash_kernel, tq=tq, tk=tk, nq=lp // tq)
    qkv_spec = pl.BlockSpec((1, 2, lp, LANE), lambda bi, hp: (bi, hp, 0, 0))
    return pl.pallas_call(
        kern,
        grid=(b, HEADS // 2),
        in_specs=[pl.BlockSpec(memory_space=pltpu.SMEM), pl.BlockSpec(memory_space=pltpu.SMEM),
                  qkv_spec, qkv_spec, qkv_spec],
        out_specs=pl.BlockSpec((1, lp, LANE), lambda bi, hp: (bi, 0, hp)),
        out_shape=jax.ShapeDtypeStruct((b, lp, HEADS * MLA_V), jnp.bfloat16),
        scratch_shapes=[pltpu.VMEM((2, tq, LANE), jnp.float32)] * 2
        + [pltpu.VMEM((2, tq // tk, tq, LANE), jnp.float32),
           pltpu.VMEM((2, tq // tk, tq, tk), jnp.bfloat16)],
        compiler_params=pltpu.CompilerParams(
            dimension_semantics=("arbitrary", "arbitrary"), vmem_limit_bytes=VMEM_LIMIT),
        name="flash",
    )(gate_end, slack, q, k, v)


def _conv_in_kernel(hn_ref, win_ref, cw_ref, y_ref, gs_ref, *, tm):
    i = pl.program_id(1)

    @pl.when(i == 0)
    def _():
        gs_ref[0:8, :] = jnp.zeros((8, D_MODEL), jnp.float32)

    hn = hn_ref[0]
    gate_c = _dot(hn, win_ref[0, :, D_MODEL:2 * D_MODEL])
    u = _dot(hn, win_ref[0, :, 2 * D_MODEL:3 * D_MODEL])
    row = lax.broadcasted_iota(jnp.int32, (tm, D_MODEL), 0)
    g = jnp.where((i * tm + row) >= PAD, gate_c * u, 0.0)
    gs_ref[8:tm + 8, :] = g
    y = (cw_ref[0:1, :] * gs_ref[6:tm + 6, :] + cw_ref[1:2, :] * gs_ref[7:tm + 7, :]
         + cw_ref[2:3, :] * g)
    gs_ref[0:8, :] = gs_ref[tm:tm + 8, :]
    gate_b = _dot(hn, win_ref[0, :, 0:D_MODEL])
    y_ref[0] = (gate_b * y).astype(jnp.bfloat16)


def _conv_in(hn, win, layer, cw, tm):
    b, lp, d = hn.shape
    kern = functools.partial(_conv_in_kernel, tm=tm)
    return pl.pallas_call(
        kern,
        grid=(b, lp // tm),
        in_specs=[
            pl.BlockSpec((1, tm, d), lambda bi, i: (bi, i, 0)),
            _layer_spec((d, 3 * d), layer),
            _const_spec((8, d)),
        ],
        out_specs=pl.BlockSpec((1, tm, d), lambda bi, i: (bi, i, 0)),
        out_shape=jax.ShapeDtypeStruct((b, lp, d), jnp.bfloat16),
        scratch_shapes=[pltpu.VMEM((tm + 8, d), jnp.float32)],
        compiler_params=pltpu.CompilerParams(
            dimension_semantics=("arbitrary", "arbitrary"), vmem_limit_bytes=VMEM_LIMIT),
        name="conv_in",
    )(hn, win, cw)


def _mlp_tile(h, y, wo_ref, gmlp_ref, wup_ref, wdn_ref):
    h1 = h + _dot(y, wo_ref[0])
    n = _rms(h1, gmlp_ref[...], D_MODEL).astype(jnp.bfloat16)
    acc = h1
    for c in range(D_FF // FF_CHUNK):
        sl = slice(c * FF_CHUNK, (c + 1) * FF_CHUNK)
        a = jnp.maximum(_dot(n, wup_ref[0, :, sl]), 0.0)
        acc = acc + _dot((a * a).astype(jnp.bfloat16), wdn_ref[0, sl, :])
    return acc


def _mix_out_mlp_kernel(h_ref, y_ref, wo_ref, gmlp_ref, gnext_ref, wup_ref, wdn_ref,
                        out_ref, hn_ref):
    out = _mlp_tile(h_ref[...], y_ref[...], wo_ref, gmlp_ref, wup_ref, wdn_ref)
    out_ref[...] = out
    hn_ref[...] = _rms(out, gnext_ref[...], D_MODEL).astype(jnp.bfloat16)


def _mix_out_mlp(h, y, wo, wo_layer, gmlp, gnext, wup, wdn, layer, tm):
    r, d = h.shape
    return pl.pallas_call(
        _mix_out_mlp_kernel,
        grid=(r // tm,),
        in_specs=[
            pl.BlockSpec((tm, d), lambda i: (i, 0)),
            pl.BlockSpec((tm, d), lambda i: (i, 0)),
            _layer_spec((d, d), wo_layer),
            _const_spec((1, d)),
            _const_spec((1, d)),
            _layer_spec((d, D_FF), layer),
            _layer_spec((D_FF, d), layer),
        ],
        out_specs=[pl.BlockSpec((tm, d), lambda i: (i, 0))] * 2,
        out_shape=[jax.ShapeDtypeStruct((r, d), jnp.float32),
                   jax.ShapeDtypeStruct((r, d), jnp.bfloat16)],
        compiler_params=pltpu.CompilerParams(
            dimension_semantics=("arbitrary",), vmem_limit_bytes=VMEM_LIMIT),
        name="mix_out_mlp",
    )(h, y, wo, gmlp, gnext, wup, wdn)


def _mix_out_mlp_first_kernel(*refs):
    x_parts, meta_ref = refs[0:REAL_PARTS], refs[REAL_PARTS]
    y_ref, wo_ref, gmlp_ref, gnext_ref, wup_ref, wdn_ref, out_ref, hn_ref = refs[REAL_PARTS + 1:]
    h = _input_tile(pl.program_id(1), x_parts, meta_ref)
    out = _mlp_tile(h, y_ref[0], wo_ref, gmlp_ref, wup_ref, wdn_ref)
    out_ref[0] = out
    hn_ref[0] = _rms(out, gnext_ref[...], D_MODEL).astype(jnp.bfloat16)


def _mix_out_mlp_first(x, meta, y, wo, wo_layer, gmlp, gnext, wup, wdn, layer, tm):
    b, lp, d = y.shape
    tile = pl.BlockSpec((1, tm, d), lambda bi, i: (bi, i, 0))
    return pl.pallas_call(
        _mix_out_mlp_first_kernel,
        grid=(b, lp // tm),
        in_specs=_input_specs(tm) + [
            tile,
            _layer_spec((d, d), wo_layer),
            _const_spec((1, d)),
            _const_spec((1, d)),
            _layer_spec((d, D_FF), layer),
            _layer_spec((D_FF, d), layer),
        ],
        out_specs=[tile, tile],
        out_shape=[jax.ShapeDtypeStruct((b, lp, d), jnp.float32),
                   jax.ShapeDtypeStruct((b, lp, d), jnp.bfloat16)],
        compiler_params=pltpu.CompilerParams(
            dimension_semantics=("arbitrary", "arbitrary"), vmem_limit_bytes=VMEM_LIMIT),
        name="mix_out_mlp_first",
    )(*([x] * REAL_PARTS), meta, y, wo, gmlp, gnext, wup, wdn)


def _mix_out_mlp_last_kernel(*refs):
    h_parts, y_parts = refs[0:REAL_PARTS], refs[REAL_PARTS:2 * REAL_PARTS]
    wo_ref, gmlp_ref, wup_ref, wdn_ref, out_ref = refs[2 * REAL_PARTS:]
    h = jnp.concatenate([r[0] for r in h_parts], axis=0)
    y = jnp.concatenate([r[0] for r in y_parts], axis=0)
    out_ref[0] = _mlp_tile(h, y, wo_ref, gmlp_ref, wup_ref, wdn_ref)


def _mix_out_mlp_last(h, y, wo, wo_layer, gmlp, wup, wdn, layer, tm, seq):
    b, lp, d = h.shape
    part = tm // REAL_PARTS
    last_part = lp // part - 1

    def part_spec(k):
        return pl.BlockSpec(
            (1, part, d),
            lambda bi, i: (bi, jnp.minimum(REAL_START // part + REAL_PARTS * i + k, last_part), 0))

    parts = [part_spec(k) for k in range(REAL_PARTS)]
    return pl.pallas_call(
        _mix_out_mlp_last_kernel,
        grid=(b, pl.cdiv(seq, tm)),
        in_specs=parts + parts + [
            _layer_spec((d, d), wo_layer),
            _const_spec((1, d)),
            _layer_spec((d, D_FF), layer),
            _layer_spec((D_FF, d), layer),
        ],
        out_specs=pl.BlockSpec((1, tm, d), lambda bi, i: (bi, i, 0)),
        out_shape=jax.ShapeDtypeStruct((b, seq, d), jnp.float32),
        compiler_params=pltpu.CompilerParams(
            dimension_semantics=("arbitrary", "arbitrary"), vmem_limit_bytes=VMEM_LIMIT),
        name="mix_out_mlp_last",
    )(*([h] * REAL_PARTS + [y] * REAL_PARTS), wo, gmlp, wup, wdn)


def _pad_heads(w, heads, dim):
    k = w.shape[0]
    w = w.reshape(k, heads, dim)
    w = jnp.pad(w, ((0, 0), (0, 0), (0, LANE - dim)))
    return w.reshape(k, heads * LANE)


def _lane_vec(v, offset=0):
    return jnp.zeros((LANE,), jnp.float32).at[offset:offset + v.shape[0]].set(v)


def _attn_params(w_in, g_cq, w_uq, g_ckv, w_ukv, g_q_mla, g_k_mla, g_q_fox, g_k_fox, b_forget):
    bf = jnp.bfloat16
    o1 = Q_LORA
    o2 = o1 + KV_LORA
    o3 = o2 + MLA_ROPE
    o4 = o3 + FOX_HEADS * FOX_DIM
    o5 = o4 + FOX_HEADS * FOX_DIM
    o6 = o5 + FOX_HEADS * FOX_DIM
    misc = jnp.zeros((D_MODEL, LANE), jnp.float32)
    misc = misc.at[:, MISC_GATE:MISC_GATE + FOX_HEADS].set(w_in[:, o6:])
    misc = misc.at[:, MISC_ROPE:MISC_ROPE + MLA_ROPE].set(w_in[:, o2:o3])
    wcat = jnp.concatenate([w_in[:, :o1], misc, w_in[:, o1:o2], w_in[:, o3:o6]], axis=1).astype(bf)
    kv = w_ukv.reshape(KV_LORA, MLA_HEADS, MLA_NOPE + MLA_V)
    wkn = _pad_heads(kv[:, :, :MLA_NOPE].reshape(KV_LORA, -1), MLA_HEADS, MLA_NOPE).astype(bf)
    wv = jnp.pad(kv[:, :, MLA_NOPE:].reshape(KV_LORA, MLA_HEADS // 2, 2, MLA_V),
                 ((0, 0), (0, 0), (0, 0), (0, LANE - MLA_V)))
    wv = jnp.concatenate([wv[:, :, 0], jnp.roll(wv[:, :, 1], MLA_V, axis=-1)], axis=-1)
    wv = wv.reshape(KV_LORA, MLA_HEADS * LANE).astype(bf)
    lo, mid, hi = MLA_NOPE, MLA_NOPE + HALF_ROPE, MLA_NOPE + MLA_ROPE
    uq = w_uq.reshape(Q_LORA, MLA_HEADS, MLA_QK)
    uq_sw = jnp.zeros((Q_LORA, MLA_HEADS, LANE), jnp.float32)
    uq_sw = uq_sw.at[:, :, lo:mid].set(uq[:, :, mid:hi]).at[:, :, mid:hi].set(uq[:, :, lo:mid])
    wuq = jnp.concatenate([_pad_heads(w_uq, MLA_HEADS, MLA_QK),
                           uq_sw.reshape(Q_LORA, MLA_HEADS * LANE)], axis=1).astype(bf)

    def swapped(g):
        return jnp.zeros((LANE,), jnp.float32).at[lo:mid].set(g[mid:hi]).at[mid:hi].set(g[lo:mid])

    zero = jnp.zeros((LANE,), jnp.float32)
    rows = [zero] * VEC_ROWS
    rows[V_GQ_MLA] = _lane_vec(g_q_mla) * LOG2E
    rows[V_GQ_MLA_SW] = swapped(g_q_mla) * LOG2E
    rows[V_GK_MLA] = _lane_vec(g_k_mla) * MLA_QK ** 0.5
    rows[V_GK_MLA_SW] = swapped(g_k_mla) * MLA_QK ** 0.5
    for par in range(2):
        feat, extra = FEATURE_BASE[par], EXTRA_BASE[par]
        rows[V_GQ_FOX + par] = _lane_vec(g_q_fox, feat) * LOG2E
        rows[V_GK_FOX + par] = _lane_vec(g_k_fox, feat) * FOX_DIM ** 0.5
        rows[V_ONES_V + par] = zero.at[extra:extra + FOX_DIM].set(1.0)
    rows[V_B_FORGET] = _lane_vec(b_forget, MISC_GATE)
    rows[V_ADD_Q_MLA] = zero.at[FLAG_MLA].set(1.0)
    vec = jnp.stack(rows)
    return dict(wcat=wcat, gcq=g_cq[None], wuq=wuq, gckv=g_ckv[None], wkn=wkn, wv=wv, vec=vec)


def _gate_selectors():
    selq = np.zeros((LANE, FOX_HEADS * LANE), np.float32)
    selk = np.zeros((LANE, FOX_HEADS * LANE), np.float32)
    for hd in range(FOX_HEADS):
        extra = hd * LANE + EXTRA_BASE[hd % 2]
        for part in range(N_SPLIT):
            selq[part * FOX_HEADS + hd, extra + part] = 1.0
            selk[part * FOX_HEADS + hd, extra + N_SPLIT + part] = -1.0
            selq[ONE_LANE, extra + N_SPLIT + part] = 1.0
            selk[ONE_LANE, extra + part] = 1.0
        selq[ONE_LANE, extra + FLAG_FOX_OFF] = 1.0
        selk[PADROW_LANE, extra + FLAG_FOX_OFF] = PAD_KEY
    return jnp.asarray(selq, jnp.bfloat16), jnp.asarray(selk, jnp.bfloat16)


def _rope_table(lp):
    lane = jnp.arange(LANE, dtype=jnp.int32)
    rotary = (lane >= MLA_NOPE) & (lane < MLA_NOPE + MLA_ROPE)
    first_half = rotary & (lane < MLA_NOPE + HALF_ROPE)
    pair = ((lane - MLA_NOPE) % HALF_ROPE).astype(jnp.float32)
    inv_freq = ROPE_BASE ** (-(2.0 * pair) / MLA_ROPE)
    pos = (jnp.arange(lp, dtype=jnp.int32) - PAD).astype(jnp.float32)
    ang = pos[:, None] * inv_freq[None, :]
    cos_t = jnp.where(lane < MLA_NOPE, 1.0, jnp.where(rotary, jnp.cos(ang), 0.0))
    sin_sw = jnp.where(rotary, jnp.where(first_half, -jnp.sin(ang), jnp.sin(ang)), 0.0)
    return jnp.concatenate([cos_t, sin_sw], axis=1)


def _pruning_tables(gate_end, g_q, g_k):
    b, nt = gate_end.shape[:2]
    per_tile = FLASH_TQ // FLASH_TK
    fox = gate_end[:, :, 0:per_tile, MISC_GATE:MISC_GATE + FOX_HEADS] * LOG2E
    fox = jnp.transpose(fox.reshape(b, nt * per_tile, FOX_HEADS), (0, 2, 1))
    table = jnp.concatenate([jnp.zeros((b, MLA_HEADS, nt * per_tile), jnp.float32), fox], axis=1)
    bound = 1.02 * FOX_DIM * (FOX_DIM ** -0.5 * LOG2E) * jnp.max(jnp.abs(g_q)) * jnp.max(jnp.abs(g_k))
    slack_fox = -(2.0 * bound + UNDERFLOW_LOG2 + 4.0)
    slack = jnp.concatenate([jnp.full((MLA_HEADS,), NEG, jnp.float32),
                             jnp.full((FOX_HEADS,), slack_fox, jnp.float32)])
    return table, slack


def _token_tile(lp):
    if lp % FLASH_TQ:
        raise ValueError(f"padded length {lp} is not a multiple of {FLASH_TQ}")
    return FLASH_TQ


def kernel(x, meta_tokens, g_mix, g_mlp, w_in_attn, g_cq, w_uq, g_ckv, w_ukv, g_q_mla, g_k_mla,
           g_q_fox, g_k_fox, b_forget, w_out_attn, w_in_conv, conv_w, w_out_conv, w_mlp_up,
           w_mlp_down):
    b, seq, d = x.shape
    assert d == D_MODEL and (PAD + N_META + seq) % BLOCK == 0
    lp = PAD + N_META + seq
    tm = _token_tile(lp)
    bf = jnp.bfloat16

    meta = meta_tokens.astype(x.dtype)
    h = hn = None

    rope_tab = _rope_table(lp)
    tri = (jnp.arange(tm)[:, None] >= jnp.arange(tm)[None, :]).astype(bf)
    selq, selk = _gate_selectors()

    wo_attn, wo_conv, w_conv = w_out_attn.astype(bf), w_out_conv.astype(bf), w_in_conv.astype(bf)
    w_up, w_down = w_mlp_up.astype(bf), w_mlp_down.astype(bf)
    for layer in range(DEPTH):
        j = layer // 2
        gmix = g_mix[layer][None]
        if layer % 2 == 0:
            p = _attn_params(w_in_attn[j], g_cq[j], w_uq[j], g_ckv[j], w_ukv[j], g_q_mla[j],
                             g_k_mla[j], g_q_fox[j], g_k_fox[j], b_forget[j])
            stream = (x, meta) if layer == 0 else (hn,)
            q, k, v, gate_end = _attn_in(stream, lp, gmix, p, rope_tab, tri, selq, selk, tm)
            y = _flash(*_pruning_tables(gate_end, g_q_fox[j], g_k_fox[j]), q, k, v)
            wo = wo_attn
        else:
            cw = jnp.zeros((8, d), jnp.float32).at[0:3].set(conv_w[j])
            y = _conv_in(hn, w_conv, j, cw, tm)
            wo = wo_conv
        gmlp = g_mlp[layer][None]
        gnext = g_mix[min(layer + 1, DEPTH - 1)][None]
        if layer == 0:
            h, hn = _mix_out_mlp_first(x, meta, y, wo, j, gmlp, gnext, w_up, w_down, layer, tm)
        elif layer < DEPTH - 1:
            h, hn = _mix_out_mlp(h.reshape(b * lp, d), y.reshape(b * lp, d), wo, j, gmlp, gnext,
                                 w_up, w_down, layer, tm)
            h, hn = h.reshape(b, lp, d), hn.reshape(b, lp, d)
        else:
            return _mix_out_mlp_last(h, y, wo, j, gmlp, w_up, w_down, layer, tm, seq)
```

```python
import functools

import numpy as np
import jax
import jax.numpy as jnp
from jax import lax
from jax.experimental import pallas as pl
from jax.experimental.pallas import tpu as pltpu

D_MODEL = 1024
DEPTH = 4
N_META = 16
BLOCK = 128
PAD = 2 * BLOCK - N_META
REAL_START = PAD + N_META
REAL_PARTS = 3
MLA_HEADS = 8
MLA_NOPE = 64
MLA_ROPE = 32
MLA_QK = MLA_NOPE + MLA_ROPE
MLA_V = 64
Q_LORA = 384
KV_LORA = 256
ROPE_BASE = 10000.0
FOX_HEADS = 8
FOX_DIM = 64
D_FF = 4 * D_MODEL
EPS = 1e-6
NEG = -1e30

LANE = 128
HEADS = MLA_HEADS + FOX_HEADS
HALF_ROPE = MLA_ROPE // 2
FEATURE_BASE = (0, FOX_DIM)
EXTRA_BASE = (FOX_DIM, 0)
N_SPLIT = 3
ONE_LANE = N_SPLIT * FOX_HEADS
PADROW_LANE = ONE_LANE + 1
FLAG_FOX_OFF = 2 * N_SPLIT
FLAG_MLA = MLA_QK
PAD_KEY = NEG
LOG2E = 1.4426950408889634
MISC_GATE = 0
MISC_ROPE = MLA_NOPE

OFF_CQ = 0
OFF_MISC = OFF_CQ + Q_LORA
OFF_CKV = OFF_MISC + LANE
OFF_FQ = OFF_CKV + KV_LORA
OFF_FK = OFF_FQ + FOX_HEADS * FOX_DIM
OFF_FV = OFF_FK + FOX_HEADS * FOX_DIM
W_CAT = OFF_FV + FOX_HEADS * FOX_DIM

(V_GQ_MLA, V_GQ_MLA_SW, V_GK_MLA, V_GK_MLA_SW, V_ADD_Q_MLA, V_B_FORGET) = range(6)
V_GQ_FOX, V_GK_FOX, V_ONES_V = 6, 8, 10
VEC_ROWS = 16
PAIR = 2 * LANE

FF_CHUNK = 1024
UNDERFLOW_LOG2 = 150.0
FLASH_TQ = 768
FLASH_TK = 256
BLOCKS_PER_BODY = 4
VMEM_LIMIT = 56 * 1024 * 1024


def _const_spec(shape):
    nd = len(shape)
    return pl.BlockSpec(shape, lambda *_: (0,) * nd, pipeline_mode=pl.Buffered(1))


def _layer_spec(shape, layer):
    nd = len(shape)
    return pl.BlockSpec((1,) + shape, lambda *_: (layer,) + (0,) * nd,
                        pipeline_mode=pl.Buffered(1))


def _input_specs(tm):
    part = tm // REAL_PARTS
    assert REAL_START == part

    def part_spec(k):
        return pl.BlockSpec((1, part, D_MODEL),
                            lambda bi, i: (bi, jnp.maximum(REAL_PARTS * i + k - 1, 0), 0))

    return [part_spec(k) for k in range(REAL_PARTS)] + [_const_spec((N_META, D_MODEL))]


def _input_tile(i, x_parts, meta_ref):
    lead = jnp.concatenate([jnp.zeros((PAD, D_MODEL), jnp.float32), meta_ref[...]], axis=0)
    first = jnp.where(i == 0, lead, x_parts[0][0])
    return jnp.concatenate([first] + [r[0] for r in x_parts[1:]], axis=0)


def _rms(x, g, n):
    ms = jnp.sum(x * x, axis=-1, keepdims=True) * (1.0 / n)
    return x * lax.rsqrt(ms + EPS) * g


def _split3(x, extra=0.0):
    hi = x.astype(jnp.bfloat16).astype(jnp.float32)
    r1 = x - hi
    mid = r1.astype(jnp.bfloat16).astype(jnp.float32)
    lo = r1 - mid
    packed = hi + pltpu.roll(mid, FOX_HEADS, 1) + pltpu.roll(lo, 2 * FOX_HEADS, 1)
    return (packed + extra).astype(jnp.bfloat16)


def _dot(a, b):
    return jnp.dot(a, b, preferred_element_type=jnp.float32)


def _attn_in_kernel(*refs, tm, from_x):
    n_stream = REAL_PARTS + 1 if from_x else 1
    stream = refs[:n_stream]
    (gmix_ref, wcat_ref, gcq_ref, wuq_ref, gckv_ref, wkn_ref, wv_ref, vec_ref, rope_ref, tri_ref,
     selq_ref, selk_ref, q_ref, k_ref, v_ref, gate_end_ref, carry_ref) = refs[n_stream:]
    i = pl.program_id(1)

    @pl.when(i == 0)
    def _():
        carry_ref[...] = jnp.zeros_like(carry_ref)

    if from_x:
        x = _input_tile(i, stream[:-1], stream[-1])
        hn = _rms(x, gmix_ref[...], D_MODEL).astype(jnp.bfloat16)
    else:
        hn = stream[0][0]

    def seg(lo, width):
        return _dot(hn, wcat_ref[:, lo:lo + width])

    def vec(r):
        return vec_ref[r:r + 1, :]

    cos_t = rope_ref[:, 0:LANE]
    sin_sw = rope_ref[:, LANE:2 * LANE]
    gc_q, gs_q = vec(V_GQ_MLA) * cos_t, vec(V_GQ_MLA_SW) * sin_sw
    gc_k, gs_k = vec(V_GK_MLA) * cos_t, vec(V_GK_MLA_SW) * sin_sw
    add_q_mla = vec(V_ADD_Q_MLA)

    lane = lax.broadcasted_iota(jnp.int32, (tm, LANE), 1)
    row = lax.broadcasted_iota(jnp.int32, (tm, LANE), 0)
    valid = (i * tm + row) >= PAD
    pad_key = jnp.where(valid, 0.0, PAD_KEY)
    add_k_mla = jnp.where(lane == FLAG_MLA, pad_key, 0.0)
    halves = (lane < FOX_DIM, lane >= FOX_DIM)
    sel_extra = jnp.where(lane == ONE_LANE, 1.0,
                          jnp.where((lane == PADROW_LANE) & ~valid, 1.0, 0.0))

    cq_misc = seg(OFF_CQ, Q_LORA + LANE)
    misc = cq_misc[:, Q_LORA:]
    kpe = jnp.where((lane >= MISC_ROPE) & (lane < MISC_ROPE + MLA_ROPE), misc, 0.0)
    k_rot = jnp.where(lane < MISC_ROPE + HALF_ROPE, pltpu.roll(kpe, LANE - HALF_ROPE, 1),
                      pltpu.roll(kpe, HALF_ROPE, 1)) * gs_k
    xl = misc + vec(V_B_FORGET)
    logf = jnp.minimum(xl, 0.0) - jnp.log1p(jnp.exp(-jnp.abs(xl)))
    logf = jnp.where(valid & (lane >= MISC_GATE) & (lane < MISC_GATE + FOX_HEADS), logf, 0.0)
    cs = _dot(tri_ref[...], _split3(logf))
    cs = (cs + pltpu.roll(cs, LANE - FOX_HEADS, 1)) + pltpu.roll(cs, LANE - 2 * FOX_HEADS, 1)
    cum = jnp.where(lane < FOX_HEADS, cs, 0.0) + carry_ref[0:1, :]
    carry_ref[0:1, :] = cum[tm - 1:tm, :]
    gate_end_ref[0, 0] = jnp.zeros((8, LANE), jnp.float32)
    for c in range(tm // FLASH_TK):
        gate_end_ref[0, 0, c:c + 1, :] = cum[(c + 1) * FLASH_TK - 1:(c + 1) * FLASH_TK, :]
    cum3 = _split3(cum * LOG2E, sel_extra)
    gate_q = _dot(cum3, selq_ref[...])
    gate_k = _dot(cum3, selk_ref[...])

    def inv_norm(sq, n):
        return lax.rsqrt(jnp.sum(sq, axis=-1, keepdims=True) + n * EPS)

    def fox_group(g):
        xq4 = seg(OFF_FQ + g * PAIR, PAIR)
        xk4 = seg(OFF_FK + g * PAIR, PAIR)
        xv4 = seg(OFF_FV + g * PAIR, PAIR)
        for e in range(4):
            hd, par = 4 * g + e, e % 2
            sl = slice((e // 2) * LANE, (e // 2 + 1) * LANE)
            gl = slice(hd * LANE, (hd + 1) * LANE)
            xq, xk = xq4[:, sl], xk4[:, sl]
            rq = inv_norm(jnp.where(halves[par], xq * xq, 0.0), FOX_DIM)
            rk = inv_norm(jnp.where(halves[par], xk * xk, 0.0), FOX_DIM)
            q_ref[0, MLA_HEADS + hd] = (xq * vec(V_GQ_FOX + par) * rq + gate_q[:, gl]
                                        ).astype(jnp.bfloat16)
            k_ref[0, MLA_HEADS + hd] = (xk * vec(V_GK_FOX + par) * rk + gate_k[:, gl]
                                        ).astype(jnp.bfloat16)
            v_ref[0, MLA_HEADS + hd] = jnp.where(halves[par], xv4[:, sl], 1.0
                                                 ).astype(jnp.bfloat16)

    cqn = _rms(cq_misc[:, :Q_LORA], gcq_ref[...], Q_LORA).astype(jnp.bfloat16)
    ckvn = _rms(seg(OFF_CKV, KV_LORA), gckv_ref[...], KV_LORA).astype(jnp.bfloat16)
    def mla_pair(g):
        cols = slice(g * PAIR, (g + 1) * PAIR)
        cols_sw = slice(MLA_HEADS * LANE + g * PAIR, MLA_HEADS * LANE + (g + 1) * PAIR)
        xq2 = _dot(cqn, wuq_ref[:, cols])
        xq2_sw = _dot(cqn, wuq_ref[:, cols_sw])
        xk2 = _dot(ckvn, wkn_ref[:, cols])
        xv2 = _dot(ckvn, wv_ref[:, cols])
        for e in range(2):
            hd, sl = 2 * g + e, slice(e * LANE, (e + 1) * LANE)
            xq = xq2[:, sl]
            q_ref[0, hd] = ((xq * gc_q + xq2_sw[:, sl] * gs_q) * inv_norm(xq * xq, MLA_QK)
                            + add_q_mla).astype(jnp.bfloat16)
            xk = xk2[:, sl] + kpe
            k_ref[0, hd] = ((xk * gc_k + k_rot) * inv_norm(xk * xk, MLA_QK) + add_k_mla
                            ).astype(jnp.bfloat16)
            v_ref[0, hd] = (xv2[:, sl] + vec(V_ONES_V + e)).astype(jnp.bfloat16)

    fox_group(0)
    mla_pair(0)
    mla_pair(1)
    fox_group(1)
    mla_pair(2)
    mla_pair(3)


def _attn_in(stream, lp, gmix, p, rope_tab, tri, selq, selk, tm):
    from_x = len(stream) == 2
    b, d = stream[0].shape[0], D_MODEL
    nt = lp // tm
    kern = functools.partial(_attn_in_kernel, tm=tm, from_x=from_x)
    stream_specs = (_input_specs(tm) if from_x
                    else [pl.BlockSpec((1, tm, d), lambda bi, i: (bi, i, 0))])
    stream_args = [stream[0]] * REAL_PARTS + [stream[1]] if from_x else [stream[0]]
    qk_shape = jax.ShapeDtypeStruct((b, HEADS, lp, LANE), jnp.bfloat16)
    qk_spec = pl.BlockSpec((1, HEADS, tm, LANE), lambda bi, i: (bi, 0, i, 0))
    return pl.pallas_call(
        kern,
        grid=(b, nt),
        in_specs=stream_specs + [
            _const_spec((1, d)),
            _const_spec((d, W_CAT)),
            _const_spec((1, Q_LORA)),
            _const_spec((Q_LORA, 2 * MLA_HEADS * LANE)),
            _const_spec((1, KV_LORA)),
            _const_spec((KV_LORA, MLA_HEADS * LANE)),
            _const_spec((KV_LORA, MLA_HEADS * LANE)),
            _const_spec((VEC_ROWS, LANE)),
            pl.BlockSpec((tm, 2 * LANE), lambda bi, i: (i, 0)),
            _const_spec((tm, tm)),
            _const_spec((LANE, FOX_HEADS * LANE)),
            _const_spec((LANE, FOX_HEADS * LANE)),
        ],
        out_specs=[qk_spec, qk_spec, qk_spec,
                   pl.BlockSpec((1, 1, 8, LANE), lambda bi, i: (bi, i, 0, 0))],
        out_shape=[qk_shape, qk_shape, qk_shape,
                   jax.ShapeDtypeStruct((b, nt, 8, LANE), jnp.float32)],
        scratch_shapes=[pltpu.VMEM((8, LANE), jnp.float32)],
        compiler_params=pltpu.CompilerParams(
            dimension_semantics=("arbitrary", "arbitrary"), vmem_limit_bytes=VMEM_LIMIT),
        name="attn_in",
    )(*stream_args, gmix, p["wcat"], p["gcq"], p["wuq"], p["gckv"], p["wkn"], p["wv"], p["vec"],
      rope_tab, tri, selq, selk)


def _flash_kernel(gate_end_ref, slack_ref, q_ref, k_ref, v_ref, o_ref, m_ref, acc_ref, al_ref,
                  p_ref, *, tq, tk, nq):
    chunks = tq // tk
    assert chunks >= 2
    bi, hp = pl.program_id(0), pl.program_id(1)

    def query_block(qi, carry):
        qbase = qi * tq

        def softmax(j, u, base, diagonal, first=False):
            r0 = u * tk if diagonal else 0
            rows = slice(r0, tq)
            start = pl.multiple_of(base + u * tk, tk)
            q_rows = pl.ds(pl.multiple_of(qbase + r0, tk), tq - r0)
            s = lax.dot_general(q_ref[0, j, q_rows, :], k_ref[0, j, pl.ds(start, tk), :],
                                (((1,), (1,)), ((), ())), preferred_element_type=jnp.float32)
            if diagonal:
                row = lax.broadcasted_iota(jnp.int32, (tk, tk), 0)
                col = lax.broadcasted_iota(jnp.int32, (tk, tk), 1)
                top = jnp.where(col <= row, s[0:tk], NEG)
                s = top if tq - r0 == tk else jnp.concatenate([top, s[tk:]], axis=0)
            if first:
                m_next = jnp.broadcast_to(jnp.max(s, axis=1, keepdims=True), (tq - r0, LANE))
            else:
                m_prev = m_ref[j, rows, :]
                m_next = jnp.maximum(m_prev, jnp.max(s, axis=1, keepdims=True))
                al_ref[j, u, rows, :] = jnp.exp2(m_prev - m_next)
            p = jnp.exp2(s - jnp.concatenate([m_next] * (tk // LANE), axis=1))
            p_ref[j, u, rows, :] = p.astype(jnp.bfloat16)
            m_ref[j, rows, :] = m_next

        def pv(j, u, base, diagonal, first=False):
            r0 = u * tk if diagonal else 0
            rows = slice(r0, tq)
            start = pl.multiple_of(base + u * tk, tk)
            new = _dot(p_ref[j, u, rows, :], v_ref[0, j, pl.ds(start, tk), :])
            if first:
                acc_ref[j, rows, :] = new
            else:
                acc_ref[j, rows, :] = acc_ref[j, rows, :] * al_ref[j, u, rows, :] + new

        def diagonal_block():
            for u in range(chunks):
                softmax(0, u, qbase, True, first=(u == 0))
                if u > 0:
                    pv(1, u - 1, qbase, True, first=(u == 1))
                softmax(1, u, qbase, True, first=(u == 0))
                pv(0, u, qbase, True, first=(u == 0))
            r0 = (chunks - 1) * tk
            al_ref[1, chunks - 1, 0:r0, :] = jnp.ones((r0, LANE), jnp.float32)
            p_ref[1, chunks - 1, 0:r0, :] = jnp.zeros((r0, tk), jnp.bfloat16)

        def block(kb, diagonal, start=0):
            base = kb * tq
            for u in range(start, chunks):
                softmax(0, u, base, diagonal)
                if u == start:
                    pv(1, chunks - 1, jnp.where(kb == first_kb, qbase, (kb - 1) * tq), False)
                else:
                    pv(1, u - 1, base, diagonal)
                softmax(1, u, base, diagonal)
                pv(0, u, base, diagonal)

        def chunks_needed(j):
            hd = 2 * hp + j
            gate_q = gate_end_ref[bi, hd, jnp.maximum(chunks * qi - 1, 0)]
            count = jnp.int32(0)
            for c in range(chunks * (nq - 1)):
                keep = (c < chunks * qi) & (gate_q - gate_end_ref[bi, hd, c] >= slack_ref[hd])
                count = count + keep.astype(jnp.int32)
            return count

        diagonal_block()
        n_chunks = jnp.maximum(chunks_needed(0), chunks_needed(1))
        partial = n_chunks % chunks
        n_full = n_chunks // chunks
        first = qi - n_full
        first_kb = first - (partial != 0).astype(jnp.int32)
        for start in range(1, chunks):

            @pl.when(partial == chunks - start)
            def _():
                block(first - 1, False, start)

        rem = n_full & (BLOCKS_PER_BODY - 1)

        @pl.when((rem & 1) == 1)
        def _():
            block(first, False)

        @pl.when((rem & 2) == 2)
        def _():
            kb = first + (rem & 1)
            block(kb, False)
            block(kb + 1, False)

        def body(group, c):
            kb = first + rem + BLOCKS_PER_BODY * group
            for d in range(BLOCKS_PER_BODY):
                block(kb + d, False)
            return c

        lax.fori_loop(0, n_full // BLOCKS_PER_BODY, body, 0)
        pv(1, chunks - 1, jnp.where(n_chunks == 0, qbase, (qi - 1) * tq), False)

        o0 = acc_ref[0]
        o1 = acc_ref[1]
        o0 = o0 * pl.reciprocal(pltpu.roll(o0, FOX_DIM, 1), approx=True)
        o1 = o1 * pl.reciprocal(pltpu.roll(o1, FOX_DIM, 1), approx=True)
        lane = lax.broadcasted_iota(jnp.int32, (tq, LANE), 1)
        o = jnp.where(lane < MLA_V, o0, o1)
        o_ref[0, pl.ds(pl.multiple_of(qbase, tq), tq), :] = o.astype(jnp.bfloat16)
        return carry

    lax.fori_loop(0, nq, query_block, 0)


def _flash(gate_end, slack, q, k, v):
    b, _, lp, _ = q.shape
    tq, tk = FLASH_TQ, FLASH_TK
    kern = functools.partial(_flash_kernel, tq=tq, tk=tk, nq=lp // tq)
    qkv_spec = pl.BlockSpec((1, 2, lp, LANE), lambda bi, hp: (bi, hp, 0, 0))
    return pl.pallas_call(
        kern,
        grid=(b, HEADS // 2),
        in_specs=[pl.BlockSpec(memory_space=pltpu.SMEM), pl.BlockSpec(memory_space=pltpu.SMEM),
                  qkv_spec, qkv_spec, qkv_spec],
        out_specs=pl.BlockSpec((1, lp, LANE), lambda bi, hp: (bi, 0, hp)),
        out_shape=jax.ShapeDtypeStruct((b, lp, HEADS * MLA_V), jnp.bfloat16),
        scratch_shapes=[pltpu.VMEM((2, tq, LANE), jnp.float32)] * 2
        + [pltpu.VMEM((2, tq // tk, tq, LANE), jnp.float32),
           pltpu.VMEM((2, tq // tk, tq, tk), jnp.bfloat16)],
        compiler_params=pltpu.CompilerParams(
            dimension_semantics=("arbitrary", "arbitrary"), vmem_limit_bytes=VMEM_LIMIT),
        name="flash",
    )(gate_end, slack, q, k, v)


def _conv_in_kernel(hn_ref, win_ref, cw_ref, y_ref, gs_ref, *, tm):
    i = pl.program_id(1)

    @pl.when(i == 0)
    def _():
        gs_ref[0:8, :] = jnp.zeros((8, D_MODEL), jnp.float32)

    hn = hn_ref[0]
    gate_c = _dot(hn, win_ref[0, :, D_MODEL:2 * D_MODEL])
    u = _dot(hn, win_ref[0, :, 2 * D_MODEL:3 * D_MODEL])
    row = lax.broadcasted_iota(jnp.int32, (tm, D_MODEL), 0)
    g = jnp.where((i * tm + row) >= PAD, gate_c * u, 0.0)
    gs_ref[8:tm + 8, :] = g
    y = (cw_ref[0:1, :] * gs_ref[6:tm + 6, :] + cw_ref[1:2, :] * gs_ref[7:tm + 7, :]
         + cw_ref[2:3, :] * g)
    gs_ref[0:8, :] = gs_ref[tm:tm + 8, :]
    gate_b = _dot(hn, win_ref[0, :, 0:D_MODEL])
    y_ref[0] = (gate_b * y).astype(jnp.bfloat16)


def _conv_in(hn, win, layer, cw, tm):
    b, lp, d = hn.shape
    kern = functools.partial(_conv_in_kernel, tm=tm)
    return pl.pallas_call(
        kern,
        grid=(b, lp // tm),
        in_specs=[
            pl.BlockSpec((1, tm, d), lambda bi, i: (bi, i, 0)),
            _layer_spec((d, 3 * d), layer),
            _const_spec((8, d)),
        ],
        out_specs=pl.BlockSpec((1, tm, d), lambda bi, i: (bi, i, 0)),
        out_shape=jax.ShapeDtypeStruct((b, lp, d), jnp.bfloat16),
        scratch_shapes=[pltpu.VMEM((tm + 8, d), jnp.float32)],
        compiler_params=pltpu.CompilerParams(
            dimension_semantics=("arbitrary", "arbitrary"), vmem_limit_bytes=VMEM_LIMIT),
        name="conv_in",
    )(hn, win, cw)


def _mlp_tile(h, y, wo_ref, gmlp_ref, wup_ref, wdn_ref):
    h1 = h + _dot(y, wo_ref[0])
    n = _rms(h1, gmlp_ref[...], D_MODEL).astype(jnp.bfloat16)
    acc = h1
    for c in range(D_FF // FF_CHUNK):
        sl = slice(c * FF_CHUNK, (c + 1) * FF_CHUNK)
        a = jnp.maximum(_dot(n, wup_ref[0, :, sl]), 0.0)
        acc = acc + _dot((a * a).astype(jnp.bfloat16), wdn_ref[0, sl, :])
    return acc


def _mix_out_mlp_kernel(h_ref, y_ref, wo_ref, gmlp_ref, gnext_ref, wup_ref, wdn_ref,
                        out_ref, hn_ref):
    out = _mlp_tile(h_ref[...], y_ref[...], wo_ref, gmlp_ref, wup_ref, wdn_ref)
    out_ref[...] = out
    hn_ref[...] = _rms(out, gnext_ref[...], D_MODEL).astype(jnp.bfloat16)


def _mix_out_mlp(h, y, wo, wo_layer, gmlp, gnext, wup, wdn, layer, tm):
    r, d = h.shape
    return pl.pallas_call(
        _mix_out_mlp_kernel,
        grid=(r // tm,),
        in_specs=[
            pl.BlockSpec((tm, d), lambda i: (i, 0)),
            pl.BlockSpec((tm, d), lambda i: (i, 0)),
            _layer_spec((d, d), wo_layer),
            _const_spec((1, d)),
            _const_spec((1, d)),
            _layer_spec((d, D_FF), layer),
            _layer_spec((D_FF, d), layer),
        ],
        out_specs=[pl.BlockSpec((tm, d), lambda i: (i, 0))] * 2,
        out_shape=[jax.ShapeDtypeStruct((r, d), jnp.float32),
                   jax.ShapeDtypeStruct((r, d), jnp.bfloat16)],
        compiler_params=pltpu.CompilerParams(
            dimension_semantics=("arbitrary",), vmem_limit_bytes=VMEM_LIMIT),
        name="mix_out_mlp",
    )(h, y, wo, gmlp, gnext, wup, wdn)


def _mix_out_mlp_first_kernel(*refs):
    x_parts, meta_ref = refs[0:REAL_PARTS], refs[REAL_PARTS]
    y_ref, wo_ref, gmlp_ref, gnext_ref, wup_ref, wdn_ref, out_ref, hn_ref = refs[REAL_PARTS + 1:]
    h = _input_tile(pl.program_id(1), x_parts, meta_ref)
    out = _mlp_tile(h, y_ref[0], wo_ref, gmlp_ref, wup_ref, wdn_ref)
    out_ref[0] = out
    hn_ref[0] = _rms(out, gnext_ref[...], D_MODEL).astype(jnp.bfloat16)


def _mix_out_mlp_first(x, meta, y, wo, wo_layer, gmlp, gnext, wup, wdn, layer, tm):
    b, lp, d = y.shape
    tile = pl.BlockSpec((1, tm, d), lambda bi, i: (bi, i, 0))
    return pl.pallas_call(
        _mix_out_mlp_first_kernel,
        grid=(b, lp // tm),
        in_specs=_input_specs(tm) + [
            tile,
            _layer_spec((d, d), wo_layer),
            _const_spec((1, d)),
            _const_spec((1, d)),
            _layer_spec((d, D_FF), layer),
            _layer_spec((D_FF, d), layer),
        ],
        out_specs=[tile, tile],
        out_shape=[jax.ShapeDtypeStruct((b, lp, d), jnp.float32),
                   jax.ShapeDtypeStruct((b, lp, d), jnp.bfloat16)],
        compiler_params=pltpu.CompilerParams(
            dimension_semantics=("arbitrary", "arbitrary"), vmem_limit_bytes=VMEM_LIMIT),
        name="mix_out_mlp_first",
    )(*([x] * REAL_PARTS), meta, y, wo, gmlp, gnext, wup, wdn)


def _mix_out_mlp_last_kernel(*refs):
    h_parts, y_parts = refs[0:REAL_PARTS], refs[REAL_PARTS:2 * REAL_PARTS]
    wo_ref, gmlp_ref, wup_ref, wdn_ref, out_ref = refs[2 * REAL_PARTS:]
    h = jnp.concatenate([r[0] for r in h_parts], axis=0)
    y = jnp.concatenate([r[0] for r in y_parts], axis=0)
    out_ref[0] = _mlp_tile(h, y, wo_ref, gmlp_ref, wup_ref, wdn_ref)


def _mix_out_mlp_last(h, y, wo, wo_layer, gmlp, wup, wdn, layer, tm, seq):
    b, lp, d = h.shape
    part = tm // REAL_PARTS
    last_part = lp // part - 1

    def part_spec(k):
        return pl.BlockSpec(
            (1, part, d),
            lambda bi, i: (bi, jnp.minimum(REAL_START // part + REAL_PARTS * i + k, last_part), 0))

    parts = [part_spec(k) for k in range(REAL_PARTS)]
    return pl.pallas_call(
        _mix_out_mlp_last_kernel,
        grid=(b, pl.cdiv(seq, tm)),
        in_specs=parts + parts + [
            _layer_spec((d, d), wo_layer),
            _const_spec((1, d)),
            _layer_spec((d, D_FF), layer),
            _layer_spec((D_FF, d), layer),
        ],
        out_specs=pl.BlockSpec((1, tm, d), lambda bi, i: (bi, i, 0)),
        out_shape=jax.ShapeDtypeStruct((b, seq, d), jnp.float32),
        compiler_params=pltpu.CompilerParams(
            dimension_semantics=("arbitrary", "arbitrary"), vmem_limit_bytes=VMEM_LIMIT),
        name="mix_out_mlp_last",
    )(*([h] * REAL_PARTS + [y] * REAL_PARTS), wo, gmlp, wup, wdn)


def _pad_heads(w, heads, dim):
    k = w.shape[0]
    w = w.reshape(k, heads, dim)
    w = jnp.pad(w, ((0, 0), (0, 0), (0, LANE - dim)))
    return w.reshape(k, heads * LANE)


def _lane_vec(v, offset=0):
    return jnp.zeros((LANE,), jnp.float32).at[offset:offset + v.shape[0]].set(v)


def _attn_params(w_in, g_cq, w_uq, g_ckv, w_ukv, g_q_mla, g_k_mla, g_q_fox, g_k_fox, b_forget):
    bf = jnp.bfloat16
    o1 = Q_LORA
    o2 = o1 + KV_LORA
    o3 = o2 + MLA_ROPE
    o4 = o3 + FOX_HEADS * FOX_DIM
    o5 = o4 + FOX_HEADS * FOX_DIM
    o6 = o5 + FOX_HEADS * FOX_DIM
    misc = jnp.zeros((D_MODEL, LANE), jnp.float32)
    misc = misc.at[:, MISC_GATE:MISC_GATE + FOX_HEADS].set(w_in[:, o6:])
    misc = misc.at[:, MISC_ROPE:MISC_ROPE + MLA_ROPE].set(w_in[:, o2:o3])
    wcat = jnp.concatenate([w_in[:, :o1], misc, w_in[:, o1:o2], w_in[:, o3:o6]], axis=1).astype(bf)
    kv = w_ukv.reshape(KV_LORA, MLA_HEADS, MLA_NOPE + MLA_V)
    wkn = _pad_heads(kv[:, :, :MLA_NOPE].reshape(KV_LORA, -1), MLA_HEADS, MLA_NOPE).astype(bf)
    wv = jnp.pad(kv[:, :, MLA_NOPE:].reshape(KV_LORA, MLA_HEADS // 2, 2, MLA_V),
                 ((0, 0), (0, 0), (0, 0), (0, LANE - MLA_V)))
    wv = jnp.concatenate([wv[:, :, 0], jnp.roll(wv[:, :, 1], MLA_V, axis=-1)], axis=-1)
    wv = wv.reshape(KV_LORA, MLA_HEADS * LANE).astype(bf)
    lo, mid, hi = MLA_NOPE, MLA_NOPE + HALF_ROPE, MLA_NOPE + MLA_ROPE
    uq = w_uq.reshape(Q_LORA, MLA_HEADS, MLA_QK)
    uq_sw = jnp.zeros((Q_LORA, MLA_HEADS, LANE), jnp.float32)
    uq_sw = uq_sw.at[:, :, lo:mid].set(uq[:, :, mid:hi]).at[:, :, mid:hi].set(uq[:, :, lo:mid])
    wuq = jnp.concatenate([_pad_heads(w_uq, MLA_HEADS, MLA_QK),
                           uq_sw.reshape(Q_LORA, MLA_HEADS * LANE)], axis=1).astype(bf)

    def swapped(g):
        return jnp.zeros((LANE,), jnp.float32).at[lo:mid].set(g[mid:hi]).at[mid:hi].set(g[lo:mid])

    zero = jnp.zeros((LANE,), jnp.float32)
    rows = [zero] * VEC_ROWS
    rows[V_GQ_MLA] = _lane_vec(g_q_mla) * LOG2E
    rows[V_GQ_MLA_SW] = swapped(g_q_mla) * LOG2E
    rows[V_GK_MLA] = _lane_vec(g_k_mla) * MLA_QK ** 0.5
    rows[V_GK_MLA_SW] = swapped(g_k_mla) * MLA_QK ** 0.5
    for par in range(2):
        feat, extra = FEATURE_BASE[par], EXTRA_BASE[par]
        rows[V_GQ_FOX + par] = _lane_vec(g_q_fox, feat) * LOG2E
        rows[V_GK_FOX + par] = _lane_vec(g_k_fox, feat) * FOX_DIM ** 0.5
        rows[V_ONES_V + par] = zero.at[extra:extra + FOX_DIM].set(1.0)
    rows[V_B_FORGET] = _lane_vec(b_forget, MISC_GATE)
    rows[V_ADD_Q_MLA] = zero.at[FLAG_MLA].set(1.0)
    vec = jnp.stack(rows)
    return dict(wcat=wcat, gcq=g_cq[None], wuq=wuq, gckv=g_ckv[None], wkn=wkn, wv=wv, vec=vec)


def _gate_selectors():
    selq = np.zeros((LANE, FOX_HEADS * LANE), np.float32)
    selk = np.zeros((LANE, FOX_HEADS * LANE), np.float32)
    for hd in range(FOX_HEADS):
        extra = hd * LANE + EXTRA_BASE[hd % 2]
        for part in range(N_SPLIT):
            selq[part * FOX_HEADS + hd, extra + part] = 1.0
            selk[part * FOX_HEADS + hd, extra + N_SPLIT + part] = -1.0
            selq[ONE_LANE, extra + N_SPLIT + part] = 1.0
            selk[ONE_LANE, extra + part] = 1.0
        selq[ONE_LANE, extra + FLAG_FOX_OFF] = 1.0
        selk[PADROW_LANE, extra + FLAG_FOX_OFF] = PAD_KEY
    return jnp.asarray(selq, jnp.bfloat16), jnp.asarray(selk, jnp.bfloat16)


def _rope_table(lp):
    lane = jnp.arange(LANE, dtype=jnp.int32)
    rotary = (lane >= MLA_NOPE) & (lane < MLA_NOPE + MLA_ROPE)
    first_half = rotary & (lane < MLA_NOPE + HALF_ROPE)
    pair = ((lane - MLA_NOPE) % HALF_ROPE).astype(jnp.float32)
    inv_freq = ROPE_BASE ** (-(2.0 * pair) / MLA_ROPE)
    pos = (jnp.arange(lp, dtype=jnp.int32) - PAD).astype(jnp.float32)
    ang = pos[:, None] * inv_freq[None, :]
    cos_t = jnp.where(lane < MLA_NOPE, 1.0, jnp.where(rotary, jnp.cos(ang), 0.0))
    sin_sw = jnp.where(rotary, jnp.where(first_half, -jnp.sin(ang), jnp.sin(ang)), 0.0)
    return jnp.concatenate([cos_t, sin_sw], axis=1)


def _pruning_tables(gate_end, g_q, g_k):
    b, nt = gate_end.shape[:2]
    per_tile = FLASH_TQ // FLASH_TK
    fox = gate_end[:, :, 0:per_tile, MISC_GATE:MISC_GATE + FOX_HEADS] * LOG2E
    fox = jnp.transpose(fox.reshape(b, nt * per_tile, FOX_HEADS), (0, 2, 1))
    table = jnp.concatenate([jnp.zeros((b, MLA_HEADS, nt * per_tile), jnp.float32), fox], axis=1)
    bound = 1.02 * FOX_DIM * (FOX_DIM ** -0.5 * LOG2E) * jnp.max(jnp.abs(g_q)) * jnp.max(jnp.abs(g_k))
    slack_fox = -(2.0 * bound + UNDERFLOW_LOG2 + 4.0)
    slack = jnp.concatenate([jnp.full((MLA_HEADS,), NEG, jnp.float32),
                             jnp.full((FOX_HEADS,), slack_fox, jnp.float32)])
    return table, slack


def _token_tile(lp):
    if lp % FLASH_TQ:
        raise ValueError(f"padded length {lp} is not a multiple of {FLASH_TQ}")
    return FLASH_TQ


def kernel(x, meta_tokens, g_mix, g_mlp, w_in_attn, g_cq, w_uq, g_ckv, w_ukv, g_q_mla, g_k_mla,
           g_q_fox, g_k_fox, b_forget, w_out_attn, w_in_conv, conv_w, w_out_conv, w_mlp_up,
           w_mlp_down):
    b, seq, d = x.shape
    assert d == D_MODEL and (PAD + N_META + seq) % BLOCK == 0
    lp = PAD + N_META + seq
    tm = _token_tile(lp)
    bf = jnp.bfloat16

    meta = meta_tokens.astype(x.dtype)
    h = hn = None

    rope_tab = _rope_table(lp)
    tri = (jnp.arange(tm)[:, None] >= jnp.arange(tm)[None, :]).astype(bf)
    selq, selk = _gate_selectors()

    wo_attn, wo_conv, w_conv = w_out_attn.astype(bf), w_out_conv.astype(bf), w_in_conv.astype(bf)
    w_up, w_down = w_mlp_up.astype(bf), w_mlp_down.astype(bf)
    for layer in range(DEPTH):
        j = layer // 2
        gmix = g_mix[layer][None]
        if layer % 2 == 0:
            p = _attn_params(w_in_attn[j], g_cq[j], w_uq[j], g_ckv[j], w_ukv[j], g_q_mla[j],
                             g_k_mla[j], g_q_fox[j], g_k_fox[j], b_forget[j])
            stream = (x, meta) if layer == 0 else (hn,)
            q, k, v, gate_end = _attn_in(stream, lp, gmix, p, rope_tab, tri, selq, selk, tm)
            y = _flash(*_pruning_tables(gate_end, g_q_fox[j], g_k_fox[j]), q, k, v)
            wo = wo_attn
        else:
            cw = jnp.zeros((8, d), jnp.float32).at[0:3].set(conv_w[j])
            y = _conv_in(hn, w_conv, j, cw, tm)
            wo = wo_conv
        gmlp = g_mlp[layer][None]
        gnext = g_mix[min(layer + 1, DEPTH - 1)][None]
        if layer == 0:
            h, hn = _mix_out_mlp_first(x, meta, y, wo, j, gmlp, gnext, w_up, w_down, layer, tm)
        elif layer < DEPTH - 1:
            h, hn = _mix_out_mlp(h.reshape(b * lp, d), y.reshape(b * lp, d), wo, j, gmlp, gnext,
                                 w_up, w_down, layer, tm)
            h, hn = h.reshape(b, lp, d), hn.reshape(b, lp, d)
        else:
            return _mix_out_mlp_last(h, y, wo, j, gmlp, w_up, w_down, layer, tm, seq)
```
